```python
import math
import jax, jax.numpy as jnp
from jax import lax
import numpy as np

D_MODEL = 1024
BATCH = 4
SEQ = 4096
DEPTH = 2

CHUNK = 64
D_MIX = D_MODEL
D_ATTN = D_MIX // 2
D_RG = D_MIX // 4
D_S5 = D_MIX - D_ATTN - D_RG
ATTN_HEAD_DIM = 64
ATTN_HEADS = D_ATTN // ATTN_HEAD_DIM
N_PREV_CHUNKS = 8
BAND = (N_PREV_CHUNKS + 1) * CHUNK
REL_CLIP = 128
RG_CONV_WIDTH = 4
RG_BLOCKS = 4
RG_BLOCK_DIM = D_RG // RG_BLOCKS
RG_C = 8.0
S5_GROUP_DIM = 16
S5_GROUPS = D_S5 // S5_GROUP_DIM
S5_STATE = 64
D_IN = 3 * D_ATTN + 2 * D_RG + D_S5
FFN_DIM = 2816
N_EXPERTS = 8
TOP_K = 2
EXPERT_DIM = 2816
EXPERT_BLOCK = 128
N_DENSE = (DEPTH + 1) // 2
N_MOE = DEPTH // 2
EPS = 1e-6

kernel_name = "hybrid_chunk_attn_rglru_s5_moe"


def _rmsnorm(x, g):
    xf = x.astype(jnp.float32)
    var = jnp.mean(xf * xf, axis=-1, keepdims=True)
    return (xf * lax.rsqrt(var + EPS) * g.astype(jnp.float32)).astype(x.dtype)


def _linear_scan(a, b):
    def op(c1, c2):
        a1, b1 = c1
        a2, b2 = c2
        return a1 * a2, a2 * b1 + b2
    return lax.associative_scan(op, (a, b), axis=1)[1]


def _chunked_attention(q, k, v, rel_bias):
    b, l, h, dh = q.shape
    nc = l // CHUNK
    qc = q.reshape(b, nc, CHUNK, h, dh)
    pad = ((0, 0), (N_PREV_CHUNKS, 0), (0, 0), (0, 0), (0, 0))
    kp = jnp.pad(k.reshape(b, nc, CHUNK, h, dh), pad)
    vp = jnp.pad(v.reshape(b, nc, CHUNK, h, dh), pad)
    band_idx = jnp.arange(nc)[:, None] + jnp.arange(N_PREV_CHUNKS + 1)[None, :]
    kb = kp[:, band_idx].reshape(b, nc, BAND, h, dh)
    vb = vp[:, band_idx].reshape(b, nc, BAND, h, dh)
    s = jnp.einsum('bcqhd,bckhd->bhcqk', qc, kb).astype(jnp.float32) * (dh ** -0.5)
    qpos = jnp.arange(CHUNK)[:, None] + N_PREV_CHUNKS * CHUNK
    kpos = jnp.arange(BAND)[None, :]
    rel = jnp.clip(qpos - kpos, -REL_CLIP, REL_CLIP) + REL_CLIP
    bias = rel_bias.astype(jnp.float32)[:, rel]
    key_pos = jnp.arange(nc)[:, None] * CHUNK + jnp.arange(BAND)[None, :] - N_PREV_CHUNKS * CHUNK
    valid = key_pos >= 0
    s = s + bias[:, None]
    s = jnp.where(valid[None, None, :, None, :], s, -1e30)
    p = jax.nn.softmax(s, axis=-1).astype(v.dtype)
    o = jnp.einsum('bhcqk,bckhd->bcqhd', p, vb)
    return o.reshape(b, l, h * dh)


def _rglru_branch(xr, gate, conv_w, conv_b, wx, bx, wa, ba, lam):
    b, l, c = xr.shape
    xc = lax.conv_general_dilated(
        xr, conv_w[:, None, :].astype(xr.dtype), window_strides=(1,),
        padding=[(RG_CONV_WIDTH - 1, 0)], dimension_numbers=('NWC', 'WIO', 'NWC'),
        feature_group_count=c) + conv_b.astype(xr.dtype)
    xf = xc.astype(jnp.float32)
    xblk = xf.reshape(b, l, RG_BLOCKS, RG_BLOCK_DIM)
    gx = jax.nn.sigmoid(jnp.einsum('blhi,hij->blhj', xblk, wx.astype(jnp.float32)).reshape(b, l, c)
                        + bx.astype(jnp.float32))
    ga = jax.nn.sigmoid(jnp.einsum('blhi,hij->blhj', xblk, wa.astype(jnp.float32)).reshape(b, l, c)
                        + ba.astype(jnp.float32))
    log_a = -RG_C * ga * jax.nn.softplus(-lam.astype(jnp.float32))
    a = jnp.exp(log_a)
    mult = jnp.sqrt(-jnp.expm1(2.0 * log_a))
    hseq = _linear_scan(a, mult * gx * xf)
    y = hseq * jax.nn.gelu(gate.astype(jnp.float32))
    return y.astype(xr.dtype)


def _s5_branch(u, a_re, a_im, log_dt, b_re, b_im, c_re, c_im, d, w_glu):
    bsz, l, _ = u.shape
    uf = u.astype(jnp.float32).reshape(bsz, l, S5_GROUPS, S5_GROUP_DIM)
    A = lax.complex(a_re.astype(jnp.float32), a_im.astype(jnp.float32))
    dt = jnp.exp(log_dt.astype(jnp.float32))[:, None]
    A_bar = jnp.exp(A * dt)
    Bm = lax.complex(b_re.astype(jnp.float32), b_im.astype(jnp.float32))
    Cm = lax.complex(c_re.astype(jnp.float32), c_im.astype(jnp.float32))
    B_bar = ((A_bar - 1.0) / A)[..., None] * Bm
    Bu = jnp.einsum('blgc,gpc->blgp', uf.astype(jnp.complex64), B_bar)
    A_seq = jnp.broadcast_to(A_bar, Bu.shape)
    states = _linear_scan(A_seq, Bu)
    y = jnp.real(jnp.einsum('blgp,gcp->blgc', states, Cm)) \
        + d.astype(jnp.float32).reshape(S5_GROUPS, S5_GROUP_DIM) * uf
    y = jax.nn.gelu(y.reshape(bsz, l, D_S5))
    y = y * jax.nn.sigmoid(y @ w_glu.astype(jnp.float32))
    return y.astype(u.dtype)


def _swiglu(h, w1, w3, w2):
    return (jax.nn.silu(h @ w1) * (h @ w3)) @ w2


def _moe(x2, router, w1, w3, w2):
    n, dm = x2.shape
    logits = x2.astype(jnp.float32) @ router.astype(jnp.float32)
    top_logit, top_e = lax.top_k(logits, TOP_K)
    gates = jax.nn.softmax(top_logit, axis=-1)
    flat_e = top_e.reshape(-1)
    flat_tok = jnp.repeat(jnp.arange(n, dtype=jnp.int32), TOP_K)
    flat_g = gates.reshape(-1)
    order = jnp.argsort(flat_e)
    se, st, sg = flat_e[order], flat_tok[order], flat_g[order]
    counts = jnp.zeros((N_EXPERTS,), jnp.int32).at[flat_e].add(1)
    starts = jnp.cumsum(counts) - counts
    pcounts = (counts + EXPERT_BLOCK - 1) // EXPERT_BLOCK * EXPERT_BLOCK
    pends = jnp.cumsum(pcounts)
    pstarts = pends - pcounts
    n_assign = n * TOP_K
    rank = jnp.arange(n_assign, dtype=jnp.int32) - starts[se]
    dest = pstarts[se] + rank
    n_pad = (n_assign + EXPERT_BLOCK - 1) // EXPERT_BLOCK * EXPERT_BLOCK + N_EXPERTS * EXPERT_BLOCK
    n_blocks = n_pad // EXPERT_BLOCK
    buf_tok = jnp.full((n_pad,), n, jnp.int32).at[dest].set(st)
    buf_g = jnp.zeros((n_pad,), jnp.float32).at[dest].set(sg)
    blk_start = jnp.arange(n_blocks, dtype=jnp.int32) * EXPERT_BLOCK
    blk_e = jnp.clip(jnp.searchsorted(pends, blk_start, side='right'), 0, N_EXPERTS - 1)
    x_pad = jnp.concatenate([x2, jnp.zeros((1, dm), x2.dtype)], axis=0)
    xb = x_pad[buf_tok].reshape(n_blocks, EXPERT_BLOCK, dm)

    def expert_block(args):
        xblk, e = args
        return _swiglu(xblk, w1[e], w3[e], w2[e])

    yb = lax.map(expert_block, (xb, blk_e)).reshape(n_pad, dm)
    yb = yb * buf_g[:, None].astype(yb.dtype)
    out = jnp.zeros((n + 1, dm), x2.dtype).at[buf_tok].add(yb)
    return out[:n]


def setup_inputs(seed: int = 0) -> dict:
    key = jax.random.key(seed)
    ks = iter(jax.random.split(key, 40))

    def nrm(shape, scale):
        return jax.random.normal(next(ks), shape, jnp.float32) * scale

    def gain(shape):
        return 1.0 + nrm(shape, 0.02)

    x = nrm((BATCH, SEQ, D_MODEL), 1.0)
    norm_mix_g = gain((DEPTH, D_MODEL))
    w_in = nrm((DEPTH, D_MODEL, D_IN), D_MODEL ** -0.5)
    attn_rel_bias = nrm((DEPTH, ATTN_HEADS, 2 * REL_CLIP + 1), 0.1)
    rg_conv_w = nrm((DEPTH, RG_CONV_WIDTH, D_RG), RG_CONV_WIDTH ** -0.5)
    rg_conv_b = nrm((DEPTH, D_RG), 0.01)
    rg_wx = nrm((DEPTH, RG_BLOCKS, RG_BLOCK_DIM, RG_BLOCK_DIM), RG_BLOCK_DIM ** -0.5)
    rg_bx = nrm((DEPTH, D_RG), 0.01)
    rg_wa = nrm((DEPTH, RG_BLOCKS, RG_BLOCK_DIM, RG_BLOCK_DIM), RG_BLOCK_DIM ** -0.5)
    rg_ba = nrm((DEPTH, D_RG), 0.01)
    a_pow = jax.random.uniform(next(ks), (DEPTH, D_RG), jnp.float32, 0.9, 0.999)
    a0 = a_pow ** (1.0 / RG_C)
    rg_lambda = jnp.log(a0) - jnp.log1p(-a0)
    s5_a_re = -0.5 + nrm((DEPTH, S5_GROUPS, S5_STATE), 0.01)
    s5_a_im = math.pi * jnp.arange(S5_STATE, dtype=jnp.float32) + nrm((DEPTH, S5_GROUPS, S5_STATE), 0.01)
    s5_log_dt = jax.random.uniform(next(ks), (DEPTH, S5_GROUPS), jnp.float32,
                                   math.log(1e-3), math.log(1e-1))
    s5_b_re = nrm((DEPTH, S5_GROUPS, S5_STATE, S5_GROUP_DIM), (2 * S5_GROUP_DIM) ** -0.5)
    s5_b_im = nrm((DEPTH, S5_GROUPS, S5_STATE, S5_GROUP_DIM), (2 * S5_GROUP_DIM) ** -0.5)
    s5_c_re = nrm((DEPTH, S5_GROUPS, S5_GROUP_DIM, S5_STATE), S5_STATE ** -0.5)
    s5_c_im = nrm((DEPTH, S5_GROUPS, S5_GROUP_DIM, S5_STATE), S5_STATE ** -0.5)
    s5_d = nrm((DEPTH, D_S5), 1.0)
    s5_w_glu = nrm((DEPTH, D_S5, D_S5), D_S5 ** -0.5)
    g_group = gain((DEPTH, D_MIX))
    w_out = nrm((DEPTH, D_MIX, D_MODEL), D_MIX ** -0.5)
    norm_ffn_g = gain((DEPTH, D_MODEL))
    ffn_w1 = nrm((N_DENSE, D_MODEL, FFN_DIM), D_MODEL ** -0.5)
    ffn_w3 = nrm((N_DENSE, D_MODEL, FFN_DIM), D_MODEL ** -0.5)
    ffn_w2 = nrm((N_DENSE, FFN_DIM, D_MODEL), FFN_DIM ** -0.5)
    moe_router = nrm((N_MOE, D_MODEL, N_EXPERTS), D_MODEL ** -0.5)
    moe_w1 = nrm((N_MOE, N_EXPERTS, D_MODEL, EXPERT_DIM), D_MODEL ** -0.5)
    moe_w3 = nrm((N_MOE, N_EXPERTS, D_MODEL, EXPERT_DIM), D_MODEL ** -0.5)
    moe_w2 = nrm((N_MOE, N_EXPERTS, EXPERT_DIM, D_MODEL), EXPERT_DIM ** -0.5)
    final_norm_g = gain((D_MODEL,))
    return {
        "x": x, "norm_mix_g": norm_mix_g, "w_in": w_in, "attn_rel_bias": attn_rel_bias,
        "rg_conv_w": rg_conv_w, "rg_conv_b": rg_conv_b, "rg_wx": rg_wx, "rg_bx": rg_bx,
        "rg_wa": rg_wa, "rg_ba": rg_ba, "rg_lambda": rg_lambda,
        "s5_a_re": s5_a_re, "s5_a_im": s5_a_im, "s5_log_dt": s5_log_dt,
        "s5_b_re": s5_b_re, "s5_b_im": s5_b_im, "s5_c_re": s5_c_re, "s5_c_im": s5_c_im,
        "s5_d": s5_d, "s5_w_glu": s5_w_glu, "g_group": g_group, "w_out": w_out,
        "norm_ffn_g": norm_ffn_g, "ffn_w1": ffn_w1, "ffn_w3": ffn_w3, "ffn_w2": ffn_w2,
        "moe_router": moe_router, "moe_w1": moe_w1, "moe_w3": moe_w3, "moe_w2": moe_w2,
        "final_norm_g": final_norm_g,
    }


def reference(x, norm_mix_g, w_in, attn_rel_bias, rg_conv_w, rg_conv_b, rg_wx, rg_bx, rg_wa, rg_ba,
              rg_lambda, s5_a_re, s5_a_im, s5_log_dt, s5_b_re, s5_b_im, s5_c_re, s5_c_im, s5_d,
              s5_w_glu, g_group, w_out, norm_ffn_g, ffn_w1, ffn_w3, ffn_w2, moe_router, moe_w1,
              moe_w3, moe_w2, final_norm_g):
    bsz, l, dm = x.shape
    split_at = [D_ATTN, 2 * D_ATTN, 3 * D_ATTN, 3 * D_ATTN + D_RG, 3 * D_ATTN + 2 * D_RG]
    for layer in range(DEPTH):
        u = _rmsnorm(x, norm_mix_g[layer])
        proj = u @ w_in[layer]
        q, k, v, xr, gate, us = jnp.split(proj, split_at, axis=-1)
        hs = (bsz, l, ATTN_HEADS, ATTN_HEAD_DIM)
        y_attn = _chunked_attention(q.reshape(hs), k.reshape(hs), v.reshape(hs), attn_rel_bias[layer])
        y_rg = _rglru_branch(xr, gate, rg_conv_w[layer], rg_conv_b[layer], rg_wx[layer], rg_bx[layer],
                             rg_wa[layer], rg_ba[layer], rg_lambda[layer])
        y_s5 = _s5_branch(us, s5_a_re[layer], s5_a_im[layer], s5_log_dt[layer], s5_b_re[layer],
                          s5_b_im[layer], s5_c_re[layer], s5_c_im[layer], s5_d[layer], s5_w_glu[layer])
        gg = g_group[layer]
        mixed = jnp.concatenate([
            _rmsnorm(y_attn, gg[:D_ATTN]),
            _rmsnorm(y_rg, gg[D_ATTN:D_ATTN + D_RG]),
            _rmsnorm(y_s5, gg[D_ATTN + D_RG:]),
        ], axis=-1)
        x = x + mixed @ w_out[layer]
        h = _rmsnorm(x, norm_ffn_g[layer])
        idx = layer // 2
        if layer % 2 == 0:
            y = _swiglu(h, ffn_w1[idx], ffn_w3[idx], ffn_w2[idx])
        else:
            y = _moe(h.reshape(-1, dm), moe_router[idx], moe_w1[idx], moe_w3[idx],
                     moe_w2[idx]).reshape(bsz, l, dm)
        x = x + y
    return _rmsnorm(x, final_norm_g)
```

```python
import functools
import math

import jax
import jax.numpy as jnp
from jax import lax
from jax.experimental import pallas as pl
from jax.experimental.pallas import tpu as pltpu

F32 = jnp.float32
BF16 = jnp.bfloat16
I32 = jnp.int32

EPS = 1e-6
CHUNK = 64
N_PREV = 8
BAND = (N_PREV + 1) * CHUNK
REL_CLIP = 128
HEAD_DIM = 64
RG_C = 8.0
RG_CONV_WIDTH = 4
S5_GROUP_DIM = 16
S5_T = 16
TOP_K = 2
NEG_BIG = -1e30

LANES = 128
BF16_ROWS = 16
VMEM_LIMIT = 52 * 1024 * 1024

ROW_TILE = 512
ATTN_TQ = 512
RG_TL = 1024
RG_SEG = 8
MOE_TILE = 256
MOE_BLK = 256
FFN_CHUNKS = ((0, 512), (512, 512), (1024, 512), (1536, 512), (2048, 512), (2560, 256))


def _cparams(*sem):
    return pltpu.CompilerParams(dimension_semantics=sem, vmem_limit_bytes=VMEM_LIMIT)


def _const_spec(shape):
    nd = len(shape)
    return pl.BlockSpec(shape, lambda *_: (0,) * nd, pipeline_mode=pl.Buffered(1))


def _rms(xf, g):
    var = jnp.mean(xf * xf, axis=-1, keepdims=True)
    return xf * lax.rsqrt(var + EPS) * g


def _sigmoid(x):
    return 1.0 / (1.0 + jnp.exp(-x))


def _gelu(x):
    c = math.sqrt(2.0 / math.pi)
    return 0.5 * x * (1.0 + jnp.tanh(c * (x + 0.044715 * (x * x * x))))


def _dot(a, b):
    return jnp.dot(a, b, preferred_element_type=F32)


def _in_proj_kernel(x_ref, g_ref, w_ref, q_ref, k_ref, v_ref, xr_ref, gt_ref, us_ref, *, d_attn, d_rg):
    u = _rms(x_ref[...], g_ref[...]).astype(BF16)
    lo = 0
    for ref, width in ((q_ref, d_attn), (k_ref, d_attn), (v_ref, d_attn),
                       (xr_ref, d_rg), (gt_ref, d_rg), (us_ref, w_ref.shape[1] - 3 * d_attn - 2 * d_rg)):
        ref[...] = _dot(u, w_ref[:, lo:lo + width]).astype(ref.dtype)
        lo += width


def _in_proj(x2, g, w_bf, d_attn, d_rg, d_s5):
    n, dm = x2.shape
    tm = min(ROW_TILE, n)
    row = lambda w: pl.BlockSpec((tm, w), lambda i: (i, 0))
    return pl.pallas_call(
        functools.partial(_in_proj_kernel, d_attn=d_attn, d_rg=d_rg),
        grid=(n // tm,),
        in_specs=[row(dm), _const_spec((1, dm)), _const_spec(w_bf.shape)],
        out_specs=[row(d_attn), row(d_attn), row(d_attn), row(d_rg), row(d_rg), row(d_s5)],
        out_shape=[jax.ShapeDtypeStruct((n, d_attn), BF16)] * 3
        + [jax.ShapeDtypeStruct((n, d_rg), F32)] * 2 + [jax.ShapeDtypeStruct((n, d_s5), F32)],
        compiler_params=_cparams("parallel"),
        name="in_proj",
    )(x2, g.reshape(1, dm), w_bf)


def _attn_kernel(q_ref, kp_ref, kc_ref, vp_ref, vc_ref, bias_ref, o_ref, kz, vz, *, n_pairs):
    qi = pl.program_id(1)
    tq = q_ref.shape[0]
    kz[0:tq, :] = kp_ref[...]
    kz[tq:2 * tq, :] = kc_ref[...]
    vz[0:tq, :] = vp_ref[...]
    vz[tq:2 * tq, :] = vc_ref[...]
    lane = lax.broadcasted_iota(I32, (CHUNK, LANES), 1)
    first = lane < HEAD_DIM
    kpos = lax.broadcasted_iota(I32, (1, BAND), 1)

    def chunk_body(c, carry):
        r0 = pl.multiple_of(c * CHUNK, CHUNK)
        first_valid = jnp.where(qi == 0, N_PREV * CHUNK - c * CHUNK, 0)
        neg = jnp.where(kpos >= first_valid, 0.0, NEG_BIG)
        for hp in range(n_pairs):
            cols = slice(hp * LANES, (hp + 1) * LANES)
            q2 = q_ref[pl.ds(r0, CHUNK), cols] * jnp.asarray(HEAD_DIM ** -0.5, BF16)
            zero = jnp.zeros_like(q2)
            qq = jnp.concatenate([jnp.where(first, q2, zero), jnp.where(first, zero, q2)], axis=0)
            kw = kz[pl.ds(r0, BAND), cols]
            vw = vz[pl.ds(r0, BAND), cols]
            s = lax.dot_general(qq, kw, (((1,), (1,)), ((), ())), preferred_element_type=F32)
            s = s + bias_ref[hp] + neg
            m = jnp.max(s, axis=-1, keepdims=True)
            p = jnp.exp(s - m)
            l = jnp.sum(p, axis=-1, keepdims=True)
            o2 = _dot(p.astype(BF16), vw) / l
            o_ref[pl.ds(r0, CHUNK), cols] = jnp.where(first, o2[0:CHUNK], o2[CHUNK:2 * CHUNK])
        return carry

    lax.fori_loop(0, tq // CHUNK, chunk_body, 0)


def _attention(q, k, v, bias2, bsz, l):
    n, da = q.shape
    tq = ATTN_TQ
    nt = l // tq
    n_pairs = da // LANES
    cur = pl.BlockSpec((tq, da), lambda b, i: (b * nt + i, 0))
    prev = pl.BlockSpec((tq, da), lambda b, i: (b * nt + jnp.maximum(i - 1, 0), 0))
    return pl.pallas_call(
        functools.partial(_attn_kernel, n_pairs=n_pairs),
        grid=(bsz, nt),
        in_specs=[cur, prev, cur, prev, cur, _const_spec(bias2.shape)],
        out_specs=cur,
        out_shape=jax.ShapeDtypeStruct((n, da), F32),
        scratch_shapes=[pltpu.VMEM((2 * tq, da), BF16), pltpu.VMEM((2 * tq, da), BF16)],
        compiler_params=_cparams("parallel", "parallel"),
        name="chunk_attn",
    )(q, k, k, v, v, bias2)


def _attn_bias(rel_bias):
    h = rel_bias.shape[0]
    qpos = jnp.arange(CHUNK)[:, None] + N_PREV * CHUNK
    kpos = jnp.arange(BAND)[None, :]
    rel = jnp.clip(qpos - kpos, -REL_CLIP, REL_CLIP) + REL_CLIP
    bias = rel_bias.astype(F32)[:, rel]
    return bias.reshape(h // 2, 2 * CHUNK, BAND)


def _rglru_kernel(x_ref, gt_ref, cw_ref, cb_ref, w_ref, b_ref, sp_ref, o_ref, xpad, a_s, b_s, hcar):
    t = pl.program_id(1)
    tl, c = x_ref.shape
    seg = tl // RG_SEG
    front = 8

    @pl.when(t == 0)
    def _():
        xpad[0:front, :] = jnp.zeros((front, c), F32)
        hcar[...] = jnp.zeros_like(hcar)

    xpad[front:front + tl, :] = x_ref[...]
    xc = cb_ref[...] + sum(
        cw_ref[j:j + 1, :] * xpad[front - (RG_CONV_WIDTH - 1) + j:front - (RG_CONV_WIDTH - 1) + j + tl, :]
        for j in range(RG_CONV_WIDTH))
    xpad[0:front, :] = xpad[tl:tl + front, :]
    pre = _dot(xc.astype(BF16), w_ref[...]) + b_ref[...]
    gx = _sigmoid(pre[:, 0:c])
    ga = _sigmoid(pre[:, c:2 * c])
    log_a = -RG_C * ga * sp_ref[...]
    a = jnp.exp(log_a)
    mult = jnp.sqrt(-jnp.tanh(log_a) * (a * a + 1.0))
    b = mult * gx * xc
    nh = c // LANES
    for j in range(nh):
        a_s[j] = a[:, j * LANES:(j + 1) * LANES]
        b_s[j] = b[:, j * LANES:(j + 1) * LANES]

    def step(i, carry):
        rows = pl.ds(i, RG_SEG, stride=seg)
        out = []
        for j in range(nh):
            h, p = carry[j]
            av = a_s[j, rows, :]
            h = av * h + b_s[j, rows, :]
            p = av * p
            b_s[j, rows, :] = h
            a_s[j, rows, :] = p
            out.append((h, p))
        return tuple(out)

    init = tuple((jnp.zeros((RG_SEG, LANES), F32), jnp.ones((RG_SEG, LANES), F32)) for _ in range(nh))
    ends = lax.fori_loop(0, seg, step, init)
    for j in range(nh):
        cols = slice(j * LANES, (j + 1) * LANES)
        h_end, p_end = ends[j]
        carry = hcar[:, cols]
        for s in range(RG_SEG):
            rows = slice(s * seg, (s + 1) * seg)
            h = b_s[j, rows, :] + a_s[j, rows, :] * carry
            o_ref[rows, cols] = h * _gelu(gt_ref[rows, cols])
            carry = h_end[s:s + 1, :] + p_end[s:s + 1, :] * carry
        hcar[:, cols] = carry


def _rglru(xr, gate, conv_w, conv_b, w_bd, b_cat, sp, bsz, l):
    n, c = xr.shape
    tl = min(RG_TL, l)
    nt = l // tl
    row = pl.BlockSpec((tl, c), lambda b, t: (b * nt + t, 0))
    return pl.pallas_call(
        _rglru_kernel,
        grid=(bsz, nt),
        in_specs=[row, row, _const_spec(conv_w.shape), _const_spec((1, c)), _const_spec(w_bd.shape),
                  _const_spec((1, 2 * c)), _const_spec((1, c))],
        out_specs=row,
        out_shape=jax.ShapeDtypeStruct((n, c), F32),
        scratch_shapes=[pltpu.VMEM((tl + 8, c), F32), pltpu.VMEM((c // LANES, tl, LANES), F32),
                        pltpu.VMEM((c // LANES, tl, LANES), F32), pltpu.VMEM((1, c), F32)],
        compiler_params=_cparams("parallel", "arbitrary"),
        name="rglru",
    )(xr, gate, conv_w, conv_b.reshape(1, c), w_bd, b_cat.reshape(1, 2 * c), sp.reshape(1, c))


def _block_diag(w):
    nb, d, _ = w.shape
    eye = jnp.eye(nb, dtype=w.dtype)
    return (eye[:, None, :, None] * w[:, :, None, :]).reshape(nb * d, nb * d)


def _s5_kernel(u_ref, m_ref, wa_ref, wb_ref, wo_ref, c1_ref, c2a_ref, c2b_ref, d_ref, y_ref, xa_s, xb_s, *, bsz):
    u = u_ref[0]
    ub = u.astype(BF16)
    nk = u.shape[0] // bsz
    xa_s[...] = _dot(ub, wa_ref[0])
    xb_s[...] = _dot(ub, wb_ref[0])
    c1, c2a, c2b = c1_ref[0], c2a_ref[0], c2b_ref[0]

    def step(k, carry):
        xa, xb = carry
        rows = pl.ds(k, bsz, stride=nk)
        ia = xa_s[rows, :]
        ib = xb_s[rows, :]
        xa_s[rows, :] = xa
        return c1 * xa + c2a * xb + ia, c1 * xb + c2b * xa + ib

    z = jnp.zeros((bsz, xa_s.shape[1]), F32)
    lax.fori_loop(0, nk, step, (z, z))
    y = _dot(ub, m_ref[0]) + _dot(xa_s[...].astype(BF16), wo_ref[0]) + d_ref[0] * u
    y_ref[0] = _gelu(y)


def _s5(ug, prm, bsz):
    g, r, w = ug.shape
    m, wa, wb, wo, c1, c2a, c2b, dt = prm
    p2 = wa.shape[2]
    grp = lambda a: pl.BlockSpec((1,) + a.shape[1:], lambda i: (i,) + (0,) * (a.ndim - 1))
    return pl.pallas_call(
        functools.partial(_s5_kernel, bsz=bsz),
        grid=(g,),
        in_specs=[grp(ug)] + [grp(a) for a in prm],
        out_specs=grp(ug),
        out_shape=jax.ShapeDtypeStruct((g, r, w), F32),
        scratch_shapes=[pltpu.VMEM((r, p2), F32), pltpu.VMEM((r, p2), F32)],
        compiler_params=_cparams("parallel"),
        name="s5",
    )(ug, *prm)


def _s5_params(a_re, a_im, log_dt, b_re, b_im, c_re, c_im, d):
    t = S5_T
    g, p = a_re.shape
    gd = S5_GROUP_DIM
    hi = lax.Precision.HIGHEST
    a = lax.complex(a_re.astype(F32), a_im.astype(F32))
    dt = jnp.exp(log_dt.astype(F32))[:, None]
    a_bar = jnp.exp(a * dt)
    bm = lax.complex(b_re.astype(F32), b_im.astype(F32))
    cm = lax.complex(c_re.astype(F32), c_im.astype(F32))
    b_bar = ((a_bar - 1.0) / a)[..., None] * bm
    steps = jnp.arange(t + 1, dtype=F32)
    pw = jnp.exp((a * dt)[:, None, :] * steps[None, :, None])

    def cdot(spec, x, y):
        e = lambda u, v: jnp.einsum(spec, u, v, precision=hi)
        return (e(x.real, y.real) - e(x.imag, y.imag)), (e(x.real, y.imag) + e(x.imag, y.real))

    cp = cm[:, None, :, :] * pw[:, :, None, :]
    kr, _ = cdot('glop,gpc->gloc', cp[:, :t], b_bar)
    lag = jnp.arange(t)[None, :] - jnp.arange(t)[:, None]
    kt = kr[:, jnp.clip(lag, 0, t - 1)]
    kt = jnp.where((lag >= 0)[None, :, :, None, None], kt, 0.0)
    m = kt.transpose(0, 1, 4, 2, 3).reshape(g, t * gd, t * gd)
    win = pw[:, t - 1 - jnp.arange(t)][:, :, :, None] * b_bar[:, None]
    win = win.transpose(0, 1, 3, 2).reshape(g, t * gd, p)
    wa = jnp.concatenate([win.real, win.imag], axis=-1)
    wb = jnp.concatenate([win.imag, win.real], axis=-1)
    co = cp[:, 1:].transpose(0, 3, 1, 2).reshape(g, p, t * gd)
    wo = jnp.concatenate([co.real, -co.imag], axis=1)
    lt = pw[:, t]
    c1 = jnp.concatenate([lt.real, lt.real], axis=-1)[:, None, :]
    c2a = jnp.concatenate([-lt.imag, lt.imag], axis=-1)[:, None, :]
    c2b = jnp.concatenate([lt.imag, -lt.imag], axis=-1)[:, None, :]
    dtile = jnp.tile(d.astype(F32).reshape(g, 1, gd), (1, t, 1)).reshape(g, 1, t * gd)
    return (m.astype(BF16), wa.astype(BF16), wb.astype(BF16), wo.astype(BF16), c1, c2a, c2b, dtile)


def _to_groups(us, bsz, l):
    gd, t = S5_GROUP_DIM, S5_T
    g = us.shape[1] // gd
    return us.reshape(bsz * l // t, t, g, gd).transpose(2, 0, 1, 3).reshape(g, bsz * l // t, t * gd)


def _from_groups(yg):
    gd, t = S5_GROUP_DIM, S5_T
    g, r, _ = yg.shape
    return yg.reshape(g, r, t, gd).transpose(1, 2, 0, 3).reshape(r * t, g * gd)


def _out_proj_kernel(ya_ref, yr_ref, ys_ref, x_ref, gg_ref, wglu_ref, wo_ref, gf_ref, xo_ref, h_ref):
    da, dr = ya_ref.shape[1], yr_ref.shape[1]
    ys = ys_ref[...]
    ys = ys * _sigmoid(_dot(ys.astype(BF16), wglu_ref[...]))
    acc = x_ref[...]
    lo = 0
    for y in (ya_ref[...], yr_ref[...], ys):
        w = y.shape[1]
        acc = acc + _dot(_rms(y, gg_ref[:, lo:lo + w]).astype(BF16), wo_ref[lo:lo + w, :])
        lo += w
    xo_ref[...] = acc
    h_ref[...] = _rms(acc, gf_ref[...]).astype(BF16)


def _out_proj(ya, yr, ys, x2, gg, wglu_bf, wo_bf, gf):
    n, dm = x2.shape
    tm = min(ROW_TILE, n)
    row = lambda w: pl.BlockSpec((tm, w), lambda i: (i, 0))
    return pl.pallas_call(
        _out_proj_kernel,
        grid=(n // tm,),
        in_specs=[row(ya.shape[1]), row(yr.shape[1]), row(ys.shape[1]), row(dm), _const_spec((1, gg.shape[0])),
                  _const_spec(wglu_bf.shape), _const_spec(wo_bf.shape), _const_spec((1, dm))],
        out_specs=[row(dm), row(dm)],
        out_shape=[jax.ShapeDtypeStruct((n, dm), F32), jax.ShapeDtypeStruct((n, dm), BF16)],
        compiler_params=_cparams("parallel"),
        name="out_proj",
    )(ya, yr, ys, x2, gg.reshape(1, -1), wglu_bf, wo_bf, gf.reshape(1, dm))


def _swiglu_chunks(h, w1, w3, w2, chunks):
    acc = None
    for lo, width in chunks:
        a = _dot(h, w1[:, lo:lo + width])
        b = _dot(h, w3[:, lo:lo + width])
        t = (a * _sigmoid(a) * b).astype(BF16)
        y = _dot(t, w2[lo:lo + width, :])
        acc = y if acc is None else acc + y
    return acc


def _ffn_chunks(f):
    if f == sum(w for _, w in FFN_CHUNKS):
        return FFN_CHUNKS
    return ((0, f),)


def _dense_ffn_kernel(h_ref, x_ref, w1_ref, w3_ref, w2_ref, o_ref, *, chunks):
    o_ref[...] = x_ref[...] + _swiglu_chunks(h_ref[...], w1_ref, w3_ref, w2_ref, chunks)


def _dense_ffn(h, x2, w1, w3, w2):
    n, dm = x2.shape
    tm = min(ROW_TILE, n)
    row = pl.BlockSpec((tm, dm), lambda i: (i, 0))
    return pl.pallas_call(
        functools.partial(_dense_ffn_kernel, chunks=_ffn_chunks(w1.shape[1])),
        grid=(n // tm,),
        in_specs=[row, row, _const_spec(w1.shape), _const_spec(w3.shape), _const_spec(w2.shape)],
        out_specs=row,
        out_shape=jax.ShapeDtypeStruct((n, dm), F32),
        compiler_params=_cparams("parallel"),
        name="dense_ffn",
    )(h, x2, w1, w3, w2)


def _expert_ffn_kernel(brow_ref, bexp_ref, nblk_ref, x_ref, w1_ref, w3_ref, w2_ref, o_ref, *, chunks):
    @pl.when(pl.program_id(0) < nblk_ref[0])
    def _():
        o_ref[...] = _swiglu_chunks(x_ref[...], w1_ref.at[0], w3_ref.at[0], w2_ref.at[0], chunks).astype(o_ref.dtype)


def _expert_ffn(xb, w1, w3, w2, blk_row, blk_exp, nblk, max_blocks):
    rows, dm = xb.shape
    f = w1.shape[2]
    row = pl.BlockSpec((MOE_BLK, dm), lambda i, br, be, nb: (br[i], 0))
    wspec = lambda a: pl.BlockSpec((1,) + a.shape[1:], lambda i, br, be, nb: (be[i], 0, 0))
    return pl.pallas_call(
        functools.partial(_expert_ffn_kernel, chunks=_ffn_chunks(f)),
        grid_spec=pltpu.PrefetchScalarGridSpec(
            num_scalar_prefetch=3, grid=(max_blocks,),
            in_specs=[row, wspec(w1), wspec(w3), wspec(w2)], out_specs=row),
        out_shape=jax.ShapeDtypeStruct((rows, dm), BF16),
        compiler_params=_cparams("arbitrary"),
        name="expert_ffn",
    )(blk_row, blk_exp, nblk, xb, w1, w3, w2)


def _route_kernel(x_ref, gf_ref, rt_ref, pos_ref, rank_ref, gate_ref, cnt_ref):
    tm = x_ref.shape[0]
    ne = rt_ref.shape[0]
    hi = lax.Precision.HIGHEST
    h = _rms(x_ref[...], gf_ref[...])
    logits = lax.dot_general(rt_ref[...], h, (((1,), (1,)), ((), ())), precision=hi,
                             preferred_element_type=F32)
    eidx = lax.broadcasted_iota(I32, (ne, tm), 0)
    m1 = jnp.max(logits, axis=0, keepdims=True)
    i1 = jnp.min(jnp.where(logits == m1, eidx, ne), axis=0, keepdims=True)
    sel1 = eidx == i1
    rest = jnp.where(sel1, -jnp.inf, logits)
    m2 = jnp.max(rest, axis=0, keepdims=True)
    i2 = jnp.min(jnp.where(rest == m2, eidx, ne), axis=0, keepdims=True)
    sel2 = eidx == i2
    e2 = jnp.exp(m2 - m1)
    den = 1.0 + e2
    routed = sel1 | sel2
    gate = jnp.where(sel1, 1.0 / den, jnp.where(sel2, e2 / den, 0.0))
    rf = jnp.where(routed, 1.0, 0.0)
    cnt = jnp.sum(rf, axis=1, keepdims=True)
    before = (lax.broadcasted_iota(I32, (tm, tm), 0) < lax.broadcasted_iota(I32, (tm, tm), 1))
    rank = _dot(rf.astype(BF16), jnp.where(before, 1.0, 0.0).astype(BF16))
    seg = jnp.floor((cnt + (BF16_ROWS - 1)) * (1.0 / BF16_ROWS)) * BF16_ROWS
    ecol = lax.broadcasted_iota(I32, (ne, 1), 0)
    off = jnp.zeros((ne, 1), F32)
    for j in range(ne - 1):
        off = off + jnp.where(ecol > j, seg[j:j + 1, :], 0.0)
    pos_ref[...] = jnp.where(routed, off + rank, -1.0).astype(I32)
    cnt_ref[0] = jnp.broadcast_to(cnt, (ne, LANES)).astype(I32)
    eye = jnp.where(lax.broadcasted_iota(I32, (tm, tm), 0) == lax.broadcasted_iota(I32, (tm, tm), 1), 1.0, 0.0)
    tr = lambda a: lax.dot_general(eye, a, (((1,), (1,)), ((), ())), precision=hi, preferred_element_type=F32)
    rank_ref[...] = tr(jnp.where(routed, rank, -1.0)).astype(I32)
    gate_ref[...] = tr(gate)


def _route(x2, gf, router_t):
    n, dm = x2.shape
    ne = router_t.shape[0]
    tm = MOE_TILE
    nt = n // tm
    return pl.pallas_call(
        _route_kernel,
        grid=(nt,),
        in_specs=[pl.BlockSpec((tm, dm), lambda i: (i, 0)), _const_spec((1, dm)), _const_spec(router_t.shape)],
        out_specs=[pl.BlockSpec((ne, tm), lambda i: (0, i)), pl.BlockSpec((tm, ne), lambda i: (i, 0)),
                   pl.BlockSpec((tm, ne), lambda i: (i, 0)), pl.BlockSpec((1, ne, LANES), lambda i: (i, 0, 0))],
        out_shape=[jax.ShapeDtypeStruct((ne, n), I32), jax.ShapeDtypeStruct((n, ne), I32),
                   jax.ShapeDtypeStruct((n, ne), F32), jax.ShapeDtypeStruct((nt, ne, LANES), I32)],
        compiler_params=_cparams("parallel"),
        name="moe_route",
    )(x2, gf.reshape(1, dm), router_t)


def _dispatch_copies(base_ref, off_ref, res, xb_ref, sems, tile, slot, ne, tm):
    out = []
    for e in range(ne):
        src = pl.multiple_of(off_ref[tile * ne + e], BF16_ROWS)
        dst = pl.multiple_of(base_ref[tile * ne + e], BF16_ROWS)
        out.append(pltpu.make_async_copy(res.at[slot, pl.ds(src, tm)], xb_ref.at[pl.ds(dst, tm)], sems.at[slot, e]))
    return out


def _tail_copies(end_ref, res, xb_ref, sems, mrows, ne):
    return [pltpu.make_async_copy(res.at[0, pl.ds(mrows, MOE_BLK)],
                                  xb_ref.at[pl.ds(pl.multiple_of(end_ref[e], BF16_ROWS), MOE_BLK)], sems.at[0, e])
            for e in range(ne)]


def _dispatch_kernel(base_ref, off_ref, end_ref, pos_ref, h_ref, xb_ref, res, sems):
    i = pl.program_id(0)
    nt = pl.num_programs(0)
    ne, tm = pos_ref.shape
    mrows = 2 * tm + ne * BF16_ROWS
    slot = i % 2

    @pl.when(i == 0)
    def _():
        res[...] = jnp.zeros_like(res)

    row = lax.broadcasted_iota(I32, (mrows, tm), 0)
    hit = row == pos_ref[0:1, :]
    for e in range(1, ne):
        hit = hit | (row == pos_ref[e:e + 1, :])
    perm = jnp.where(hit, 1.0, 0.0).astype(BF16)
    res[slot, 0:mrows, :] = _dot(perm, h_ref[...]).astype(BF16)

    @pl.when(i > 0)
    def _():
        for cp in _dispatch_copies(base_ref, off_ref, res, xb_ref, sems, i - 1, 1 - slot, ne, tm):
            cp.wait()

    for cp in _dispatch_copies(base_ref, off_ref, res, xb_ref, sems, i, slot, ne, tm):
        cp.start()

    @pl.when(i == nt - 1)
    def _():
        for cp in _dispatch_copies(base_ref, off_ref, res, xb_ref, sems, i, slot, ne, tm):
            cp.wait()
        for cp in _tail_copies(end_ref, res, xb_ref, sems, mrows, ne):
            cp.start()
        for cp in _tail_copies(end_ref, res, xb_ref, sems, mrows, ne):
            cp.wait()


def _dispatch(h, pos_t, base, off, end, rows_total):
    n, dm = h.shape
    ne = pos_t.shape[0]
    tm = MOE_TILE
    nt = n // tm
    mrows = 2 * tm + ne * BF16_ROWS
    return pl.pallas_call(
        _dispatch_kernel,
        grid_spec=pltpu.PrefetchScalarGridSpec(
            num_scalar_prefetch=3, grid=(nt,),
            in_specs=[pl.BlockSpec((ne, tm), lambda i, *_: (0, i)), pl.BlockSpec((tm, dm), lambda i, *_: (i, 0))],
            out_specs=pl.BlockSpec(memory_space=pl.ANY),
            scratch_shapes=[pltpu.VMEM((2, mrows + tm, dm), BF16), pltpu.SemaphoreType.DMA((2, ne))]),
        out_shape=jax.ShapeDtypeStruct((rows_total, dm), BF16),
        compiler_params=_cparams("arbitrary"),
        name="moe_dispatch",
    )(base, off, end, pos_t, h)


def _combine_fetch(start_ref, live_ref, yb_ref, win, sems, tile, slot, ne, tm, wait):
    for e in range(ne):
        src = pl.multiple_of(start_ref[tile * ne + e], BF16_ROWS)
        cp = pltpu.make_async_copy(yb_ref.at[pl.ds(src, tm)], win.at[slot, e], sems.at[slot, e])

        @pl.when(live_ref[tile * ne + e] > 0)
        def _():
            cp.wait() if wait else cp.start()


def _combine_kernel(start_ref, shift_ref, live_ref, rank_ref, gate_ref, x_ref, gn_ref, yb_ref, o_ref, win, sems):
    i = pl.program_id(0)
    nt = pl.num_programs(0)
    tm, ne = rank_ref.shape
    slot = i % 2

    @pl.when(i == 0)
    def _():
        win[...] = jnp.zeros_like(win)
        _combine_fetch(start_ref, live_ref, yb_ref, win, sems, i, slot, ne, tm, wait=False)

    @pl.when(i + 1 < nt)
    def _():
        _combine_fetch(start_ref, live_ref, yb_ref, win, sems, i + 1, 1 - slot, ne, tm, wait=False)

    _combine_fetch(start_ref, live_ref, yb_ref, win, sems, i, slot, ne, tm, wait=True)

    col = lax.broadcasted_iota(I32, (tm, tm), 1)
    rank = rank_ref[...]
    gate = gate_ref[...]
    y = jnp.zeros(o_ref.shape, F32)
    for e in range(ne):
        want = jnp.where(rank[:, e:e + 1] >= 0, rank[:, e:e + 1] + shift_ref[i * ne + e], -1)
        sel = jnp.where(col == want, 1.0, 0.0).astype(BF16)
        y = y + gate[:, e:e + 1] * _dot(sel, win[slot, e])
    o_ref[...] = _rms(x_ref[...] + y, gn_ref[...])


def _combine(yb, rank, gate, x2, gn, start, shift, live):
    n, dm = x2.shape
    ne = rank.shape[1]
    tm = MOE_TILE
    nt = n // tm
    tok = lambda w: pl.BlockSpec((tm, w), lambda i, *_: (i, 0))
    return pl.pallas_call(
        _combine_kernel,
        grid_spec=pltpu.PrefetchScalarGridSpec(
            num_scalar_prefetch=3, grid=(nt,),
            in_specs=[tok(ne), tok(ne), tok(dm), pl.BlockSpec((1, dm), lambda i, *_: (0, 0)),
                      pl.BlockSpec(memory_space=pl.ANY)],
            out_specs=tok(dm),
            scratch_shapes=[pltpu.VMEM((2, ne, tm, dm), BF16), pltpu.SemaphoreType.DMA((2, ne))]),
        out_shape=jax.ShapeDtypeStruct((n, dm), F32),
        compiler_params=_cparams("arbitrary"),
        name="moe_combine",
    )(start, shift, live, rank, gate, x2, gn.reshape(1, dm), yb)


def _moe_layout(cnt, n):
    nt, ne = cnt.shape
    tm, blk = MOE_TILE, MOE_BLK
    seg = (cnt + BF16_ROWS - 1) // BF16_ROWS * BF16_ROWS
    cap = -(-(n + nt * BF16_ROWS + tm) // blk) * blk
    within = jnp.cumsum(seg, axis=0) - seg
    base = (within + jnp.arange(ne, dtype=I32)[None, :] * cap).astype(I32)
    off = (jnp.cumsum(seg, axis=1) - seg).astype(I32)
    rows_e = jnp.sum(seg, axis=0)
    nblk_e = (rows_e + blk - 1) // blk
    region = jnp.arange(ne, dtype=I32) * cap
    end = (region + rows_e).astype(I32)
    start = jnp.minimum(base, jnp.maximum(region + nblk_e * blk - tm, region)[None, :]).astype(I32)
    shift = (base - start).astype(I32)
    live = (cnt > 0).astype(I32)
    ends = jnp.cumsum(nblk_e)
    max_blocks = (2 * n + nt * ne * BF16_ROWS) // blk + ne
    bid = jnp.arange(max_blocks + 1, dtype=I32)
    bexp = jnp.clip(jnp.searchsorted(ends, bid, side='right'), 0, ne - 1).astype(I32)
    brow = bexp * (cap // blk) + bid - (ends - nblk_e)[bexp]
    nblk = ends[-1:].astype(I32)
    last = jnp.maximum(nblk[0] - 1, 0)
    used = bid < nblk[0]
    bexp = jnp.where(used, bexp, bexp[last]).astype(I32)
    brow = jnp.where(used, brow, brow[last]).astype(I32)
    flat = lambda a: a.reshape(-1)
    return dict(base=flat(base), off=flat(off), end=end, start=flat(start), shift=flat(shift), live=flat(live),
                brow=brow, bexp=bexp, nblk=nblk, rows_total=ne * cap, max_blocks=max_blocks)


def _moe(x2, h, gf, gn, router, w1, w3, w2):
    n, dm = x2.shape
    pos_t, rank, gate, cnt = _route(x2, gf, router.astype(F32).T)
    lay = _moe_layout(cnt[:, :, 0], n)
    xb = _dispatch(h, pos_t, lay['base'], lay['off'], lay['end'], lay['rows_total'])
    yb = _expert_ffn(xb, w1.astype(BF16), w3.astype(BF16), w2.astype(BF16), lay['brow'], lay['bexp'], lay['nblk'],
                     lay['max_blocks'])
    return _combine(yb, rank, gate, x2, gn, lay['start'], lay['shift'], lay['live'])


def kernel(x, norm_mix_g, w_in, attn_rel_bias, rg_conv_w, rg_conv_b, rg_wx, rg_bx, rg_wa, rg_ba, rg_lambda, s5_a_re, s5_a_im, s5_log_dt, s5_b_re, s5_b_im, s5_c_re, s5_c_im, s5_d, s5_w_glu, g_group, w_out, norm_ffn_g, ffn_w1, ffn_w3, ffn_w2, moe_router, moe_w1, moe_w3, moe_w2, final_norm_g):
    bsz, l, dm = x.shape
    depth = w_in.shape[0]
    assert depth == 2 and l % ATTN_TQ == 0 and ATTN_TQ == N_PREV * CHUNK, "dense layer, then the MoE layer"
    d_rg = rg_conv_w.shape[2]
    d_s5 = s5_w_glu.shape[1]
    d_attn = (w_in.shape[2] - 2 * d_rg - d_s5) // 3
    x2 = x.reshape(bsz * l, dm)
    for layer in range(depth):
        q, k, v, xr, gate, us = _in_proj(x2, norm_mix_g[layer], w_in[layer].astype(BF16), d_attn, d_rg, d_s5)
        y_attn = _attention(q, k, v, _attn_bias(attn_rel_bias[layer]), bsz, l)
        w_gates = jnp.concatenate([_block_diag(rg_wx[layer]), _block_diag(rg_wa[layer])], axis=1).astype(BF16)
        b_gates = jnp.concatenate([rg_bx[layer], rg_ba[layer]]).astype(F32)
        y_rg = _rglru(xr, gate, rg_conv_w[layer].astype(F32), rg_conv_b[layer].astype(F32), w_gates, b_gates,
                      jax.nn.softplus(-rg_lambda[layer].astype(F32)), bsz, l)
        prm = _s5_params(s5_a_re[layer], s5_a_im[layer], s5_log_dt[layer], s5_b_re[layer], s5_b_im[layer],
                         s5_c_re[layer], s5_c_im[layer], s5_d[layer])
        y_s5 = _from_groups(_s5(_to_groups(us, bsz, l), prm, bsz))
        x2, h = _out_proj(y_attn, y_rg, y_s5, x2, g_group[layer].astype(F32), s5_w_glu[layer].astype(BF16),
                          w_out[layer].astype(BF16), norm_ffn_g[layer].astype(F32))
        if layer == 0:
            x2 = _dense_ffn(h, x2, ffn_w1[0].astype(BF16), ffn_w3[0].astype(BF16), ffn_w2[0].astype(BF16))
        else:
            x2 = _moe(x2, h, norm_ffn_g[layer].astype(F32), final_norm_g.astype(F32), moe_router[0],
                      moe_w1[0], moe_w3[0], moe_w2[0])
    return x2.reshape(bsz, l, dm)
```

```python
import functools
import math

import jax
import jax.numpy as jnp
from jax import lax
from jax.experimental import pallas as pl
from jax.experimental.pallas import tpu as pltpu

F32 = jnp.float32
BF16 = jnp.bfloat16
I32 = jnp.int32

EPS = 1e-6
CHUNK = 64
N_PREV = 8
BAND = (N_PREV + 1) * CHUNK
REL_CLIP = 128
HEAD_DIM = 64
RG_C = 8.0
RG_CONV_WIDTH = 4
S5_GROUP_DIM = 16
S5_T = 16
TOP_K = 2
NEG_BIG = -1e30

LANES = 128
BF16_ROWS = 16
VMEM_LIMIT = 52 * 1024 * 1024

ROW_TILE = 512
ATTN_TQ = 512
RG_TL = 1024
RG_SEG = 8
MOE_TILE = 256
MOE_BLK = 256
FFN_CHUNKS = ((0, 512), (512, 512), (1024, 512), (1536, 512), (2048, 512), (2560, 256))


def _cparams(*sem):
    return pltpu.CompilerParams(dimension_semantics=sem, vmem_limit_bytes=VMEM_LIMIT)


def _const_spec(shape):
    nd = len(shape)
    return pl.BlockSpec(shape, lambda *_: (0,) * nd, pipeline_mode=pl.Buffered(1))


def _rms(xf, g):
    var = jnp.mean(xf * xf, axis=-1, keepdims=True)
    return xf * lax.rsqrt(var + EPS) * g


def _sigmoid(x):
    return 1.0 / (1.0 + jnp.exp(-x))


def _gelu(x):
    c = math.sqrt(2.0 / math.pi)
    return 0.5 * x * (1.0 + jnp.tanh(c * (x + 0.044715 * (x * x * x))))


def _dot(a, b):
    return jnp.dot(a, b, preferred_element_type=F32)


def _in_proj_kernel(x_ref, g_ref, w_ref, q_ref, k_ref, v_ref, xr_ref, gt_ref, us_ref, *, d_attn, d_rg):
    u = _rms(x_ref[...], g_ref[...]).astype(BF16)
    lo = 0
    for ref, width in ((q_ref, d_attn), (k_ref, d_attn), (v_ref, d_attn),
                       (xr_ref, d_rg), (gt_ref, d_rg), (us_ref, w_ref.shape[1] - 3 * d_attn - 2 * d_rg)):
        ref[...] = _dot(u, w_ref[:, lo:lo + width]).astype(ref.dtype)
        lo += width


def _in_proj(x2, g, w_bf, d_attn, d_rg, d_s5):
    n, dm = x2.shape
    tm = min(ROW_TILE, n)
    row = lambda w: pl.BlockSpec((tm, w), lambda i: (i, 0))
    return pl.pallas_call(
        functools.partial(_in_proj_kernel, d_attn=d_attn, d_rg=d_rg),
        grid=(n // tm,),
        in_specs=[row(dm), _const_spec((1, dm)), _const_spec(w_bf.shape)],
        out_specs=[row(d_attn), row(d_attn), row(d_attn), row(d_rg), row(d_rg), row(d_s5)],
        out_shape=[jax.ShapeDtypeStruct((n, d_attn), BF16)] * 3
        + [jax.ShapeDtypeStruct((n, d_rg), F32)] * 2 + [jax.ShapeDtypeStruct((n, d_s5), F32)],
        compiler_params=_cparams("parallel"),
        name="in_proj",
    )(x2, g.reshape(1, dm), w_bf)


def _attn_kernel(q_ref, kp_ref, kc_ref, vp_ref, vc_ref, bias_ref, o_ref, kz, vz, *, n_pairs):
    qi = pl.program_id(1)
    tq = q_ref.shape[0]
    kz[0:tq, :] = kp_ref[...]
    kz[tq:2 * tq, :] = kc_ref[...]
    vz[0:tq, :] = vp_ref[...]
    vz[tq:2 * tq, :] = vc_ref[...]
    lane = lax.broadcasted_iota(I32, (CHUNK, LANES), 1)
    first = lane < HEAD_DIM
    kpos = lax.broadcasted_iota(I32, (1, BAND), 1)
    pairs = [slice(hp * LANES, (hp + 1) * LANES) for hp in range(n_pairs)]

    def chunk_body(c, carry, *, masked):
        r0 = pl.multiple_of(c * CHUNK, CHUNK)
        scores = []
        for hp, cols in enumerate(pairs):
            q2 = q_ref[pl.ds(r0, CHUNK), cols] * jnp.asarray(HEAD_DIM ** -0.5, BF16)
            zero = jnp.zeros_like(q2)
            qq = jnp.concatenate([jnp.where(first, q2, zero), jnp.where(first, zero, q2)], axis=0)
            s = lax.dot_general(qq, kz[pl.ds(r0, BAND), cols], (((1,), (1,)), ((), ())), preferred_element_type=F32)
            scores.append(s + bias_ref[hp])
        if masked:
            neg = jnp.where(kpos >= N_PREV * CHUNK - c * CHUNK, 0.0, NEG_BIG)
            scores = [s + neg for s in scores]
        probs, sums = [], []
        for s in scores:
            p = jnp.exp(s - jnp.max(s, axis=-1, keepdims=True))
            sums.append(jnp.sum(p, axis=-1, keepdims=True))
            probs.append(p.astype(BF16))
        for cols, p, l in zip(pairs, probs, sums):
            o2 = _dot(p, vz[pl.ds(r0, BAND), cols]) / l
            o_ref[pl.ds(r0, CHUNK), cols] = jnp.where(first, o2[0:CHUNK], o2[CHUNK:2 * CHUNK])
        return carry

    @pl.when(qi == 0)
    def _():
        lax.fori_loop(0, tq // CHUNK, functools.partial(chunk_body, masked=True), 0)

    @pl.when(qi != 0)
    def _():
        lax.fori_loop(0, tq // CHUNK, functools.partial(chunk_body, masked=False), 0)


def _attention(q, k, v, bias2, bsz, l):
    n, da = q.shape
    tq = ATTN_TQ
    nt = l // tq
    n_pairs = da // LANES
    cur = pl.BlockSpec((tq, da), lambda b, i: (b * nt + i, 0))
    prev = pl.BlockSpec((tq, da), lambda b, i: (b * nt + jnp.maximum(i - 1, 0), 0))
    return pl.pallas_call(
        functools.partial(_attn_kernel, n_pairs=n_pairs),
        grid=(bsz, nt),
        in_specs=[cur, prev, cur, prev, cur, _const_spec(bias2.shape)],
        out_specs=cur,
        out_shape=jax.ShapeDtypeStruct((n, da), F32),
        scratch_shapes=[pltpu.VMEM((2 * tq, da), BF16), pltpu.VMEM((2 * tq, da), BF16)],
        compiler_params=_cparams("parallel", "parallel"),
        name="chunk_attn",
    )(q, k, k, v, v, bias2)


def _attn_bias(rel_bias):
    h = rel_bias.shape[0]
    tab = rel_bias.astype(F32)
    n_far = N_PREV * CHUNK - REL_CLIP + CHUNK
    lo = 2 * REL_CLIP - (BAND + CHUNK - 1 - n_far)
    ext = jnp.concatenate([jnp.broadcast_to(tab[:, 2 * REL_CLIP:], (h, n_far)), tab[:, lo:2 * REL_CLIP][:, ::-1]], axis=1)
    bias = jnp.stack([ext[:, CHUNK - 1 - i:CHUNK - 1 - i + BAND] for i in range(CHUNK)], axis=1)
    return bias.reshape(h // 2, 2 * CHUNK, BAND)


def _rglru_kernel(x_ref, gt_ref, cw_ref, cb_ref, w_ref, b_ref, sp_ref, o_ref, xpad, a_s, b_s, hcar):
    t = pl.program_id(1)
    tl, c = x_ref.shape
    seg = tl // RG_SEG
    front = 8

    @pl.when(t == 0)
    def _():
        xpad[0:front, :] = jnp.zeros((front, c), F32)
        hcar[...] = jnp.zeros_like(hcar)

    xpad[front:front + tl, :] = x_ref[...]
    xc = cb_ref[...] + sum(
        cw_ref[j:j + 1, :] * xpad[front - (RG_CONV_WIDTH - 1) + j:front - (RG_CONV_WIDTH - 1) + j + tl, :]
        for j in range(RG_CONV_WIDTH))
    xpad[0:front, :] = xpad[tl:tl + front, :]
    pre = _dot(xc.astype(BF16), w_ref[...]) + b_ref[...]
    gx = _sigmoid(pre[:, 0:c])
    ga = _sigmoid(pre[:, c:2 * c])
    log_a = -RG_C * ga * sp_ref[...]
    a = jnp.exp(log_a)
    mult = jnp.sqrt(-jnp.tanh(log_a) * (a * a + 1.0))
    b = mult * gx * xc
    nh = c // LANES
    for j in range(nh):
        a_s[j] = a[:, j * LANES:(j + 1) * LANES]
        b_s[j] = b[:, j * LANES:(j + 1) * LANES]

    def step(i, carry):
        rows = pl.ds(i, RG_SEG, stride=seg)
        out = []
        for j in range(nh):
            h, p = carry[j]
            av = a_s[j, rows, :]
            h = av * h + b_s[j, rows, :]
            p = av * p
            b_s[j, rows, :] = h
            a_s[j, rows, :] = p
            out.append((h, p))
        return tuple(out)

    init = tuple((jnp.zeros((RG_SEG, LANES), F32), jnp.ones((RG_SEG, LANES), F32)) for _ in range(nh))
    ends = lax.fori_loop(0, seg, step, init, unroll=8)
    for j in range(nh):
        cols = slice(j * LANES, (j + 1) * LANES)
        h_end, p_end = ends[j]
        carry = hcar[:, cols]
        for s in range(RG_SEG):
            rows = slice(s * seg, (s + 1) * seg)
            h = b_s[j, rows, :] + a_s[j, rows, :] * carry
            o_ref[rows, cols] = h * _gelu(gt_ref[rows, cols])
            carry = h_end[s:s + 1, :] + p_end[s:s + 1, :] * carry
        hcar[:, cols] = carry


def _rglru(xr, gate, conv_w, conv_b, w_bd, b_cat, sp, bsz, l):
    n, c = xr.shape
    tl = min(RG_TL, l)
    nt = l // tl
    row = pl.BlockSpec((tl, c), lambda b, t: (b * nt + t, 0))
    return pl.pallas_call(
        _rglru_kernel,
        grid=(bsz, nt),
        in_specs=[row, row, _const_spec(conv_w.shape), _const_spec((1, c)), _const_spec(w_bd.shape),
                  _const_spec((1, 2 * c)), _const_spec((1, c))],
        out_specs=row,
        out_shape=jax.ShapeDtypeStruct((n, c), F32),
        scratch_shapes=[pltpu.VMEM((tl + 8, c), F32), pltpu.VMEM((c // LANES, tl, LANES), F32),
                        pltpu.VMEM((c // LANES, tl, LANES), F32), pltpu.VMEM((1, c), F32)],
        compiler_params=_cparams("parallel", "arbitrary"),
        name="rglru",
    )(xr, gate, conv_w, conv_b.reshape(1, c), w_bd, b_cat.reshape(1, 2 * c), sp.reshape(1, c))


def _block_diag(w):
    nb, d, _ = w.shape
    eye = jnp.eye(nb, dtype=w.dtype)
    return (eye[:, None, :, None] * w[:, :, None, :]).reshape(nb * d, nb * d)


def _s5_kernel(u_ref, m_ref, wa_ref, wb_ref, wo_ref, c1_ref, c2a_ref, c2b_ref, d_ref, y_ref, ps, ug, xa_s, xb_s):
    t, gd = S5_T, S5_GROUP_DIM
    ng, nk, _ = ug.shape
    nh = ps.shape[0]
    per_half = LANES // gd
    rows8 = 8

    for h in range(nh):
        ps[h] = u_ref[:, h * LANES:(h + 1) * LANES]

    def to_groups(r, carry):
        base = pl.multiple_of(r * rows8 * t, rows8 * t)
        steps = [[ps[h, pl.ds(base + s, rows8, stride=t), :] for h in range(nh)] for s in range(t)]
        for g in range(ng):
            lo = (g % per_half) * gd
            ug[g, pl.ds(pl.multiple_of(r * rows8, rows8), rows8), :] = jnp.concatenate(
                [steps[s][g // per_half][:, lo:lo + gd] for s in range(t)], axis=1)
        return carry

    lax.fori_loop(0, nk // rows8, to_groups, 0)

    def project(g, carry):
        u = ug[g]
        ub = u.astype(BF16)
        rows = pl.ds(pl.multiple_of(g * nk, nk), nk)
        xa_s[rows, :] = _dot(ub, wa_ref[g])
        xb_s[rows, :] = _dot(ub, wb_ref[g])
        ug[g] = _dot(ub, m_ref[g]) + d_ref[g] * u
        return carry

    lax.fori_loop(0, ng, project, 0)

    c1, c2a, c2b = c1_ref[...], c2a_ref[...], c2b_ref[...]

    def step(k, carry):
        xa, xb = carry
        rows = pl.ds(k, ng, stride=nk)
        ia = xa_s[rows, :]
        ib = xb_s[rows, :]
        xa_s[rows, :] = xa
        return c1 * xa + c2a * xb + ia, c1 * xb + c2b * xa + ib

    z = jnp.zeros((ng, xa_s.shape[1]), F32)
    lax.fori_loop(0, nk, step, (z, z))

    def respond(g, carry):
        rows = pl.ds(pl.multiple_of(g * nk, nk), nk)
        ug[g] = _gelu(ug[g] + _dot(xa_s[rows, :].astype(BF16), wo_ref[g]))
        return carry

    lax.fori_loop(0, ng, respond, 0)

    def from_groups(r, carry):
        base = pl.multiple_of(r * rows8 * t, rows8 * t)
        rows = pl.ds(pl.multiple_of(r * rows8, rows8), rows8)
        grp = [ug[g, rows, :] for g in range(ng)]
        for s in range(t):
            lo = (s % per_half) * gd + (s // per_half) * LANES
            for h in range(nh):
                ps[h, pl.ds(base + s, rows8, stride=t), :] = jnp.concatenate(
                    [grp[g][:, lo:lo + gd] for g in range(h * per_half, (h + 1) * per_half)], axis=1)
        return carry

    lax.fori_loop(0, nk // rows8, from_groups, 0)
    for h in range(nh):
        y_ref[:, h * LANES:(h + 1) * LANES] = ps[h]


def _s5(us, prm, bsz, l):
    n, w = us.shape
    ng = w // S5_GROUP_DIM
    nk = l // S5_T
    p2 = prm[1].shape[2]
    row = pl.BlockSpec((l, w), lambda b: (b, 0))
    return pl.pallas_call(
        _s5_kernel,
        grid=(bsz,),
        in_specs=[row] + [_const_spec(a.shape) for a in prm],
        out_specs=row,
        out_shape=jax.ShapeDtypeStruct((n, w), F32),
        scratch_shapes=[pltpu.VMEM((w // LANES, l, LANES), F32), pltpu.VMEM((ng, nk, S5_T * S5_GROUP_DIM), F32),
                        pltpu.VMEM((ng * nk, p2), F32), pltpu.VMEM((ng * nk, p2), F32)],
        compiler_params=_cparams("parallel"),
        name="s5",
    )(us, *prm)


def _s5_params(a_re, a_im, log_dt, b_re, b_im, c_re, c_im, d):
    t = S5_T
    g, p = a_re.shape
    gd = S5_GROUP_DIM
    hi = lax.Precision.HIGHEST
    a = lax.complex(a_re.astype(F32), a_im.astype(F32))
    dt = jnp.exp(log_dt.astype(F32))[:, None]
    a_bar = jnp.exp(a * dt)
    bm = lax.complex(b_re.astype(F32), b_im.astype(F32))
    cm = lax.complex(c_re.astype(F32), c_im.astype(F32))
    b_bar = ((a_bar - 1.0) / a)[..., None] * bm
    steps = jnp.arange(t + 1, dtype=F32)
    pw = jnp.exp((a * dt)[:, None, :] * steps[None, :, None])

    def cdot(spec, x, y):
        e = lambda u, v: jnp.einsum(spec, u, v, precision=hi)
        return (e(x.real, y.real) - e(x.imag, y.imag)), (e(x.real, y.imag) + e(x.imag, y.real))

    cp = cm[:, None, :, :] * pw[:, :, None, :]
    kr, _ = cdot('glop,gpc->gloc', cp[:, :t], b_bar)
    krt = kr.transpose(0, 3, 1, 2)
    m = jnp.stack([jnp.pad(krt[:, :, :t - s, :], ((0, 0), (0, 0), (s, 0), (0, 0))) for s in range(t)], axis=1)
    m = m.reshape(g, t * gd, t * gd)
    win = pw[:, :t][:, ::-1][:, :, :, None] * b_bar[:, None]
    win = win.transpose(0, 1, 3, 2).reshape(g, t * gd, p)
    wa = jnp.concatenate([win.real, win.imag], axis=-1)
    wb = jnp.concatenate([win.imag, win.real], axis=-1)
    co = cp[:, 1:].transpose(0, 3, 1, 2).reshape(g, p, t * gd)
    wo = jnp.concatenate([co.real, -co.imag], axis=1)
    lt = pw[:, t]
    c1 = jnp.concatenate([lt.real, lt.real], axis=-1)
    c2a = jnp.concatenate([-lt.imag, lt.imag], axis=-1)
    c2b = jnp.concatenate([lt.imag, -lt.imag], axis=-1)
    dtile = jnp.tile(d.astype(F32).reshape(g, 1, gd), (1, t, 1)).reshape(g, 1, t * gd)
    return (m.astype(BF16), wa.astype(BF16), wb.astype(BF16), wo.astype(BF16), c1, c2a, c2b, dtile)


def _out_proj_kernel(ya_ref, yr_ref, ys_ref, x_ref, gg_ref, wglu_ref, wo_ref, gf_ref, xo_ref, h_ref):
    da, dr = ya_ref.shape[1], yr_ref.shape[1]
    ys = ys_ref[...]
    ys = ys * _sigmoid(_dot(ys.astype(BF16), wglu_ref[...]))
    acc = x_ref[...]
    lo = 0
    for y in (ya_ref[...], yr_ref[...], ys):
        w = y.shape[1]
        acc = acc + _dot(_rms(y, gg_ref[:, lo:lo + w]).astype(BF16), wo_ref[lo:lo + w, :])
        lo += w
    xo_ref[...] = acc
    h_ref[...] = _rms(acc, gf_ref[...]).astype(BF16)


def _out_proj(ya, yr, ys, x2, gg, wglu_bf, wo_bf, gf):
    n, dm = x2.shape
    tm = min(ROW_TILE, n)
    row = lambda w: pl.BlockSpec((tm, w), lambda i: (i, 0))
    return pl.pallas_call(
        _out_proj_kernel,
        grid=(n // tm,),
        in_specs=[row(ya.shape[1]), row(yr.shape[1]), row(ys.shape[1]), row(dm), _const_spec((1, gg.shape[0])),
                  _const_spec(wglu_bf.shape), _const_spec(wo_bf.shape), _const_spec((1, dm))],
        out_specs=[row(dm), row(dm)],
        out_shape=[jax.ShapeDtypeStruct((n, dm), F32), jax.ShapeDtypeStruct((n, dm), BF16)],
        compiler_params=_cparams("parallel"),
        name="out_proj",
    )(ya, yr, ys, x2, gg.reshape(1, -1), wglu_bf, wo_bf, gf.reshape(1, dm))


def _swiglu_chunks(h, w1, w3, w2, chunks):
    acc = None
    for lo, width in chunks:
        a = _dot(h, w1[:, lo:lo + width])
        b = _dot(h, w3[:, lo:lo + width])
        t = (a * _sigmoid(a) * b).astype(BF16)
        y = _dot(t, w2[lo:lo + width, :])
        acc = y if acc is None else acc + y
    return acc


def _ffn_chunks(f):
    if f == sum(w for _, w in FFN_CHUNKS):
        return FFN_CHUNKS
    return ((0, f),)


def _dense_ffn_kernel(h_ref, x_ref, w1_ref, w3_ref, w2_ref, o_ref, *, chunks):
    o_ref[...] = x_ref[...] + _swiglu_chunks(h_ref[...], w1_ref, w3_ref, w2_ref, chunks)


def _dense_ffn(h, x2, w1, w3, w2):
    n, dm = x2.shape
    tm = min(ROW_TILE, n)
    row = pl.BlockSpec((tm, dm), lambda i: (i, 0))
    return pl.pallas_call(
        functools.partial(_dense_ffn_kernel, chunks=_ffn_chunks(w1.shape[1])),
        grid=(n // tm,),
        in_specs=[row, row, _const_spec(w1.shape), _const_spec(w3.shape), _const_spec(w2.shape)],
        out_specs=row,
        out_shape=jax.ShapeDtypeStruct((n, dm), F32),
        compiler_params=_cparams("parallel"),
        name="dense_ffn",
    )(h, x2, w1, w3, w2)


def _expert_ffn_kernel(brow_ref, bexp_ref, nblk_ref, x_ref, w1_ref, w3_ref, w2_ref, o_ref, *, chunks):
    @pl.when(pl.program_id(0) < nblk_ref[0])
    def _():
        o_ref[...] = _swiglu_chunks(x_ref[...], w1_ref.at[0], w3_ref.at[0], w2_ref.at[0], chunks).astype(o_ref.dtype)


def _expert_ffn(xb, w1, w3, w2, blk_row, blk_exp, nblk, max_blocks):
    rows, dm = xb.shape
    f = w1.shape[2]
    row = pl.BlockSpec((MOE_BLK, dm), lambda i, br, be, nb: (br[i], 0))
    wspec = lambda a: pl.BlockSpec((1,) + a.shape[1:], lambda i, br, be, nb: (be[i], 0, 0))
    return pl.pallas_call(
        functools.partial(_expert_ffn_kernel, chunks=_ffn_chunks(f)),
        grid_spec=pltpu.PrefetchScalarGridSpec(
            num_scalar_prefetch=3, grid=(max_blocks,),
            in_specs=[row, wspec(w1), wspec(w3), wspec(w2)], out_specs=row),
        out_shape=jax.ShapeDtypeStruct((rows, dm), BF16),
        compiler_params=_cparams("arbitrary"),
        name="expert_ffn",
    )(blk_row, blk_exp, nblk, xb, w1, w3, w2)


def _route_kernel(x_ref, gf_ref, rt_ref, pos_ref, rank_ref, gate_ref, cnt_ref):
    tm = x_ref.shape[0]
    ne = rt_ref.shape[0]
    hi = lax.Precision.HIGHEST
    h = _rms(x_ref[...], gf_ref[...])
    logits = lax.dot_general(rt_ref[...], h, (((1,), (1,)), ((), ())), precision=hi,
                             preferred_element_type=F32)
    eidx = lax.broadcasted_iota(I32, (ne, tm), 0)
    m1 = jnp.max(logits, axis=0, keepdims=True)
    i1 = jnp.min(jnp.where(logits == m1, eidx, ne), axis=0, keepdims=True)
    sel1 = eidx == i1
    rest = jnp.where(sel1, -jnp.inf, logits)
    m2 = jnp.max(rest, axis=0, keepdims=True)
    i2 = jnp.min(jnp.where(rest == m2, eidx, ne), axis=0, keepdims=True)
    sel2 = eidx == i2
    e2 = jnp.exp(m2 - m1)
    den = 1.0 + e2
    routed = sel1 | sel2
    gate = jnp.where(sel1, 1.0 / den, jnp.where(sel2, e2 / den, 0.0))
    rf = jnp.where(routed, 1.0, 0.0)
    cnt = jnp.sum(rf, axis=1, keepdims=True)
    before = (lax.broadcasted_iota(I32, (tm, tm), 0) < lax.broadcasted_iota(I32, (tm, tm), 1))
    rank = _dot(rf.astype(BF16), jnp.where(before, 1.0, 0.0).astype(BF16))
    seg = jnp.floor((cnt + (BF16_ROWS - 1)) * (1.0 / BF16_ROWS)) * BF16_ROWS
    ecol = lax.broadcasted_iota(I32, (ne, 1), 0)
    off = jnp.zeros((ne, 1), F32)
    for j in range(ne - 1):
        off = off + jnp.where(ecol > j, seg[j:j + 1, :], 0.0)
    pos_ref[...] = jnp.where(routed, off + rank, -1.0).astype(I32)
    cnt_ref[0] = jnp.broadcast_to(cnt, (ne, LANES)).astype(I32)
    eye = jnp.where(lax.broadcasted_iota(I32, (tm, tm), 0) == lax.broadcasted_iota(I32, (tm, tm), 1), 1.0, 0.0)
    tr = lambda a: lax.dot_general(eye, a, (((1,), (1,)), ((), ())), precision=hi, preferred_element_type=F32)
    rank_ref[...] = tr(jnp.where(routed, rank, -1.0)).astype(I32)
    gate_ref[...] = tr(gate)


def _route(x2, gf, router_t):
    n, dm = x2.shape
    ne = router_t.shape[0]
    tm = MOE_TILE
    nt = n // tm
    return pl.pallas_call(
        _route_kernel,
        grid=(nt,),
        in_specs=[pl.BlockSpec((tm, dm), lambda i: (i, 0)), _const_spec((1, dm)), _const_spec(router_t.shape)],
        out_specs=[pl.BlockSpec((ne, tm), lambda i: (0, i)), pl.BlockSpec((tm, ne), lambda i: (i, 0)),
                   pl.BlockSpec((tm, ne), lambda i: (i, 0)), pl.BlockSpec((1, ne, LANES), lambda i: (i, 0, 0))],
        out_shape=[jax.ShapeDtypeStruct((ne, n), I32), jax.ShapeDtypeStruct((n, ne), I32),
                   jax.ShapeDtypeStruct((n, ne), F32), jax.ShapeDtypeStruct((nt, ne, LANES), I32)],
        compiler_params=_cparams("parallel"),
        name="moe_route",
    )(x2, gf.reshape(1, dm), router_t)


def _dispatch_copies(base_ref, off_ref, res, xb_ref, sems, tile, slot, ne, tm):
    out = []
    for e in range(ne):
        src = pl.multiple_of(off_ref[tile * ne + e], BF16_ROWS)
        dst = pl.multiple_of(base_ref[tile * ne + e], BF16_ROWS)
        out.append(pltpu.make_async_copy(res.at[slot, pl.ds(src, tm)], xb_ref.at[pl.ds(dst, tm)], sems.at[slot, e]))
    return out


def _tail_copies(end_ref, res, xb_ref, sems, mrows, ne):
    return [pltpu.make_async_copy(res.at[0, pl.ds(mrows, MOE_BLK)],
                                  xb_ref.at[pl.ds(pl.multiple_of(end_ref[e], BF16_ROWS), MOE_BLK)], sems.at[0, e])
            for e in range(ne)]


def _dispatch_kernel(base_ref, off_ref, end_ref, pos_ref, h_ref, xb_ref, res, sems):
    i = pl.program_id(0)
    nt = pl.num_programs(0)
    ne, tm = pos_ref.shape
    mrows = 2 * tm + ne * BF16_ROWS
    slot = i % 2

    @pl.when(i == 0)
    def _():
        res[...] = jnp.zeros_like(res)

    row = lax.broadcasted_iota(I32, (mrows, tm), 0)
    hit = row == pos_ref[0:1, :]
    for e in range(1, ne):
        hit = hit | (row == pos_ref[e:e + 1, :])
    perm = jnp.where(hit, 1.0, 0.0).astype(BF16)
    res[slot, 0:mrows, :] = _dot(perm, h_ref[...]).astype(BF16)

    @pl.when(i > 0)
    def _():
        for cp in _dispatch_copies(base_ref, off_ref, res, xb_ref, sems, i - 1, 1 - slot, ne, tm):
            cp.wait()

    for cp in _dispatch_copies(base_ref, off_ref, res, xb_ref, sems, i, slot, ne, tm):
        cp.start()

    @pl.when(i == nt - 1)
    def _():
        for cp in _dispatch_copies(base_ref, off_ref, res, xb_ref, sems, i, slot, ne, tm):
            cp.wait()
        for cp in _tail_copies(end_ref, res, xb_ref, sems, mrows, ne):
            cp.start()
        for cp in _tail_copies(end_ref, res, xb_ref, sems, mrows, ne):
            cp.wait()


def _dispatch(h, pos_t, base, off, end, rows_total):
    n, dm = h.shape
    ne = pos_t.shape[0]
    tm = MOE_TILE
    nt = n // tm
    mrows = 2 * tm + ne * BF16_ROWS
    return pl.pallas_call(
        _dispatch_kernel,
        grid_spec=pltpu.PrefetchScalarGridSpec(
            num_scalar_prefetch=3, grid=(nt,),
            in_specs=[pl.BlockSpec((ne, tm), lambda i, *_: (0, i)), pl.BlockSpec((tm, dm), lambda i, *_: (i, 0))],
            out_specs=pl.BlockSpec(memory_space=pl.ANY),
            scratch_shapes=[pltpu.VMEM((2, mrows + tm, dm), BF16), pltpu.SemaphoreType.DMA((2, ne))]),
        out_shape=jax.ShapeDtypeStruct((rows_total, dm), BF16),
        compiler_params=_cparams("arbitrary"),
        name="moe_dispatch",
    )(base, off, end, pos_t, h)


def _combine_fetch(start_ref, live_ref, yb_ref, win, sems, tile, slot, ne, tm, wait):
    for e in range(ne):
        src = pl.multiple_of(start_ref[tile * ne + e], BF16_ROWS)
        cp = pltpu.make_async_copy(yb_ref.at[pl.ds(src, tm)], win.at[slot, e], sems.at[slot, e])

        @pl.when(live_ref[tile * ne + e] > 0)
        def _():
            cp.wait() if wait else cp.start()


def _combine_kernel(start_ref, shift_ref, live_ref, rank_ref, gate_ref, x_ref, gn_ref, yb_ref, o_ref, win, sems):
    i = pl.program_id(0)
    nt = pl.num_programs(0)
    tm, ne = rank_ref.shape
    slot = i % 2

    @pl.when(i == 0)
    def _():
        win[...] = jnp.zeros_like(win)
        _combine_fetch(start_ref, live_ref, yb_ref, win, sems, i, slot, ne, tm, wait=False)

    @pl.when(i + 1 < nt)
    def _():
        _combine_fetch(start_ref, live_ref, yb_ref, win, sems, i + 1, 1 - slot, ne, tm, wait=False)

    _combine_fetch(start_ref, live_ref, yb_ref, win, sems, i, slot, ne, tm, wait=True)

    col = lax.broadcasted_iota(I32, (tm, tm), 1)
    rank = rank_ref[...]
    gate = gate_ref[...]
    y = jnp.zeros(o_ref.shape, F32)
    for e in range(ne):
        want = jnp.where(rank[:, e:e + 1] >= 0, rank[:, e:e + 1] + shift_ref[i * ne + e], -1)
        sel = jnp.where(col == want, 1.0, 0.0).astype(BF16)
        y = y + gate[:, e:e + 1] * _dot(sel, win[slot, e])
    o_ref[...] = _rms(x_ref[...] + y, gn_ref[...])


def _combine(yb, rank, gate, x2, gn, start, shift, live):
    n, dm = x2.shape
    ne = rank.shape[1]
    tm = MOE_TILE
    nt = n // tm
    tok = lambda w: pl.BlockSpec((tm, w), lambda i, *_: (i, 0))
    return pl.pallas_call(
        _combine_kernel,
        grid_spec=pltpu.PrefetchScalarGridSpec(
            num_scalar_prefetch=3, grid=(nt,),
            in_specs=[tok(ne), tok(ne), tok(dm), pl.BlockSpec((1, dm), lambda i, *_: (0, 0)),
                      pl.BlockSpec(memory_space=pl.ANY)],
            out_specs=tok(dm),
            scratch_shapes=[pltpu.VMEM((2, ne, tm, dm), BF16), pltpu.SemaphoreType.DMA((2, ne))]),
        out_shape=jax.ShapeDtypeStruct((n, dm), F32),
        compiler_params=_cparams("arbitrary"),
        name="moe_combine",
    )(start, shift, live, rank, gate, x2, gn.reshape(1, dm), yb)


def _moe_layout(cnt, n):
    nt, ne = cnt.shape
    tm, blk = MOE_TILE, MOE_BLK
    seg = (cnt + BF16_ROWS - 1) // BF16_ROWS * BF16_ROWS
    cap = -(-(n + nt * BF16_ROWS + tm) // blk) * blk
    within = jnp.cumsum(seg, axis=0) - seg
    base = (within + jnp.arange(ne, dtype=I32)[None, :] * cap).astype(I32)
    off = (jnp.cumsum(seg, axis=1) - seg).astype(I32)
    rows_e = jnp.sum(seg, axis=0)
    nblk_e = (rows_e + blk - 1) // blk
    region = jnp.arange(ne, dtype=I32) * cap
    end = (region + rows_e).astype(I32)
    start = jnp.minimum(base, jnp.maximum(region + nblk_e * blk - tm, region)[None, :]).astype(I32)
    shift = (base - start).astype(I32)
    live = (cnt > 0).astype(I32)
    ends = jnp.cumsum(nblk_e)
    max_blocks = (2 * n + nt * ne * BF16_ROWS) // blk + ne
    bid = jnp.arange(max_blocks + 1, dtype=I32)
    bexp = jnp.minimum(jnp.sum((bid[:, None] >= ends[None, :]).astype(I32), axis=1), ne - 1)
    brow = bexp * (cap // blk) + bid - (ends - nblk_e)[bexp]
    nblk = ends[-1:].astype(I32)
    last = jnp.maximum(nblk[0] - 1, 0)
    used = bid < nblk[0]
    bexp = jnp.where(used, bexp, bexp[last]).astype(I32)
    brow = jnp.where(used, brow, brow[last]).astype(I32)
    flat = lambda a: a.reshape(-1)
    return dict(base=flat(base), off=flat(off), end=end, start=flat(start), shift=flat(shift), live=flat(live),
                brow=brow, bexp=bexp, nblk=nblk, rows_total=ne * cap, max_blocks=max_blocks)


def _moe(x2, h, gf, gn, router, w1, w3, w2):
    n, dm = x2.shape
    pos_t, rank, gate, cnt = _route(x2, gf, router.astype(F32).T)
    lay = _moe_layout(cnt[:, :, 0], n)
    xb = _dispatch(h, pos_t, lay['base'], lay['off'], lay['end'], lay['rows_total'])
    yb = _expert_ffn(xb, w1.astype(BF16), w3.astype(BF16), w2.astype(BF16), lay['brow'], lay['bexp'], lay['nblk'],
                     lay['max_blocks'])
    return _combine(yb, rank, gate, x2, gn, lay['start'], lay['shift'], lay['live'])


def kernel(x, norm_mix_g, w_in, attn_rel_bias, rg_conv_w, rg_conv_b, rg_wx, rg_bx, rg_wa, rg_ba, rg_lambda, s5_a_re, s5_a_im, s5_log_dt, s5_b_re, s5_b_im, s5_c_re, s5_c_im, s5_d, s5_w_glu, g_group, w_out, norm_ffn_g, ffn_w1, ffn_w3, ffn_w2, moe_router, moe_w1, moe_w3, moe_w2, final_norm_g):
    bsz, l, dm = x.shape
    depth = w_in.shape[0]
    assert depth == 2 and l % ATTN_TQ == 0 and ATTN_TQ == N_PREV * CHUNK, "dense layer, then the MoE layer"
    d_rg = rg_conv_w.shape[2]
    d_s5 = s5_w_glu.shape[1]
    d_attn = (w_in.shape[2] - 2 * d_rg - d_s5) // 3
    x2 = x.reshape(bsz * l, dm)
    for layer in range(depth):
        q, k, v, xr, gate, us = _in_proj(x2, norm_mix_g[layer], w_in[layer].astype(BF16), d_attn, d_rg, d_s5)
        y_attn = _attention(q, k, v, _attn_bias(attn_rel_bias[layer]), bsz, l)
        w_gates = jnp.concatenate([_block_diag(rg_wx[layer]), _block_diag(rg_wa[layer])], axis=1).astype(BF16)
        b_gates = jnp.concatenate([rg_bx[layer], rg_ba[layer]]).astype(F32)
        y_rg = _rglru(xr, gate, rg_conv_w[layer].astype(F32), rg_conv_b[layer].astype(F32), w_gates, b_gates,
                      jax.nn.softplus(-rg_lambda[layer].astype(F32)), bsz, l)
        prm = _s5_params(s5_a_re[layer], s5_a_im[layer], s5_log_dt[layer], s5_b_re[layer], s5_b_im[layer],
                         s5_c_re[layer], s5_c_im[layer], s5_d[layer])
        y_s5 = _s5(us, prm, bsz, l)
        x2, h = _out_proj(y_attn, y_rg, y_s5, x2, g_group[layer].astype(F32), s5_w_glu[layer].astype(BF16),
                          w_out[layer].astype(BF16), norm_ffn_g[layer].astype(F32))
        if layer == 0:
            x2 = _dense_ffn(h, x2, ffn_w1[0].astype(BF16), ffn_w3[0].astype(BF16), ffn_w2[0].astype(BF16))
        else:
            x2 = _moe(x2, h, norm_ffn_g[layer].astype(F32), final_norm_g.astype(F32), moe_router[0],
                      moe_w1[0], moe_w3[0], moe_w2[0])
    return x2.reshape(bsz, l, dm)
```

```python
import functools
import math

import jax
import jax.numpy as jnp
from jax import lax
from jax.experimental import pallas as pl
from jax.experimental.pallas import tpu as pltpu

F32 = jnp.float32
BF16 = jnp.bfloat16
I32 = jnp.int32

EPS = 1e-6
CHUNK = 64
N_PREV = 8
BAND = (N_PREV + 1) * CHUNK
REL_CLIP = 128
HEAD_DIM = 64
RG_C = 8.0
RG_CONV_WIDTH = 4
S5_GROUP_DIM = 16
S5_T = 16
TOP_K = 2
NEG_BIG = -1e30

LANES = 128
BF16_ROWS = 16
VMEM_LIMIT = 52 * 1024 * 1024

ROW_TILE = 512
ATTN_TQ = 512
RG_TL = 1024
RG_SEG = 8
MOE_TILE = 256
MOE_BLK = 256
ROUTE_SPAN = 2048
SEG_PIECES = tuple(BF16_ROWS << b for b in range((MOE_TILE // BF16_ROWS).bit_length() - 1, -1, -1))
FFN_CHUNKS = ((0, 512), (512, 512), (1024, 512), (1536, 512), (2048, 512), (2560, 256))


def _cparams(*sem):
    return pltpu.CompilerParams(dimension_semantics=sem, vmem_limit_bytes=VMEM_LIMIT)


def _const_spec(shape):
    nd = len(shape)
    return pl.BlockSpec(shape, lambda *_: (0,) * nd, pipeline_mode=pl.Buffered(1))


def _rms(xf, g):
    var = jnp.mean(xf * xf, axis=-1, keepdims=True)
    return xf * lax.rsqrt(var + EPS) * g


def _sigmoid(x):
    return 1.0 / (1.0 + jnp.exp(-x))


def _gelu(x):
    c = math.sqrt(2.0 / math.pi)
    return 0.5 * x * (1.0 + jnp.tanh(c * (x + 0.044715 * (x * x * x))))


def _dot(a, b):
    return jnp.dot(a, b, preferred_element_type=F32)


def _in_proj_kernel(x_ref, g_ref, w_ref, q_ref, k_ref, v_ref, xr_ref, gt_ref, us_ref, *, d_attn, d_rg):
    u = _rms(x_ref[...], g_ref[...]).astype(BF16)
    lo = 0
    for ref, width in ((q_ref, d_attn), (k_ref, d_attn), (v_ref, d_attn),
                       (xr_ref, d_rg), (gt_ref, d_rg), (us_ref, w_ref.shape[1] - 3 * d_attn - 2 * d_rg)):
        ref[...] = _dot(u, w_ref[:, lo:lo + width]).astype(ref.dtype)
        lo += width


def _in_proj(x2, g, w_bf, d_attn, d_rg, d_s5):
    n, dm = x2.shape
    tm = min(ROW_TILE, n)
    row = lambda w: pl.BlockSpec((tm, w), lambda i: (i, 0))
    return pl.pallas_call(
        functools.partial(_in_proj_kernel, d_attn=d_attn, d_rg=d_rg),
        grid=(n // tm,),
        in_specs=[row(dm), _const_spec((1, dm)), _const_spec(w_bf.shape)],
        out_specs=[row(d_attn), row(d_attn), row(d_attn), row(d_rg), row(d_rg), row(d_s5)],
        out_shape=[jax.ShapeDtypeStruct((n, d_attn), BF16)] * 3
        + [jax.ShapeDtypeStruct((n, d_rg), F32)] * 2 + [jax.ShapeDtypeStruct((n, d_s5), F32)],
        compiler_params=_cparams("parallel"),
        name="in_proj",
    )(x2, g.reshape(1, dm), w_bf)


def _attn_kernel(q_ref, kp_ref, kc_ref, vp_ref, vc_ref, bias_ref, o_ref, kz, vz, *, n_pairs):
    qi = pl.program_id(1)
    tq = q_ref.shape[0]
    kz[0:tq, :] = kp_ref[...]
    kz[tq:2 * tq, :] = kc_ref[...]
    vz[0:tq, :] = vp_ref[...]
    vz[tq:2 * tq, :] = vc_ref[...]
    lane = lax.broadcasted_iota(I32, (CHUNK, LANES), 1)
    first = lane < HEAD_DIM
    kpos = lax.broadcasted_iota(I32, (1, BAND), 1)
    pairs = [slice(hp * LANES, (hp + 1) * LANES) for hp in range(n_pairs)]

    def chunk_body(c, carry, *, masked):
        r0 = pl.multiple_of(c * CHUNK, CHUNK)
        scores = []
        for hp, cols in enumerate(pairs):
            q2 = q_ref[pl.ds(r0, CHUNK), cols] * jnp.asarray(HEAD_DIM ** -0.5, BF16)
            zero = jnp.zeros_like(q2)
            qq = jnp.concatenate([jnp.where(first, q2, zero), jnp.where(first, zero, q2)], axis=0)
            s = lax.dot_general(qq, kz[pl.ds(r0, BAND), cols], (((1,), (1,)), ((), ())), preferred_element_type=F32)
            scores.append(s + bias_ref[hp])
        if masked:
            neg = jnp.where(kpos >= N_PREV * CHUNK - c * CHUNK, 0.0, NEG_BIG)
            scores = [s + neg for s in scores]
        probs, sums = [], []
        for s in scores:
            p = jnp.exp(s - jnp.max(s, axis=-1, keepdims=True))
            sums.append(jnp.sum(p, axis=-1, keepdims=True))
            probs.append(p.astype(BF16))
        for cols, p, l in zip(pairs, probs, sums):
            o2 = _dot(p, vz[pl.ds(r0, BAND), cols]) / l
            o_ref[pl.ds(r0, CHUNK), cols] = jnp.where(first, o2[0:CHUNK], o2[CHUNK:2 * CHUNK])
        return carry

    @pl.when(qi == 0)
    def _():
        lax.fori_loop(0, tq // CHUNK, functools.partial(chunk_body, masked=True), 0)

    @pl.when(qi != 0)
    def _():
        lax.fori_loop(0, tq // CHUNK, functools.partial(chunk_body, masked=False), 0)


def _attention(q, k, v, bias2, bsz, l):
    n, da = q.shape
    tq = ATTN_TQ
    nt = l // tq
    n_pairs = da // LANES
    cur = pl.BlockSpec((tq, da), lambda b, i: (b * nt + i, 0))
    prev = pl.BlockSpec((tq, da), lambda b, i: (b * nt + jnp.maximum(i - 1, 0), 0))
    return pl.pallas_call(
        functools.partial(_attn_kernel, n_pairs=n_pairs),
        grid=(bsz, nt),
        in_specs=[cur, prev, cur, prev, cur, _const_spec(bias2.shape)],
        out_specs=cur,
        out_shape=jax.ShapeDtypeStruct((n, da), F32),
        scratch_shapes=[pltpu.VMEM((2 * tq, da), BF16), pltpu.VMEM((2 * tq, da), BF16)],
        compiler_params=_cparams("parallel", "parallel"),
        name="chunk_attn",
    )(q, k, k, v, v, bias2)


def _attn_bias(rel_bias):
    h = rel_bias.shape[0]
    tab = rel_bias.astype(F32)
    n_far = N_PREV * CHUNK - REL_CLIP + CHUNK
    lo = 2 * REL_CLIP - (BAND + CHUNK - 1 - n_far)
    ext = jnp.concatenate([jnp.broadcast_to(tab[:, 2 * REL_CLIP:], (h, n_far)), tab[:, lo:2 * REL_CLIP][:, ::-1]], axis=1)
    bias = jnp.stack([ext[:, CHUNK - 1 - i:CHUNK - 1 - i + BAND] for i in range(CHUNK)], axis=1)
    return bias.reshape(h // 2, 2 * CHUNK, BAND)


def _rglru_kernel(x_ref, gt_ref, cw_ref, cb_ref, w_ref, b_ref, sp_ref, o_ref, xpad, a_s, b_s, hcar):
    t = pl.program_id(1)
    tl, c = x_ref.shape
    seg = tl // RG_SEG
    front = 8

    @pl.when(t == 0)
    def _():
        xpad[0:front, :] = jnp.zeros((front, c), F32)
        hcar[...] = jnp.zeros_like(hcar)

    xpad[front:front + tl, :] = x_ref[...]
    xc = cb_ref[...] + sum(
        cw_ref[j:j + 1, :] * xpad[front - (RG_CONV_WIDTH - 1) + j:front - (RG_CONV_WIDTH - 1) + j + tl, :]
        for j in range(RG_CONV_WIDTH))
    xpad[0:front, :] = xpad[tl:tl + front, :]
    pre = _dot(xc.astype(BF16), w_ref[...]) + b_ref[...]
    gx = _sigmoid(pre[:, 0:c])
    ga = _sigmoid(pre[:, c:2 * c])
    log_a = -RG_C * ga * sp_ref[...]
    a = jnp.exp(log_a)
    mult = jnp.sqrt(-jnp.tanh(log_a) * (a * a + 1.0))
    b = mult * gx * xc
    nh = c // LANES
    for j in range(nh):
        a_s[j] = a[:, j * LANES:(j + 1) * LANES]
        b_s[j] = b[:, j * LANES:(j + 1) * LANES]

    def step(i, carry):
        rows = pl.ds(i, RG_SEG, stride=seg)
        out = []
        for j in range(nh):
            h, p = carry[j]
            av = a_s[j, rows, :]
            h = av * h + b_s[j, rows, :]
            p = av * p
            b_s[j, rows, :] = h
            a_s[j, rows, :] = p
            out.append((h, p))
        return tuple(out)

    init = tuple((jnp.zeros((RG_SEG, LANES), F32), jnp.ones((RG_SEG, LANES), F32)) for _ in range(nh))
    ends = lax.fori_loop(0, seg, step, init, unroll=8)
    for j in range(nh):
        cols = slice(j * LANES, (j + 1) * LANES)
        h_end, p_end = ends[j]
        carry = hcar[:, cols]
        for s in range(RG_SEG):
            rows = slice(s * seg, (s + 1) * seg)
            h = b_s[j, rows, :] + a_s[j, rows, :] * carry
            o_ref[rows, cols] = h * _gelu(gt_ref[rows, cols])
            carry = h_end[s:s + 1, :] + p_end[s:s + 1, :] * carry
        hcar[:, cols] = carry


def _rglru(xr, gate, conv_w, conv_b, w_bd, b_cat, sp, bsz, l):
    n, c = xr.shape
    tl = min(RG_TL, l)
    nt = l // tl
    row = pl.BlockSpec((tl, c), lambda b, t: (b * nt + t, 0))
    return pl.pallas_call(
        _rglru_kernel,
        grid=(bsz, nt),
        in_specs=[row, row, _const_spec(conv_w.shape), _const_spec((1, c)), _const_spec(w_bd.shape),
                  _const_spec((1, 2 * c)), _const_spec((1, c))],
        out_specs=row,
        out_shape=jax.ShapeDtypeStruct((n, c), F32),
        scratch_shapes=[pltpu.VMEM((tl + 8, c), F32), pltpu.VMEM((c // LANES, tl, LANES), F32),
                        pltpu.VMEM((c // LANES, tl, LANES), F32), pltpu.VMEM((1, c), F32)],
        compiler_params=_cparams("parallel", "arbitrary"),
        name="rglru",
    )(xr, gate, conv_w, conv_b.reshape(1, c), w_bd, b_cat.reshape(1, 2 * c), sp.reshape(1, c))


def _block_diag(w):
    nb, d, _ = w.shape
    eye = jnp.eye(nb, dtype=w.dtype)
    return (eye[:, None, :, None] * w[:, :, None, :]).reshape(nb * d, nb * d)


def _s5_kernel(u_ref, m_ref, wa_ref, wb_ref, wo_ref, c1_ref, c2a_ref, c2b_ref, d_ref, y_ref, ps, ug, xa_s, xb_s):
    t, gd = S5_T, S5_GROUP_DIM
    ng, nk, _ = ug.shape
    nh = ps.shape[0]
    per_half = LANES // gd
    rows8 = 8

    for h in range(nh):
        ps[h] = u_ref[:, h * LANES:(h + 1) * LANES]

    def to_groups(r, carry):
        base = pl.multiple_of(r * rows8 * t, rows8 * t)
        steps = [[ps[h, pl.ds(base + s, rows8, stride=t), :] for h in range(nh)] for s in range(t)]
        for g in range(ng):
            lo = (g % per_half) * gd
            ug[g, pl.ds(pl.multiple_of(r * rows8, rows8), rows8), :] = jnp.concatenate(
                [steps[s][g // per_half][:, lo:lo + gd] for s in range(t)], axis=1)
        return carry

    lax.fori_loop(0, nk // rows8, to_groups, 0)

    def project(g, carry):
        u = ug[g]
        ub = u.astype(BF16)
        rows = pl.ds(pl.multiple_of(g * nk, nk), nk)
        xa_s[rows, :] = _dot(ub, wa_ref[g])
        xb_s[rows, :] = _dot(ub, wb_ref[g])
        ug[g] = _dot(ub, m_ref[g]) + d_ref[g] * u
        return carry

    lax.fori_loop(0, ng, project, 0)

    c1, c2a, c2b = c1_ref[...], c2a_ref[...], c2b_ref[...]

    def step(k, carry):
        xa, xb = carry
        rows = pl.ds(k, ng, stride=nk)
        ia = xa_s[rows, :]
        ib = xb_s[rows, :]
        xa_s[rows, :] = xa
        return c1 * xa + c2a * xb + ia, c1 * xb + c2b * xa + ib

    z = jnp.zeros((ng, xa_s.shape[1]), F32)
    lax.fori_loop(0, nk, step, (z, z))

    def respond(g, carry):
        rows = pl.ds(pl.multiple_of(g * nk, nk), nk)
        ug[g] = _gelu(ug[g] + _dot(xa_s[rows, :].astype(BF16), wo_ref[g]))
        return carry

    lax.fori_loop(0, ng, respond, 0)

    def from_groups(r, carry):
        base = pl.multiple_of(r * rows8 * t, rows8 * t)
        rows = pl.ds(pl.multiple_of(r * rows8, rows8), rows8)
        grp = [ug[g, rows, :] for g in range(ng)]
        for s in range(t):
            lo = (s % per_half) * gd + (s // per_half) * LANES
            for h in range(nh):
                ps[h, pl.ds(base + s, rows8, stride=t), :] = jnp.concatenate(
                    [grp[g][:, lo:lo + gd] for g in range(h * per_half, (h + 1) * per_half)], axis=1)
        return carry

    lax.fori_loop(0, nk // rows8, from_groups, 0)
    for h in range(nh):
        y_ref[:, h * LANES:(h + 1) * LANES] = ps[h]


def _s5(us, prm, bsz, l):
    n, w = us.shape
    ng = w // S5_GROUP_DIM
    nk = l // S5_T
    p2 = prm[1].shape[2]
    row = pl.BlockSpec((l, w), lambda b: (b, 0))
    return pl.pallas_call(
        _s5_kernel,
        grid=(bsz,),
        in_specs=[row] + [_const_spec(a.shape) for a in prm],
        out_specs=row,
        out_shape=jax.ShapeDtypeStruct((n, w), F32),
        scratch_shapes=[pltpu.VMEM((w // LANES, l, LANES), F32), pltpu.VMEM((ng, nk, S5_T * S5_GROUP_DIM), F32),
                        pltpu.VMEM((ng * nk, p2), F32), pltpu.VMEM((ng * nk, p2), F32)],
        compiler_params=_cparams("parallel"),
        name="s5",
    )(us, *prm)


def _s5_params(a_re, a_im, log_dt, b_re, b_im, c_re, c_im, d):
    t = S5_T
    g, p = a_re.shape
    gd = S5_GROUP_DIM
    hi = lax.Precision.HIGHEST
    a = lax.complex(a_re.astype(F32), a_im.astype(F32))
    dt = jnp.exp(log_dt.astype(F32))[:, None]
    a_bar = jnp.exp(a * dt)
    bm = lax.complex(b_re.astype(F32), b_im.astype(F32))
    cm = lax.complex(c_re.astype(F32), c_im.astype(F32))
    b_bar = ((a_bar - 1.0) / a)[..., None] * bm
    steps = jnp.arange(t + 1, dtype=F32)
    pw = jnp.exp((a * dt)[:, None, :] * steps[None, :, None])

    def cdot(spec, x, y):
        e = lambda u, v: jnp.einsum(spec, u, v, precision=hi)
        return (e(x.real, y.real) - e(x.imag, y.imag)), (e(x.real, y.imag) + e(x.imag, y.real))

    cp = cm[:, None, :, :] * pw[:, :, None, :]
    kr, _ = cdot('glop,gpc->gloc', cp[:, :t], b_bar)
    krt = kr.transpose(0, 3, 1, 2)
    m = jnp.stack([jnp.pad(krt[:, :, :t - s, :], ((0, 0), (0, 0), (s, 0), (0, 0))) for s in range(t)], axis=1)
    m = m.reshape(g, t * gd, t * gd)
    win = pw[:, :t][:, ::-1][:, :, :, None] * b_bar[:, None]
    win = win.transpose(0, 1, 3, 2).reshape(g, t * gd, p)
    wa = jnp.concatenate([win.real, win.imag], axis=-1)
    wb = jnp.concatenate([win.imag, win.real], axis=-1)
    co = cp[:, 1:].transpose(0, 3, 1, 2).reshape(g, p, t * gd)
    wo = jnp.concatenate([co.real, -co.imag], axis=1)
    lt = pw[:, t]
    c1 = jnp.concatenate([lt.real, lt.real], axis=-1)
    c2a = jnp.concatenate([-lt.imag, lt.imag], axis=-1)
    c2b = jnp.concatenate([lt.imag, -lt.imag], axis=-1)
    dtile = jnp.tile(d.astype(F32).reshape(g, 1, gd), (1, t, 1)).reshape(g, 1, t * gd)
    return (m.astype(BF16), wa.astype(BF16), wb.astype(BF16), wo.astype(BF16), c1, c2a, c2b, dtile)


def _out_proj_kernel(ya_ref, yr_ref, ys_ref, x_ref, gg_ref, wglu_ref, wo_ref, gf_ref, *rest, with_router):
    if with_router:
        rt_ref, xo_ref, h_ref, lg_ref = rest
    else:
        xo_ref, h_ref = rest
    ys = ys_ref[...]
    ys = ys * _sigmoid(_dot(ys.astype(BF16), wglu_ref[...]))
    acc = x_ref[...]
    lo = 0
    for y in (ya_ref[...], yr_ref[...], ys):
        w = y.shape[1]
        acc = acc + _dot(_rms(y, gg_ref[:, lo:lo + w]).astype(BF16), wo_ref[lo:lo + w, :])
        lo += w
    xo_ref[...] = acc
    h = _rms(acc, gf_ref[...])
    h_ref[...] = h.astype(BF16)
    if with_router:
        lg_ref[...] = jnp.dot(h, rt_ref[...], precision=lax.Precision.HIGHEST, preferred_element_type=F32)


def _out_proj(ya, yr, ys, x2, gg, wglu_bf, wo_bf, gf, router=None):
    n, dm = x2.shape
    tm = min(ROW_TILE, n)
    row = lambda w: pl.BlockSpec((tm, w), lambda i: (i, 0))
    ins = [ya, yr, ys, x2, gg.reshape(1, -1), wglu_bf, wo_bf, gf.reshape(1, dm)]
    in_specs = [row(ya.shape[1]), row(yr.shape[1]), row(ys.shape[1]), row(dm), _const_spec((1, gg.shape[0])),
                _const_spec(wglu_bf.shape), _const_spec(wo_bf.shape), _const_spec((1, dm))]
    out_specs = [row(dm), row(dm)]
    out_shape = [jax.ShapeDtypeStruct((n, dm), F32), jax.ShapeDtypeStruct((n, dm), BF16)]
    if router is not None:
        ne = router.shape[1]
        ins.append(jnp.pad(router.astype(F32), ((0, 0), (0, LANES - ne))))
        in_specs.append(_const_spec((dm, LANES)))
        out_specs.append(row(LANES))
        out_shape.append(jax.ShapeDtypeStruct((n, LANES), F32))
    return pl.pallas_call(
        functools.partial(_out_proj_kernel, with_router=router is not None),
        grid=(n // tm,),
        in_specs=in_specs, out_specs=out_specs, out_shape=out_shape,
        compiler_params=_cparams("parallel"),
        name="out_proj",
    )(*ins)


def _swiglu_chunks(h, w1, w3, w2, chunks):
    acc = None
    for lo, width in chunks:
        a = _dot(h, w1[:, lo:lo + width])
        b = _dot(h, w3[:, lo:lo + width])
        t = (a * _sigmoid(a) * b).astype(BF16)
        y = _dot(t, w2[lo:lo + width, :])
        acc = y if acc is None else acc + y
    return acc


def _ffn_chunks(f):
    if f == sum(w for _, w in FFN_CHUNKS):
        return FFN_CHUNKS
    return ((0, f),)


def _dense_ffn_kernel(h_ref, x_ref, w1_ref, w3_ref, w2_ref, o_ref, *, chunks):
    o_ref[...] = x_ref[...] + _swiglu_chunks(h_ref[...], w1_ref, w3_ref, w2_ref, chunks)


def _dense_ffn(h, x2, w1, w3, w2):
    n, dm = x2.shape
    tm = min(ROW_TILE, n)
    row = pl.BlockSpec((tm, dm), lambda i: (i, 0))
    return pl.pallas_call(
        functools.partial(_dense_ffn_kernel, chunks=_ffn_chunks(w1.shape[1])),
        grid=(n // tm,),
        in_specs=[row, row, _const_spec(w1.shape), _const_spec(w3.shape), _const_spec(w2.shape)],
        out_specs=row,
        out_shape=jax.ShapeDtypeStruct((n, dm), F32),
        compiler_params=_cparams("parallel"),
        name="dense_ffn",
    )(h, x2, w1, w3, w2)


def _expert_ffn_kernel(brow_ref, bexp_ref, nblk_ref, x_ref, w1_ref, w3_ref, w2_ref, o_ref, *, chunks):
    @pl.when(pl.program_id(0) < nblk_ref[0])
    def _():
        o_ref[...] = _swiglu_chunks(x_ref[...], w1_ref.at[0], w3_ref.at[0], w2_ref.at[0], chunks).astype(o_ref.dtype)


def _expert_ffn(xb, w1, w3, w2, blk_row, blk_exp, nblk, max_blocks):
    rows, dm = xb.shape
    f = w1.shape[2]
    row = pl.BlockSpec((MOE_BLK, dm), lambda i, br, be, nb: (br[i], 0))
    wspec = lambda a: pl.BlockSpec((1,) + a.shape[1:], lambda i, br, be, nb: (be[i], 0, 0))
    return pl.pallas_call(
        functools.partial(_expert_ffn_kernel, chunks=_ffn_chunks(f)),
        grid_spec=pltpu.PrefetchScalarGridSpec(
            num_scalar_prefetch=3, grid=(max_blocks,),
            in_specs=[row, wspec(w1), wspec(w3), wspec(w2)], out_specs=row),
        out_shape=jax.ShapeDtypeStruct((rows, dm), BF16),
        compiler_params=_cparams("arbitrary"),
        name="expert_ffn",
    )(blk_row, blk_exp, nblk, xb, w1, w3, w2)


def _route_tile(logits_tok, ne):
    tm = logits_tok.shape[0]
    logits = logits_tok.T[0:ne, :]
    eidx = lax.broadcasted_iota(I32, (ne, tm), 0)
    m1 = jnp.max(logits, axis=0, keepdims=True)
    i1 = jnp.min(jnp.where(logits == m1, eidx, ne), axis=0, keepdims=True)
    sel1 = eidx == i1
    rest = jnp.where(sel1, -jnp.inf, logits)
    m2 = jnp.max(rest, axis=0, keepdims=True)
    i2 = jnp.min(jnp.where(rest == m2, eidx, ne), axis=0, keepdims=True)
    sel2 = eidx == i2
    e2 = jnp.exp(m2 - m1)
    den = 1.0 + e2
    rf = jnp.where(sel1 | sel2, 1.0, 0.0)
    cnt = jnp.sum(rf, axis=1, keepdims=True)
    before = (lax.broadcasted_iota(I32, (tm, tm), 0) < lax.broadcasted_iota(I32, (tm, tm), 1))
    rank = _dot(rf.astype(BF16), jnp.where(before, 1.0, 0.0).astype(BF16))
    seg = jnp.floor((cnt + (BF16_ROWS - 1)) * (1.0 / BF16_ROWS)) * BF16_ROWS
    ecol = lax.broadcasted_iota(I32, (ne, 1), 0)
    off = jnp.zeros((ne, 1), F32)
    for j in range(ne - 1):
        off = off + jnp.where(ecol > j, seg[j:j + 1, :], 0.0)
    place = off + rank
    pos1 = jnp.sum(jnp.where(sel1, place, 0.0), axis=0, keepdims=True)
    pos2 = jnp.sum(jnp.where(sel2, place, 0.0), axis=0, keepdims=True)
    lane_major = jnp.concatenate([pos1, pos2, 1.0 / den, e2 / den, jnp.zeros((4, tm), F32)], axis=0)
    tok_major = jnp.concatenate([lane_major, jnp.zeros((LANES - 8, tm), F32)], axis=0).T
    return lane_major, tok_major, cnt


def _route_kernel(lg_ref, lane_ref, tok_ref, cnt_ref, *, ne):
    tm = MOE_TILE
    for t in range(lg_ref.shape[0] // tm):
        lane_major, tok_major, cnt = _route_tile(lg_ref[t * tm:(t + 1) * tm, :], ne)
        lane_ref[:, t * tm:(t + 1) * tm] = lane_major
        tok_ref[t * tm:(t + 1) * tm, :] = tok_major
        cnt_ref[t] = jnp.broadcast_to(cnt, (ne, LANES)).astype(I32)


def _route(logits, ne):
    n = logits.shape[0]
    tm = MOE_TILE
    span = min(ROUTE_SPAN, n)
    return pl.pallas_call(
        functools.partial(_route_kernel, ne=ne),
        grid=(n // span,),
        in_specs=[pl.BlockSpec((span, LANES), lambda i: (i, 0))],
        out_specs=[pl.BlockSpec((8, span), lambda i: (0, i)), pl.BlockSpec((span, LANES), lambda i: (i, 0)),
                   pl.BlockSpec((span // tm, ne, LANES), lambda i: (i, 0, 0))],
        out_shape=[jax.ShapeDtypeStruct((8, n), F32), jax.ShapeDtypeStruct((n, LANES), F32),
                   jax.ShapeDtypeStruct((n // tm, ne, LANES), I32)],
        compiler_params=_cparams("parallel"),
        name="moe_route",
    )(logits)


def _segment_copies(seg_ref, grouped_ref, base_ref, tile_buf, off_ref, sems, tile, slot, ne, *, to_grouped, wait):
    for e in range(ne):
        seg = seg_ref[tile * ne + e]
        far = base_ref[tile * ne + e]
        near = off_ref[tile * ne + e]
        for piece in SEG_PIECES:
            g = grouped_ref.at[pl.ds(pl.multiple_of(far, BF16_ROWS), piece)]
            t = tile_buf.at[slot, pl.ds(pl.multiple_of(near, BF16_ROWS), piece)]
            cp = pltpu.make_async_copy(t, g, sems.at[slot, e]) if to_grouped else \
                pltpu.make_async_copy(g, t, sems.at[slot, e])
            has = (seg & piece) != 0

            @pl.when(has)
            def _():
                cp.wait() if wait else cp.start()

            step = jnp.where(has, piece, 0)
            far = far + step
            near = near + step


def _tail_copies(end_ref, zeros, xb_ref, sems, ne):
    return [pltpu.make_async_copy(zeros, xb_ref.at[pl.ds(pl.multiple_of(end_ref[e], BF16_ROWS), MOE_BLK)], sems.at[e])
            for e in range(ne)]


def _dispatch_kernel(seg_ref, base_ref, off_ref, end_ref, rt_ref, h_ref, xb_ref, res, zeros, sems, tail_sems):
    i = pl.program_id(0)
    nt = pl.num_programs(0)
    ne = tail_sems.shape[0]
    tm = h_ref.shape[0]
    mrows = res.shape[1]
    slot = i % 2
    copies = functools.partial(_segment_copies, seg_ref, xb_ref, base_ref, res, off_ref, sems, ne=ne, to_grouped=True)

    @pl.when(i == 0)
    def _():
        zeros[...] = jnp.zeros_like(zeros)
        for cp in _tail_copies(end_ref, zeros, xb_ref, tail_sems, ne):
            cp.start()

    row = lax.broadcasted_iota(I32, (mrows, tm), 0)
    pos = rt_ref[0:2, :].astype(I32)
    perm = jnp.where((row == pos[0:1, :]) | (row == pos[1:2, :]), 1.0, 0.0).astype(BF16)
    res[slot] = _dot(perm, h_ref[...]).astype(BF16)
    copies(tile=i, slot=slot, wait=False)

    @pl.when(i > 0)
    def _():
        copies(tile=i - 1, slot=1 - slot, wait=True)

    @pl.when(i == nt - 1)
    def _():
        copies(tile=i, slot=slot, wait=True)
        for cp in _tail_copies(end_ref, zeros, xb_ref, tail_sems, ne):
            cp.wait()


def _dispatch(h, rt_lane, lay, ne):
    n, dm = h.shape
    tm = MOE_TILE
    mrows = 2 * tm + ne * BF16_ROWS
    return pl.pallas_call(
        _dispatch_kernel,
        grid_spec=pltpu.PrefetchScalarGridSpec(
            num_scalar_prefetch=4, grid=(n // tm,),
            in_specs=[pl.BlockSpec((8, tm), lambda i, *_: (0, i)), pl.BlockSpec((tm, dm), lambda i, *_: (i, 0))],
            out_specs=pl.BlockSpec(memory_space=pl.ANY),
            scratch_shapes=[pltpu.VMEM((2, mrows, dm), BF16), pltpu.VMEM((MOE_BLK, dm), BF16),
                            pltpu.SemaphoreType.DMA((2, ne)), pltpu.SemaphoreType.DMA((ne,))]),
        out_shape=jax.ShapeDtypeStruct((lay['rows_total'], dm), BF16),
        compiler_params=_cparams("arbitrary"),
        name="moe_dispatch",
    )(lay['seg'], lay['base'], lay['off'], lay['end'], rt_lane, h)


def _combine_kernel(seg_ref, base_ref, off_ref, rt_ref, x_ref, gn_ref, yb_ref, o_ref, got, sems, *, ne):
    i = pl.program_id(0)
    nt = pl.num_programs(0)
    tm = x_ref.shape[0]
    mrows = got.shape[1]
    slot = i % 2
    copies = functools.partial(_segment_copies, seg_ref, yb_ref, base_ref, got, off_ref, sems, ne=ne, to_grouped=False)

    @pl.when(i == 0)
    def _():
        got[...] = jnp.zeros_like(got)
        copies(tile=i, slot=slot, wait=False)

    @pl.when(i + 1 < nt)
    def _():
        copies(tile=i + 1, slot=1 - slot, wait=False)

    copies(tile=i, slot=slot, wait=True)

    col = lax.broadcasted_iota(I32, (tm, mrows), 1)
    rt = rt_ref[...]
    rows = got[slot]
    sel = jnp.concatenate([jnp.where(col == rt[:, k:k + 1].astype(I32), 1.0, 0.0).astype(BF16) for k in range(TOP_K)],
                          axis=0)
    picked = _dot(sel, rows)
    y = sum(rt[:, TOP_K + k:TOP_K + k + 1] * picked[k * tm:(k + 1) * tm] for k in range(TOP_K))
    o_ref[...] = _rms(x_ref[...] + y, gn_ref[...])


def _combine(yb, rt_tok, x2, gn, lay, ne):
    n, dm = x2.shape
    tm = MOE_TILE
    mrows = 2 * tm + ne * BF16_ROWS
    tok = lambda w: pl.BlockSpec((tm, w), lambda i, *_: (i, 0))
    return pl.pallas_call(
        functools.partial(_combine_kernel, ne=ne),
        grid_spec=pltpu.PrefetchScalarGridSpec(
            num_scalar_prefetch=3, grid=(n // tm,),
            in_specs=[tok(LANES), tok(dm), pl.BlockSpec((1, dm), lambda i, *_: (0, 0)),
                      pl.BlockSpec(memory_space=pl.ANY)],
            out_specs=tok(dm),
            scratch_shapes=[pltpu.VMEM((2, mrows, dm), BF16), pltpu.SemaphoreType.DMA((2, ne))]),
        out_shape=jax.ShapeDtypeStruct((n, dm), F32),
        compiler_params=_cparams("arbitrary"),
        name="moe_combine",
    )(lay['seg'], lay['base'], lay['off'], rt_tok, x2, gn.reshape(1, dm), yb)


def _moe_layout(cnt, n):
    nt, ne = cnt.shape
    tm, blk = MOE_TILE, MOE_BLK
    seg = (cnt + BF16_ROWS - 1) // BF16_ROWS * BF16_ROWS
    cap = -(-(n + nt * BF16_ROWS + blk) // blk) * blk
    within = jnp.cumsum(seg, axis=0) - seg
    base = (within + jnp.arange(ne, dtype=I32)[None, :] * cap).astype(I32)
    off = (jnp.cumsum(seg, axis=1) - seg).astype(I32)
    rows_e = jnp.sum(seg, axis=0)
    nblk_e = (rows_e + blk - 1) // blk
    region = jnp.arange(ne, dtype=I32) * cap
    end = (region + rows_e).astype(I32)
    ends = jnp.cumsum(nblk_e)
    max_blocks = (2 * n + nt * ne * BF16_ROWS) // blk + ne
    bid = jnp.arange(max_blocks + 1, dtype=I32)
    bexp = jnp.minimum(jnp.sum((bid[:, None] >= ends[None, :]).astype(I32), axis=1), ne - 1)
    brow = bexp * (cap // blk) + bid - (ends - nblk_e)[bexp]
    nblk = ends[-1:].astype(I32)
    last = jnp.maximum(nblk[0] - 1, 0)
    used = bid < nblk[0]
    bexp = jnp.where(used, bexp, bexp[last]).astype(I32)
    brow = jnp.where(used, brow, brow[last]).astype(I32)
    flat = lambda a: a.reshape(-1)
    return dict(seg=flat(seg.astype(I32)), base=flat(base), off=flat(off), end=end,
                brow=brow, bexp=bexp, nblk=nblk, rows_total=ne * cap, max_blocks=max_blocks)


def _moe(x2, h, logits, gn, w1, w3, w2):
    n, dm = x2.shape
    ne = w1.shape[0]
    rt_lane, rt_tok, cnt = _route(logits, ne)
    lay = _moe_layout(cnt[:, :, 0], n)
    xb = _dispatch(h, rt_lane, lay, ne)
    yb = _expert_ffn(xb, w1.astype(BF16), w3.astype(BF16), w2.astype(BF16), lay['brow'], lay['bexp'], lay['nblk'],
                     lay['max_blocks'])
    return _combine(yb, rt_tok, x2, gn, lay, ne)


def kernel(x, norm_mix_g, w_in, attn_rel_bias, rg_conv_w, rg_conv_b, rg_wx, rg_bx, rg_wa, rg_ba, rg_lambda, s5_a_re, s5_a_im, s5_log_dt, s5_b_re, s5_b_im, s5_c_re, s5_c_im, s5_d, s5_w_glu, g_group, w_out, norm_ffn_g, ffn_w1, ffn_w3, ffn_w2, moe_router, moe_w1, moe_w3, moe_w2, final_norm_g):
    bsz, l, dm = x.shape
    depth = w_in.shape[0]
    assert depth == 2 and l % ATTN_TQ == 0 and ATTN_TQ == N_PREV * CHUNK, "dense layer, then the MoE layer"
    d_rg = rg_conv_w.shape[2]
    d_s5 = s5_w_glu.shape[1]
    d_attn = (w_in.shape[2] - 2 * d_rg - d_s5) // 3
    x2 = x.reshape(bsz * l, dm)
    for layer in range(depth):
        q, k, v, xr, gate, us = _in_proj(x2, norm_mix_g[layer], w_in[layer].astype(BF16), d_attn, d_rg, d_s5)
        y_attn = _attention(q, k, v, _attn_bias(attn_rel_bias[layer]), bsz, l)
        w_gates = jnp.concatenate([_block_diag(rg_wx[layer]), _block_diag(rg_wa[layer])], axis=1).astype(BF16)
        b_gates = jnp.concatenate([rg_bx[layer], rg_ba[layer]]).astype(F32)
        y_rg = _rglru(xr, gate, rg_conv_w[layer].astype(F32), rg_conv_b[layer].astype(F32), w_gates, b_gates,
                      jax.nn.softplus(-rg_lambda[layer].astype(F32)), bsz, l)
        prm = _s5_params(s5_a_re[layer], s5_a_im[layer], s5_log_dt[layer], s5_b_re[layer], s5_b_im[layer],
                         s5_c_re[layer], s5_c_im[layer], s5_d[layer])
        y_s5 = _s5(us, prm, bsz, l)
        mixed = (y_attn, y_rg, y_s5, x2, g_group[layer].astype(F32), s5_w_glu[layer].astype(BF16),
                 w_out[layer].astype(BF16), norm_ffn_g[layer].astype(F32))
        if layer == 0:
            x2, h = _out_proj(*mixed)
            x2 = _dense_ffn(h, x2, ffn_w1[0].astype(BF16), ffn_w3[0].astype(BF16), ffn_w2[0].astype(BF16))
        else:
            x2, h, logits = _out_proj(*mixed, router=moe_router[0])
            x2 = _moe(x2, h, logits, final_norm_g.astype(F32), moe_w1[0], moe_w3[0], moe_w2[0])
    return x2.reshape(bsz, l, dm)
```

```python
import functools
import math

import jax
import jax.numpy as jnp
from jax import lax
from jax.experimental import pallas as pl
from jax.experimental.pallas import tpu as pltpu

F32 = jnp.float32
BF16 = jnp.bfloat16
I32 = jnp.int32

EPS = 1e-6
CHUNK = 64
N_PREV = 8
BAND = (N_PREV + 1) * CHUNK
REL_CLIP = 128
HEAD_DIM = 64
RG_C = 8.0
RG_CONV_WIDTH = 4
S5_GROUP_DIM = 16
S5_T = 16
TOP_K = 2
NEG_BIG = -1e30

LANES = 128
BF16_ROWS = 16
VMEM_LIMIT = 52 * 1024 * 1024

ROW_TILE = 512
ATTN_TQ = 512
RG_TL = 1024
RG_SEG = 8
MOE_TILE = 256
MOE_BLK = 256
ROUTE_SPAN = 2048
SEG_PIECES = tuple(BF16_ROWS << b for b in range((MOE_TILE // BF16_ROWS).bit_length() - 1, -1, -1))
FFN_CHUNKS = ((0, 512), (512, 512), (1024, 512), (1536, 512), (2048, 512), (2560, 256))


def _cparams(*sem):
    return pltpu.CompilerParams(dimension_semantics=sem, vmem_limit_bytes=VMEM_LIMIT)


def _const_spec(shape):
    nd = len(shape)
    return pl.BlockSpec(shape, lambda *_: (0,) * nd, pipeline_mode=pl.Buffered(1))


def _rms(xf, g):
    var = jnp.mean(xf * xf, axis=-1, keepdims=True)
    return xf * lax.rsqrt(var + EPS) * g


def _sigmoid(x):
    return 1.0 / (1.0 + jnp.exp(-x))


def _gelu(x):
    c = math.sqrt(2.0 / math.pi)
    return 0.5 * x * (1.0 + jnp.tanh(c * (x + 0.044715 * (x * x * x))))


def _dot(a, b):
    return jnp.dot(a, b, preferred_element_type=F32)


def _in_proj_kernel(x_ref, g_ref, w_ref, q_ref, k_ref, v_ref, xr_ref, gt_ref, us_ref, *, d_attn, d_rg):
    u = _rms(x_ref[...], g_ref[...]).astype(BF16)
    lo = 0
    for ref, width in ((q_ref, d_attn), (k_ref, d_attn), (v_ref, d_attn),
                       (xr_ref, d_rg), (gt_ref, d_rg), (us_ref, w_ref.shape[1] - 3 * d_attn - 2 * d_rg)):
        ref[...] = _dot(u, w_ref[:, lo:lo + width].astype(BF16)).astype(ref.dtype)
        lo += width


def _in_proj(x2, g, w_bf, d_attn, d_rg, d_s5):
    n, dm = x2.shape
    tm = min(ROW_TILE, n)
    row = lambda w: pl.BlockSpec((tm, w), lambda i: (i, 0))
    return pl.pallas_call(
        functools.partial(_in_proj_kernel, d_attn=d_attn, d_rg=d_rg),
        grid=(n // tm,),
        in_specs=[row(dm), _const_spec((1, dm)), _const_spec(w_bf.shape)],
        out_specs=[row(d_attn), row(d_attn), row(d_attn), row(d_rg), row(d_rg), row(d_s5)],
        out_shape=[jax.ShapeDtypeStruct((n, d_attn), BF16)] * 3
        + [jax.ShapeDtypeStruct((n, d_rg), F32)] * 2 + [jax.ShapeDtypeStruct((n, d_s5), F32)],
        compiler_params=_cparams("parallel"),
        name="in_proj",
    )(x2, g.reshape(1, dm), w_bf)


def _attn_kernel(q_ref, kp_ref, kc_ref, vp_ref, vc_ref, bias_ref, o_ref, kz, vz, *, n_pairs):
    qi = pl.program_id(1)
    tq = q_ref.shape[0]
    kz[0:tq, :] = kp_ref[...]
    kz[tq:2 * tq, :] = kc_ref[...]
    vz[0:tq, :] = vp_ref[...]
    vz[tq:2 * tq, :] = vc_ref[...]
    lane = lax.broadcasted_iota(I32, (CHUNK, LANES), 1)
    first = lane < HEAD_DIM
    kpos = lax.broadcasted_iota(I32, (1, BAND), 1)
    pairs = [slice(hp * LANES, (hp + 1) * LANES) for hp in range(n_pairs)]

    def chunk_body(c, carry, *, masked):
        r0 = pl.multiple_of(c * CHUNK, CHUNK)
        scores = []
        for hp, cols in enumerate(pairs):
            q2 = q_ref[pl.ds(r0, CHUNK), cols] * jnp.asarray(HEAD_DIM ** -0.5, BF16)
            zero = jnp.zeros_like(q2)
            qq = jnp.concatenate([jnp.where(first, q2, zero), jnp.where(first, zero, q2)], axis=0)
            s = lax.dot_general(qq, kz[pl.ds(r0, BAND), cols], (((1,), (1,)), ((), ())), preferred_element_type=F32)
            scores.append(s + bias_ref[hp])
        if masked:
            neg = jnp.where(kpos >= N_PREV * CHUNK - c * CHUNK, 0.0, NEG_BIG)
            scores = [s + neg for s in scores]
        probs, sums = [], []
        for s in scores:
            p = jnp.exp(s - jnp.max(s, axis=-1, keepdims=True))
            sums.append(jnp.sum(p, axis=-1, keepdims=True))
            probs.append(p.astype(BF16))
        for cols, p, l in zip(pairs, probs, sums):
            o2 = _dot(p, vz[pl.ds(r0, BAND), cols]) / l
            o_ref[pl.ds(r0, CHUNK), cols] = jnp.where(first, o2[0:CHUNK], o2[CHUNK:2 * CHUNK])
        return carry

    @pl.when(qi == 0)
    def _():
        lax.fori_loop(0, tq // CHUNK, functools.partial(chunk_body, masked=True), 0)

    @pl.when(qi != 0)
    def _():
        lax.fori_loop(0, tq // CHUNK, functools.partial(chunk_body, masked=False), 0)


def _attention(q, k, v, bias2, bsz, l):
    n, da = q.shape
    tq = ATTN_TQ
    nt = l // tq
    n_pairs = da // LANES
    cur = pl.BlockSpec((tq, da), lambda b, i: (b * nt + i, 0))
    prev = pl.BlockSpec((tq, da), lambda b, i: (b * nt + jnp.maximum(i - 1, 0), 0))
    return pl.pallas_call(
        functools.partial(_attn_kernel, n_pairs=n_pairs),
        grid=(bsz, nt),
        in_specs=[cur, prev, cur, prev, cur, _const_spec(bias2.shape)],
        out_specs=cur,
        out_shape=jax.ShapeDtypeStruct((n, da), F32),
        scratch_shapes=[pltpu.VMEM((2 * tq, da), BF16), pltpu.VMEM((2 * tq, da), BF16)],
        compiler_params=_cparams("parallel", "parallel"),
        name="chunk_attn",
    )(q, k, k, v, v, bias2)


def _attn_bias(rel_bias):
    h = rel_bias.shape[0]
    tab = rel_bias.astype(F32)
    n_far = N_PREV * CHUNK - REL_CLIP + CHUNK
    lo = 2 * REL_CLIP - (BAND + CHUNK - 1 - n_far)
    ext = jnp.concatenate([jnp.broadcast_to(tab[:, 2 * REL_CLIP:], (h, n_far)), tab[:, lo:2 * REL_CLIP][:, ::-1]], axis=1)
    bias = jnp.stack([ext[:, CHUNK - 1 - i:CHUNK - 1 - i + BAND] for i in range(CHUNK)], axis=1)
    return bias.reshape(h // 2, 2 * CHUNK, BAND)


def _rglru_kernel(x_ref, gt_ref, cw_ref, cb_ref, w_ref, b_ref, sp_ref, o_ref, xpad, a_s, b_s, hcar):
    t = pl.program_id(1)
    tl, c = x_ref.shape
    seg = tl // RG_SEG
    front = 8

    @pl.when(t == 0)
    def _():
        xpad[0:front, :] = jnp.zeros((front, c), F32)
        hcar[...] = jnp.zeros_like(hcar)

    xpad[front:front + tl, :] = x_ref[...]
    xc = cb_ref[...] + sum(
        cw_ref[j:j + 1, :] * xpad[front - (RG_CONV_WIDTH - 1) + j:front - (RG_CONV_WIDTH - 1) + j + tl, :]
        for j in range(RG_CONV_WIDTH))
    xpad[0:front, :] = xpad[tl:tl + front, :]
    pre = _dot(xc.astype(BF16), w_ref[...]) + b_ref[...]
    gx = _sigmoid(pre[:, 0:c])
    ga = _sigmoid(pre[:, c:2 * c])
    log_a = -RG_C * ga * sp_ref[...]
    a = jnp.exp(log_a)
    mult = jnp.sqrt(-jnp.tanh(log_a) * (a * a + 1.0))
    b = mult * gx * xc
    nh = c // LANES
    for j in range(nh):
        a_s[j] = a[:, j * LANES:(j + 1) * LANES]
        b_s[j] = b[:, j * LANES:(j + 1) * LANES]

    def step(i, carry):
        rows = pl.ds(i, RG_SEG, stride=seg)
        out = []
        for j in range(nh):
            h, p = carry[j]
            av = a_s[j, rows, :]
            h = av * h + b_s[j, rows, :]
            p = av * p
            b_s[j, rows, :] = h
            a_s[j, rows, :] = p
            out.append((h, p))
        return tuple(out)

    init = tuple((jnp.zeros((RG_SEG, LANES), F32), jnp.ones((RG_SEG, LANES), F32)) for _ in range(nh))
    ends = lax.fori_loop(0, seg, step, init, unroll=8)
    for j in range(nh):
        cols = slice(j * LANES, (j + 1) * LANES)
        h_end, p_end = ends[j]
        carry = hcar[:, cols]
        for s in range(RG_SEG):
            rows = slice(s * seg, (s + 1) * seg)
            h = b_s[j, rows, :] + a_s[j, rows, :] * carry
            o_ref[rows, cols] = h * _gelu(gt_ref[rows, cols])
            carry = h_end[s:s + 1, :] + p_end[s:s + 1, :] * carry
        hcar[:, cols] = carry


def _rglru(xr, gate, conv_w, conv_b, w_bd, b_cat, sp, bsz, l):
    n, c = xr.shape
    tl = min(RG_TL, l)
    nt = l // tl
    row = pl.BlockSpec((tl, c), lambda b, t: (b * nt + t, 0))
    return pl.pallas_call(
        _rglru_kernel,
        grid=(bsz, nt),
        in_specs=[row, row, _const_spec(conv_w.shape), _const_spec((1, c)), _const_spec(w_bd.shape),
                  _const_spec((1, 2 * c)), _const_spec((1, c))],
        out_specs=row,
        out_shape=jax.ShapeDtypeStruct((n, c), F32),
        scratch_shapes=[pltpu.VMEM((tl + 8, c), F32), pltpu.VMEM((c // LANES, tl, LANES), F32),
                        pltpu.VMEM((c // LANES, tl, LANES), F32), pltpu.VMEM((1, c), F32)],
        compiler_params=_cparams("parallel", "arbitrary"),
        name="rglru",
    )(xr, gate, conv_w, conv_b.reshape(1, c), w_bd, b_cat.reshape(1, 2 * c), sp.reshape(1, c))


def _block_diag(w):
    nb, d, _ = w.shape
    eye = jnp.eye(nb, dtype=w.dtype)
    return (eye[:, None, :, None] * w[:, :, None, :]).reshape(nb * d, nb * d)


def _s5_kernel(u_ref, m_ref, wa_ref, wb_ref, wo_ref, c1_ref, c2a_ref, c2b_ref, d_ref, y_ref, ps, ug, xa_s, xb_s):
    t, gd = S5_T, S5_GROUP_DIM
    ng, nk, _ = ug.shape
    nh = ps.shape[0]
    per_half = LANES // gd
    rows8 = 8

    for h in range(nh):
        ps[h] = u_ref[:, h * LANES:(h + 1) * LANES]

    def to_groups(r, carry):
        base = pl.multiple_of(r * rows8 * t, rows8 * t)
        steps = [[ps[h, pl.ds(base + s, rows8, stride=t), :] for h in range(nh)] for s in range(t)]
        for g in range(ng):
            lo = (g % per_half) * gd
            ug[g, pl.ds(pl.multiple_of(r * rows8, rows8), rows8), :] = jnp.concatenate(
                [steps[s][g // per_half][:, lo:lo + gd] for s in range(t)], axis=1)
        return carry

    lax.fori_loop(0, nk // rows8, to_groups, 0)

    def project(g, carry):
        u = ug[g]
        ub = u.astype(BF16)
        rows = pl.ds(pl.multiple_of(g * nk, nk), nk)
        xa_s[rows, :] = _dot(ub, wa_ref[g])
        xb_s[rows, :] = _dot(ub, wb_ref[g])
        ug[g] = _dot(ub, m_ref[g]) + d_ref[g] * u
        return carry

    lax.fori_loop(0, ng, project, 0)

    c1, c2a, c2b = c1_ref[...], c2a_ref[...], c2b_ref[...]

    def step(k, carry):
        xa, xb = carry
        rows = pl.ds(k, ng, stride=nk)
        ia = xa_s[rows, :]
        ib = xb_s[rows, :]
        xa_s[rows, :] = xa
        return c1 * xa + c2a * xb + ia, c1 * xb + c2b * xa + ib

    z = jnp.zeros((ng, xa_s.shape[1]), F32)
    lax.fori_loop(0, nk, step, (z, z))

    def respond(g, carry):
        rows = pl.ds(pl.multiple_of(g * nk, nk), nk)
        ug[g] = _gelu(ug[g] + _dot(xa_s[rows, :].astype(BF16), wo_ref[g]))
        return carry

    lax.fori_loop(0, ng, respond, 0)

    def from_groups(r, carry):
        base = pl.multiple_of(r * rows8 * t, rows8 * t)
        rows = pl.ds(pl.multiple_of(r * rows8, rows8), rows8)
        grp = [ug[g, rows, :] for g in range(ng)]
        for s in range(t):
            lo = (s % per_half) * gd + (s // per_half) * LANES
            for h in range(nh):
                ps[h, pl.ds(base + s, rows8, stride=t), :] = jnp.concatenate(
                    [grp[g][:, lo:lo + gd] for g in range(h * per_half, (h + 1) * per_half)], axis=1)
        return carry

    lax.fori_loop(0, nk // rows8, from_groups, 0)
    for h in range(nh):
        y_ref[:, h * LANES:(h + 1) * LANES] = ps[h]


def _s5(us, prm, bsz, l):
    n, w = us.shape
    ng = w // S5_GROUP_DIM
    nk = l // S5_T
    p2 = prm[1].shape[2]
    row = pl.BlockSpec((l, w), lambda b: (b, 0))
    return pl.pallas_call(
        _s5_kernel,
        grid=(bsz,),
        in_specs=[row] + [_const_spec(a.shape) for a in prm],
        out_specs=row,
        out_shape=jax.ShapeDtypeStruct((n, w), F32),
        scratch_shapes=[pltpu.VMEM((w // LANES, l, LANES), F32), pltpu.VMEM((ng, nk, S5_T * S5_GROUP_DIM), F32),
                        pltpu.VMEM((ng * nk, p2), F32), pltpu.VMEM((ng * nk, p2), F32)],
        compiler_params=_cparams("parallel"),
        name="s5",
    )(us, *prm)


def _s5_params(a_re, a_im, log_dt, b_re, b_im, c_re, c_im, d):
    t = S5_T
    g, p = a_re.shape
    gd = S5_GROUP_DIM
    hi = lax.Precision.HIGHEST
    a = lax.complex(a_re.astype(F32), a_im.astype(F32))
    dt = jnp.exp(log_dt.astype(F32))[:, None]
    a_bar = jnp.exp(a * dt)
    bm = lax.complex(b_re.astype(F32), b_im.astype(F32))
    cm = lax.complex(c_re.astype(F32), c_im.astype(F32))
    b_bar = ((a_bar - 1.0) / a)[..., None] * bm
    steps = jnp.arange(t + 1, dtype=F32)
    pw = jnp.exp((a * dt)[:, None, :] * steps[None, :, None])

    def cdot(spec, x, y):
        e = lambda u, v: jnp.einsum(spec, u, v, precision=hi)
        return (e(x.real, y.real) - e(x.imag, y.imag)), (e(x.real, y.imag) + e(x.imag, y.real))

    cp = cm[:, None, :, :] * pw[:, :, None, :]
    kr, _ = cdot('glop,gpc->gloc', cp[:, :t], b_bar)
    krt = kr.transpose(0, 3, 1, 2)
    m = jnp.stack([jnp.pad(krt[:, :, :t - s, :], ((0, 0), (0, 0), (s, 0), (0, 0))) for s in range(t)], axis=1)
    m = m.reshape(g, t * gd, t * gd)
    win = pw[:, :t][:, ::-1][:, :, :, None] * b_bar[:, None]
    win = win.transpose(0, 1, 3, 2).reshape(g, t * gd, p)
    wa = jnp.concatenate([win.real, win.imag], axis=-1)
    wb = jnp.concatenate([win.imag, win.real], axis=-1)
    co = cp[:, 1:].transpose(0, 3, 1, 2).reshape(g, p, t * gd)
    wo = jnp.concatenate([co.real, -co.imag], axis=1)
    lt = pw[:, t]
    c1 = jnp.concatenate([lt.real, lt.real], axis=-1)
    c2a = jnp.concatenate([-lt.imag, lt.imag], axis=-1)
    c2b = jnp.concatenate([lt.imag, -lt.imag], axis=-1)
    dtile = jnp.tile(d.astype(F32).reshape(g, 1, gd), (1, t, 1)).reshape(g, 1, t * gd)
    return (m.astype(BF16), wa.astype(BF16), wb.astype(BF16), wo.astype(BF16), c1, c2a, c2b, dtile)


def _out_proj_kernel(ya_ref, yr_ref, ys_ref, x_ref, gg_ref, wglu_ref, wo_ref, gf_ref, *rest, with_router):
    if with_router:
        rt_ref, xo_ref, h_ref, lg_ref = rest
    else:
        xo_ref, h_ref = rest
    ys = ys_ref[...]
    ys = ys * _sigmoid(_dot(ys.astype(BF16), wglu_ref[...].astype(BF16)))
    acc = x_ref[...]
    lo = 0
    for y in (ya_ref[...], yr_ref[...], ys):
        w = y.shape[1]
        acc = acc + _dot(_rms(y, gg_ref[:, lo:lo + w]).astype(BF16), wo_ref[lo:lo + w, :].astype(BF16))
        lo += w
    xo_ref[...] = acc
    h = _rms(acc, gf_ref[...])
    hb = h.astype(BF16)
    h_ref[...] = hb
    if with_router:
        both = _dot(hb, rt_ref[...])
        h_lo = (h - hb.astype(F32)).astype(BF16)
        lg_ref[...] = both[:, 0:LANES] + both[:, LANES:2 * LANES] + _dot(h_lo, rt_ref[:, 0:LANES])


def _out_proj(ya, yr, ys, x2, gg, wglu_bf, wo_bf, gf, router=None):
    n, dm = x2.shape
    tm = min(ROW_TILE, n)
    row = lambda w: pl.BlockSpec((tm, w), lambda i: (i, 0))
    ins = [ya, yr, ys, x2, gg.reshape(1, -1), wglu_bf, wo_bf, gf.reshape(1, dm)]
    in_specs = [row(ya.shape[1]), row(yr.shape[1]), row(ys.shape[1]), row(dm), _const_spec((1, gg.shape[0])),
                _const_spec(wglu_bf.shape), _const_spec(wo_bf.shape), _const_spec((1, dm))]
    out_specs = [row(dm), row(dm)]
    out_shape = [jax.ShapeDtypeStruct((n, dm), F32), jax.ShapeDtypeStruct((n, dm), BF16)]
    if router is not None:
        ne = router.shape[1]
        r_hi = router.astype(BF16)
        r_lo = (router.astype(F32) - r_hi.astype(F32)).astype(BF16)
        pad = lambda a: jnp.pad(a, ((0, 0), (0, LANES - ne)))
        ins.append(jnp.concatenate([pad(r_hi), pad(r_lo)], axis=1))
        in_specs.append(_const_spec((dm, 2 * LANES)))
        out_specs.append(row(LANES))
        out_shape.append(jax.ShapeDtypeStruct((n, LANES), F32))
    return pl.pallas_call(
        functools.partial(_out_proj_kernel, with_router=router is not None),
        grid=(n // tm,),
        in_specs=in_specs, out_specs=out_specs, out_shape=out_shape,
        compiler_params=_cparams("parallel"),
        name="out_proj",
    )(*ins)


def _swiglu_chunks(h, w1, w3, w2, chunks):
    acc = None
    for lo, width in chunks:
        a = _dot(h, w1[:, lo:lo + width].astype(BF16))
        b = _dot(h, w3[:, lo:lo + width].astype(BF16))
        t = (a * _sigmoid(a) * b).astype(BF16)
        y = _dot(t, w2[lo:lo + width, :].astype(BF16))
        acc = y if acc is None else acc + y
    return acc


def _ffn_chunks(f):
    if f == sum(w for _, w in FFN_CHUNKS):
        return FFN_CHUNKS
    return ((0, f),)


def _dense_ffn_kernel(h_ref, x_ref, w1_ref, w3_ref, w2_ref, o_ref, *, chunks):
    o_ref[...] = x_ref[...] + _swiglu_chunks(h_ref[...], w1_ref, w3_ref, w2_ref, chunks)


def _dense_ffn(h, x2, w1, w3, w2):
    n, dm = x2.shape
    tm = min(ROW_TILE, n)
    row = pl.BlockSpec((tm, dm), lambda i: (i, 0))
    return pl.pallas_call(
        functools.partial(_dense_ffn_kernel, chunks=_ffn_chunks(w1.shape[1])),
        grid=(n // tm,),
        in_specs=[row, row, _const_spec(w1.shape), _const_spec(w3.shape), _const_spec(w2.shape)],
        out_specs=row,
        out_shape=jax.ShapeDtypeStruct((n, dm), F32),
        compiler_params=_cparams("parallel"),
        name="dense_ffn",
    )(h, x2, w1, w3, w2)


def _expert_ffn_kernel(brow_ref, bexp_ref, nblk_ref, x_ref, w1_ref, w3_ref, w2_ref, o_ref, *, chunks):
    @pl.when(pl.program_id(0) < nblk_ref[0])
    def _():
        o_ref[...] = _swiglu_chunks(x_ref[...], w1_ref.at[0], w3_ref.at[0], w2_ref.at[0], chunks).astype(o_ref.dtype)


def _expert_ffn(xb, w1, w3, w2, blk_row, blk_exp, nblk, max_blocks):
    rows, dm = xb.shape
    f = w1.shape[2]
    row = pl.BlockSpec((MOE_BLK, dm), lambda i, br, be, nb: (br[i], 0))
    wspec = lambda a: pl.BlockSpec((1,) + a.shape[1:], lambda i, br, be, nb: (be[i], 0, 0))
    return pl.pallas_call(
        functools.partial(_expert_ffn_kernel, chunks=_ffn_chunks(f)),
        grid_spec=pltpu.PrefetchScalarGridSpec(
            num_scalar_prefetch=3, grid=(max_blocks,),
            in_specs=[row, wspec(w1), wspec(w3), wspec(w2)], out_specs=row),
        out_shape=jax.ShapeDtypeStruct((rows, dm), BF16),
        compiler_params=_cparams("arbitrary"),
        name="expert_ffn",
    )(blk_row, blk_exp, nblk, xb, w1, w3, w2)


def _route_tile(logits_tok, ne):
    tm = logits_tok.shape[0]
    logits = logits_tok.T[0:ne, :]
    eidx = lax.broadcasted_iota(I32, (ne, tm), 0)
    m1 = jnp.max(logits, axis=0, keepdims=True)
    i1 = jnp.min(jnp.where(logits == m1, eidx, ne), axis=0, keepdims=True)
    sel1 = eidx == i1
    rest = jnp.where(sel1, -jnp.inf, logits)
    m2 = jnp.max(rest, axis=0, keepdims=True)
    i2 = jnp.min(jnp.where(rest == m2, eidx, ne), axis=0, keepdims=True)
    sel2 = eidx == i2
    e2 = jnp.exp(m2 - m1)
    den = 1.0 + e2
    rf = jnp.where(sel1 | sel2, 1.0, 0.0)
    cnt = jnp.sum(rf, axis=1, keepdims=True)
    before = (lax.broadcasted_iota(I32, (tm, tm), 0) < lax.broadcasted_iota(I32, (tm, tm), 1))
    rank = _dot(rf.astype(BF16), jnp.where(before, 1.0, 0.0).astype(BF16))
    seg = jnp.floor((cnt + (BF16_ROWS - 1)) * (1.0 / BF16_ROWS)) * BF16_ROWS
    ecol = lax.broadcasted_iota(I32, (ne, 1), 0)
    off = jnp.zeros((ne, 1), F32)
    for j in range(ne - 1):
        off = off + jnp.where(ecol > j, seg[j:j + 1, :], 0.0)
    place = off + rank
    pos1 = jnp.sum(jnp.where(sel1, place, 0.0), axis=0, keepdims=True)
    pos2 = jnp.sum(jnp.where(sel2, place, 0.0), axis=0, keepdims=True)
    lane_major = jnp.concatenate([pos1, pos2, 1.0 / den, e2 / den, jnp.zeros((4, tm), F32)], axis=0)
    tok_major = jnp.concatenate([lane_major, jnp.zeros((LANES - 8, tm), F32)], axis=0).T
    return lane_major, tok_major, cnt


def _route_kernel(lg_ref, lane_ref, tok_ref, cnt_ref, *, ne):
    tm = MOE_TILE
    for t in range(lg_ref.shape[0] // tm):
        lane_major, tok_major, cnt = _route_tile(lg_ref[t * tm:(t + 1) * tm, :], ne)
        lane_ref[:, t * tm:(t + 1) * tm] = lane_major
        tok_ref[t * tm:(t + 1) * tm, :] = tok_major
        cnt_ref[t] = jnp.broadcast_to(cnt, (ne, LANES)).astype(I32)


def _route(logits, ne):
    n = logits.shape[0]
    tm = MOE_TILE
    span = min(ROUTE_SPAN, n)
    return pl.pallas_call(
        functools.partial(_route_kernel, ne=ne),
        grid=(n // span,),
        in_specs=[pl.BlockSpec((span, LANES), lambda i: (i, 0))],
        out_specs=[pl.BlockSpec((8, span), lambda i: (0, i)), pl.BlockSpec((span, LANES), lambda i: (i, 0)),
                   pl.BlockSpec((span // tm, ne, LANES), lambda i: (i, 0, 0))],
        out_shape=[jax.ShapeDtypeStruct((8, n), F32), jax.ShapeDtypeStruct((n, LANES), F32),
                   jax.ShapeDtypeStruct((n // tm, ne, LANES), I32)],
        compiler_params=_cparams("parallel"),
        name="moe_route",
    )(logits)


def _segment_copies(seg_ref, grouped_ref, base_ref, tile_buf, off_ref, sems, tile, slot, ne, *, to_grouped, wait):
    for e in range(ne):
        seg = seg_ref[tile * ne + e]
        far = base_ref[tile * ne + e]
        near = off_ref[tile * ne + e]
        for piece in SEG_PIECES:
            g = grouped_ref.at[pl.ds(pl.multiple_of(far, BF16_ROWS), piece)]
            t = tile_buf.at[slot, pl.ds(pl.multiple_of(near, BF16_ROWS), piece)]
            cp = pltpu.make_async_copy(t, g, sems.at[slot, e]) if to_grouped else \
                pltpu.make_async_copy(g, t, sems.at[slot, e])
            has = (seg & piece) != 0

            @pl.when(has)
            def _():
                cp.wait() if wait else cp.start()

            step = jnp.where(has, piece, 0)
            far = far + step
            near = near + step


def _tail_copies(end_ref, zeros, xb_ref, sems, ne):
    return [pltpu.make_async_copy(zeros, xb_ref.at[pl.ds(pl.multiple_of(end_ref[e], BF16_ROWS), MOE_BLK)], sems.at[e])
            for e in range(ne)]


def _dispatch_kernel(seg_ref, base_ref, off_ref, end_ref, rt_ref, h_ref, xb_ref, res, zeros, sems, tail_sems):
    i = pl.program_id(0)
    nt = pl.num_programs(0)
    ne = tail_sems.shape[0]
    tm = h_ref.shape[0]
    mrows = res.shape[1]
    slot = i % 2
    copies = functools.partial(_segment_copies, seg_ref, xb_ref, base_ref, res, off_ref, sems, ne=ne, to_grouped=True)

    @pl.when(i == 0)
    def _():
        zeros[...] = jnp.zeros_like(zeros)
        for cp in _tail_copies(end_ref, zeros, xb_ref, tail_sems, ne):
            cp.start()

    row = lax.broadcasted_iota(I32, (mrows, tm), 0)
    pos = rt_ref[0:2, :].astype(I32)
    perm = jnp.where((row == pos[0:1, :]) | (row == pos[1:2, :]), 1.0, 0.0).astype(BF16)
    res[slot] = _dot(perm, h_ref[...]).astype(BF16)
    copies(tile=i, slot=slot, wait=False)

    @pl.when(i > 0)
    def _():
        copies(tile=i - 1, slot=1 - slot, wait=True)

    @pl.when(i == nt - 1)
    def _():
        copies(tile=i, slot=slot, wait=True)
        for cp in _tail_copies(end_ref, zeros, xb_ref, tail_sems, ne):
            cp.wait()


def _dispatch(h, rt_lane, lay, ne):
    n, dm = h.shape
    tm = MOE_TILE
    mrows = 2 * tm + ne * BF16_ROWS
    return pl.pallas_call(
        _dispatch_kernel,
        grid_spec=pltpu.PrefetchScalarGridSpec(
            num_scalar_prefetch=4, grid=(n // tm,),
            in_specs=[pl.BlockSpec((8, tm), lambda i, *_: (0, i)), pl.BlockSpec((tm, dm), lambda i, *_: (i, 0))],
            out_specs=pl.BlockSpec(memory_space=pl.ANY),
            scratch_shapes=[pltpu.VMEM((2, mrows, dm), BF16), pltpu.VMEM((MOE_BLK, dm), BF16),
                            pltpu.SemaphoreType.DMA((2, ne)), pltpu.SemaphoreType.DMA((ne,))]),
        out_shape=jax.ShapeDtypeStruct((lay['rows_total'], dm), BF16),
        compiler_params=_cparams("arbitrary"),
        name="moe_dispatch",
    )(lay['seg'], lay['base'], lay['off'], lay['end'], rt_lane, h)


def _combine_kernel(seg_ref, base_ref, off_ref, rt_ref, x_ref, gn_ref, yb_ref, o_ref, got, sems, *, ne):
    i = pl.program_id(0)
    nt = pl.num_programs(0)
    tm = x_ref.shape[0]
    mrows = got.shape[1]
    slot = i % 2
    copies = functools.partial(_segment_copies, seg_ref, yb_ref, base_ref, got, off_ref, sems, ne=ne, to_grouped=False)

    @pl.when(i == 0)
    def _():
        got[...] = jnp.zeros_like(got)
        copies(tile=i, slot=slot, wait=False)

    @pl.when(i + 1 < nt)
    def _():
        copies(tile=i + 1, slot=1 - slot, wait=False)

    copies(tile=i, slot=slot, wait=True)

    col = lax.broadcasted_iota(I32, (tm, mrows), 1)
    rt = rt_ref[...]
    rows = got[slot]
    sel = jnp.concatenate([jnp.where(col == rt[:, k:k + 1].astype(I32), 1.0, 0.0).astype(BF16) for k in range(TOP_K)],
                          axis=0)
    picked = _dot(sel, rows)
    y = sum(rt[:, TOP_K + k:TOP_K + k + 1] * picked[k * tm:(k + 1) * tm] for k in range(TOP_K))
    o_ref[...] = _rms(x_ref[...] + y, gn_ref[...])


def _combine(yb, rt_tok, x2, gn, lay, ne):
    n, dm = x2.shape
    tm = MOE_TILE
    mrows = 2 * tm + ne * BF16_ROWS
    tok = lambda w: pl.BlockSpec((tm, w), lambda i, *_: (i, 0))
    return pl.pallas_call(
        functools.partial(_combine_kernel, ne=ne),
        grid_spec=pltpu.PrefetchScalarGridSpec(
            num_scalar_prefetch=3, grid=(n // tm,),
            in_specs=[tok(LANES), tok(dm), pl.BlockSpec((1, dm), lambda i, *_: (0, 0)),
                      pl.BlockSpec(memory_space=pl.ANY)],
            out_specs=tok(dm),
            scratch_shapes=[pltpu.VMEM((2, mrows, dm), BF16), pltpu.SemaphoreType.DMA((2, ne))]),
        out_shape=jax.ShapeDtypeStruct((n, dm), F32),
        compiler_params=_cparams("arbitrary"),
        name="moe_combine",
    )(lay['seg'], lay['base'], lay['off'], rt_tok, x2, gn.reshape(1, dm), yb)


def _moe_layout(cnt, n):
    nt, ne = cnt.shape
    tm, blk = MOE_TILE, MOE_BLK
    seg = (cnt + BF16_ROWS - 1) // BF16_ROWS * BF16_ROWS
    cap = -(-(n + nt * BF16_ROWS + blk) // blk) * blk
    within = jnp.cumsum(seg, axis=0) - seg
    base = (within + jnp.arange(ne, dtype=I32)[None, :] * cap).astype(I32)
    off = (jnp.cumsum(seg, axis=1) - seg).astype(I32)
    rows_e = jnp.sum(seg, axis=0)
    nblk_e = (rows_e + blk - 1) // blk
    region = jnp.arange(ne, dtype=I32) * cap
    end = (region + rows_e).astype(I32)
    ends = jnp.cumsum(nblk_e)
    max_blocks = (2 * n + nt * ne * BF16_ROWS) // blk + ne
    bid = jnp.arange(max_blocks + 1, dtype=I32)
    bexp = jnp.minimum(jnp.sum((bid[:, None] >= ends[None, :]).astype(I32), axis=1), ne - 1)
    brow = bexp * (cap // blk) + bid - (ends - nblk_e)[bexp]
    nblk = ends[-1:].astype(I32)
    last = jnp.maximum(nblk[0] - 1, 0)
    used = bid < nblk[0]
    bexp = jnp.where(used, bexp, bexp[last]).astype(I32)
    brow = jnp.where(used, brow, brow[last]).astype(I32)
    flat = lambda a: a.reshape(-1)
    return dict(seg=flat(seg.astype(I32)), base=flat(base), off=flat(off), end=end,
                brow=brow, bexp=bexp, nblk=nblk, rows_total=ne * cap, max_blocks=max_blocks)


def _moe(x2, h, logits, gn, w1, w3, w2):
    n, dm = x2.shape
    ne = w1.shape[0]
    rt_lane, rt_tok, cnt = _route(logits, ne)
    lay = _moe_layout(cnt[:, :, 0], n)
    xb = _dispatch(h, rt_lane, lay, ne)
    yb = _expert_ffn(xb, w1.astype(BF16), w3.astype(BF16), w2.astype(BF16), lay['brow'], lay['bexp'], lay['nblk'],
                     lay['max_blocks'])
    return _combine(yb, rt_tok, x2, gn, lay, ne)


def kernel(x, norm_mix_g, w_in, attn_rel_bias, rg_conv_w, rg_conv_b, rg_wx, rg_bx, rg_wa, rg_ba, rg_lambda, s5_a_re, s5_a_im, s5_log_dt, s5_b_re, s5_b_im, s5_c_re, s5_c_im, s5_d, s5_w_glu, g_group, w_out, norm_ffn_g, ffn_w1, ffn_w3, ffn_w2, moe_router, moe_w1, moe_w3, moe_w2, final_norm_g):
    bsz, l, dm = x.shape
    depth = w_in.shape[0]
    assert depth == 2 and l % ATTN_TQ == 0 and ATTN_TQ == N_PREV * CHUNK, "dense layer, then the MoE layer"
    d_rg = rg_conv_w.shape[2]
    d_s5 = s5_w_glu.shape[1]
    d_attn = (w_in.shape[2] - 2 * d_rg - d_s5) // 3
    x2 = x.reshape(bsz * l, dm)
    for layer in range(depth):
        q, k, v, xr, gate, us = _in_proj(x2, norm_mix_g[layer], w_in[layer], d_attn, d_rg, d_s5)
        y_attn = _attention(q, k, v, _attn_bias(attn_rel_bias[layer]), bsz, l)
        w_gates = jnp.concatenate([_block_diag(rg_wx[layer]), _block_diag(rg_wa[layer])], axis=1).astype(BF16)
        b_gates = jnp.concatenate([rg_bx[layer], rg_ba[layer]]).astype(F32)
        y_rg = _rglru(xr, gate, rg_conv_w[layer].astype(F32), rg_conv_b[layer].astype(F32), w_gates, b_gates,
                      jax.nn.softplus(-rg_lambda[layer].astype(F32)), bsz, l)
        prm = _s5_params(s5_a_re[layer], s5_a_im[layer], s5_log_dt[layer], s5_b_re[layer], s5_b_im[layer],
                         s5_c_re[layer], s5_c_im[layer], s5_d[layer])
        y_s5 = _s5(us, prm, bsz, l)
        mixed = (y_attn, y_rg, y_s5, x2, g_group[layer].astype(F32), s5_w_glu[layer], w_out[layer],
                 norm_ffn_g[layer].astype(F32))
        if layer == 0:
            x2, h = _out_proj(*mixed)
            x2 = _dense_ffn(h, x2, ffn_w1[0], ffn_w3[0], ffn_w2[0])
        else:
            x2, h, logits = _out_proj(*mixed, router=moe_router[0])
            x2 = _moe(x2, h, logits, final_norm_g.astype(F32), moe_w1[0], moe_w3[0], moe_w2[0])
    return x2.reshape(bsz, l, dm)
```

```python
import functools
import math

import jax
import jax.numpy as jnp
from jax import lax
from jax.experimental import pallas as pl
from jax.experimental.pallas import tpu as pltpu

F32 = jnp.float32
BF16 = jnp.bfloat16
I32 = jnp.int32

EPS = 1e-6
CHUNK = 64
N_PREV = 8
BAND = (N_PREV + 1) * CHUNK
REL_CLIP = 128
HEAD_DIM = 64
RG_C = 8.0
RG_CONV_WIDTH = 4
S5_GROUP_DIM = 16
S5_T = 16
TOP_K = 2
NEG_BIG = -1e30

LANES = 128
BF16_ROWS = 16
VMEM_LIMIT = 52 * 1024 * 1024

ROW_TILE = 512
ATTN_TQ = 512
RG_TL = 1024
RG_SEG = 8
MOE_TILE = 256
MOE_BLK = 256
ROUTE_SPAN = 2048
SEG_PIECES = tuple(BF16_ROWS << b for b in range((MOE_TILE // BF16_ROWS).bit_length() - 1, -1, -1))
FFN_CHUNKS = ((0, 512), (512, 512), (1024, 512), (1536, 512), (2048, 512), (2560, 256))


def _cparams(*sem):
    return pltpu.CompilerParams(dimension_semantics=sem, vmem_limit_bytes=VMEM_LIMIT)


def _const_spec(shape):
    nd = len(shape)
    return pl.BlockSpec(shape, lambda *_: (0,) * nd, pipeline_mode=pl.Buffered(1))


def _rms(xf, g):
    var = jnp.mean(xf * xf, axis=-1, keepdims=True)
    return xf * lax.rsqrt(var + EPS) * g


def _sigmoid(x):
    return 1.0 / (1.0 + jnp.exp(-x))


def _gelu(x):
    c = math.sqrt(2.0 / math.pi)
    return 0.5 * x * (1.0 + jnp.tanh(c * (x + 0.044715 * (x * x * x))))


def _dot(a, b):
    return jnp.dot(a, b, preferred_element_type=F32)


def _in_proj_kernel(x_ref, g_ref, w_ref, q_ref, k_ref, v_ref, xr_ref, gt_ref, us_ref, *, d_attn, d_rg):
    u = _rms(x_ref[...], g_ref[...]).astype(BF16)
    lo = 0
    for ref, width in ((q_ref, d_attn), (k_ref, d_attn), (v_ref, d_attn),
                       (xr_ref, d_rg), (gt_ref, d_rg), (us_ref, w_ref.shape[1] - 3 * d_attn - 2 * d_rg)):
        ref[...] = _dot(u, w_ref[:, lo:lo + width].astype(BF16)).astype(ref.dtype)
        lo += width


def _in_proj(x2, g, w_bf, d_attn, d_rg, d_s5):
    n, dm = x2.shape
    tm = min(ROW_TILE, n)
    row = lambda w: pl.BlockSpec((tm, w), lambda i: (i, 0))
    return pl.pallas_call(
        functools.partial(_in_proj_kernel, d_attn=d_attn, d_rg=d_rg),
        grid=(n // tm,),
        in_specs=[row(dm), _const_spec((1, dm)), _const_spec(w_bf.shape)],
        out_specs=[row(d_attn), row(d_attn), row(d_attn), row(d_rg), row(d_rg), row(d_s5)],
        out_shape=[jax.ShapeDtypeStruct((n, d_attn), BF16)] * 3
        + [jax.ShapeDtypeStruct((n, d_rg), F32)] * 2 + [jax.ShapeDtypeStruct((n, d_s5), F32)],
        compiler_params=_cparams("parallel"),
        name="in_proj",
    )(x2, g.reshape(1, dm), w_bf)


def _attn_kernel(q_ref, kp_ref, kc_ref, vp_ref, vc_ref, bias_ref, o_ref, kz, vz, *, n_pairs):
    qi = pl.program_id(1)
    tq = q_ref.shape[0]
    kz[0:tq, :] = kp_ref[...]
    kz[tq:2 * tq, :] = kc_ref[...]
    vz[0:tq, :] = vp_ref[...]
    vz[tq:2 * tq, :] = vc_ref[...]
    lane = lax.broadcasted_iota(I32, (CHUNK, LANES), 1)
    first = lane < HEAD_DIM
    kpos = lax.broadcasted_iota(I32, (1, BAND), 1)
    pairs = [slice(hp * LANES, (hp + 1) * LANES) for hp in range(n_pairs)]

    def chunk_body(c, carry, *, masked):
        r0 = pl.multiple_of(c * CHUNK, CHUNK)
        scores = []
        for hp, cols in enumerate(pairs):
            q2 = q_ref[pl.ds(r0, CHUNK), cols] * jnp.asarray(HEAD_DIM ** -0.5, BF16)
            zero = jnp.zeros_like(q2)
            qq = jnp.concatenate([jnp.where(first, q2, zero), jnp.where(first, zero, q2)], axis=0)
            s = lax.dot_general(qq, kz[pl.ds(r0, BAND), cols], (((1,), (1,)), ((), ())), preferred_element_type=F32)
            scores.append(s + bias_ref[hp])
        if masked:
            neg = jnp.where(kpos >= N_PREV * CHUNK - c * CHUNK, 0.0, NEG_BIG)
            scores = [s + neg for s in scores]
        probs, sums = [], []
        for s in scores:
            p = jnp.exp(s - jnp.max(s, axis=-1, keepdims=True))
            sums.append(jnp.sum(p, axis=-1, keepdims=True))
            probs.append(p.astype(BF16))
        for cols, p, l in zip(pairs, probs, sums):
            o2 = _dot(p, vz[pl.ds(r0, BAND), cols]) / l
            o_ref[pl.ds(r0, CHUNK), cols] = jnp.where(first, o2[0:CHUNK], o2[CHUNK:2 * CHUNK])
        return carry

    @pl.when(qi == 0)
    def _():
        lax.fori_loop(0, tq // CHUNK, functools.partial(chunk_body, masked=True), 0)

    @pl.when(qi != 0)
    def _():
        lax.fori_loop(0, tq // CHUNK, functools.partial(chunk_body, masked=False), 0, unroll=2)


def _attention(q, k, v, bias2, bsz, l):
    n, da = q.shape
    tq = ATTN_TQ
    nt = l // tq
    n_pairs = da // LANES
    cur = pl.BlockSpec((tq, da), lambda b, i: (b * nt + i, 0))
    prev = pl.BlockSpec((tq, da), lambda b, i: (b * nt + jnp.maximum(i - 1, 0), 0))
    return pl.pallas_call(
        functools.partial(_attn_kernel, n_pairs=n_pairs),
        grid=(bsz, nt),
        in_specs=[cur, prev, cur, prev, cur, _const_spec(bias2.shape)],
        out_specs=cur,
        out_shape=jax.ShapeDtypeStruct((n, da), F32),
        scratch_shapes=[pltpu.VMEM((2 * tq, da), BF16), pltpu.VMEM((2 * tq, da), BF16)],
        compiler_params=_cparams("parallel", "parallel"),
        name="chunk_attn",
    )(q, k, k, v, v, bias2)


def _attn_bias(rel_bias):
    h = rel_bias.shape[0]
    tab = rel_bias.astype(F32)
    n_far = N_PREV * CHUNK - REL_CLIP + CHUNK
    lo = 2 * REL_CLIP - (BAND + CHUNK - 1 - n_far)
    ext = jnp.concatenate([jnp.broadcast_to(tab[:, 2 * REL_CLIP:], (h, n_far)), tab[:, lo:2 * REL_CLIP][:, ::-1]], axis=1)
    bias = jnp.stack([ext[:, CHUNK - 1 - i:CHUNK - 1 - i + BAND] for i in range(CHUNK)], axis=1)
    return bias.reshape(h // 2, 2 * CHUNK, BAND)


def _rglru_kernel(x_ref, gt_ref, cw_ref, cb_ref, w_ref, b_ref, sp_ref, o_ref, xpad, a_s, b_s, hcar):
    t = pl.program_id(1)
    tl, c = x_ref.shape
    seg = tl // RG_SEG
    front = 8

    @pl.when(t == 0)
    def _():
        xpad[0:front, :] = jnp.zeros((front, c), F32)
        hcar[...] = jnp.zeros_like(hcar)

    xpad[front:front + tl, :] = x_ref[...]
    xc = cb_ref[...] + sum(
        cw_ref[j:j + 1, :] * xpad[front - (RG_CONV_WIDTH - 1) + j:front - (RG_CONV_WIDTH - 1) + j + tl, :]
        for j in range(RG_CONV_WIDTH))
    xpad[0:front, :] = xpad[tl:tl + front, :]
    pre = _dot(xc.astype(BF16), w_ref[...]) + b_ref[...]
    gx = _sigmoid(pre[:, 0:c])
    ga = _sigmoid(pre[:, c:2 * c])
    log_a = -RG_C * ga * sp_ref[...]
    a = jnp.exp(log_a)
    mult = jnp.sqrt(-jnp.tanh(log_a) * (a * a + 1.0))
    b = mult * gx * xc
    nh = c // LANES
    for j in range(nh):
        a_s[j] = a[:, j * LANES:(j + 1) * LANES]
        b_s[j] = b[:, j * LANES:(j + 1) * LANES]

    def step(i, carry):
        rows = pl.ds(i, RG_SEG, stride=seg)
        out = []
        for j in range(nh):
            h, p = carry[j]
            av = a_s[j, rows, :]
            h = av * h + b_s[j, rows, :]
            p = av * p
            b_s[j, rows, :] = h
            a_s[j, rows, :] = p
            out.append((h, p))
        return tuple(out)

    init = tuple((jnp.zeros((RG_SEG, LANES), F32), jnp.ones((RG_SEG, LANES), F32)) for _ in range(nh))
    ends = lax.fori_loop(0, seg, step, init, unroll=8)
    for j in range(nh):
        cols = slice(j * LANES, (j + 1) * LANES)
        h_end, p_end = ends[j]
        carry = hcar[:, cols]
        for s in range(RG_SEG):
            rows = slice(s * seg, (s + 1) * seg)
            h = b_s[j, rows, :] + a_s[j, rows, :] * carry
            o_ref[rows, cols] = h * _gelu(gt_ref[rows, cols])
            carry = h_end[s:s + 1, :] + p_end[s:s + 1, :] * carry
        hcar[:, cols] = carry


def _rglru(xr, gate, conv_w, conv_b, w_bd, b_cat, sp, bsz, l):
    n, c = xr.shape
    tl = min(RG_TL, l)
    nt = l // tl
    row = pl.BlockSpec((tl, c), lambda b, t: (b * nt + t, 0))
    return pl.pallas_call(
        _rglru_kernel,
        grid=(bsz, nt),
        in_specs=[row, row, _const_spec(conv_w.shape), _const_spec((1, c)), _const_spec(w_bd.shape),
                  _const_spec((1, 2 * c)), _const_spec((1, c))],
        out_specs=row,
        out_shape=jax.ShapeDtypeStruct((n, c), F32),
        scratch_shapes=[pltpu.VMEM((tl + 8, c), F32), pltpu.VMEM((c // LANES, tl, LANES), F32),
                        pltpu.VMEM((c // LANES, tl, LANES), F32), pltpu.VMEM((1, c), F32)],
        compiler_params=_cparams("parallel", "arbitrary"),
        name="rglru",
    )(xr, gate, conv_w, conv_b.reshape(1, c), w_bd, b_cat.reshape(1, 2 * c), sp.reshape(1, c))


def _block_diag(w):
    nb, d, _ = w.shape
    eye = jnp.eye(nb, dtype=w.dtype)
    return (eye[:, None, :, None] * w[:, :, None, :]).reshape(nb * d, nb * d)


def _s5_kernel(u_ref, m_ref, wa_ref, wb_ref, wo_ref, c1_ref, c2a_ref, c2b_ref, d_ref, y_ref, ps, ug, xa_s, xb_s):
    t, gd = S5_T, S5_GROUP_DIM
    ng, nk, _ = ug.shape
    nh = ps.shape[0]
    per_half = LANES // gd
    rows8 = 8

    for h in range(nh):
        ps[h] = u_ref[:, h * LANES:(h + 1) * LANES]

    def to_groups(r, carry):
        base = pl.multiple_of(r * rows8 * t, rows8 * t)
        steps = [[ps[h, pl.ds(base + s, rows8, stride=t), :] for h in range(nh)] for s in range(t)]
        for g in range(ng):
            lo = (g % per_half) * gd
            ug[g, pl.ds(pl.multiple_of(r * rows8, rows8), rows8), :] = jnp.concatenate(
                [steps[s][g // per_half][:, lo:lo + gd] for s in range(t)], axis=1)
        return carry

    lax.fori_loop(0, nk // rows8, to_groups, 0)

    def project(g, carry):
        u = ug[g]
        ub = u.astype(BF16)
        rows = pl.ds(pl.multiple_of(g * nk, nk), nk)
        xa_s[rows, :] = _dot(ub, wa_ref[g])
        xb_s[rows, :] = _dot(ub, wb_ref[g])
        ug[g] = _dot(ub, m_ref[g]) + d_ref[g] * u
        return carry

    lax.fori_loop(0, ng, project, 0)

    c1, c2a, c2b = c1_ref[...], c2a_ref[...], c2b_ref[...]

    def step(k, carry):
        xa, xb = carry
        rows = pl.ds(k, ng, stride=nk)
        ia = xa_s[rows, :]
        ib = xb_s[rows, :]
        xa_s[rows, :] = xa
        return c1 * xa + c2a * xb + ia, c1 * xb + c2b * xa + ib

    z = jnp.zeros((ng, xa_s.shape[1]), F32)
    lax.fori_loop(0, nk, step, (z, z))

    def respond(g, carry):
        rows = pl.ds(pl.multiple_of(g * nk, nk), nk)
        ug[g] = _gelu(ug[g] + _dot(xa_s[rows, :].astype(BF16), wo_ref[g]))
        return carry

    lax.fori_loop(0, ng, respond, 0)

    def from_groups(r, carry):
        base = pl.multiple_of(r * rows8 * t, rows8 * t)
        rows = pl.ds(pl.multiple_of(r * rows8, rows8), rows8)
        grp = [ug[g, rows, :] for g in range(ng)]
        for s in range(t):
            lo = (s % per_half) * gd + (s // per_half) * LANES
            for h in range(nh):
                ps[h, pl.ds(base + s, rows8, stride=t), :] = jnp.concatenate(
                    [grp[g][:, lo:lo + gd] for g in range(h * per_half, (h + 1) * per_half)], axis=1)
        return carry

    lax.fori_loop(0, nk // rows8, from_groups, 0)
    for h in range(nh):
        y_ref[:, h * LANES:(h + 1) * LANES] = ps[h]


def _s5(us, prm, bsz, l):
    n, w = us.shape
    ng = w // S5_GROUP_DIM
    nk = l // S5_T
    p2 = prm[1].shape[2]
    row = pl.BlockSpec((l, w), lambda b: (b, 0))
    return pl.pallas_call(
        _s5_kernel,
        grid=(bsz,),
        in_specs=[row] + [_const_spec(a.shape) for a in prm],
        out_specs=row,
        out_shape=jax.ShapeDtypeStruct((n, w), F32),
        scratch_shapes=[pltpu.VMEM((w // LANES, l, LANES), F32), pltpu.VMEM((ng, nk, S5_T * S5_GROUP_DIM), F32),
                        pltpu.VMEM((ng * nk, p2), F32), pltpu.VMEM((ng * nk, p2), F32)],
        compiler_params=_cparams("parallel"),
        name="s5",
    )(us, *prm)


def _s5_params(a_re, a_im, log_dt, b_re, b_im, c_re, c_im, d):
    t = S5_T
    g, p = a_re.shape
    gd = S5_GROUP_DIM
    hi = lax.Precision.HIGHEST
    a = lax.complex(a_re.astype(F32), a_im.astype(F32))
    dt = jnp.exp(log_dt.astype(F32))[:, None]
    a_bar = jnp.exp(a * dt)
    bm = lax.complex(b_re.astype(F32), b_im.astype(F32))
    cm = lax.complex(c_re.astype(F32), c_im.astype(F32))
    b_bar = ((a_bar - 1.0) / a)[..., None] * bm
    steps = jnp.arange(t + 1, dtype=F32)
    pw = jnp.exp((a * dt)[:, None, :] * steps[None, :, None])

    def cdot(spec, x, y):
        e = lambda u, v: jnp.einsum(spec, u, v, precision=hi)
        return (e(x.real, y.real) - e(x.imag, y.imag)), (e(x.real, y.imag) + e(x.imag, y.real))

    cp = cm[:, None, :, :] * pw[:, :, None, :]
    kr, _ = cdot('glop,gpc->gloc', cp[:, :t], b_bar)
    krt = kr.transpose(0, 3, 1, 2)
    m = jnp.stack([jnp.pad(krt[:, :, :t - s, :], ((0, 0), (0, 0), (s, 0), (0, 0))) for s in range(t)], axis=1)
    m = m.reshape(g, t * gd, t * gd)
    win = pw[:, :t][:, ::-1][:, :, :, None] * b_bar[:, None]
    win = win.transpose(0, 1, 3, 2).reshape(g, t * gd, p)
    wa = jnp.concatenate([win.real, win.imag], axis=-1)
    wb = jnp.concatenate([win.imag, win.real], axis=-1)
    co = cp[:, 1:].transpose(0, 3, 1, 2).reshape(g, p, t * gd)
    wo = jnp.concatenate([co.real, -co.imag], axis=1)
    lt = pw[:, t]
    c1 = jnp.concatenate([lt.real, lt.real], axis=-1)
    c2a = jnp.concatenate([-lt.imag, lt.imag], axis=-1)
    c2b = jnp.concatenate([lt.imag, -lt.imag], axis=-1)
    dtile = jnp.tile(d.astype(F32).reshape(g, 1, gd), (1, t, 1)).reshape(g, 1, t * gd)
    return (m.astype(BF16), wa.astype(BF16), wb.astype(BF16), wo.astype(BF16), c1, c2a, c2b, dtile)


def _out_proj_kernel(ya_ref, yr_ref, ys_ref, x_ref, gg_ref, wglu_ref, wo_ref, gf_ref, *rest, with_router):
    if with_router:
        rt_ref, xo_ref, h_ref, lg_ref = rest
    else:
        xo_ref, h_ref = rest
    ys = ys_ref[...]
    ys = ys * _sigmoid(_dot(ys.astype(BF16), wglu_ref[...].astype(BF16)))
    acc = x_ref[...]
    lo = 0
    for y in (ya_ref[...], yr_ref[...], ys):
        w = y.shape[1]
        acc = acc + _dot(_rms(y, gg_ref[:, lo:lo + w]).astype(BF16), wo_ref[lo:lo + w, :].astype(BF16))
        lo += w
    xo_ref[...] = acc
    h = _rms(acc, gf_ref[...])
    hb = h.astype(BF16)
    h_ref[...] = hb
    if with_router:
        both = _dot(hb, rt_ref[...])
        h_lo = (h - hb.astype(F32)).astype(BF16)
        lg_ref[...] = both[:, 0:LANES] + both[:, LANES:2 * LANES] + _dot(h_lo, rt_ref[:, 0:LANES])


def _out_proj(ya, yr, ys, x2, gg, wglu_bf, wo_bf, gf, router=None):
    n, dm = x2.shape
    tm = min(ROW_TILE, n)
    row = lambda w: pl.BlockSpec((tm, w), lambda i: (i, 0))
    ins = [ya, yr, ys, x2, gg.reshape(1, -1), wglu_bf, wo_bf, gf.reshape(1, dm)]
    in_specs = [row(ya.shape[1]), row(yr.shape[1]), row(ys.shape[1]), row(dm), _const_spec((1, gg.shape[0])),
                _const_spec(wglu_bf.shape), _const_spec(wo_bf.shape), _const_spec((1, dm))]
    out_specs = [row(dm), row(dm)]
    out_shape = [jax.ShapeDtypeStruct((n, dm), F32), jax.ShapeDtypeStruct((n, dm), BF16)]
    if router is not None:
        ne = router.shape[1]
        r_hi = router.astype(BF16)
        r_lo = (router.astype(F32) - r_hi.astype(F32)).astype(BF16)
        pad = lambda a: jnp.pad(a, ((0, 0), (0, LANES - ne)))
        ins.append(jnp.concatenate([pad(r_hi), pad(r_lo)], axis=1))
        in_specs.append(_const_spec((dm, 2 * LANES)))
        out_specs.append(row(LANES))
        out_shape.append(jax.ShapeDtypeStruct((n, LANES), F32))
    return pl.pallas_call(
        functools.partial(_out_proj_kernel, with_router=router is not None),
        grid=(n // tm,),
        in_specs=in_specs, out_specs=out_specs, out_shape=out_shape,
        compiler_params=_cparams("parallel"),
        name="out_proj",
    )(*ins)


def _swiglu_chunks(h, w1, w3, w2, chunks):
    acc = None
    for lo, width in chunks:
        a = _dot(h, w1[:, lo:lo + width].astype(BF16))
        b = _dot(h, w3[:, lo:lo + width].astype(BF16))
        t = (a * _sigmoid(a) * b).astype(BF16)
        y = _dot(t, w2[lo:lo + width, :].astype(BF16))
        acc = y if acc is None else acc + y
    return acc


def _ffn_chunks(f):
    if f == sum(w for _, w in FFN_CHUNKS):
        return FFN_CHUNKS
    return ((0, f),)


def _dense_ffn_kernel(h_ref, x_ref, w1_ref, w3_ref, w2_ref, o_ref, *, chunks):
    o_ref[...] = x_ref[...] + _swiglu_chunks(h_ref[...], w1_ref, w3_ref, w2_ref, chunks)


def _dense_ffn(h, x2, w1, w3, w2):
    n, dm = x2.shape
    tm = min(ROW_TILE, n)
    row = pl.BlockSpec((tm, dm), lambda i: (i, 0))
    return pl.pallas_call(
        functools.partial(_dense_ffn_kernel, chunks=_ffn_chunks(w1.shape[1])),
        grid=(n // tm,),
        in_specs=[row, row, _const_spec(w1.shape), _const_spec(w3.shape), _const_spec(w2.shape)],
        out_specs=row,
        out_shape=jax.ShapeDtypeStruct((n, dm), F32),
        compiler_params=_cparams("parallel"),
        name="dense_ffn",
    )(h, x2, w1, w3, w2)


def _expert_ffn_kernel(brow_ref, bexp_ref, nblk_ref, x_ref, w1_ref, w3_ref, w2_ref, o_ref, *, chunks):
    @pl.when(pl.program_id(0) < nblk_ref[0])
    def _():
        o_ref[...] = _swiglu_chunks(x_ref[...], w1_ref.at[0], w3_ref.at[0], w2_ref.at[0], chunks).astype(o_ref.dtype)


def _expert_ffn(xb, w1, w3, w2, blk_row, blk_exp, nblk, max_blocks):
    rows, dm = xb.shape
    f = w1.shape[2]
    row = pl.BlockSpec((MOE_BLK, dm), lambda i, br, be, nb: (br[i], 0))
    wspec = lambda a: pl.BlockSpec((1,) + a.shape[1:], lambda i, br, be, nb: (be[i], 0, 0))
    return pl.pallas_call(
        functools.partial(_expert_ffn_kernel, chunks=_ffn_chunks(f)),
        grid_spec=pltpu.PrefetchScalarGridSpec(
            num_scalar_prefetch=3, grid=(max_blocks,),
            in_specs=[row, wspec(w1), wspec(w3), wspec(w2)], out_specs=row),
        out_shape=jax.ShapeDtypeStruct((rows, dm), BF16),
        compiler_params=_cparams("arbitrary"),
        name="expert_ffn",
    )(blk_row, blk_exp, nblk, xb, w1, w3, w2)


def _route_tile(logits_tok, filled, ne):
    tm = logits_tok.shape[0]
    logits = logits_tok.T[0:ne, :]
    eidx = lax.broadcasted_iota(I32, (ne, tm), 0)
    m1 = jnp.max(logits, axis=0, keepdims=True)
    i1 = jnp.min(jnp.where(logits == m1, eidx, ne), axis=0, keepdims=True)
    sel1 = eidx == i1
    rest = jnp.where(sel1, -jnp.inf, logits)
    m2 = jnp.max(rest, axis=0, keepdims=True)
    i2 = jnp.min(jnp.where(rest == m2, eidx, ne), axis=0, keepdims=True)
    sel2 = eidx == i2
    e2 = jnp.exp(m2 - m1)
    den = 1.0 + e2
    rf = jnp.where(sel1 | sel2, 1.0, 0.0)
    cnt = jnp.sum(rf, axis=1, keepdims=True)
    before = (lax.broadcasted_iota(I32, (tm, tm), 0) < lax.broadcasted_iota(I32, (tm, tm), 1))
    rank = _dot(rf.astype(BF16), jnp.where(before, 1.0, 0.0).astype(BF16))
    whole = lambda a: jnp.floor(a * (1.0 / BF16_ROWS)) * BF16_ROWS
    phase = filled - whole(filled)
    span = jnp.where(cnt > 0, whole(phase + cnt + (BF16_ROWS - 1)), 0.0)
    ecol = lax.broadcasted_iota(I32, (ne, 1), 0)
    off = jnp.zeros((ne, 1), F32)
    for j in range(ne - 1):
        off = off + jnp.where(ecol > j, span[j:j + 1, :], 0.0)
    place = off + phase + rank
    pos1 = jnp.sum(jnp.where(sel1, place, 0.0), axis=0, keepdims=True)
    pos2 = jnp.sum(jnp.where(sel2, place, 0.0), axis=0, keepdims=True)
    lane_major = jnp.concatenate([pos1, pos2, 1.0 / den, e2 / den, jnp.zeros((4, tm), F32)], axis=0)
    tok_major = jnp.concatenate([lane_major, jnp.zeros((LANES - 8, tm), F32)], axis=0).T
    return lane_major, tok_major, cnt


def _route_kernel(lg_ref, lane_ref, tok_ref, cnt_ref, filled, *, ne):
    tm = MOE_TILE

    @pl.when(pl.program_id(0) == 0)
    def _():
        filled[...] = jnp.zeros_like(filled)

    for t in range(lg_ref.shape[0] // tm):
        lane_major, tok_major, cnt = _route_tile(lg_ref[t * tm:(t + 1) * tm, :], filled[...], ne)
        lane_ref[:, t * tm:(t + 1) * tm] = lane_major
        tok_ref[t * tm:(t + 1) * tm, :] = tok_major
        cnt_ref[t] = jnp.broadcast_to(cnt, (ne, LANES)).astype(I32)
        filled[...] = filled[...] + cnt


def _route(logits, ne):
    n = logits.shape[0]
    tm = MOE_TILE
    span = min(ROUTE_SPAN, n)
    return pl.pallas_call(
        functools.partial(_route_kernel, ne=ne),
        grid=(n // span,),
        in_specs=[pl.BlockSpec((span, LANES), lambda i: (i, 0))],
        out_specs=[pl.BlockSpec((8, span), lambda i: (0, i)), pl.BlockSpec((span, LANES), lambda i: (i, 0)),
                   pl.BlockSpec((span // tm, ne, LANES), lambda i: (i, 0, 0))],
        out_shape=[jax.ShapeDtypeStruct((8, n), F32), jax.ShapeDtypeStruct((n, LANES), F32),
                   jax.ShapeDtypeStruct((n // tm, ne, LANES), I32)],
        scratch_shapes=[pltpu.VMEM((ne, 1), F32)],
        compiler_params=_cparams("arbitrary"),
        name="moe_route",
    )(logits)


def _segment_copies(seg_ref, grouped_ref, base_ref, tile_buf, off_ref, sems, tile, slot, ne, *, to_grouped, wait):
    for e in range(ne):
        seg = seg_ref[tile * ne + e]
        far = base_ref[tile * ne + e]
        near = off_ref[tile * ne + e]
        for piece in SEG_PIECES:
            g = grouped_ref.at[pl.ds(pl.multiple_of(far, BF16_ROWS), piece)]
            t = tile_buf.at[slot, pl.ds(pl.multiple_of(near, BF16_ROWS), piece)]
            cp = pltpu.make_async_copy(t, g, sems.at[slot, e]) if to_grouped else \
                pltpu.make_async_copy(g, t, sems.at[slot, e])
            has = (seg & piece) != 0

            @pl.when(has)
            def _():
                cp.wait() if wait else cp.start()

            step = jnp.where(has, piece, 0)
            far = far + step
            near = near + step


def _tile_rows(ne):
    return TOP_K * MOE_TILE + ne * 2 * BF16_ROWS


def _tail_copies(end_ref, zeros, xb_ref, sems, ne):
    return [pltpu.make_async_copy(zeros, xb_ref.at[pl.ds(pl.multiple_of(end_ref[e], BF16_ROWS), MOE_BLK)], sems.at[e])
            for e in range(ne)]


def _dispatch_kernel(seg_ref, base_ref, off_ref, phase_ref, end_ref, rt_ref, h_ref, xb_ref, res, zeros, partial,
                     sems, tail_sems):
    i = pl.program_id(0)
    nt = pl.num_programs(0)
    ne = tail_sems.shape[0]
    tm = h_ref.shape[0]
    mrows = res.shape[1]
    slot = i % 2
    copies = functools.partial(_segment_copies, seg_ref, xb_ref, base_ref, res, off_ref, sems, ne=ne, to_grouped=True)

    @pl.when(i == 0)
    def _():
        zeros[...] = jnp.zeros_like(zeros)
        partial[...] = jnp.zeros_like(partial)
        for cp in _tail_copies(end_ref, zeros, xb_ref, tail_sems, ne):
            cp.start()

    row = lax.broadcasted_iota(I32, (mrows, tm), 0)
    pos = rt_ref[0:2, :].astype(I32)
    perm = jnp.where((row == pos[0:1, :]) | (row == pos[1:2, :]), 1.0, 0.0).astype(BF16)
    res[slot] = _dot(perm, h_ref[...]).astype(BF16)

    tile_row = lax.broadcasted_iota(I32, (BF16_ROWS, res.shape[2]), 0)
    for e in range(ne):
        @pl.when(seg_ref[i * ne + e] > 0)
        def _():
            first = pl.ds(pl.multiple_of(off_ref[i * ne + e], BF16_ROWS), BF16_ROWS)
            merged = jnp.where(tile_row < phase_ref[i * ne + e], partial[e].astype(F32), res[slot, first, :].astype(F32))
            res[slot, first, :] = merged.astype(BF16)
            last = pl.ds(pl.multiple_of(off_ref[i * ne + e] + seg_ref[i * ne + e] - BF16_ROWS, BF16_ROWS), BF16_ROWS)
            partial[e] = res[slot, last, :]

    @pl.when(i > 0)
    def _():
        copies(tile=i - 1, slot=1 - slot, wait=True)

    copies(tile=i, slot=slot, wait=False)

    @pl.when(i == nt - 1)
    def _():
        copies(tile=i, slot=slot, wait=True)
        for cp in _tail_copies(end_ref, zeros, xb_ref, tail_sems, ne):
            cp.wait()


def _dispatch(h, rt_lane, lay, ne):
    n, dm = h.shape
    tm = MOE_TILE
    return pl.pallas_call(
        _dispatch_kernel,
        grid_spec=pltpu.PrefetchScalarGridSpec(
            num_scalar_prefetch=5, grid=(n // tm,),
            in_specs=[pl.BlockSpec((8, tm), lambda i, *_: (0, i)), pl.BlockSpec((tm, dm), lambda i, *_: (i, 0))],
            out_specs=pl.BlockSpec(memory_space=pl.ANY),
            scratch_shapes=[pltpu.VMEM((2, _tile_rows(ne), dm), BF16), pltpu.VMEM((MOE_BLK, dm), BF16),
                            pltpu.VMEM((ne, BF16_ROWS, dm), BF16),
                            pltpu.SemaphoreType.DMA((2, ne)), pltpu.SemaphoreType.DMA((ne,))]),
        out_shape=jax.ShapeDtypeStruct((lay['rows_total'], dm), BF16),
        compiler_params=_cparams("arbitrary"),
        name="moe_dispatch",
    )(lay['seg'], lay['base'], lay['off'], lay['phase'], lay['end'], rt_lane, h)


def _combine_kernel(seg_ref, base_ref, off_ref, rt_ref, x_ref, gn_ref, yb_ref, o_ref, got, sems, *, ne):
    i = pl.program_id(0)
    nt = pl.num_programs(0)
    tm = x_ref.shape[0]
    mrows = got.shape[1]
    slot = i % 2
    copies = functools.partial(_segment_copies, seg_ref, yb_ref, base_ref, got, off_ref, sems, ne=ne, to_grouped=False)

    @pl.when(i == 0)
    def _():
        got[...] = jnp.zeros_like(got)
        copies(tile=i, slot=slot, wait=False)

    @pl.when(i + 1 < nt)
    def _():
        copies(tile=i + 1, slot=1 - slot, wait=False)

    copies(tile=i, slot=slot, wait=True)

    col = lax.broadcasted_iota(I32, (tm, mrows), 1)
    rt = rt_ref[...]
    rows = got[slot]
    sel = jnp.concatenate([jnp.where(col == rt[:, k:k + 1].astype(I32), 1.0, 0.0).astype(BF16) for k in range(TOP_K)],
                          axis=0)
    picked = _dot(sel, rows)
    y = sum(rt[:, TOP_K + k:TOP_K + k + 1] * picked[k * tm:(k + 1) * tm] for k in range(TOP_K))
    o_ref[...] = _rms(x_ref[...] + y, gn_ref[...])


def _combine(yb, rt_tok, x2, gn, lay, ne):
    n, dm = x2.shape
    tm = MOE_TILE
    mrows = _tile_rows(ne)
    tok = lambda w: pl.BlockSpec((tm, w), lambda i, *_: (i, 0))
    return pl.pallas_call(
        functools.partial(_combine_kernel, ne=ne),
        grid_spec=pltpu.PrefetchScalarGridSpec(
            num_scalar_prefetch=3, grid=(n // tm,),
            in_specs=[tok(LANES), tok(dm), pl.BlockSpec((1, dm), lambda i, *_: (0, 0)),
                      pl.BlockSpec(memory_space=pl.ANY)],
            out_specs=tok(dm),
            scratch_shapes=[pltpu.VMEM((2, mrows, dm), BF16), pltpu.SemaphoreType.DMA((2, ne))]),
        out_shape=jax.ShapeDtypeStruct((n, dm), F32),
        compiler_params=_cparams("arbitrary"),
        name="moe_combine",
    )(lay['seg'], lay['base'], lay['off'], rt_tok, x2, gn.reshape(1, dm), yb)


def _moe_layout(cnt, n):
    nt, ne = cnt.shape
    blk = MOE_BLK
    whole = lambda a: a // BF16_ROWS * BF16_ROWS
    cap = -(-(n + BF16_ROWS + blk) // blk) * blk
    region = jnp.arange(ne, dtype=I32) * cap
    filled = jnp.cumsum(cnt, axis=0) - cnt
    phase = filled - whole(filled)
    span = jnp.where(cnt > 0, whole(phase + cnt + BF16_ROWS - 1), 0)
    base = (region[None, :] + filled - phase).astype(I32)
    off = (jnp.cumsum(span, axis=1) - span).astype(I32)
    rows_e = jnp.sum(cnt, axis=0)
    nblk_e = (rows_e + blk - 1) // blk
    end = (region + whole(rows_e + BF16_ROWS - 1)).astype(I32)
    ends = jnp.cumsum(nblk_e)
    max_blocks = TOP_K * n // blk + ne
    bid = jnp.arange(max_blocks + 1, dtype=I32)
    bexp = jnp.minimum(jnp.sum((bid[:, None] >= ends[None, :]).astype(I32), axis=1), ne - 1)
    brow = bexp * (cap // blk) + bid - (ends - nblk_e)[bexp]
    nblk = ends[-1:].astype(I32)
    last = jnp.maximum(nblk[0] - 1, 0)
    used = bid < nblk[0]
    bexp = jnp.where(used, bexp, bexp[last]).astype(I32)
    brow = jnp.where(used, brow, brow[last]).astype(I32)
    flat = lambda a: a.reshape(-1)
    return dict(seg=flat(span.astype(I32)), base=flat(base), off=flat(off), phase=flat(phase.astype(I32)), end=end,
                brow=brow, bexp=bexp, nblk=nblk, rows_total=ne * cap, max_blocks=max_blocks)


def _moe(x2, h, logits, gn, w1, w3, w2):
    n, dm = x2.shape
    ne = w1.shape[0]
    rt_lane, rt_tok, cnt = _route(logits, ne)
    lay = _moe_layout(cnt[:, :, 0], n)
    xb = _dispatch(h, rt_lane, lay, ne)
    yb = _expert_ffn(xb, w1.astype(BF16), w3.astype(BF16), w2.astype(BF16), lay['brow'], lay['bexp'], lay['nblk'],
                     lay['max_blocks'])
    return _combine(yb, rt_tok, x2, gn, lay, ne)


def kernel(x, norm_mix_g, w_in, attn_rel_bias, rg_conv_w, rg_conv_b, rg_wx, rg_bx, rg_wa, rg_ba, rg_lambda, s5_a_re, s5_a_im, s5_log_dt, s5_b_re, s5_b_im, s5_c_re, s5_c_im, s5_d, s5_w_glu, g_group, w_out, norm_ffn_g, ffn_w1, ffn_w3, ffn_w2, moe_router, moe_w1, moe_w3, moe_w2, final_norm_g):
    bsz, l, dm = x.shape
    depth = w_in.shape[0]
    assert depth == 2 and l % ATTN_TQ == 0 and ATTN_TQ == N_PREV * CHUNK, "dense layer, then the MoE layer"
    d_rg = rg_conv_w.shape[2]
    d_s5 = s5_w_glu.shape[1]
    d_attn = (w_in.shape[2] - 2 * d_rg - d_s5) // 3
    x2 = x.reshape(bsz * l, dm)
    for layer in range(depth):
        q, k, v, xr, gate, us = _in_proj(x2, norm_mix_g[layer], w_in[layer], d_attn, d_rg, d_s5)
        y_attn = _attention(q, k, v, _attn_bias(attn_rel_bias[layer]), bsz, l)
        w_gates = jnp.concatenate([_block_diag(rg_wx[layer]), _block_diag(rg_wa[layer])], axis=1).astype(BF16)
        b_gates = jnp.concatenate([rg_bx[layer], rg_ba[layer]]).astype(F32)
        y_rg = _rglru(xr, gate, rg_conv_w[layer].astype(F32), rg_conv_b[layer].astype(F32), w_gates, b_gates,
                      jax.nn.softplus(-rg_lambda[layer].astype(F32)), bsz, l)
        prm = _s5_params(s5_a_re[layer], s5_a_im[layer], s5_log_dt[layer], s5_b_re[layer], s5_b_im[layer],
                         s5_c_re[layer], s5_c_im[layer], s5_d[layer])
        y_s5 = _s5(us, prm, bsz, l)
        mixed = (y_attn, y_rg, y_s5, x2, g_group[layer].astype(F32), s5_w_glu[layer], w_out[layer],
                 norm_ffn_g[layer].astype(F32))
        if layer == 0:
            x2, h = _out_proj(*mixed)
            x2 = _dense_ffn(h, x2, ffn_w1[0], ffn_w3[0], ffn_w2[0])
        else:
            x2, h, logits = _out_proj(*mixed, router=moe_router[0])
            x2 = _moe(x2, h, logits, final_norm_g.astype(F32), moe_w1[0], moe_w3[0], moe_w2[0])
    return x2.reshape(bsz, l, dm)
```

```python
import functools
import math

import jax
import jax.numpy as jnp
from jax import lax
from jax.experimental import pallas as pl
from jax.experimental.pallas import tpu as pltpu

F32 = jnp.float32
BF16 = jnp.bfloat16
I32 = jnp.int32

EPS = 1e-6
CHUNK = 64
N_PREV = 8
BAND = (N_PREV + 1) * CHUNK
REL_CLIP = 128
HEAD_DIM = 64
RG_C = 8.0
RG_CONV_WIDTH = 4
S5_GROUP_DIM = 16
S5_T = 16
TOP_K = 2
NEG_BIG = -1e30

LANES = 128
BF16_ROWS = 16
VMEM_LIMIT = 52 * 1024 * 1024

ROW_TILE = 512
ATTN_TQ = 512
RG_TL = 1024
RG_SEG = 8
MOE_TILE = 256
MOE_BLK = 256
ROUTE_SPAN = 2048
SEG_PIECES = tuple(BF16_ROWS << b for b in range((MOE_TILE // BF16_ROWS).bit_length() - 1, -1, -1))
FFN_CHUNKS = ((0, 512), (512, 512), (1024, 512), (1536, 512), (2048, 512), (2560, 256))


def _cparams(*sem):
    return pltpu.CompilerParams(dimension_semantics=sem, vmem_limit_bytes=VMEM_LIMIT)


def _const_spec(shape):
    nd = len(shape)
    return pl.BlockSpec(shape, lambda *_: (0,) * nd, pipeline_mode=pl.Buffered(1))


def _rms(xf, g):
    var = jnp.mean(xf * xf, axis=-1, keepdims=True)
    return xf * lax.rsqrt(var + EPS) * g


def _sigmoid(x):
    return 1.0 / (1.0 + jnp.exp(-x))


def _gelu(x):
    c = math.sqrt(2.0 / math.pi)
    return 0.5 * x * (1.0 + jnp.tanh(c * (x + 0.044715 * (x * x * x))))


def _dot(a, b):
    return jnp.dot(a, b, preferred_element_type=F32)


def _in_proj_kernel(x_ref, g_ref, w_ref, q_ref, k_ref, v_ref, xr_ref, gt_ref, us_ref, *, d_attn, d_rg):
    u = _rms(x_ref[...], g_ref[...]).astype(BF16)
    lo = 0
    for ref, width in ((q_ref, d_attn), (k_ref, d_attn), (v_ref, d_attn),
                       (xr_ref, d_rg), (gt_ref, d_rg), (us_ref, w_ref.shape[1] - 3 * d_attn - 2 * d_rg)):
        ref[...] = _dot(u, w_ref[:, lo:lo + width].astype(BF16)).astype(ref.dtype)
        lo += width


def _layer_spec(stacked, layer):
    return pl.BlockSpec((None,) + stacked.shape[1:], lambda *_: (layer, 0, 0), pipeline_mode=pl.Buffered(1))


def _in_proj(x2, g, w_all, layer, d_attn, d_rg, d_s5):
    n, dm = x2.shape
    tm = min(ROW_TILE, n)
    row = lambda w: pl.BlockSpec((tm, w), lambda i: (i, 0))
    return pl.pallas_call(
        functools.partial(_in_proj_kernel, d_attn=d_attn, d_rg=d_rg),
        grid=(n // tm,),
        in_specs=[row(dm), _const_spec((1, dm)), _layer_spec(w_all, layer)],
        out_specs=[row(d_attn), row(d_attn), row(d_attn), row(d_rg), row(d_rg), row(d_s5)],
        out_shape=[jax.ShapeDtypeStruct((n, d_attn), BF16)] * 3
        + [jax.ShapeDtypeStruct((n, d_rg), F32)] * 2 + [jax.ShapeDtypeStruct((n, d_s5), F32)],
        compiler_params=_cparams("parallel"),
        name="in_proj",
    )(x2, g.reshape(1, dm), w_all)


def _attn_kernel(q_ref, kp_ref, kc_ref, vp_ref, vc_ref, bias_ref, o_ref, kz, vz, *, n_pairs):
    qi = pl.program_id(1)
    tq = q_ref.shape[0]
    kz[0:tq, :] = kp_ref[...]
    kz[tq:2 * tq, :] = kc_ref[...]
    vz[0:tq, :] = vp_ref[...]
    vz[tq:2 * tq, :] = vc_ref[...]
    lane = lax.broadcasted_iota(I32, (CHUNK, LANES), 1)
    first = lane < HEAD_DIM
    kpos = lax.broadcasted_iota(I32, (1, BAND), 1)
    pairs = [slice(hp * LANES, (hp + 1) * LANES) for hp in range(n_pairs)]

    def chunk_body(c, carry, *, masked):
        r0 = pl.multiple_of(c * CHUNK, CHUNK)
        scores = []
        for hp, cols in enumerate(pairs):
            q2 = q_ref[pl.ds(r0, CHUNK), cols] * jnp.asarray(HEAD_DIM ** -0.5, BF16)
            zero = jnp.zeros_like(q2)
            qq = jnp.concatenate([jnp.where(first, q2, zero), jnp.where(first, zero, q2)], axis=0)
            s = lax.dot_general(qq, kz[pl.ds(r0, BAND), cols], (((1,), (1,)), ((), ())), preferred_element_type=F32)
            scores.append(s + bias_ref[hp])
        if masked:
            neg = jnp.where(kpos >= N_PREV * CHUNK - c * CHUNK, 0.0, NEG_BIG)
            scores = [s + neg for s in scores]
        probs, sums = [], []
        for s in scores:
            p = jnp.exp(s - jnp.max(s, axis=-1, keepdims=True))
            sums.append(jnp.sum(p, axis=-1, keepdims=True))
            probs.append(p.astype(BF16))
        for cols, p, l in zip(pairs, probs, sums):
            o2 = _dot(p, vz[pl.ds(r0, BAND), cols]) / l
            o_ref[pl.ds(r0, CHUNK), cols] = jnp.where(first, o2[0:CHUNK], o2[CHUNK:2 * CHUNK])
        return carry

    @pl.when(qi == 0)
    def _():
        lax.fori_loop(0, tq // CHUNK, functools.partial(chunk_body, masked=True), 0)

    @pl.when(qi != 0)
    def _():
        lax.fori_loop(0, tq // CHUNK, functools.partial(chunk_body, masked=False), 0, unroll=2)


def _attention(q, k, v, bias2, bsz, l):
    n, da = q.shape
    tq = ATTN_TQ
    nt = l // tq
    n_pairs = da // LANES
    cur = pl.BlockSpec((tq, da), lambda b, i: (b * nt + i, 0))
    prev = pl.BlockSpec((tq, da), lambda b, i: (b * nt + jnp.maximum(i - 1, 0), 0))
    return pl.pallas_call(
        functools.partial(_attn_kernel, n_pairs=n_pairs),
        grid=(bsz, nt),
        in_specs=[cur, prev, cur, prev, cur, _const_spec(bias2.shape)],
        out_specs=cur,
        out_shape=jax.ShapeDtypeStruct((n, da), F32),
        scratch_shapes=[pltpu.VMEM((2 * tq, da), BF16), pltpu.VMEM((2 * tq, da), BF16)],
        compiler_params=_cparams("parallel", "parallel"),
        name="chunk_attn",
    )(q, k, k, v, v, bias2)


def _attn_bias(rel_bias):
    h = rel_bias.shape[0]
    tab = rel_bias.astype(F32)
    n_far = N_PREV * CHUNK - REL_CLIP + CHUNK
    lo = 2 * REL_CLIP - (BAND + CHUNK - 1 - n_far)
    ext = jnp.concatenate([jnp.broadcast_to(tab[:, 2 * REL_CLIP:], (h, n_far)), tab[:, lo:2 * REL_CLIP][:, ::-1]], axis=1)
    wide = BAND + CHUNK
    ring = jnp.concatenate([ext[:, CHUNK - 1:], jnp.zeros((h, 1), F32), ext[:, :CHUNK - 1]], axis=1)
    bias = jnp.broadcast_to(ring[:, None, :], (h, CHUNK, wide)).reshape(h, CHUNK * wide)
    bias = bias[:, :CHUNK * (wide - 1)].reshape(h, CHUNK, wide - 1)[..., :BAND]
    return bias.reshape(h // 2, 2 * CHUNK, BAND)


def _rglru_kernel(x_ref, gt_ref, cw_ref, cb_ref, w_ref, b_ref, sp_ref, o_ref, xpad, a_s, b_s, hcar):
    t = pl.program_id(1)
    tl, c = x_ref.shape
    seg = tl // RG_SEG
    front = 8

    @pl.when(t == 0)
    def _():
        xpad[0:front, :] = jnp.zeros((front, c), F32)
        hcar[...] = jnp.zeros_like(hcar)

    xpad[front:front + tl, :] = x_ref[...]
    xc = cb_ref[...] + sum(
        cw_ref[j:j + 1, :] * xpad[front - (RG_CONV_WIDTH - 1) + j:front - (RG_CONV_WIDTH - 1) + j + tl, :]
        for j in range(RG_CONV_WIDTH))
    xpad[0:front, :] = xpad[tl:tl + front, :]
    pre = _dot(xc.astype(BF16), w_ref[...]) + b_ref[...]
    gx = _sigmoid(pre[:, 0:c])
    ga = _sigmoid(pre[:, c:2 * c])
    log_a = -RG_C * ga * sp_ref[...]
    a = jnp.exp(log_a)
    mult = jnp.sqrt(-jnp.tanh(log_a) * (a * a + 1.0))
    b = mult * gx * xc
    nh = c // LANES
    for j in range(nh):
        a_s[j] = a[:, j * LANES:(j + 1) * LANES]
        b_s[j] = b[:, j * LANES:(j + 1) * LANES]

    def step(i, carry):
        rows = pl.ds(i, RG_SEG, stride=seg)
        out = []
        for j in range(nh):
            h, p = carry[j]
            av = a_s[j, rows, :]
            h = av * h + b_s[j, rows, :]
            p = av * p
            b_s[j, rows, :] = h
            a_s[j, rows, :] = p
            out.append((h, p))
        return tuple(out)

    init = tuple((jnp.zeros((RG_SEG, LANES), F32), jnp.ones((RG_SEG, LANES), F32)) for _ in range(nh))
    ends = lax.fori_loop(0, seg, step, init, unroll=8)
    for j in range(nh):
        cols = slice(j * LANES, (j + 1) * LANES)
        h_end, p_end = ends[j]
        carry = hcar[:, cols]
        for s in range(RG_SEG):
            rows = slice(s * seg, (s + 1) * seg)
            h = b_s[j, rows, :] + a_s[j, rows, :] * carry
            o_ref[rows, cols] = h * _gelu(gt_ref[rows, cols])
            carry = h_end[s:s + 1, :] + p_end[s:s + 1, :] * carry
        hcar[:, cols] = carry


def _rglru(xr, gate, conv_w, conv_b, w_bd, b_cat, sp, bsz, l):
    n, c = xr.shape
    tl = min(RG_TL, l)
    nt = l // tl
    row = pl.BlockSpec((tl, c), lambda b, t: (b * nt + t, 0))
    return pl.pallas_call(
        _rglru_kernel,
        grid=(bsz, nt),
        in_specs=[row, row, _const_spec(conv_w.shape), _const_spec((1, c)), _const_spec(w_bd.shape),
                  _const_spec((1, 2 * c)), _const_spec((1, c))],
        out_specs=row,
        out_shape=jax.ShapeDtypeStruct((n, c), F32),
        scratch_shapes=[pltpu.VMEM((tl + 8, c), F32), pltpu.VMEM((c // LANES, tl, LANES), F32),
                        pltpu.VMEM((c // LANES, tl, LANES), F32), pltpu.VMEM((1, c), F32)],
        compiler_params=_cparams("parallel", "arbitrary"),
        name="rglru",
    )(xr, gate, conv_w, conv_b.reshape(1, c), w_bd, b_cat.reshape(1, 2 * c), sp.reshape(1, c))


def _block_diag(w):
    nb, d, _ = w.shape
    eye = jnp.eye(nb, dtype=w.dtype)
    return (eye[:, None, :, None] * w[:, :, None, :]).reshape(nb * d, nb * d)


def _s5_kernel(u_ref, m_ref, wa_ref, wb_ref, wo_ref, c1_ref, c2a_ref, c2b_ref, d_ref, y_ref, ps, ug, xa_s, xb_s):
    t, gd = S5_T, S5_GROUP_DIM
    ng, nk, _ = ug.shape
    nh = ps.shape[0]
    per_half = LANES // gd
    rows8 = 8

    for h in range(nh):
        ps[h] = u_ref[:, h * LANES:(h + 1) * LANES]

    lane_block = lax.broadcasted_iota(I32, (rows8, LANES), 1) // gd

    def block_transpose(vs):
        d = per_half // 2
        while d:
            keep = (lane_block & d) == 0
            out = list(vs)
            for i in range(per_half):
                if i & d == 0:
                    out[i] = jnp.where(keep, vs[i], pltpu.roll(vs[i + d], d * gd, 1))
                    out[i + d] = jnp.where(keep, pltpu.roll(vs[i], LANES - d * gd, 1), vs[i + d])
            vs, d = out, d // 2
        return vs

    def to_groups(r, carry):
        base = pl.multiple_of(r * rows8 * t, rows8 * t)
        rows = pl.ds(pl.multiple_of(r * rows8, rows8), rows8)
        for h in range(nh):
            for j in range(t // per_half):
                steps = [ps[h, pl.ds(base + j * per_half + i, rows8, stride=t), :] for i in range(per_half)]
                for k, v in enumerate(block_transpose(steps)):
                    ug[h * per_half + k, rows, j * LANES:(j + 1) * LANES] = v
        return carry

    lax.fori_loop(0, nk // rows8, to_groups, 0, unroll=2)

    def project(g, carry):
        u = ug[g]
        ub = u.astype(BF16)
        rows = pl.ds(pl.multiple_of(g * nk, nk), nk)
        xa_s[rows, :] = _dot(ub, wa_ref[g])
        xb_s[rows, :] = _dot(ub, wb_ref[g])
        ug[g] = _dot(ub, m_ref[g]) + d_ref[g] * u
        return carry

    lax.fori_loop(0, ng, project, 0)

    c1, c2a, c2b = c1_ref[...], c2a_ref[...], c2b_ref[...]

    def step(k, carry):
        xa, xb = carry
        rows = pl.ds(k, ng, stride=nk)
        ia = xa_s[rows, :]
        ib = xb_s[rows, :]
        xa_s[rows, :] = xa
        return c1 * xa + c2a * xb + ia, c1 * xb + c2b * xa + ib

    z = jnp.zeros((ng, xa_s.shape[1]), F32)
    lax.fori_loop(0, nk, step, (z, z), unroll=4)

    def respond(g, carry):
        rows = pl.ds(pl.multiple_of(g * nk, nk), nk)
        ug[g] = _gelu(ug[g] + _dot(xa_s[rows, :].astype(BF16), wo_ref[g]))
        return carry

    lax.fori_loop(0, ng, respond, 0)

    def from_groups(r, carry):
        base = pl.multiple_of(r * rows8 * t, rows8 * t)
        rows = pl.ds(pl.multiple_of(r * rows8, rows8), rows8)
        for h in range(nh):
            for j in range(t // per_half):
                groups = [ug[h * per_half + k, rows, j * LANES:(j + 1) * LANES] for k in range(per_half)]
                for i, v in enumerate(block_transpose(groups)):
                    ps[h, pl.ds(base + j * per_half + i, rows8, stride=t), :] = v
        return carry

    lax.fori_loop(0, nk // rows8, from_groups, 0, unroll=2)
    for h in range(nh):
        y_ref[:, h * LANES:(h + 1) * LANES] = ps[h]


def _s5(us, prm, bsz, l):
    n, w = us.shape
    ng = w // S5_GROUP_DIM
    nk = l // S5_T
    p2 = prm[1].shape[2]
    row = pl.BlockSpec((l, w), lambda b: (b, 0))
    return pl.pallas_call(
        _s5_kernel,
        grid=(bsz,),
        in_specs=[row] + [_const_spec(a.shape) for a in prm],
        out_specs=row,
        out_shape=jax.ShapeDtypeStruct((n, w), F32),
        scratch_shapes=[pltpu.VMEM((w // LANES, l, LANES), F32), pltpu.VMEM((ng, nk, S5_T * S5_GROUP_DIM), F32),
                        pltpu.VMEM((ng * nk, p2), F32), pltpu.VMEM((ng * nk, p2), F32)],
        compiler_params=_cparams("parallel"),
        name="s5",
    )(us, *prm)


def _s5_params(a_re, a_im, log_dt, b_re, b_im, c_re, c_im, d):
    t = S5_T
    g, p = a_re.shape
    gd = S5_GROUP_DIM
    hi = lax.Precision.HIGHEST
    a = lax.complex(a_re.astype(F32), a_im.astype(F32))
    dt = jnp.exp(log_dt.astype(F32))[:, None]
    a_bar = jnp.exp(a * dt)
    bm = lax.complex(b_re.astype(F32), b_im.astype(F32))
    cm = lax.complex(c_re.astype(F32), c_im.astype(F32))
    b_bar = ((a_bar - 1.0) / a)[..., None] * bm
    steps = jnp.arange(t + 1, dtype=F32)
    pw = jnp.exp((a * dt)[:, None, :] * steps[None, :, None])

    cp = (cm[:, None, :, :] * pw[:, :, None, :]).transpose(0, 3, 1, 2)
    w = t * gd
    lag = cp[:, :, :t].reshape(g, p, w)
    k2 = jnp.einsum('gpc,gpx->gcx', jnp.concatenate([b_bar.real, -b_bar.imag], axis=1),
                    jnp.concatenate([lag.real, lag.imag], axis=1), precision=hi)
    skew = jnp.broadcast_to(jnp.concatenate([k2, jnp.zeros_like(k2)], axis=-1)[:, :, None, :], (g, gd, t, 2 * w))
    skew = skew.reshape(g, gd, t * 2 * w)[:, :, :t * (2 * w - gd)].reshape(g, gd, t, 2 * w - gd)[..., :w]
    m = skew.transpose(0, 2, 1, 3).reshape(g, w, w)
    win = pw[:, :t][:, ::-1][:, :, :, None] * b_bar[:, None]
    win = win.transpose(0, 1, 3, 2).reshape(g, w, p)
    wa = jnp.concatenate([win.real, win.imag], axis=-1)
    wb = jnp.concatenate([win.imag, win.real], axis=-1)
    co = cp[:, :, 1:].reshape(g, p, w)
    wo = jnp.concatenate([co.real, -co.imag], axis=1)
    lt = pw[:, t]
    c1 = jnp.concatenate([lt.real, lt.real], axis=-1)
    c2a = jnp.concatenate([-lt.imag, lt.imag], axis=-1)
    c2b = jnp.concatenate([lt.imag, -lt.imag], axis=-1)
    dtile = jnp.tile(d.astype(F32).reshape(g, 1, gd), (1, t, 1)).reshape(g, 1, t * gd)
    return (m.astype(BF16), wa.astype(BF16), wb.astype(BF16), wo.astype(BF16), c1, c2a, c2b, dtile)


def _out_proj_kernel(ya_ref, yr_ref, ys_ref, x_ref, gg_ref, wglu_ref, wo_ref, gf_ref, *rest, with_router):
    if with_router:
        rt_ref, xo_ref, h_ref, lg_ref = rest
    else:
        xo_ref, h_ref = rest
    ys = ys_ref[...]
    ys = ys * _sigmoid(_dot(ys.astype(BF16), wglu_ref[...].astype(BF16)))
    acc = x_ref[...]
    lo = 0
    for y in (ya_ref[...], yr_ref[...], ys):
        w = y.shape[1]
        acc = acc + _dot(_rms(y, gg_ref[:, lo:lo + w]).astype(BF16), wo_ref[lo:lo + w, :].astype(BF16))
        lo += w
    xo_ref[...] = acc
    h = _rms(acc, gf_ref[...])
    hb = h.astype(BF16)
    h_ref[...] = hb
    if with_router:
        both = _dot(hb, rt_ref[...])
        h_lo = (h - hb.astype(F32)).astype(BF16)
        lg_ref[...] = both[:, 0:LANES] + both[:, LANES:2 * LANES] + _dot(h_lo, rt_ref[:, 0:LANES])


def _out_proj(ya, yr, ys, x2, gg, wglu_all, wo_all, layer, gf, router=None):
    n, dm = x2.shape
    tm = min(ROW_TILE, n)
    row = lambda w: pl.BlockSpec((tm, w), lambda i: (i, 0))
    ins = [ya, yr, ys, x2, gg.reshape(1, -1), wglu_all, wo_all, gf.reshape(1, dm)]
    in_specs = [row(ya.shape[1]), row(yr.shape[1]), row(ys.shape[1]), row(dm), _const_spec((1, gg.shape[0])),
                _layer_spec(wglu_all, layer), _layer_spec(wo_all, layer), _const_spec((1, dm))]
    out_specs = [row(dm), row(dm)]
    out_shape = [jax.ShapeDtypeStruct((n, dm), F32), jax.ShapeDtypeStruct((n, dm), BF16)]
    if router is not None:
        ne = router.shape[1]
        r_hi = router.astype(BF16)
        r_lo = (router.astype(F32) - r_hi.astype(F32)).astype(BF16)
        pad = lambda a: jnp.pad(a, ((0, 0), (0, LANES - ne)))
        ins.append(jnp.concatenate([pad(r_hi), pad(r_lo)], axis=1))
        in_specs.append(_const_spec((dm, 2 * LANES)))
        out_specs.append(row(LANES))
        out_shape.append(jax.ShapeDtypeStruct((n, LANES), F32))
    return pl.pallas_call(
        functools.partial(_out_proj_kernel, with_router=router is not None),
        grid=(n // tm,),
        in_specs=in_specs, out_specs=out_specs, out_shape=out_shape,
        compiler_params=_cparams("parallel"),
        name="out_proj",
    )(*ins)


def _swiglu_chunks(h, w1, w3, w2, chunks):
    acc = None
    for lo, width in chunks:
        a = _dot(h, w1[:, lo:lo + width].astype(BF16))
        b = _dot(h, w3[:, lo:lo + width].astype(BF16))
        t = (a * _sigmoid(a) * b).astype(BF16)
        y = _dot(t, w2[lo:lo + width, :].astype(BF16))
        acc = y if acc is None else acc + y
    return acc


def _ffn_chunks(f):
    if f == sum(w for _, w in FFN_CHUNKS):
        return FFN_CHUNKS
    return ((0, f),)


def _dense_ffn_kernel(h_ref, x_ref, w1_ref, w3_ref, w2_ref, o_ref, *, chunks):
    o_ref[...] = x_ref[...] + _swiglu_chunks(h_ref[...], w1_ref, w3_ref, w2_ref, chunks)


def _dense_ffn(h, x2, w1, w3, w2):
    n, dm = x2.shape
    tm = min(ROW_TILE, n)
    row = pl.BlockSpec((tm, dm), lambda i: (i, 0))
    return pl.pallas_call(
        functools.partial(_dense_ffn_kernel, chunks=_ffn_chunks(w1.shape[1])),
        grid=(n // tm,),
        in_specs=[row, row, _const_spec(w1.shape), _const_spec(w3.shape), _const_spec(w2.shape)],
        out_specs=row,
        out_shape=jax.ShapeDtypeStruct((n, dm), F32),
        compiler_params=_cparams("parallel"),
        name="dense_ffn",
    )(h, x2, w1, w3, w2)


def _expert_ffn_kernel(brow_ref, bexp_ref, nblk_ref, x_ref, w1_ref, w3_ref, w2_ref, o_ref, *, chunks):
    @pl.when(pl.program_id(0) < nblk_ref[0])
    def _():
        o_ref[...] = _swiglu_chunks(x_ref[...], w1_ref.at[0], w3_ref.at[0], w2_ref.at[0], chunks).astype(o_ref.dtype)


def _expert_ffn(xb, w1, w3, w2, blk_row, blk_exp, nblk, max_blocks):
    rows, dm = xb.shape
    f = w1.shape[2]
    row = pl.BlockSpec((MOE_BLK, dm), lambda i, br, be, nb: (br[i], 0))
    wspec = lambda a: pl.BlockSpec((1,) + a.shape[1:], lambda i, br, be, nb: (be[i], 0, 0))
    return pl.pallas_call(
        functools.partial(_expert_ffn_kernel, chunks=_ffn_chunks(f)),
        grid_spec=pltpu.PrefetchScalarGridSpec(
            num_scalar_prefetch=3, grid=(max_blocks,),
            in_specs=[row, wspec(w1), wspec(w3), wspec(w2)], out_specs=row),
        out_shape=jax.ShapeDtypeStruct((rows, dm), BF16),
        compiler_params=_cparams("arbitrary"),
        name="expert_ffn",
    )(blk_row, blk_exp, nblk, xb, w1, w3, w2)


def _route_tile(logits_tok, filled, ne):
    tm = logits_tok.shape[0]
    logits = logits_tok.T[0:ne, :]
    eidx = lax.broadcasted_iota(I32, (ne, tm), 0)
    m1 = jnp.max(logits, axis=0, keepdims=True)
    i1 = jnp.min(jnp.where(logits == m1, eidx, ne), axis=0, keepdims=True)
    sel1 = eidx == i1
    rest = jnp.where(sel1, -jnp.inf, logits)
    m2 = jnp.max(rest, axis=0, keepdims=True)
    i2 = jnp.min(jnp.where(rest == m2, eidx, ne), axis=0, keepdims=True)
    sel2 = eidx == i2
    e2 = jnp.exp(m2 - m1)
    den = 1.0 + e2
    rf = jnp.where(sel1 | sel2, 1.0, 0.0)
    cnt = jnp.sum(rf, axis=1, keepdims=True)
    before = (lax.broadcasted_iota(I32, (tm, tm), 0) < lax.broadcasted_iota(I32, (tm, tm), 1))
    rank = _dot(rf.astype(BF16), jnp.where(before, 1.0, 0.0).astype(BF16))
    whole = lambda a: jnp.floor(a * (1.0 / BF16_ROWS)) * BF16_ROWS
    phase = filled - whole(filled)
    span = jnp.where(cnt > 0, whole(phase + cnt + (BF16_ROWS - 1)), 0.0)
    ecol = lax.broadcasted_iota(I32, (ne, 1), 0)
    off = jnp.zeros((ne, 1), F32)
    for j in range(ne - 1):
        off = off + jnp.where(ecol > j, span[j:j + 1, :], 0.0)
    place = off + phase + rank
    pos1 = jnp.sum(jnp.where(sel1, place, 0.0), axis=0, keepdims=True)
    pos2 = jnp.sum(jnp.where(sel2, place, 0.0), axis=0, keepdims=True)
    lane_major = jnp.concatenate([pos1, pos2, 1.0 / den, e2 / den, jnp.zeros((4, tm), F32)], axis=0)
    tok_major = jnp.concatenate([lane_major, jnp.zeros((LANES - 8, tm), F32)], axis=0).T
    return lane_major, tok_major, cnt


def _route_kernel(lg_ref, lane_ref, tok_ref, cnt_ref, filled, *, ne):
    tm = MOE_TILE

    @pl.when(pl.program_id(0) == 0)
    def _():
        filled[...] = jnp.zeros_like(filled)

    for t in range(lg_ref.shape[0] // tm):
        lane_major, tok_major, cnt = _route_tile(lg_ref[t * tm:(t + 1) * tm, :], filled[...], ne)
        lane_ref[:, t * tm:(t + 1) * tm] = lane_major
        tok_ref[t * tm:(t + 1) * tm, :] = tok_major
        cnt_ref[t] = jnp.broadcast_to(cnt, (ne, LANES)).astype(I32)
        filled[...] = filled[...] + cnt


def _route(logits, ne):
    n = logits.shape[0]
    tm = MOE_TILE
    span = min(ROUTE_SPAN, n)
    return pl.pallas_call(
        functools.partial(_route_kernel, ne=ne),
        grid=(n // span,),
        in_specs=[pl.BlockSpec((span, LANES), lambda i: (i, 0))],
        out_specs=[pl.BlockSpec((8, span), lambda i: (0, i)), pl.BlockSpec((span, LANES), lambda i: (i, 0)),
                   pl.BlockSpec((span // tm, ne, LANES), lambda i: (i, 0, 0))],
        out_shape=[jax.ShapeDtypeStruct((8, n), F32), jax.ShapeDtypeStruct((n, LANES), F32),
                   jax.ShapeDtypeStruct((n // tm, ne, LANES), I32)],
        scratch_shapes=[pltpu.VMEM((ne, 1), F32)],
        compiler_params=_cparams("arbitrary"),
        name="moe_route",
    )(logits)


def _segment_copies(seg_ref, grouped_ref, base_ref, tile_buf, off_ref, sems, tile, slot, ne, *, to_grouped, wait):
    for e in range(ne):
        seg = seg_ref[tile * ne + e]
        far = base_ref[tile * ne + e]
        near = off_ref[tile * ne + e]
        for piece in SEG_PIECES:
            g = grouped_ref.at[pl.ds(pl.multiple_of(far, BF16_ROWS), piece)]
            t = tile_buf.at[slot, pl.ds(pl.multiple_of(near, BF16_ROWS), piece)]
            cp = pltpu.make_async_copy(t, g, sems.at[slot, e]) if to_grouped else \
                pltpu.make_async_copy(g, t, sems.at[slot, e])
            has = (seg & piece) != 0

            @pl.when(has)
            def _():
                cp.wait() if wait else cp.start()

            step = jnp.where(has, piece, 0)
            far = far + step
            near = near + step


def _tile_rows(ne):
    return TOP_K * MOE_TILE + ne * 2 * BF16_ROWS


def _tail_copies(end_ref, zeros, xb_ref, sems, ne):
    return [pltpu.make_async_copy(zeros, xb_ref.at[pl.ds(pl.multiple_of(end_ref[e], BF16_ROWS), MOE_BLK)], sems.at[e])
            for e in range(ne)]


def _dispatch_kernel(seg_ref, base_ref, off_ref, phase_ref, end_ref, rt_ref, h_ref, xb_ref, res, zeros, partial,
                     sems, tail_sems):
    i = pl.program_id(0)
    nt = pl.num_programs(0)
    ne = tail_sems.shape[0]
    tm = h_ref.shape[0]
    mrows = res.shape[1]
    slot = i % 2
    copies = functools.partial(_segment_copies, seg_ref, xb_ref, base_ref, res, off_ref, sems, ne=ne, to_grouped=True)

    @pl.when(i == 0)
    def _():
        zeros[...] = jnp.zeros_like(zeros)
        partial[...] = jnp.zeros_like(partial)
        for cp in _tail_copies(end_ref, zeros, xb_ref, tail_sems, ne):
            cp.start()

    row = lax.broadcasted_iota(I32, (mrows, tm), 0)
    pos = rt_ref[0:2, :].astype(I32)
    perm = jnp.where((row == pos[0:1, :]) | (row == pos[1:2, :]), 1.0, 0.0).astype(BF16)
    res[slot] = _dot(perm, h_ref[...]).astype(BF16)

    tile_row = lax.broadcasted_iota(I32, (BF16_ROWS, res.shape[2]), 0)
    for e in range(ne):
        @pl.when(seg_ref[i * ne + e] > 0)
        def _():
            first = pl.ds(pl.multiple_of(off_ref[i * ne + e], BF16_ROWS), BF16_ROWS)
            merged = jnp.where(tile_row < phase_ref[i * ne + e], partial[e].astype(F32), res[slot, first, :].astype(F32))
            res[slot, first, :] = merged.astype(BF16)
            last = pl.ds(pl.multiple_of(off_ref[i * ne + e] + seg_ref[i * ne + e] - BF16_ROWS, BF16_ROWS), BF16_ROWS)
            partial[e] = res[slot, last, :]

    @pl.when(i > 0)
    def _():
        copies(tile=i - 1, slot=1 - slot, wait=True)

    copies(tile=i, slot=slot, wait=False)

    @pl.when(i == nt - 1)
    def _():
        copies(tile=i, slot=slot, wait=True)
        for cp in _tail_copies(end_ref, zeros, xb_ref, tail_sems, ne):
            cp.wait()


def _dispatch(h, rt_lane, lay, ne):
    n, dm = h.shape
    tm = MOE_TILE
    return pl.pallas_call(
        _dispatch_kernel,
        grid_spec=pltpu.PrefetchScalarGridSpec(
            num_scalar_prefetch=5, grid=(n // tm,),
            in_specs=[pl.BlockSpec((8, tm), lambda i, *_: (0, i)), pl.BlockSpec((tm, dm), lambda i, *_: (i, 0))],
            out_specs=pl.BlockSpec(memory_space=pl.ANY),
            scratch_shapes=[pltpu.VMEM((2, _tile_rows(ne), dm), BF16), pltpu.VMEM((MOE_BLK, dm), BF16),
                            pltpu.VMEM((ne, BF16_ROWS, dm), BF16),
                            pltpu.SemaphoreType.DMA((2, ne)), pltpu.SemaphoreType.DMA((ne,))]),
        out_shape=jax.ShapeDtypeStruct((lay['rows_total'], dm), BF16),
        compiler_params=_cparams("arbitrary"),
        name="moe_dispatch",
    )(lay['seg'], lay['base'], lay['off'], lay['phase'], lay['end'], rt_lane, h)


def _combine_kernel(seg_ref, base_ref, off_ref, rt_ref, x_ref, gn_ref, yb_ref, o_ref, got, sems, *, ne):
    i = pl.program_id(0)
    nt = pl.num_programs(0)
    tm = x_ref.shape[0]
    mrows = got.shape[1]
    slot = i % 2
    copies = functools.partial(_segment_copies, seg_ref, yb_ref, base_ref, got, off_ref, sems, ne=ne, to_grouped=False)

    @pl.when(i == 0)
    def _():
        got[...] = jnp.zeros_like(got)
        copies(tile=i, slot=slot, wait=False)

    @pl.when(i + 1 < nt)
    def _():
        copies(tile=i + 1, slot=1 - slot, wait=False)

    copies(tile=i, slot=slot, wait=True)

    col = lax.broadcasted_iota(I32, (tm, mrows), 1)
    rt = rt_ref[...]
    rows = got[slot]
    sel = jnp.concatenate([jnp.where(col == rt[:, k:k + 1].astype(I32), 1.0, 0.0).astype(BF16) for k in range(TOP_K)],
                          axis=0)
    picked = _dot(sel, rows)
    y = sum(rt[:, TOP_K + k:TOP_K + k + 1] * picked[k * tm:(k + 1) * tm] for k in range(TOP_K))
    o_ref[...] = _rms(x_ref[...] + y, gn_ref[...])


def _combine(yb, rt_tok, x2, gn, lay, ne):
    n, dm = x2.shape
    tm = MOE_TILE
    mrows = _tile_rows(ne)
    tok = lambda w: pl.BlockSpec((tm, w), lambda i, *_: (i, 0))
    return pl.pallas_call(
        functools.partial(_combine_kernel, ne=ne),
        grid_spec=pltpu.PrefetchScalarGridSpec(
            num_scalar_prefetch=3, grid=(n // tm,),
            in_specs=[tok(LANES), tok(dm), pl.BlockSpec((1, dm), lambda i, *_: (0, 0)),
                      pl.BlockSpec(memory_space=pl.ANY)],
            out_specs=tok(dm),
            scratch_shapes=[pltpu.VMEM((2, mrows, dm), BF16), pltpu.SemaphoreType.DMA((2, ne))]),
        out_shape=jax.ShapeDtypeStruct((n, dm), F32),
        compiler_params=_cparams("arbitrary"),
        name="moe_combine",
    )(lay['seg'], lay['base'], lay['off'], rt_tok, x2, gn.reshape(1, dm), yb)


def _moe_layout(cnt, n):
    nt, ne = cnt.shape
    blk = MOE_BLK
    whole = lambda a: a // BF16_ROWS * BF16_ROWS
    cap = -(-(n + BF16_ROWS + blk) // blk) * blk
    region = jnp.arange(ne, dtype=I32) * cap
    filled = jnp.cumsum(cnt, axis=0) - cnt
    phase = filled - whole(filled)
    span = jnp.where(cnt > 0, whole(phase + cnt + BF16_ROWS - 1), 0)
    base = (region[None, :] + filled - phase).astype(I32)
    off = (jnp.cumsum(span, axis=1) - span).astype(I32)
    rows_e = jnp.sum(cnt, axis=0)
    nblk_e = (rows_e + blk - 1) // blk
    end = (region + whole(rows_e + BF16_ROWS - 1)).astype(I32)
    ends = jnp.cumsum(nblk_e)
    max_blocks = TOP_K * n // blk + ne
    bid = jnp.arange(max_blocks + 1, dtype=I32)
    bexp = jnp.minimum(jnp.sum((bid[:, None] >= ends[None, :]).astype(I32), axis=1), ne - 1)
    brow = bexp * (cap // blk) + bid - (ends - nblk_e)[bexp]
    nblk = ends[-1:].astype(I32)
    last = jnp.maximum(nblk[0] - 1, 0)
    used = bid < nblk[0]
    bexp = jnp.where(used, bexp, bexp[last]).astype(I32)
    brow = jnp.where(used, brow, brow[last]).astype(I32)
    flat = lambda a: a.reshape(-1)
    return dict(seg=flat(span.astype(I32)), base=flat(base), off=flat(off), phase=flat(phase.astype(I32)), end=end,
                brow=brow, bexp=bexp, nblk=nblk, rows_total=ne * cap, max_blocks=max_blocks)


def _moe(x2, h, logits, gn, w1, w3, w2):
    n, dm = x2.shape
    ne = w1.shape[0]
    rt_lane, rt_tok, cnt = _route(logits, ne)
    lay = _moe_layout(cnt[:, :, 0], n)
    xb = _dispatch(h, rt_lane, lay, ne)
    yb = _expert_ffn(xb, w1.astype(BF16), w3.astype(BF16), w2.astype(BF16), lay['brow'], lay['bexp'], lay['nblk'],
                     lay['max_blocks'])
    return _combine(yb, rt_tok, x2, gn, lay, ne)


def kernel(x, norm_mix_g, w_in, attn_rel_bias, rg_conv_w, rg_conv_b, rg_wx, rg_bx, rg_wa, rg_ba, rg_lambda, s5_a_re, s5_a_im, s5_log_dt, s5_b_re, s5_b_im, s5_c_re, s5_c_im, s5_d, s5_w_glu, g_group, w_out, norm_ffn_g, ffn_w1, ffn_w3, ffn_w2, moe_router, moe_w1, moe_w3, moe_w2, final_norm_g):
    bsz, l, dm = x.shape
    depth = w_in.shape[0]
    assert depth == 2 and l % ATTN_TQ == 0 and ATTN_TQ == N_PREV * CHUNK, "dense layer, then the MoE layer"
    d_rg = rg_conv_w.shape[2]
    d_s5 = s5_w_glu.shape[1]
    d_attn = (w_in.shape[2] - 2 * d_rg - d_s5) // 3
    x2 = x.reshape(bsz * l, dm)
    for layer in range(depth):
        q, k, v, xr, gate, us = _in_proj(x2, norm_mix_g[layer], w_in, layer, d_attn, d_rg, d_s5)
        y_attn = _attention(q, k, v, _attn_bias(attn_rel_bias[layer]), bsz, l)
        w_gates = jnp.concatenate([_block_diag(rg_wx[layer]), _block_diag(rg_wa[layer])], axis=1).astype(BF16)
        b_gates = jnp.concatenate([rg_bx[layer], rg_ba[layer]]).astype(F32)
        y_rg = _rglru(xr, gate, rg_conv_w[layer].astype(F32), rg_conv_b[layer].astype(F32), w_gates, b_gates,
                      jax.nn.softplus(-rg_lambda[layer].astype(F32)), bsz, l)
        prm = _s5_params(s5_a_re[layer], s5_a_im[layer], s5_log_dt[layer], s5_b_re[layer], s5_b_im[layer],
                         s5_c_re[layer], s5_c_im[layer], s5_d[layer])
        y_s5 = _s5(us, prm, bsz, l)
        mixed = (y_attn, y_rg, y_s5, x2, g_group[layer].astype(F32), s5_w_glu, w_out, layer,
                 norm_ffn_g[layer].astype(F32))
        if layer == 0:
            x2, h = _out_proj(*mixed)
            x2 = _dense_ffn(h, x2, ffn_w1[0], ffn_w3[0], ffn_w2[0])
        else:
            x2, h, logits = _out_proj(*mixed, router=moe_router[0])
            x2 = _moe(x2, h, logits, final_norm_g.astype(F32), moe_w1[0], moe_w3[0], moe_w2[0])
    return x2.reshape(bsz, l, dm)
```

```python
import functools
import math

import jax
import jax.numpy as jnp
from jax import lax
from jax.experimental import pallas as pl
from jax.experimental.pallas import tpu as pltpu

F32 = jnp.float32
BF16 = jnp.bfloat16
I32 = jnp.int32

EPS = 1e-6
CHUNK = 64
N_PREV = 8
BAND = (N_PREV + 1) * CHUNK
REL_CLIP = 128
HEAD_DIM = 64
RG_C = 8.0
RG_CONV_WIDTH = 4
S5_GROUP_DIM = 16
S5_T = 16
TOP_K = 2
NEG_BIG = -1e30

LANES = 128
BF16_ROWS = 16
VMEM_LIMIT = 52 * 1024 * 1024

ROW_TILE = 512
ATTN_TQ = 512
RG_TL = 1024
RG_SEG = 8
MOE_TILE = 256
MOE_BLK = 256
ROUTE_SPAN = 2048
CAST_UNITS = 16
DENSE_VMEM_LIMIT = 58 * 1024 * 1024
SEG_PIECES = tuple(BF16_ROWS << b for b in range((MOE_TILE // BF16_ROWS).bit_length() - 1, -1, -1))
FFN_CHUNKS = ((0, 512), (512, 512), (1024, 512), (1536, 512), (2048, 512), (2560, 256))


def _cparams(*sem):
    return pltpu.CompilerParams(dimension_semantics=sem, vmem_limit_bytes=VMEM_LIMIT)


def _const_spec(shape):
    nd = len(shape)
    return pl.BlockSpec(shape, lambda *_: (0,) * nd, pipeline_mode=pl.Buffered(1))


def _rms(xf, g):
    var = jnp.mean(xf * xf, axis=-1, keepdims=True)
    return xf * lax.rsqrt(var + EPS) * g


def _sigmoid(x):
    return 1.0 / (1.0 + jnp.exp(-x))


def _gelu(x):
    c = math.sqrt(2.0 / math.pi)
    return 0.5 * x * (1.0 + jnp.tanh(c * (x + 0.044715 * (x * x * x))))


def _dot(a, b):
    return jnp.dot(a, b, preferred_element_type=F32)


def _in_proj_kernel(x_ref, g_ref, w_ref, q_ref, k_ref, v_ref, xr_ref, gt_ref, us_ref, *, d_attn, d_rg):
    u = _rms(x_ref[...], g_ref[...]).astype(BF16)
    lo = 0
    for ref, width in ((q_ref, d_attn), (k_ref, d_attn), (v_ref, d_attn),
                       (xr_ref, d_rg), (gt_ref, d_rg), (us_ref, w_ref.shape[1] - 3 * d_attn - 2 * d_rg)):
        ref[...] = _dot(u, w_ref[:, lo:lo + width].astype(BF16)).astype(ref.dtype)
        lo += width


def _layer_spec(stacked, layer):
    return pl.BlockSpec((None,) + stacked.shape[1:], lambda *_: (layer, 0, 0), pipeline_mode=pl.Buffered(1))


def _in_proj(x2, g, w_all, layer, d_attn, d_rg, d_s5):
    n, dm = x2.shape
    tm = min(ROW_TILE, n)
    row = lambda w: pl.BlockSpec((tm, w), lambda i: (i, 0))
    return pl.pallas_call(
        functools.partial(_in_proj_kernel, d_attn=d_attn, d_rg=d_rg),
        grid=(n // tm,),
        in_specs=[row(dm), _const_spec((1, dm)), _layer_spec(w_all, layer)],
        out_specs=[row(d_attn), row(d_attn), row(d_attn), row(d_rg), row(d_rg), row(d_s5)],
        out_shape=[jax.ShapeDtypeStruct((n, d_attn), BF16)] * 3
        + [jax.ShapeDtypeStruct((n, d_rg), F32)] * 2 + [jax.ShapeDtypeStruct((n, d_s5), F32)],
        compiler_params=_cparams("parallel"),
        name="in_proj",
    )(x2, g.reshape(1, dm), w_all)


def _attn_kernel(q_ref, kp_ref, kc_ref, vp_ref, vc_ref, bias_ref, o_ref, kz, vz, *, n_pairs):
    qi = pl.program_id(1)
    tq = q_ref.shape[0]
    kz[0:tq, :] = kp_ref[...]
    kz[tq:2 * tq, :] = kc_ref[...]
    vz[0:tq, :] = vp_ref[...]
    vz[tq:2 * tq, :] = vc_ref[...]
    lane = lax.broadcasted_iota(I32, (CHUNK, LANES), 1)
    first = lane < HEAD_DIM
    kpos = lax.broadcasted_iota(I32, (1, BAND), 1)
    pairs = [slice(hp * LANES, (hp + 1) * LANES) for hp in range(n_pairs)]

    def chunk_body(c, carry, *, masked):
        r0 = pl.multiple_of(c * CHUNK, CHUNK)
        scores = []
        for hp, cols in enumerate(pairs):
            q2 = q_ref[pl.ds(r0, CHUNK), cols] * jnp.asarray(HEAD_DIM ** -0.5, BF16)
            zero = jnp.zeros_like(q2)
            qq = jnp.concatenate([jnp.where(first, q2, zero), jnp.where(first, zero, q2)], axis=0)
            s = lax.dot_general(qq, kz[pl.ds(r0, BAND), cols], (((1,), (1,)), ((), ())), preferred_element_type=F32)
            scores.append(s + bias_ref[hp])
        if masked:
            neg = jnp.where(kpos >= N_PREV * CHUNK - c * CHUNK, 0.0, NEG_BIG)
            scores = [s + neg for s in scores]
        probs, sums = [], []
        for s in scores:
            p = jnp.exp(s - jnp.max(s, axis=-1, keepdims=True))
            sums.append(jnp.sum(p, axis=-1, keepdims=True))
            probs.append(p.astype(BF16))
        for cols, p, l in zip(pairs, probs, sums):
            o2 = _dot(p, vz[pl.ds(r0, BAND), cols]) / l
            o_ref[pl.ds(r0, CHUNK), cols] = jnp.where(first, o2[0:CHUNK], o2[CHUNK:2 * CHUNK])
        return carry

    @pl.when(qi == 0)
    def _():
        lax.fori_loop(0, tq // CHUNK, functools.partial(chunk_body, masked=True), 0)

    @pl.when(qi != 0)
    def _():
        lax.fori_loop(0, tq // CHUNK, functools.partial(chunk_body, masked=False), 0, unroll=2)


def _attention(q, k, v, bias2, bsz, l):
    n, da = q.shape
    tq = ATTN_TQ
    nt = l // tq
    n_pairs = da // LANES
    cur = pl.BlockSpec((tq, da), lambda b, i: (b * nt + i, 0))
    prev = pl.BlockSpec((tq, da), lambda b, i: (b * nt + jnp.maximum(i - 1, 0), 0))
    return pl.pallas_call(
        functools.partial(_attn_kernel, n_pairs=n_pairs),
        grid=(bsz, nt),
        in_specs=[cur, prev, cur, prev, cur, _const_spec(bias2.shape)],
        out_specs=cur,
        out_shape=jax.ShapeDtypeStruct((n, da), F32),
        scratch_shapes=[pltpu.VMEM((2 * tq, da), BF16), pltpu.VMEM((2 * tq, da), BF16)],
        compiler_params=_cparams("parallel", "parallel"),
        name="chunk_attn",
    )(q, k, k, v, v, bias2)


def _attn_bias(rel_bias):
    h = rel_bias.shape[0]
    tab = rel_bias.astype(F32)
    n_far = N_PREV * CHUNK - REL_CLIP + CHUNK
    lo = 2 * REL_CLIP - (BAND + CHUNK - 1 - n_far)
    ext = jnp.concatenate([jnp.broadcast_to(tab[:, 2 * REL_CLIP:], (h, n_far)), tab[:, lo:2 * REL_CLIP][:, ::-1]], axis=1)
    wide = BAND + CHUNK
    ring = jnp.concatenate([ext[:, CHUNK - 1:], jnp.zeros((h, 1), F32), ext[:, :CHUNK - 1]], axis=1)
    bias = jnp.broadcast_to(ring[:, None, :], (h, CHUNK, wide)).reshape(h, CHUNK * wide)
    bias = bias[:, :CHUNK * (wide - 1)].reshape(h, CHUNK, wide - 1)[..., :BAND]
    return bias.reshape(h // 2, 2 * CHUNK, BAND)


def _rglru_kernel(x_ref, gt_ref, cw_ref, cb_ref, w_ref, b_ref, sp_ref, o_ref, xpad, a_s, b_s, hcar):
    t = pl.program_id(1)
    tl, c = x_ref.shape
    seg = tl // RG_SEG
    front = 8

    @pl.when(t == 0)
    def _():
        xpad[0:front, :] = jnp.zeros((front, c), F32)
        hcar[...] = jnp.zeros_like(hcar)

    xpad[front:front + tl, :] = x_ref[...]
    xc = cb_ref[...] + sum(
        cw_ref[j:j + 1, :] * xpad[front - (RG_CONV_WIDTH - 1) + j:front - (RG_CONV_WIDTH - 1) + j + tl, :]
        for j in range(RG_CONV_WIDTH))
    xpad[0:front, :] = xpad[tl:tl + front, :]
    pre = _dot(xc.astype(BF16), w_ref[...]) + b_ref[...]
    gx = _sigmoid(pre[:, 0:c])
    ga = _sigmoid(pre[:, c:2 * c])
    log_a = -RG_C * ga * sp_ref[...]
    a = jnp.exp(log_a)
    mult = jnp.sqrt(-jnp.tanh(log_a) * (a * a + 1.0))
    b = mult * gx * xc
    nh = c // LANES
    for j in range(nh):
        a_s[j] = a[:, j * LANES:(j + 1) * LANES]
        b_s[j] = b[:, j * LANES:(j + 1) * LANES]

    def step(i, carry):
        rows = pl.ds(i, RG_SEG, stride=seg)
        out = []
        for j in range(nh):
            h, p = carry[j]
            av = a_s[j, rows, :]
            h = av * h + b_s[j, rows, :]
            p = av * p
            b_s[j, rows, :] = h
            a_s[j, rows, :] = p
            out.append((h, p))
        return tuple(out)

    init = tuple((jnp.zeros((RG_SEG, LANES), F32), jnp.ones((RG_SEG, LANES), F32)) for _ in range(nh))
    ends = lax.fori_loop(0, seg, step, init, unroll=8)
    for j in range(nh):
        cols = slice(j * LANES, (j + 1) * LANES)
        h_end, p_end = ends[j]
        carry = hcar[:, cols]
        for s in range(RG_SEG):
            rows = slice(s * seg, (s + 1) * seg)
            h = b_s[j, rows, :] + a_s[j, rows, :] * carry
            o_ref[rows, cols] = h * _gelu(gt_ref[rows, cols])
            carry = h_end[s:s + 1, :] + p_end[s:s + 1, :] * carry
        hcar[:, cols] = carry


def _rglru(xr, gate, conv_w, conv_b, w_bd, b_cat, sp, bsz, l):
    n, c = xr.shape
    tl = min(RG_TL, l)
    nt = l // tl
    row = pl.BlockSpec((tl, c), lambda b, t: (b * nt + t, 0))
    return pl.pallas_call(
        _rglru_kernel,
        grid=(bsz, nt),
        in_specs=[row, row, _const_spec(conv_w.shape), _const_spec((1, c)), _const_spec(w_bd.shape),
                  _const_spec((1, 2 * c)), _const_spec((1, c))],
        out_specs=row,
        out_shape=jax.ShapeDtypeStruct((n, c), F32),
        scratch_shapes=[pltpu.VMEM((tl + 8, c), F32), pltpu.VMEM((c // LANES, tl, LANES), F32),
                        pltpu.VMEM((c // LANES, tl, LANES), F32), pltpu.VMEM((1, c), F32)],
        compiler_params=_cparams("parallel", "arbitrary"),
        name="rglru",
    )(xr, gate, conv_w, conv_b.reshape(1, c), w_bd, b_cat.reshape(1, 2 * c), sp.reshape(1, c))


def _block_diag(w):
    nb, d, _ = w.shape
    eye = jnp.eye(nb, dtype=w.dtype)
    return (eye[:, None, :, None] * w[:, :, None, :]).reshape(nb * d, nb * d)


def _s5_kernel(u_ref, m_ref, wa_ref, wb_ref, wo_ref, c1_ref, c2a_ref, c2b_ref, d_ref, y_ref, ps, ug, xa_s, xb_s):
    t, gd = S5_T, S5_GROUP_DIM
    ng, nk, _ = ug.shape
    nh = ps.shape[0]
    per_half = LANES // gd
    rows8 = 8

    for h in range(nh):
        ps[h] = u_ref[:, h * LANES:(h + 1) * LANES]

    lane_block = lax.broadcasted_iota(I32, (rows8, LANES), 1) // gd

    def block_transpose(vs):
        d = per_half // 2
        while d:
            keep = (lane_block & d) == 0
            out = list(vs)
            for i in range(per_half):
                if i & d == 0:
                    out[i] = jnp.where(keep, vs[i], pltpu.roll(vs[i + d], d * gd, 1))
                    out[i + d] = jnp.where(keep, pltpu.roll(vs[i], LANES - d * gd, 1), vs[i + d])
            vs, d = out, d // 2
        return vs

    def to_groups(r, carry):
        base = pl.multiple_of(r * rows8 * t, rows8 * t)
        rows = pl.ds(pl.multiple_of(r * rows8, rows8), rows8)
        for h in range(nh):
            for j in range(t // per_half):
                steps = [ps[h, pl.ds(base + j * per_half + i, rows8, stride=t), :] for i in range(per_half)]
                for k, v in enumerate(block_transpose(steps)):
                    ug[h * per_half + k, rows, j * LANES:(j + 1) * LANES] = v
        return carry

    lax.fori_loop(0, nk // rows8, to_groups, 0, unroll=2)

    def project(g, carry):
        u = ug[g]
        ub = u.astype(BF16)
        rows = pl.ds(pl.multiple_of(g * nk, nk), nk)
        xa_s[rows, :] = _dot(ub, wa_ref[g])
        xb_s[rows, :] = _dot(ub, wb_ref[g])
        ug[g] = _dot(ub, m_ref[g]) + d_ref[g] * u
        return carry

    lax.fori_loop(0, ng, project, 0)

    c1, c2a, c2b = c1_ref[...], c2a_ref[...], c2b_ref[...]

    def step(k, carry):
        xa, xb = carry
        rows = pl.ds(k, ng, stride=nk)
        ia = xa_s[rows, :]
        ib = xb_s[rows, :]
        xa_s[rows, :] = xa
        return c1 * xa + c2a * xb + ia, c1 * xb + c2b * xa + ib

    z = jnp.zeros((ng, xa_s.shape[1]), F32)
    lax.fori_loop(0, nk, step, (z, z), unroll=4)

    def respond(g, carry):
        rows = pl.ds(pl.multiple_of(g * nk, nk), nk)
        ug[g] = _gelu(ug[g] + _dot(xa_s[rows, :].astype(BF16), wo_ref[g]))
        return carry

    lax.fori_loop(0, ng, respond, 0)

    def from_groups(r, carry):
        base = pl.multiple_of(r * rows8 * t, rows8 * t)
        rows = pl.ds(pl.multiple_of(r * rows8, rows8), rows8)
        for h in range(nh):
            for j in range(t // per_half):
                groups = [ug[h * per_half + k, rows, j * LANES:(j + 1) * LANES] for k in range(per_half)]
                for i, v in enumerate(block_transpose(groups)):
                    ps[h, pl.ds(base + j * per_half + i, rows8, stride=t), :] = v
        return carry

    lax.fori_loop(0, nk // rows8, from_groups, 0, unroll=2)
    for h in range(nh):
        y_ref[:, h * LANES:(h + 1) * LANES] = ps[h]


def _s5(us, prm, bsz, l):
    n, w = us.shape
    ng = w // S5_GROUP_DIM
    nk = l // S5_T
    p2 = prm[1].shape[2]
    row = pl.BlockSpec((l, w), lambda b: (b, 0))
    return pl.pallas_call(
        _s5_kernel,
        grid=(bsz,),
        in_specs=[row] + [_const_spec(a.shape) for a in prm],
        out_specs=row,
        out_shape=jax.ShapeDtypeStruct((n, w), F32),
        scratch_shapes=[pltpu.VMEM((w // LANES, l, LANES), F32), pltpu.VMEM((ng, nk, S5_T * S5_GROUP_DIM), F32),
                        pltpu.VMEM((ng * nk, p2), F32), pltpu.VMEM((ng * nk, p2), F32)],
        compiler_params=_cparams("parallel"),
        name="s5",
    )(us, *prm)


def _s5_params(a_re, a_im, log_dt, b_re, b_im, c_re, c_im, d):
    t = S5_T
    g, p = a_re.shape
    gd = S5_GROUP_DIM
    hi = lax.Precision.HIGHEST
    a = lax.complex(a_re.astype(F32), a_im.astype(F32))
    dt = jnp.exp(log_dt.astype(F32))[:, None]
    a_bar = jnp.exp(a * dt)
    bm = lax.complex(b_re.astype(F32), b_im.astype(F32))
    cm = lax.complex(c_re.astype(F32), c_im.astype(F32))
    b_bar = ((a_bar - 1.0) / a)[..., None] * bm
    steps = jnp.arange(t + 1, dtype=F32)
    pw = jnp.exp((a * dt)[:, None, :] * steps[None, :, None])

    cp = (cm[:, None, :, :] * pw[:, :, None, :]).transpose(0, 3, 1, 2)
    w = t * gd
    lag = cp[:, :, :t].reshape(g, p, w)
    k2 = jnp.einsum('gpc,gpx->gcx', jnp.concatenate([b_bar.real, -b_bar.imag], axis=1),
                    jnp.concatenate([lag.real, lag.imag], axis=1), precision=hi)
    skew = jnp.broadcast_to(jnp.concatenate([k2, jnp.zeros_like(k2)], axis=-1)[:, :, None, :], (g, gd, t, 2 * w))
    skew = skew.reshape(g, gd, t * 2 * w)[:, :, :t * (2 * w - gd)].reshape(g, gd, t, 2 * w - gd)[..., :w]
    m = skew.transpose(0, 2, 1, 3).reshape(g, w, w)
    win = pw[:, :t][:, ::-1][:, :, :, None] * b_bar[:, None]
    win = win.transpose(0, 1, 3, 2).reshape(g, w, p)
    wa = jnp.concatenate([win.real, win.imag], axis=-1)
    wb = jnp.concatenate([win.imag, win.real], axis=-1)
    co = cp[:, :, 1:].reshape(g, p, w)
    wo = jnp.concatenate([co.real, -co.imag], axis=1)
    lt = pw[:, t]
    c1 = jnp.concatenate([lt.real, lt.real], axis=-1)
    c2a = jnp.concatenate([-lt.imag, lt.imag], axis=-1)
    c2b = jnp.concatenate([lt.imag, -lt.imag], axis=-1)
    dtile = jnp.tile(d.astype(F32).reshape(g, 1, gd), (1, t, 1)).reshape(g, 1, t * gd)
    return (m.astype(BF16), wa.astype(BF16), wb.astype(BF16), wo.astype(BF16), c1, c2a, c2b, dtile)


def _out_proj_kernel(ya_ref, yr_ref, ys_ref, x_ref, gg_ref, wglu_ref, wo_ref, gf_ref, *rest, with_router):
    if with_router:
        rt_ref, xo_ref, h_ref, lg_ref = rest
    else:
        xo_ref, h_ref = rest
    ys = ys_ref[...]
    ys = ys * _sigmoid(_dot(ys.astype(BF16), wglu_ref[...].astype(BF16)))
    acc = x_ref[...]
    lo = 0
    for y in (ya_ref[...], yr_ref[...], ys):
        w = y.shape[1]
        acc = acc + _dot(_rms(y, gg_ref[:, lo:lo + w]).astype(BF16), wo_ref[lo:lo + w, :].astype(BF16))
        lo += w
    xo_ref[...] = acc
    h = _rms(acc, gf_ref[...])
    hb = h.astype(BF16)
    h_ref[...] = hb
    if with_router:
        both = _dot(hb, rt_ref[...])
        h_lo = (h - hb.astype(F32)).astype(BF16)
        lg_ref[...] = both[:, 0:LANES] + both[:, LANES:2 * LANES] + _dot(h_lo, rt_ref[:, 0:LANES])


def _out_proj(ya, yr, ys, x2, gg, wglu_all, wo_all, layer, gf, router=None):
    n, dm = x2.shape
    tm = min(ROW_TILE, n)
    row = lambda w: pl.BlockSpec((tm, w), lambda i: (i, 0))
    ins = [ya, yr, ys, x2, gg.reshape(1, -1), wglu_all, wo_all, gf.reshape(1, dm)]
    in_specs = [row(ya.shape[1]), row(yr.shape[1]), row(ys.shape[1]), row(dm), _const_spec((1, gg.shape[0])),
                _layer_spec(wglu_all, layer), _layer_spec(wo_all, layer), _const_spec((1, dm))]
    out_specs = [row(dm), row(dm)]
    out_shape = [jax.ShapeDtypeStruct((n, dm), F32), jax.ShapeDtypeStruct((n, dm), BF16)]
    if router is not None:
        ne = router.shape[1]
        r_hi = router.astype(BF16)
        r_lo = (router.astype(F32) - r_hi.astype(F32)).astype(BF16)
        pad = lambda a: jnp.pad(a, ((0, 0), (0, LANES - ne)))
        ins.append(jnp.concatenate([pad(r_hi), pad(r_lo)], axis=1))
        in_specs.append(_const_spec((dm, 2 * LANES)))
        out_specs.append(row(LANES))
        out_shape.append(jax.ShapeDtypeStruct((n, LANES), F32))
    return pl.pallas_call(
        functools.partial(_out_proj_kernel, with_router=router is not None),
        grid=(n // tm,),
        in_specs=in_specs, out_specs=out_specs, out_shape=out_shape,
        compiler_params=_cparams("parallel"),
        name="out_proj",
    )(*ins)


def _swiglu_chunks(h, w1, w3, w2, chunks):
    acc = None
    for lo, width in chunks:
        a = _dot(h, w1[:, lo:lo + width].astype(BF16))
        b = _dot(h, w3[:, lo:lo + width].astype(BF16))
        t = (a * _sigmoid(a) * b).astype(BF16)
        y = _dot(t, w2[lo:lo + width, :].astype(BF16))
        acc = y if acc is None else acc + y
    return acc


def _ffn_chunks(f):
    if f == sum(w for _, w in FFN_CHUNKS):
        return FFN_CHUNKS
    return ((0, f),)


def _cast_copies(srcs, dsts, stage_in, stage_out, sems, unit, slot, inbound):
    out = []
    for m, (src, dst) in enumerate(zip(srcs, dsts)):
        rows = src.shape[1] // CAST_UNITS
        where = (unit // CAST_UNITS, pl.ds(pl.multiple_of((unit % CAST_UNITS) * rows, BF16_ROWS), rows))
        a, b = (src.at[where], stage_in[m].at[slot]) if inbound else (stage_out[m].at[slot], dst.at[where])
        out.append(pltpu.make_async_copy(a, b, sems.at[int(inbound), slot, m]))
    return out


def _dense_ffn_kernel(h_ref, x_ref, w1_ref, w3_ref, w2_ref, e1_ref, e3_ref, e2_ref, o_ref, c1_ref, c3_ref, c2_ref,
                      si1, si3, si2, so1, so3, so2, sems, *, chunks, per_step):
    i = pl.program_id(0)
    last = pl.num_programs(0) - 1
    copies = functools.partial(_cast_copies, (e1_ref, e3_ref, e2_ref), (c1_ref, c3_ref, c2_ref),
                               (si1, si3, si2), (so1, so3, so2), sems)
    def go(cps):
        for cp in cps:
            cp.start()

    def done(cps):
        for cp in cps:
            cp.wait()

    @pl.when(i == 0)
    def _():
        go(copies(0, 0, True))

    for k in range(per_step):
        unit, slot = i * per_step + k, k % 2
        done(copies(unit, slot, True))
        if k + 1 < per_step:
            go(copies(unit + 1, 1 - slot, True))
        else:
            pl.when(i < last)(lambda: go(copies(unit + 1, 1 - slot, True)))
        if k >= 2:
            done(copies(unit - 2, slot, False))
        else:
            pl.when(i > 0)(lambda: done(copies(unit - 2, slot, False)))
        for s_in, s_out in ((si1, so1), (si3, so3), (si2, so2)):
            s_out[slot] = s_in[slot].astype(BF16)
        go(copies(unit, slot, False))

    o_ref[...] = x_ref[...] + _swiglu_chunks(h_ref[...], w1_ref, w3_ref, w2_ref, chunks)

    @pl.when(i == last)
    def _():
        for k in (per_step - 2, per_step - 1):
            done(copies(i * per_step + k, k % 2, False))


def _dense_ffn(h, x2, w1, w3, w2, e1, e3, e2):
    n, dm = x2.shape
    tm = min(ROW_TILE, n)
    steps = n // tm
    ne = e1.shape[0]
    per_step = ne * CAST_UNITS // steps
    assert per_step * steps == ne * CAST_UNITS and per_step % 2 == 0
    row = pl.BlockSpec((tm, dm), lambda i: (i, 0))
    hbm = pl.BlockSpec(memory_space=pl.ANY)
    slab = lambda e, dt: pltpu.VMEM((2, e.shape[1] // CAST_UNITS, e.shape[2]), dt)
    experts = (e1, e3, e2)
    return pl.pallas_call(
        functools.partial(_dense_ffn_kernel, chunks=_ffn_chunks(w1.shape[1]), per_step=per_step),
        grid=(steps,),
        in_specs=[row, row, _const_spec(w1.shape), _const_spec(w3.shape), _const_spec(w2.shape), hbm, hbm, hbm],
        out_specs=[row, hbm, hbm, hbm],
        out_shape=[jax.ShapeDtypeStruct((n, dm), F32)] + [jax.ShapeDtypeStruct(e.shape, BF16) for e in experts],
        scratch_shapes=[slab(e, F32) for e in experts] + [slab(e, BF16) for e in experts]
        + [pltpu.SemaphoreType.DMA((2, 2, 3))],
        compiler_params=pltpu.CompilerParams(dimension_semantics=("arbitrary",), vmem_limit_bytes=DENSE_VMEM_LIMIT),
        name="dense_ffn",
    )(h, x2, w1, w3, w2, e1, e3, e2)


def _expert_ffn_kernel(brow_ref, bexp_ref, nblk_ref, x_ref, w1_ref, w3_ref, w2_ref, o_ref, *, chunks):
    @pl.when(pl.program_id(0) < nblk_ref[0])
    def _():
        o_ref[...] = _swiglu_chunks(x_ref[...], w1_ref.at[0], w3_ref.at[0], w2_ref.at[0], chunks).astype(o_ref.dtype)


def _expert_ffn(xb, w1, w3, w2, blk_row, blk_exp, nblk, max_blocks):
    rows, dm = xb.shape
    f = w1.shape[2]
    row = pl.BlockSpec((MOE_BLK, dm), lambda i, br, be, nb: (br[i], 0))
    wspec = lambda a: pl.BlockSpec((1,) + a.shape[1:], lambda i, br, be, nb: (be[i], 0, 0))
    return pl.pallas_call(
        functools.partial(_expert_ffn_kernel, chunks=_ffn_chunks(f)),
        grid_spec=pltpu.PrefetchScalarGridSpec(
            num_scalar_prefetch=3, grid=(max_blocks,),
            in_specs=[row, wspec(w1), wspec(w3), wspec(w2)], out_specs=row),
        out_shape=jax.ShapeDtypeStruct((rows, dm), BF16),
        compiler_params=_cparams("arbitrary"),
        name="expert_ffn",
    )(blk_row, blk_exp, nblk, xb, w1, w3, w2)


def _route_tile(logits_tok, filled, ne):
    tm = logits_tok.shape[0]
    logits = logits_tok.T[0:ne, :]
    eidx = lax.broadcasted_iota(I32, (ne, tm), 0)
    m1 = jnp.max(logits, axis=0, keepdims=True)
    i1 = jnp.min(jnp.where(logits == m1, eidx, ne), axis=0, keepdims=True)
    sel1 = eidx == i1
    rest = jnp.where(sel1, -jnp.inf, logits)
    m2 = jnp.max(rest, axis=0, keepdims=True)
    i2 = jnp.min(jnp.where(rest == m2, eidx, ne), axis=0, keepdims=True)
    sel2 = eidx == i2
    e2 = jnp.exp(m2 - m1)
    den = 1.0 + e2
    rf = jnp.where(sel1 | sel2, 1.0, 0.0)
    cnt = jnp.sum(rf, axis=1, keepdims=True)
    before = (lax.broadcasted_iota(I32, (tm, tm), 0) < lax.broadcasted_iota(I32, (tm, tm), 1))
    rank = _dot(rf.astype(BF16), jnp.where(before, 1.0, 0.0).astype(BF16))
    whole = lambda a: jnp.floor(a * (1.0 / BF16_ROWS)) * BF16_ROWS
    phase = filled - whole(filled)
    span = jnp.where(cnt > 0, whole(phase + cnt + (BF16_ROWS - 1)), 0.0)
    ecol = lax.broadcasted_iota(I32, (ne, 1), 0)
    off = jnp.zeros((ne, 1), F32)
    for j in range(ne - 1):
        off = off + jnp.where(ecol > j, span[j:j + 1, :], 0.0)
    place = off + phase + rank
    pos1 = jnp.sum(jnp.where(sel1, place, 0.0), axis=0, keepdims=True)
    pos2 = jnp.sum(jnp.where(sel2, place, 0.0), axis=0, keepdims=True)
    lane_major = jnp.concatenate([pos1, pos2, 1.0 / den, e2 / den, jnp.zeros((4, tm), F32)], axis=0)
    tok_major = jnp.concatenate([lane_major, jnp.zeros((LANES - 8, tm), F32)], axis=0).T
    return lane_major, tok_major, cnt


def _route_kernel(lg_ref, lane_ref, tok_ref, cnt_ref, filled, *, ne):
    tm = MOE_TILE

    @pl.when(pl.program_id(0) == 0)
    def _():
        filled[...] = jnp.zeros_like(filled)

    for t in range(lg_ref.shape[0] // tm):
        lane_major, tok_major, cnt = _route_tile(lg_ref[t * tm:(t + 1) * tm, :], filled[...], ne)
        lane_ref[:, t * tm:(t + 1) * tm] = lane_major
        tok_ref[t * tm:(t + 1) * tm, :] = tok_major
        cnt_ref[t] = jnp.broadcast_to(cnt, (ne, LANES)).astype(I32)
        filled[...] = filled[...] + cnt


def _route(logits, ne):
    n = logits.shape[0]
    tm = MOE_TILE
    span = min(ROUTE_SPAN, n)
    return pl.pallas_call(
        functools.partial(_route_kernel, ne=ne),
        grid=(n // span,),
        in_specs=[pl.BlockSpec((span, LANES), lambda i: (i, 0))],
        out_specs=[pl.BlockSpec((8, span), lambda i: (0, i)), pl.BlockSpec((span, LANES), lambda i: (i, 0)),
                   pl.BlockSpec((span // tm, ne, LANES), lambda i: (i, 0, 0))],
        out_shape=[jax.ShapeDtypeStruct((8, n), F32), jax.ShapeDtypeStruct((n, LANES), F32),
                   jax.ShapeDtypeStruct((n // tm, ne, LANES), I32)],
        scratch_shapes=[pltpu.VMEM((ne, 1), F32)],
        compiler_params=_cparams("arbitrary"),
        name="moe_route",
    )(logits)


def _segment_copies(seg_ref, grouped_ref, base_ref, tile_buf, off_ref, sems, tile, slot, ne, *, to_grouped, wait):
    for e in range(ne):
        seg = seg_ref[tile * ne + e]
        far = base_ref[tile * ne + e]
        near = off_ref[tile * ne + e]
        for piece in SEG_PIECES:
            g = grouped_ref.at[pl.ds(pl.multiple_of(far, BF16_ROWS), piece)]
            t = tile_buf.at[slot, pl.ds(pl.multiple_of(near, BF16_ROWS), piece)]
            cp = pltpu.make_async_copy(t, g, sems.at[slot, e]) if to_grouped else \
                pltpu.make_async_copy(g, t, sems.at[slot, e])
            has = (seg & piece) != 0

            @pl.when(has)
            def _():
                cp.wait() if wait else cp.start()

            step = jnp.where(has, piece, 0)
            far = far + step
            near = near + step


def _tile_rows(ne):
    return TOP_K * MOE_TILE + ne * 2 * BF16_ROWS


def _tail_copies(end_ref, zeros, xb_ref, sems, ne):
    return [pltpu.make_async_copy(zeros, xb_ref.at[pl.ds(pl.multiple_of(end_ref[e], BF16_ROWS), MOE_BLK)], sems.at[e])
            for e in range(ne)]


def _dispatch_kernel(seg_ref, base_ref, off_ref, phase_ref, end_ref, rt_ref, h_ref, xb_ref, res, zeros, partial,
                     sems, tail_sems):
    i = pl.program_id(0)
    nt = pl.num_programs(0)
    ne = tail_sems.shape[0]
    tm = h_ref.shape[0]
    mrows = res.shape[1]
    slot = i % 2
    copies = functools.partial(_segment_copies, seg_ref, xb_ref, base_ref, res, off_ref, sems, ne=ne, to_grouped=True)

    @pl.when(i == 0)
    def _():
        zeros[...] = jnp.zeros_like(zeros)
        partial[...] = jnp.zeros_like(partial)
        for cp in _tail_copies(end_ref, zeros, xb_ref, tail_sems, ne):
            cp.start()

    row = lax.broadcasted_iota(I32, (mrows, tm), 0)
    pos = rt_ref[0:2, :].astype(I32)
    perm = jnp.where((row == pos[0:1, :]) | (row == pos[1:2, :]), 1.0, 0.0).astype(BF16)
    res[slot] = _dot(perm, h_ref[...]).astype(BF16)

    tile_row = lax.broadcasted_iota(I32, (BF16_ROWS, res.shape[2]), 0)
    for e in range(ne):
        @pl.when(seg_ref[i * ne + e] > 0)
        def _():
            first = pl.ds(pl.multiple_of(off_ref[i * ne + e], BF16_ROWS), BF16_ROWS)
            merged = jnp.where(tile_row < phase_ref[i * ne + e], partial[e].astype(F32), res[slot, first, :].astype(F32))
            res[slot, first, :] = merged.astype(BF16)
            last = pl.ds(pl.multiple_of(off_ref[i * ne + e] + seg_ref[i * ne + e] - BF16_ROWS, BF16_ROWS), BF16_ROWS)
            partial[e] = res[slot, last, :]

    @pl.when(i > 0)
    def _():
        copies(tile=i - 1, slot=1 - slot, wait=True)

    copies(tile=i, slot=slot, wait=False)

    @pl.when(i == nt - 1)
    def _():
        copies(tile=i, slot=slot, wait=True)
        for cp in _tail_copies(end_ref, zeros, xb_ref, tail_sems, ne):
            cp.wait()


def _dispatch(h, rt_lane, lay, ne):
    n, dm = h.shape
    tm = MOE_TILE
    return pl.pallas_call(
        _dispatch_kernel,
        grid_spec=pltpu.PrefetchScalarGridSpec(
            num_scalar_prefetch=5, grid=(n // tm,),
            in_specs=[pl.BlockSpec((8, tm), lambda i, *_: (0, i)), pl.BlockSpec((tm, dm), lambda i, *_: (i, 0))],
            out_specs=pl.BlockSpec(memory_space=pl.ANY),
            scratch_shapes=[pltpu.VMEM((2, _tile_rows(ne), dm), BF16), pltpu.VMEM((MOE_BLK, dm), BF16),
                            pltpu.VMEM((ne, BF16_ROWS, dm), BF16),
                            pltpu.SemaphoreType.DMA((2, ne)), pltpu.SemaphoreType.DMA((ne,))]),
        out_shape=jax.ShapeDtypeStruct((lay['rows_total'], dm), BF16),
        compiler_params=_cparams("arbitrary"),
        name="moe_dispatch",
    )(lay['seg'], lay['base'], lay['off'], lay['phase'], lay['end'], rt_lane, h)


def _combine_kernel(seg_ref, base_ref, off_ref, rt_ref, x_ref, gn_ref, yb_ref, o_ref, got, sems, *, ne):
    i = pl.program_id(0)
    nt = pl.num_programs(0)
    tm = x_ref.shape[0]
    mrows = got.shape[1]
    slot = i % 2
    copies = functools.partial(_segment_copies, seg_ref, yb_ref, base_ref, got, off_ref, sems, ne=ne, to_grouped=False)

    @pl.when(i == 0)
    def _():
        got[...] = jnp.zeros_like(got)
        copies(tile=i, slot=slot, wait=False)

    @pl.when(i + 1 < nt)
    def _():
        copies(tile=i + 1, slot=1 - slot, wait=False)

    copies(tile=i, slot=slot, wait=True)

    col = lax.broadcasted_iota(I32, (tm, mrows), 1)
    rt = rt_ref[...]
    rows = got[slot]
    sel = jnp.concatenate([jnp.where(col == rt[:, k:k + 1].astype(I32), 1.0, 0.0).astype(BF16) for k in range(TOP_K)],
                          axis=0)
    picked = _dot(sel, rows)
    y = sum(rt[:, TOP_K + k:TOP_K + k + 1] * picked[k * tm:(k + 1) * tm] for k in range(TOP_K))
    o_ref[...] = _rms(x_ref[...] + y, gn_ref[...])


def _combine(yb, rt_tok, x2, gn, lay, ne):
    n, dm = x2.shape
    tm = MOE_TILE
    mrows = _tile_rows(ne)
    tok = lambda w: pl.BlockSpec((tm, w), lambda i, *_: (i, 0))
    return pl.pallas_call(
        functools.partial(_combine_kernel, ne=ne),
        grid_spec=pltpu.PrefetchScalarGridSpec(
            num_scalar_prefetch=3, grid=(n // tm,),
            in_specs=[tok(LANES), tok(dm), pl.BlockSpec((1, dm), lambda i, *_: (0, 0)),
                      pl.BlockSpec(memory_space=pl.ANY)],
            out_specs=tok(dm),
            scratch_shapes=[pltpu.VMEM((2, mrows, dm), BF16), pltpu.SemaphoreType.DMA((2, ne))]),
        out_shape=jax.ShapeDtypeStruct((n, dm), F32),
        compiler_params=_cparams("arbitrary"),
        name="moe_combine",
    )(lay['seg'], lay['base'], lay['off'], rt_tok, x2, gn.reshape(1, dm), yb)


def _moe_layout(cnt, n):
    nt, ne = cnt.shape
    blk = MOE_BLK
    whole = lambda a: a // BF16_ROWS * BF16_ROWS
    cap = -(-(n + BF16_ROWS + blk) // blk) * blk
    region = jnp.arange(ne, dtype=I32) * cap
    filled = jnp.cumsum(cnt, axis=0) - cnt
    phase = filled - whole(filled)
    span = jnp.where(cnt > 0, whole(phase + cnt + BF16_ROWS - 1), 0)
    base = (region[None, :] + filled - phase).astype(I32)
    off = (jnp.cumsum(span, axis=1) - span).astype(I32)
    rows_e = jnp.sum(cnt, axis=0)
    nblk_e = (rows_e + blk - 1) // blk
    end = (region + whole(rows_e + BF16_ROWS - 1)).astype(I32)
    ends = jnp.cumsum(nblk_e)
    max_blocks = TOP_K * n // blk + ne
    bid = jnp.arange(max_blocks + 1, dtype=I32)
    bexp = jnp.minimum(jnp.sum((bid[:, None] >= ends[None, :]).astype(I32), axis=1), ne - 1)
    brow = bexp * (cap // blk) + bid - (ends - nblk_e)[bexp]
    nblk = ends[-1:].astype(I32)
    last = jnp.maximum(nblk[0] - 1, 0)
    used = bid < nblk[0]
    bexp = jnp.where(used, bexp, bexp[last]).astype(I32)
    brow = jnp.where(used, brow, brow[last]).astype(I32)
    flat = lambda a: a.reshape(-1)
    return dict(seg=flat(span.astype(I32)), base=flat(base), off=flat(off), phase=flat(phase.astype(I32)), end=end,
                brow=brow, bexp=bexp, nblk=nblk, rows_total=ne * cap, max_blocks=max_blocks)


def _moe(x2, h, logits, gn, w1, w3, w2):
    n, dm = x2.shape
    ne = w1.shape[0]
    rt_lane, rt_tok, cnt = _route(logits, ne)
    lay = _moe_layout(cnt[:, :, 0], n)
    xb = _dispatch(h, rt_lane, lay, ne)
    yb = _expert_ffn(xb, w1, w3, w2, lay['brow'], lay['bexp'], lay['nblk'],
                     lay['max_blocks'])
    return _combine(yb, rt_tok, x2, gn, lay, ne)


def kernel(x, norm_mix_g, w_in, attn_rel_bias, rg_conv_w, rg_conv_b, rg_wx, rg_bx, rg_wa, rg_ba, rg_lambda, s5_a_re, s5_a_im, s5_log_dt, s5_b_re, s5_b_im, s5_c_re, s5_c_im, s5_d, s5_w_glu, g_group, w_out, norm_ffn_g, ffn_w1, ffn_w3, ffn_w2, moe_router, moe_w1, moe_w3, moe_w2, final_norm_g):
    bsz, l, dm = x.shape
    depth = w_in.shape[0]
    assert depth == 2 and l % ATTN_TQ == 0 and ATTN_TQ == N_PREV * CHUNK, "dense layer, then the MoE layer"
    d_rg = rg_conv_w.shape[2]
    d_s5 = s5_w_glu.shape[1]
    d_attn = (w_in.shape[2] - 2 * d_rg - d_s5) // 3
    x2 = x.reshape(bsz * l, dm)
    for layer in range(depth):
        q, k, v, xr, gate, us = _in_proj(x2, norm_mix_g[layer], w_in, layer, d_attn, d_rg, d_s5)
        y_attn = _attention(q, k, v, _attn_bias(attn_rel_bias[layer]), bsz, l)
        w_gates = jnp.concatenate([_block_diag(rg_wx[layer]), _block_diag(rg_wa[layer])], axis=1).astype(BF16)
        b_gates = jnp.concatenate([rg_bx[layer], rg_ba[layer]]).astype(F32)
        y_rg = _rglru(xr, gate, rg_conv_w[layer].astype(F32), rg_conv_b[layer].astype(F32), w_gates, b_gates,
                      jax.nn.softplus(-rg_lambda[layer].astype(F32)), bsz, l)
        prm = _s5_params(s5_a_re[layer], s5_a_im[layer], s5_log_dt[layer], s5_b_re[layer], s5_b_im[layer],
                         s5_c_re[layer], s5_c_im[layer], s5_d[layer])
        y_s5 = _s5(us, prm, bsz, l)
        mixed = (y_attn, y_rg, y_s5, x2, g_group[layer].astype(F32), s5_w_glu, w_out, layer,
                 norm_ffn_g[layer].astype(F32))
        if layer == 0:
            x2, h = _out_proj(*mixed)
            x2, *experts_bf = _dense_ffn(h, x2, ffn_w1[0], ffn_w3[0], ffn_w2[0], moe_w1[0], moe_w3[0], moe_w2[0])
        else:
            x2, h, logits = _out_proj(*mixed, router=moe_router[0])
            x2 = _moe(x2, h, logits, final_norm_g.astype(F32), *experts_bf)
    return x2.reshape(bsz, l, dm)
```

```python
import functools
import math

import jax
import jax.numpy as jnp
from jax import lax
from jax.experimental import pallas as pl
from jax.experimental.pallas import tpu as pltpu

F32 = jnp.float32
BF16 = jnp.bfloat16
I32 = jnp.int32

EPS = 1e-6
CHUNK = 64
N_PREV = 8
BAND = (N_PREV + 1) * CHUNK
REL_CLIP = 128
HEAD_DIM = 64
RG_C = 8.0
RG_CONV_WIDTH = 4
S5_GROUP_DIM = 16
S5_T = 16
TOP_K = 2
NEG_BIG = -1e30

LANES = 128
BF16_ROWS = 16
VMEM_LIMIT = 52 * 1024 * 1024

ROW_TILE = 512
ATTN_TQ = 512
RG_TL = 1024
RG_SEG = 8
MOE_TILE = 256
MOE_BLK = 256
ROUTE_SPAN = 2048
CAST_UNITS = 16
DENSE_VMEM_LIMIT = 58 * 1024 * 1024
SEG_PIECES = tuple(BF16_ROWS << b for b in range((MOE_TILE // BF16_ROWS).bit_length() - 1, -1, -1))
FFN_CHUNKS = ((0, 512), (512, 512), (1024, 512), (1536, 512), (2048, 512), (2560, 256))


def _cparams(*sem):
    return pltpu.CompilerParams(dimension_semantics=sem, vmem_limit_bytes=VMEM_LIMIT)


def _const_spec(shape):
    nd = len(shape)
    return pl.BlockSpec(shape, lambda *_: (0,) * nd, pipeline_mode=pl.Buffered(1))


def _rms(xf, g):
    var = jnp.mean(xf * xf, axis=-1, keepdims=True)
    return xf * lax.rsqrt(var + EPS) * g


def _sigmoid(x):
    return 1.0 / (1.0 + jnp.exp(-x))


def _gelu(x):
    c = math.sqrt(2.0 / math.pi)
    return 0.5 * x * (1.0 + jnp.tanh(c * (x + 0.044715 * (x * x * x))))


def _dot(a, b):
    return jnp.dot(a, b, preferred_element_type=F32)


def _in_proj_kernel(x_ref, g_ref, w_ref, q_ref, k_ref, v_ref, xr_ref, gt_ref, us_ref, *, d_attn, d_rg):
    u = _rms(x_ref[...], g_ref[...]).astype(BF16)
    lo = 0
    for ref, width in ((q_ref, d_attn), (k_ref, d_attn), (v_ref, d_attn),
                       (xr_ref, d_rg), (gt_ref, d_rg), (us_ref, w_ref.shape[1] - 3 * d_attn - 2 * d_rg)):
        ref[...] = _dot(u, w_ref[:, lo:lo + width].astype(BF16)).astype(ref.dtype)
        lo += width


def _layer_spec(stacked, layer):
    return pl.BlockSpec((None,) + stacked.shape[1:], lambda *_: (layer, 0, 0), pipeline_mode=pl.Buffered(1))


def _in_proj(x2, g, w_all, layer, d_attn, d_rg, d_s5):
    n, dm = x2.shape
    tm = min(ROW_TILE, n)
    row = lambda w: pl.BlockSpec((tm, w), lambda i: (i, 0))
    return pl.pallas_call(
        functools.partial(_in_proj_kernel, d_attn=d_attn, d_rg=d_rg),
        grid=(n // tm,),
        in_specs=[row(dm), _const_spec((1, dm)), _layer_spec(w_all, layer)],
        out_specs=[row(d_attn), row(d_attn), row(d_attn), row(d_rg), row(d_rg), row(d_s5)],
        out_shape=[jax.ShapeDtypeStruct((n, d_attn), BF16)] * 3
        + [jax.ShapeDtypeStruct((n, d_rg), F32)] * 2 + [jax.ShapeDtypeStruct((n, d_s5), F32)],
        compiler_params=_cparams("parallel"),
        name="in_proj",
    )(x2, g.reshape(1, dm), w_all)


def _attn_kernel(q_ref, kp_ref, kc_ref, vp_ref, vc_ref, bias_ref, o_ref, kz, vz, *, n_pairs):
    qi = pl.program_id(1)
    tq = q_ref.shape[0]
    kz[0:tq, :] = kp_ref[...]
    kz[tq:2 * tq, :] = kc_ref[...]
    vz[0:tq, :] = vp_ref[...]
    vz[tq:2 * tq, :] = vc_ref[...]
    lane = lax.broadcasted_iota(I32, (CHUNK, LANES), 1)
    first = lane < HEAD_DIM
    kpos = lax.broadcasted_iota(I32, (1, BAND), 1)
    pairs = [slice(hp * LANES, (hp + 1) * LANES) for hp in range(n_pairs)]

    def chunk_body(c, carry, *, masked):
        r0 = pl.multiple_of(c * CHUNK, CHUNK)
        scores = []
        for hp, cols in enumerate(pairs):
            q2 = q_ref[pl.ds(r0, CHUNK), cols] * jnp.asarray(HEAD_DIM ** -0.5, BF16)
            zero = jnp.zeros_like(q2)
            qq = jnp.concatenate([jnp.where(first, q2, zero), jnp.where(first, zero, q2)], axis=0)
            s = lax.dot_general(qq, kz[pl.ds(r0, BAND), cols], (((1,), (1,)), ((), ())), preferred_element_type=F32)
            scores.append(s + bias_ref[hp])
        if masked:
            neg = jnp.where(kpos >= N_PREV * CHUNK - c * CHUNK, 0.0, NEG_BIG)
            scores = [s + neg for s in scores]
        probs, sums = [], []
        for s in scores:
            p = jnp.exp(s - jnp.max(s, axis=-1, keepdims=True))
            sums.append(jnp.sum(p, axis=-1, keepdims=True))
            probs.append(p.astype(BF16))
        for cols, p, l in zip(pairs, probs, sums):
            o2 = _dot(p, vz[pl.ds(r0, BAND), cols]) / l
            o_ref[pl.ds(r0, CHUNK), cols] = jnp.where(first, o2[0:CHUNK], o2[CHUNK:2 * CHUNK])
        return carry

    @pl.when(qi == 0)
    def _():
        lax.fori_loop(0, tq // CHUNK, functools.partial(chunk_body, masked=True), 0)

    @pl.when(qi != 0)
    def _():
        lax.fori_loop(0, tq // CHUNK, functools.partial(chunk_body, masked=False), 0, unroll=2)


def _attention(q, k, v, bias2, bsz, l):
    n, da = q.shape
    tq = ATTN_TQ
    nt = l // tq
    n_pairs = da // LANES
    cur = pl.BlockSpec((tq, da), lambda b, i: (b * nt + i, 0))
    prev = pl.BlockSpec((tq, da), lambda b, i: (b * nt + jnp.maximum(i - 1, 0), 0))
    return pl.pallas_call(
        functools.partial(_attn_kernel, n_pairs=n_pairs),
        grid=(bsz, nt),
        in_specs=[cur, prev, cur, prev, cur, _const_spec(bias2.shape)],
        out_specs=cur,
        out_shape=jax.ShapeDtypeStruct((n, da), F32),
        scratch_shapes=[pltpu.VMEM((2 * tq, da), BF16), pltpu.VMEM((2 * tq, da), BF16)],
        compiler_params=_cparams("parallel", "parallel"),
        name="chunk_attn",
    )(q, k, k, v, v, bias2)


def _attn_bias(rel_bias):
    h = rel_bias.shape[0]
    tab = rel_bias.astype(F32)
    n_far = N_PREV * CHUNK - REL_CLIP + CHUNK
    lo = 2 * REL_CLIP - (BAND + CHUNK - 1 - n_far)
    ext = jnp.concatenate([jnp.broadcast_to(tab[:, 2 * REL_CLIP:], (h, n_far)), tab[:, lo:2 * REL_CLIP][:, ::-1]], axis=1)
    wide = BAND + CHUNK
    ring = jnp.concatenate([ext[:, CHUNK - 1:], jnp.zeros((h, 1), F32), ext[:, :CHUNK - 1]], axis=1)
    bias = jnp.broadcast_to(ring[:, None, :], (h, CHUNK, wide)).reshape(h, CHUNK * wide)
    bias = bias[:, :CHUNK * (wide - 1)].reshape(h, CHUNK, wide - 1)[..., :BAND]
    return bias.reshape(h // 2, 2 * CHUNK, BAND)


def _rglru_kernel(x_ref, gt_ref, cw_ref, cb_ref, w_ref, b_ref, sp_ref, o_ref, xpad, a_s, b_s, hcar):
    t = pl.program_id(1)
    tl, c = x_ref.shape
    seg = tl // RG_SEG
    front = 8

    @pl.when(t == 0)
    def _():
        xpad[0:front, :] = jnp.zeros((front, c), F32)
        hcar[...] = jnp.zeros_like(hcar)

    xpad[front:front + tl, :] = x_ref[...]
    xc = cb_ref[...] + sum(
        cw_ref[j:j + 1, :] * xpad[front - (RG_CONV_WIDTH - 1) + j:front - (RG_CONV_WIDTH - 1) + j + tl, :]
        for j in range(RG_CONV_WIDTH))
    xpad[0:front, :] = xpad[tl:tl + front, :]
    pre = _dot(xc.astype(BF16), w_ref[...]) + b_ref[...]
    gx = _sigmoid(pre[:, 0:c])
    ga = _sigmoid(pre[:, c:2 * c])
    log_a = -RG_C * ga * sp_ref[...]
    a = jnp.exp(log_a)
    mult = jnp.sqrt(-jnp.tanh(log_a) * (a * a + 1.0))
    b = mult * gx * xc
    nh = c // LANES
    for j in range(nh):
        a_s[j] = a[:, j * LANES:(j + 1) * LANES]
        b_s[j] = b[:, j * LANES:(j + 1) * LANES]

    def step(i, carry):
        rows = pl.ds(i, RG_SEG, stride=seg)
        out = []
        for j in range(nh):
            h, p = carry[j]
            av = a_s[j, rows, :]
            h = av * h + b_s[j, rows, :]
            p = av * p
            b_s[j, rows, :] = h
            a_s[j, rows, :] = p
            out.append((h, p))
        return tuple(out)

    init = tuple((jnp.zeros((RG_SEG, LANES), F32), jnp.ones((RG_SEG, LANES), F32)) for _ in range(nh))
    ends = lax.fori_loop(0, seg, step, init, unroll=8)
    for j in range(nh):
        cols = slice(j * LANES, (j + 1) * LANES)
        h_end, p_end = ends[j]
        carry = hcar[:, cols]
        for s in range(RG_SEG):
            rows = slice(s * seg, (s + 1) * seg)
            h = b_s[j, rows, :] + a_s[j, rows, :] * carry
            o_ref[rows, cols] = h * _gelu(gt_ref[rows, cols])
            carry = h_end[s:s + 1, :] + p_end[s:s + 1, :] * carry
        hcar[:, cols] = carry


def _rglru(xr, gate, conv_w, conv_b, w_bd, b_cat, sp, bsz, l):
    n, c = xr.shape
    tl = min(RG_TL, l)
    nt = l // tl
    row = pl.BlockSpec((tl, c), lambda b, t: (b * nt + t, 0))
    return pl.pallas_call(
        _rglru_kernel,
        grid=(bsz, nt),
        in_specs=[row, row, _const_spec(conv_w.shape), _const_spec((1, c)), _const_spec(w_bd.shape),
                  _const_spec((1, 2 * c)), _const_spec((1, c))],
        out_specs=row,
        out_shape=jax.ShapeDtypeStruct((n, c), F32),
        scratch_shapes=[pltpu.VMEM((tl + 8, c), F32), pltpu.VMEM((c // LANES, tl, LANES), F32),
                        pltpu.VMEM((c // LANES, tl, LANES), F32), pltpu.VMEM((1, c), F32)],
        compiler_params=_cparams("parallel", "arbitrary"),
        name="rglru",
    )(xr, gate, conv_w, conv_b.reshape(1, c), w_bd, b_cat.reshape(1, 2 * c), sp.reshape(1, c))


def _block_diag(w):
    nb, d, _ = w.shape
    eye = jnp.eye(nb, dtype=w.dtype)
    return (eye[:, None, :, None] * w[:, :, None, :]).reshape(nb * d, nb * d)


def _s5_kernel(u_ref, m_ref, wa_ref, wb_ref, wo_ref, c1_ref, c2a_ref, c2b_ref, d_ref, y_ref, ps, ug, xa_s, xb_s):
    t, gd = S5_T, S5_GROUP_DIM
    ng, nk, _ = ug.shape
    nh = ps.shape[0]
    per_half = LANES // gd
    rows8 = 8

    for h in range(nh):
        ps[h] = u_ref[:, h * LANES:(h + 1) * LANES]

    lane_block = lax.broadcasted_iota(I32, (rows8, LANES), 1) // gd

    def block_transpose(vs):
        d = per_half // 2
        while d:
            keep = (lane_block & d) == 0
            out = list(vs)
            for i in range(per_half):
                if i & d == 0:
                    out[i] = jnp.where(keep, vs[i], pltpu.roll(vs[i + d], d * gd, 1))
                    out[i + d] = jnp.where(keep, pltpu.roll(vs[i], LANES - d * gd, 1), vs[i + d])
            vs, d = out, d // 2
        return vs

    def to_groups(r, carry):
        base = pl.multiple_of(r * rows8 * t, rows8 * t)
        rows = pl.ds(pl.multiple_of(r * rows8, rows8), rows8)
        for h in range(nh):
            for j in range(t // per_half):
                steps = [ps[h, pl.ds(base + j * per_half + i, rows8, stride=t), :] for i in range(per_half)]
                for k, v in enumerate(block_transpose(steps)):
                    ug[h * per_half + k, rows, j * LANES:(j + 1) * LANES] = v
        return carry

    lax.fori_loop(0, nk // rows8, to_groups, 0, unroll=2)

    def project(g, carry):
        u = ug[g]
        ub = u.astype(BF16)
        rows = pl.ds(pl.multiple_of(g * nk, nk), nk)
        xa_s[rows, :] = _dot(ub, wa_ref[g])
        xb_s[rows, :] = _dot(ub, wb_ref[g])
        ug[g] = _dot(ub, m_ref[g]) + d_ref[g] * u
        return carry

    lax.fori_loop(0, ng, project, 0)

    c1, c2a, c2b = c1_ref[...], c2a_ref[...], c2b_ref[...]

    def step(k, carry):
        xa, xb = carry
        rows = pl.ds(k, ng, stride=nk)
        ia = xa_s[rows, :]
        ib = xb_s[rows, :]
        xa_s[rows, :] = xa
        return c1 * xa + c2a * xb + ia, c1 * xb + c2b * xa + ib

    z = jnp.zeros((ng, xa_s.shape[1]), F32)
    lax.fori_loop(0, nk, step, (z, z), unroll=4)

    def respond(g, carry):
        rows = pl.ds(pl.multiple_of(g * nk, nk), nk)
        ug[g] = _gelu(ug[g] + _dot(xa_s[rows, :].astype(BF16), wo_ref[g]))
        return carry

    lax.fori_loop(0, ng, respond, 0)

    def from_groups(r, carry):
        base = pl.multiple_of(r * rows8 * t, rows8 * t)
        rows = pl.ds(pl.multiple_of(r * rows8, rows8), rows8)
        for h in range(nh):
            for j in range(t // per_half):
                groups = [ug[h * per_half + k, rows, j * LANES:(j + 1) * LANES] for k in range(per_half)]
                for i, v in enumerate(block_transpose(groups)):
                    ps[h, pl.ds(base + j * per_half + i, rows8, stride=t), :] = v
        return carry

    lax.fori_loop(0, nk // rows8, from_groups, 0, unroll=2)
    for h in range(nh):
        y_ref[:, h * LANES:(h + 1) * LANES] = ps[h]


def _s5(us, prm, bsz, l):
    n, w = us.shape
    ng = w // S5_GROUP_DIM
    nk = l // S5_T
    p2 = prm[1].shape[2]
    row = pl.BlockSpec((l, w), lambda b: (b, 0))
    return pl.pallas_call(
        _s5_kernel,
        grid=(bsz,),
        in_specs=[row] + [_const_spec(a.shape) for a in prm],
        out_specs=row,
        out_shape=jax.ShapeDtypeStruct((n, w), F32),
        scratch_shapes=[pltpu.VMEM((w // LANES, l, LANES), F32), pltpu.VMEM((ng, nk, S5_T * S5_GROUP_DIM), F32),
                        pltpu.VMEM((ng * nk, p2), F32), pltpu.VMEM((ng * nk, p2), F32)],
        compiler_params=_cparams("parallel"),
        name="s5",
    )(us, *prm)


def _s5_params(a_re, a_im, log_dt, b_re, b_im, c_re, c_im, d):
    t = S5_T
    g, p = a_re.shape
    gd = S5_GROUP_DIM
    hi = lax.Precision.HIGHEST
    a = lax.complex(a_re.astype(F32), a_im.astype(F32))
    dt = jnp.exp(log_dt.astype(F32))[:, None]
    a_bar = jnp.exp(a * dt)
    bm = lax.complex(b_re.astype(F32), b_im.astype(F32))
    cm = lax.complex(c_re.astype(F32), c_im.astype(F32))
    b_bar = ((a_bar - 1.0) / a)[..., None] * bm
    steps = jnp.arange(t + 1, dtype=F32)
    pw = jnp.exp((a * dt)[:, None, :] * steps[None, :, None])

    cp = (cm[:, None, :, :] * pw[:, :, None, :]).transpose(0, 3, 1, 2)
    w = t * gd
    lag = cp[:, :, :t].reshape(g, p, w)
    k2 = jnp.einsum('gpc,gpx->gcx', jnp.concatenate([b_bar.real, -b_bar.imag], axis=1),
                    jnp.concatenate([lag.real, lag.imag], axis=1), precision=hi)
    skew = jnp.broadcast_to(jnp.concatenate([k2, jnp.zeros_like(k2)], axis=-1)[:, :, None, :], (g, gd, t, 2 * w))
    skew = skew.reshape(g, gd, t * 2 * w)[:, :, :t * (2 * w - gd)].reshape(g, gd, t, 2 * w - gd)[..., :w]
    m = skew.transpose(0, 2, 1, 3).reshape(g, w, w)
    win = pw[:, :t][:, ::-1][:, :, :, None] * b_bar[:, None]
    win = win.transpose(0, 1, 3, 2).reshape(g, w, p)
    wa = jnp.concatenate([win.real, win.imag], axis=-1)
    wb = jnp.concatenate([win.imag, win.real], axis=-1)
    co = cp[:, :, 1:].reshape(g, p, w)
    wo = jnp.concatenate([co.real, -co.imag], axis=1)
    lt = pw[:, t]
    c1 = jnp.concatenate([lt.real, lt.real], axis=-1)
    c2a = jnp.concatenate([-lt.imag, lt.imag], axis=-1)
    c2b = jnp.concatenate([lt.imag, -lt.imag], axis=-1)
    dtile = jnp.tile(d.astype(F32).reshape(g, 1, gd), (1, t, 1)).reshape(g, 1, t * gd)
    return (m.astype(BF16), wa.astype(BF16), wb.astype(BF16), wo.astype(BF16), c1, c2a, c2b, dtile)


def _out_proj_kernel(ya_ref, yr_ref, ys_ref, x_ref, gg_ref, wglu_ref, wo_ref, gf_ref, *rest, with_router):
    if with_router:
        rt_ref, xo_ref, h_ref, lg_ref = rest
    else:
        xo_ref, h_ref = rest
    ys = ys_ref[...]
    ys = ys * _sigmoid(_dot(ys.astype(BF16), wglu_ref[...].astype(BF16)))
    acc = x_ref[...]
    lo = 0
    for y in (ya_ref[...], yr_ref[...], ys):
        w = y.shape[1]
        acc = acc + _dot(_rms(y, gg_ref[:, lo:lo + w]).astype(BF16), wo_ref[lo:lo + w, :].astype(BF16))
        lo += w
    xo_ref[...] = acc
    h = _rms(acc, gf_ref[...])
    hb = h.astype(BF16)
    h_ref[...] = hb
    if with_router:
        both = _dot(hb, rt_ref[...])
        h_lo = (h - hb.astype(F32)).astype(BF16)
        lg_ref[...] = both[:, 0:LANES] + both[:, LANES:2 * LANES] + _dot(h_lo, rt_ref[:, 0:LANES])


def _out_proj(ya, yr, ys, x2, gg, wglu_all, wo_all, layer, gf, router=None):
    n, dm = x2.shape
    tm = min(ROW_TILE, n)
    row = lambda w: pl.BlockSpec((tm, w), lambda i: (i, 0))
    ins = [ya, yr, ys, x2, gg.reshape(1, -1), wglu_all, wo_all, gf.reshape(1, dm)]
    in_specs = [row(ya.shape[1]), row(yr.shape[1]), row(ys.shape[1]), row(dm), _const_spec((1, gg.shape[0])),
                _layer_spec(wglu_all, layer), _layer_spec(wo_all, layer), _const_spec((1, dm))]
    out_specs = [row(dm), row(dm)]
    out_shape = [jax.ShapeDtypeStruct((n, dm), F32), jax.ShapeDtypeStruct((n, dm), BF16)]
    if router is not None:
        ne = router.shape[1]
        r_hi = router.astype(BF16)
        r_lo = (router.astype(F32) - r_hi.astype(F32)).astype(BF16)
        pad = lambda a: jnp.pad(a, ((0, 0), (0, LANES - ne)))
        ins.append(jnp.concatenate([pad(r_hi), pad(r_lo)], axis=1))
        in_specs.append(_const_spec((dm, 2 * LANES)))
        out_specs.append(row(LANES))
        out_shape.append(jax.ShapeDtypeStruct((n, LANES), F32))
    return pl.pallas_call(
        functools.partial(_out_proj_kernel, with_router=router is not None),
        grid=(n // tm,),
        in_specs=in_specs, out_specs=out_specs, out_shape=out_shape,
        compiler_params=_cparams("parallel"),
        name="out_proj",
    )(*ins)


def _swiglu_chunks(h, w1, w3, w2, chunks, after_chunk=None):
    acc = None
    for j, (lo, width) in enumerate(chunks):
        a = _dot(h, w1[:, lo:lo + width].astype(BF16))
        b = _dot(h, w3[:, lo:lo + width].astype(BF16))
        t = (a * _sigmoid(a) * b).astype(BF16)
        y = _dot(t, w2[lo:lo + width, :].astype(BF16))
        acc = y if acc is None else acc + y
        if after_chunk is not None:
            after_chunk(j)
    return acc


def _ffn_chunks(f):
    if f == sum(w for _, w in FFN_CHUNKS):
        return FFN_CHUNKS
    return ((0, f),)


def _cast_copies(srcs, dsts, stage_in, stage_out, sems, unit, slot, inbound):
    out = []
    for m, (src, dst) in enumerate(zip(srcs, dsts)):
        rows = src.shape[1] // CAST_UNITS
        where = (unit // CAST_UNITS, pl.ds(pl.multiple_of((unit % CAST_UNITS) * rows, BF16_ROWS), rows))
        a, b = (src.at[where], stage_in[m].at[slot]) if inbound else (stage_out[m].at[slot], dst.at[where])
        out.append(pltpu.make_async_copy(a, b, sems.at[int(inbound), slot, m]))
    return out


def _dense_ffn_kernel(h_ref, x_ref, w1_ref, w3_ref, w2_ref, e1_ref, e3_ref, e2_ref, o_ref, c1_ref, c3_ref, c2_ref,
                      si1, si3, si2, so1, so3, so2, sems, *, chunks, per_step):
    i = pl.program_id(0)
    last = pl.num_programs(0) - 1
    copies = functools.partial(_cast_copies, (e1_ref, e3_ref, e2_ref), (c1_ref, c3_ref, c2_ref),
                               (si1, si3, si2), (so1, so3, so2), sems)
    def go(cps):
        for cp in cps:
            cp.start()

    def done(cps):
        for cp in cps:
            cp.wait()

    @pl.when(i == 0)
    def _():
        go(copies(0, 0, True))

    final = (last + 1) * per_step - 1

    def cast_unit(k):
        unit, slot = i * per_step + k, k % 2
        done(copies(unit, slot, True))
        go(copies(jnp.minimum(unit + 1, final), 1 - slot, True))
        if k >= 2:
            done(copies(unit - 2, slot, False))
        for s_in, s_out in ((si1, so1), (si3, so3), (si2, so2)):
            s_out[slot] = s_in[slot].astype(BF16)
        go(copies(unit, slot, False))

    o_ref[...] = x_ref[...] + _swiglu_chunks(h_ref[...], w1_ref, w3_ref, w2_ref, chunks,
                                             after_chunk=lambda j: cast_unit(j) if j < per_step else None)
    for k in (per_step - 2, per_step - 1):
        done(copies(i * per_step + k, k % 2, False))

    @pl.when(i == last)
    def _():
        done(copies(final, per_step % 2, True))


def _dense_ffn(h, x2, w1, w3, w2, e1, e3, e2):
    n, dm = x2.shape
    tm = min(ROW_TILE, n)
    steps = n // tm
    ne = e1.shape[0]
    per_step = ne * CAST_UNITS // steps
    chunks = _ffn_chunks(w1.shape[1])
    assert per_step * steps == ne * CAST_UNITS and per_step % 2 == 0 and per_step <= len(chunks)
    row = pl.BlockSpec((tm, dm), lambda i: (i, 0))
    hbm = pl.BlockSpec(memory_space=pl.ANY)
    slab = lambda e, dt: pltpu.VMEM((2, e.shape[1] // CAST_UNITS, e.shape[2]), dt)
    experts = (e1, e3, e2)
    return pl.pallas_call(
        functools.partial(_dense_ffn_kernel, chunks=chunks, per_step=per_step),
        grid=(steps,),
        in_specs=[row, row, _const_spec(w1.shape), _const_spec(w3.shape), _const_spec(w2.shape), hbm, hbm, hbm],
        out_specs=[row, hbm, hbm, hbm],
        out_shape=[jax.ShapeDtypeStruct((n, dm), F32)] + [jax.ShapeDtypeStruct(e.shape, BF16) for e in experts],
        scratch_shapes=[slab(e, F32) for e in experts] + [slab(e, BF16) for e in experts]
        + [pltpu.SemaphoreType.DMA((2, 2, 3))],
        compiler_params=pltpu.CompilerParams(dimension_semantics=("arbitrary",), vmem_limit_bytes=DENSE_VMEM_LIMIT),
        name="dense_ffn",
    )(h, x2, w1, w3, w2, e1, e3, e2)


def _expert_ffn_kernel(brow_ref, bexp_ref, nblk_ref, x_ref, w1_ref, w3_ref, w2_ref, o_ref, *, chunks):
    @pl.when(pl.program_id(0) < nblk_ref[0])
    def _():
        o_ref[...] = _swiglu_chunks(x_ref[...], w1_ref.at[0], w3_ref.at[0], w2_ref.at[0], chunks).astype(o_ref.dtype)


def _expert_ffn(xb, w1, w3, w2, blk_row, blk_exp, nblk, max_blocks):
    rows, dm = xb.shape
    f = w1.shape[2]
    row = pl.BlockSpec((MOE_BLK, dm), lambda i, br, be, nb: (br[i], 0))
    wspec = lambda a: pl.BlockSpec((1,) + a.shape[1:], lambda i, br, be, nb: (be[i], 0, 0))
    return pl.pallas_call(
        functools.partial(_expert_ffn_kernel, chunks=_ffn_chunks(f)),
        grid_spec=pltpu.PrefetchScalarGridSpec(
            num_scalar_prefetch=3, grid=(max_blocks,),
            in_specs=[row, wspec(w1), wspec(w3), wspec(w2)], out_specs=row),
        out_shape=jax.ShapeDtypeStruct((rows, dm), BF16),
        compiler_params=_cparams("arbitrary"),
        name="expert_ffn",
    )(blk_row, blk_exp, nblk, xb, w1, w3, w2)


def _route_tile(logits_tok, filled, ne):
    tm = logits_tok.shape[0]
    logits = logits_tok.T[0:ne, :]
    eidx = lax.broadcasted_iota(I32, (ne, tm), 0)
    m1 = jnp.max(logits, axis=0, keepdims=True)
    i1 = jnp.min(jnp.where(logits == m1, eidx, ne), axis=0, keepdims=True)
    sel1 = eidx == i1
    rest = jnp.where(sel1, -jnp.inf, logits)
    m2 = jnp.max(rest, axis=0, keepdims=True)
    i2 = jnp.min(jnp.where(rest == m2, eidx, ne), axis=0, keepdims=True)
    sel2 = eidx == i2
    e2 = jnp.exp(m2 - m1)
    den = 1.0 + e2
    rf = jnp.where(sel1 | sel2, 1.0, 0.0)
    cnt = jnp.sum(rf, axis=1, keepdims=True)
    before = (lax.broadcasted_iota(I32, (tm, tm), 0) < lax.broadcasted_iota(I32, (tm, tm), 1))
    rank = _dot(rf.astype(BF16), jnp.where(before, 1.0, 0.0).astype(BF16))
    whole = lambda a: jnp.floor(a * (1.0 / BF16_ROWS)) * BF16_ROWS
    phase = filled - whole(filled)
    span = jnp.where(cnt > 0, whole(phase + cnt + (BF16_ROWS - 1)), 0.0)
    ecol = lax.broadcasted_iota(I32, (ne, 1), 0)
    off = jnp.zeros((ne, 1), F32)
    for j in range(ne - 1):
        off = off + jnp.where(ecol > j, span[j:j + 1, :], 0.0)
    place = off + phase + rank
    pos1 = jnp.sum(jnp.where(sel1, place, 0.0), axis=0, keepdims=True)
    pos2 = jnp.sum(jnp.where(sel2, place, 0.0), axis=0, keepdims=True)
    lane_major = jnp.concatenate([pos1, pos2, 1.0 / den, e2 / den, jnp.zeros((4, tm), F32)], axis=0)
    tok_major = jnp.concatenate([lane_major, jnp.zeros((LANES - 8, tm), F32)], axis=0).T
    return lane_major, tok_major, cnt


def _route_kernel(lg_ref, lane_ref, tok_ref, cnt_ref, filled, *, ne):
    tm = MOE_TILE

    @pl.when(pl.program_id(0) == 0)
    def _():
        filled[...] = jnp.zeros_like(filled)

    for t in range(lg_ref.shape[0] // tm):
        lane_major, tok_major, cnt = _route_tile(lg_ref[t * tm:(t + 1) * tm, :], filled[...], ne)
        lane_ref[:, t * tm:(t + 1) * tm] = lane_major
        tok_ref[t * tm:(t + 1) * tm, :] = tok_major
        cnt_ref[t] = jnp.broadcast_to(cnt, (ne, LANES)).astype(I32)
        filled[...] = filled[...] + cnt


def _route(logits, ne):
    n = logits.shape[0]
    tm = MOE_TILE
    span = min(ROUTE_SPAN, n)
    return pl.pallas_call(
        functools.partial(_route_kernel, ne=ne),
        grid=(n // span,),
        in_specs=[pl.BlockSpec((span, LANES), lambda i: (i, 0))],
        out_specs=[pl.BlockSpec((8, span), lambda i: (0, i)), pl.BlockSpec((span, LANES), lambda i: (i, 0)),
                   pl.BlockSpec((span // tm, ne, LANES), lambda i: (i, 0, 0))],
        out_shape=[jax.ShapeDtypeStruct((8, n), F32), jax.ShapeDtypeStruct((n, LANES), F32),
                   jax.ShapeDtypeStruct((n // tm, ne, LANES), I32)],
        scratch_shapes=[pltpu.VMEM((ne, 1), F32)],
        compiler_params=_cparams("arbitrary"),
        name="moe_route",
    )(logits)


def _segment_copies(seg_ref, grouped_ref, base_ref, tile_buf, off_ref, sems, tile, slot, ne, *, to_grouped, wait):
    for e in range(ne):
        seg = seg_ref[tile * ne + e]
        far = base_ref[tile * ne + e]
        near = off_ref[tile * ne + e]
        for piece in SEG_PIECES:
            g = grouped_ref.at[pl.ds(pl.multiple_of(far, BF16_ROWS), piece)]
            t = tile_buf.at[slot, pl.ds(pl.multiple_of(near, BF16_ROWS), piece)]
            cp = pltpu.make_async_copy(t, g, sems.at[slot, e]) if to_grouped else \
                pltpu.make_async_copy(g, t, sems.at[slot, e])
            has = (seg & piece) != 0

            @pl.when(has)
            def _():
                cp.wait() if wait else cp.start()

            step = jnp.where(has, piece, 0)
            far = far + step
            near = near + step


def _tile_rows(ne):
    return TOP_K * MOE_TILE + ne * 2 * BF16_ROWS


def _tail_copies(end_ref, zeros, xb_ref, sems, ne):
    return [pltpu.make_async_copy(zeros, xb_ref.at[pl.ds(pl.multiple_of(end_ref[e], BF16_ROWS), MOE_BLK)], sems.at[e])
            for e in range(ne)]


def _dispatch_kernel(seg_ref, base_ref, off_ref, phase_ref, end_ref, rt_ref, h_ref, xb_ref, res, zeros, partial,
                     sems, tail_sems):
    i = pl.program_id(0)
    nt = pl.num_programs(0)
    ne = tail_sems.shape[0]
    tm = h_ref.shape[0]
    mrows = res.shape[1]
    slot = i % 2
    copies = functools.partial(_segment_copies, seg_ref, xb_ref, base_ref, res, off_ref, sems, ne=ne, to_grouped=True)

    @pl.when(i == 0)
    def _():
        zeros[...] = jnp.zeros_like(zeros)
        partial[...] = jnp.zeros_like(partial)
        for cp in _tail_copies(end_ref, zeros, xb_ref, tail_sems, ne):
            cp.start()

    row = lax.broadcasted_iota(I32, (mrows, tm), 0)
    pos = rt_ref[0:2, :].astype(I32)
    perm = jnp.where((row == pos[0:1, :]) | (row == pos[1:2, :]), 1.0, 0.0).astype(BF16)
    res[slot] = _dot(perm, h_ref[...]).astype(BF16)

    tile_row = lax.broadcasted_iota(I32, (BF16_ROWS, res.shape[2]), 0)
    for e in range(ne):
        @pl.when(seg_ref[i * ne + e] > 0)
        def _():
            first = pl.ds(pl.multiple_of(off_ref[i * ne + e], BF16_ROWS), BF16_ROWS)
            merged = jnp.where(tile_row < phase_ref[i * ne + e], partial[e].astype(F32), res[slot, first, :].astype(F32))
            res[slot, first, :] = merged.astype(BF16)
            last = pl.ds(pl.multiple_of(off_ref[i * ne + e] + seg_ref[i * ne + e] - BF16_ROWS, BF16_ROWS), BF16_ROWS)
            partial[e] = res[slot, last, :]

    @pl.when(i > 0)
    def _():
        copies(tile=i - 1, slot=1 - slot, wait=True)

    copies(tile=i, slot=slot, wait=False)

    @pl.when(i == nt - 1)
    def _():
        copies(tile=i, slot=slot, wait=True)
        for cp in _tail_copies(end_ref, zeros, xb_ref, tail_sems, ne):
            cp.wait()


def _dispatch(h, rt_lane, lay, ne):
    n, dm = h.shape
    tm = MOE_TILE
    return pl.pallas_call(
        _dispatch_kernel,
        grid_spec=pltpu.PrefetchScalarGridSpec(
            num_scalar_prefetch=5, grid=(n // tm,),
            in_specs=[pl.BlockSpec((8, tm), lambda i, *_: (0, i)), pl.BlockSpec((tm, dm), lambda i, *_: (i, 0))],
            out_specs=pl.BlockSpec(memory_space=pl.ANY),
            scratch_shapes=[pltpu.VMEM((2, _tile_rows(ne), dm), BF16), pltpu.VMEM((MOE_BLK, dm), BF16),
                            pltpu.VMEM((ne, BF16_ROWS, dm), BF16),
                            pltpu.SemaphoreType.DMA((2, ne)), pltpu.SemaphoreType.DMA((ne,))]),
        out_shape=jax.ShapeDtypeStruct((lay['rows_total'], dm), BF16),
        compiler_params=_cparams("arbitrary"),
        name="moe_dispatch",
    )(lay['seg'], lay['base'], lay['off'], lay['phase'], lay['end'], rt_lane, h)


def _combine_kernel(seg_ref, base_ref, off_ref, rt_ref, x_ref, gn_ref, yb_ref, o_ref, got, sems, *, ne):
    i = pl.program_id(0)
    nt = pl.num_programs(0)
    tm = x_ref.shape[0]
    mrows = got.shape[1]
    slot = i % 2
    copies = functools.partial(_segment_copies, seg_ref, yb_ref, base_ref, got, off_ref, sems, ne=ne, to_grouped=False)

    @pl.when(i == 0)
    def _():
        got[...] = jnp.zeros_like(got)
        copies(tile=i, slot=slot, wait=False)

    @pl.when(i + 1 < nt)
    def _():
        copies(tile=i + 1, slot=1 - slot, wait=False)

    copies(tile=i, slot=slot, wait=True)

    col = lax.broadcasted_iota(I32, (tm, mrows), 1)
    rt = rt_ref[...]
    rows = got[slot]
    sel = jnp.concatenate([jnp.where(col == rt[:, k:k + 1].astype(I32), 1.0, 0.0).astype(BF16) for k in range(TOP_K)],
                          axis=0)
    picked = _dot(sel, rows)
    y = sum(rt[:, TOP_K + k:TOP_K + k + 1] * picked[k * tm:(k + 1) * tm] for k in range(TOP_K))
    o_ref[...] = _rms(x_ref[...] + y, gn_ref[...])


def _combine(yb, rt_tok, x2, gn, lay, ne):
    n, dm = x2.shape
    tm = MOE_TILE
    mrows = _tile_rows(ne)
    tok = lambda w: pl.BlockSpec((tm, w), lambda i, *_: (i, 0))
    return pl.pallas_call(
        functools.partial(_combine_kernel, ne=ne),
        grid_spec=pltpu.PrefetchScalarGridSpec(
            num_scalar_prefetch=3, grid=(n // tm,),
            in_specs=[tok(LANES), tok(dm), pl.BlockSpec((1, dm), lambda i, *_: (0, 0)),
                      pl.BlockSpec(memory_space=pl.ANY)],
            out_specs=tok(dm),
            scratch_shapes=[pltpu.VMEM((2, mrows, dm), BF16), pltpu.SemaphoreType.DMA((2, ne))]),
        out_shape=jax.ShapeDtypeStruct((n, dm), F32),
        compiler_params=_cparams("arbitrary"),
        name="moe_combine",
    )(lay['seg'], lay['base'], lay['off'], rt_tok, x2, gn.reshape(1, dm), yb)


def _moe_layout(cnt, n):
    nt, ne = cnt.shape
    blk = MOE_BLK
    whole = lambda a: a // BF16_ROWS * BF16_ROWS
    cap = -(-(n + BF16_ROWS + blk) // blk) * blk
    region = jnp.arange(ne, dtype=I32) * cap
    filled = jnp.cumsum(cnt, axis=0) - cnt
    phase = filled - whole(filled)
    span = jnp.where(cnt > 0, whole(phase + cnt + BF16_ROWS - 1), 0)
    base = (region[None, :] + filled - phase).astype(I32)
    off = (jnp.cumsum(span, axis=1) - span).astype(I32)
    rows_e = jnp.sum(cnt, axis=0)
    nblk_e = (rows_e + blk - 1) // blk
    end = (region + whole(rows_e + BF16_ROWS - 1)).astype(I32)
    ends = jnp.cumsum(nblk_e)
    max_blocks = TOP_K * n // blk + ne
    bid = jnp.arange(max_blocks + 1, dtype=I32)
    bexp = jnp.minimum(jnp.sum((bid[:, None] >= ends[None, :]).astype(I32), axis=1), ne - 1)
    brow = bexp * (cap // blk) + bid - (ends - nblk_e)[bexp]
    nblk = ends[-1:].astype(I32)
    last = jnp.maximum(nblk[0] - 1, 0)
    used = bid < nblk[0]
    bexp = jnp.where(used, bexp, bexp[last]).astype(I32)
    brow = jnp.where(used, brow, brow[last]).astype(I32)
    flat = lambda a: a.reshape(-1)
    return dict(seg=flat(span.astype(I32)), base=flat(base), off=flat(off), phase=flat(phase.astype(I32)), end=end,
                brow=brow, bexp=bexp, nblk=nblk, rows_total=ne * cap, max_blocks=max_blocks)


def _moe(x2, h, logits, gn, w1, w3, w2):
    n, dm = x2.shape
    ne = w1.shape[0]
    rt_lane, rt_tok, cnt = _route(logits, ne)
    lay = _moe_layout(cnt[:, :, 0], n)
    xb = _dispatch(h, rt_lane, lay, ne)
    yb = _expert_ffn(xb, w1, w3, w2, lay['brow'], lay['bexp'], lay['nblk'],
                     lay['max_blocks'])
    return _combine(yb, rt_tok, x2, gn, lay, ne)


def kernel(x, norm_mix_g, w_in, attn_rel_bias, rg_conv_w, rg_conv_b, rg_wx, rg_bx, rg_wa, rg_ba, rg_lambda, s5_a_re, s5_a_im, s5_log_dt, s5_b_re, s5_b_im, s5_c_re, s5_c_im, s5_d, s5_w_glu, g_group, w_out, norm_ffn_g, ffn_w1, ffn_w3, ffn_w2, moe_router, moe_w1, moe_w3, moe_w2, final_norm_g):
    bsz, l, dm = x.shape
    depth = w_in.shape[0]
    assert depth == 2 and l % ATTN_TQ == 0 and ATTN_TQ == N_PREV * CHUNK, "dense layer, then the MoE layer"
    d_rg = rg_conv_w.shape[2]
    d_s5 = s5_w_glu.shape[1]
    d_attn = (w_in.shape[2] - 2 * d_rg - d_s5) // 3
    x2 = x.reshape(bsz * l, dm)
    for layer in range(depth):
        q, k, v, xr, gate, us = _in_proj(x2, norm_mix_g[layer], w_in, layer, d_attn, d_rg, d_s5)
        y_attn = _attention(q, k, v, _attn_bias(attn_rel_bias[layer]), bsz, l)
        w_gates = jnp.concatenate([_block_diag(rg_wx[layer]), _block_diag(rg_wa[layer])], axis=1).astype(BF16)
        b_gates = jnp.concatenate([rg_bx[layer], rg_ba[layer]]).astype(F32)
        y_rg = _rglru(xr, gate, rg_conv_w[layer].astype(F32), rg_conv_b[layer].astype(F32), w_gates, b_gates,
                      jax.nn.softplus(-rg_lambda[layer].astype(F32)), bsz, l)
        prm = _s5_params(s5_a_re[layer], s5_a_im[layer], s5_log_dt[layer], s5_b_re[layer], s5_b_im[layer],
                         s5_c_re[layer], s5_c_im[layer], s5_d[layer])
        y_s5 = _s5(us, prm, bsz, l)
        mixed = (y_attn, y_rg, y_s5, x2, g_group[layer].astype(F32), s5_w_glu, w_out, layer,
                 norm_ffn_g[layer].astype(F32))
        if layer == 0:
            x2, h = _out_proj(*mixed)
            x2, *experts_bf = _dense_ffn(h, x2, ffn_w1[0], ffn_w3[0], ffn_w2[0], moe_w1[0], moe_w3[0], moe_w2[0])
        else:
            x2, h, logits = _out_proj(*mixed, router=moe_router[0])
            x2 = _moe(x2, h, logits, final_norm_g.astype(F32), *experts_bf)
    return x2.reshape(bsz, l, dm)
```

```python
import functools
import math

import jax
import jax.numpy as jnp
from jax import lax
from jax.experimental import pallas as pl
from jax.experimental.pallas import tpu as pltpu

F32 = jnp.float32
BF16 = jnp.bfloat16
I32 = jnp.int32

EPS = 1e-6
CHUNK = 64
N_PREV = 8
BAND = (N_PREV + 1) * CHUNK
REL_CLIP = 128
HEAD_DIM = 64
RG_C = 8.0
RG_CONV_WIDTH = 4
S5_GROUP_DIM = 16
S5_T = 16
TOP_K = 2
NEG_BIG = -1e30

LANES = 128
BF16_ROWS = 16
VMEM_LIMIT = 52 * 1024 * 1024

ROW_TILE = 512
ATTN_TQ = 512
RG_TL = 1024
RG_SEG = 8
MOE_TILE = 256
MOE_BLK = 256
ROUTE_SPAN = 2048
SEG_PIECES = tuple(BF16_ROWS << b for b in range((MOE_TILE // BF16_ROWS).bit_length() - 1, -1, -1))
FFN_CHUNKS = ((0, 512), (512, 512), (1024, 512), (1536, 512), (2048, 512), (2560, 256))


def _cparams(*sem):
    return pltpu.CompilerParams(dimension_semantics=sem, vmem_limit_bytes=VMEM_LIMIT)


def _const_spec(shape):
    nd = len(shape)
    return pl.BlockSpec(shape, lambda *_: (0,) * nd, pipeline_mode=pl.Buffered(1))


def _rms(xf, g):
    var = jnp.mean(xf * xf, axis=-1, keepdims=True)
    return xf * lax.rsqrt(var + EPS) * g


def _sigmoid(x):
    return 1.0 / (1.0 + jnp.exp(-x))


def _gelu(x):
    c = math.sqrt(2.0 / math.pi)
    return 0.5 * x * (1.0 + jnp.tanh(c * (x + 0.044715 * (x * x * x))))


def _dot(a, b):
    return jnp.dot(a, b, preferred_element_type=F32)


def _in_proj_kernel(x_ref, g_ref, w_ref, q_ref, k_ref, v_ref, xr_ref, gt_ref, us_ref, *, d_attn, d_rg):
    u = _rms(x_ref[...], g_ref[...]).astype(BF16)
    lo = 0
    for ref, width in ((q_ref, d_attn), (k_ref, d_attn), (v_ref, d_attn),
                       (xr_ref, d_rg), (gt_ref, d_rg), (us_ref, w_ref.shape[1] - 3 * d_attn - 2 * d_rg)):
        ref[...] = _dot(u, w_ref[:, lo:lo + width].astype(BF16)).astype(ref.dtype)
        lo += width


def _layer_spec(stacked, layer):
    return pl.BlockSpec((None,) + stacked.shape[1:], lambda *_: (layer, 0, 0), pipeline_mode=pl.Buffered(1))


def _in_proj(x2, g, w_all, layer, d_attn, d_rg, d_s5):
    n, dm = x2.shape
    tm = min(ROW_TILE, n)
    row = lambda w: pl.BlockSpec((tm, w), lambda i: (i, 0))
    return pl.pallas_call(
        functools.partial(_in_proj_kernel, d_attn=d_attn, d_rg=d_rg),
        grid=(n // tm,),
        in_specs=[row(dm), _const_spec((1, dm)), _layer_spec(w_all, layer)],
        out_specs=[row(d_attn), row(d_attn), row(d_attn), row(d_rg), row(d_rg), row(d_s5)],
        out_shape=[jax.ShapeDtypeStruct((n, d_attn), BF16)] * 3
        + [jax.ShapeDtypeStruct((n, d_rg), F32)] * 2 + [jax.ShapeDtypeStruct((n, d_s5), F32)],
        compiler_params=_cparams("parallel"),
        name="in_proj",
    )(x2, g.reshape(1, dm), w_all)


def _attn_kernel(q_ref, kp_ref, kc_ref, vp_ref, vc_ref, bias_ref, o_ref, kz, vz, *, n_pairs):
    qi = pl.program_id(1)
    tq = q_ref.shape[0]
    kz[0:tq, :] = kp_ref[...]
    kz[tq:2 * tq, :] = kc_ref[...]
    vz[0:tq, :] = vp_ref[...]
    vz[tq:2 * tq, :] = vc_ref[...]
    lane = lax.broadcasted_iota(I32, (CHUNK, LANES), 1)
    first = lane < HEAD_DIM
    kpos = lax.broadcasted_iota(I32, (1, BAND), 1)
    pairs = [slice(hp * LANES, (hp + 1) * LANES) for hp in range(n_pairs)]

    def chunk_body(c, carry, *, masked):
        r0 = pl.multiple_of(c * CHUNK, CHUNK)
        scores = []
        for hp, cols in enumerate(pairs):
            q2 = q_ref[pl.ds(r0, CHUNK), cols] * jnp.asarray(HEAD_DIM ** -0.5, BF16)
            zero = jnp.zeros_like(q2)
            qq = jnp.concatenate([jnp.where(first, q2, zero), jnp.where(first, zero, q2)], axis=0)
            s = lax.dot_general(qq, kz[pl.ds(r0, BAND), cols], (((1,), (1,)), ((), ())), preferred_element_type=F32)
            scores.append(s + bias_ref[hp])
        if masked:
            neg = jnp.where(kpos >= N_PREV * CHUNK - c * CHUNK, 0.0, NEG_BIG)
            scores = [s + neg for s in scores]
        probs, sums = [], []
        for s in scores:
            p = jnp.exp(s - jnp.max(s, axis=-1, keepdims=True))
            sums.append(jnp.sum(p, axis=-1, keepdims=True))
            probs.append(p.astype(BF16))
        for cols, p, l in zip(pairs, probs, sums):
            o2 = _dot(p, vz[pl.ds(r0, BAND), cols]) / l
            o_ref[pl.ds(r0, CHUNK), cols] = jnp.where(first, o2[0:CHUNK], o2[CHUNK:2 * CHUNK])
        return carry

    @pl.when(qi == 0)
    def _():
        lax.fori_loop(0, tq // CHUNK, functools.partial(chunk_body, masked=True), 0)

    @pl.when(qi != 0)
    def _():
        lax.fori_loop(0, tq // CHUNK, functools.partial(chunk_body, masked=False), 0, unroll=2)


def _attention(q, k, v, bias2, bsz, l):
    n, da = q.shape
    tq = ATTN_TQ
    nt = l // tq
    n_pairs = da // LANES
    cur = pl.BlockSpec((tq, da), lambda b, i: (b * nt + i, 0))
    prev = pl.BlockSpec((tq, da), lambda b, i: (b * nt + jnp.maximum(i - 1, 0), 0))
    return pl.pallas_call(
        functools.partial(_attn_kernel, n_pairs=n_pairs),
        grid=(bsz, nt),
        in_specs=[cur, prev, cur, prev, cur, _const_spec(bias2.shape)],
        out_specs=cur,
        out_shape=jax.ShapeDtypeStruct((n, da), F32),
        scratch_shapes=[pltpu.VMEM((2 * tq, da), BF16), pltpu.VMEM((2 * tq, da), BF16)],
        compiler_params=_cparams("parallel", "parallel"),
        name="chunk_attn",
    )(q, k, k, v, v, bias2)


def _attn_bias(rel_bias):
    h = rel_bias.shape[0]
    tab = rel_bias.astype(F32)
    n_far = N_PREV * CHUNK - REL_CLIP + CHUNK
    lo = 2 * REL_CLIP - (BAND + CHUNK - 1 - n_far)
    ext = jnp.concatenate([jnp.broadcast_to(tab[:, 2 * REL_CLIP:], (h, n_far)), tab[:, lo:2 * REL_CLIP][:, ::-1]], axis=1)
    wide = BAND + CHUNK
    ring = jnp.concatenate([ext[:, CHUNK - 1:], jnp.zeros((h, 1), F32), ext[:, :CHUNK - 1]], axis=1)
    bias = jnp.broadcast_to(ring[:, None, :], (h, CHUNK, wide)).reshape(h, CHUNK * wide)
    bias = bias[:, :CHUNK * (wide - 1)].reshape(h, CHUNK, wide - 1)[..., :BAND]
    return bias.reshape(h // 2, 2 * CHUNK, BAND)


def _rglru_kernel(x_ref, gt_ref, cw_ref, cb_ref, w_ref, b_ref, sp_ref, o_ref, xpad, a_s, b_s, hcar):
    t = pl.program_id(1)
    tl, c = x_ref.shape
    seg = tl // RG_SEG
    front = 8

    @pl.when(t == 0)
    def _():
        xpad[0:front, :] = jnp.zeros((front, c), F32)
        hcar[...] = jnp.zeros_like(hcar)

    xpad[front:front + tl, :] = x_ref[...]
    xc = cb_ref[...] + sum(
        cw_ref[j:j + 1, :] * xpad[front - (RG_CONV_WIDTH - 1) + j:front - (RG_CONV_WIDTH - 1) + j + tl, :]
        for j in range(RG_CONV_WIDTH))
    xpad[0:front, :] = xpad[tl:tl + front, :]
    pre = _dot(xc.astype(BF16), w_ref[...]) + b_ref[...]
    gx = _sigmoid(pre[:, 0:c])
    ga = _sigmoid(pre[:, c:2 * c])
    log_a = -RG_C * ga * sp_ref[...]
    a = jnp.exp(log_a)
    mult = jnp.sqrt(-jnp.tanh(log_a) * (a * a + 1.0))
    b = mult * gx * xc
    nh = c // LANES
    for j in range(nh):
        a_s[j] = a[:, j * LANES:(j + 1) * LANES]
        b_s[j] = b[:, j * LANES:(j + 1) * LANES]

    def step(i, carry):
        rows = pl.ds(i, RG_SEG, stride=seg)
        out = []
        for j in range(nh):
            h, p = carry[j]
            av = a_s[j, rows, :]
            h = av * h + b_s[j, rows, :]
            p = av * p
            b_s[j, rows, :] = h
            a_s[j, rows, :] = p
            out.append((h, p))
        return tuple(out)

    init = tuple((jnp.zeros((RG_SEG, LANES), F32), jnp.ones((RG_SEG, LANES), F32)) for _ in range(nh))
    ends = lax.fori_loop(0, seg, step, init, unroll=8)
    for j in range(nh):
        cols = slice(j * LANES, (j + 1) * LANES)
        h_end, p_end = ends[j]
        carry = hcar[:, cols]
        for s in range(RG_SEG):
            rows = slice(s * seg, (s + 1) * seg)
            h = b_s[j, rows, :] + a_s[j, rows, :] * carry
            o_ref[rows, cols] = h * _gelu(gt_ref[rows, cols])
            carry = h_end[s:s + 1, :] + p_end[s:s + 1, :] * carry
        hcar[:, cols] = carry


def _rglru(xr, gate, conv_w, conv_b, w_bd, b_cat, sp, bsz, l):
    n, c = xr.shape
    tl = min(RG_TL, l)
    nt = l // tl
    row = pl.BlockSpec((tl, c), lambda b, t: (b * nt + t, 0))
    return pl.pallas_call(
        _rglru_kernel,
        grid=(bsz, nt),
        in_specs=[row, row, _const_spec(conv_w.shape), _const_spec((1, c)), _const_spec(w_bd.shape),
                  _const_spec((1, 2 * c)), _const_spec((1, c))],
        out_specs=row,
        out_shape=jax.ShapeDtypeStruct((n, c), F32),
        scratch_shapes=[pltpu.VMEM((tl + 8, c), F32), pltpu.VMEM((c // LANES, tl, LANES), F32),
                        pltpu.VMEM((c // LANES, tl, LANES), F32), pltpu.VMEM((1, c), F32)],
        compiler_params=_cparams("parallel", "arbitrary"),
        name="rglru",
    )(xr, gate, conv_w, conv_b.reshape(1, c), w_bd, b_cat.reshape(1, 2 * c), sp.reshape(1, c))


def _block_diag(w):
    nb, d, _ = w.shape
    eye = jnp.eye(nb, dtype=w.dtype)
    return (eye[:, None, :, None] * w[:, :, None, :]).reshape(nb * d, nb * d)


def _s5_kernel(u_ref, m_ref, wa_ref, wb_ref, wo_ref, c1_ref, c2a_ref, c2b_ref, d_ref, y_ref, ps, ug, xa_s, xb_s):
    t, gd = S5_T, S5_GROUP_DIM
    ng, nk, _ = ug.shape
    nh = ps.shape[0]
    per_half = LANES // gd
    rows8 = 8

    for h in range(nh):
        ps[h] = u_ref[:, h * LANES:(h + 1) * LANES]

    lane_block = lax.broadcasted_iota(I32, (rows8, LANES), 1) // gd

    def block_transpose(vs):
        d = per_half // 2
        while d:
            keep = (lane_block & d) == 0
            out = list(vs)
            for i in range(per_half):
                if i & d == 0:
                    out[i] = jnp.where(keep, vs[i], pltpu.roll(vs[i + d], d * gd, 1))
                    out[i + d] = jnp.where(keep, pltpu.roll(vs[i], LANES - d * gd, 1), vs[i + d])
            vs, d = out, d // 2
        return vs

    def to_groups(r, carry):
        base = pl.multiple_of(r * rows8 * t, rows8 * t)
        rows = pl.ds(pl.multiple_of(r * rows8, rows8), rows8)
        for h in range(nh):
            for j in range(t // per_half):
                steps = [ps[h, pl.ds(base + j * per_half + i, rows8, stride=t), :] for i in range(per_half)]
                for k, v in enumerate(block_transpose(steps)):
                    ug[h * per_half + k, rows, j * LANES:(j + 1) * LANES] = v
        return carry

    lax.fori_loop(0, nk // rows8, to_groups, 0, unroll=2)

    def project(g, carry):
        u = ug[g]
        ub = u.astype(BF16)
        rows = pl.ds(pl.multiple_of(g * nk, nk), nk)
        xa_s[rows, :] = _dot(ub, wa_ref[g])
        xb_s[rows, :] = _dot(ub, wb_ref[g])
        ug[g] = _dot(ub, m_ref[g]) + d_ref[g] * u
        return carry

    lax.fori_loop(0, ng, project, 0)

    c1, c2a, c2b = c1_ref[...], c2a_ref[...], c2b_ref[...]

    def step(k, carry):
        xa, xb = carry
        rows = pl.ds(k, ng, stride=nk)
        ia = xa_s[rows, :]
        ib = xb_s[rows, :]
        xa_s[rows, :] = xa
        return c1 * xa + c2a * xb + ia, c1 * xb + c2b * xa + ib

    z = jnp.zeros((ng, xa_s.shape[1]), F32)
    lax.fori_loop(0, nk, step, (z, z), unroll=4)

    def respond(g, carry):
        rows = pl.ds(pl.multiple_of(g * nk, nk), nk)
        ug[g] = _gelu(ug[g] + _dot(xa_s[rows, :].astype(BF16), wo_ref[g]))
        return carry

    lax.fori_loop(0, ng, respond, 0)

    def from_groups(r, carry):
        base = pl.multiple_of(r * rows8 * t, rows8 * t)
        rows = pl.ds(pl.multiple_of(r * rows8, rows8), rows8)
        for h in range(nh):
            for j in range(t // per_half):
                groups = [ug[h * per_half + k, rows, j * LANES:(j + 1) * LANES] for k in range(per_half)]
                for i, v in enumerate(block_transpose(groups)):
                    ps[h, pl.ds(base + j * per_half + i, rows8, stride=t), :] = v
        return carry

    lax.fori_loop(0, nk // rows8, from_groups, 0, unroll=2)
    for h in range(nh):
        y_ref[:, h * LANES:(h + 1) * LANES] = ps[h]


def _s5(us, prm, bsz, l):
    n, w = us.shape
    ng = w // S5_GROUP_DIM
    nk = l // S5_T
    p2 = prm[1].shape[2]
    row = pl.BlockSpec((l, w), lambda b: (b, 0))
    return pl.pallas_call(
        _s5_kernel,
        grid=(bsz,),
        in_specs=[row] + [_const_spec(a.shape) for a in prm],
        out_specs=row,
        out_shape=jax.ShapeDtypeStruct((n, w), F32),
        scratch_shapes=[pltpu.VMEM((w // LANES, l, LANES), F32), pltpu.VMEM((ng, nk, S5_T * S5_GROUP_DIM), F32),
                        pltpu.VMEM((ng * nk, p2), F32), pltpu.VMEM((ng * nk, p2), F32)],
        compiler_params=_cparams("parallel"),
        name="s5",
    )(us, *prm)


def _s5_params(a_re, a_im, log_dt, b_re, b_im, c_re, c_im, d):
    t = S5_T
    g, p = a_re.shape
    gd = S5_GROUP_DIM
    hi = lax.Precision.HIGHEST
    a = lax.complex(a_re.astype(F32), a_im.astype(F32))
    dt = jnp.exp(log_dt.astype(F32))[:, None]
    a_bar = jnp.exp(a * dt)
    bm = lax.complex(b_re.astype(F32), b_im.astype(F32))
    cm = lax.complex(c_re.astype(F32), c_im.astype(F32))
    b_bar = ((a_bar - 1.0) / a)[..., None] * bm
    steps = jnp.arange(t + 1, dtype=F32)
    pw = jnp.exp((a * dt)[:, None, :] * steps[None, :, None])

    cp = (cm[:, None, :, :] * pw[:, :, None, :]).transpose(0, 3, 1, 2)
    w = t * gd
    lag = cp[:, :, :t].reshape(g, p, w)
    k2 = jnp.einsum('gpc,gpx->gcx', jnp.concatenate([b_bar.real, -b_bar.imag], axis=1),
                    jnp.concatenate([lag.real, lag.imag], axis=1), precision=hi)
    skew = jnp.broadcast_to(jnp.concatenate([k2, jnp.zeros_like(k2)], axis=-1)[:, :, None, :], (g, gd, t, 2 * w))
    skew = skew.reshape(g, gd, t * 2 * w)[:, :, :t * (2 * w - gd)].reshape(g, gd, t, 2 * w - gd)[..., :w]
    m = skew.transpose(0, 2, 1, 3).reshape(g, w, w)
    win = pw[:, :t][:, ::-1][:, :, :, None] * b_bar[:, None]
    win = win.transpose(0, 1, 3, 2).reshape(g, w, p)
    wa = jnp.concatenate([win.real, win.imag], axis=-1)
    wb = jnp.concatenate([win.imag, win.real], axis=-1)
    co = cp[:, :, 1:].reshape(g, p, w)
    wo = jnp.concatenate([co.real, -co.imag], axis=1)
    lt = pw[:, t]
    c1 = jnp.concatenate([lt.real, lt.real], axis=-1)
    c2a = jnp.concatenate([-lt.imag, lt.imag], axis=-1)
    c2b = jnp.concatenate([lt.imag, -lt.imag], axis=-1)
    dtile = jnp.tile(d.astype(F32).reshape(g, 1, gd), (1, t, 1)).reshape(g, 1, t * gd)
    return (m.astype(BF16), wa.astype(BF16), wb.astype(BF16), wo.astype(BF16), c1, c2a, c2b, dtile)


def _out_proj_kernel(ya_ref, yr_ref, ys_ref, x_ref, gg_ref, wglu_ref, wo_ref, gf_ref, *rest, with_router):
    if with_router:
        rt_ref, xo_ref, h_ref, lg_ref = rest
    else:
        xo_ref, h_ref = rest
    ys = ys_ref[...]
    ys = ys * _sigmoid(_dot(ys.astype(BF16), wglu_ref[...].astype(BF16)))
    acc = x_ref[...]
    lo = 0
    for y in (ya_ref[...], yr_ref[...], ys):
        w = y.shape[1]
        acc = acc + _dot(_rms(y, gg_ref[:, lo:lo + w]).astype(BF16), wo_ref[lo:lo + w, :].astype(BF16))
        lo += w
    xo_ref[...] = acc
    h = _rms(acc, gf_ref[...])
    hb = h.astype(BF16)
    h_ref[...] = hb
    if with_router:
        both = _dot(hb, rt_ref[...])
        h_lo = (h - hb.astype(F32)).astype(BF16)
        lg_ref[...] = both[:, 0:LANES] + both[:, LANES:2 * LANES] + _dot(h_lo, rt_ref[:, 0:LANES])


def _out_proj(ya, yr, ys, x2, gg, wglu_all, wo_all, layer, gf, router=None):
    n, dm = x2.shape
    tm = min(ROW_TILE, n)
    row = lambda w: pl.BlockSpec((tm, w), lambda i: (i, 0))
    ins = [ya, yr, ys, x2, gg.reshape(1, -1), wglu_all, wo_all, gf.reshape(1, dm)]
    in_specs = [row(ya.shape[1]), row(yr.shape[1]), row(ys.shape[1]), row(dm), _const_spec((1, gg.shape[0])),
                _layer_spec(wglu_all, layer), _layer_spec(wo_all, layer), _const_spec((1, dm))]
    out_specs = [row(dm), row(dm)]
    out_shape = [jax.ShapeDtypeStruct((n, dm), F32), jax.ShapeDtypeStruct((n, dm), BF16)]
    if router is not None:
        ne = router.shape[1]
        r_hi = router.astype(BF16)
        r_lo = (router.astype(F32) - r_hi.astype(F32)).astype(BF16)
        pad = lambda a: jnp.pad(a, ((0, 0), (0, LANES - ne)))
        ins.append(jnp.concatenate([pad(r_hi), pad(r_lo)], axis=1))
        in_specs.append(_const_spec((dm, 2 * LANES)))
        out_specs.append(row(LANES))
        out_shape.append(jax.ShapeDtypeStruct((n, LANES), F32))
    return pl.pallas_call(
        functools.partial(_out_proj_kernel, with_router=router is not None),
        grid=(n // tm,),
        in_specs=in_specs, out_specs=out_specs, out_shape=out_shape,
        compiler_params=_cparams("parallel"),
        name="out_proj",
    )(*ins)


def _swiglu_chunks(h, w1, w3, w2, chunks, before_chunk=None, after_chunk=None):
    acc = None
    for j, (lo, width) in enumerate(chunks):
        if before_chunk is not None:
            before_chunk(j)
        a = _dot(h, w1[:, lo:lo + width].astype(BF16))
        b = _dot(h, w3[:, lo:lo + width].astype(BF16))
        t = (a * _sigmoid(a) * b).astype(BF16)
        y = _dot(t, w2[lo:lo + width, :].astype(BF16))
        acc = y if acc is None else acc + y
        if after_chunk is not None:
            after_chunk(j)
    return acc


def _ffn_chunks(f):
    if f == sum(w for _, w in FFN_CHUNKS):
        return FFN_CHUNKS
    return ((0, f),)


def _dense_ffn_kernel(h_ref, x_ref, w1_ref, w3_ref, w2_ref, o_ref, *, chunks):
    o_ref[...] = x_ref[...] + _swiglu_chunks(h_ref[...], w1_ref, w3_ref, w2_ref, chunks)


def _dense_ffn(h, x2, w1, w3, w2):
    n, dm = x2.shape
    tm = min(ROW_TILE, n)
    row = pl.BlockSpec((tm, dm), lambda i: (i, 0))
    return pl.pallas_call(
        functools.partial(_dense_ffn_kernel, chunks=_ffn_chunks(w1.shape[1])),
        grid=(n // tm,),
        in_specs=[row, row, _const_spec(w1.shape), _const_spec(w3.shape), _const_spec(w2.shape)],
        out_specs=row,
        out_shape=jax.ShapeDtypeStruct((n, dm), F32),
        compiler_params=_cparams("parallel"),
        name="dense_ffn",
    )(h, x2, w1, w3, w2)


def _expert_weight_copies(hbm, vmem, sems, chunks, expert, j):
    h1, h3, h2 = hbm
    v1, v3, v2 = vmem
    lo, w = chunks[j]
    return [pltpu.make_async_copy(h1.at[expert, :, lo:lo + w], v1.at[:, lo:lo + w], sems.at[j, 0]),
            pltpu.make_async_copy(h3.at[expert, :, lo:lo + w], v3.at[:, lo:lo + w], sems.at[j, 1]),
            pltpu.make_async_copy(h2.at[expert, lo:lo + w, :], v2.at[lo:lo + w, :], sems.at[j, 2])]


def _expert_ffn_kernel(brow_ref, bexp_ref, role_ref, x_ref, w1_hbm, w3_hbm, w2_hbm, o_ref, w1_v, w3_v, w2_v, sems,
                       *, chunks):
    i = pl.program_id(0)
    copies = functools.partial(_expert_weight_copies, (w1_hbm, w3_hbm, w2_hbm), (w1_v, w3_v, w2_v), sems, chunks)

    @pl.when(i == 0)
    def _():
        for j in range(len(chunks)):
            for cp in copies(bexp_ref[0], j):
                cp.start()

    def block(first, last):
        def before(j):
            for cp in copies(bexp_ref[i], j):
                cp.wait()

        def after(j):
            for cp in copies(bexp_ref[i + 1], j):
                cp.start()

        o_ref[...] = _swiglu_chunks(x_ref[...], w1_v, w3_v, w2_v, chunks, before if first else None,
                                    after if last else None).astype(o_ref.dtype)

    for role in range(4):
        pl.when(role_ref[i] == role)(functools.partial(block, bool(role & 1), bool(role & 2)))


def _expert_ffn(xb, w1, w3, w2, blk_row, blk_exp, role, max_blocks):
    rows, dm = xb.shape
    f = w1.shape[2]
    chunks = _ffn_chunks(f)
    row = pl.BlockSpec((MOE_BLK, dm), lambda i, br, be, ro: (br[i], 0))
    hbm = pl.BlockSpec(memory_space=pl.ANY)
    return pl.pallas_call(
        functools.partial(_expert_ffn_kernel, chunks=chunks),
        grid_spec=pltpu.PrefetchScalarGridSpec(
            num_scalar_prefetch=3, grid=(max_blocks,),
            in_specs=[row, hbm, hbm, hbm], out_specs=row,
            scratch_shapes=[pltpu.VMEM(w1.shape[1:], w1.dtype), pltpu.VMEM(w3.shape[1:], w3.dtype),
                            pltpu.VMEM(w2.shape[1:], w2.dtype), pltpu.SemaphoreType.DMA((len(chunks), 3))]),
        out_shape=jax.ShapeDtypeStruct((rows, dm), BF16),
        compiler_params=_cparams("arbitrary"),
        name="expert_ffn",
    )(blk_row, blk_exp, role, xb, w1, w3, w2)


def _route_tile(logits_tok, filled, ne):
    tm = logits_tok.shape[0]
    logits = logits_tok.T[0:ne, :]
    eidx = lax.broadcasted_iota(I32, (ne, tm), 0)
    m1 = jnp.max(logits, axis=0, keepdims=True)
    i1 = jnp.min(jnp.where(logits == m1, eidx, ne), axis=0, keepdims=True)
    sel1 = eidx == i1
    rest = jnp.where(sel1, -jnp.inf, logits)
    m2 = jnp.max(rest, axis=0, keepdims=True)
    i2 = jnp.min(jnp.where(rest == m2, eidx, ne), axis=0, keepdims=True)
    sel2 = eidx == i2
    e2 = jnp.exp(m2 - m1)
    den = 1.0 + e2
    rf = jnp.where(sel1 | sel2, 1.0, 0.0)
    cnt = jnp.sum(rf, axis=1, keepdims=True)
    before = (lax.broadcasted_iota(I32, (tm, tm), 0) < lax.broadcasted_iota(I32, (tm, tm), 1))
    rank = _dot(rf.astype(BF16), jnp.where(before, 1.0, 0.0).astype(BF16))
    whole = lambda a: jnp.floor(a * (1.0 / BF16_ROWS)) * BF16_ROWS
    phase = filled - whole(filled)
    span = jnp.where(cnt > 0, whole(phase + cnt + (BF16_ROWS - 1)), 0.0)
    ecol = lax.broadcasted_iota(I32, (ne, 1), 0)
    off = jnp.zeros((ne, 1), F32)
    for j in range(ne - 1):
        off = off + jnp.where(ecol > j, span[j:j + 1, :], 0.0)
    place = off + phase + rank
    pos1 = jnp.sum(jnp.where(sel1, place, 0.0), axis=0, keepdims=True)
    pos2 = jnp.sum(jnp.where(sel2, place, 0.0), axis=0, keepdims=True)
    lane_major = jnp.concatenate([pos1, pos2, 1.0 / den, e2 / den, jnp.zeros((4, tm), F32)], axis=0)
    tok_major = jnp.concatenate([lane_major, jnp.zeros((LANES - 8, tm), F32)], axis=0).T
    return lane_major, tok_major, cnt


def _route_kernel(lg_ref, lane_ref, tok_ref, cnt_ref, filled, *, ne):
    tm = MOE_TILE

    @pl.when(pl.program_id(0) == 0)
    def _():
        filled[...] = jnp.zeros_like(filled)

    for t in range(lg_ref.shape[0] // tm):
        lane_major, tok_major, cnt = _route_tile(lg_ref[t * tm:(t + 1) * tm, :], filled[...], ne)
        lane_ref[:, t * tm:(t + 1) * tm] = lane_major
        tok_ref[t * tm:(t + 1) * tm, :] = tok_major
        cnt_ref[t] = jnp.broadcast_to(cnt, (ne, LANES)).astype(I32)
        filled[...] = filled[...] + cnt


def _route(logits, ne):
    n = logits.shape[0]
    tm = MOE_TILE
    span = min(ROUTE_SPAN, n)
    return pl.pallas_call(
        functools.partial(_route_kernel, ne=ne),
        grid=(n // span,),
        in_specs=[pl.BlockSpec((span, LANES), lambda i: (i, 0))],
        out_specs=[pl.BlockSpec((8, span), lambda i: (0, i)), pl.BlockSpec((span, LANES), lambda i: (i, 0)),
                   pl.BlockSpec((span // tm, ne, LANES), lambda i: (i, 0, 0))],
        out_shape=[jax.ShapeDtypeStruct((8, n), F32), jax.ShapeDtypeStruct((n, LANES), F32),
                   jax.ShapeDtypeStruct((n // tm, ne, LANES), I32)],
        scratch_shapes=[pltpu.VMEM((ne, 1), F32)],
        compiler_params=_cparams("arbitrary"),
        name="moe_route",
    )(logits)


def _segment_copies(seg_ref, grouped_ref, base_ref, tile_buf, off_ref, sems, tile, slot, ne, *, to_grouped, wait):
    for e in range(ne):
        seg = seg_ref[tile * ne + e]
        far = base_ref[tile * ne + e]
        near = off_ref[tile * ne + e]
        for piece in SEG_PIECES:
            g = grouped_ref.at[pl.ds(pl.multiple_of(far, BF16_ROWS), piece)]
            t = tile_buf.at[slot, pl.ds(pl.multiple_of(near, BF16_ROWS), piece)]
            cp = pltpu.make_async_copy(t, g, sems.at[slot, e]) if to_grouped else \
                pltpu.make_async_copy(g, t, sems.at[slot, e])
            has = (seg & piece) != 0

            @pl.when(has)
            def _():
                cp.wait() if wait else cp.start()

            step = jnp.where(has, piece, 0)
            far = far + step
            near = near + step


def _tile_rows(ne):
    return TOP_K * MOE_TILE + ne * 2 * BF16_ROWS


def _tail_copies(end_ref, zeros, xb_ref, sems, ne):
    return [pltpu.make_async_copy(zeros, xb_ref.at[pl.ds(pl.multiple_of(end_ref[e], BF16_ROWS), MOE_BLK)], sems.at[e])
            for e in range(ne)]


def _dispatch_kernel(seg_ref, base_ref, off_ref, phase_ref, end_ref, rt_ref, h_ref, xb_ref, res, zeros, partial,
                     sems, tail_sems):
    i = pl.program_id(0)
    nt = pl.num_programs(0)
    ne = tail_sems.shape[0]
    tm = h_ref.shape[0]
    mrows = res.shape[1]
    slot = i % 2
    copies = functools.partial(_segment_copies, seg_ref, xb_ref, base_ref, res, off_ref, sems, ne=ne, to_grouped=True)

    @pl.when(i == 0)
    def _():
        zeros[...] = jnp.zeros_like(zeros)
        partial[...] = jnp.zeros_like(partial)
        for cp in _tail_copies(end_ref, zeros, xb_ref, tail_sems, ne):
            cp.start()

    row = lax.broadcasted_iota(I32, (mrows, tm), 0)
    pos = rt_ref[0:2, :].astype(I32)
    perm = jnp.where((row == pos[0:1, :]) | (row == pos[1:2, :]), 1.0, 0.0).astype(BF16)
    res[slot] = _dot(perm, h_ref[...]).astype(BF16)

    tile_row = lax.broadcasted_iota(I32, (BF16_ROWS, res.shape[2]), 0)
    for e in range(ne):
        @pl.when(seg_ref[i * ne + e] > 0)
        def _():
            first = pl.ds(pl.multiple_of(off_ref[i * ne + e], BF16_ROWS), BF16_ROWS)
            merged = jnp.where(tile_row < phase_ref[i * ne + e], partial[e].astype(F32), res[slot, first, :].astype(F32))
            res[slot, first, :] = merged.astype(BF16)
            last = pl.ds(pl.multiple_of(off_ref[i * ne + e] + seg_ref[i * ne + e] - BF16_ROWS, BF16_ROWS), BF16_ROWS)
            partial[e] = res[slot, last, :]

    @pl.when(i > 0)
    def _():
        copies(tile=i - 1, slot=1 - slot, wait=True)

    copies(tile=i, slot=slot, wait=False)

    @pl.when(i == nt - 1)
    def _():
        copies(tile=i, slot=slot, wait=True)
        for cp in _tail_copies(end_ref, zeros, xb_ref, tail_sems, ne):
            cp.wait()


def _dispatch(h, rt_lane, lay, ne):
    n, dm = h.shape
    tm = MOE_TILE
    return pl.pallas_call(
        _dispatch_kernel,
        grid_spec=pltpu.PrefetchScalarGridSpec(
            num_scalar_prefetch=5, grid=(n // tm,),
            in_specs=[pl.BlockSpec((8, tm), lambda i, *_: (0, i)), pl.BlockSpec((tm, dm), lambda i, *_: (i, 0))],
            out_specs=pl.BlockSpec(memory_space=pl.ANY),
            scratch_shapes=[pltpu.VMEM((2, _tile_rows(ne), dm), BF16), pltpu.VMEM((MOE_BLK, dm), BF16),
                            pltpu.VMEM((ne, BF16_ROWS, dm), BF16),
                            pltpu.SemaphoreType.DMA((2, ne)), pltpu.SemaphoreType.DMA((ne,))]),
        out_shape=jax.ShapeDtypeStruct((lay['rows_total'], dm), BF16),
        compiler_params=_cparams("arbitrary"),
        name="moe_dispatch",
    )(lay['seg'], lay['base'], lay['off'], lay['phase'], lay['end'], rt_lane, h)


def _combine_kernel(seg_ref, base_ref, off_ref, rt_ref, x_ref, gn_ref, yb_ref, o_ref, got, sems, *, ne):
    i = pl.program_id(0)
    nt = pl.num_programs(0)
    tm = x_ref.shape[0]
    mrows = got.shape[1]
    slot = i % 2
    copies = functools.partial(_segment_copies, seg_ref, yb_ref, base_ref, got, off_ref, sems, ne=ne, to_grouped=False)

    @pl.when(i == 0)
    def _():
        got[...] = jnp.zeros_like(got)
        copies(tile=i, slot=slot, wait=False)

    @pl.when(i + 1 < nt)
    def _():
        copies(tile=i + 1, slot=1 - slot, wait=False)

    copies(tile=i, slot=slot, wait=True)

    col = lax.broadcasted_iota(I32, (tm, mrows), 1)
    rt = rt_ref[...]
    rows = got[slot]
    sel = jnp.concatenate([jnp.where(col == rt[:, k:k + 1].astype(I32), 1.0, 0.0).astype(BF16) for k in range(TOP_K)],
                          axis=0)
    picked = _dot(sel, rows)
    y = sum(rt[:, TOP_K + k:TOP_K + k + 1] * picked[k * tm:(k + 1) * tm] for k in range(TOP_K))
    o_ref[...] = _rms(x_ref[...] + y, gn_ref[...])


def _combine(yb, rt_tok, x2, gn, lay, ne):
    n, dm = x2.shape
    tm = MOE_TILE
    mrows = _tile_rows(ne)
    tok = lambda w: pl.BlockSpec((tm, w), lambda i, *_: (i, 0))
    return pl.pallas_call(
        functools.partial(_combine_kernel, ne=ne),
        grid_spec=pltpu.PrefetchScalarGridSpec(
            num_scalar_prefetch=3, grid=(n // tm,),
            in_specs=[tok(LANES), tok(dm), pl.BlockSpec((1, dm), lambda i, *_: (0, 0)),
                      pl.BlockSpec(memory_space=pl.ANY)],
            out_specs=tok(dm),
            scratch_shapes=[pltpu.VMEM((2, mrows, dm), BF16), pltpu.SemaphoreType.DMA((2, ne))]),
        out_shape=jax.ShapeDtypeStruct((n, dm), F32),
        compiler_params=_cparams("arbitrary"),
        name="moe_combine",
    )(lay['seg'], lay['base'], lay['off'], rt_tok, x2, gn.reshape(1, dm), yb)


def _moe_layout(cnt, n):
    nt, ne = cnt.shape
    blk = MOE_BLK
    whole = lambda a: a // BF16_ROWS * BF16_ROWS
    cap = -(-(n + BF16_ROWS + blk) // blk) * blk
    region = jnp.arange(ne, dtype=I32) * cap
    filled = jnp.cumsum(cnt, axis=0) - cnt
    phase = filled - whole(filled)
    span = jnp.where(cnt > 0, whole(phase + cnt + BF16_ROWS - 1), 0)
    base = (region[None, :] + filled - phase).astype(I32)
    off = (jnp.cumsum(span, axis=1) - span).astype(I32)
    rows_e = jnp.sum(cnt, axis=0)
    nblk_e = (rows_e + blk - 1) // blk
    end = (region + whole(rows_e + BF16_ROWS - 1)).astype(I32)
    ends = jnp.cumsum(nblk_e)
    max_blocks = TOP_K * n // blk + ne
    bid = jnp.arange(max_blocks + 1, dtype=I32)
    bexp = jnp.minimum(jnp.sum((bid[:, None] >= ends[None, :]).astype(I32), axis=1), ne - 1)
    brow = bexp * (cap // blk) + bid - (ends - nblk_e)[bexp]
    nblk = ends[-1]
    last = jnp.maximum(nblk - 1, 0)
    used = bid < nblk
    first = bid == (ends - nblk_e)[bexp]
    hand_over = (bid == ends[bexp] - 1) & (bid < last)
    role = jnp.where(used, first.astype(I32) + 2 * hand_over.astype(I32), 4).astype(I32)
    bexp = jnp.where(used, bexp, bexp[last]).astype(I32)
    brow = jnp.where(used, brow, brow[last]).astype(I32)
    flat = lambda a: a.reshape(-1)
    return dict(seg=flat(span.astype(I32)), base=flat(base), off=flat(off), phase=flat(phase.astype(I32)), end=end,
                brow=brow, bexp=bexp, role=role, rows_total=ne * cap, max_blocks=max_blocks)


def _moe(x2, h, logits, gn, w1, w3, w2):
    n, dm = x2.shape
    ne = w1.shape[0]
    rt_lane, rt_tok, cnt = _route(logits, ne)
    lay = _moe_layout(cnt[:, :, 0], n)
    xb = _dispatch(h, rt_lane, lay, ne)
    yb = _expert_ffn(xb, w1, w3, w2, lay['brow'], lay['bexp'], lay['role'], lay['max_blocks'])
    return _combine(yb, rt_tok, x2, gn, lay, ne)


def kernel(x, norm_mix_g, w_in, attn_rel_bias, rg_conv_w, rg_conv_b, rg_wx, rg_bx, rg_wa, rg_ba, rg_lambda, s5_a_re, s5_a_im, s5_log_dt, s5_b_re, s5_b_im, s5_c_re, s5_c_im, s5_d, s5_w_glu, g_group, w_out, norm_ffn_g, ffn_w1, ffn_w3, ffn_w2, moe_router, moe_w1, moe_w3, moe_w2, final_norm_g):
    bsz, l, dm = x.shape
    depth = w_in.shape[0]
    assert depth == 2 and l % ATTN_TQ == 0 and ATTN_TQ == N_PREV * CHUNK, "dense layer, then the MoE layer"
    d_rg = rg_conv_w.shape[2]
    d_s5 = s5_w_glu.shape[1]
    d_attn = (w_in.shape[2] - 2 * d_rg - d_s5) // 3
    x2 = x.reshape(bsz * l, dm)
    for layer in range(depth):
        q, k, v, xr, gate, us = _in_proj(x2, norm_mix_g[layer], w_in, layer, d_attn, d_rg, d_s5)
        y_attn = _attention(q, k, v, _attn_bias(attn_rel_bias[layer]), bsz, l)
        w_gates = jnp.concatenate([_block_diag(rg_wx[layer]), _block_diag(rg_wa[layer])], axis=1).astype(BF16)
        b_gates = jnp.concatenate([rg_bx[layer], rg_ba[layer]]).astype(F32)
        y_rg = _rglru(xr, gate, rg_conv_w[layer].astype(F32), rg_conv_b[layer].astype(F32), w_gates, b_gates,
                      jax.nn.softplus(-rg_lambda[layer].astype(F32)), bsz, l)
        prm = _s5_params(s5_a_re[layer], s5_a_im[layer], s5_log_dt[layer], s5_b_re[layer], s5_b_im[layer],
                         s5_c_re[layer], s5_c_im[layer], s5_d[layer])
        y_s5 = _s5(us, prm, bsz, l)
        mixed = (y_attn, y_rg, y_s5, x2, g_group[layer].astype(F32), s5_w_glu, w_out, layer,
                 norm_ffn_g[layer].astype(F32))
        if layer == 0:
            x2, h = _out_proj(*mixed)
            x2 = _dense_ffn(h, x2, ffn_w1[0], ffn_w3[0], ffn_w2[0])
        else:
            x2, h, logits = _out_proj(*mixed, router=moe_router[0])
            x2 = _moe(x2, h, logits, final_norm_g.astype(F32), moe_w1[0], moe_w3[0], moe_w2[0])
    return x2.reshape(bsz, l, dm)
```

```python
import functools
import math

import jax
import jax.numpy as jnp
import numpy as np
from jax import lax
from jax.experimental import pallas as pl
from jax.experimental.pallas import tpu as pltpu

F32 = jnp.float32
BF16 = jnp.bfloat16
I32 = jnp.int32

EPS = 1e-6
CHUNK = 64
N_PREV = 8
BAND = (N_PREV + 1) * CHUNK
REL_CLIP = 128
HEAD_DIM = 64
RG_C = 8.0
RG_CONV_WIDTH = 4
S5_GROUP_DIM = 16
S5_T = 16
TOP_K = 2
NEG_BIG = -1e30

LANES = 128
BF16_ROWS = 16
VMEM_LIMIT = 52 * 1024 * 1024

ROW_TILE = 512
ATTN_TQ = 512
RG_TL = 1024
RG_SEG = 8
MOE_TILE = 256
MOE_BLK = 256
ROUTE_SPAN = 2048
SEG_PIECES = tuple(BF16_ROWS << b for b in range((MOE_TILE // BF16_ROWS).bit_length() - 1, -1, -1))
FFN_CHUNKS = ((0, 512), (512, 512), (1024, 512), (1536, 512), (2048, 512), (2560, 256))


def _cparams(*sem):
    return pltpu.CompilerParams(dimension_semantics=sem, vmem_limit_bytes=VMEM_LIMIT)


def _const_spec(shape):
    nd = len(shape)
    return pl.BlockSpec(shape, lambda *_: (0,) * nd, pipeline_mode=pl.Buffered(1))


def _rms(xf, g):
    var = jnp.mean(xf * xf, axis=-1, keepdims=True)
    return xf * lax.rsqrt(var + EPS) * g


def _sigmoid(x):
    return 1.0 / (1.0 + jnp.exp(-x))


def _gelu(x):
    c = math.sqrt(2.0 / math.pi)
    return 0.5 * x * (1.0 + jnp.tanh(c * (x + 0.044715 * (x * x * x))))


def _dot(a, b):
    return jnp.dot(a, b, preferred_element_type=F32)


def _in_proj_kernel(x_ref, g_ref, w_ref, q_ref, k_ref, v_ref, xr_ref, gt_ref, us_ref, *, d_attn, d_rg):
    u = _rms(x_ref[...], g_ref[...]).astype(BF16)
    lo = 0
    for ref, width in ((q_ref, d_attn), (k_ref, d_attn), (v_ref, d_attn),
                       (xr_ref, d_rg), (gt_ref, d_rg), (us_ref, w_ref.shape[1] - 3 * d_attn - 2 * d_rg)):
        ref[...] = _dot(u, w_ref[:, lo:lo + width].astype(BF16)).astype(ref.dtype)
        lo += width


def _layer_spec(stacked, layer):
    return pl.BlockSpec((None,) + stacked.shape[1:], lambda *_: (layer, 0, 0), pipeline_mode=pl.Buffered(1))


def _in_proj(x2, g, w_all, layer, d_attn, d_rg, d_s5):
    n, dm = x2.shape
    tm = min(ROW_TILE, n)
    row = lambda w: pl.BlockSpec((tm, w), lambda i: (i, 0))
    return pl.pallas_call(
        functools.partial(_in_proj_kernel, d_attn=d_attn, d_rg=d_rg),
        grid=(n // tm,),
        in_specs=[row(dm), _const_spec((1, dm)), _layer_spec(w_all, layer)],
        out_specs=[row(d_attn), row(d_attn), row(d_attn), row(d_rg), row(d_rg), row(d_s5)],
        out_shape=[jax.ShapeDtypeStruct((n, d_attn), BF16)] * 3
        + [jax.ShapeDtypeStruct((n, d_rg), F32)] * 2 + [jax.ShapeDtypeStruct((n, d_s5), F32)],
        compiler_params=_cparams("parallel"),
        name="in_proj",
    )(x2, g.reshape(1, dm), w_all)


def _attn_kernel(q_ref, kp_ref, kc_ref, vp_ref, vc_ref, bias_ref, o_ref, kz, vz, *, n_pairs):
    qi = pl.program_id(1)
    tq = q_ref.shape[0]
    kz[0:tq, :] = kp_ref[...]
    kz[tq:2 * tq, :] = kc_ref[...]
    vz[0:tq, :] = vp_ref[...]
    vz[tq:2 * tq, :] = vc_ref[...]
    lane = lax.broadcasted_iota(I32, (CHUNK, LANES), 1)
    first = lane < HEAD_DIM
    kpos = lax.broadcasted_iota(I32, (1, BAND), 1)
    pairs = [slice(hp * LANES, (hp + 1) * LANES) for hp in range(n_pairs)]

    def chunk_body(c, carry, *, masked):
        r0 = pl.multiple_of(c * CHUNK, CHUNK)
        scores = []
        for hp, cols in enumerate(pairs):
            q2 = q_ref[pl.ds(r0, CHUNK), cols] * jnp.asarray(HEAD_DIM ** -0.5, BF16)
            zero = jnp.zeros_like(q2)
            qq = jnp.concatenate([jnp.where(first, q2, zero), jnp.where(first, zero, q2)], axis=0)
            s = lax.dot_general(qq, kz[pl.ds(r0, BAND), cols], (((1,), (1,)), ((), ())), preferred_element_type=F32)
            scores.append(s + bias_ref[hp])
        if masked:
            neg = jnp.where(kpos >= N_PREV * CHUNK - c * CHUNK, 0.0, NEG_BIG)
            scores = [s + neg for s in scores]
        probs, sums = [], []
        for s in scores:
            p = jnp.exp(s - jnp.max(s, axis=-1, keepdims=True))
            sums.append(jnp.sum(p, axis=-1, keepdims=True))
            probs.append(p.astype(BF16))
        for cols, p, l in zip(pairs, probs, sums):
            o2 = _dot(p, vz[pl.ds(r0, BAND), cols]) / l
            o_ref[pl.ds(r0, CHUNK), cols] = jnp.where(first, o2[0:CHUNK], o2[CHUNK:2 * CHUNK]).astype(o_ref.dtype)
        return carry

    @pl.when(qi == 0)
    def _():
        lax.fori_loop(0, tq // CHUNK, functools.partial(chunk_body, masked=True), 0)

    @pl.when(qi != 0)
    def _():
        lax.fori_loop(0, tq // CHUNK, functools.partial(chunk_body, masked=False), 0, unroll=2)


def _attention(q, k, v, bias2, bsz, l):
    n, da = q.shape
    tq = ATTN_TQ
    nt = l // tq
    n_pairs = da // LANES
    cur = pl.BlockSpec((tq, da), lambda b, i: (b * nt + i, 0))
    prev = pl.BlockSpec((tq, da), lambda b, i: (b * nt + jnp.maximum(i - 1, 0), 0))
    return pl.pallas_call(
        functools.partial(_attn_kernel, n_pairs=n_pairs),
        grid=(bsz, nt),
        in_specs=[cur, prev, cur, prev, cur, _const_spec(bias2.shape)],
        out_specs=cur,
        out_shape=jax.ShapeDtypeStruct((n, da), BF16),
        scratch_shapes=[pltpu.VMEM((2 * tq, da), BF16), pltpu.VMEM((2 * tq, da), BF16)],
        compiler_params=_cparams("parallel", "parallel"),
        name="chunk_attn",
    )(q, k, k, v, v, bias2)


def _attn_bias(rel_bias):
    h = rel_bias.shape[0]
    tab = rel_bias.astype(F32)
    n_far = N_PREV * CHUNK - REL_CLIP + CHUNK
    lo = 2 * REL_CLIP - (BAND + CHUNK - 1 - n_far)
    ext = jnp.concatenate([jnp.broadcast_to(tab[:, 2 * REL_CLIP:], (h, n_far)), tab[:, lo:2 * REL_CLIP][:, ::-1]], axis=1)
    wide = BAND + CHUNK
    ring = jnp.concatenate([ext[:, CHUNK - 1:], jnp.zeros((h, 1), F32), ext[:, :CHUNK - 1]], axis=1)
    bias = jnp.broadcast_to(ring[:, None, :], (h, CHUNK, wide)).reshape(h, CHUNK * wide)
    bias = bias[:, :CHUNK * (wide - 1)].reshape(h, CHUNK, wide - 1)[..., :BAND]
    return bias.reshape(h // 2, 2 * CHUNK, BAND)


def _rglru_kernel(x_ref, gt_ref, cw_ref, cb_ref, w_ref, b_ref, sp_ref, o_ref, xpad, a_s, b_s, hcar):
    t = pl.program_id(1)
    tl, c = x_ref.shape
    seg = tl // RG_SEG
    front = 8

    @pl.when(t == 0)
    def _():
        xpad[0:front, :] = jnp.zeros((front, c), F32)
        hcar[...] = jnp.zeros_like(hcar)

    xpad[front:front + tl, :] = x_ref[...]
    xc = cb_ref[...] + sum(
        cw_ref[j:j + 1, :] * xpad[front - (RG_CONV_WIDTH - 1) + j:front - (RG_CONV_WIDTH - 1) + j + tl, :]
        for j in range(RG_CONV_WIDTH))
    xpad[0:front, :] = xpad[tl:tl + front, :]
    pre = _dot(xc.astype(BF16), w_ref[...]) + b_ref[...]
    gx = _sigmoid(pre[:, 0:c])
    ga = _sigmoid(pre[:, c:2 * c])
    log_a = -RG_C * ga * sp_ref[...]
    a = jnp.exp(log_a)
    mult = jnp.sqrt(-jnp.tanh(log_a) * (a * a + 1.0))
    b = mult * gx * xc
    nh = c // LANES
    for j in range(nh):
        a_s[j] = a[:, j * LANES:(j + 1) * LANES]
        b_s[j] = b[:, j * LANES:(j + 1) * LANES]

    def step(i, carry):
        rows = pl.ds(i, RG_SEG, stride=seg)
        out = []
        for j in range(nh):
            h, p = carry[j]
            av = a_s[j, rows, :]
            h = av * h + b_s[j, rows, :]
            p = av * p
            b_s[j, rows, :] = h
            a_s[j, rows, :] = p
            out.append((h, p))
        return tuple(out)

    init = tuple((jnp.zeros((RG_SEG, LANES), F32), jnp.ones((RG_SEG, LANES), F32)) for _ in range(nh))
    ends = lax.fori_loop(0, seg, step, init, unroll=8)
    for j in range(nh):
        cols = slice(j * LANES, (j + 1) * LANES)
        h_end, p_end = ends[j]
        carry = hcar[:, cols]
        for s in range(RG_SEG):
            rows = slice(s * seg, (s + 1) * seg)
            h = b_s[j, rows, :] + a_s[j, rows, :] * carry
            o_ref[rows, cols] = (h * _gelu(gt_ref[rows, cols])).astype(o_ref.dtype)
            carry = h_end[s:s + 1, :] + p_end[s:s + 1, :] * carry
        hcar[:, cols] = carry


def _rglru(xr, gate, conv_w, conv_b, w_bd, b_cat, sp, bsz, l):
    n, c = xr.shape
    tl = min(RG_TL, l)
    nt = l // tl
    row = pl.BlockSpec((tl, c), lambda b, t: (b * nt + t, 0))
    return pl.pallas_call(
        _rglru_kernel,
        grid=(bsz, nt),
        in_specs=[row, row, _const_spec(conv_w.shape), _const_spec((1, c)), _const_spec(w_bd.shape),
                  _const_spec((1, 2 * c)), _const_spec((1, c))],
        out_specs=row,
        out_shape=jax.ShapeDtypeStruct((n, c), BF16),
        scratch_shapes=[pltpu.VMEM((tl + 8, c), F32), pltpu.VMEM((c // LANES, tl, LANES), F32),
                        pltpu.VMEM((c // LANES, tl, LANES), F32), pltpu.VMEM((1, c), F32)],
        compiler_params=_cparams("parallel", "arbitrary"),
        name="rglru",
    )(xr, gate, conv_w, conv_b.reshape(1, c), w_bd, b_cat.reshape(1, 2 * c), sp.reshape(1, c))


def _block_diag(w):
    nb, d, _ = w.shape
    eye = jnp.eye(nb, dtype=w.dtype)
    return (eye[:, None, :, None] * w[:, :, None, :]).reshape(nb * d, nb * d)


def _s5_kernel(u_ref, m_ref, wa_ref, wb_ref, wo_ref, c1_ref, c2a_ref, c2b_ref, d_ref, y_ref, ps, ug, xa_s, xb_s):
    t, gd = S5_T, S5_GROUP_DIM
    ng, nk, _ = ug.shape
    nh = ps.shape[0]
    per_half = LANES // gd
    rows8 = 8

    for h in range(nh):
        ps[h] = u_ref[:, h * LANES:(h + 1) * LANES]

    lane_block = lax.broadcasted_iota(I32, (rows8, LANES), 1) // gd

    def block_transpose(vs):
        d = per_half // 2
        while d:
            keep = (lane_block & d) == 0
            out = list(vs)
            for i in range(per_half):
                if i & d == 0:
                    out[i] = jnp.where(keep, vs[i], pltpu.roll(vs[i + d], d * gd, 1))
                    out[i + d] = jnp.where(keep, pltpu.roll(vs[i], LANES - d * gd, 1), vs[i + d])
            vs, d = out, d // 2
        return vs

    def to_groups(r, carry):
        base = pl.multiple_of(r * rows8 * t, rows8 * t)
        rows = pl.ds(pl.multiple_of(r * rows8, rows8), rows8)
        for h in range(nh):
            for j in range(t // per_half):
                steps = [ps[h, pl.ds(base + j * per_half + i, rows8, stride=t), :] for i in range(per_half)]
                for k, v in enumerate(block_transpose(steps)):
                    ug[h * per_half + k, rows, j * LANES:(j + 1) * LANES] = v
        return carry

    lax.fori_loop(0, nk // rows8, to_groups, 0, unroll=2)

    def project(g, carry):
        u = ug[g]
        ub = u.astype(BF16)
        rows = pl.ds(pl.multiple_of(g * nk, nk), nk)
        xa_s[rows, :] = _dot(ub, wa_ref[g])
        xb_s[rows, :] = _dot(ub, wb_ref[g])
        ug[g] = _dot(ub, m_ref[g]) + d_ref[g] * u
        return carry

    lax.fori_loop(0, ng, project, 0)

    c1, c2a, c2b = c1_ref[...], c2a_ref[...], c2b_ref[...]

    def step(k, carry):
        xa, xb = carry
        rows = pl.ds(k, ng, stride=nk)
        ia = xa_s[rows, :]
        ib = xb_s[rows, :]
        xa_s[rows, :] = xa
        return c1 * xa + c2a * xb + ia, c1 * xb + c2b * xa + ib

    z = jnp.zeros((ng, xa_s.shape[1]), F32)
    lax.fori_loop(0, nk, step, (z, z), unroll=4)

    def respond(g, carry):
        rows = pl.ds(pl.multiple_of(g * nk, nk), nk)
        ug[g] = _gelu(ug[g] + _dot(xa_s[rows, :].astype(BF16), wo_ref[g]))
        return carry

    lax.fori_loop(0, ng, respond, 0)

    def from_groups(r, carry):
        base = pl.multiple_of(r * rows8 * t, rows8 * t)
        rows = pl.ds(pl.multiple_of(r * rows8, rows8), rows8)
        for h in range(nh):
            for j in range(t // per_half):
                groups = [ug[h * per_half + k, rows, j * LANES:(j + 1) * LANES] for k in range(per_half)]
                for i, v in enumerate(block_transpose(groups)):
                    ps[h, pl.ds(base + j * per_half + i, rows8, stride=t), :] = v
        return carry

    lax.fori_loop(0, nk // rows8, from_groups, 0, unroll=2)
    for h in range(nh):
        y_ref[:, h * LANES:(h + 1) * LANES] = ps[h].astype(y_ref.dtype)


def _s5(us, prm, bsz, l):
    n, w = us.shape
    ng = w // S5_GROUP_DIM
    nk = l // S5_T
    p2 = prm[1].shape[2]
    row = pl.BlockSpec((l, w), lambda b: (b, 0))
    return pl.pallas_call(
        _s5_kernel,
        grid=(bsz,),
        in_specs=[row] + [_const_spec(a.shape) for a in prm],
        out_specs=row,
        out_shape=jax.ShapeDtypeStruct((n, w), BF16),
        scratch_shapes=[pltpu.VMEM((w // LANES, l, LANES), F32), pltpu.VMEM((ng, nk, S5_T * S5_GROUP_DIM), F32),
                        pltpu.VMEM((ng * nk, p2), F32), pltpu.VMEM((ng * nk, p2), F32)],
        compiler_params=_cparams("parallel"),
        name="s5",
    )(us, *prm)


def _s5_params(a_re, a_im, log_dt, b_re, b_im, c_re, c_im, d):
    t = S5_T
    g, p = a_re.shape
    gd = S5_GROUP_DIM
    hi = lax.Precision.HIGHEST
    a = lax.complex(a_re.astype(F32), a_im.astype(F32))
    dt = jnp.exp(log_dt.astype(F32))[:, None]
    a_bar = jnp.exp(a * dt)
    bm = lax.complex(b_re.astype(F32), b_im.astype(F32))
    cm = lax.complex(c_re.astype(F32), c_im.astype(F32))
    b_bar = ((a_bar - 1.0) / a)[..., None] * bm
    steps = jnp.arange(t + 1, dtype=F32)
    pw = jnp.exp((a * dt)[:, None, :] * steps[None, :, None])

    cp = (cm[:, None, :, :] * pw[:, :, None, :]).transpose(0, 3, 1, 2)
    w = t * gd
    lag = cp[:, :, :t].reshape(g, p, w)
    k2 = jnp.einsum('gpc,gpx->gcx', jnp.concatenate([b_bar.real, -b_bar.imag], axis=1),
                    jnp.concatenate([lag.real, lag.imag], axis=1), precision=hi)
    shift = np.zeros((t, w, w), np.float32)
    for s in range(t):
        shift[s, np.arange(w - s * gd), np.arange(w - s * gd) + s * gd] = 1.0
    m = jnp.einsum('gcx,sxl->gscl', k2, shift, precision=hi).reshape(g, w, w)
    win = pw[:, :t][:, ::-1][:, :, :, None] * b_bar[:, None]
    win = win.transpose(0, 1, 3, 2).reshape(g, w, p)
    wa = jnp.concatenate([win.real, win.imag], axis=-1)
    wb = jnp.concatenate([win.imag, win.real], axis=-1)
    co = cp[:, :, 1:].reshape(g, p, w)
    wo = jnp.concatenate([co.real, -co.imag], axis=1)
    lt = pw[:, t]
    c1 = jnp.concatenate([lt.real, lt.real], axis=-1)
    c2a = jnp.concatenate([-lt.imag, lt.imag], axis=-1)
    c2b = jnp.concatenate([lt.imag, -lt.imag], axis=-1)
    dtile = jnp.tile(d.astype(F32).reshape(g, 1, gd), (1, t, 1)).reshape(g, 1, t * gd)
    return (m.astype(BF16), wa.astype(BF16), wb.astype(BF16), wo.astype(BF16), c1, c2a, c2b, dtile)


def _out_proj_kernel(ya_ref, yr_ref, ys_ref, x_ref, gg_ref, wglu_ref, wo_ref, gf_ref, *rest, with_router):
    if with_router:
        rt_ref, xo_ref, h_ref, lg_ref = rest
    else:
        xo_ref, h_ref = rest
    ys = ys_ref[...]
    ys = ys.astype(F32) * _sigmoid(_dot(ys.astype(BF16), wglu_ref[...].astype(BF16)))
    acc = x_ref[...]
    lo = 0
    for y in (ya_ref[...].astype(F32), yr_ref[...].astype(F32), ys):
        w = y.shape[1]
        acc = acc + _dot(_rms(y, gg_ref[:, lo:lo + w]).astype(BF16), wo_ref[lo:lo + w, :].astype(BF16))
        lo += w
    xo_ref[...] = acc
    h = _rms(acc, gf_ref[...])
    hb = h.astype(BF16)
    h_ref[...] = hb
    if with_router:
        both = _dot(hb, rt_ref[...])
        h_lo = (h - hb.astype(F32)).astype(BF16)
        lg_ref[...] = both[:, 0:LANES] + both[:, LANES:2 * LANES] + _dot(h_lo, rt_ref[:, 0:LANES])


def _out_proj(ya, yr, ys, x2, gg, wglu_all, wo_all, layer, gf, router=None):
    n, dm = x2.shape
    tm = min(ROW_TILE, n)
    row = lambda w: pl.BlockSpec((tm, w), lambda i: (i, 0))
    ins = [ya, yr, ys, x2, gg.reshape(1, -1), wglu_all, wo_all, gf.reshape(1, dm)]
    in_specs = [row(ya.shape[1]), row(yr.shape[1]), row(ys.shape[1]), row(dm), _const_spec((1, gg.shape[0])),
                _layer_spec(wglu_all, layer), _layer_spec(wo_all, layer), _const_spec((1, dm))]
    out_specs = [row(dm), row(dm)]
    out_shape = [jax.ShapeDtypeStruct((n, dm), F32), jax.ShapeDtypeStruct((n, dm), BF16)]
    if router is not None:
        ne = router.shape[1]
        r_hi = router.astype(BF16)
        r_lo = (router.astype(F32) - r_hi.astype(F32)).astype(BF16)
        pad = lambda a: jnp.pad(a, ((0, 0), (0, LANES - ne)))
        ins.append(jnp.concatenate([pad(r_hi), pad(r_lo)], axis=1))
        in_specs.append(_const_spec((dm, 2 * LANES)))
        out_specs.append(row(LANES))
        out_shape.append(jax.ShapeDtypeStruct((n, LANES), F32))
    return pl.pallas_call(
        functools.partial(_out_proj_kernel, with_router=router is not None),
        grid=(n // tm,),
        in_specs=in_specs, out_specs=out_specs, out_shape=out_shape,
        compiler_params=_cparams("parallel"),
        name="out_proj",
    )(*ins)


def _swiglu_chunks(h, w1, w3, w2, chunks, before_chunk=None, after_chunk=None):
    acc = None
    for j, (lo, width) in enumerate(chunks):
        if before_chunk is not None:
            before_chunk(j)
        a = _dot(h, w1[:, lo:lo + width].astype(BF16))
        b = _dot(h, w3[:, lo:lo + width].astype(BF16))
        t = (a * _sigmoid(a) * b).astype(BF16)
        y = _dot(t, w2[lo:lo + width, :].astype(BF16))
        acc = y if acc is None else acc + y
        if after_chunk is not None:
            after_chunk(j)
    return acc


def _ffn_chunks(f):
    if f == sum(w for _, w in FFN_CHUNKS):
        return FFN_CHUNKS
    return ((0, f),)


def _dense_ffn_kernel(h_ref, x_ref, w1_ref, w3_ref, w2_ref, o_ref, *, chunks):
    o_ref[...] = x_ref[...] + _swiglu_chunks(h_ref[...], w1_ref, w3_ref, w2_ref, chunks)


def _dense_ffn(h, x2, w1, w3, w2):
    n, dm = x2.shape
    tm = min(ROW_TILE, n)
    row = pl.BlockSpec((tm, dm), lambda i: (i, 0))
    return pl.pallas_call(
        functools.partial(_dense_ffn_kernel, chunks=_ffn_chunks(w1.shape[1])),
        grid=(n // tm,),
        in_specs=[row, row, _const_spec(w1.shape), _const_spec(w3.shape), _const_spec(w2.shape)],
        out_specs=row,
        out_shape=jax.ShapeDtypeStruct((n, dm), F32),
        compiler_params=_cparams("parallel"),
        name="dense_ffn",
    )(h, x2, w1, w3, w2)


def _expert_weight_copies(hbm, vmem, sems, chunks, expert, j):
    h1, h3, h2 = hbm
    v1, v3, v2 = vmem
    lo, w = chunks[j]
    return [pltpu.make_async_copy(h1.at[expert, :, lo:lo + w], v1.at[:, lo:lo + w], sems.at[j, 0]),
            pltpu.make_async_copy(h3.at[expert, :, lo:lo + w], v3.at[:, lo:lo + w], sems.at[j, 1]),
            pltpu.make_async_copy(h2.at[expert, lo:lo + w, :], v2.at[lo:lo + w, :], sems.at[j, 2])]


def _expert_ffn_kernel(brow_ref, bexp_ref, role_ref, x_ref, w1_hbm, w3_hbm, w2_hbm, o_ref, w1_v, w3_v, w2_v, sems,
                       *, chunks):
    i = pl.program_id(0)
    copies = functools.partial(_expert_weight_copies, (w1_hbm, w3_hbm, w2_hbm), (w1_v, w3_v, w2_v), sems, chunks)

    @pl.when(i == 0)
    def _():
        for j in range(len(chunks)):
            for cp in copies(bexp_ref[0], j):
                cp.start()

    def block(first, last):
        def before(j):
            for cp in copies(bexp_ref[i], j):
                cp.wait()

        def after(j):
            for cp in copies(bexp_ref[i + 1], j):
                cp.start()

        o_ref[...] = _swiglu_chunks(x_ref[...], w1_v, w3_v, w2_v, chunks, before if first else None,
                                    after if last else None).astype(o_ref.dtype)

    for role in range(4):
        pl.when(role_ref[i] == role)(functools.partial(block, bool(role & 1), bool(role & 2)))


def _expert_ffn(xb, w1, w3, w2, blk_row, blk_exp, role, max_blocks):
    rows, dm = xb.shape
    f = w1.shape[2]
    chunks = _ffn_chunks(f)
    row = pl.BlockSpec((MOE_BLK, dm), lambda i, br, be, ro: (br[i], 0))
    hbm = pl.BlockSpec(memory_space=pl.ANY)
    return pl.pallas_call(
        functools.partial(_expert_ffn_kernel, chunks=chunks),
        grid_spec=pltpu.PrefetchScalarGridSpec(
            num_scalar_prefetch=3, grid=(max_blocks,),
            in_specs=[row, hbm, hbm, hbm], out_specs=row,
            scratch_shapes=[pltpu.VMEM(w1.shape[1:], w1.dtype), pltpu.VMEM(w3.shape[1:], w3.dtype),
                            pltpu.VMEM(w2.shape[1:], w2.dtype), pltpu.SemaphoreType.DMA((len(chunks), 3))]),
        out_shape=jax.ShapeDtypeStruct((rows, dm), BF16),
        compiler_params=_cparams("arbitrary"),
        name="expert_ffn",
    )(blk_row, blk_exp, role, xb, w1, w3, w2)


def _route_tile(logits_tok, filled, ne):
    tm = logits_tok.shape[0]
    logits = logits_tok.T[0:ne, :]
    eidx = lax.broadcasted_iota(I32, (ne, tm), 0)
    m1 = jnp.max(logits, axis=0, keepdims=True)
    i1 = jnp.min(jnp.where(logits == m1, eidx, ne), axis=0, keepdims=True)
    sel1 = eidx == i1
    rest = jnp.where(sel1, -jnp.inf, logits)
    m2 = jnp.max(rest, axis=0, keepdims=True)
    i2 = jnp.min(jnp.where(rest == m2, eidx, ne), axis=0, keepdims=True)
    sel2 = eidx == i2
    e2 = jnp.exp(m2 - m1)
    den = 1.0 + e2
    rf = jnp.where(sel1 | sel2, 1.0, 0.0)
    cnt = jnp.sum(rf, axis=1, keepdims=True)
    before = (lax.broadcasted_iota(I32, (tm, tm), 0) < lax.broadcasted_iota(I32, (tm, tm), 1))
    rank = _dot(rf.astype(BF16), jnp.where(before, 1.0, 0.0).astype(BF16))
    whole = lambda a: jnp.floor(a * (1.0 / BF16_ROWS)) * BF16_ROWS
    phase = filled - whole(filled)
    span = jnp.where(cnt > 0, whole(phase + cnt + (BF16_ROWS - 1)), 0.0)
    ecol = lax.broadcasted_iota(I32, (ne, 1), 0)
    off = jnp.zeros((ne, 1), F32)
    for j in range(ne - 1):
        off = off + jnp.where(ecol > j, span[j:j + 1, :], 0.0)
    place = off + phase + rank
    pos1 = jnp.sum(jnp.where(sel1, place, 0.0), axis=0, keepdims=True)
    pos2 = jnp.sum(jnp.where(sel2, place, 0.0), axis=0, keepdims=True)
    lane_major = jnp.concatenate([pos1, pos2, 1.0 / den, e2 / den, jnp.zeros((4, tm), F32)], axis=0)
    tok_major = jnp.concatenate([lane_major, jnp.zeros((LANES - 8, tm), F32)], axis=0).T
    return lane_major, tok_major, cnt


def _route_kernel(lg_ref, lane_ref, tok_ref, cnt_ref, filled, *, ne):
    tm = MOE_TILE

    @pl.when(pl.program_id(0) == 0)
    def _():
        filled[...] = jnp.zeros_like(filled)

    for t in range(lg_ref.shape[0] // tm):
        lane_major, tok_major, cnt = _route_tile(lg_ref[t * tm:(t + 1) * tm, :], filled[...], ne)
        lane_ref[:, t * tm:(t + 1) * tm] = lane_major
        tok_ref[t * tm:(t + 1) * tm, :] = tok_major
        cnt_ref[t] = jnp.broadcast_to(cnt, (ne, LANES)).astype(I32)
        filled[...] = filled[...] + cnt


def _route(logits, ne):
    n = logits.shape[0]
    tm = MOE_TILE
    span = min(ROUTE_SPAN, n)
    return pl.pallas_call(
        functools.partial(_route_kernel, ne=ne),
        grid=(n // span,),
        in_specs=[pl.BlockSpec((span, LANES), lambda i: (i, 0))],
        out_specs=[pl.BlockSpec((8, span), lambda i: (0, i)), pl.BlockSpec((span, LANES), lambda i: (i, 0)),
                   pl.BlockSpec((span // tm, ne, LANES), lambda i: (i, 0, 0))],
        out_shape=[jax.ShapeDtypeStruct((8, n), F32), jax.ShapeDtypeStruct((n, LANES), F32),
                   jax.ShapeDtypeStruct((n // tm, ne, LANES), I32)],
        scratch_shapes=[pltpu.VMEM((ne, 1), F32)],
        compiler_params=_cparams("arbitrary"),
        name="moe_route",
    )(logits)


def _segment_copies(seg_ref, grouped_ref, base_ref, tile_buf, off_ref, sems, tile, slot, ne, *, to_grouped, wait):
    for e in range(ne):
        seg = seg_ref[tile * ne + e]
        far = base_ref[tile * ne + e]
        near = off_ref[tile * ne + e]
        for piece in SEG_PIECES:
            g = grouped_ref.at[pl.ds(pl.multiple_of(far, BF16_ROWS), piece)]
            t = tile_buf.at[slot, pl.ds(pl.multiple_of(near, BF16_ROWS), piece)]
            cp = pltpu.make_async_copy(t, g, sems.at[slot, e]) if to_grouped else \
                pltpu.make_async_copy(g, t, sems.at[slot, e])
            has = (seg & piece) != 0

            @pl.when(has)
            def _():
                cp.wait() if wait else cp.start()

            step = jnp.where(has, piece, 0)
            far = far + step
            near = near + step


def _tile_rows(ne):
    return TOP_K * MOE_TILE + ne * 2 * BF16_ROWS


def _tail_copies(end_ref, zeros, xb_ref, sems, ne):
    return [pltpu.make_async_copy(zeros, xb_ref.at[pl.ds(pl.multiple_of(end_ref[e], BF16_ROWS), MOE_BLK)], sems.at[e])
            for e in range(ne)]


def _dispatch_kernel(seg_ref, base_ref, off_ref, phase_ref, end_ref, rt_ref, h_ref, xb_ref, res, zeros, partial,
                     sems, tail_sems):
    i = pl.program_id(0)
    nt = pl.num_programs(0)
    ne = tail_sems.shape[0]
    tm = h_ref.shape[0]
    mrows = res.shape[1]
    slot = i % 2
    copies = functools.partial(_segment_copies, seg_ref, xb_ref, base_ref, res, off_ref, sems, ne=ne, to_grouped=True)

    @pl.when(i == 0)
    def _():
        zeros[...] = jnp.zeros_like(zeros)
        partial[...] = jnp.zeros_like(partial)
        for cp in _tail_copies(end_ref, zeros, xb_ref, tail_sems, ne):
            cp.start()

    row = lax.broadcasted_iota(I32, (mrows, tm), 0)
    pos = rt_ref[0:2, :].astype(I32)
    perm = jnp.where((row == pos[0:1, :]) | (row == pos[1:2, :]), 1.0, 0.0).astype(BF16)
    res[slot] = _dot(perm, h_ref[...]).astype(BF16)

    tile_row = lax.broadcasted_iota(I32, (BF16_ROWS, res.shape[2]), 0)
    for e in range(ne):
        @pl.when(seg_ref[i * ne + e] > 0)
        def _():
            first = pl.ds(pl.multiple_of(off_ref[i * ne + e], BF16_ROWS), BF16_ROWS)
            merged = jnp.where(tile_row < phase_ref[i * ne + e], partial[e].astype(F32), res[slot, first, :].astype(F32))
            res[slot, first, :] = merged.astype(BF16)
            last = pl.ds(pl.multiple_of(off_ref[i * ne + e] + seg_ref[i * ne + e] - BF16_ROWS, BF16_ROWS), BF16_ROWS)
            partial[e] = res[slot, last, :]

    @pl.when(i > 0)
    def _():
        copies(tile=i - 1, slot=1 - slot, wait=True)

    copies(tile=i, slot=slot, wait=False)

    @pl.when(i == nt - 1)
    def _():
        copies(tile=i, slot=slot, wait=True)
        for cp in _tail_copies(end_ref, zeros, xb_ref, tail_sems, ne):
            cp.wait()


def _dispatch(h, rt_lane, lay, ne):
    n, dm = h.shape
    tm = MOE_TILE
    return pl.pallas_call(
        _dispatch_kernel,
        grid_spec=pltpu.PrefetchScalarGridSpec(
            num_scalar_prefetch=5, grid=(n // tm,),
            in_specs=[pl.BlockSpec((8, tm), lambda i, *_: (0, i)), pl.BlockSpec((tm, dm), lambda i, *_: (i, 0))],
            out_specs=pl.BlockSpec(memory_space=pl.ANY),
            scratch_shapes=[pltpu.VMEM((2, _tile_rows(ne), dm), BF16), pltpu.VMEM((MOE_BLK, dm), BF16),
                            pltpu.VMEM((ne, BF16_ROWS, dm), BF16),
                            pltpu.SemaphoreType.DMA((2, ne)), pltpu.SemaphoreType.DMA((ne,))]),
        out_shape=jax.ShapeDtypeStruct((lay['rows_total'], dm), BF16),
        compiler_params=_cparams("arbitrary"),
        name="moe_dispatch",
    )(lay['seg'], lay['base'], lay['off'], lay['phase'], lay['end'], rt_lane, h)


def _combine_kernel(seg_ref, base_ref, off_ref, rt_ref, x_ref, gn_ref, yb_ref, o_ref, got, sems, *, ne):
    i = pl.program_id(0)
    nt = pl.num_programs(0)
    tm = x_ref.shape[0]
    mrows = got.shape[1]
    slot = i % 2
    copies = functools.partial(_segment_copies, seg_ref, yb_ref, base_ref, got, off_ref, sems, ne=ne, to_grouped=False)

    @pl.when(i == 0)
    def _():
        got[...] = jnp.zeros_like(got)
        copies(tile=i, slot=slot, wait=False)

    @pl.when(i + 1 < nt)
    def _():
        copies(tile=i + 1, slot=1 - slot, wait=False)

    copies(tile=i, slot=slot, wait=True)

    col = lax.broadcasted_iota(I32, (tm, mrows), 1)
    rt = rt_ref[...]
    rows = got[slot]
    sel = jnp.concatenate([jnp.where(col == rt[:, k:k + 1].astype(I32), 1.0, 0.0).astype(BF16) for k in range(TOP_K)],
                          axis=0)
    picked = _dot(sel, rows)
    y = sum(rt[:, TOP_K + k:TOP_K + k + 1] * picked[k * tm:(k + 1) * tm] for k in range(TOP_K))
    o_ref[...] = _rms(x_ref[...] + y, gn_ref[...])


def _combine(yb, rt_tok, x2, gn, lay, ne):
    n, dm = x2.shape
    tm = MOE_TILE
    mrows = _tile_rows(ne)
    tok = lambda w: pl.BlockSpec((tm, w), lambda i, *_: (i, 0))
    return pl.pallas_call(
        functools.partial(_combine_kernel, ne=ne),
        grid_spec=pltpu.PrefetchScalarGridSpec(
            num_scalar_prefetch=3, grid=(n // tm,),
            in_specs=[tok(LANES), tok(dm), pl.BlockSpec((1, dm), lambda i, *_: (0, 0)),
                      pl.BlockSpec(memory_space=pl.ANY)],
            out_specs=tok(dm),
            scratch_shapes=[pltpu.VMEM((2, mrows, dm), BF16), pltpu.SemaphoreType.DMA((2, ne))]),
        out_shape=jax.ShapeDtypeStruct((n, dm), F32),
        compiler_params=_cparams("arbitrary"),
        name="moe_combine",
    )(lay['seg'], lay['base'], lay['off'], rt_tok, x2, gn.reshape(1, dm), yb)


def _moe_layout(cnt, n):
    nt, ne = cnt.shape
    blk = MOE_BLK
    whole = lambda a: a // BF16_ROWS * BF16_ROWS
    cap = -(-(n + BF16_ROWS + blk) // blk) * blk
    region = jnp.arange(ne, dtype=I32) * cap
    filled = jnp.cumsum(cnt, axis=0) - cnt
    phase = filled - whole(filled)
    span = jnp.where(cnt > 0, whole(phase + cnt + BF16_ROWS - 1), 0)
    base = (region[None, :] + filled - phase).astype(I32)
    off = (jnp.cumsum(span, axis=1) - span).astype(I32)
    rows_e = jnp.sum(cnt, axis=0)
    nblk_e = (rows_e + blk - 1) // blk
    end = (region + whole(rows_e + BF16_ROWS - 1)).astype(I32)
    ends = jnp.cumsum(nblk_e)
    max_blocks = TOP_K * n // blk + ne
    bid = jnp.arange(max_blocks + 1, dtype=I32)
    bexp = jnp.minimum(jnp.sum((bid[:, None] >= ends[None, :]).astype(I32), axis=1), ne - 1)
    brow = bexp * (cap // blk) + bid - (ends - nblk_e)[bexp]
    nblk = ends[-1]
    last = jnp.maximum(nblk - 1, 0)
    used = bid < nblk
    first = bid == (ends - nblk_e)[bexp]
    hand_over = (bid == ends[bexp] - 1) & (bid < last)
    role = jnp.where(used, first.astype(I32) + 2 * hand_over.astype(I32), 4).astype(I32)
    bexp = jnp.where(used, bexp, bexp[last]).astype(I32)
    brow = jnp.where(used, brow, brow[last]).astype(I32)
    flat = lambda a: a.reshape(-1)
    return dict(seg=flat(span.astype(I32)), base=flat(base), off=flat(off), phase=flat(phase.astype(I32)), end=end,
                brow=brow, bexp=bexp, role=role, rows_total=ne * cap, max_blocks=max_blocks)


def _moe(x2, h, logits, gn, w1, w3, w2):
    n, dm = x2.shape
    ne = w1.shape[0]
    rt_lane, rt_tok, cnt = _route(logits, ne)
    lay = _moe_layout(cnt[:, :, 0], n)
    xb = _dispatch(h, rt_lane, lay, ne)
    yb = _expert_ffn(xb, w1, w3, w2, lay['brow'], lay['bexp'], lay['role'], lay['max_blocks'])
    return _combine(yb, rt_tok, x2, gn, lay, ne)


def kernel(x, norm_mix_g, w_in, attn_rel_bias, rg_conv_w, rg_conv_b, rg_wx, rg_bx, rg_wa, rg_ba, rg_lambda, s5_a_re, s5_a_im, s5_log_dt, s5_b_re, s5_b_im, s5_c_re, s5_c_im, s5_d, s5_w_glu, g_group, w_out, norm_ffn_g, ffn_w1, ffn_w3, ffn_w2, moe_router, moe_w1, moe_w3, moe_w2, final_norm_g):
    bsz, l, dm = x.shape
    depth = w_in.shape[0]
    assert depth == 2 and l % ATTN_TQ == 0 and ATTN_TQ == N_PREV * CHUNK, "dense layer, then the MoE layer"
    d_rg = rg_conv_w.shape[2]
    d_s5 = s5_w_glu.shape[1]
    d_attn = (w_in.shape[2] - 2 * d_rg - d_s5) // 3
    x2 = x.reshape(bsz * l, dm)
    for layer in range(depth):
        q, k, v, xr, gate, us = _in_proj(x2, norm_mix_g[layer], w_in, layer, d_attn, d_rg, d_s5)
        y_attn = _attention(q, k, v, _attn_bias(attn_rel_bias[layer]), bsz, l)
        w_gates = jnp.concatenate([_block_diag(rg_wx[layer]), _block_diag(rg_wa[layer])], axis=1).astype(BF16)
        b_gates = jnp.concatenate([rg_bx[layer], rg_ba[layer]]).astype(F32)
        y_rg = _rglru(xr, gate, rg_conv_w[layer].astype(F32), rg_conv_b[layer].astype(F32), w_gates, b_gates,
                      jax.nn.softplus(-rg_lambda[layer].astype(F32)), bsz, l)
        prm = _s5_params(s5_a_re[layer], s5_a_im[layer], s5_log_dt[layer], s5_b_re[layer], s5_b_im[layer],
                         s5_c_re[layer], s5_c_im[layer], s5_d[layer])
        y_s5 = _s5(us, prm, bsz, l)
        mixed = (y_attn, y_rg, y_s5, x2, g_group[layer].astype(F32), s5_w_glu, w_out, layer,
                 norm_ffn_g[layer].astype(F32))
        if layer == 0:
            x2, h = _out_proj(*mixed)
            x2 = _dense_ffn(h, x2, ffn_w1[0], ffn_w3[0], ffn_w2[0])
        else:
            x2, h, logits = _out_proj(*mixed, router=moe_router[0])
            x2 = _moe(x2, h, logits, final_norm_g.astype(F32), moe_w1[0], moe_w3[0], moe_w2[0])
    return x2.reshape(bsz, l, dm)
```

```python
import functools
import math

import jax
import jax.numpy as jnp
import numpy as np
from jax import lax
from jax.experimental import pallas as pl
from jax.experimental.pallas import tpu as pltpu

F32 = jnp.float32
BF16 = jnp.bfloat16
I32 = jnp.int32

EPS = 1e-6
CHUNK = 64
N_PREV = 8
BAND = (N_PREV + 1) * CHUNK
REL_CLIP = 128
HEAD_DIM = 64
RG_C = 8.0
RG_CONV_WIDTH = 4
S5_GROUP_DIM = 16
S5_T = 16
TOP_K = 2
NEG_BIG = -1e30

LANES = 128
BF16_ROWS = 16
VMEM_LIMIT = 52 * 1024 * 1024

ROW_TILE = 512
ATTN_TQ = 512
ATTN_SPAN = 2 * CHUNK
ATTN_WIN = N_PREV * CHUNK + ATTN_SPAN
RG_TL = 1024
RG_SEG = 8
MOE_TILE = 256
MOE_BLK = 256
ROUTE_SPAN = 2048
SEG_PIECES = tuple(BF16_ROWS << b for b in range((MOE_TILE // BF16_ROWS).bit_length() - 1, -1, -1))
FFN_CHUNKS = ((0, 512), (512, 512), (1024, 512), (1536, 512), (2048, 512), (2560, 256))


def _cparams(*sem):
    return pltpu.CompilerParams(dimension_semantics=sem, vmem_limit_bytes=VMEM_LIMIT)


def _const_spec(shape):
    nd = len(shape)
    return pl.BlockSpec(shape, lambda *_: (0,) * nd, pipeline_mode=pl.Buffered(1))


def _rms(xf, g):
    var = jnp.mean(xf * xf, axis=-1, keepdims=True)
    return xf * lax.rsqrt(var + EPS) * g


def _sigmoid(x):
    return 1.0 / (1.0 + jnp.exp(-x))


def _gelu(x):
    c = math.sqrt(2.0 / math.pi)
    return 0.5 * x * (1.0 + jnp.tanh(c * (x + 0.044715 * (x * x * x))))


def _dot(a, b):
    return jnp.dot(a, b, preferred_element_type=F32)


def _in_proj_kernel(x_ref, g_ref, w_ref, wvt_ref, q_ref, k_ref, vt_ref, xr_ref, gt_ref, us_ref, *, d_attn, d_rg):
    u = _rms(x_ref[...], g_ref[...]).astype(BF16)
    lo = 0
    for ref, width in ((q_ref, d_attn), (k_ref, d_attn), (None, d_attn),
                       (xr_ref, d_rg), (gt_ref, d_rg), (us_ref, w_ref.shape[1] - 3 * d_attn - 2 * d_rg)):
        if ref is not None:
            ref[...] = _dot(u, w_ref[:, lo:lo + width].astype(BF16)).astype(ref.dtype)
        lo += width
    vt_ref[...] = lax.dot_general(wvt_ref[...].astype(BF16), u, (((1,), (1,)), ((), ())),
                                  preferred_element_type=F32).astype(vt_ref.dtype)


def _layer_spec(stacked, layer):
    return pl.BlockSpec((None,) + stacked.shape[1:], lambda *_: (layer, 0, 0), pipeline_mode=pl.Buffered(1))


def _in_proj(x2, g, w_all, layer, d_attn, d_rg, d_s5):
    n, dm = x2.shape
    tm = min(ROW_TILE, n)
    row = lambda w: pl.BlockSpec((tm, w), lambda i: (i, 0))
    wv_t = w_all[layer, :, 2 * d_attn:3 * d_attn].T
    return pl.pallas_call(
        functools.partial(_in_proj_kernel, d_attn=d_attn, d_rg=d_rg),
        grid=(n // tm,),
        in_specs=[row(dm), _const_spec((1, dm)), _layer_spec(w_all, layer), _const_spec(wv_t.shape)],
        out_specs=[row(d_attn), row(d_attn), pl.BlockSpec((d_attn, tm), lambda i: (0, i)),
                   row(d_rg), row(d_rg), row(d_s5)],
        out_shape=[jax.ShapeDtypeStruct((n, d_attn), BF16)] * 2 + [jax.ShapeDtypeStruct((d_attn, n), BF16)]
        + [jax.ShapeDtypeStruct((n, d_rg), F32)] * 2 + [jax.ShapeDtypeStruct((n, d_s5), F32)],
        compiler_params=_cparams("parallel"),
        name="in_proj",
    )(x2, g.reshape(1, dm), w_all, wv_t)


def _attn_kernel(q_ref, kp_ref, kc_ref, vp_ref, vc_ref, bias_ref, o_ref, kz, vwin, sc, *, n_pairs):
    qi = pl.program_id(1)
    tq = q_ref.shape[0]
    vrows = vwin.shape[1] // n_pairs
    kz[0:tq, :] = kp_ref[...]
    kz[tq:2 * tq, :] = kc_ref[...]
    for j in range(tq // ATTN_SPAN):
        a = tq - j * ATTN_SPAN
        for hp in range(n_pairs):
            chan = slice(hp * LANES, (hp + 1) * LANES)
            vwin[j, hp * vrows:hp * vrows + LANES, 0:a] = vp_ref[chan, j * ATTN_SPAN:tq]
            vwin[j, hp * vrows:hp * vrows + LANES, a:ATTN_WIN] = vc_ref[chan, 0:ATTN_WIN - a]
            vwin[j, hp * vrows + LANES:(hp + 1) * vrows, :] = jnp.ones((vrows - LANES, ATTN_WIN), BF16)
    first = lax.broadcasted_iota(I32, (ATTN_SPAN, LANES), 1) < HEAD_DIM
    pairs = [slice(hp * LANES, (hp + 1) * LANES) for hp in range(n_pairs)]

    def score(j, masked):
        r0 = j * ATTN_SPAN if isinstance(j, int) else pl.multiple_of(j * ATTN_SPAN, ATTN_SPAN)
        for hp, cols in enumerate(pairs):
            q2 = q_ref[pl.ds(r0, ATTN_SPAN), cols] * jnp.asarray(HEAD_DIM ** -0.5, BF16)
            zero = jnp.zeros_like(q2)
            qq = jnp.concatenate([jnp.where(first, q2, zero), jnp.where(first, zero, q2)], axis=0)
            s = lax.dot_general(kz[pl.ds(r0, ATTN_WIN), cols], qq, (((1,), (1,)), ((), ())),
                                preferred_element_type=F32) + bias_ref[hp]
            if masked:
                key = lax.broadcasted_iota(I32, (ATTN_WIN, 2 * ATTN_SPAN), 0)
                s = jnp.where(key >= tq - r0, s, NEG_BIG)
            sc[j % 2, hp] = s

    def finish(j):
        r0 = j * ATTN_SPAN if isinstance(j, int) else pl.multiple_of(j * ATTN_SPAN, ATTN_SPAN)
        probs = []
        for hp in range(n_pairs):
            s = sc[j % 2, hp]
            probs.append(jnp.exp((s - jnp.max(s, axis=0, keepdims=True)).astype(BF16)))
        for hp, (cols, p) in enumerate(zip(pairs, probs)):
            ov = _dot(vwin[j, hp * vrows:(hp + 1) * vrows, :], p)
            o2 = (ov[0:LANES] / ov[LANES:LANES + 1]).T
            o_ref[pl.ds(r0, ATTN_SPAN), cols] = jnp.where(first, o2[0:ATTN_SPAN], o2[ATTN_SPAN:]).astype(o_ref.dtype)

    def run(masked):
        groups = tq // ATTN_SPAN
        score(0, masked)
        for j in range(groups):
            if j + 1 < groups:
                score(j + 1, masked)
            finish(j)

    pl.when(qi == 0)(functools.partial(run, True))
    pl.when(qi != 0)(functools.partial(run, False))


def _attention(q, k, vt, bias_t, bsz, l):
    n, da = q.shape
    tq = ATTN_TQ
    nt = l // tq
    n_pairs = da // LANES
    cur = pl.BlockSpec((tq, da), lambda b, i: (b * nt + i, 0))
    prev = pl.BlockSpec((tq, da), lambda b, i: (b * nt + jnp.maximum(i - 1, 0), 0))
    cur_t = pl.BlockSpec((da, tq), lambda b, i: (0, b * nt + i))
    prev_t = pl.BlockSpec((da, tq), lambda b, i: (0, b * nt + jnp.maximum(i - 1, 0)))
    return pl.pallas_call(
        functools.partial(_attn_kernel, n_pairs=n_pairs),
        grid=(bsz, nt),
        in_specs=[cur, prev, cur, prev_t, cur_t, _const_spec(bias_t.shape)],
        out_specs=cur,
        out_shape=jax.ShapeDtypeStruct((n, da), BF16),
        scratch_shapes=[pltpu.VMEM((2 * tq, da), BF16),
                        pltpu.VMEM((tq // ATTN_SPAN, n_pairs * (LANES + BF16_ROWS), ATTN_WIN), BF16),
                        pltpu.VMEM((2, n_pairs, ATTN_WIN, 2 * ATTN_SPAN), F32)],
        compiler_params=_cparams("parallel", "parallel"),
        name="chunk_attn",
    )(q, k, k, vt, vt, bias_t)


def _attn_bias(rel_bias):
    h = rel_bias.shape[0]
    tab = rel_bias.astype(F32)
    n_far = N_PREV * CHUNK - REL_CLIP + CHUNK
    lo = 2 * REL_CLIP - (BAND + CHUNK - 1 - n_far)
    ext = jnp.concatenate([jnp.broadcast_to(tab[:, 2 * REL_CLIP:], (h, n_far)), tab[:, lo:2 * REL_CLIP][:, ::-1]], axis=1)
    wide = BAND + CHUNK
    ring = jnp.concatenate([ext[:, CHUNK - 1:], jnp.zeros((h, 1), F32), ext[:, :CHUNK - 1]], axis=1)
    bias = jnp.broadcast_to(ring[:, None, :], (h, CHUNK, wide)).reshape(h, CHUNK * wide)
    bias = bias[:, :CHUNK * (wide - 1)].reshape(h, CHUNK, wide - 1)[..., :BAND]
    band_t = bias.transpose(0, 2, 1)
    groups = ATTN_SPAN // CHUNK
    per_chunk = [jnp.pad(band_t, ((0, 0), (c * CHUNK, (groups - 1 - c) * CHUNK), (0, 0)), constant_values=NEG_BIG)
                 for c in range(groups)]
    win = jnp.stack(per_chunk, axis=2)
    return win.reshape(h // 2, 2, ATTN_WIN, ATTN_SPAN).transpose(0, 2, 1, 3).reshape(h // 2, ATTN_WIN, 2 * ATTN_SPAN)


def _rglru_kernel(x_ref, gt_ref, cw_ref, cb_ref, w_ref, b_ref, sp_ref, o_ref, xpad, a_s, b_s, hcar):
    t = pl.program_id(1)
    tl, c = x_ref.shape
    seg = tl // RG_SEG
    front = 8

    @pl.when(t == 0)
    def _():
        xpad[0:front, :] = jnp.zeros((front, c), F32)
        hcar[...] = jnp.zeros_like(hcar)

    xpad[front:front + tl, :] = x_ref[...]
    xc = cb_ref[...] + sum(
        cw_ref[j:j + 1, :] * xpad[front - (RG_CONV_WIDTH - 1) + j:front - (RG_CONV_WIDTH - 1) + j + tl, :]
        for j in range(RG_CONV_WIDTH))
    xpad[0:front, :] = xpad[tl:tl + front, :]
    pre = _dot(xc.astype(BF16), w_ref[...]) + b_ref[...]
    gx = _sigmoid(pre[:, 0:c])
    ga = _sigmoid(pre[:, c:2 * c])
    log_a = -RG_C * ga * sp_ref[...]
    a = jnp.exp(log_a)
    mult = jnp.sqrt(-jnp.tanh(log_a) * (a * a + 1.0))
    b = mult * gx * xc
    nh = c // LANES
    for j in range(nh):
        a_s[j] = a[:, j * LANES:(j + 1) * LANES]
        b_s[j] = b[:, j * LANES:(j + 1) * LANES]

    def step(i, carry):
        rows = pl.ds(i, RG_SEG, stride=seg)
        out = []
        for j in range(nh):
            h, p = carry[j]
            av = a_s[j, rows, :]
            h = av * h + b_s[j, rows, :]
            p = av * p
            b_s[j, rows, :] = h
            a_s[j, rows, :] = p
            out.append((h, p))
        return tuple(out)

    init = tuple((jnp.zeros((RG_SEG, LANES), F32), jnp.ones((RG_SEG, LANES), F32)) for _ in range(nh))
    ends = lax.fori_loop(0, seg, step, init, unroll=8)
    for j in range(nh):
        cols = slice(j * LANES, (j + 1) * LANES)
        h_end, p_end = ends[j]
        carry = hcar[:, cols]
        for s in range(RG_SEG):
            rows = slice(s * seg, (s + 1) * seg)
            h = b_s[j, rows, :] + a_s[j, rows, :] * carry
            o_ref[rows, cols] = (h * _gelu(gt_ref[rows, cols])).astype(o_ref.dtype)
            carry = h_end[s:s + 1, :] + p_end[s:s + 1, :] * carry
        hcar[:, cols] = carry


def _rglru(xr, gate, conv_w, conv_b, w_bd, b_cat, sp, bsz, l):
    n, c = xr.shape
    tl = min(RG_TL, l)
    nt = l // tl
    row = pl.BlockSpec((tl, c), lambda b, t: (b * nt + t, 0))
    return pl.pallas_call(
        _rglru_kernel,
        grid=(bsz, nt),
        in_specs=[row, row, _const_spec(conv_w.shape), _const_spec((1, c)), _const_spec(w_bd.shape),
                  _const_spec((1, 2 * c)), _const_spec((1, c))],
        out_specs=row,
        out_shape=jax.ShapeDtypeStruct((n, c), BF16),
        scratch_shapes=[pltpu.VMEM((tl + 8, c), F32), pltpu.VMEM((c // LANES, tl, LANES), F32),
                        pltpu.VMEM((c // LANES, tl, LANES), F32), pltpu.VMEM((1, c), F32)],
        compiler_params=_cparams("parallel", "arbitrary"),
        name="rglru",
    )(xr, gate, conv_w, conv_b.reshape(1, c), w_bd, b_cat.reshape(1, 2 * c), sp.reshape(1, c))


def _block_diag(w):
    nb, d, _ = w.shape
    eye = jnp.eye(nb, dtype=w.dtype)
    return (eye[:, None, :, None] * w[:, :, None, :]).reshape(nb * d, nb * d)


def _s5_kernel(u_ref, m_ref, wa_ref, wb_ref, wo_ref, c1_ref, c2a_ref, c2b_ref, d_ref, y_ref, ps, ug, xa_s, xb_s):
    t, gd = S5_T, S5_GROUP_DIM
    ng, nk, _ = ug.shape
    nh = ps.shape[0]
    per_half = LANES // gd
    rows8 = 8

    for h in range(nh):
        ps[h] = u_ref[:, h * LANES:(h + 1) * LANES]

    lane_block = lax.broadcasted_iota(I32, (rows8, LANES), 1) // gd

    def block_transpose(vs):
        d = per_half // 2
        while d:
            keep = (lane_block & d) == 0
            out = list(vs)
            for i in range(per_half):
                if i & d == 0:
                    out[i] = jnp.where(keep, vs[i], pltpu.roll(vs[i + d], d * gd, 1))
                    out[i + d] = jnp.where(keep, pltpu.roll(vs[i], LANES - d * gd, 1), vs[i + d])
            vs, d = out, d // 2
        return vs

    def to_groups(r, carry):
        base = pl.multiple_of(r * rows8 * t, rows8 * t)
        rows = pl.ds(pl.multiple_of(r * rows8, rows8), rows8)
        for h in range(nh):
            for j in range(t // per_half):
                steps = [ps[h, pl.ds(base + j * per_half + i, rows8, stride=t), :] for i in range(per_half)]
                for k, v in enumerate(block_transpose(steps)):
                    ug[h * per_half + k, rows, j * LANES:(j + 1) * LANES] = v
        return carry

    lax.fori_loop(0, nk // rows8, to_groups, 0, unroll=2)

    def project(g, carry):
        u = ug[g]
        ub = u.astype(BF16)
        rows = pl.ds(pl.multiple_of(g * nk, nk), nk)
        xa_s[rows, :] = _dot(ub, wa_ref[g])
        xb_s[rows, :] = _dot(ub, wb_ref[g])
        ug[g] = _dot(ub, m_ref[g]) + d_ref[g] * u
        return carry

    lax.fori_loop(0, ng, project, 0)

    c1, c2a, c2b = c1_ref[...], c2a_ref[...], c2b_ref[...]

    def step(k, carry):
        xa, xb = carry
        rows = pl.ds(k, ng, stride=nk)
        ia = xa_s[rows, :]
        ib = xb_s[rows, :]
        xa_s[rows, :] = xa
        return c1 * xa + c2a * xb + ia, c1 * xb + c2b * xa + ib

    z = jnp.zeros((ng, xa_s.shape[1]), F32)
    lax.fori_loop(0, nk, step, (z, z), unroll=4)

    def respond(g, carry):
        rows = pl.ds(pl.multiple_of(g * nk, nk), nk)
        ug[g] = _gelu(ug[g] + _dot(xa_s[rows, :].astype(BF16), wo_ref[g]))
        return carry

    lax.fori_loop(0, ng, respond, 0)

    def from_groups(r, carry):
        base = pl.multiple_of(r * rows8 * t, rows8 * t)
        rows = pl.ds(pl.multiple_of(r * rows8, rows8), rows8)
        for h in range(nh):
            for j in range(t // per_half):
                groups = [ug[h * per_half + k, rows, j * LANES:(j + 1) * LANES] for k in range(per_half)]
                for i, v in enumerate(block_transpose(groups)):
                    ps[h, pl.ds(base + j * per_half + i, rows8, stride=t), :] = v
        return carry

    lax.fori_loop(0, nk // rows8, from_groups, 0, unroll=2)
    for h in range(nh):
        y_ref[:, h * LANES:(h + 1) * LANES] = ps[h].astype(y_ref.dtype)


def _s5(us, prm, bsz, l):
    n, w = us.shape
    ng = w // S5_GROUP_DIM
    nk = l // S5_T
    p2 = prm[1].shape[2]
    row = pl.BlockSpec((l, w), lambda b: (b, 0))
    return pl.pallas_call(
        _s5_kernel,
        grid=(bsz,),
        in_specs=[row] + [_const_spec(a.shape) for a in prm],
        out_specs=row,
        out_shape=jax.ShapeDtypeStruct((n, w), BF16),
        scratch_shapes=[pltpu.VMEM((w // LANES, l, LANES), F32), pltpu.VMEM((ng, nk, S5_T * S5_GROUP_DIM), F32),
                        pltpu.VMEM((ng * nk, p2), F32), pltpu.VMEM((ng * nk, p2), F32)],
        compiler_params=_cparams("parallel"),
        name="s5",
    )(us, *prm)


def _s5_params(a_re, a_im, log_dt, b_re, b_im, c_re, c_im, d):
    t = S5_T
    g, p = a_re.shape
    gd = S5_GROUP_DIM
    hi = lax.Precision.HIGHEST
    a = lax.complex(a_re.astype(F32), a_im.astype(F32))
    dt = jnp.exp(log_dt.astype(F32))[:, None]
    a_bar = jnp.exp(a * dt)
    bm = lax.complex(b_re.astype(F32), b_im.astype(F32))
    cm = lax.complex(c_re.astype(F32), c_im.astype(F32))
    b_bar = ((a_bar - 1.0) / a)[..., None] * bm
    steps = jnp.arange(t + 1, dtype=F32)
    pw = jnp.exp((a * dt)[:, None, :] * steps[None, :, None])

    cp = (cm[:, None, :, :] * pw[:, :, None, :]).transpose(0, 3, 1, 2)
    w = t * gd
    lag = cp[:, :, :t].reshape(g, p, w)
    k2 = jnp.einsum('gpc,gpx->gcx', jnp.concatenate([b_bar.real, -b_bar.imag], axis=1),
                    jnp.concatenate([lag.real, lag.imag], axis=1), precision=hi)
    shift = np.zeros((t, w, w), np.float32)
    for s in range(t):
        shift[s, np.arange(w - s * gd), np.arange(w - s * gd) + s * gd] = 1.0
    m = jnp.einsum('gcx,sxl->gscl', k2, shift, precision=hi).reshape(g, w, w)
    win = pw[:, :t][:, ::-1][:, :, :, None] * b_bar[:, None]
    win = win.transpose(0, 1, 3, 2).reshape(g, w, p)
    wa = jnp.concatenate([win.real, win.imag], axis=-1)
    wb = jnp.concatenate([win.imag, win.real], axis=-1)
    co = cp[:, :, 1:].reshape(g, p, w)
    wo = jnp.concatenate([co.real, -co.imag], axis=1)
    lt = pw[:, t]
    c1 = jnp.concatenate([lt.real, lt.real], axis=-1)
    c2a = jnp.concatenate([-lt.imag, lt.imag], axis=-1)
    c2b = jnp.concatenate([lt.imag, -lt.imag], axis=-1)
    dtile = jnp.tile(d.astype(F32).reshape(g, 1, gd), (1, t, 1)).reshape(g, 1, t * gd)
    return (m.astype(BF16), wa.astype(BF16), wb.astype(BF16), wo.astype(BF16), c1, c2a, c2b, dtile)


def _out_proj_kernel(ya_ref, yr_ref, ys_ref, x_ref, gg_ref, wglu_ref, wo_ref, gf_ref, *rest, with_router):
    if with_router:
        rt_ref, xo_ref, h_ref, lg_ref = rest
    else:
        xo_ref, h_ref = rest
    ys = ys_ref[...]
    ys = ys.astype(F32) * _sigmoid(_dot(ys.astype(BF16), wglu_ref[...].astype(BF16)))
    acc = x_ref[...]
    lo = 0
    for y in (ya_ref[...].astype(F32), yr_ref[...].astype(F32), ys):
        w = y.shape[1]
        acc = acc + _dot(_rms(y, gg_ref[:, lo:lo + w]).astype(BF16), wo_ref[lo:lo + w, :].astype(BF16))
        lo += w
    xo_ref[...] = acc
    h = _rms(acc, gf_ref[...])
    hb = h.astype(BF16)
    h_ref[...] = hb
    if with_router:
        both = _dot(hb, rt_ref[...])
        h_lo = (h - hb.astype(F32)).astype(BF16)
        lg_ref[...] = both[:, 0:LANES] + both[:, LANES:2 * LANES] + _dot(h_lo, rt_ref[:, 0:LANES])


def _out_proj(ya, yr, ys, x2, gg, wglu_all, wo_all, layer, gf, router=None):
    n, dm = x2.shape
    tm = min(ROW_TILE, n)
    row = lambda w: pl.BlockSpec((tm, w), lambda i: (i, 0))
    ins = [ya, yr, ys, x2, gg.reshape(1, -1), wglu_all, wo_all, gf.reshape(1, dm)]
    in_specs = [row(ya.shape[1]), row(yr.shape[1]), row(ys.shape[1]), row(dm), _const_spec((1, gg.shape[0])),
                _layer_spec(wglu_all, layer), _layer_spec(wo_all, layer), _const_spec((1, dm))]
    out_specs = [row(dm), row(dm)]
    out_shape = [jax.ShapeDtypeStruct((n, dm), F32), jax.ShapeDtypeStruct((n, dm), BF16)]
    if router is not None:
        ne = router.shape[1]
        r_hi = router.astype(BF16)
        r_lo = (router.astype(F32) - r_hi.astype(F32)).astype(BF16)
        pad = lambda a: jnp.pad(a, ((0, 0), (0, LANES - ne)))
        ins.append(jnp.concatenate([pad(r_hi), pad(r_lo)], axis=1))
        in_specs.append(_const_spec((dm, 2 * LANES)))
        out_specs.append(row(LANES))
        out_shape.append(jax.ShapeDtypeStruct((n, LANES), F32))
    return pl.pallas_call(
        functools.partial(_out_proj_kernel, with_router=router is not None),
        grid=(n // tm,),
        in_specs=in_specs, out_specs=out_specs, out_shape=out_shape,
        compiler_params=_cparams("parallel"),
        name="out_proj",
    )(*ins)


def _swiglu_chunks(h, w1, w3, w2, chunks, before_chunk=None, after_chunk=None):
    acc = None
    for j, (lo, width) in enumerate(chunks):
        if before_chunk is not None:
            before_chunk(j)
        a = _dot(h, w1[:, lo:lo + width].astype(BF16))
        b = _dot(h, w3[:, lo:lo + width].astype(BF16))
        t = (a * _sigmoid(a) * b).astype(BF16)
        y = _dot(t, w2[lo:lo + width, :].astype(BF16))
        acc = y if acc is None else acc + y
        if after_chunk is not None:
            after_chunk(j)
    return acc


def _ffn_chunks(f):
    if f == sum(w for _, w in FFN_CHUNKS):
        return FFN_CHUNKS
    return ((0, f),)


def _dense_ffn_kernel(h_ref, x_ref, w1_ref, w3_ref, w2_ref, o_ref, *, chunks):
    o_ref[...] = x_ref[...] + _swiglu_chunks(h_ref[...], w1_ref, w3_ref, w2_ref, chunks)


def _dense_ffn(h, x2, w1, w3, w2):
    n, dm = x2.shape
    tm = min(ROW_TILE, n)
    row = pl.BlockSpec((tm, dm), lambda i: (i, 0))
    return pl.pallas_call(
        functools.partial(_dense_ffn_kernel, chunks=_ffn_chunks(w1.shape[1])),
        grid=(n // tm,),
        in_specs=[row, row, _const_spec(w1.shape), _const_spec(w3.shape), _const_spec(w2.shape)],
        out_specs=row,
        out_shape=jax.ShapeDtypeStruct((n, dm), F32),
        compiler_params=_cparams("parallel"),
        name="dense_ffn",
    )(h, x2, w1, w3, w2)


def _expert_weight_copies(hbm, vmem, sems, chunks, expert, j):
    h1, h3, h2 = hbm
    v1, v3, v2 = vmem
    lo, w = chunks[j]
    return [pltpu.make_async_copy(h1.at[expert, :, lo:lo + w], v1.at[:, lo:lo + w], sems.at[j, 0]),
            pltpu.make_async_copy(h3.at[expert, :, lo:lo + w], v3.at[:, lo:lo + w], sems.at[j, 1]),
            pltpu.make_async_copy(h2.at[expert, lo:lo + w, :], v2.at[lo:lo + w, :], sems.at[j, 2])]


def _expert_ffn_kernel(brow_ref, bexp_ref, role_ref, x_ref, w1_hbm, w3_hbm, w2_hbm, o_ref, w1_v, w3_v, w2_v, sems,
                       *, chunks):
    i = pl.program_id(0)
    copies = functools.partial(_expert_weight_copies, (w1_hbm, w3_hbm, w2_hbm), (w1_v, w3_v, w2_v), sems, chunks)

    @pl.when(i == 0)
    def _():
        for j in range(len(chunks)):
            for cp in copies(bexp_ref[0], j):
                cp.start()

    def block(first, last):
        def before(j):
            for cp in copies(bexp_ref[i], j):
                cp.wait()

        def after(j):
            for cp in copies(bexp_ref[i + 1], j):
                cp.start()

        o_ref[...] = _swiglu_chunks(x_ref[...], w1_v, w3_v, w2_v, chunks, before if first else None,
                                    after if last else None).astype(o_ref.dtype)

    for role in range(4):
        pl.when(role_ref[i] == role)(functools.partial(block, bool(role & 1), bool(role & 2)))


def _expert_ffn(xb, w1, w3, w2, blk_row, blk_exp, role, max_blocks):
    rows, dm = xb.shape
    f = w1.shape[2]
    chunks = _ffn_chunks(f)
    row = pl.BlockSpec((MOE_BLK, dm), lambda i, br, be, ro: (br[i], 0))
    hbm = pl.BlockSpec(memory_space=pl.ANY)
    return pl.pallas_call(
        functools.partial(_expert_ffn_kernel, chunks=chunks),
        grid_spec=pltpu.PrefetchScalarGridSpec(
            num_scalar_prefetch=3, grid=(max_blocks,),
            in_specs=[row, hbm, hbm, hbm], out_specs=row,
            scratch_shapes=[pltpu.VMEM(w1.shape[1:], w1.dtype), pltpu.VMEM(w3.shape[1:], w3.dtype),
                            pltpu.VMEM(w2.shape[1:], w2.dtype), pltpu.SemaphoreType.DMA((len(chunks), 3))]),
        out_shape=jax.ShapeDtypeStruct((rows, dm), BF16),
        compiler_params=_cparams("arbitrary"),
        name="expert_ffn",
    )(blk_row, blk_exp, role, xb, w1, w3, w2)


def _route_tile(logits_tok, filled, ne):
    tm = logits_tok.shape[0]
    logits = logits_tok.T[0:ne, :]
    eidx = lax.broadcasted_iota(I32, (ne, tm), 0)
    m1 = jnp.max(logits, axis=0, keepdims=True)
    i1 = jnp.min(jnp.where(logits == m1, eidx, ne), axis=0, keepdims=True)
    sel1 = eidx == i1
    rest = jnp.where(sel1, -jnp.inf, logits)
    m2 = jnp.max(rest, axis=0, keepdims=True)
    i2 = jnp.min(jnp.where(rest == m2, eidx, ne), axis=0, keepdims=True)
    sel2 = eidx == i2
    e2 = jnp.exp(m2 - m1)
    den = 1.0 + e2
    rf = jnp.where(sel1 | sel2, 1.0, 0.0)
    cnt = jnp.sum(rf, axis=1, keepdims=True)
    before = (lax.broadcasted_iota(I32, (tm, tm), 0) < lax.broadcasted_iota(I32, (tm, tm), 1))
    rank = _dot(rf.astype(BF16), jnp.where(before, 1.0, 0.0).astype(BF16))
    whole = lambda a: jnp.floor(a * (1.0 / BF16_ROWS)) * BF16_ROWS
    phase = filled - whole(filled)
    span = jnp.where(cnt > 0, whole(phase + cnt + (BF16_ROWS - 1)), 0.0)
    ecol = lax.broadcasted_iota(I32, (ne, 1), 0)
    off = jnp.zeros((ne, 1), F32)
    for j in range(ne - 1):
        off = off + jnp.where(ecol > j, span[j:j + 1, :], 0.0)
    place = off + phase + rank
    pos1 = jnp.sum(jnp.where(sel1, place, 0.0), axis=0, keepdims=True)
    pos2 = jnp.sum(jnp.where(sel2, place, 0.0), axis=0, keepdims=True)
    lane_major = jnp.concatenate([pos1, pos2, 1.0 / den, e2 / den, jnp.zeros((4, tm), F32)], axis=0)
    tok_major = jnp.concatenate([lane_major, jnp.zeros((LANES - 8, tm), F32)], axis=0).T
    return lane_major, tok_major, cnt


def _route_kernel(lg_ref, lane_ref, tok_ref, cnt_ref, filled, *, ne):
    tm = MOE_TILE

    @pl.when(pl.program_id(0) == 0)
    def _():
        filled[...] = jnp.zeros_like(filled)

    for t in range(lg_ref.shape[0] // tm):
        lane_major, tok_major, cnt = _route_tile(lg_ref[t * tm:(t + 1) * tm, :], filled[...], ne)
        lane_ref[:, t * tm:(t + 1) * tm] = lane_major
        tok_ref[t * tm:(t + 1) * tm, :] = tok_major
        cnt_ref[t] = jnp.broadcast_to(cnt, (ne, LANES)).astype(I32)
        filled[...] = filled[...] + cnt


def _route(logits, ne):
    n = logits.shape[0]
    tm = MOE_TILE
    span = min(ROUTE_SPAN, n)
    return pl.pallas_call(
        functools.partial(_route_kernel, ne=ne),
        grid=(n // span,),
        in_specs=[pl.BlockSpec((span, LANES), lambda i: (i, 0))],
        out_specs=[pl.BlockSpec((8, span), lambda i: (0, i)), pl.BlockSpec((span, LANES), lambda i: (i, 0)),
                   pl.BlockSpec((span // tm, ne, LANES), lambda i: (i, 0, 0))],
        out_shape=[jax.ShapeDtypeStruct((8, n), F32), jax.ShapeDtypeStruct((n, LANES), F32),
                   jax.ShapeDtypeStruct((n // tm, ne, LANES), I32)],
        scratch_shapes=[pltpu.VMEM((ne, 1), F32)],
        compiler_params=_cparams("arbitrary"),
        name="moe_route",
    )(logits)


def _segment_copies(seg_ref, grouped_ref, base_ref, tile_buf, off_ref, sems, tile, slot, ne, *, to_grouped, wait):
    for e in range(ne):
        seg = seg_ref[tile * ne + e]
        far = base_ref[tile * ne + e]
        near = off_ref[tile * ne + e]
        for piece in SEG_PIECES:
            g = grouped_ref.at[pl.ds(pl.multiple_of(far, BF16_ROWS), piece)]
            t = tile_buf.at[slot, pl.ds(pl.multiple_of(near, BF16_ROWS), piece)]
            cp = pltpu.make_async_copy(t, g, sems.at[slot, e]) if to_grouped else \
                pltpu.make_async_copy(g, t, sems.at[slot, e])
            has = (seg & piece) != 0

            @pl.when(has)
            def _():
                cp.wait() if wait else cp.start()

            step = jnp.where(has, piece, 0)
            far = far + step
            near = near + step


def _tile_rows(ne):
    return TOP_K * MOE_TILE + ne * 2 * BF16_ROWS


def _tail_copies(end_ref, zeros, xb_ref, sems, ne):
    return [pltpu.make_async_copy(zeros, xb_ref.at[pl.ds(pl.multiple_of(end_ref[e], BF16_ROWS), MOE_BLK)], sems.at[e])
            for e in range(ne)]


def _dispatch_kernel(seg_ref, base_ref, off_ref, phase_ref, end_ref, rt_ref, h_ref, xb_ref, res, zeros, partial,
                     sems, tail_sems):
    i = pl.program_id(0)
    nt = pl.num_programs(0)
    ne = tail_sems.shape[0]
    tm = h_ref.shape[0]
    mrows = res.shape[1]
    slot = i % 2
    copies = functools.partial(_segment_copies, seg_ref, xb_ref, base_ref, res, off_ref, sems, ne=ne, to_grouped=True)

    @pl.when(i == 0)
    def _():
        zeros[...] = jnp.zeros_like(zeros)
        partial[...] = jnp.zeros_like(partial)
        for cp in _tail_copies(end_ref, zeros, xb_ref, tail_sems, ne):
            cp.start()

    row = lax.broadcasted_iota(I32, (mrows, tm), 0)
    pos = rt_ref[0:2, :].astype(I32)
    perm = jnp.where((row == pos[0:1, :]) | (row == pos[1:2, :]), 1.0, 0.0).astype(BF16)
    res[slot] = _dot(perm, h_ref[...]).astype(BF16)

    tile_row = lax.broadcasted_iota(I32, (BF16_ROWS, res.shape[2]), 0)
    for e in range(ne):
        @pl.when(seg_ref[i * ne + e] > 0)
        def _():
            first = pl.ds(pl.multiple_of(off_ref[i * ne + e], BF16_ROWS), BF16_ROWS)
            merged = jnp.where(tile_row < phase_ref[i * ne + e], partial[e].astype(F32), res[slot, first, :].astype(F32))
            res[slot, first, :] = merged.astype(BF16)
            last = pl.ds(pl.multiple_of(off_ref[i * ne + e] + seg_ref[i * ne + e] - BF16_ROWS, BF16_ROWS), BF16_ROWS)
            partial[e] = res[slot, last, :]

    @pl.when(i > 0)
    def _():
        copies(tile=i - 1, slot=1 - slot, wait=True)

    copies(tile=i, slot=slot, wait=False)

    @pl.when(i == nt - 1)
    def _():
        copies(tile=i, slot=slot, wait=True)
        for cp in _tail_copies(end_ref, zeros, xb_ref, tail_sems, ne):
            cp.wait()


def _dispatch(h, rt_lane, lay, ne):
    n, dm = h.shape
    tm = MOE_TILE
    return pl.pallas_call(
        _dispatch_kernel,
        grid_spec=pltpu.PrefetchScalarGridSpec(
            num_scalar_prefetch=5, grid=(n // tm,),
            in_specs=[pl.BlockSpec((8, tm), lambda i, *_: (0, i)), pl.BlockSpec((tm, dm), lambda i, *_: (i, 0))],
            out_specs=pl.BlockSpec(memory_space=pl.ANY),
            scratch_shapes=[pltpu.VMEM((2, _tile_rows(ne), dm), BF16), pltpu.VMEM((MOE_BLK, dm), BF16),
                            pltpu.VMEM((ne, BF16_ROWS, dm), BF16),
                            pltpu.SemaphoreType.DMA((2, ne)), pltpu.SemaphoreType.DMA((ne,))]),
        out_shape=jax.ShapeDtypeStruct((lay['rows_total'], dm), BF16),
        compiler_params=_cparams("arbitrary"),
        name="moe_dispatch",
    )(lay['seg'], lay['base'], lay['off'], lay['phase'], lay['end'], rt_lane, h)


def _combine_kernel(seg_ref, base_ref, off_ref, rt_ref, x_ref, gn_ref, yb_ref, o_ref, got, sems, *, ne):
    i = pl.program_id(0)
    nt = pl.num_programs(0)
    tm = x_ref.shape[0]
    mrows = got.shape[1]
    slot = i % 2
    copies = functools.partial(_segment_copies, seg_ref, yb_ref, base_ref, got, off_ref, sems, ne=ne, to_grouped=False)

    @pl.when(i == 0)
    def _():
        got[...] = jnp.zeros_like(got)
        copies(tile=i, slot=slot, wait=False)

    @pl.when(i + 1 < nt)
    def _():
        copies(tile=i + 1, slot=1 - slot, wait=False)

    copies(tile=i, slot=slot, wait=True)

    col = lax.broadcasted_iota(I32, (tm, mrows), 1)
    rt = rt_ref[...]
    rows = got[slot]
    sel = jnp.concatenate([jnp.where(col == rt[:, k:k + 1].astype(I32), 1.0, 0.0).astype(BF16) for k in range(TOP_K)],
                          axis=0)
    picked = _dot(sel, rows)
    y = sum(rt[:, TOP_K + k:TOP_K + k + 1] * picked[k * tm:(k + 1) * tm] for k in range(TOP_K))
    o_ref[...] = _rms(x_ref[...] + y, gn_ref[...])


def _combine(yb, rt_tok, x2, gn, lay, ne):
    n, dm = x2.shape
    tm = MOE_TILE
    mrows = _tile_rows(ne)
    tok = lambda w: pl.BlockSpec((tm, w), lambda i, *_: (i, 0))
    return pl.pallas_call(
        functools.partial(_combine_kernel, ne=ne),
        grid_spec=pltpu.PrefetchScalarGridSpec(
            num_scalar_prefetch=3, grid=(n // tm,),
            in_specs=[tok(LANES), tok(dm), pl.BlockSpec((1, dm), lambda i, *_: (0, 0)),
                      pl.BlockSpec(memory_space=pl.ANY)],
            out_specs=tok(dm),
            scratch_shapes=[pltpu.VMEM((2, mrows, dm), BF16), pltpu.SemaphoreType.DMA((2, ne))]),
        out_shape=jax.ShapeDtypeStruct((n, dm), F32),
        compiler_params=_cparams("arbitrary"),
        name="moe_combine",
    )(lay['seg'], lay['base'], lay['off'], rt_tok, x2, gn.reshape(1, dm), yb)


def _moe_layout(cnt, n):
    nt, ne = cnt.shape
    blk = MOE_BLK
    whole = lambda a: a // BF16_ROWS * BF16_ROWS
    cap = -(-(n + BF16_ROWS + blk) // blk) * blk
    region = jnp.arange(ne, dtype=I32) * cap
    filled = jnp.cumsum(cnt, axis=0) - cnt
    phase = filled - whole(filled)
    span = jnp.where(cnt > 0, whole(phase + cnt + BF16_ROWS - 1), 0)
    base = (region[None, :] + filled - phase).astype(I32)
    off = (jnp.cumsum(span, axis=1) - span).astype(I32)
    rows_e = jnp.sum(cnt, axis=0)
    nblk_e = (rows_e + blk - 1) // blk
    end = (region + whole(rows_e + BF16_ROWS - 1)).astype(I32)
    ends = jnp.cumsum(nblk_e)
    max_blocks = TOP_K * n // blk + ne
    bid = jnp.arange(max_blocks + 1, dtype=I32)
    bexp = jnp.minimum(jnp.sum((bid[:, None] >= ends[None, :]).astype(I32), axis=1), ne - 1)
    brow = bexp * (cap // blk) + bid - (ends - nblk_e)[bexp]
    nblk = ends[-1]
    last = jnp.maximum(nblk - 1, 0)
    used = bid < nblk
    first = bid == (ends - nblk_e)[bexp]
    hand_over = (bid == ends[bexp] - 1) & (bid < last)
    role = jnp.where(used, first.astype(I32) + 2 * hand_over.astype(I32), 4).astype(I32)
    bexp = jnp.where(used, bexp, bexp[last]).astype(I32)
    brow = jnp.where(used, brow, brow[last]).astype(I32)
    flat = lambda a: a.reshape(-1)
    return dict(seg=flat(span.astype(I32)), base=flat(base), off=flat(off), phase=flat(phase.astype(I32)), end=end,
                brow=brow, bexp=bexp, role=role, rows_total=ne * cap, max_blocks=max_blocks)


def _moe(x2, h, logits, gn, w1, w3, w2):
    n, dm = x2.shape
    ne = w1.shape[0]
    rt_lane, rt_tok, cnt = _route(logits, ne)
    lay = _moe_layout(cnt[:, :, 0], n)
    xb = _dispatch(h, rt_lane, lay, ne)
    yb = _expert_ffn(xb, w1, w3, w2, lay['brow'], lay['bexp'], lay['role'], lay['max_blocks'])
    return _combine(yb, rt_tok, x2, gn, lay, ne)


def kernel(x, norm_mix_g, w_in, attn_rel_bias, rg_conv_w, rg_conv_b, rg_wx, rg_bx, rg_wa, rg_ba, rg_lambda, s5_a_re, s5_a_im, s5_log_dt, s5_b_re, s5_b_im, s5_c_re, s5_c_im, s5_d, s5_w_glu, g_group, w_out, norm_ffn_g, ffn_w1, ffn_w3, ffn_w2, moe_router, moe_w1, moe_w3, moe_w2, final_norm_g):
    bsz, l, dm = x.shape
    depth = w_in.shape[0]
    assert depth == 2 and l % ATTN_TQ == 0 and ATTN_TQ == N_PREV * CHUNK, "dense layer, then the MoE layer"
    d_rg = rg_conv_w.shape[2]
    d_s5 = s5_w_glu.shape[1]
    d_attn = (w_in.shape[2] - 2 * d_rg - d_s5) // 3
    x2 = x.reshape(bsz * l, dm)
    for layer in range(depth):
        q, k, v, xr, gate, us = _in_proj(x2, norm_mix_g[layer], w_in, layer, d_attn, d_rg, d_s5)
        y_attn = _attention(q, k, v, _attn_bias(attn_rel_bias[layer]), bsz, l)
        w_gates = jnp.concatenate([_block_diag(rg_wx[layer]), _block_diag(rg_wa[layer])], axis=1).astype(BF16)
        b_gates = jnp.concatenate([rg_bx[layer], rg_ba[layer]]).astype(F32)
        y_rg = _rglru(xr, gate, rg_conv_w[layer].astype(F32), rg_conv_b[layer].astype(F32), w_gates, b_gates,
                      jax.nn.softplus(-rg_lambda[layer].astype(F32)), bsz, l)
        prm = _s5_params(s5_a_re[layer], s5_a_im[layer], s5_log_dt[layer], s5_b_re[layer], s5_b_im[layer],
                         s5_c_re[layer], s5_c_im[layer], s5_d[layer])
        y_s5 = _s5(us, prm, bsz, l)
        mixed = (y_attn, y_rg, y_s5, x2, g_group[layer].astype(F32), s5_w_glu, w_out, layer,
                 norm_ffn_g[layer].astype(F32))
        if layer == 0:
            x2, h = _out_proj(*mixed)
            x2 = _dense_ffn(h, x2, ffn_w1[0], ffn_w3[0], ffn_w2[0])
        else:
            x2, h, logits = _out_proj(*mixed, router=moe_router[0])
            x2 = _moe(x2, h, logits, final_norm_g.astype(F32), moe_w1[0], moe_w3[0], moe_w2[0])
    return x2.reshape(bsz, l, dm)
```

```python
import functools
import math

import jax
import jax.numpy as jnp
import numpy as np
from jax import lax
from jax.experimental import pallas as pl
from jax.experimental.pallas import tpu as pltpu

F32 = jnp.float32
BF16 = jnp.bfloat16
I32 = jnp.int32

EPS = 1e-6
CHUNK = 64
N_PREV = 8
BAND = (N_PREV + 1) * CHUNK
REL_CLIP = 128
HEAD_DIM = 64
RG_C = 8.0
RG_CONV_WIDTH = 4
S5_GROUP_DIM = 16
S5_T = 16
TOP_K = 2
NEG_BIG = -1e30

LANES = 128
BF16_ROWS = 16
VMEM_LIMIT = 52 * 1024 * 1024

ROW_TILE = 512
ATTN_TQ = 512
ATTN_SPAN = 2 * CHUNK
ATTN_WIN = N_PREV * CHUNK + ATTN_SPAN
RG_TL = 1024
RG_SEG = 8
MOE_TILE = 256
MOE_BLK = 512
ROUTE_SPAN = 2048
SEG_PIECES = tuple(BF16_ROWS << b for b in range((MOE_TILE // BF16_ROWS).bit_length() - 1, -1, -1))
FFN_CHUNKS = ((0, 512), (512, 512), (1024, 512), (1536, 512), (2048, 512), (2560, 256))


def _cparams(*sem):
    return pltpu.CompilerParams(dimension_semantics=sem, vmem_limit_bytes=VMEM_LIMIT)


def _const_spec(shape):
    nd = len(shape)
    return pl.BlockSpec(shape, lambda *_: (0,) * nd, pipeline_mode=pl.Buffered(1))


def _rms(xf, g):
    var = jnp.mean(xf * xf, axis=-1, keepdims=True)
    return xf * lax.rsqrt(var + EPS) * g


def _sigmoid(x):
    return 1.0 / (1.0 + jnp.exp(-x))


def _gelu(x):
    c = math.sqrt(2.0 / math.pi)
    return 0.5 * x * (1.0 + jnp.tanh(c * (x + 0.044715 * (x * x * x))))


def _dot(a, b):
    return jnp.dot(a, b, preferred_element_type=F32)


def _in_proj_kernel(x_ref, g_ref, w_ref, wvt_ref, q_ref, k_ref, vt_ref, xr_ref, gt_ref, us_ref, *, d_attn, d_rg):
    u = _rms(x_ref[...], g_ref[...]).astype(BF16)
    lo = 0
    for ref, width in ((q_ref, d_attn), (k_ref, d_attn), (None, d_attn),
                       (xr_ref, d_rg), (gt_ref, d_rg), (us_ref, w_ref.shape[1] - 3 * d_attn - 2 * d_rg)):
        if ref is not None:
            ref[...] = _dot(u, w_ref[:, lo:lo + width].astype(BF16)).astype(ref.dtype)
        lo += width
    vt_ref[...] = lax.dot_general(wvt_ref[...].astype(BF16), u, (((1,), (1,)), ((), ())),
                                  preferred_element_type=F32).astype(vt_ref.dtype)


def _layer_spec(stacked, layer):
    rest = stacked.shape[1:]
    return pl.BlockSpec((None,) + rest, lambda *_: (layer,) + (0,) * len(rest), pipeline_mode=pl.Buffered(1))


def _in_proj(x2, g, w_all, layer, d_attn, d_rg, d_s5):
    n, dm = x2.shape
    tm = min(ROW_TILE, n)
    row = lambda w: pl.BlockSpec((tm, w), lambda i: (i, 0))
    wv_t = w_all[layer, :, 2 * d_attn:3 * d_attn].T
    return pl.pallas_call(
        functools.partial(_in_proj_kernel, d_attn=d_attn, d_rg=d_rg),
        grid=(n // tm,),
        in_specs=[row(dm), _const_spec((1, dm)), _layer_spec(w_all, layer), _const_spec(wv_t.shape)],
        out_specs=[row(d_attn), row(d_attn), pl.BlockSpec((d_attn, tm), lambda i: (0, i)),
                   row(d_rg), row(d_rg), row(d_s5)],
        out_shape=[jax.ShapeDtypeStruct((n, d_attn), BF16)] * 2 + [jax.ShapeDtypeStruct((d_attn, n), BF16)]
        + [jax.ShapeDtypeStruct((n, d_rg), F32)] * 2 + [jax.ShapeDtypeStruct((n, d_s5), F32)],
        compiler_params=_cparams("parallel"),
        name="in_proj",
    )(x2, g.reshape(1, dm), w_all, wv_t)


def _attn_kernel(q_ref, kp_ref, kc_ref, vp_ref, vc_ref, bias_ref, o_ref, kz, vwin, sc, *, n_pairs):
    qi = pl.program_id(1)
    tq = q_ref.shape[0]
    vrows = vwin.shape[1] // n_pairs
    kz[0:tq, :] = kp_ref[...]
    kz[tq:2 * tq, :] = kc_ref[...]
    for j in range(tq // ATTN_SPAN):
        a = tq - j * ATTN_SPAN
        for hp in range(n_pairs):
            chan = slice(hp * LANES, (hp + 1) * LANES)
            vwin[j, hp * vrows:hp * vrows + LANES, 0:a] = vp_ref[chan, j * ATTN_SPAN:tq]
            vwin[j, hp * vrows:hp * vrows + LANES, a:ATTN_WIN] = vc_ref[chan, 0:ATTN_WIN - a]
            vwin[j, hp * vrows + LANES:(hp + 1) * vrows, :] = jnp.ones((vrows - LANES, ATTN_WIN), BF16)
    first = lax.broadcasted_iota(I32, (ATTN_SPAN, LANES), 1) < HEAD_DIM
    pairs = [slice(hp * LANES, (hp + 1) * LANES) for hp in range(n_pairs)]

    def score(j, masked):
        r0 = j * ATTN_SPAN if isinstance(j, int) else pl.multiple_of(j * ATTN_SPAN, ATTN_SPAN)
        for hp, cols in enumerate(pairs):
            q2 = q_ref[pl.ds(r0, ATTN_SPAN), cols] * jnp.asarray(HEAD_DIM ** -0.5, BF16)
            zero = jnp.zeros_like(q2)
            qq = jnp.concatenate([jnp.where(first, q2, zero), jnp.where(first, zero, q2)], axis=0)
            s = lax.dot_general(kz[pl.ds(r0, ATTN_WIN), cols], qq, (((1,), (1,)), ((), ())),
                                preferred_element_type=F32) + bias_ref[hp]
            if masked:
                key = lax.broadcasted_iota(I32, (ATTN_WIN, 2 * ATTN_SPAN), 0)
                s = jnp.where(key >= tq - r0, s, NEG_BIG)
            sc[j % 2, hp] = s

    def finish(j):
        r0 = j * ATTN_SPAN if isinstance(j, int) else pl.multiple_of(j * ATTN_SPAN, ATTN_SPAN)
        probs = []
        for hp in range(n_pairs):
            s = sc[j % 2, hp]
            probs.append(jnp.exp((s - jnp.max(s, axis=0, keepdims=True)).astype(BF16)))
        for hp, (cols, p) in enumerate(zip(pairs, probs)):
            ov = _dot(vwin[j, hp * vrows:(hp + 1) * vrows, :], p)
            o2 = (ov[0:LANES] / ov[LANES:LANES + 1]).T
            o_ref[pl.ds(r0, ATTN_SPAN), cols] = jnp.where(first, o2[0:ATTN_SPAN], o2[ATTN_SPAN:]).astype(o_ref.dtype)

    def run(masked):
        groups = tq // ATTN_SPAN
        score(0, masked)
        for j in range(groups):
            if j + 1 < groups:
                score(j + 1, masked)
            finish(j)

    pl.when(qi == 0)(functools.partial(run, True))
    pl.when(qi != 0)(functools.partial(run, False))


def _attention(q, k, vt, bias_t, layer, bsz, l):
    n, da = q.shape
    tq = ATTN_TQ
    nt = l // tq
    n_pairs = da // LANES
    cur = pl.BlockSpec((tq, da), lambda b, i: (b * nt + i, 0))
    prev = pl.BlockSpec((tq, da), lambda b, i: (b * nt + jnp.maximum(i - 1, 0), 0))
    cur_t = pl.BlockSpec((da, tq), lambda b, i: (0, b * nt + i))
    prev_t = pl.BlockSpec((da, tq), lambda b, i: (0, b * nt + jnp.maximum(i - 1, 0)))
    return pl.pallas_call(
        functools.partial(_attn_kernel, n_pairs=n_pairs),
        grid=(bsz, nt),
        in_specs=[cur, prev, cur, prev_t, cur_t, _layer_spec(bias_t, layer)],
        out_specs=cur,
        out_shape=jax.ShapeDtypeStruct((n, da), BF16),
        scratch_shapes=[pltpu.VMEM((2 * tq, da), BF16),
                        pltpu.VMEM((tq // ATTN_SPAN, n_pairs * (LANES + BF16_ROWS), ATTN_WIN), BF16),
                        pltpu.VMEM((2, n_pairs, ATTN_WIN, 2 * ATTN_SPAN), F32)],
        compiler_params=_cparams("parallel", "parallel"),
        name="chunk_attn",
    )(q, k, k, vt, vt, bias_t)


def _attn_bias(rel_bias):
    h = rel_bias.shape[0]
    tab = rel_bias.astype(F32)
    n_far = N_PREV * CHUNK - REL_CLIP + CHUNK
    lo = 2 * REL_CLIP - (BAND + CHUNK - 1 - n_far)
    ext = jnp.concatenate([jnp.broadcast_to(tab[:, 2 * REL_CLIP:], (h, n_far)), tab[:, lo:2 * REL_CLIP][:, ::-1]], axis=1)
    wide = BAND + CHUNK
    ring = jnp.concatenate([ext[:, CHUNK - 1:], jnp.zeros((h, 1), F32), ext[:, :CHUNK - 1]], axis=1)
    bias = jnp.broadcast_to(ring[:, None, :], (h, CHUNK, wide)).reshape(h, CHUNK * wide)
    bias = bias[:, :CHUNK * (wide - 1)].reshape(h, CHUNK, wide - 1)[..., :BAND]
    band_t = bias.transpose(0, 2, 1)
    groups = ATTN_SPAN // CHUNK
    per_chunk = [jnp.pad(band_t, ((0, 0), (c * CHUNK, (groups - 1 - c) * CHUNK), (0, 0)), constant_values=NEG_BIG)
                 for c in range(groups)]
    win = jnp.stack(per_chunk, axis=2)
    return win.reshape(h // 2, 2, ATTN_WIN, ATTN_SPAN).transpose(0, 2, 1, 3).reshape(h // 2, ATTN_WIN, 2 * ATTN_SPAN)


def _rglru_kernel(x_ref, gt_ref, cw_ref, cb_ref, w_ref, b_ref, sp_ref, o_ref, xpad, a_s, b_s, hcar):
    t = pl.program_id(1)
    tl, c = x_ref.shape
    seg = tl // RG_SEG
    front = 8

    @pl.when(t == 0)
    def _():
        xpad[0:front, :] = jnp.zeros((front, c), F32)
        hcar[...] = jnp.zeros_like(hcar)

    xpad[front:front + tl, :] = x_ref[...]
    xc = cb_ref[...] + sum(
        cw_ref[j:j + 1, :] * xpad[front - (RG_CONV_WIDTH - 1) + j:front - (RG_CONV_WIDTH - 1) + j + tl, :]
        for j in range(RG_CONV_WIDTH))
    xpad[0:front, :] = xpad[tl:tl + front, :]
    pre = _dot(xc.astype(BF16), w_ref[...]) + b_ref[...]
    gx = _sigmoid(pre[:, 0:c])
    ga = _sigmoid(pre[:, c:2 * c])
    log_a = -RG_C * ga * sp_ref[...]
    a = jnp.exp(log_a)
    mult = jnp.sqrt(-jnp.tanh(log_a) * (a * a + 1.0))
    b = mult * gx * xc
    nh = c // LANES
    for j in range(nh):
        a_s[j] = a[:, j * LANES:(j + 1) * LANES]
        b_s[j] = b[:, j * LANES:(j + 1) * LANES]

    def step(i, carry):
        rows = pl.ds(i, RG_SEG, stride=seg)
        out = []
        for j in range(nh):
            h, p = carry[j]
            av = a_s[j, rows, :]
            h = av * h + b_s[j, rows, :]
            p = av * p
            b_s[j, rows, :] = h
            a_s[j, rows, :] = p
            out.append((h, p))
        return tuple(out)

    init = tuple((jnp.zeros((RG_SEG, LANES), F32), jnp.ones((RG_SEG, LANES), F32)) for _ in range(nh))
    ends = lax.fori_loop(0, seg, step, init, unroll=8)
    for j in range(nh):
        cols = slice(j * LANES, (j + 1) * LANES)
        h_end, p_end = ends[j]
        carry = hcar[:, cols]
        for s in range(RG_SEG):
            rows = slice(s * seg, (s + 1) * seg)
            h = b_s[j, rows, :] + a_s[j, rows, :] * carry
            o_ref[rows, cols] = (h * _gelu(gt_ref[rows, cols])).astype(o_ref.dtype)
            carry = h_end[s:s + 1, :] + p_end[s:s + 1, :] * carry
        hcar[:, cols] = carry


def _rglru(xr, gate, conv_w, conv_b, w_bd, b_cat, sp, layer, bsz, l):
    n, c = xr.shape
    tl = min(RG_TL, l)
    nt = l // tl
    row = pl.BlockSpec((tl, c), lambda b, t: (b * nt + t, 0))
    return pl.pallas_call(
        _rglru_kernel,
        grid=(bsz, nt),
        in_specs=[row, row] + [_layer_spec(a, layer) for a in (conv_w, conv_b, w_bd, b_cat, sp)],
        out_specs=row,
        out_shape=jax.ShapeDtypeStruct((n, c), BF16),
        scratch_shapes=[pltpu.VMEM((tl + 8, c), F32), pltpu.VMEM((c // LANES, tl, LANES), F32),
                        pltpu.VMEM((c // LANES, tl, LANES), F32), pltpu.VMEM((1, c), F32)],
        compiler_params=_cparams("parallel", "arbitrary"),
        name="rglru",
    )(xr, gate, conv_w, conv_b, w_bd, b_cat, sp)


def _block_diag(w):
    nb, d, _ = w.shape
    eye = jnp.eye(nb, dtype=w.dtype)
    return (eye[:, None, :, None] * w[:, :, None, :]).reshape(nb * d, nb * d)


def _s5_kernel(u_ref, m_ref, wa_ref, wb_ref, wo_ref, c1_ref, c2a_ref, c2b_ref, d_ref, y_ref, ps, ug, xa_s, xb_s):
    t, gd = S5_T, S5_GROUP_DIM
    ng, nk, _ = ug.shape
    nh = ps.shape[0]
    per_half = LANES // gd
    rows8 = 8

    for h in range(nh):
        ps[h] = u_ref[:, h * LANES:(h + 1) * LANES]

    lane_block = lax.broadcasted_iota(I32, (rows8, LANES), 1) // gd

    def block_transpose(vs):
        d = per_half // 2
        while d:
            keep = (lane_block & d) == 0
            out = list(vs)
            for i in range(per_half):
                if i & d == 0:
                    out[i] = jnp.where(keep, vs[i], pltpu.roll(vs[i + d], d * gd, 1))
                    out[i + d] = jnp.where(keep, pltpu.roll(vs[i], LANES - d * gd, 1), vs[i + d])
            vs, d = out, d // 2
        return vs

    def to_groups(r, carry):
        base = pl.multiple_of(r * rows8 * t, rows8 * t)
        rows = pl.ds(pl.multiple_of(r * rows8, rows8), rows8)
        for h in range(nh):
            for j in range(t // per_half):
                steps = [ps[h, pl.ds(base + j * per_half + i, rows8, stride=t), :] for i in range(per_half)]
                for k, v in enumerate(block_transpose(steps)):
                    ug[h * per_half + k, rows, j * LANES:(j + 1) * LANES] = v
        return carry

    lax.fori_loop(0, nk // rows8, to_groups, 0, unroll=2)

    def project(g, carry):
        u = ug[g]
        ub = u.astype(BF16)
        rows = pl.ds(pl.multiple_of(g * nk, nk), nk)
        xa_s[rows, :] = _dot(ub, wa_ref[g])
        xb_s[rows, :] = _dot(ub, wb_ref[g])
        ug[g] = _dot(ub, m_ref[g]) + d_ref[g] * u
        return carry

    lax.fori_loop(0, ng, project, 0)

    c1, c2a, c2b = c1_ref[...], c2a_ref[...], c2b_ref[...]

    def step(k, carry):
        xa, xb = carry
        rows = pl.ds(k, ng, stride=nk)
        ia = xa_s[rows, :]
        ib = xb_s[rows, :]
        xa_s[rows, :] = xa
        return c1 * xa + c2a * xb + ia, c1 * xb + c2b * xa + ib

    z = jnp.zeros((ng, xa_s.shape[1]), F32)
    lax.fori_loop(0, nk, step, (z, z), unroll=4)

    def respond(g, carry):
        rows = pl.ds(pl.multiple_of(g * nk, nk), nk)
        ug[g] = _gelu(ug[g] + _dot(xa_s[rows, :].astype(BF16), wo_ref[g]))
        return carry

    lax.fori_loop(0, ng, respond, 0)

    def from_groups(r, carry):
        base = pl.multiple_of(r * rows8 * t, rows8 * t)
        rows = pl.ds(pl.multiple_of(r * rows8, rows8), rows8)
        for h in range(nh):
            for j in range(t // per_half):
                groups = [ug[h * per_half + k, rows, j * LANES:(j + 1) * LANES] for k in range(per_half)]
                for i, v in enumerate(block_transpose(groups)):
                    ps[h, pl.ds(base + j * per_half + i, rows8, stride=t), :] = v
        return carry

    lax.fori_loop(0, nk // rows8, from_groups, 0, unroll=2)
    for h in range(nh):
        y_ref[:, h * LANES:(h + 1) * LANES] = ps[h].astype(y_ref.dtype)


def _s5(us, prm, layer, bsz, l):
    n, w = us.shape
    ng = w // S5_GROUP_DIM
    nk = l // S5_T
    p2 = prm[1].shape[-1]
    row = pl.BlockSpec((l, w), lambda b: (b, 0))
    return pl.pallas_call(
        _s5_kernel,
        grid=(bsz,),
        in_specs=[row] + [_layer_spec(a, layer) for a in prm],
        out_specs=row,
        out_shape=jax.ShapeDtypeStruct((n, w), BF16),
        scratch_shapes=[pltpu.VMEM((w // LANES, l, LANES), F32), pltpu.VMEM((ng, nk, S5_T * S5_GROUP_DIM), F32),
                        pltpu.VMEM((ng * nk, p2), F32), pltpu.VMEM((ng * nk, p2), F32)],
        compiler_params=_cparams("parallel"),
        name="s5",
    )(us, *prm)


def _s5_params(a_re, a_im, log_dt, b_re, b_im, c_re, c_im, d):
    t = S5_T
    g, p = a_re.shape
    gd = S5_GROUP_DIM
    hi = lax.Precision.HIGHEST
    a = lax.complex(a_re.astype(F32), a_im.astype(F32))
    dt = jnp.exp(log_dt.astype(F32))[:, None]
    a_bar = jnp.exp(a * dt)
    bm = lax.complex(b_re.astype(F32), b_im.astype(F32))
    cm = lax.complex(c_re.astype(F32), c_im.astype(F32))
    b_bar = ((a_bar - 1.0) / a)[..., None] * bm
    steps = jnp.arange(t + 1, dtype=F32)
    pw = jnp.exp((a * dt)[:, None, :] * steps[None, :, None])

    cp = (cm[:, None, :, :] * pw[:, :, None, :]).transpose(0, 3, 1, 2)
    w = t * gd
    lag = cp[:, :, :t].reshape(g, p, w)
    k2 = jnp.einsum('gpc,gpx->gcx', jnp.concatenate([b_bar.real, -b_bar.imag], axis=1),
                    jnp.concatenate([lag.real, lag.imag], axis=1), precision=hi)
    shift = np.zeros((t, w, w), np.float32)
    for s in range(t):
        shift[s, np.arange(w - s * gd), np.arange(w - s * gd) + s * gd] = 1.0
    m = jnp.einsum('gcx,sxl->gscl', k2, shift, precision=hi).reshape(g, w, w)
    win = pw[:, :t][:, ::-1][:, :, :, None] * b_bar[:, None]
    win = win.transpose(0, 1, 3, 2).reshape(g, w, p)
    wa = jnp.concatenate([win.real, win.imag], axis=-1)
    wb = jnp.concatenate([win.imag, win.real], axis=-1)
    co = cp[:, :, 1:].reshape(g, p, w)
    wo = jnp.concatenate([co.real, -co.imag], axis=1)
    lt = pw[:, t]
    c1 = jnp.concatenate([lt.real, lt.real], axis=-1)
    c2a = jnp.concatenate([-lt.imag, lt.imag], axis=-1)
    c2b = jnp.concatenate([lt.imag, -lt.imag], axis=-1)
    dtile = jnp.tile(d.astype(F32).reshape(g, 1, gd), (1, t, 1)).reshape(g, 1, t * gd)
    return (m.astype(BF16), wa.astype(BF16), wb.astype(BF16), wo.astype(BF16), c1, c2a, c2b, dtile)


def _out_proj_kernel(ya_ref, yr_ref, ys_ref, x_ref, gg_ref, wglu_ref, wo_ref, gf_ref, *rest, with_router):
    if with_router:
        rt_ref, xo_ref, h_ref, lg_ref = rest
    else:
        xo_ref, h_ref = rest
    ys = ys_ref[...]
    ys = ys.astype(F32) * _sigmoid(_dot(ys.astype(BF16), wglu_ref[...].astype(BF16)))
    acc = x_ref[...]
    lo = 0
    for y in (ya_ref[...].astype(F32), yr_ref[...].astype(F32), ys):
        w = y.shape[1]
        acc = acc + _dot(_rms(y, gg_ref[:, lo:lo + w]).astype(BF16), wo_ref[lo:lo + w, :].astype(BF16))
        lo += w
    xo_ref[...] = acc
    h = _rms(acc, gf_ref[...])
    hb = h.astype(BF16)
    h_ref[...] = hb
    if with_router:
        both = _dot(hb, rt_ref[...])
        h_lo = (h - hb.astype(F32)).astype(BF16)
        lg_ref[...] = both[:, 0:LANES] + both[:, LANES:2 * LANES] + _dot(h_lo, rt_ref[:, 0:LANES])


def _out_proj(ya, yr, ys, x2, gg, wglu_all, wo_all, layer, gf, router=None):
    n, dm = x2.shape
    tm = min(ROW_TILE, n)
    row = lambda w: pl.BlockSpec((tm, w), lambda i: (i, 0))
    ins = [ya, yr, ys, x2, gg.reshape(1, -1), wglu_all, wo_all, gf.reshape(1, dm)]
    in_specs = [row(ya.shape[1]), row(yr.shape[1]), row(ys.shape[1]), row(dm), _const_spec((1, gg.shape[0])),
                _layer_spec(wglu_all, layer), _layer_spec(wo_all, layer), _const_spec((1, dm))]
    out_specs = [row(dm), row(dm)]
    out_shape = [jax.ShapeDtypeStruct((n, dm), F32), jax.ShapeDtypeStruct((n, dm), BF16)]
    if router is not None:
        ne = router.shape[1]
        r_hi = router.astype(BF16)
        r_lo = (router.astype(F32) - r_hi.astype(F32)).astype(BF16)
        pad = lambda a: jnp.pad(a, ((0, 0), (0, LANES - ne)))
        ins.append(jnp.concatenate([pad(r_hi), pad(r_lo)], axis=1))
        in_specs.append(_const_spec((dm, 2 * LANES)))
        out_specs.append(row(LANES))
        out_shape.append(jax.ShapeDtypeStruct((n, LANES), F32))
    return pl.pallas_call(
        functools.partial(_out_proj_kernel, with_router=router is not None),
        grid=(n // tm,),
        in_specs=in_specs, out_specs=out_specs, out_shape=out_shape,
        compiler_params=_cparams("parallel"),
        name="out_proj",
    )(*ins)


def _swiglu_chunks(h, w1, w3, w2, chunks, before_chunk=None, after_chunk=None):
    acc = None
    for j, (lo, width) in enumerate(chunks):
        if before_chunk is not None:
            before_chunk(j)
        a = _dot(h, w1[:, lo:lo + width].astype(BF16))
        b = _dot(h, w3[:, lo:lo + width].astype(BF16))
        t = (a * _sigmoid(a) * b).astype(BF16)
        y = _dot(t, w2[lo:lo + width, :].astype(BF16))
        acc = y if acc is None else acc + y
        if after_chunk is not None:
            after_chunk(j)
    return acc


def _ffn_chunks(f):
    if f == sum(w for _, w in FFN_CHUNKS):
        return FFN_CHUNKS
    return ((0, f),)


def _dense_ffn_kernel(h_ref, x_ref, w1_ref, w3_ref, w2_ref, o_ref, *, chunks):
    o_ref[...] = x_ref[...] + _swiglu_chunks(h_ref[...], w1_ref, w3_ref, w2_ref, chunks)


def _dense_ffn(h, x2, w1, w3, w2):
    n, dm = x2.shape
    tm = min(ROW_TILE, n)
    row = pl.BlockSpec((tm, dm), lambda i: (i, 0))
    return pl.pallas_call(
        functools.partial(_dense_ffn_kernel, chunks=_ffn_chunks(w1.shape[1])),
        grid=(n // tm,),
        in_specs=[row, row, _const_spec(w1.shape), _const_spec(w3.shape), _const_spec(w2.shape)],
        out_specs=row,
        out_shape=jax.ShapeDtypeStruct((n, dm), F32),
        compiler_params=_cparams("parallel"),
        name="dense_ffn",
    )(h, x2, w1, w3, w2)


def _expert_weight_copies(hbm, vmem, sems, chunks, expert, j):
    h1, h3, h2 = hbm
    v1, v3, v2 = vmem
    lo, w = chunks[j]
    return [pltpu.make_async_copy(h1.at[expert, :, lo:lo + w], v1.at[:, lo:lo + w], sems.at[j, 0]),
            pltpu.make_async_copy(h3.at[expert, :, lo:lo + w], v3.at[:, lo:lo + w], sems.at[j, 1]),
            pltpu.make_async_copy(h2.at[expert, lo:lo + w, :], v2.at[lo:lo + w, :], sems.at[j, 2])]


def _expert_ffn_kernel(brow_ref, bexp_ref, role_ref, x_ref, w1_hbm, w3_hbm, w2_hbm, o_ref, w1_v, w3_v, w2_v, sems,
                       *, chunks):
    i = pl.program_id(0)
    copies = functools.partial(_expert_weight_copies, (w1_hbm, w3_hbm, w2_hbm), (w1_v, w3_v, w2_v), sems, chunks)

    @pl.when(i == 0)
    def _():
        for j in range(len(chunks)):
            for cp in copies(bexp_ref[0], j):
                cp.start()

    def block(first, last):
        def before(j):
            for cp in copies(bexp_ref[i], j):
                cp.wait()

        def after(j):
            for cp in copies(bexp_ref[i + 1], j):
                cp.start()

        o_ref[...] = _swiglu_chunks(x_ref[...], w1_v, w3_v, w2_v, chunks, before if first else None,
                                    after if last else None).astype(o_ref.dtype)

    for role in range(4):
        pl.when(role_ref[i] == role)(functools.partial(block, bool(role & 1), bool(role & 2)))


def _expert_ffn(xb, w1, w3, w2, blk_row, blk_exp, role, max_blocks):
    rows, dm = xb.shape
    f = w1.shape[2]
    chunks = _ffn_chunks(f)
    row = pl.BlockSpec((MOE_BLK, dm), lambda i, br, be, ro: (br[i], 0))
    hbm = pl.BlockSpec(memory_space=pl.ANY)
    return pl.pallas_call(
        functools.partial(_expert_ffn_kernel, chunks=chunks),
        grid_spec=pltpu.PrefetchScalarGridSpec(
            num_scalar_prefetch=3, grid=(max_blocks,),
            in_specs=[row, hbm, hbm, hbm], out_specs=row,
            scratch_shapes=[pltpu.VMEM(w1.shape[1:], w1.dtype), pltpu.VMEM(w3.shape[1:], w3.dtype),
                            pltpu.VMEM(w2.shape[1:], w2.dtype), pltpu.SemaphoreType.DMA((len(chunks), 3))]),
        out_shape=jax.ShapeDtypeStruct((rows, dm), BF16),
        compiler_params=_cparams("arbitrary"),
        name="expert_ffn",
    )(blk_row, blk_exp, role, xb, w1, w3, w2)


def _route_tile(logits_tok, filled, ne):
    tm = logits_tok.shape[0]
    logits = logits_tok.T[0:ne, :]
    eidx = lax.broadcasted_iota(I32, (ne, tm), 0)
    m1 = jnp.max(logits, axis=0, keepdims=True)
    i1 = jnp.min(jnp.where(logits == m1, eidx, ne), axis=0, keepdims=True)
    sel1 = eidx == i1
    rest = jnp.where(sel1, -jnp.inf, logits)
    m2 = jnp.max(rest, axis=0, keepdims=True)
    i2 = jnp.min(jnp.where(rest == m2, eidx, ne), axis=0, keepdims=True)
    sel2 = eidx == i2
    e2 = jnp.exp(m2 - m1)
    den = 1.0 + e2
    rf = jnp.where(sel1 | sel2, 1.0, 0.0)
    cnt = jnp.sum(rf, axis=1, keepdims=True)
    before = (lax.broadcasted_iota(I32, (tm, tm), 0) < lax.broadcasted_iota(I32, (tm, tm), 1))
    rank = _dot(rf.astype(BF16), jnp.where(before, 1.0, 0.0).astype(BF16))
    whole = lambda a: jnp.floor(a * (1.0 / BF16_ROWS)) * BF16_ROWS
    phase = filled - whole(filled)
    span = jnp.where(cnt > 0, whole(phase + cnt + (BF16_ROWS - 1)), 0.0)
    ecol = lax.broadcasted_iota(I32, (ne, 1), 0)
    off = jnp.zeros((ne, 1), F32)
    for j in range(ne - 1):
        off = off + jnp.where(ecol > j, span[j:j + 1, :], 0.0)
    place = off + phase + rank
    pos1 = jnp.sum(jnp.where(sel1, place, 0.0), axis=0, keepdims=True)
    pos2 = jnp.sum(jnp.where(sel2, place, 0.0), axis=0, keepdims=True)
    lane_major = jnp.concatenate([pos1, pos2, 1.0 / den, e2 / den, jnp.zeros((4, tm), F32)], axis=0)
    tok_major = jnp.concatenate([lane_major, jnp.zeros((LANES - 8, tm), F32)], axis=0).T
    return lane_major, tok_major, cnt


def _route_kernel(lg_ref, lane_ref, tok_ref, cnt_ref, filled, *, ne):
    tm = MOE_TILE

    @pl.when(pl.program_id(0) == 0)
    def _():
        filled[...] = jnp.zeros_like(filled)

    for t in range(lg_ref.shape[0] // tm):
        lane_major, tok_major, cnt = _route_tile(lg_ref[t * tm:(t + 1) * tm, :], filled[...], ne)
        lane_ref[:, t * tm:(t + 1) * tm] = lane_major
        tok_ref[t * tm:(t + 1) * tm, :] = tok_major
        cnt_ref[t] = jnp.broadcast_to(cnt, (ne, LANES)).astype(I32)
        filled[...] = filled[...] + cnt


def _route(logits, ne):
    n = logits.shape[0]
    tm = MOE_TILE
    span = min(ROUTE_SPAN, n)
    return pl.pallas_call(
        functools.partial(_route_kernel, ne=ne),
        grid=(n // span,),
        in_specs=[pl.BlockSpec((span, LANES), lambda i: (i, 0))],
        out_specs=[pl.BlockSpec((8, span), lambda i: (0, i)), pl.BlockSpec((span, LANES), lambda i: (i, 0)),
                   pl.BlockSpec((span // tm, ne, LANES), lambda i: (i, 0, 0))],
        out_shape=[jax.ShapeDtypeStruct((8, n), F32), jax.ShapeDtypeStruct((n, LANES), F32),
                   jax.ShapeDtypeStruct((n // tm, ne, LANES), I32)],
        scratch_shapes=[pltpu.VMEM((ne, 1), F32)],
        compiler_params=_cparams("arbitrary"),
        name="moe_route",
    )(logits)


def _segment_copies(seg_ref, grouped_ref, base_ref, tile_buf, off_ref, sems, tile, slot, ne, *, to_grouped, wait):
    for e in range(ne):
        seg = seg_ref[tile * ne + e]
        far = base_ref[tile * ne + e]
        near = off_ref[tile * ne + e]
        for piece in SEG_PIECES:
            g = grouped_ref.at[pl.ds(pl.multiple_of(far, BF16_ROWS), piece)]
            t = tile_buf.at[slot, pl.ds(pl.multiple_of(near, BF16_ROWS), piece)]
            cp = pltpu.make_async_copy(t, g, sems.at[slot, e]) if to_grouped else \
                pltpu.make_async_copy(g, t, sems.at[slot, e])
            has = (seg & piece) != 0

            @pl.when(has)
            def _():
                cp.wait() if wait else cp.start()

            step = jnp.where(has, piece, 0)
            far = far + step
            near = near + step


def _tile_rows(ne):
    return TOP_K * MOE_TILE + ne * 2 * BF16_ROWS


def _tail_copies(end_ref, zeros, xb_ref, sems, ne):
    return [pltpu.make_async_copy(zeros, xb_ref.at[pl.ds(pl.multiple_of(end_ref[e], BF16_ROWS), MOE_BLK)], sems.at[e])
            for e in range(ne)]


def _dispatch_kernel(seg_ref, base_ref, off_ref, phase_ref, end_ref, rt_ref, h_ref, xb_ref, res, zeros, partial,
                     sems, tail_sems):
    i = pl.program_id(0)
    nt = pl.num_programs(0)
    ne = tail_sems.shape[0]
    tm = h_ref.shape[0]
    mrows = res.shape[1]
    slot = i % 2
    copies = functools.partial(_segment_copies, seg_ref, xb_ref, base_ref, res, off_ref, sems, ne=ne, to_grouped=True)

    @pl.when(i == 0)
    def _():
        zeros[...] = jnp.zeros_like(zeros)
        partial[...] = jnp.zeros_like(partial)
        for cp in _tail_copies(end_ref, zeros, xb_ref, tail_sems, ne):
            cp.start()

    row = lax.broadcasted_iota(I32, (mrows, tm), 0)
    pos = rt_ref[0:2, :].astype(I32)
    perm = jnp.where((row == pos[0:1, :]) | (row == pos[1:2, :]), 1.0, 0.0).astype(BF16)
    res[slot] = _dot(perm, h_ref[...]).astype(BF16)

    tile_row = lax.broadcasted_iota(I32, (BF16_ROWS, res.shape[2]), 0)
    for e in range(ne):
        @pl.when(seg_ref[i * ne + e] > 0)
        def _():
            first = pl.ds(pl.multiple_of(off_ref[i * ne + e], BF16_ROWS), BF16_ROWS)
            merged = jnp.where(tile_row < phase_ref[i * ne + e], partial[e].astype(F32), res[slot, first, :].astype(F32))
            res[slot, first, :] = merged.astype(BF16)
            last = pl.ds(pl.multiple_of(off_ref[i * ne + e] + seg_ref[i * ne + e] - BF16_ROWS, BF16_ROWS), BF16_ROWS)
            partial[e] = res[slot, last, :]

    @pl.when(i > 0)
    def _():
        copies(tile=i - 1, slot=1 - slot, wait=True)

    copies(tile=i, slot=slot, wait=False)

    @pl.when(i == nt - 1)
    def _():
        copies(tile=i, slot=slot, wait=True)
        for cp in _tail_copies(end_ref, zeros, xb_ref, tail_sems, ne):
            cp.wait()


def _dispatch(h, rt_lane, lay, ne):
    n, dm = h.shape
    tm = MOE_TILE
    return pl.pallas_call(
        _dispatch_kernel,
        grid_spec=pltpu.PrefetchScalarGridSpec(
            num_scalar_prefetch=5, grid=(n // tm,),
            in_specs=[pl.BlockSpec((8, tm), lambda i, *_: (0, i)), pl.BlockSpec((tm, dm), lambda i, *_: (i, 0))],
            out_specs=pl.BlockSpec(memory_space=pl.ANY),
            scratch_shapes=[pltpu.VMEM((2, _tile_rows(ne), dm), BF16), pltpu.VMEM((MOE_BLK, dm), BF16),
                            pltpu.VMEM((ne, BF16_ROWS, dm), BF16),
                            pltpu.SemaphoreType.DMA((2, ne)), pltpu.SemaphoreType.DMA((ne,))]),
        out_shape=jax.ShapeDtypeStruct((lay['rows_total'], dm), BF16),
        compiler_params=_cparams("arbitrary"),
        name="moe_dispatch",
    )(lay['seg'], lay['base'], lay['off'], lay['phase'], lay['end'], rt_lane, h)


def _combine_kernel(seg_ref, base_ref, off_ref, rt_ref, x_ref, gn_ref, yb_ref, o_ref, got, sems, *, ne):
    i = pl.program_id(0)
    nt = pl.num_programs(0)
    tm = x_ref.shape[0]
    mrows = got.shape[1]
    slot = i % 2
    copies = functools.partial(_segment_copies, seg_ref, yb_ref, base_ref, got, off_ref, sems, ne=ne, to_grouped=False)

    @pl.when(i == 0)
    def _():
        got[...] = jnp.zeros_like(got)
        copies(tile=i, slot=slot, wait=False)

    @pl.when(i + 1 < nt)
    def _():
        copies(tile=i + 1, slot=1 - slot, wait=False)

    copies(tile=i, slot=slot, wait=True)

    col = lax.broadcasted_iota(I32, (tm, mrows), 1)
    rt = rt_ref[...]
    rows = got[slot]
    sel = jnp.concatenate([jnp.where(col == rt[:, k:k + 1].astype(I32), 1.0, 0.0).astype(BF16) for k in range(TOP_K)],
                          axis=0)
    picked = _dot(sel, rows)
    y = sum(rt[:, TOP_K + k:TOP_K + k + 1] * picked[k * tm:(k + 1) * tm] for k in range(TOP_K))
    o_ref[...] = _rms(x_ref[...] + y, gn_ref[...])


def _combine(yb, rt_tok, x2, gn, lay, ne):
    n, dm = x2.shape
    tm = MOE_TILE
    mrows = _tile_rows(ne)
    tok = lambda w: pl.BlockSpec((tm, w), lambda i, *_: (i, 0))
    return pl.pallas_call(
        functools.partial(_combine_kernel, ne=ne),
        grid_spec=pltpu.PrefetchScalarGridSpec(
            num_scalar_prefetch=3, grid=(n // tm,),
            in_specs=[tok(LANES), tok(dm), pl.BlockSpec((1, dm), lambda i, *_: (0, 0)),
                      pl.BlockSpec(memory_space=pl.ANY)],
            out_specs=tok(dm),
            scratch_shapes=[pltpu.VMEM((2, mrows, dm), BF16), pltpu.SemaphoreType.DMA((2, ne))]),
        out_shape=jax.ShapeDtypeStruct((n, dm), F32),
        compiler_params=_cparams("arbitrary"),
        name="moe_combine",
    )(lay['seg'], lay['base'], lay['off'], rt_tok, x2, gn.reshape(1, dm), yb)


def _moe_layout(cnt, n):
    nt, ne = cnt.shape
    blk = MOE_BLK
    whole = lambda a: a // BF16_ROWS * BF16_ROWS
    cap = -(-(n + BF16_ROWS + blk) // blk) * blk
    region = jnp.arange(ne, dtype=I32) * cap
    filled = jnp.cumsum(cnt, axis=0) - cnt
    phase = filled - whole(filled)
    span = jnp.where(cnt > 0, whole(phase + cnt + BF16_ROWS - 1), 0)
    base = (region[None, :] + filled - phase).astype(I32)
    off = (jnp.cumsum(span, axis=1) - span).astype(I32)
    rows_e = jnp.sum(cnt, axis=0)
    nblk_e = (rows_e + blk - 1) // blk
    end = (region + whole(rows_e + BF16_ROWS - 1)).astype(I32)
    ends = jnp.cumsum(nblk_e)
    max_blocks = TOP_K * n // blk + ne
    bid = jnp.arange(max_blocks + 1, dtype=I32)
    bexp = jnp.minimum(jnp.sum((bid[:, None] >= ends[None, :]).astype(I32), axis=1), ne - 1)
    brow = bexp * (cap // blk) + bid - (ends - nblk_e)[bexp]
    nblk = ends[-1]
    last = jnp.maximum(nblk - 1, 0)
    used = bid < nblk
    first = bid == (ends - nblk_e)[bexp]
    hand_over = (bid == ends[bexp] - 1) & (bid < last)
    role = jnp.where(used, first.astype(I32) + 2 * hand_over.astype(I32), 4).astype(I32)
    bexp = jnp.where(used, bexp, bexp[last]).astype(I32)
    brow = jnp.where(used, brow, brow[last]).astype(I32)
    flat = lambda a: a.reshape(-1)
    return dict(seg=flat(span.astype(I32)), base=flat(base), off=flat(off), phase=flat(phase.astype(I32)), end=end,
                brow=brow, bexp=bexp, role=role, rows_total=ne * cap, max_blocks=max_blocks)


def _moe(x2, h, logits, gn, w1, w3, w2):
    n, dm = x2.shape
    ne = w1.shape[0]
    rt_lane, rt_tok, cnt = _route(logits, ne)
    lay = _moe_layout(cnt[:, :, 0], n)
    xb = _dispatch(h, rt_lane, lay, ne)
    yb = _expert_ffn(xb, w1, w3, w2, lay['brow'], lay['bexp'], lay['role'], lay['max_blocks'])
    return _combine(yb, rt_tok, x2, gn, lay, ne)


def kernel(x, norm_mix_g, w_in, attn_rel_bias, rg_conv_w, rg_conv_b, rg_wx, rg_bx, rg_wa, rg_ba, rg_lambda, s5_a_re, s5_a_im, s5_log_dt, s5_b_re, s5_b_im, s5_c_re, s5_c_im, s5_d, s5_w_glu, g_group, w_out, norm_ffn_g, ffn_w1, ffn_w3, ffn_w2, moe_router, moe_w1, moe_w3, moe_w2, final_norm_g):
    bsz, l, dm = x.shape
    depth = w_in.shape[0]
    assert depth == 2 and l % ATTN_TQ == 0 and ATTN_TQ == N_PREV * CHUNK, "dense layer, then the MoE layer"
    d_rg = rg_conv_w.shape[2]
    d_s5 = s5_w_glu.shape[1]
    d_attn = (w_in.shape[2] - 2 * d_rg - d_s5) // 3
    x2 = x.reshape(bsz * l, dm)
    attn_bias = jax.vmap(_attn_bias)(attn_rel_bias)
    rg_gates_w = jnp.concatenate([jax.vmap(_block_diag)(rg_wx), jax.vmap(_block_diag)(rg_wa)], axis=2).astype(BF16)
    rg_gates_b = jnp.concatenate([rg_bx, rg_ba], axis=1).astype(F32)[:, None, :]
    rg_decay = jax.nn.softplus(-rg_lambda.astype(F32))[:, None, :]
    s5_prm = jax.vmap(_s5_params)(s5_a_re, s5_a_im, s5_log_dt, s5_b_re, s5_b_im, s5_c_re, s5_c_im, s5_d)
    for layer in range(depth):
        q, k, vt, xr, gate, us = _in_proj(x2, norm_mix_g[layer], w_in, layer, d_attn, d_rg, d_s5)
        y_attn = _attention(q, k, vt, attn_bias, layer, bsz, l)
        y_rg = _rglru(xr, gate, rg_conv_w.astype(F32), rg_conv_b.astype(F32)[:, None, :], rg_gates_w, rg_gates_b,
                      rg_decay, layer, bsz, l)
        y_s5 = _s5(us, s5_prm, layer, bsz, l)
        mixed = (y_attn, y_rg, y_s5, x2, g_group[layer].astype(F32), s5_w_glu, w_out, layer,
                 norm_ffn_g[layer].astype(F32))
        if layer == 0:
            x2, h = _out_proj(*mixed)
            x2 = _dense_ffn(h, x2, ffn_w1[0], ffn_w3[0], ffn_w2[0])
        else:
            x2, h, logits = _out_proj(*mixed, router=moe_router[0])
            x2 = _moe(x2, h, logits, final_norm_g.astype(F32), moe_w1[0], moe_w3[0], moe_w2[0])
    return x2.reshape(bsz, l, dm)
```

```python
import functools
import math

import jax
import jax.numpy as jnp
import numpy as np
from jax import lax
from jax.experimental import pallas as pl
from jax.experimental.pallas import tpu as pltpu

F32 = jnp.float32
BF16 = jnp.bfloat16
I32 = jnp.int32

EPS = 1e-6
CHUNK = 64
N_PREV = 8
BAND = (N_PREV + 1) * CHUNK
REL_CLIP = 128
HEAD_DIM = 64
RG_C = 8.0
RG_CONV_WIDTH = 4
S5_GROUP_DIM = 16
S5_T = 16
TOP_K = 2
NEG_BIG = -1e30

LANES = 128
BF16_ROWS = 16
VMEM_LIMIT = 52 * 1024 * 1024
MIX_VMEM_LIMIT = 58 * 1024 * 1024

ROW_TILE = 512
PROJ_TILE = 1024
ATTN_TQ = 512
ATTN_SPAN = 2 * CHUNK
ATTN_WIN = N_PREV * CHUNK + ATTN_SPAN
RG_TL = 1024
RG_SEG = 8
MOE_TILE = 256
MOE_BLK = 512
ROUTE_SPAN = 2048
SEG_PIECES = tuple(BF16_ROWS << b for b in range((MOE_TILE // BF16_ROWS).bit_length() - 1, -1, -1))
FFN_CHUNKS = ((0, 512), (512, 512), (1024, 512), (1536, 512), (2048, 512), (2560, 256))


def _cparams(*sem):
    return pltpu.CompilerParams(dimension_semantics=sem, vmem_limit_bytes=VMEM_LIMIT)


def _const_spec(shape):
    nd = len(shape)
    return pl.BlockSpec(shape, lambda *_: (0,) * nd, pipeline_mode=pl.Buffered(1))


def _rms(xf, g):
    var = jnp.mean(xf * xf, axis=-1, keepdims=True)
    return xf * lax.rsqrt(var + EPS) * g


def _sigmoid(x):
    return 1.0 / (1.0 + jnp.exp(-x))


def _gelu(x):
    c = math.sqrt(2.0 / math.pi)
    return 0.5 * x * (1.0 + jnp.tanh(c * (x + 0.044715 * (x * x * x))))


def _dot(a, b):
    return jnp.dot(a, b, preferred_element_type=F32)


def _in_proj_kernel(x_ref, g_ref, w_ref, wvt_ref, q_ref, k_ref, vt_ref, xr_ref, gt_ref, us_ref, *, d_attn, d_rg):
    u = _rms(x_ref[...], g_ref[...]).astype(BF16)
    lo = 0
    for ref, width in ((q_ref, d_attn), (k_ref, d_attn), (None, d_attn),
                       (xr_ref, d_rg), (gt_ref, d_rg), (us_ref, w_ref.shape[1] - 3 * d_attn - 2 * d_rg)):
        if ref is not None:
            ref[...] = _dot(u, w_ref[:, lo:lo + width].astype(BF16)).astype(ref.dtype)
        lo += width
    vt_ref[...] = lax.dot_general(wvt_ref[...].astype(BF16), u, (((1,), (1,)), ((), ())),
                                  preferred_element_type=F32).astype(vt_ref.dtype)


def _layer_spec(stacked, layer):
    rest = stacked.shape[1:]
    return pl.BlockSpec((None,) + rest, lambda *_: (layer,) + (0,) * len(rest), pipeline_mode=pl.Buffered(1))


def _in_proj(x2, g, w_all, layer, d_attn, d_rg, d_s5):
    n, dm = x2.shape
    tm = min(PROJ_TILE, n)
    row = lambda w: pl.BlockSpec((tm, w), lambda i: (i, 0))
    wv_t = w_all[layer, :, 2 * d_attn:3 * d_attn].T
    return pl.pallas_call(
        functools.partial(_in_proj_kernel, d_attn=d_attn, d_rg=d_rg),
        grid=(n // tm,),
        in_specs=[row(dm), _const_spec((1, dm)), _layer_spec(w_all, layer), _const_spec(wv_t.shape)],
        out_specs=[row(d_attn), row(d_attn), pl.BlockSpec((d_attn, tm), lambda i: (0, i)),
                   row(d_rg), row(d_rg), row(d_s5)],
        out_shape=[jax.ShapeDtypeStruct((n, d_attn), BF16)] * 2 + [jax.ShapeDtypeStruct((d_attn, n), BF16)]
        + [jax.ShapeDtypeStruct((n, d_rg), F32)] * 2 + [jax.ShapeDtypeStruct((n, d_s5), F32)],
        compiler_params=_cparams("parallel"),
        name="in_proj",
    )(x2, g.reshape(1, dm), w_all, wv_t)


def _attn_kernel(q_ref, kp_ref, kc_ref, vp_ref, vc_ref, bias_ref, o_ref, kz, vwin, sc, *, n_pairs):
    qi = pl.program_id(1)
    tq = q_ref.shape[0]
    vrows = vwin.shape[1] // n_pairs
    kz[0:tq, :] = kp_ref[...]
    kz[tq:2 * tq, :] = kc_ref[...]
    for j in range(tq // ATTN_SPAN):
        a = tq - j * ATTN_SPAN
        for hp in range(n_pairs):
            chan = slice(hp * LANES, (hp + 1) * LANES)
            vwin[j, hp * vrows:hp * vrows + LANES, 0:a] = vp_ref[chan, j * ATTN_SPAN:tq]
            vwin[j, hp * vrows:hp * vrows + LANES, a:ATTN_WIN] = vc_ref[chan, 0:ATTN_WIN - a]
            vwin[j, hp * vrows + LANES:(hp + 1) * vrows, :] = jnp.ones((vrows - LANES, ATTN_WIN), BF16)
    first = lax.broadcasted_iota(I32, (ATTN_SPAN, LANES), 1) < HEAD_DIM
    pairs = [slice(hp * LANES, (hp + 1) * LANES) for hp in range(n_pairs)]

    def score(j, masked):
        r0 = j * ATTN_SPAN if isinstance(j, int) else pl.multiple_of(j * ATTN_SPAN, ATTN_SPAN)
        for hp, cols in enumerate(pairs):
            q2 = q_ref[pl.ds(r0, ATTN_SPAN), cols] * jnp.asarray(HEAD_DIM ** -0.5, BF16)
            zero = jnp.zeros_like(q2)
            qq = jnp.concatenate([jnp.where(first, q2, zero), jnp.where(first, zero, q2)], axis=0)
            s = lax.dot_general(kz[pl.ds(r0, ATTN_WIN), cols], qq, (((1,), (1,)), ((), ())),
                                preferred_element_type=F32) + bias_ref[hp]
            if masked:
                key = lax.broadcasted_iota(I32, (ATTN_WIN, 2 * ATTN_SPAN), 0)
                s = jnp.where(key >= tq - r0, s, NEG_BIG)
            sc[j % 2, hp] = s

    def finish(j):
        r0 = j * ATTN_SPAN if isinstance(j, int) else pl.multiple_of(j * ATTN_SPAN, ATTN_SPAN)
        probs = []
        for hp in range(n_pairs):
            s = sc[j % 2, hp]
            probs.append(jnp.exp((s - jnp.max(s, axis=0, keepdims=True)).astype(BF16)))
        for hp, (cols, p) in enumerate(zip(pairs, probs)):
            ov = _dot(vwin[j, hp * vrows:(hp + 1) * vrows, :], p)
            o2 = (ov[0:LANES] / ov[LANES:LANES + 1]).T
            o_ref[pl.ds(r0, ATTN_SPAN), cols] = jnp.where(first, o2[0:ATTN_SPAN], o2[ATTN_SPAN:]).astype(o_ref.dtype)

    def run(masked):
        groups = tq // ATTN_SPAN
        score(0, masked)
        for j in range(groups):
            if j + 1 < groups:
                score(j + 1, masked)
            finish(j)

    pl.when(qi == 0)(functools.partial(run, True))
    pl.when(qi != 0)(functools.partial(run, False))


def _attention(q, k, vt, bias_t, layer, bsz, l):
    n, da = q.shape
    tq = ATTN_TQ
    nt = l // tq
    n_pairs = da // LANES
    cur = pl.BlockSpec((tq, da), lambda b, i: (b * nt + i, 0))
    prev = pl.BlockSpec((tq, da), lambda b, i: (b * nt + jnp.maximum(i - 1, 0), 0))
    cur_t = pl.BlockSpec((da, tq), lambda b, i: (0, b * nt + i))
    prev_t = pl.BlockSpec((da, tq), lambda b, i: (0, b * nt + jnp.maximum(i - 1, 0)))
    return pl.pallas_call(
        functools.partial(_attn_kernel, n_pairs=n_pairs),
        grid=(bsz, nt),
        in_specs=[cur, prev, cur, prev_t, cur_t, _layer_spec(bias_t, layer)],
        out_specs=cur,
        out_shape=jax.ShapeDtypeStruct((n, da), BF16),
        scratch_shapes=[pltpu.VMEM((2 * tq, da), BF16),
                        pltpu.VMEM((tq // ATTN_SPAN, n_pairs * (LANES + BF16_ROWS), ATTN_WIN), BF16),
                        pltpu.VMEM((2, n_pairs, ATTN_WIN, 2 * ATTN_SPAN), F32)],
        compiler_params=_cparams("parallel", "parallel"),
        name="chunk_attn",
    )(q, k, k, vt, vt, bias_t)


def _attn_bias(rel_bias):
    h = rel_bias.shape[0]
    tab = rel_bias.astype(F32)
    n_far = N_PREV * CHUNK - REL_CLIP + CHUNK
    lo = 2 * REL_CLIP - (BAND + CHUNK - 1 - n_far)
    ext = jnp.concatenate([jnp.broadcast_to(tab[:, 2 * REL_CLIP:], (h, n_far)), tab[:, lo:2 * REL_CLIP][:, ::-1]], axis=1)
    wide = BAND + CHUNK
    ring = jnp.concatenate([ext[:, CHUNK - 1:], jnp.zeros((h, 1), F32), ext[:, :CHUNK - 1]], axis=1)
    bias = jnp.broadcast_to(ring[:, None, :], (h, CHUNK, wide)).reshape(h, CHUNK * wide)
    bias = bias[:, :CHUNK * (wide - 1)].reshape(h, CHUNK, wide - 1)[..., :BAND]
    band_t = bias.transpose(0, 2, 1)
    groups = ATTN_SPAN // CHUNK
    per_chunk = [jnp.pad(band_t, ((0, 0), (c * CHUNK, (groups - 1 - c) * CHUNK), (0, 0)), constant_values=NEG_BIG)
                 for c in range(groups)]
    win = jnp.stack(per_chunk, axis=2)
    return win.reshape(h // 2, 2, ATTN_WIN, ATTN_SPAN).transpose(0, 2, 1, 3).reshape(h // 2, ATTN_WIN, 2 * ATTN_SPAN)


def _rglru_kernel(x_ref, gt_ref, cw_ref, cb_ref, w_ref, b_ref, sp_ref, o_ref, xpad, a_s, b_s, hcar):
    t = pl.program_id(1)
    tl, c = x_ref.shape
    seg = tl // RG_SEG
    front = 8

    @pl.when(t == 0)
    def _():
        xpad[0:front, :] = jnp.zeros((front, c), F32)
        hcar[...] = jnp.zeros_like(hcar)

    xpad[front:front + tl, :] = x_ref[...]
    xc = cb_ref[...] + sum(
        cw_ref[j:j + 1, :] * xpad[front - (RG_CONV_WIDTH - 1) + j:front - (RG_CONV_WIDTH - 1) + j + tl, :]
        for j in range(RG_CONV_WIDTH))
    xpad[0:front, :] = xpad[tl:tl + front, :]
    pre = _dot(xc.astype(BF16), w_ref[...]) + b_ref[...]
    gx = _sigmoid(pre[:, 0:c])
    ga = _sigmoid(pre[:, c:2 * c])
    log_a = -RG_C * ga * sp_ref[...]
    a = jnp.exp(log_a)
    mult = jnp.sqrt(-jnp.tanh(log_a) * (a * a + 1.0))
    b = mult * gx * xc
    nh = c // LANES
    for j in range(nh):
        a_s[j] = a[:, j * LANES:(j + 1) * LANES]
        b_s[j] = b[:, j * LANES:(j + 1) * LANES]

    def step(i, carry):
        rows = pl.ds(i, RG_SEG, stride=seg)
        out = []
        for j in range(nh):
            h, p = carry[j]
            av = a_s[j, rows, :]
            h = av * h + b_s[j, rows, :]
            p = av * p
            b_s[j, rows, :] = h
            a_s[j, rows, :] = p
            out.append((h, p))
        return tuple(out)

    init = tuple((jnp.zeros((RG_SEG, LANES), F32), jnp.ones((RG_SEG, LANES), F32)) for _ in range(nh))
    ends = lax.fori_loop(0, seg, step, init, unroll=8)
    for j in range(nh):
        cols = slice(j * LANES, (j + 1) * LANES)
        h_end, p_end = ends[j]
        carry = hcar[:, cols]
        for s in range(RG_SEG):
            rows = slice(s * seg, (s + 1) * seg)
            h = b_s[j, rows, :] + a_s[j, rows, :] * carry
            o_ref[rows, cols] = (h * _gelu(gt_ref[rows, cols])).astype(o_ref.dtype)
            carry = h_end[s:s + 1, :] + p_end[s:s + 1, :] * carry
        hcar[:, cols] = carry


def _rglru(xr, gate, conv_w, conv_b, w_bd, b_cat, sp, layer, bsz, l):
    n, c = xr.shape
    tl = min(RG_TL, l)
    nt = l // tl
    row = pl.BlockSpec((tl, c), lambda b, t: (b * nt + t, 0))
    return pl.pallas_call(
        _rglru_kernel,
        grid=(bsz, nt),
        in_specs=[row, row] + [_layer_spec(a, layer) for a in (conv_w, conv_b, w_bd, b_cat, sp)],
        out_specs=row,
        out_shape=jax.ShapeDtypeStruct((n, c), BF16),
        scratch_shapes=[pltpu.VMEM((tl + 8, c), F32), pltpu.VMEM((c // LANES, tl, LANES), F32),
                        pltpu.VMEM((c // LANES, tl, LANES), F32), pltpu.VMEM((1, c), F32)],
        compiler_params=_cparams("parallel", "arbitrary"),
        name="rglru",
    )(xr, gate, conv_w, conv_b, w_bd, b_cat, sp)


def _block_diag(w):
    nb, d, _ = w.shape
    eye = jnp.eye(nb, dtype=w.dtype)
    return (eye[:, None, :, None] * w[:, :, None, :]).reshape(nb * d, nb * d)


def _s5_kernel(u_ref, m_ref, wa_ref, wb_ref, wo_ref, c1_ref, c2a_ref, c2b_ref, d_ref, y_ref, ps, ug, xa_s, xb_s):
    t, gd = S5_T, S5_GROUP_DIM
    ng, nk, _ = ug.shape
    nh = ps.shape[0]
    per_half = LANES // gd
    rows8 = 8

    for h in range(nh):
        ps[h] = u_ref[:, h * LANES:(h + 1) * LANES]

    lane_block = lax.broadcasted_iota(I32, (rows8, LANES), 1) // gd

    def block_transpose(vs):
        d = per_half // 2
        while d:
            keep = (lane_block & d) == 0
            out = list(vs)
            for i in range(per_half):
                if i & d == 0:
                    out[i] = jnp.where(keep, vs[i], pltpu.roll(vs[i + d], d * gd, 1))
                    out[i + d] = jnp.where(keep, pltpu.roll(vs[i], LANES - d * gd, 1), vs[i + d])
            vs, d = out, d // 2
        return vs

    def to_groups(r, carry):
        base = pl.multiple_of(r * rows8 * t, rows8 * t)
        rows = pl.ds(pl.multiple_of(r * rows8, rows8), rows8)
        for h in range(nh):
            for j in range(t // per_half):
                steps = [ps[h, pl.ds(base + j * per_half + i, rows8, stride=t), :] for i in range(per_half)]
                for k, v in enumerate(block_transpose(steps)):
                    ug[h * per_half + k, rows, j * LANES:(j + 1) * LANES] = v
        return carry

    lax.fori_loop(0, nk // rows8, to_groups, 0, unroll=2)

    def project(g, carry):
        u = ug[g]
        ub = u.astype(BF16)
        rows = pl.ds(pl.multiple_of(g * nk, nk), nk)
        xa_s[rows, :] = _dot(ub, wa_ref[g])
        xb_s[rows, :] = _dot(ub, wb_ref[g])
        ug[g] = _dot(ub, m_ref[g]) + d_ref[g] * u
        return carry

    lax.fori_loop(0, ng, project, 0)

    c1, c2a, c2b = c1_ref[...], c2a_ref[...], c2b_ref[...]

    def step(k, carry):
        xa, xb = carry
        rows = pl.ds(k, ng, stride=nk)
        ia = xa_s[rows, :]
        ib = xb_s[rows, :]
        xa_s[rows, :] = xa
        return c1 * xa + c2a * xb + ia, c1 * xb + c2b * xa + ib

    z = jnp.zeros((ng, xa_s.shape[1]), F32)
    lax.fori_loop(0, nk, step, (z, z), unroll=4)

    def respond(g, carry):
        rows = pl.ds(pl.multiple_of(g * nk, nk), nk)
        ug[g] = _gelu(ug[g] + _dot(xa_s[rows, :].astype(BF16), wo_ref[g]))
        return carry

    lax.fori_loop(0, ng, respond, 0)

    def from_groups(r, carry):
        base = pl.multiple_of(r * rows8 * t, rows8 * t)
        rows = pl.ds(pl.multiple_of(r * rows8, rows8), rows8)
        for h in range(nh):
            for j in range(t // per_half):
                groups = [ug[h * per_half + k, rows, j * LANES:(j + 1) * LANES] for k in range(per_half)]
                for i, v in enumerate(block_transpose(groups)):
                    ps[h, pl.ds(base + j * per_half + i, rows8, stride=t), :] = v
        return carry

    lax.fori_loop(0, nk // rows8, from_groups, 0, unroll=2)
    for h in range(nh):
        y_ref[:, h * LANES:(h + 1) * LANES] = ps[h].astype(y_ref.dtype)


def _s5(us, prm, layer, bsz, l):
    n, w = us.shape
    ng = w // S5_GROUP_DIM
    nk = l // S5_T
    p2 = prm[1].shape[-1]
    row = pl.BlockSpec((l, w), lambda b: (b, 0))
    return pl.pallas_call(
        _s5_kernel,
        grid=(bsz,),
        in_specs=[row] + [_layer_spec(a, layer) for a in prm],
        out_specs=row,
        out_shape=jax.ShapeDtypeStruct((n, w), BF16),
        scratch_shapes=[pltpu.VMEM((w // LANES, l, LANES), F32), pltpu.VMEM((ng, nk, S5_T * S5_GROUP_DIM), F32),
                        pltpu.VMEM((ng * nk, p2), F32), pltpu.VMEM((ng * nk, p2), F32)],
        compiler_params=_cparams("parallel"),
        name="s5",
    )(us, *prm)


def _s5_params(a_re, a_im, log_dt, b_re, b_im, c_re, c_im, d):
    t = S5_T
    g, p = a_re.shape
    gd = S5_GROUP_DIM
    hi = lax.Precision.HIGHEST
    a = lax.complex(a_re.astype(F32), a_im.astype(F32))
    dt = jnp.exp(log_dt.astype(F32))[:, None]
    a_bar = jnp.exp(a * dt)
    bm = lax.complex(b_re.astype(F32), b_im.astype(F32))
    cm = lax.complex(c_re.astype(F32), c_im.astype(F32))
    b_bar = ((a_bar - 1.0) / a)[..., None] * bm
    steps = jnp.arange(t + 1, dtype=F32)
    pw = jnp.exp((a * dt)[:, None, :] * steps[None, :, None])

    cp = (cm[:, None, :, :] * pw[:, :, None, :]).transpose(0, 3, 1, 2)
    w = t * gd
    lag = cp[:, :, :t].reshape(g, p, w)
    k2 = jnp.einsum('gpc,gpx->gcx', jnp.concatenate([b_bar.real, -b_bar.imag], axis=1),
                    jnp.concatenate([lag.real, lag.imag], axis=1), precision=hi)
    shift = np.zeros((t, w, w), np.float32)
    for s in range(t):
        shift[s, np.arange(w - s * gd), np.arange(w - s * gd) + s * gd] = 1.0
    m = jnp.einsum('gcx,sxl->gscl', k2, shift, precision=hi).reshape(g, w, w)
    win = pw[:, :t][:, ::-1][:, :, :, None] * b_bar[:, None]
    win = win.transpose(0, 1, 3, 2).reshape(g, w, p)
    wa = jnp.concatenate([win.real, win.imag], axis=-1)
    wb = jnp.concatenate([win.imag, win.real], axis=-1)
    co = cp[:, :, 1:].reshape(g, p, w)
    wo = jnp.concatenate([co.real, -co.imag], axis=1)
    lt = pw[:, t]
    c1 = jnp.concatenate([lt.real, lt.real], axis=-1)
    c2a = jnp.concatenate([-lt.imag, lt.imag], axis=-1)
    c2b = jnp.concatenate([lt.imag, -lt.imag], axis=-1)
    dtile = jnp.tile(d.astype(F32).reshape(g, 1, gd), (1, t, 1)).reshape(g, 1, t * gd)
    return (m.astype(BF16), wa.astype(BF16), wb.astype(BF16), wo.astype(BF16), c1, c2a, c2b, dtile)


def _out_proj_kernel(ya_ref, yr_ref, ys_ref, x_ref, gg_ref, wglu_ref, wo_ref, gf_ref, *rest, dense, chunks):
    ys = ys_ref[...]
    ys = ys.astype(F32) * _sigmoid(_dot(ys.astype(BF16), wglu_ref[...].astype(BF16)))
    acc = x_ref[...]
    lo = 0
    for y in (ya_ref[...].astype(F32), yr_ref[...].astype(F32), ys):
        w = y.shape[1]
        acc = acc + _dot(_rms(y, gg_ref[:, lo:lo + w]).astype(BF16), wo_ref[lo:lo + w, :].astype(BF16))
        lo += w
    h = _rms(acc, gf_ref[...])
    hb = h.astype(BF16)
    if dense:
        w1_ref, w3_ref, w2_ref, xo_ref = rest
        xo_ref[...] = acc + _swiglu_chunks(hb, w1_ref, w3_ref, w2_ref, chunks)
        return
    rt_ref, xo_ref, h_ref, lg_ref = rest
    xo_ref[...] = acc
    h_ref[...] = hb
    both = _dot(hb, rt_ref[...])
    h_lo = (h - hb.astype(F32)).astype(BF16)
    lg_ref[...] = both[:, 0:LANES] + both[:, LANES:2 * LANES] + _dot(h_lo, rt_ref[:, 0:LANES])


def _out_proj(ya, yr, ys, x2, gg, wglu_all, wo_all, layer, gf, *, ffn=None, router=None):
    n, dm = x2.shape
    tm = min(ROW_TILE, n)
    row = lambda w: pl.BlockSpec((tm, w), lambda i: (i, 0))
    ins = [ya, yr, ys, x2, gg.reshape(1, -1), wglu_all, wo_all, gf.reshape(1, dm)]
    in_specs = [row(ya.shape[1]), row(yr.shape[1]), row(ys.shape[1]), row(dm), _const_spec((1, gg.shape[0])),
                _layer_spec(wglu_all, layer), _layer_spec(wo_all, layer), _const_spec((1, dm))]
    if ffn is not None:
        ins += list(ffn)
        in_specs += [_const_spec(w.shape) for w in ffn]
        out_specs = row(dm)
        out_shape = jax.ShapeDtypeStruct((n, dm), F32)
        chunks = _ffn_chunks(ffn[0].shape[1])
    else:
        ne = router.shape[1]
        r_hi = router.astype(BF16)
        r_lo = (router.astype(F32) - r_hi.astype(F32)).astype(BF16)
        pad = lambda a: jnp.pad(a, ((0, 0), (0, LANES - ne)))
        ins.append(jnp.concatenate([pad(r_hi), pad(r_lo)], axis=1))
        in_specs.append(_const_spec((dm, 2 * LANES)))
        out_specs = [row(dm), row(dm), row(LANES)]
        out_shape = [jax.ShapeDtypeStruct((n, dm), F32), jax.ShapeDtypeStruct((n, dm), BF16),
                     jax.ShapeDtypeStruct((n, LANES), F32)]
        chunks = None
    return pl.pallas_call(
        functools.partial(_out_proj_kernel, dense=ffn is not None, chunks=chunks),
        grid=(n // tm,),
        in_specs=in_specs, out_specs=out_specs, out_shape=out_shape,
        compiler_params=pltpu.CompilerParams(dimension_semantics=("parallel",), vmem_limit_bytes=MIX_VMEM_LIMIT),
        name="out_proj",
    )(*ins)


def _swiglu_chunks(h, w1, w3, w2, chunks, before_chunk=None, after_chunk=None):
    acc = None
    for j, (lo, width) in enumerate(chunks):
        if before_chunk is not None:
            before_chunk(j)
        a = _dot(h, w1[:, lo:lo + width].astype(BF16))
        b = _dot(h, w3[:, lo:lo + width].astype(BF16))
        t = (a * _sigmoid(a) * b).astype(BF16)
        y = _dot(t, w2[lo:lo + width, :].astype(BF16))
        acc = y if acc is None else acc + y
        if after_chunk is not None:
            after_chunk(j)
    return acc


def _ffn_chunks(f):
    if f == sum(w for _, w in FFN_CHUNKS):
        return FFN_CHUNKS
    return ((0, f),)


def _expert_weight_copies(hbm, vmem, sems, chunks, expert, j):
    h1, h3, h2 = hbm
    v1, v3, v2 = vmem
    lo, w = chunks[j]
    return [pltpu.make_async_copy(h1.at[expert, :, lo:lo + w], v1.at[:, lo:lo + w], sems.at[j, 0]),
            pltpu.make_async_copy(h3.at[expert, :, lo:lo + w], v3.at[:, lo:lo + w], sems.at[j, 1]),
            pltpu.make_async_copy(h2.at[expert, lo:lo + w, :], v2.at[lo:lo + w, :], sems.at[j, 2])]


def _expert_ffn_kernel(brow_ref, bexp_ref, role_ref, x_ref, w1_hbm, w3_hbm, w2_hbm, o_ref, w1_v, w3_v, w2_v, sems,
                       *, chunks):
    i = pl.program_id(0)
    copies = functools.partial(_expert_weight_copies, (w1_hbm, w3_hbm, w2_hbm), (w1_v, w3_v, w2_v), sems, chunks)

    @pl.when(i == 0)
    def _():
        for j in range(len(chunks)):
            for cp in copies(bexp_ref[0], j):
                cp.start()

    def block(first, last):
        def before(j):
            for cp in copies(bexp_ref[i], j):
                cp.wait()

        def after(j):
            for cp in copies(bexp_ref[i + 1], j):
                cp.start()

        o_ref[...] = _swiglu_chunks(x_ref[...], w1_v, w3_v, w2_v, chunks, before if first else None,
                                    after if last else None).astype(o_ref.dtype)

    for role in range(4):
        pl.when(role_ref[i] == role)(functools.partial(block, bool(role & 1), bool(role & 2)))


def _expert_ffn(xb, w1, w3, w2, blk_row, blk_exp, role, max_blocks):
    rows, dm = xb.shape
    f = w1.shape[2]
    chunks = _ffn_chunks(f)
    row = pl.BlockSpec((MOE_BLK, dm), lambda i, br, be, ro: (br[i], 0))
    hbm = pl.BlockSpec(memory_space=pl.ANY)
    return pl.pallas_call(
        functools.partial(_expert_ffn_kernel, chunks=chunks),
        grid_spec=pltpu.PrefetchScalarGridSpec(
            num_scalar_prefetch=3, grid=(max_blocks,),
            in_specs=[row, hbm, hbm, hbm], out_specs=row,
            scratch_shapes=[pltpu.VMEM(w1.shape[1:], w1.dtype), pltpu.VMEM(w3.shape[1:], w3.dtype),
                            pltpu.VMEM(w2.shape[1:], w2.dtype), pltpu.SemaphoreType.DMA((len(chunks), 3))]),
        out_shape=jax.ShapeDtypeStruct((rows, dm), BF16),
        compiler_params=_cparams("arbitrary"),
        name="expert_ffn",
    )(blk_row, blk_exp, role, xb, w1, w3, w2)


def _route_tile(logits_tok, filled, ne):
    tm = logits_tok.shape[0]
    logits = logits_tok.T[0:ne, :]
    eidx = lax.broadcasted_iota(I32, (ne, tm), 0)
    m1 = jnp.max(logits, axis=0, keepdims=True)
    i1 = jnp.min(jnp.where(logits == m1, eidx, ne), axis=0, keepdims=True)
    sel1 = eidx == i1
    rest = jnp.where(sel1, -jnp.inf, logits)
    m2 = jnp.max(rest, axis=0, keepdims=True)
    i2 = jnp.min(jnp.where(rest == m2, eidx, ne), axis=0, keepdims=True)
    sel2 = eidx == i2
    e2 = jnp.exp(m2 - m1)
    den = 1.0 + e2
    rf = jnp.where(sel1 | sel2, 1.0, 0.0)
    cnt = jnp.sum(rf, axis=1, keepdims=True)
    before = (lax.broadcasted_iota(I32, (tm, tm), 0) < lax.broadcasted_iota(I32, (tm, tm), 1))
    rank = _dot(rf.astype(BF16), jnp.where(before, 1.0, 0.0).astype(BF16))
    whole = lambda a: jnp.floor(a * (1.0 / BF16_ROWS)) * BF16_ROWS
    phase = filled - whole(filled)
    span = jnp.where(cnt > 0, whole(phase + cnt + (BF16_ROWS - 1)), 0.0)
    ecol = lax.broadcasted_iota(I32, (ne, 1), 0)
    off = jnp.zeros((ne, 1), F32)
    for j in range(ne - 1):
        off = off + jnp.where(ecol > j, span[j:j + 1, :], 0.0)
    place = off + phase + rank
    pos1 = jnp.sum(jnp.where(sel1, place, 0.0), axis=0, keepdims=True)
    pos2 = jnp.sum(jnp.where(sel2, place, 0.0), axis=0, keepdims=True)
    lane_major = jnp.concatenate([pos1, pos2, 1.0 / den, e2 / den, jnp.zeros((4, tm), F32)], axis=0)
    tok_major = jnp.concatenate([lane_major, jnp.zeros((LANES - 8, tm), F32)], axis=0).T
    return lane_major, tok_major, cnt


def _route_kernel(lg_ref, lane_ref, tok_ref, cnt_ref, filled, *, ne):
    tm = MOE_TILE

    @pl.when(pl.program_id(0) == 0)
    def _():
        filled[...] = jnp.zeros_like(filled)

    for t in range(lg_ref.shape[0] // tm):
        lane_major, tok_major, cnt = _route_tile(lg_ref[t * tm:(t + 1) * tm, :], filled[...], ne)
        lane_ref[:, t * tm:(t + 1) * tm] = lane_major
        tok_ref[t * tm:(t + 1) * tm, :] = tok_major
        cnt_ref[t] = jnp.broadcast_to(cnt, (ne, LANES)).astype(I32)
        filled[...] = filled[...] + cnt


def _route(logits, ne):
    n = logits.shape[0]
    tm = MOE_TILE
    span = min(ROUTE_SPAN, n)
    return pl.pallas_call(
        functools.partial(_route_kernel, ne=ne),
        grid=(n // span,),
        in_specs=[pl.BlockSpec((span, LANES), lambda i: (i, 0))],
        out_specs=[pl.BlockSpec((8, span), lambda i: (0, i)), pl.BlockSpec((span, LANES), lambda i: (i, 0)),
                   pl.BlockSpec((span // tm, ne, LANES), lambda i: (i, 0, 0))],
        out_shape=[jax.ShapeDtypeStruct((8, n), F32), jax.ShapeDtypeStruct((n, LANES), F32),
                   jax.ShapeDtypeStruct((n // tm, ne, LANES), I32)],
        scratch_shapes=[pltpu.VMEM((ne, 1), F32)],
        compiler_params=_cparams("arbitrary"),
        name="moe_route",
    )(logits)


def _segment_copies(seg_ref, grouped_ref, base_ref, tile_buf, off_ref, sems, tile, slot, ne, *, to_grouped, wait):
    for e in range(ne):
        seg = seg_ref[tile * ne + e]
        far = base_ref[tile * ne + e]
        near = off_ref[tile * ne + e]
        for piece in SEG_PIECES:
            g = grouped_ref.at[pl.ds(pl.multiple_of(far, BF16_ROWS), piece)]
            t = tile_buf.at[slot, pl.ds(pl.multiple_of(near, BF16_ROWS), piece)]
            cp = pltpu.make_async_copy(t, g, sems.at[slot, e]) if to_grouped else \
                pltpu.make_async_copy(g, t, sems.at[slot, e])
            has = (seg & piece) != 0

            @pl.when(has)
            def _():
                cp.wait() if wait else cp.start()

            step = jnp.where(has, piece, 0)
            far = far + step
            near = near + step


def _tile_rows(ne):
    return TOP_K * MOE_TILE + ne * 2 * BF16_ROWS


def _tail_copies(end_ref, zeros, xb_ref, sems, ne):
    return [pltpu.make_async_copy(zeros, xb_ref.at[pl.ds(pl.multiple_of(end_ref[e], BF16_ROWS), MOE_BLK)], sems.at[e])
            for e in range(ne)]


def _dispatch_kernel(seg_ref, base_ref, off_ref, phase_ref, end_ref, rt_ref, h_ref, xb_ref, res, zeros, partial,
                     sems, tail_sems):
    i = pl.program_id(0)
    nt = pl.num_programs(0)
    ne = tail_sems.shape[0]
    tm = h_ref.shape[0]
    mrows = res.shape[1]
    slot = i % 2
    copies = functools.partial(_segment_copies, seg_ref, xb_ref, base_ref, res, off_ref, sems, ne=ne, to_grouped=True)

    @pl.when(i == 0)
    def _():
        zeros[...] = jnp.zeros_like(zeros)
        partial[...] = jnp.zeros_like(partial)
        for cp in _tail_copies(end_ref, zeros, xb_ref, tail_sems, ne):
            cp.start()

    row = lax.broadcasted_iota(I32, (mrows, tm), 0)
    pos = rt_ref[0:2, :].astype(I32)
    perm = jnp.where((row == pos[0:1, :]) | (row == pos[1:2, :]), 1.0, 0.0).astype(BF16)
    res[slot] = _dot(perm, h_ref[...]).astype(BF16)

    tile_row = lax.broadcasted_iota(I32, (BF16_ROWS, res.shape[2]), 0)
    for e in range(ne):
        @pl.when(seg_ref[i * ne + e] > 0)
        def _():
            first = pl.ds(pl.multiple_of(off_ref[i * ne + e], BF16_ROWS), BF16_ROWS)
            merged = jnp.where(tile_row < phase_ref[i * ne + e], partial[e].astype(F32), res[slot, first, :].astype(F32))
            res[slot, first, :] = merged.astype(BF16)
            last = pl.ds(pl.multiple_of(off_ref[i * ne + e] + seg_ref[i * ne + e] - BF16_ROWS, BF16_ROWS), BF16_ROWS)
            partial[e] = res[slot, last, :]

    @pl.when(i > 0)
    def _():
        copies(tile=i - 1, slot=1 - slot, wait=True)

    copies(tile=i, slot=slot, wait=False)

    @pl.when(i == nt - 1)
    def _():
        copies(tile=i, slot=slot, wait=True)
        for cp in _tail_copies(end_ref, zeros, xb_ref, tail_sems, ne):
            cp.wait()


def _dispatch(h, rt_lane, lay, ne):
    n, dm = h.shape
    tm = MOE_TILE
    return pl.pallas_call(
        _dispatch_kernel,
        grid_spec=pltpu.PrefetchScalarGridSpec(
            num_scalar_prefetch=5, grid=(n // tm,),
            in_specs=[pl.BlockSpec((8, tm), lambda i, *_: (0, i)), pl.BlockSpec((tm, dm), lambda i, *_: (i, 0))],
            out_specs=pl.BlockSpec(memory_space=pl.ANY),
            scratch_shapes=[pltpu.VMEM((2, _tile_rows(ne), dm), BF16), pltpu.VMEM((MOE_BLK, dm), BF16),
                            pltpu.VMEM((ne, BF16_ROWS, dm), BF16),
                            pltpu.SemaphoreType.DMA((2, ne)), pltpu.SemaphoreType.DMA((ne,))]),
        out_shape=jax.ShapeDtypeStruct((lay['rows_total'], dm), BF16),
        compiler_params=_cparams("arbitrary"),
        name="moe_dispatch",
    )(lay['seg'], lay['base'], lay['off'], lay['phase'], lay['end'], rt_lane, h)


def _combine_kernel(seg_ref, base_ref, off_ref, rt_ref, x_ref, gn_ref, yb_ref, o_ref, got, sems, *, ne):
    i = pl.program_id(0)
    nt = pl.num_programs(0)
    tm = x_ref.shape[0]
    mrows = got.shape[1]
    slot = i % 2
    copies = functools.partial(_segment_copies, seg_ref, yb_ref, base_ref, got, off_ref, sems, ne=ne, to_grouped=False)

    @pl.when(i == 0)
    def _():
        got[...] = jnp.zeros_like(got)
        copies(tile=i, slot=slot, wait=False)

    @pl.when(i + 1 < nt)
    def _():
        copies(tile=i + 1, slot=1 - slot, wait=False)

    copies(tile=i, slot=slot, wait=True)

    col = lax.broadcasted_iota(I32, (tm, mrows), 1)
    rt = rt_ref[...]
    rows = got[slot]
    sel = jnp.concatenate([jnp.where(col == rt[:, k:k + 1].astype(I32), 1.0, 0.0).astype(BF16) for k in range(TOP_K)],
                          axis=0)
    picked = _dot(sel, rows)
    y = sum(rt[:, TOP_K + k:TOP_K + k + 1] * picked[k * tm:(k + 1) * tm] for k in range(TOP_K))
    o_ref[...] = _rms(x_ref[...] + y, gn_ref[...])


def _combine(yb, rt_tok, x2, gn, lay, ne):
    n, dm = x2.shape
    tm = MOE_TILE
    mrows = _tile_rows(ne)
    tok = lambda w: pl.BlockSpec((tm, w), lambda i, *_: (i, 0))
    return pl.pallas_call(
        functools.partial(_combine_kernel, ne=ne),
        grid_spec=pltpu.PrefetchScalarGridSpec(
            num_scalar_prefetch=3, grid=(n // tm,),
            in_specs=[tok(LANES), tok(dm), pl.BlockSpec((1, dm), lambda i, *_: (0, 0)),
                      pl.BlockSpec(memory_space=pl.ANY)],
            out_specs=tok(dm),
            scratch_shapes=[pltpu.VMEM((2, mrows, dm), BF16), pltpu.SemaphoreType.DMA((2, ne))]),
        out_shape=jax.ShapeDtypeStruct((n, dm), F32),
        compiler_params=_cparams("arbitrary"),
        name="moe_combine",
    )(lay['seg'], lay['base'], lay['off'], rt_tok, x2, gn.reshape(1, dm), yb)


def _moe_layout(cnt, n):
    nt, ne = cnt.shape
    blk = MOE_BLK
    whole = lambda a: a // BF16_ROWS * BF16_ROWS
    cap = -(-(n + BF16_ROWS + blk) // blk) * blk
    region = jnp.arange(ne, dtype=I32) * cap
    filled = jnp.cumsum(cnt, axis=0) - cnt
    phase = filled - whole(filled)
    span = jnp.where(cnt > 0, whole(phase + cnt + BF16_ROWS - 1), 0)
    base = (region[None, :] + filled - phase).astype(I32)
    off = (jnp.cumsum(span, axis=1) - span).astype(I32)
    rows_e = jnp.sum(cnt, axis=0)
    nblk_e = (rows_e + blk - 1) // blk
    end = (region + whole(rows_e + BF16_ROWS - 1)).astype(I32)
    ends = jnp.cumsum(nblk_e)
    max_blocks = TOP_K * n // blk + ne
    bid = jnp.arange(max_blocks + 1, dtype=I32)
    bexp = jnp.minimum(jnp.sum((bid[:, None] >= ends[None, :]).astype(I32), axis=1), ne - 1)
    brow = bexp * (cap // blk) + bid - (ends - nblk_e)[bexp]
    nblk = ends[-1]
    last = jnp.maximum(nblk - 1, 0)
    used = bid < nblk
    first = bid == (ends - nblk_e)[bexp]
    hand_over = (bid == ends[bexp] - 1) & (bid < last)
    role = jnp.where(used, first.astype(I32) + 2 * hand_over.astype(I32), 4).astype(I32)
    bexp = jnp.where(used, bexp, bexp[last]).astype(I32)
    brow = jnp.where(used, brow, brow[last]).astype(I32)
    flat = lambda a: a.reshape(-1)
    return dict(seg=flat(span.astype(I32)), base=flat(base), off=flat(off), phase=flat(phase.astype(I32)), end=end,
                brow=brow, bexp=bexp, role=role, rows_total=ne * cap, max_blocks=max_blocks)


def _moe(x2, h, logits, gn, w1, w3, w2):
    n, dm = x2.shape
    ne = w1.shape[0]
    rt_lane, rt_tok, cnt = _route(logits, ne)
    lay = _moe_layout(cnt[:, :, 0], n)
    xb = _dispatch(h, rt_lane, lay, ne)
    yb = _expert_ffn(xb, w1, w3, w2, lay['brow'], lay['bexp'], lay['role'], lay['max_blocks'])
    return _combine(yb, rt_tok, x2, gn, lay, ne)


def kernel(x, norm_mix_g, w_in, attn_rel_bias, rg_conv_w, rg_conv_b, rg_wx, rg_bx, rg_wa, rg_ba, rg_lambda, s5_a_re, s5_a_im, s5_log_dt, s5_b_re, s5_b_im, s5_c_re, s5_c_im, s5_d, s5_w_glu, g_group, w_out, norm_ffn_g, ffn_w1, ffn_w3, ffn_w2, moe_router, moe_w1, moe_w3, moe_w2, final_norm_g):
    bsz, l, dm = x.shape
    depth = w_in.shape[0]
    assert depth == 2 and l % ATTN_TQ == 0 and ATTN_TQ == N_PREV * CHUNK, "dense layer, then the MoE layer"
    d_rg = rg_conv_w.shape[2]
    d_s5 = s5_w_glu.shape[1]
    d_attn = (w_in.shape[2] - 2 * d_rg - d_s5) // 3
    x2 = x.reshape(bsz * l, dm)
    attn_bias = jax.vmap(_attn_bias)(attn_rel_bias)
    rg_gates_w = jnp.concatenate([jax.vmap(_block_diag)(rg_wx), jax.vmap(_block_diag)(rg_wa)], axis=2).astype(BF16)
    rg_gates_b = jnp.concatenate([rg_bx, rg_ba], axis=1).astype(F32)[:, None, :]
    rg_decay = jax.nn.softplus(-rg_lambda.astype(F32))[:, None, :]
    s5_prm = jax.vmap(_s5_params)(s5_a_re, s5_a_im, s5_log_dt, s5_b_re, s5_b_im, s5_c_re, s5_c_im, s5_d)
    for layer in range(depth):
        q, k, vt, xr, gate, us = _in_proj(x2, norm_mix_g[layer], w_in, layer, d_attn, d_rg, d_s5)
        y_attn = _attention(q, k, vt, attn_bias, layer, bsz, l)
        y_rg = _rglru(xr, gate, rg_conv_w.astype(F32), rg_conv_b.astype(F32)[:, None, :], rg_gates_w, rg_gates_b,
                      rg_decay, layer, bsz, l)
        y_s5 = _s5(us, s5_prm, layer, bsz, l)
        mixed = (y_attn, y_rg, y_s5, x2, g_group[layer].astype(F32), s5_w_glu, w_out, layer,
                 norm_ffn_g[layer].astype(F32))
        if layer == 0:
            x2 = _out_proj(*mixed, ffn=(ffn_w1[0], ffn_w3[0], ffn_w2[0]))
        else:
            x2, h, logits = _out_proj(*mixed, router=moe_router[0])
            x2 = _moe(x2, h, logits, final_norm_g.astype(F32), moe_w1[0], moe_w3[0], moe_w2[0])
    return x2.reshape(bsz, l, dm)
```

```python
import functools
import math

import jax
import jax.numpy as jnp
import numpy as np
from jax import lax
from jax.experimental import pallas as pl
from jax.experimental.pallas import tpu as pltpu

F32 = jnp.float32
BF16 = jnp.bfloat16
I32 = jnp.int32

EPS = 1e-6
CHUNK = 64
N_PREV = 8
BAND = (N_PREV + 1) * CHUNK
REL_CLIP = 128
HEAD_DIM = 64
RG_C = 8.0
RG_CONV_WIDTH = 4
S5_GROUP_DIM = 16
S5_T = 16
TOP_K = 2
NEG_BIG = -1e30

LANES = 128
BF16_ROWS = 16
VMEM_LIMIT = 52 * 1024 * 1024
MIX_VMEM_LIMIT = 58 * 1024 * 1024

ROW_TILE = 512
PROJ_TILE = 1024
ATTN_TQ = 512
ATTN_SPAN = 2 * CHUNK
ATTN_WIN = N_PREV * CHUNK + ATTN_SPAN
RG_TL = 1024
RG_SEG = 8
MOE_TILE = 256
MOE_BLK = 512
ROUTE_SPAN = 2048
SEG_PIECES = tuple(BF16_ROWS << b for b in range((MOE_TILE // BF16_ROWS).bit_length() - 1, -1, -1))
FFN_CHUNKS = ((0, 512), (512, 512), (1024, 512), (1536, 512), (2048, 512), (2560, 256))


def _cparams(*sem):
    return pltpu.CompilerParams(dimension_semantics=sem, vmem_limit_bytes=VMEM_LIMIT)


def _const_spec(shape):
    nd = len(shape)
    return pl.BlockSpec(shape, lambda *_: (0,) * nd, pipeline_mode=pl.Buffered(1))


def _rms(xf, g):
    var = jnp.mean(xf * xf, axis=-1, keepdims=True)
    return xf * lax.rsqrt(var + EPS) * g


def _sigmoid(x):
    return 1.0 / (1.0 + jnp.exp(-x))


def _gelu(x):
    c = math.sqrt(2.0 / math.pi)
    return 0.5 * x * (1.0 + jnp.tanh(c * (x + 0.044715 * (x * x * x))))


def _dot(a, b):
    return jnp.dot(a, b, preferred_element_type=F32)


def _in_proj_kernel(x_ref, g_ref, w_ref, wvt_ref, q_ref, k_ref, vt_ref, xr_ref, gt_ref, us_ref, *, d_attn, d_rg):
    u = _rms(x_ref[...], g_ref[...]).astype(BF16)
    lo = 0
    for ref, width in ((q_ref, d_attn), (k_ref, d_attn), (None, d_attn),
                       (xr_ref, d_rg), (gt_ref, d_rg), (us_ref, w_ref.shape[1] - 3 * d_attn - 2 * d_rg)):
        if ref is not None:
            ref[...] = _dot(u, w_ref[:, lo:lo + width].astype(BF16)).astype(ref.dtype)
        lo += width
    vt_ref[...] = lax.dot_general(wvt_ref[...].astype(BF16), u, (((1,), (1,)), ((), ())),
                                  preferred_element_type=F32).astype(vt_ref.dtype)


def _layer_spec(stacked, layer):
    rest = stacked.shape[1:]
    return pl.BlockSpec((None,) + rest, lambda *_: (layer,) + (0,) * len(rest), pipeline_mode=pl.Buffered(1))


def _in_proj(x2, g, w_all, layer, d_attn, d_rg, d_s5):
    n, dm = x2.shape
    tm = min(PROJ_TILE, n)
    row = lambda w: pl.BlockSpec((tm, w), lambda i: (i, 0))
    wv_t = w_all[layer, :, 2 * d_attn:3 * d_attn].T
    return pl.pallas_call(
        functools.partial(_in_proj_kernel, d_attn=d_attn, d_rg=d_rg),
        grid=(n // tm,),
        in_specs=[row(dm), _const_spec((1, dm)), _layer_spec(w_all, layer), _const_spec(wv_t.shape)],
        out_specs=[row(d_attn), row(d_attn), pl.BlockSpec((d_attn, tm), lambda i: (0, i)),
                   row(d_rg), row(d_rg), row(d_s5)],
        out_shape=[jax.ShapeDtypeStruct((n, d_attn), BF16)] * 2 + [jax.ShapeDtypeStruct((d_attn, n), BF16)]
        + [jax.ShapeDtypeStruct((n, d_rg), F32)] * 2 + [jax.ShapeDtypeStruct((n, d_s5), F32)],
        compiler_params=_cparams("parallel"),
        name="in_proj",
    )(x2, g.reshape(1, dm), w_all, wv_t)


def _attn_kernel(q_ref, kp_ref, kc_ref, vp_ref, vc_ref, bias_ref, o_ref, kz, sc, *, n_pairs):
    qi = pl.program_id(1)
    tq = q_ref.shape[0]
    kz[0:tq, :] = kp_ref[...]
    kz[tq:2 * tq, :] = kc_ref[...]
    first = lax.broadcasted_iota(I32, (ATTN_SPAN, LANES), 1) < HEAD_DIM
    pairs = [slice(hp * LANES, (hp + 1) * LANES) for hp in range(n_pairs)]

    def score(j, masked):
        r0 = j * ATTN_SPAN
        for hp, cols in enumerate(pairs):
            q2 = q_ref[pl.ds(r0, ATTN_SPAN), cols] * jnp.asarray(HEAD_DIM ** -0.5, BF16)
            zero = jnp.zeros_like(q2)
            qq = jnp.concatenate([jnp.where(first, q2, zero), jnp.where(first, zero, q2)], axis=0)
            s = lax.dot_general(kz[pl.ds(r0, ATTN_WIN), cols], qq, (((1,), (1,)), ((), ())),
                                preferred_element_type=F32) + bias_ref[hp]
            if masked:
                key = lax.broadcasted_iota(I32, (ATTN_WIN, 2 * ATTN_SPAN), 0)
                s = jnp.where(key >= tq - r0, s, NEG_BIG)
            sc[j % 2, hp] = s

    def finish(j):
        r0 = j * ATTN_SPAN
        old = tq - r0
        probs = []
        for hp in range(n_pairs):
            s = sc[j % 2, hp]
            probs.append(jnp.exp((s - jnp.max(s, axis=0, keepdims=True)).astype(BF16)))
        for hp, (cols, p) in enumerate(zip(pairs, probs)):
            ones = lambda w: jnp.ones((BF16_ROWS, w), BF16)
            ov = (_dot(jnp.concatenate([vp_ref[cols, r0:tq], ones(old)], axis=0), p[0:old])
                  + _dot(jnp.concatenate([vc_ref[cols, 0:ATTN_WIN - old], ones(ATTN_WIN - old)], axis=0), p[old:]))
            o2 = (ov[0:LANES] / ov[LANES:LANES + 1]).T
            o_ref[pl.ds(r0, ATTN_SPAN), cols] = jnp.where(first, o2[0:ATTN_SPAN], o2[ATTN_SPAN:]).astype(o_ref.dtype)

    def run(masked):
        groups = tq // ATTN_SPAN
        score(0, masked)
        for j in range(groups):
            if j + 1 < groups:
                score(j + 1, masked)
            finish(j)

    pl.when(qi == 0)(functools.partial(run, True))
    pl.when(qi != 0)(functools.partial(run, False))


def _attention(q, k, vt, bias_t, layer, bsz, l):
    n, da = q.shape
    tq = ATTN_TQ
    nt = l // tq
    n_pairs = da // LANES
    cur = pl.BlockSpec((tq, da), lambda b, i: (b * nt + i, 0))
    prev = pl.BlockSpec((tq, da), lambda b, i: (b * nt + jnp.maximum(i - 1, 0), 0))
    cur_t = pl.BlockSpec((da, tq), lambda b, i: (0, b * nt + i))
    prev_t = pl.BlockSpec((da, tq), lambda b, i: (0, b * nt + jnp.maximum(i - 1, 0)))
    return pl.pallas_call(
        functools.partial(_attn_kernel, n_pairs=n_pairs),
        grid=(bsz, nt),
        in_specs=[cur, prev, cur, prev_t, cur_t, _layer_spec(bias_t, layer)],
        out_specs=cur,
        out_shape=jax.ShapeDtypeStruct((n, da), BF16),
        scratch_shapes=[pltpu.VMEM((2 * tq, da), BF16), pltpu.VMEM((2, n_pairs, ATTN_WIN, 2 * ATTN_SPAN), F32)],
        compiler_params=_cparams("parallel", "parallel"),
        name="chunk_attn",
    )(q, k, k, vt, vt, bias_t)


def _attn_bias(rel_bias):
    h = rel_bias.shape[0]
    tab = rel_bias.astype(F32)
    n_far = N_PREV * CHUNK - REL_CLIP + CHUNK
    lo = 2 * REL_CLIP - (BAND + CHUNK - 1 - n_far)
    ext = jnp.concatenate([jnp.broadcast_to(tab[:, 2 * REL_CLIP:], (h, n_far)), tab[:, lo:2 * REL_CLIP][:, ::-1]], axis=1)
    wide = BAND + CHUNK
    ring = jnp.concatenate([ext[:, CHUNK - 1:], jnp.zeros((h, 1), F32), ext[:, :CHUNK - 1]], axis=1)
    bias = jnp.broadcast_to(ring[:, None, :], (h, CHUNK, wide)).reshape(h, CHUNK * wide)
    bias = bias[:, :CHUNK * (wide - 1)].reshape(h, CHUNK, wide - 1)[..., :BAND]
    band_t = bias.transpose(0, 2, 1)
    groups = ATTN_SPAN // CHUNK
    per_chunk = [jnp.pad(band_t, ((0, 0), (c * CHUNK, (groups - 1 - c) * CHUNK), (0, 0)), constant_values=NEG_BIG)
                 for c in range(groups)]
    win = jnp.stack(per_chunk, axis=2)
    return win.reshape(h // 2, 2, ATTN_WIN, ATTN_SPAN).transpose(0, 2, 1, 3).reshape(h // 2, ATTN_WIN, 2 * ATTN_SPAN)


def _rglru_kernel(x_ref, gt_ref, cw_ref, cb_ref, w_ref, b_ref, sp_ref, o_ref, xpad, a_s, b_s, hcar):
    t = pl.program_id(1)
    tl, c = x_ref.shape
    seg = tl // RG_SEG
    front = 8

    @pl.when(t == 0)
    def _():
        xpad[0:front, :] = jnp.zeros((front, c), F32)
        hcar[...] = jnp.zeros_like(hcar)

    xpad[front:front + tl, :] = x_ref[...]
    xc = cb_ref[...] + sum(
        cw_ref[j:j + 1, :] * xpad[front - (RG_CONV_WIDTH - 1) + j:front - (RG_CONV_WIDTH - 1) + j + tl, :]
        for j in range(RG_CONV_WIDTH))
    xpad[0:front, :] = xpad[tl:tl + front, :]
    pre = _dot(xc.astype(BF16), w_ref[...]) + b_ref[...]
    gx = _sigmoid(pre[:, 0:c])
    ga = _sigmoid(pre[:, c:2 * c])
    log_a = -RG_C * ga * sp_ref[...]
    a = jnp.exp(log_a)
    mult = jnp.sqrt(-jnp.tanh(log_a) * (a * a + 1.0))
    b = mult * gx * xc
    nh = c // LANES
    for j in range(nh):
        a_s[j] = a[:, j * LANES:(j + 1) * LANES]
        b_s[j] = b[:, j * LANES:(j + 1) * LANES]

    def step(i, carry):
        rows = pl.ds(i, RG_SEG, stride=seg)
        out = []
        for j in range(nh):
            h, p = carry[j]
            av = a_s[j, rows, :]
            h = av * h + b_s[j, rows, :]
            p = av * p
            b_s[j, rows, :] = h
            a_s[j, rows, :] = p
            out.append((h, p))
        return tuple(out)

    init = tuple((jnp.zeros((RG_SEG, LANES), F32), jnp.ones((RG_SEG, LANES), F32)) for _ in range(nh))
    ends = lax.fori_loop(0, seg, step, init, unroll=8)
    for j in range(nh):
        cols = slice(j * LANES, (j + 1) * LANES)
        h_end, p_end = ends[j]
        carry = hcar[:, cols]
        for s in range(RG_SEG):
            rows = slice(s * seg, (s + 1) * seg)
            h = b_s[j, rows, :] + a_s[j, rows, :] * carry
            o_ref[rows, cols] = (h * _gelu(gt_ref[rows, cols])).astype(o_ref.dtype)
            carry = h_end[s:s + 1, :] + p_end[s:s + 1, :] * carry
        hcar[:, cols] = carry


def _rglru(xr, gate, conv_w, conv_b, w_bd, b_cat, sp, layer, bsz, l):
    n, c = xr.shape
    tl = min(RG_TL, l)
    nt = l // tl
    row = pl.BlockSpec((tl, c), lambda b, t: (b * nt + t, 0))
    return pl.pallas_call(
        _rglru_kernel,
        grid=(bsz, nt),
        in_specs=[row, row] + [_layer_spec(a, layer) for a in (conv_w, conv_b, w_bd, b_cat, sp)],
        out_specs=row,
        out_shape=jax.ShapeDtypeStruct((n, c), BF16),
        scratch_shapes=[pltpu.VMEM((tl + 8, c), F32), pltpu.VMEM((c // LANES, tl, LANES), F32),
                        pltpu.VMEM((c // LANES, tl, LANES), F32), pltpu.VMEM((1, c), F32)],
        compiler_params=_cparams("parallel", "arbitrary"),
        name="rglru",
    )(xr, gate, conv_w, conv_b, w_bd, b_cat, sp)


def _block_diag(w):
    nb, d, _ = w.shape
    eye = jnp.eye(nb, dtype=w.dtype)
    return (eye[:, None, :, None] * w[:, :, None, :]).reshape(nb * d, nb * d)


def _s5_kernel(u_ref, m_ref, wab_ref, wo_ref, c1_ref, c2a_ref, c2b_ref, d_ref, y_ref, ps, ug, xa_s, xb_s):
    t, gd = S5_T, S5_GROUP_DIM
    ng, nk, _ = ug.shape
    nh = ps.shape[0]
    per_half = LANES // gd
    rows8 = 8

    for h in range(nh):
        ps[h] = u_ref[:, h * LANES:(h + 1) * LANES]

    lane_block = lax.broadcasted_iota(I32, (rows8, LANES), 1) // gd

    def block_transpose(vs):
        d = per_half // 2
        while d:
            keep = (lane_block & d) == 0
            out = list(vs)
            for i in range(per_half):
                if i & d == 0:
                    out[i] = jnp.where(keep, vs[i], pltpu.roll(vs[i + d], d * gd, 1))
                    out[i + d] = jnp.where(keep, pltpu.roll(vs[i], LANES - d * gd, 1), vs[i + d])
            vs, d = out, d // 2
        return vs

    def to_groups(r, carry):
        base = pl.multiple_of(r * rows8 * t, rows8 * t)
        rows = pl.ds(pl.multiple_of(r * rows8, rows8), rows8)
        for h in range(nh):
            for j in range(t // per_half):
                steps = [ps[h, pl.ds(base + j * per_half + i, rows8, stride=t), :] for i in range(per_half)]
                for k, v in enumerate(block_transpose(steps)):
                    ug[h * per_half + k, rows, j * LANES:(j + 1) * LANES] = v
        return carry

    lax.fori_loop(0, nk // rows8, to_groups, 0, unroll=2)

    def project(g, carry):
        u = ug[g]
        ub = u.astype(BF16)
        rows = pl.ds(pl.multiple_of(g * nk, nk), nk)
        xab = _dot(ub, wab_ref[g])
        xa_s[rows, :] = xab[:, 0:LANES]
        xb_s[rows, :] = xab[:, LANES:2 * LANES]
        ug[g] = _dot(ub, m_ref[g]) + d_ref[g] * u
        return carry

    lax.fori_loop(0, ng, project, 0)

    c1, c2a, c2b = c1_ref[...], c2a_ref[...], c2b_ref[...]

    def step(k, carry):
        xa, xb = carry
        rows = pl.ds(k, ng, stride=nk)
        ia = xa_s[rows, :]
        ib = xb_s[rows, :]
        xa_s[rows, :] = xa
        return c1 * xa + c2a * xb + ia, c1 * xb + c2b * xa + ib

    z = jnp.zeros((ng, xa_s.shape[1]), F32)
    lax.fori_loop(0, nk, step, (z, z), unroll=4)

    def respond(g, carry):
        rows = pl.ds(pl.multiple_of(g * nk, nk), nk)
        ug[g] = _gelu(ug[g] + _dot(xa_s[rows, :].astype(BF16), wo_ref[g]))
        return carry

    lax.fori_loop(0, ng, respond, 0)

    def from_groups(r, carry):
        base = pl.multiple_of(r * rows8 * t, rows8 * t)
        rows = pl.ds(pl.multiple_of(r * rows8, rows8), rows8)
        for h in range(nh):
            for j in range(t // per_half):
                groups = [ug[h * per_half + k, rows, j * LANES:(j + 1) * LANES] for k in range(per_half)]
                for i, v in enumerate(block_transpose(groups)):
                    ps[h, pl.ds(base + j * per_half + i, rows8, stride=t), :] = v
        return carry

    lax.fori_loop(0, nk // rows8, from_groups, 0, unroll=2)
    for h in range(nh):
        y_ref[:, h * LANES:(h + 1) * LANES] = ps[h].astype(y_ref.dtype)


def _s5(us, prm, layer, bsz, l):
    n, w = us.shape
    ng = w // S5_GROUP_DIM
    nk = l // S5_T
    p2 = prm[2].shape[-2]
    row = pl.BlockSpec((l, w), lambda b: (b, 0))
    return pl.pallas_call(
        _s5_kernel,
        grid=(bsz,),
        in_specs=[row] + [_layer_spec(a, layer) for a in prm],
        out_specs=row,
        out_shape=jax.ShapeDtypeStruct((n, w), BF16),
        scratch_shapes=[pltpu.VMEM((w // LANES, l, LANES), F32), pltpu.VMEM((ng, nk, S5_T * S5_GROUP_DIM), F32),
                        pltpu.VMEM((ng * nk, p2), F32), pltpu.VMEM((ng * nk, p2), F32)],
        compiler_params=_cparams("parallel"),
        name="s5",
    )(us, *prm)


def _s5_params(a_re, a_im, log_dt, b_re, b_im, c_re, c_im, d):
    t = S5_T
    g, p = a_re.shape
    gd = S5_GROUP_DIM
    hi = lax.Precision.HIGHEST
    a = lax.complex(a_re.astype(F32), a_im.astype(F32))
    dt = jnp.exp(log_dt.astype(F32))[:, None]
    a_bar = jnp.exp(a * dt)
    bm = lax.complex(b_re.astype(F32), b_im.astype(F32))
    cm = lax.complex(c_re.astype(F32), c_im.astype(F32))
    b_bar = ((a_bar - 1.0) / a)[..., None] * bm
    steps = jnp.arange(t + 1, dtype=F32)
    pw = jnp.exp((a * dt)[:, None, :] * steps[None, :, None])

    cp = (cm[:, None, :, :] * pw[:, :, None, :]).transpose(0, 3, 1, 2)
    w = t * gd
    lag = cp[:, :, :t].reshape(g, p, w)
    k2 = jnp.einsum('gpc,gpx->gcx', jnp.concatenate([b_bar.real, -b_bar.imag], axis=1),
                    jnp.concatenate([lag.real, lag.imag], axis=1), precision=hi)
    shift = np.zeros((t, w, w), np.float32)
    for s in range(t):
        shift[s, np.arange(w - s * gd), np.arange(w - s * gd) + s * gd] = 1.0
    m = jnp.einsum('gcx,sxl->gscl', k2, shift, precision=hi).reshape(g, w, w)
    win = pw[:, :t][:, ::-1][:, :, :, None] * b_bar[:, None]
    win = win.transpose(0, 1, 3, 2).reshape(g, w, p)
    wab = jnp.concatenate([win.real, win.imag, win.imag, win.real], axis=-1)
    co = cp[:, :, 1:].reshape(g, p, w)
    wo = jnp.concatenate([co.real, -co.imag], axis=1)
    lt = pw[:, t]
    c1 = jnp.concatenate([lt.real, lt.real], axis=-1)
    c2a = jnp.concatenate([-lt.imag, lt.imag], axis=-1)
    c2b = jnp.concatenate([lt.imag, -lt.imag], axis=-1)
    dtile = jnp.tile(d.astype(F32).reshape(g, 1, gd), (1, t, 1)).reshape(g, 1, t * gd)
    return (m.astype(BF16), wab.astype(BF16), wo.astype(BF16), c1, c2a, c2b, dtile)


def _out_proj_kernel(ya_ref, yr_ref, ys_ref, x_ref, gg_ref, wglu_ref, wo_ref, gf_ref, *rest, dense, chunks):
    ys = ys_ref[...]
    ys = ys.astype(F32) * _sigmoid(_dot(ys.astype(BF16), wglu_ref[...].astype(BF16)))
    acc = x_ref[...]
    lo = 0
    for y in (ya_ref[...].astype(F32), yr_ref[...].astype(F32), ys):
        w = y.shape[1]
        acc = acc + _dot(_rms(y, gg_ref[:, lo:lo + w]).astype(BF16), wo_ref[lo:lo + w, :].astype(BF16))
        lo += w
    h = _rms(acc, gf_ref[...])
    hb = h.astype(BF16)
    if dense:
        w1_ref, w3_ref, w2_ref, xo_ref = rest
        xo_ref[...] = acc + _swiglu_chunks(hb, w1_ref, w3_ref, w2_ref, chunks)
        return
    rt_ref, xo_ref, h_ref, lg_ref = rest
    xo_ref[...] = acc
    h_ref[...] = hb
    both = _dot(hb, rt_ref[...])
    h_lo = (h - hb.astype(F32)).astype(BF16)
    lg_ref[...] = both[:, 0:LANES] + both[:, LANES:2 * LANES] + _dot(h_lo, rt_ref[:, 0:LANES])


def _out_proj(ya, yr, ys, x2, gg, wglu_all, wo_all, layer, gf, *, ffn=None, router=None):
    n, dm = x2.shape
    tm = min(ROW_TILE, n)
    row = lambda w: pl.BlockSpec((tm, w), lambda i: (i, 0))
    ins = [ya, yr, ys, x2, gg.reshape(1, -1), wglu_all, wo_all, gf.reshape(1, dm)]
    in_specs = [row(ya.shape[1]), row(yr.shape[1]), row(ys.shape[1]), row(dm), _const_spec((1, gg.shape[0])),
                _layer_spec(wglu_all, layer), _layer_spec(wo_all, layer), _const_spec((1, dm))]
    if ffn is not None:
        ins += list(ffn)
        in_specs += [_const_spec(w.shape) for w in ffn]
        out_specs = row(dm)
        out_shape = jax.ShapeDtypeStruct((n, dm), F32)
        chunks = _ffn_chunks(ffn[0].shape[1])
    else:
        ne = router.shape[1]
        r_hi = router.astype(BF16)
        r_lo = (router.astype(F32) - r_hi.astype(F32)).astype(BF16)
        pad = lambda a: jnp.pad(a, ((0, 0), (0, LANES - ne)))
        ins.append(jnp.concatenate([pad(r_hi), pad(r_lo)], axis=1))
        in_specs.append(_const_spec((dm, 2 * LANES)))
        out_specs = [row(dm), row(dm), row(LANES)]
        out_shape = [jax.ShapeDtypeStruct((n, dm), F32), jax.ShapeDtypeStruct((n, dm), BF16),
                     jax.ShapeDtypeStruct((n, LANES), F32)]
        chunks = None
    return pl.pallas_call(
        functools.partial(_out_proj_kernel, dense=ffn is not None, chunks=chunks),
        grid=(n // tm,),
        in_specs=in_specs, out_specs=out_specs, out_shape=out_shape,
        compiler_params=pltpu.CompilerParams(dimension_semantics=("parallel",), vmem_limit_bytes=MIX_VMEM_LIMIT),
        name="out_proj",
    )(*ins)


def _swiglu_chunks(h, w1, w3, w2, chunks, before_chunk=None, after_chunk=None):
    acc = None
    for j, (lo, width) in enumerate(chunks):
        if before_chunk is not None:
            before_chunk(j)
        a = _dot(h, w1[:, lo:lo + width].astype(BF16))
        b = _dot(h, w3[:, lo:lo + width].astype(BF16))
        t = (a * _sigmoid(a) * b).astype(BF16)
        y = _dot(t, w2[lo:lo + width, :].astype(BF16))
        acc = y if acc is None else acc + y
        if after_chunk is not None:
            after_chunk(j)
    return acc


def _ffn_chunks(f):
    if f == sum(w for _, w in FFN_CHUNKS):
        return FFN_CHUNKS
    return ((0, f),)


def _expert_weight_copies(hbm, vmem, sems, chunks, expert, j):
    h1, h3, h2 = hbm
    v1, v3, v2 = vmem
    lo, w = chunks[j]
    return [pltpu.make_async_copy(h1.at[expert, :, lo:lo + w], v1.at[:, lo:lo + w], sems.at[j, 0]),
            pltpu.make_async_copy(h3.at[expert, :, lo:lo + w], v3.at[:, lo:lo + w], sems.at[j, 1]),
            pltpu.make_async_copy(h2.at[expert, lo:lo + w, :], v2.at[lo:lo + w, :], sems.at[j, 2])]


def _expert_ffn_kernel(brow_ref, bexp_ref, role_ref, x_ref, w1_hbm, w3_hbm, w2_hbm, o_ref, w1_v, w3_v, w2_v, sems,
                       *, chunks):
    i = pl.program_id(0)
    copies = functools.partial(_expert_weight_copies, (w1_hbm, w3_hbm, w2_hbm), (w1_v, w3_v, w2_v), sems, chunks)

    @pl.when(i == 0)
    def _():
        for j in range(len(chunks)):
            for cp in copies(bexp_ref[0], j):
                cp.start()

    def block(first, last):
        def before(j):
            for cp in copies(bexp_ref[i], j):
                cp.wait()

        def after(j):
            for cp in copies(bexp_ref[i + 1], j):
                cp.start()

        o_ref[...] = _swiglu_chunks(x_ref[...], w1_v, w3_v, w2_v, chunks, before if first else None,
                                    after if last else None).astype(o_ref.dtype)

    for role in range(4):
        pl.when(role_ref[i] == role)(functools.partial(block, bool(role & 1), bool(role & 2)))


def _expert_ffn(xb, w1, w3, w2, blk_row, blk_exp, role, max_blocks):
    rows, dm = xb.shape
    f = w1.shape[2]
    chunks = _ffn_chunks(f)
    row = pl.BlockSpec((MOE_BLK, dm), lambda i, br, be, ro: (br[i], 0))
    hbm = pl.BlockSpec(memory_space=pl.ANY)
    return pl.pallas_call(
        functools.partial(_expert_ffn_kernel, chunks=chunks),
        grid_spec=pltpu.PrefetchScalarGridSpec(
            num_scalar_prefetch=3, grid=(max_blocks,),
            in_specs=[row, hbm, hbm, hbm], out_specs=row,
            scratch_shapes=[pltpu.VMEM(w1.shape[1:], w1.dtype), pltpu.VMEM(w3.shape[1:], w3.dtype),
                            pltpu.VMEM(w2.shape[1:], w2.dtype), pltpu.SemaphoreType.DMA((len(chunks), 3))]),
        out_shape=jax.ShapeDtypeStruct((rows, dm), BF16),
        compiler_params=_cparams("arbitrary"),
        name="expert_ffn",
    )(blk_row, blk_exp, role, xb, w1, w3, w2)


def _route_tile(logits_tok, filled, ne):
    tm = logits_tok.shape[0]
    logits = logits_tok.T[0:ne, :]
    eidx = lax.broadcasted_iota(I32, (ne, tm), 0)
    m1 = jnp.max(logits, axis=0, keepdims=True)
    i1 = jnp.min(jnp.where(logits == m1, eidx, ne), axis=0, keepdims=True)
    sel1 = eidx == i1
    rest = jnp.where(sel1, -jnp.inf, logits)
    m2 = jnp.max(rest, axis=0, keepdims=True)
    i2 = jnp.min(jnp.where(rest == m2, eidx, ne), axis=0, keepdims=True)
    sel2 = eidx == i2
    e2 = jnp.exp(m2 - m1)
    den = 1.0 + e2
    rf = jnp.where(sel1 | sel2, 1.0, 0.0)
    cnt = jnp.sum(rf, axis=1, keepdims=True)
    before = (lax.broadcasted_iota(I32, (tm, tm), 0) < lax.broadcasted_iota(I32, (tm, tm), 1))
    rank = _dot(rf.astype(BF16), jnp.where(before, 1.0, 0.0).astype(BF16))
    whole = lambda a: jnp.floor(a * (1.0 / BF16_ROWS)) * BF16_ROWS
    phase = filled - whole(filled)
    span = jnp.where(cnt > 0, whole(phase + cnt + (BF16_ROWS - 1)), 0.0)
    ecol = lax.broadcasted_iota(I32, (ne, 1), 0)
    off = jnp.zeros((ne, 1), F32)
    for j in range(ne - 1):
        off = off + jnp.where(ecol > j, span[j:j + 1, :], 0.0)
    place = off + phase + rank
    pos1 = jnp.sum(jnp.where(sel1, place, 0.0), axis=0, keepdims=True)
    pos2 = jnp.sum(jnp.where(sel2, place, 0.0), axis=0, keepdims=True)
    lane_major = jnp.concatenate([pos1, pos2, 1.0 / den, e2 / den, jnp.zeros((4, tm), F32)], axis=0)
    tok_major = jnp.concatenate([lane_major, jnp.zeros((LANES - 8, tm), F32)], axis=0).T
    return lane_major, tok_major, cnt


def _route_kernel(lg_ref, lane_ref, tok_ref, cnt_ref, filled, *, ne):
    tm = MOE_TILE

    @pl.when(pl.program_id(0) == 0)
    def _():
        filled[...] = jnp.zeros_like(filled)

    for t in range(lg_ref.shape[0] // tm):
        lane_major, tok_major, cnt = _route_tile(lg_ref[t * tm:(t + 1) * tm, :], filled[...], ne)
        lane_ref[:, t * tm:(t + 1) * tm] = lane_major
        tok_ref[t * tm:(t + 1) * tm, :] = tok_major
        cnt_ref[t] = jnp.broadcast_to(cnt, (ne, LANES)).astype(I32)
        filled[...] = filled[...] + cnt


def _route(logits, ne):
    n = logits.shape[0]
    tm = MOE_TILE
    span = min(ROUTE_SPAN, n)
    return pl.pallas_call(
        functools.partial(_route_kernel, ne=ne),
        grid=(n // span,),
        in_specs=[pl.BlockSpec((span, LANES), lambda i: (i, 0))],
        out_specs=[pl.BlockSpec((8, span), lambda i: (0, i)), pl.BlockSpec((span, LANES), lambda i: (i, 0)),
                   pl.BlockSpec((span // tm, ne, LANES), lambda i: (i, 0, 0))],
        out_shape=[jax.ShapeDtypeStruct((8, n), F32), jax.ShapeDtypeStruct((n, LANES), F32),
                   jax.ShapeDtypeStruct((n // tm, ne, LANES), I32)],
        scratch_shapes=[pltpu.VMEM((ne, 1), F32)],
        compiler_params=_cparams("arbitrary"),
        name="moe_route",
    )(logits)


def _segment_copies(seg_ref, grouped_ref, base_ref, tile_buf, off_ref, sems, tile, slot, ne, *, to_grouped, wait):
    for e in range(ne):
        seg = seg_ref[tile * ne + e]
        far = base_ref[tile * ne + e]
        near = off_ref[tile * ne + e]
        for piece in SEG_PIECES:
            g = grouped_ref.at[pl.ds(pl.multiple_of(far, BF16_ROWS), piece)]
            t = tile_buf.at[slot, pl.ds(pl.multiple_of(near, BF16_ROWS), piece)]
            cp = pltpu.make_async_copy(t, g, sems.at[slot, e]) if to_grouped else \
                pltpu.make_async_copy(g, t, sems.at[slot, e])
            has = (seg & piece) != 0

            @pl.when(has)
            def _():
                cp.wait() if wait else cp.start()

            step = jnp.where(has, piece, 0)
            far = far + step
            near = near + step


def _tile_rows(ne):
    return TOP_K * MOE_TILE + ne * 2 * BF16_ROWS


def _tail_copies(end_ref, zeros, xb_ref, sems, ne):
    return [pltpu.make_async_copy(zeros, xb_ref.at[pl.ds(pl.multiple_of(end_ref[e], BF16_ROWS), MOE_BLK)], sems.at[e])
            for e in range(ne)]


def _dispatch_kernel(seg_ref, base_ref, off_ref, phase_ref, end_ref, rt_ref, h_ref, xb_ref, res, zeros, partial,
                     sems, tail_sems):
    i = pl.program_id(0)
    nt = pl.num_programs(0)
    ne = tail_sems.shape[0]
    tm = h_ref.shape[0]
    mrows = res.shape[1]
    slot = i % 2
    copies = functools.partial(_segment_copies, seg_ref, xb_ref, base_ref, res, off_ref, sems, ne=ne, to_grouped=True)

    @pl.when(i == 0)
    def _():
        zeros[...] = jnp.zeros_like(zeros)
        partial[...] = jnp.zeros_like(partial)
        for cp in _tail_copies(end_ref, zeros, xb_ref, tail_sems, ne):
            cp.start()

    row = lax.broadcasted_iota(I32, (mrows, tm), 0)
    pos = rt_ref[0:2, :].astype(I32)
    perm = jnp.where((row == pos[0:1, :]) | (row == pos[1:2, :]), 1.0, 0.0).astype(BF16)
    res[slot] = _dot(perm, h_ref[...]).astype(BF16)

    tile_row = lax.broadcasted_iota(I32, (BF16_ROWS, res.shape[2]), 0)
    for e in range(ne):
        @pl.when(seg_ref[i * ne + e] > 0)
        def _():
            first = pl.ds(pl.multiple_of(off_ref[i * ne + e], BF16_ROWS), BF16_ROWS)
            merged = jnp.where(tile_row < phase_ref[i * ne + e], partial[e].astype(F32), res[slot, first, :].astype(F32))
            res[slot, first, :] = merged.astype(BF16)
            last = pl.ds(pl.multiple_of(off_ref[i * ne + e] + seg_ref[i * ne + e] - BF16_ROWS, BF16_ROWS), BF16_ROWS)
            partial[e] = res[slot, last, :]

    @pl.when(i > 0)
    def _():
        copies(tile=i - 1, slot=1 - slot, wait=True)

    copies(tile=i, slot=slot, wait=False)

    @pl.when(i == nt - 1)
    def _():
        copies(tile=i, slot=slot, wait=True)
        for cp in _tail_copies(end_ref, zeros, xb_ref, tail_sems, ne):
            cp.wait()


def _dispatch(h, rt_lane, lay, ne):
    n, dm = h.shape
    tm = MOE_TILE
    return pl.pallas_call(
        _dispatch_kernel,
        grid_spec=pltpu.PrefetchScalarGridSpec(
            num_scalar_prefetch=5, grid=(n // tm,),
            in_specs=[pl.BlockSpec((8, tm), lambda i, *_: (0, i)), pl.BlockSpec((tm, dm), lambda i, *_: (i, 0))],
            out_specs=pl.BlockSpec(memory_space=pl.ANY),
            scratch_shapes=[pltpu.VMEM((2, _tile_rows(ne), dm), BF16), pltpu.VMEM((MOE_BLK, dm), BF16),
                            pltpu.VMEM((ne, BF16_ROWS, dm), BF16),
                            pltpu.SemaphoreType.DMA((2, ne)), pltpu.SemaphoreType.DMA((ne,))]),
        out_shape=jax.ShapeDtypeStruct((lay['rows_total'], dm), BF16),
        compiler_params=_cparams("arbitrary"),
        name="moe_dispatch",
    )(lay['seg'], lay['base'], lay['off'], lay['phase'], lay['end'], rt_lane, h)


def _combine_kernel(seg_ref, base_ref, off_ref, rt_ref, x_ref, gn_ref, yb_ref, o_ref, got, sems, *, ne):
    i = pl.program_id(0)
    nt = pl.num_programs(0)
    tm = x_ref.shape[0]
    mrows = got.shape[1]
    slot = i % 2
    copies = functools.partial(_segment_copies, seg_ref, yb_ref, base_ref, got, off_ref, sems, ne=ne, to_grouped=False)

    @pl.when(i == 0)
    def _():
        got[...] = jnp.zeros_like(got)
        copies(tile=i, slot=slot, wait=False)

    @pl.when(i + 1 < nt)
    def _():
        copies(tile=i + 1, slot=1 - slot, wait=False)

    copies(tile=i, slot=slot, wait=True)

    col = lax.broadcasted_iota(I32, (tm, mrows), 1)
    rt = rt_ref[...]
    rows = got[slot]
    sel = jnp.concatenate([jnp.where(col == rt[:, k:k + 1].astype(I32), 1.0, 0.0).astype(BF16) for k in range(TOP_K)],
                          axis=0)
    picked = _dot(sel, rows)
    y = sum(rt[:, TOP_K + k:TOP_K + k + 1] * picked[k * tm:(k + 1) * tm] for k in range(TOP_K))
    o_ref[...] = _rms(x_ref[...] + y, gn_ref[...])


def _combine(yb, rt_tok, x2, gn, lay, ne):
    n, dm = x2.shape
    tm = MOE_TILE
    mrows = _tile_rows(ne)
    tok = lambda w: pl.BlockSpec((tm, w), lambda i, *_: (i, 0))
    return pl.pallas_call(
        functools.partial(_combine_kernel, ne=ne),
        grid_spec=pltpu.PrefetchScalarGridSpec(
            num_scalar_prefetch=3, grid=(n // tm,),
            in_specs=[tok(LANES), tok(dm), pl.BlockSpec((1, dm), lambda i, *_: (0, 0)),
                      pl.BlockSpec(memory_space=pl.ANY)],
            out_specs=tok(dm),
            scratch_shapes=[pltpu.VMEM((2, mrows, dm), BF16), pltpu.SemaphoreType.DMA((2, ne))]),
        out_shape=jax.ShapeDtypeStruct((n, dm), F32),
        compiler_params=_cparams("arbitrary"),
        name="moe_combine",
    )(lay['seg'], lay['base'], lay['off'], rt_tok, x2, gn.reshape(1, dm), yb)


def _moe_layout(cnt, n):
    nt, ne = cnt.shape
    blk = MOE_BLK
    whole = lambda a: a // BF16_ROWS * BF16_ROWS
    cap = -(-(n + BF16_ROWS + blk) // blk) * blk
    region = jnp.arange(ne, dtype=I32) * cap
    filled = jnp.cumsum(cnt, axis=0) - cnt
    phase = filled - whole(filled)
    span = jnp.where(cnt > 0, whole(phase + cnt + BF16_ROWS - 1), 0)
    base = (region[None, :] + filled - phase).astype(I32)
    off = (jnp.cumsum(span, axis=1) - span).astype(I32)
    rows_e = jnp.sum(cnt, axis=0)
    nblk_e = (rows_e + blk - 1) // blk
    end = (region + whole(rows_e + BF16_ROWS - 1)).astype(I32)
    ends = jnp.cumsum(nblk_e)
    max_blocks = TOP_K * n // blk + ne
    bid = jnp.arange(max_blocks + 1, dtype=I32)
    bexp = jnp.minimum(jnp.sum((bid[:, None] >= ends[None, :]).astype(I32), axis=1), ne - 1)
    brow = bexp * (cap // blk) + bid - (ends - nblk_e)[bexp]
    nblk = ends[-1]
    last = jnp.maximum(nblk - 1, 0)
    used = bid < nblk
    first = bid == (ends - nblk_e)[bexp]
    hand_over = (bid == ends[bexp] - 1) & (bid < last)
    role = jnp.where(used, first.astype(I32) + 2 * hand_over.astype(I32), 4).astype(I32)
    bexp = jnp.where(used, bexp, bexp[last]).astype(I32)
    brow = jnp.where(used, brow, brow[last]).astype(I32)
    flat = lambda a: a.reshape(-1)
    return dict(seg=flat(span.astype(I32)), base=flat(base), off=flat(off), phase=flat(phase.astype(I32)), end=end,
                brow=brow, bexp=bexp, role=role, rows_total=ne * cap, max_blocks=max_blocks)


def _moe(x2, h, logits, gn, w1, w3, w2):
    n, dm = x2.shape
    ne = w1.shape[0]
    rt_lane, rt_tok, cnt = _route(logits, ne)
    lay = _moe_layout(cnt[:, :, 0], n)
    xb = _dispatch(h, rt_lane, lay, ne)
    yb = _expert_ffn(xb, w1, w3, w2, lay['brow'], lay['bexp'], lay['role'], lay['max_blocks'])
    return _combine(yb, rt_tok, x2, gn, lay, ne)


def kernel(x, norm_mix_g, w_in, attn_rel_bias, rg_conv_w, rg_conv_b, rg_wx, rg_bx, rg_wa, rg_ba, rg_lambda, s5_a_re, s5_a_im, s5_log_dt, s5_b_re, s5_b_im, s5_c_re, s5_c_im, s5_d, s5_w_glu, g_group, w_out, norm_ffn_g, ffn_w1, ffn_w3, ffn_w2, moe_router, moe_w1, moe_w3, moe_w2, final_norm_g):
    bsz, l, dm = x.shape
    depth = w_in.shape[0]
    assert depth == 2 and l % ATTN_TQ == 0 and ATTN_TQ == N_PREV * CHUNK, "dense layer, then the MoE layer"
    d_rg = rg_conv_w.shape[2]
    d_s5 = s5_w_glu.shape[1]
    d_attn = (w_in.shape[2] - 2 * d_rg - d_s5) // 3
    x2 = x.reshape(bsz * l, dm)
    attn_bias = jax.vmap(_attn_bias)(attn_rel_bias)
    rg_gates_w = jnp.concatenate([jax.vmap(_block_diag)(rg_wx), jax.vmap(_block_diag)(rg_wa)], axis=2).astype(BF16)
    rg_gates_b = jnp.concatenate([rg_bx, rg_ba], axis=1).astype(F32)[:, None, :]
    rg_decay = jax.nn.softplus(-rg_lambda.astype(F32))[:, None, :]
    s5_prm = jax.vmap(_s5_params)(s5_a_re, s5_a_im, s5_log_dt, s5_b_re, s5_b_im, s5_c_re, s5_c_im, s5_d)
    for layer in range(depth):
        q, k, vt, xr, gate, us = _in_proj(x2, norm_mix_g[layer], w_in, layer, d_attn, d_rg, d_s5)
        y_attn = _attention(q, k, vt, attn_bias, layer, bsz, l)
        y_rg = _rglru(xr, gate, rg_conv_w.astype(F32), rg_conv_b.astype(F32)[:, None, :], rg_gates_w, rg_gates_b,
                      rg_decay, layer, bsz, l)
        y_s5 = _s5(us, s5_prm, layer, bsz, l)
        mixed = (y_attn, y_rg, y_s5, x2, g_group[layer].astype(F32), s5_w_glu, w_out, layer,
                 norm_ffn_g[layer].astype(F32))
        if layer == 0:
            x2 = _out_proj(*mixed, ffn=(ffn_w1[0], ffn_w3[0], ffn_w2[0]))
        else:
            x2, h, logits = _out_proj(*mixed, router=moe_router[0])
            x2 = _moe(x2, h, logits, final_norm_g.astype(F32), moe_w1[0], moe_w3[0], moe_w2[0])
    return x2.reshape(bsz, l, dm)
```

```python
import functools
import math

import jax
import jax.numpy as jnp
import numpy as np
from jax import lax
from jax.experimental import pallas as pl
from jax.experimental.pallas import tpu as pltpu

F32 = jnp.float32
BF16 = jnp.bfloat16
I32 = jnp.int32

EPS = 1e-6
CHUNK = 64
N_PREV = 8
BAND = (N_PREV + 1) * CHUNK
REL_CLIP = 128
HEAD_DIM = 64
RG_C = 8.0
RG_CONV_WIDTH = 4
S5_GROUP_DIM = 16
S5_T = 16
TOP_K = 2
NEG_BIG = -1e30

LANES = 128
SUBLANES = 8
BF16_ROWS = 16
VMEM_LIMIT = 52 * 1024 * 1024
MIX_VMEM_LIMIT = 58 * 1024 * 1024

ROW_TILE = 512
PROJ_TILE = 1024
ATTN_TQ = 512
ATTN_SPAN = 2 * CHUNK
ATTN_WIN = N_PREV * CHUNK + ATTN_SPAN
RG_TL = 1024
MOE_TILE = 256
MOE_BLK = 512
ROUTE_SPAN = 2048
SEG_PIECES = tuple(BF16_ROWS << b for b in range((MOE_TILE // BF16_ROWS).bit_length() - 1, -1, -1))
FFN_CHUNKS = ((0, 512), (512, 512), (1024, 512), (1536, 512), (2048, 512), (2560, 256))


def _cparams(*sem):
    return pltpu.CompilerParams(dimension_semantics=sem, vmem_limit_bytes=VMEM_LIMIT)


def _const_spec(shape):
    nd = len(shape)
    return pl.BlockSpec(shape, lambda *_: (0,) * nd, pipeline_mode=pl.Buffered(1))


def _rms(xf, g):
    var = jnp.mean(xf * xf, axis=-1, keepdims=True)
    return xf * lax.rsqrt(var + EPS) * g


def _sigmoid(x):
    return 1.0 / (1.0 + jnp.exp(-x))


def _gelu(x):
    c = math.sqrt(2.0 / math.pi)
    return 0.5 * x * (1.0 + jnp.tanh(c * (x + 0.044715 * (x * x * x))))


def _dot(a, b):
    return jnp.dot(a, b, preferred_element_type=F32)


def _in_proj_kernel(x_ref, g_ref, w_ref, wvt_ref, q_ref, k_ref, vt_ref, xr_ref, gt_ref, us_ref, *, d_attn, d_rg):
    u = _rms(x_ref[...], g_ref[...]).astype(BF16)
    lo = 0
    for ref, width in ((q_ref, d_attn), (k_ref, d_attn), (None, d_attn),
                       (xr_ref, d_rg), (gt_ref, d_rg), (us_ref, w_ref.shape[1] - 3 * d_attn - 2 * d_rg)):
        if ref is not None:
            ref[...] = _dot(u, w_ref[:, lo:lo + width].astype(BF16)).astype(ref.dtype)
        lo += width
    vt_ref[...] = lax.dot_general(wvt_ref[...].astype(BF16), u, (((1,), (1,)), ((), ())),
                                  preferred_element_type=F32).astype(vt_ref.dtype)


def _layer_spec(stacked, layer):
    rest = stacked.shape[1:]
    return pl.BlockSpec((None,) + rest, lambda *_: (layer,) + (0,) * len(rest), pipeline_mode=pl.Buffered(1))


def _in_proj(x2, g, w_all, layer, d_attn, d_rg, d_s5):
    n, dm = x2.shape
    tm = min(PROJ_TILE, n)
    row = lambda w: pl.BlockSpec((tm, w), lambda i: (i, 0))
    wv_t = w_all[layer, :, 2 * d_attn:3 * d_attn].T
    return pl.pallas_call(
        functools.partial(_in_proj_kernel, d_attn=d_attn, d_rg=d_rg),
        grid=(n // tm,),
        in_specs=[row(dm), _const_spec((1, dm)), _layer_spec(w_all, layer), _const_spec(wv_t.shape)],
        out_specs=[row(d_attn), row(d_attn), pl.BlockSpec((d_attn, tm), lambda i: (0, i)),
                   row(d_rg), row(d_rg), row(d_s5)],
        out_shape=[jax.ShapeDtypeStruct((n, d_attn), BF16)] * 2 + [jax.ShapeDtypeStruct((d_attn, n), BF16)]
        + [jax.ShapeDtypeStruct((n, d_rg), F32)] * 2 + [jax.ShapeDtypeStruct((n, d_s5), F32)],
        compiler_params=_cparams("parallel"),
        name="in_proj",
    )(x2, g.reshape(1, dm), w_all, wv_t)


def _attn_kernel(q_ref, kp_ref, kc_ref, vp_ref, vc_ref, bias_ref, o_ref, kz, sc, *, n_pairs):
    qi = pl.program_id(1)
    tq = q_ref.shape[0]
    kz[0:tq, :] = kp_ref[...]
    kz[tq:2 * tq, :] = kc_ref[...]
    first = lax.broadcasted_iota(I32, (ATTN_SPAN, LANES), 1) < HEAD_DIM
    pairs = [slice(hp * LANES, (hp + 1) * LANES) for hp in range(n_pairs)]

    def score(j, masked):
        r0 = j * ATTN_SPAN
        for hp, cols in enumerate(pairs):
            q2 = q_ref[pl.ds(r0, ATTN_SPAN), cols] * jnp.asarray(HEAD_DIM ** -0.5, BF16)
            zero = jnp.zeros_like(q2)
            qq = jnp.concatenate([jnp.where(first, q2, zero), jnp.where(first, zero, q2)], axis=0)
            s = lax.dot_general(kz[pl.ds(r0, ATTN_WIN), cols], qq, (((1,), (1,)), ((), ())),
                                preferred_element_type=F32) + bias_ref[hp]
            if masked:
                key = lax.broadcasted_iota(I32, (ATTN_WIN, 2 * ATTN_SPAN), 0)
                s = jnp.where(key >= tq - r0, s, NEG_BIG)
            sc[j % 2, hp] = s

    def finish(j):
        r0 = j * ATTN_SPAN
        old = tq - r0
        probs = []
        for hp in range(n_pairs):
            s = sc[j % 2, hp]
            probs.append(jnp.exp((s - jnp.max(s, axis=0, keepdims=True)).astype(BF16)))
        for hp, (cols, p) in enumerate(zip(pairs, probs)):
            ones = lambda w: jnp.ones((BF16_ROWS, w), BF16)
            ov = (_dot(jnp.concatenate([vp_ref[cols, r0:tq], ones(old)], axis=0), p[0:old])
                  + _dot(jnp.concatenate([vc_ref[cols, 0:ATTN_WIN - old], ones(ATTN_WIN - old)], axis=0), p[old:]))
            o2 = (ov[0:LANES] / ov[LANES:LANES + 1]).T
            o_ref[pl.ds(r0, ATTN_SPAN), cols] = jnp.where(first, o2[0:ATTN_SPAN], o2[ATTN_SPAN:]).astype(o_ref.dtype)

    def run(masked):
        groups = tq // ATTN_SPAN
        score(0, masked)
        for j in range(groups):
            if j + 1 < groups:
                score(j + 1, masked)
            finish(j)

    pl.when(qi == 0)(functools.partial(run, True))
    pl.when(qi != 0)(functools.partial(run, False))


def _attention(q, k, vt, bias_t, layer, bsz, l):
    n, da = q.shape
    tq = ATTN_TQ
    nt = l // tq
    n_pairs = da // LANES
    cur = pl.BlockSpec((tq, da), lambda b, i: (b * nt + i, 0))
    prev = pl.BlockSpec((tq, da), lambda b, i: (b * nt + jnp.maximum(i - 1, 0), 0))
    cur_t = pl.BlockSpec((da, tq), lambda b, i: (0, b * nt + i))
    prev_t = pl.BlockSpec((da, tq), lambda b, i: (0, b * nt + jnp.maximum(i - 1, 0)))
    return pl.pallas_call(
        functools.partial(_attn_kernel, n_pairs=n_pairs),
        grid=(bsz, nt),
        in_specs=[cur, prev, cur, prev_t, cur_t, _layer_spec(bias_t, layer)],
        out_specs=cur,
        out_shape=jax.ShapeDtypeStruct((n, da), BF16),
        scratch_shapes=[pltpu.VMEM((2 * tq, da), BF16), pltpu.VMEM((2, n_pairs, ATTN_WIN, 2 * ATTN_SPAN), F32)],
        compiler_params=_cparams("parallel", "parallel"),
        name="chunk_attn",
    )(q, k, k, vt, vt, bias_t)


def _attn_bias(rel_bias):
    h = rel_bias.shape[0]
    tab = rel_bias.astype(F32)
    n_far = N_PREV * CHUNK - REL_CLIP + CHUNK
    lo = 2 * REL_CLIP - (BAND + CHUNK - 1 - n_far)
    ext = jnp.concatenate([jnp.broadcast_to(tab[:, 2 * REL_CLIP:], (h, n_far)), tab[:, lo:2 * REL_CLIP][:, ::-1]], axis=1)
    wide = BAND + CHUNK
    ring = jnp.concatenate([ext[:, CHUNK - 1:], jnp.zeros((h, 1), F32), ext[:, :CHUNK - 1]], axis=1)
    bias = jnp.broadcast_to(ring[:, None, :], (h, CHUNK, wide)).reshape(h, CHUNK * wide)
    bias = bias[:, :CHUNK * (wide - 1)].reshape(h, CHUNK, wide - 1)[..., :BAND]
    band_t = bias.transpose(0, 2, 1)
    groups = ATTN_SPAN // CHUNK
    per_chunk = [jnp.pad(band_t, ((0, 0), (c * CHUNK, (groups - 1 - c) * CHUNK), (0, 0)), constant_values=NEG_BIG)
                 for c in range(groups)]
    win = jnp.stack(per_chunk, axis=2)
    return win.reshape(h // 2, 2, ATTN_WIN, ATTN_SPAN).transpose(0, 2, 1, 3).reshape(h // 2, ATTN_WIN, 2 * ATTN_SPAN)


def _rglru_kernel(x_ref, gt_ref, cw_ref, cb_ref, w_ref, b_ref, sp_ref, o_ref, xpad, a_s, b_s, hcar):
    t = pl.program_id(1)
    tl, c = x_ref.shape
    front = SUBLANES

    @pl.when(t == 0)
    def _():
        xpad[0:front, :] = jnp.zeros((front, c), F32)
        hcar[...] = jnp.zeros_like(hcar)

    xpad[front:front + tl, :] = x_ref[...]
    xc = cb_ref[...] + sum(
        cw_ref[j:j + 1, :] * xpad[front - (RG_CONV_WIDTH - 1) + j:front - (RG_CONV_WIDTH - 1) + j + tl, :]
        for j in range(RG_CONV_WIDTH))
    xpad[0:front, :] = xpad[tl:tl + front, :]
    pre = _dot(xc.astype(BF16), w_ref[...]) + b_ref[...]
    gx = _sigmoid(pre[:, 0:c])
    ga = _sigmoid(pre[:, c:2 * c])
    log_a = -RG_C * ga * sp_ref[...]
    a = jnp.exp(log_a)
    mult = jnp.sqrt(-jnp.tanh(log_a) * (a * a + 1.0))
    b = mult * gx * xc
    nh = c // LANES
    for j in range(nh):
        a_s[j] = a[:, j * LANES:(j + 1) * LANES]
        b_s[j] = b[:, j * LANES:(j + 1) * LANES]

    step_in_vreg = lax.broadcasted_iota(I32, (SUBLANES, LANES), 0)

    def vreg_scan(r, h_in):
        rows = pl.ds(pl.multiple_of(r * SUBLANES, SUBLANES), SUBLANES)
        out = []
        for j in range(nh):
            av, bv = a_s[j, rows, :], b_s[j, rows, :]
            for d in (1, 2, 4):
                seen = step_in_vreg >= d
                a_prev = jnp.where(seen, pltpu.roll(av, d, 0), 1.0)
                b_prev = jnp.where(seen, pltpu.roll(bv, d, 0), 0.0)
                bv = bv + av * b_prev
                av = av * a_prev
            h = bv + av * h_in[j]
            b_s[j, rows, :] = h
            out.append(h[SUBLANES - 1:SUBLANES, :])
        return tuple(out)

    h_last = lax.fori_loop(0, tl // SUBLANES, vreg_scan,
                           tuple(hcar[:, j * LANES:(j + 1) * LANES] for j in range(nh)), unroll=4)
    for j in range(nh):
        cols = slice(j * LANES, (j + 1) * LANES)
        hcar[:, cols] = h_last[j]
        o_ref[:, cols] = (b_s[j] * _gelu(gt_ref[:, cols])).astype(o_ref.dtype)


def _rglru(xr, gate, conv_w, conv_b, w_bd, b_cat, sp, layer, bsz, l):
    n, c = xr.shape
    tl = min(RG_TL, l)
    nt = l // tl
    row = pl.BlockSpec((tl, c), lambda b, t: (b * nt + t, 0))
    return pl.pallas_call(
        _rglru_kernel,
        grid=(bsz, nt),
        in_specs=[row, row] + [_layer_spec(a, layer) for a in (conv_w, conv_b, w_bd, b_cat, sp)],
        out_specs=row,
        out_shape=jax.ShapeDtypeStruct((n, c), BF16),
        scratch_shapes=[pltpu.VMEM((tl + 8, c), F32), pltpu.VMEM((c // LANES, tl, LANES), F32),
                        pltpu.VMEM((c // LANES, tl, LANES), F32), pltpu.VMEM((1, c), F32)],
        compiler_params=_cparams("parallel", "arbitrary"),
        name="rglru",
    )(xr, gate, conv_w, conv_b, w_bd, b_cat, sp)


def _block_diag(w):
    nb, d, _ = w.shape
    eye = jnp.eye(nb, dtype=w.dtype)
    return (eye[:, None, :, None] * w[:, :, None, :]).reshape(nb * d, nb * d)


def _s5_kernel(u_ref, m_ref, wab_ref, wo_ref, c1_ref, c2a_ref, c2b_ref, d_ref, y_ref, ps, ug, xa_s, xb_s):
    t, gd = S5_T, S5_GROUP_DIM
    ng, nk, _ = ug.shape
    nh = ps.shape[0]
    per_half = LANES // gd
    rows8 = 8

    for h in range(nh):
        ps[h] = u_ref[:, h * LANES:(h + 1) * LANES]

    lane_block = lax.broadcasted_iota(I32, (rows8, LANES), 1) // gd

    def block_transpose(vs):
        d = per_half // 2
        while d:
            keep = (lane_block & d) == 0
            out = list(vs)
            for i in range(per_half):
                if i & d == 0:
                    out[i] = jnp.where(keep, vs[i], pltpu.roll(vs[i + d], d * gd, 1))
                    out[i + d] = jnp.where(keep, pltpu.roll(vs[i], LANES - d * gd, 1), vs[i + d])
            vs, d = out, d // 2
        return vs

    def to_groups(r, carry):
        base = pl.multiple_of(r * rows8 * t, rows8 * t)
        rows = pl.ds(pl.multiple_of(r * rows8, rows8), rows8)
        for h in range(nh):
            for j in range(t // per_half):
                steps = [ps[h, pl.ds(base + j * per_half + i, rows8, stride=t), :] for i in range(per_half)]
                for k, v in enumerate(block_transpose(steps)):
                    ug[h * per_half + k, rows, j * LANES:(j + 1) * LANES] = v
        return carry

    lax.fori_loop(0, nk // rows8, to_groups, 0, unroll=2)

    def project(g, carry):
        u = ug[g]
        ub = u.astype(BF16)
        rows = pl.ds(pl.multiple_of(g * nk, nk), nk)
        xab = _dot(ub, wab_ref[g])
        xa_s[rows, :] = xab[:, 0:LANES]
        xb_s[rows, :] = xab[:, LANES:2 * LANES]
        ug[g] = _dot(ub, m_ref[g]) + d_ref[g] * u
        return carry

    lax.fori_loop(0, ng, project, 0)

    c1, c2a, c2b = c1_ref[...], c2a_ref[...], c2b_ref[...]

    def step(k, carry):
        xa, xb = carry
        rows = pl.ds(k, ng, stride=nk)
        ia = xa_s[rows, :]
        ib = xb_s[rows, :]
        xa_s[rows, :] = xa
        return c1 * xa + c2a * xb + ia, c1 * xb + c2b * xa + ib

    z = jnp.zeros((ng, xa_s.shape[1]), F32)
    lax.fori_loop(0, nk, step, (z, z), unroll=4)

    def respond(g, carry):
        rows = pl.ds(pl.multiple_of(g * nk, nk), nk)
        ug[g] = _gelu(ug[g] + _dot(xa_s[rows, :].astype(BF16), wo_ref[g]))
        return carry

    lax.fori_loop(0, ng, respond, 0)

    def from_groups(r, carry):
        base = pl.multiple_of(r * rows8 * t, rows8 * t)
        rows = pl.ds(pl.multiple_of(r * rows8, rows8), rows8)
        for h in range(nh):
            for j in range(t // per_half):
                groups = [ug[h * per_half + k, rows, j * LANES:(j + 1) * LANES] for k in range(per_half)]
                for i, v in enumerate(block_transpose(groups)):
                    ps[h, pl.ds(base + j * per_half + i, rows8, stride=t), :] = v
        return carry

    lax.fori_loop(0, nk // rows8, from_groups, 0, unroll=2)
    for h in range(nh):
        y_ref[:, h * LANES:(h + 1) * LANES] = ps[h].astype(y_ref.dtype)


def _s5(us, prm, layer, bsz, l):
    n, w = us.shape
    ng = w // S5_GROUP_DIM
    nk = l // S5_T
    p2 = prm[2].shape[-2]
    row = pl.BlockSpec((l, w), lambda b: (b, 0))
    return pl.pallas_call(
        _s5_kernel,
        grid=(bsz,),
        in_specs=[row] + [_layer_spec(a, layer) for a in prm],
        out_specs=row,
        out_shape=jax.ShapeDtypeStruct((n, w), BF16),
        scratch_shapes=[pltpu.VMEM((w // LANES, l, LANES), F32), pltpu.VMEM((ng, nk, S5_T * S5_GROUP_DIM), F32),
                        pltpu.VMEM((ng * nk, p2), F32), pltpu.VMEM((ng * nk, p2), F32)],
        compiler_params=_cparams("parallel"),
        name="s5",
    )(us, *prm)


def _s5_params(a_re, a_im, log_dt, b_re, b_im, c_re, c_im, d):
    t = S5_T
    g, p = a_re.shape
    gd = S5_GROUP_DIM
    hi = lax.Precision.HIGHEST
    a = lax.complex(a_re.astype(F32), a_im.astype(F32))
    dt = jnp.exp(log_dt.astype(F32))[:, None]
    a_bar = jnp.exp(a * dt)
    bm = lax.complex(b_re.astype(F32), b_im.astype(F32))
    cm = lax.complex(c_re.astype(F32), c_im.astype(F32))
    b_bar = ((a_bar - 1.0) / a)[..., None] * bm
    steps = jnp.arange(t + 1, dtype=F32)
    pw = jnp.exp((a * dt)[:, None, :] * steps[None, :, None])

    cp = (cm[:, None, :, :] * pw[:, :, None, :]).transpose(0, 3, 1, 2)
    w = t * gd
    lag = cp[:, :, :t].reshape(g, p, w)
    k2 = jnp.einsum('gpc,gpx->gcx', jnp.concatenate([b_bar.real, -b_bar.imag], axis=1),
                    jnp.concatenate([lag.real, lag.imag], axis=1), precision=hi)
    shift = np.zeros((t, w, w), np.float32)
    for s in range(t):
        shift[s, np.arange(w - s * gd), np.arange(w - s * gd) + s * gd] = 1.0
    m = jnp.einsum('gcx,sxl->gscl', k2, shift, precision=hi).reshape(g, w, w)
    win = pw[:, :t][:, ::-1][:, :, :, None] * b_bar[:, None]
    win = win.transpose(0, 1, 3, 2).reshape(g, w, p)
    wab = jnp.concatenate([win.real, win.imag, win.imag, win.real], axis=-1)
    co = cp[:, :, 1:].reshape(g, p, w)
    wo = jnp.concatenate([co.real, -co.imag], axis=1)
    lt = pw[:, t]
    c1 = jnp.concatenate([lt.real, lt.real], axis=-1)
    c2a = jnp.concatenate([-lt.imag, lt.imag], axis=-1)
    c2b = jnp.concatenate([lt.imag, -lt.imag], axis=-1)
    dtile = jnp.tile(d.astype(F32).reshape(g, 1, gd), (1, t, 1)).reshape(g, 1, t * gd)
    return (m.astype(BF16), wab.astype(BF16), wo.astype(BF16), c1, c2a, c2b, dtile)


def _out_proj_kernel(ya_ref, yr_ref, ys_ref, x_ref, gg_ref, wglu_ref, wo_ref, gf_ref, *rest, dense, chunks):
    ys = ys_ref[...]
    ys = ys.astype(F32) * _sigmoid(_dot(ys.astype(BF16), wglu_ref[...].astype(BF16)))
    acc = x_ref[...]
    lo = 0
    for y in (ya_ref[...].astype(F32), yr_ref[...].astype(F32), ys):
        w = y.shape[1]
        acc = acc + _dot(_rms(y, gg_ref[:, lo:lo + w]).astype(BF16), wo_ref[lo:lo + w, :].astype(BF16))
        lo += w
    h = _rms(acc, gf_ref[...])
    hb = h.astype(BF16)
    if dense:
        w1_ref, w3_ref, w2_ref, xo_ref = rest
        xo_ref[...] = acc + _swiglu_chunks(hb, w1_ref, w3_ref, w2_ref, chunks)
        return
    rt_ref, xo_ref, h_ref, lg_ref = rest
    xo_ref[...] = acc
    h_ref[...] = hb
    both = _dot(hb, rt_ref[...])
    h_lo = (h - hb.astype(F32)).astype(BF16)
    lg_ref[...] = both[:, 0:LANES] + both[:, LANES:2 * LANES] + _dot(h_lo, rt_ref[:, 0:LANES])


def _out_proj(ya, yr, ys, x2, gg, wglu_all, wo_all, layer, gf, *, ffn=None, router=None):
    n, dm = x2.shape
    tm = min(ROW_TILE, n)
    row = lambda w: pl.BlockSpec((tm, w), lambda i: (i, 0))
    ins = [ya, yr, ys, x2, gg.reshape(1, -1), wglu_all, wo_all, gf.reshape(1, dm)]
    in_specs = [row(ya.shape[1]), row(yr.shape[1]), row(ys.shape[1]), row(dm), _const_spec((1, gg.shape[0])),
                _layer_spec(wglu_all, layer), _layer_spec(wo_all, layer), _const_spec((1, dm))]
    if ffn is not None:
        ins += list(ffn)
        in_specs += [_const_spec(w.shape) for w in ffn]
        out_specs = row(dm)
        out_shape = jax.ShapeDtypeStruct((n, dm), F32)
        chunks = _ffn_chunks(ffn[0].shape[1])
    else:
        ne = router.shape[1]
        r_hi = router.astype(BF16)
        r_lo = (router.astype(F32) - r_hi.astype(F32)).astype(BF16)
        pad = lambda a: jnp.pad(a, ((0, 0), (0, LANES - ne)))
        ins.append(jnp.concatenate([pad(r_hi), pad(r_lo)], axis=1))
        in_specs.append(_const_spec((dm, 2 * LANES)))
        out_specs = [row(dm), row(dm), row(LANES)]
        out_shape = [jax.ShapeDtypeStruct((n, dm), F32), jax.ShapeDtypeStruct((n, dm), BF16),
                     jax.ShapeDtypeStruct((n, LANES), F32)]
        chunks = None
    return pl.pallas_call(
        functools.partial(_out_proj_kernel, dense=ffn is not None, chunks=chunks),
        grid=(n // tm,),
        in_specs=in_specs, out_specs=out_specs, out_shape=out_shape,
        compiler_params=pltpu.CompilerParams(dimension_semantics=("parallel",), vmem_limit_bytes=MIX_VMEM_LIMIT),
        name="out_proj",
    )(*ins)


def _swiglu_chunks(h, w1, w3, w2, chunks, before_chunk=None, after_chunk=None):
    acc = None
    for j, (lo, width) in enumerate(chunks):
        if before_chunk is not None:
            before_chunk(j)
        a = _dot(h, w1[:, lo:lo + width].astype(BF16))
        b = _dot(h, w3[:, lo:lo + width].astype(BF16))
        t = (a * _sigmoid(a) * b).astype(BF16)
        y = _dot(t, w2[lo:lo + width, :].astype(BF16))
        acc = y if acc is None else acc + y
        if after_chunk is not None:
            after_chunk(j)
    return acc


def _ffn_chunks(f):
    if f == sum(w for _, w in FFN_CHUNKS):
        return FFN_CHUNKS
    return ((0, f),)


def _expert_weight_copies(hbm, vmem, sems, chunks, expert, j):
    h1, h3, h2 = hbm
    v1, v3, v2 = vmem
    lo, w = chunks[j]
    return [pltpu.make_async_copy(h1.at[expert, :, lo:lo + w], v1.at[:, lo:lo + w], sems.at[j, 0]),
            pltpu.make_async_copy(h3.at[expert, :, lo:lo + w], v3.at[:, lo:lo + w], sems.at[j, 1]),
            pltpu.make_async_copy(h2.at[expert, lo:lo + w, :], v2.at[lo:lo + w, :], sems.at[j, 2])]


def _expert_ffn_kernel(brow_ref, bexp_ref, role_ref, x_ref, w1_hbm, w3_hbm, w2_hbm, o_ref, w1_v, w3_v, w2_v, sems,
                       *, chunks):
    i = pl.program_id(0)
    copies = functools.partial(_expert_weight_copies, (w1_hbm, w3_hbm, w2_hbm), (w1_v, w3_v, w2_v), sems, chunks)

    @pl.when(i == 0)
    def _():
        for j in range(len(chunks)):
            for cp in copies(bexp_ref[0], j):
                cp.start()

    def block(first, last):
        def before(j):
            for cp in copies(bexp_ref[i], j):
                cp.wait()

        def after(j):
            for cp in copies(bexp_ref[i + 1], j):
                cp.start()

        o_ref[...] = _swiglu_chunks(x_ref[...], w1_v, w3_v, w2_v, chunks, before if first else None,
                                    after if last else None).astype(o_ref.dtype)

    for role in range(4):
        pl.when(role_ref[i] == role)(functools.partial(block, bool(role & 1), bool(role & 2)))


def _expert_ffn(xb, w1, w3, w2, blk_row, blk_exp, role, max_blocks):
    rows, dm = xb.shape
    f = w1.shape[2]
    chunks = _ffn_chunks(f)
    row = pl.BlockSpec((MOE_BLK, dm), lambda i, br, be, ro: (br[i], 0))
    hbm = pl.BlockSpec(memory_space=pl.ANY)
    return pl.pallas_call(
        functools.partial(_expert_ffn_kernel, chunks=chunks),
        grid_spec=pltpu.PrefetchScalarGridSpec(
            num_scalar_prefetch=3, grid=(max_blocks,),
            in_specs=[row, hbm, hbm, hbm], out_specs=row,
            scratch_shapes=[pltpu.VMEM(w1.shape[1:], w1.dtype), pltpu.VMEM(w3.shape[1:], w3.dtype),
                            pltpu.VMEM(w2.shape[1:], w2.dtype), pltpu.SemaphoreType.DMA((len(chunks), 3))]),
        out_shape=jax.ShapeDtypeStruct((rows, dm), BF16),
        compiler_params=_cparams("arbitrary"),
        name="expert_ffn",
    )(blk_row, blk_exp, role, xb, w1, w3, w2)


def _route_tile(logits_tok, filled, ne):
    tm = logits_tok.shape[0]
    logits = logits_tok.T[0:ne, :]
    eidx = lax.broadcasted_iota(I32, (ne, tm), 0)
    m1 = jnp.max(logits, axis=0, keepdims=True)
    i1 = jnp.min(jnp.where(logits == m1, eidx, ne), axis=0, keepdims=True)
    sel1 = eidx == i1
    rest = jnp.where(sel1, -jnp.inf, logits)
    m2 = jnp.max(rest, axis=0, keepdims=True)
    i2 = jnp.min(jnp.where(rest == m2, eidx, ne), axis=0, keepdims=True)
    sel2 = eidx == i2
    e2 = jnp.exp(m2 - m1)
    den = 1.0 + e2
    rf = jnp.where(sel1 | sel2, 1.0, 0.0)
    cnt = jnp.sum(rf, axis=1, keepdims=True)
    before = (lax.broadcasted_iota(I32, (tm, tm), 0) < lax.broadcasted_iota(I32, (tm, tm), 1))
    rank = _dot(rf.astype(BF16), jnp.where(before, 1.0, 0.0).astype(BF16))
    whole = lambda a: jnp.floor(a * (1.0 / BF16_ROWS)) * BF16_ROWS
    phase = filled - whole(filled)
    span = jnp.where(cnt > 0, whole(phase + cnt + (BF16_ROWS - 1)), 0.0)
    ecol = lax.broadcasted_iota(I32, (ne, 1), 0)
    off = jnp.zeros((ne, 1), F32)
    for j in range(ne - 1):
        off = off + jnp.where(ecol > j, span[j:j + 1, :], 0.0)
    place = off + phase + rank
    pos1 = jnp.sum(jnp.where(sel1, place, 0.0), axis=0, keepdims=True)
    pos2 = jnp.sum(jnp.where(sel2, place, 0.0), axis=0, keepdims=True)
    lane_major = jnp.concatenate([pos1, pos2, 1.0 / den, e2 / den, jnp.zeros((4, tm), F32)], axis=0)
    tok_major = jnp.concatenate([lane_major, jnp.zeros((LANES - 8, tm), F32)], axis=0).T
    return lane_major, tok_major, cnt


def _route_kernel(lg_ref, lane_ref, tok_ref, cnt_ref, filled, *, ne):
    tm = MOE_TILE

    @pl.when(pl.program_id(0) == 0)
    def _():
        filled[...] = jnp.zeros_like(filled)

    for t in range(lg_ref.shape[0] // tm):
        lane_major, tok_major, cnt = _route_tile(lg_ref[t * tm:(t + 1) * tm, :], filled[...], ne)
        lane_ref[:, t * tm:(t + 1) * tm] = lane_major
        tok_ref[t * tm:(t + 1) * tm, :] = tok_major
        cnt_ref[t] = jnp.broadcast_to(cnt, (ne, LANES)).astype(I32)
        filled[...] = filled[...] + cnt


def _route(logits, ne):
    n = logits.shape[0]
    tm = MOE_TILE
    span = min(ROUTE_SPAN, n)
    return pl.pallas_call(
        functools.partial(_route_kernel, ne=ne),
        grid=(n // span,),
        in_specs=[pl.BlockSpec((span, LANES), lambda i: (i, 0))],
        out_specs=[pl.BlockSpec((8, span), lambda i: (0, i)), pl.BlockSpec((span, LANES), lambda i: (i, 0)),
                   pl.BlockSpec((span // tm, ne, LANES), lambda i: (i, 0, 0))],
        out_shape=[jax.ShapeDtypeStruct((8, n), F32), jax.ShapeDtypeStruct((n, LANES), F32),
                   jax.ShapeDtypeStruct((n // tm, ne, LANES), I32)],
        scratch_shapes=[pltpu.VMEM((ne, 1), F32)],
        compiler_params=_cparams("arbitrary"),
        name="moe_route",
    )(logits)


def _segment_copies(seg_ref, grouped_ref, base_ref, tile_buf, off_ref, sems, tile, slot, ne, *, to_grouped, wait):
    for e in range(ne):
        seg = seg_ref[tile * ne + e]
        far = base_ref[tile * ne + e]
        near = off_ref[tile * ne + e]
        for piece in SEG_PIECES:
            g = grouped_ref.at[pl.ds(pl.multiple_of(far, BF16_ROWS), piece)]
            t = tile_buf.at[slot, pl.ds(pl.multiple_of(near, BF16_ROWS), piece)]
            cp = pltpu.make_async_copy(t, g, sems.at[slot, e]) if to_grouped else \
                pltpu.make_async_copy(g, t, sems.at[slot, e])
            has = (seg & piece) != 0

            @pl.when(has)
            def _():
                cp.wait() if wait else cp.start()

            step = jnp.where(has, piece, 0)
            far = far + step
            near = near + step


def _tile_rows(ne):
    return TOP_K * MOE_TILE + ne * 2 * BF16_ROWS


def _tail_copies(end_ref, zeros, xb_ref, sems, ne):
    return [pltpu.make_async_copy(zeros, xb_ref.at[pl.ds(pl.multiple_of(end_ref[e], BF16_ROWS), MOE_BLK)], sems.at[e])
            for e in range(ne)]


def _dispatch_kernel(seg_ref, base_ref, off_ref, phase_ref, end_ref, rt_ref, h_ref, xb_ref, res, zeros, partial,
                     sems, tail_sems):
    i = pl.program_id(0)
    nt = pl.num_programs(0)
    ne = tail_sems.shape[0]
    tm = h_ref.shape[0]
    mrows = res.shape[1]
    slot = i % 2
    copies = functools.partial(_segment_copies, seg_ref, xb_ref, base_ref, res, off_ref, sems, ne=ne, to_grouped=True)

    @pl.when(i == 0)
    def _():
        zeros[...] = jnp.zeros_like(zeros)
        partial[...] = jnp.zeros_like(partial)
        for cp in _tail_copies(end_ref, zeros, xb_ref, tail_sems, ne):
            cp.start()

    row = lax.broadcasted_iota(I32, (mrows, tm), 0)
    pos = rt_ref[0:2, :].astype(I32)
    perm = jnp.where((row == pos[0:1, :]) | (row == pos[1:2, :]), 1.0, 0.0).astype(BF16)
    res[slot] = _dot(perm, h_ref[...]).astype(BF16)

    tile_row = lax.broadcasted_iota(I32, (BF16_ROWS, res.shape[2]), 0)
    for e in range(ne):
        @pl.when(seg_ref[i * ne + e] > 0)
        def _():
            first = pl.ds(pl.multiple_of(off_ref[i * ne + e], BF16_ROWS), BF16_ROWS)
            merged = jnp.where(tile_row < phase_ref[i * ne + e], partial[e].astype(F32), res[slot, first, :].astype(F32))
            res[slot, first, :] = merged.astype(BF16)
            last = pl.ds(pl.multiple_of(off_ref[i * ne + e] + seg_ref[i * ne + e] - BF16_ROWS, BF16_ROWS), BF16_ROWS)
            partial[e] = res[slot, last, :]

    @pl.when(i > 0)
    def _():
        copies(tile=i - 1, slot=1 - slot, wait=True)

    copies(tile=i, slot=slot, wait=False)

    @pl.when(i == nt - 1)
    def _():
        copies(tile=i, slot=slot, wait=True)
        for cp in _tail_copies(end_ref, zeros, xb_ref, tail_sems, ne):
            cp.wait()


def _dispatch(h, rt_lane, lay, ne):
    n, dm = h.shape
    tm = MOE_TILE
    return pl.pallas_call(
        _dispatch_kernel,
        grid_spec=pltpu.PrefetchScalarGridSpec(
            num_scalar_prefetch=5, grid=(n // tm,),
            in_specs=[pl.BlockSpec((8, tm), lambda i, *_: (0, i)), pl.BlockSpec((tm, dm), lambda i, *_: (i, 0))],
            out_specs=pl.BlockSpec(memory_space=pl.ANY),
            scratch_shapes=[pltpu.VMEM((2, _tile_rows(ne), dm), BF16), pltpu.VMEM((MOE_BLK, dm), BF16),
                            pltpu.VMEM((ne, BF16_ROWS, dm), BF16),
                            pltpu.SemaphoreType.DMA((2, ne)), pltpu.SemaphoreType.DMA((ne,))]),
        out_shape=jax.ShapeDtypeStruct((lay['rows_total'], dm), BF16),
        compiler_params=_cparams("arbitrary"),
        name="moe_dispatch",
    )(lay['seg'], lay['base'], lay['off'], lay['phase'], lay['end'], rt_lane, h)


def _combine_kernel(seg_ref, base_ref, off_ref, rt_ref, x_ref, gn_ref, yb_ref, o_ref, got, sems, *, ne):
    i = pl.program_id(0)
    nt = pl.num_programs(0)
    tm = x_ref.shape[0]
    mrows = got.shape[1]
    slot = i % 2
    copies = functools.partial(_segment_copies, seg_ref, yb_ref, base_ref, got, off_ref, sems, ne=ne, to_grouped=False)

    @pl.when(i == 0)
    def _():
        got[...] = jnp.zeros_like(got)
        copies(tile=i, slot=slot, wait=False)

    @pl.when(i + 1 < nt)
    def _():
        copies(tile=i + 1, slot=1 - slot, wait=False)

    copies(tile=i, slot=slot, wait=True)

    col = lax.broadcasted_iota(I32, (tm, mrows), 1)
    rt = rt_ref[...]
    rows = got[slot]
    sel = jnp.concatenate([jnp.where(col == rt[:, k:k + 1].astype(I32), 1.0, 0.0).astype(BF16) for k in range(TOP_K)],
                          axis=0)
    picked = _dot(sel, rows)
    y = sum(rt[:, TOP_K + k:TOP_K + k + 1] * picked[k * tm:(k + 1) * tm] for k in range(TOP_K))
    o_ref[...] = _rms(x_ref[...] + y, gn_ref[...])


def _combine(yb, rt_tok, x2, gn, lay, ne):
    n, dm = x2.shape
    tm = MOE_TILE
    mrows = _tile_rows(ne)
    tok = lambda w: pl.BlockSpec((tm, w), lambda i, *_: (i, 0))
    return pl.pallas_call(
        functools.partial(_combine_kernel, ne=ne),
        grid_spec=pltpu.PrefetchScalarGridSpec(
            num_scalar_prefetch=3, grid=(n // tm,),
            in_specs=[tok(LANES), tok(dm), pl.BlockSpec((1, dm), lambda i, *_: (0, 0)),
                      pl.BlockSpec(memory_space=pl.ANY)],
            out_specs=tok(dm),
            scratch_shapes=[pltpu.VMEM((2, mrows, dm), BF16), pltpu.SemaphoreType.DMA((2, ne))]),
        out_shape=jax.ShapeDtypeStruct((n, dm), F32),
        compiler_params=_cparams("arbitrary"),
        name="moe_combine",
    )(lay['seg'], lay['base'], lay['off'], rt_tok, x2, gn.reshape(1, dm), yb)


def _moe_layout(cnt, n):
    nt, ne = cnt.shape
    blk = MOE_BLK
    whole = lambda a: a // BF16_ROWS * BF16_ROWS
    cap = -(-(n + BF16_ROWS + blk) // blk) * blk
    region = jnp.arange(ne, dtype=I32) * cap
    filled = jnp.cumsum(cnt, axis=0) - cnt
    phase = filled - whole(filled)
    span = jnp.where(cnt > 0, whole(phase + cnt + BF16_ROWS - 1), 0)
    base = (region[None, :] + filled - phase).astype(I32)
    off = (jnp.cumsum(span, axis=1) - span).astype(I32)
    rows_e = jnp.sum(cnt, axis=0)
    nblk_e = (rows_e + blk - 1) // blk
    end = (region + whole(rows_e + BF16_ROWS - 1)).astype(I32)
    ends = jnp.cumsum(nblk_e)
    max_blocks = TOP_K * n // blk + ne
    bid = jnp.arange(max_blocks + 1, dtype=I32)
    bexp = jnp.minimum(jnp.sum((bid[:, None] >= ends[None, :]).astype(I32), axis=1), ne - 1)
    brow = bexp * (cap // blk) + bid - (ends - nblk_e)[bexp]
    nblk = ends[-1]
    last = jnp.maximum(nblk - 1, 0)
    used = bid < nblk
    first = bid == (ends - nblk_e)[bexp]
    hand_over = (bid == ends[bexp] - 1) & (bid < last)
    role = jnp.where(used, first.astype(I32) + 2 * hand_over.astype(I32), 4).astype(I32)
    bexp = jnp.where(used, bexp, bexp[last]).astype(I32)
    brow = jnp.where(used, brow, brow[last]).astype(I32)
    flat = lambda a: a.reshape(-1)
    return dict(seg=flat(span.astype(I32)), base=flat(base), off=flat(off), phase=flat(phase.astype(I32)), end=end,
                brow=brow, bexp=bexp, role=role, rows_total=ne * cap, max_blocks=max_blocks)


def _moe(x2, h, logits, gn, w1, w3, w2):
    n, dm = x2.shape
    ne = w1.shape[0]
    rt_lane, rt_tok, cnt = _route(logits, ne)
    lay = _moe_layout(cnt[:, :, 0], n)
    xb = _dispatch(h, rt_lane, lay, ne)
    yb = _expert_ffn(xb, w1, w3, w2, lay['brow'], lay['bexp'], lay['role'], lay['max_blocks'])
    return _combine(yb, rt_tok, x2, gn, lay, ne)


def kernel(x, norm_mix_g, w_in, attn_rel_bias, rg_conv_w, rg_conv_b, rg_wx, rg_bx, rg_wa, rg_ba, rg_lambda, s5_a_re, s5_a_im, s5_log_dt, s5_b_re, s5_b_im, s5_c_re, s5_c_im, s5_d, s5_w_glu, g_group, w_out, norm_ffn_g, ffn_w1, ffn_w3, ffn_w2, moe_router, moe_w1, moe_w3, moe_w2, final_norm_g):
    bsz, l, dm = x.shape
    depth = w_in.shape[0]
    assert depth == 2 and l % ATTN_TQ == 0 and ATTN_TQ == N_PREV * CHUNK, "dense layer, then the MoE layer"
    d_rg = rg_conv_w.shape[2]
    d_s5 = s5_w_glu.shape[1]
    d_attn = (w_in.shape[2] - 2 * d_rg - d_s5) // 3
    x2 = x.reshape(bsz * l, dm)
    attn_bias = jax.vmap(_attn_bias)(attn_rel_bias)
    rg_gates_w = jnp.concatenate([jax.vmap(_block_diag)(rg_wx), jax.vmap(_block_diag)(rg_wa)], axis=2).astype(BF16)
    rg_gates_b = jnp.concatenate([rg_bx, rg_ba], axis=1).astype(F32)[:, None, :]
    rg_decay = jax.nn.softplus(-rg_lambda.astype(F32))[:, None, :]
    s5_prm = jax.vmap(_s5_params)(s5_a_re, s5_a_im, s5_log_dt, s5_b_re, s5_b_im, s5_c_re, s5_c_im, s5_d)
    for layer in range(depth):
        q, k, vt, xr, gate, us = _in_proj(x2, norm_mix_g[layer], w_in, layer, d_attn, d_rg, d_s5)
        y_attn = _attention(q, k, vt, attn_bias, layer, bsz, l)
        y_rg = _rglru(xr, gate, rg_conv_w.astype(F32), rg_conv_b.astype(F32)[:, None, :], rg_gates_w, rg_gates_b,
                      rg_decay, layer, bsz, l)
        y_s5 = _s5(us, s5_prm, layer, bsz, l)
        mixed = (y_attn, y_rg, y_s5, x2, g_group[layer].astype(F32), s5_w_glu, w_out, layer,
                 norm_ffn_g[layer].astype(F32))
        if layer == 0:
            x2 = _out_proj(*mixed, ffn=(ffn_w1[0], ffn_w3[0], ffn_w2[0]))
        else:
            x2, h, logits = _out_proj(*mixed, router=moe_router[0])
            x2 = _moe(x2, h, logits, final_norm_g.astype(F32), moe_w1[0], moe_w3[0], moe_w2[0])
    return x2.reshape(bsz, l, dm)
```

```python
import functools
import math

import jax
import jax.numpy as jnp
import numpy as np
from jax import lax
from jax.experimental import pallas as pl
from jax.experimental.pallas import tpu as pltpu

F32 = jnp.float32
BF16 = jnp.bfloat16
I32 = jnp.int32

EPS = 1e-6
CHUNK = 64
N_PREV = 8
BAND = (N_PREV + 1) * CHUNK
REL_CLIP = 128
HEAD_DIM = 64
RG_C = 8.0
RG_CONV_WIDTH = 4
S5_GROUP_DIM = 16
S5_T = 16
TOP_K = 2
NEG_BIG = -1e30

LANES = 128
SUBLANES = 8
BF16_ROWS = 16
VMEM_LIMIT = 52 * 1024 * 1024
MIX_VMEM_LIMIT = 58 * 1024 * 1024

ROW_TILE = 512
PROJ_TILE = 1024
ATTN_TQ = 512
ATTN_SPAN = 2 * CHUNK
ATTN_WIN = N_PREV * CHUNK + ATTN_SPAN
RG_TL = 1024
MOE_TILE = 256
MOE_BLK = 512
ROUTE_SPAN = 2048
SEG_PIECES = tuple(BF16_ROWS << b for b in range((MOE_TILE // BF16_ROWS).bit_length() - 1, -1, -1))
FFN_CHUNKS = ((0, 512), (512, 512), (1024, 512), (1536, 512), (2048, 512), (2560, 256))


def _cparams(*sem):
    return pltpu.CompilerParams(dimension_semantics=sem, vmem_limit_bytes=VMEM_LIMIT)


def _const_spec(shape):
    nd = len(shape)
    return pl.BlockSpec(shape, lambda *_: (0,) * nd, pipeline_mode=pl.Buffered(1))


def _rms(xf, g):
    var = jnp.mean(xf * xf, axis=-1, keepdims=True)
    return xf * lax.rsqrt(var + EPS) * g


def _sigmoid(x):
    return 1.0 / (1.0 + jnp.exp(-x))


def _gelu(x):
    c = math.sqrt(2.0 / math.pi)
    return 0.5 * x * (1.0 + jnp.tanh(c * (x + 0.044715 * (x * x * x))))


def _dot(a, b):
    return jnp.dot(a, b, preferred_element_type=F32)


def _in_proj_kernel(x_ref, g_ref, w_ref, wvt_ref, q_ref, k_ref, vt_ref, xr_ref, gt_ref, us_ref, *, d_attn, d_rg):
    u = _rms(x_ref[...], g_ref[...]).astype(BF16)
    lo = 0
    for ref, width in ((q_ref, d_attn), (k_ref, d_attn), (None, d_attn),
                       (xr_ref, d_rg), (gt_ref, d_rg), (us_ref, w_ref.shape[1] - 3 * d_attn - 2 * d_rg)):
        if ref is not None:
            ref[...] = _dot(u, w_ref[:, lo:lo + width].astype(BF16)).astype(ref.dtype)
        lo += width
    vt_ref[...] = lax.dot_general(wvt_ref[...].astype(BF16), u, (((1,), (1,)), ((), ())),
                                  preferred_element_type=F32).astype(vt_ref.dtype)


def _layer_spec(stacked, layer):
    rest = stacked.shape[1:]
    return pl.BlockSpec((None,) + rest, lambda *_: (layer,) + (0,) * len(rest), pipeline_mode=pl.Buffered(1))


def _in_proj(x2, g, w_all, layer, d_attn, d_rg, d_s5):
    n, dm = x2.shape
    tm = min(PROJ_TILE, n)
    row = lambda w: pl.BlockSpec((tm, w), lambda i: (i, 0))
    wv_t = w_all[layer, :, 2 * d_attn:3 * d_attn].T
    return pl.pallas_call(
        functools.partial(_in_proj_kernel, d_attn=d_attn, d_rg=d_rg),
        grid=(n // tm,),
        in_specs=[row(dm), _const_spec((1, dm)), _layer_spec(w_all, layer), _const_spec(wv_t.shape)],
        out_specs=[row(d_attn), row(d_attn), pl.BlockSpec((d_attn, tm), lambda i: (0, i)),
                   row(d_rg), row(d_rg), row(d_s5)],
        out_shape=[jax.ShapeDtypeStruct((n, d_attn), BF16)] * 2 + [jax.ShapeDtypeStruct((d_attn, n), BF16)]
        + [jax.ShapeDtypeStruct((n, d_rg), F32)] * 2 + [jax.ShapeDtypeStruct((n, d_s5), F32)],
        compiler_params=_cparams("parallel"),
        name="in_proj",
    )(x2, g.reshape(1, dm), w_all, wv_t)


def _attn_kernel(q_ref, kp_ref, kc_ref, vp_ref, vc_ref, bias_ref, o_ref, kz, sc, *, n_pairs):
    qi = pl.program_id(1)
    tq = q_ref.shape[0]
    kz[0:tq, :] = kp_ref[...]
    kz[tq:2 * tq, :] = kc_ref[...]
    first = lax.broadcasted_iota(I32, (ATTN_SPAN, LANES), 1) < HEAD_DIM
    pairs = [slice(hp * LANES, (hp + 1) * LANES) for hp in range(n_pairs)]

    def score(j, masked):
        r0 = j * ATTN_SPAN
        for hp, cols in enumerate(pairs):
            q2 = q_ref[pl.ds(r0, ATTN_SPAN), cols] * jnp.asarray(HEAD_DIM ** -0.5, BF16)
            zero = jnp.zeros_like(q2)
            qq = jnp.concatenate([jnp.where(first, q2, zero), jnp.where(first, zero, q2)], axis=0)
            s = lax.dot_general(kz[pl.ds(r0, ATTN_WIN), cols], qq, (((1,), (1,)), ((), ())),
                                preferred_element_type=F32) + bias_ref[hp]
            if masked:
                key = lax.broadcasted_iota(I32, (ATTN_WIN, 2 * ATTN_SPAN), 0)
                s = jnp.where(key >= tq - r0, s, NEG_BIG)
            sc[j % 2, hp] = s

    def finish(j):
        r0 = j * ATTN_SPAN
        old = tq - r0
        probs = []
        for hp in range(n_pairs):
            s = sc[j % 2, hp]
            probs.append(jnp.exp((s - jnp.max(s, axis=0, keepdims=True)).astype(BF16)))
        for hp, (cols, p) in enumerate(zip(pairs, probs)):
            ones = lambda w: jnp.ones((BF16_ROWS, w), BF16)
            ov = (_dot(jnp.concatenate([vp_ref[cols, r0:tq], ones(old)], axis=0), p[0:old])
                  + _dot(jnp.concatenate([vc_ref[cols, 0:ATTN_WIN - old], ones(ATTN_WIN - old)], axis=0), p[old:]))
            o2 = (ov[0:LANES] / ov[LANES:LANES + 1]).T
            o_ref[pl.ds(r0, ATTN_SPAN), cols] = jnp.where(first, o2[0:ATTN_SPAN], o2[ATTN_SPAN:]).astype(o_ref.dtype)

    def run(masked):
        groups = tq // ATTN_SPAN
        score(0, masked)
        for j in range(groups):
            if j + 1 < groups:
                score(j + 1, masked)
            finish(j)

    pl.when(qi == 0)(functools.partial(run, True))
    pl.when(qi != 0)(functools.partial(run, False))


def _attention(q, k, vt, bias_t, layer, bsz, l):
    n, da = q.shape
    tq = ATTN_TQ
    nt = l // tq
    n_pairs = da // LANES
    cur = pl.BlockSpec((tq, da), lambda b, i: (b * nt + i, 0))
    prev = pl.BlockSpec((tq, da), lambda b, i: (b * nt + jnp.maximum(i - 1, 0), 0))
    cur_t = pl.BlockSpec((da, tq), lambda b, i: (0, b * nt + i))
    prev_t = pl.BlockSpec((da, tq), lambda b, i: (0, b * nt + jnp.maximum(i - 1, 0)))
    return pl.pallas_call(
        functools.partial(_attn_kernel, n_pairs=n_pairs),
        grid=(bsz, nt),
        in_specs=[cur, prev, cur, prev_t, cur_t, _layer_spec(bias_t, layer)],
        out_specs=cur,
        out_shape=jax.ShapeDtypeStruct((n, da), BF16),
        scratch_shapes=[pltpu.VMEM((2 * tq, da), BF16), pltpu.VMEM((2, n_pairs, ATTN_WIN, 2 * ATTN_SPAN), F32)],
        compiler_params=_cparams("parallel", "parallel"),
        name="chunk_attn",
    )(q, k, k, vt, vt, bias_t)


def _attn_bias(rel_bias):
    h = rel_bias.shape[0]
    tab = rel_bias.astype(F32)
    n_far = N_PREV * CHUNK - REL_CLIP + CHUNK
    lo = 2 * REL_CLIP - (BAND + CHUNK - 1 - n_far)
    ext = jnp.concatenate([jnp.broadcast_to(tab[:, 2 * REL_CLIP:], (h, n_far)), tab[:, lo:2 * REL_CLIP][:, ::-1]], axis=1)
    wide = BAND + CHUNK
    ring = jnp.concatenate([ext[:, CHUNK - 1:], jnp.zeros((h, 1), F32), ext[:, :CHUNK - 1]], axis=1)
    bias = jnp.broadcast_to(ring[:, None, :], (h, CHUNK, wide)).reshape(h, CHUNK * wide)
    bias = bias[:, :CHUNK * (wide - 1)].reshape(h, CHUNK, wide - 1)[..., :BAND]
    band_t = bias.transpose(0, 2, 1)
    groups = ATTN_SPAN // CHUNK
    per_chunk = [jnp.pad(band_t, ((0, 0), (c * CHUNK, (groups - 1 - c) * CHUNK), (0, 0)), constant_values=NEG_BIG)
                 for c in range(groups)]
    win = jnp.stack(per_chunk, axis=2)
    return win.reshape(h // 2, 2, ATTN_WIN, ATTN_SPAN).transpose(0, 2, 1, 3).reshape(h // 2, ATTN_WIN, 2 * ATTN_SPAN)


def _rglru_kernel(x_ref, gt_ref, cw_ref, cb_ref, w_ref, b_ref, sp_ref, o_ref, xpad, a_s, b_s, hcar):
    t = pl.program_id(1)
    tl, c = x_ref.shape
    front = SUBLANES

    @pl.when(t == 0)
    def _():
        xpad[0:front, :] = jnp.zeros((front, c), F32)
        hcar[...] = jnp.zeros_like(hcar)

    xpad[front:front + tl, :] = x_ref[...]
    xc = cb_ref[...] + sum(
        cw_ref[j:j + 1, :] * xpad[front - (RG_CONV_WIDTH - 1) + j:front - (RG_CONV_WIDTH - 1) + j + tl, :]
        for j in range(RG_CONV_WIDTH))
    xpad[0:front, :] = xpad[tl:tl + front, :]
    pre = _dot(xc.astype(BF16), w_ref[...]) + b_ref[...]
    gx = _sigmoid(pre[:, 0:c])
    ga = _sigmoid(pre[:, c:2 * c])
    log_a = -RG_C * ga * sp_ref[...]
    a = jnp.exp(log_a)
    mult = jnp.sqrt(-jnp.tanh(log_a) * (a * a + 1.0))
    b = mult * gx * xc
    nh = c // LANES
    for j in range(nh):
        a_s[j] = a[:, j * LANES:(j + 1) * LANES]
        b_s[j] = b[:, j * LANES:(j + 1) * LANES]

    step_in_vreg = lax.broadcasted_iota(I32, (SUBLANES, LANES), 0)

    def vreg_scan(r, h_in):
        rows = pl.ds(pl.multiple_of(r * SUBLANES, SUBLANES), SUBLANES)
        out = []
        for j in range(nh):
            av, bv = a_s[j, rows, :], b_s[j, rows, :]
            for d in (1, 2, 4):
                seen = step_in_vreg >= d
                a_prev = jnp.where(seen, pltpu.roll(av, d, 0), 1.0)
                b_prev = jnp.where(seen, pltpu.roll(bv, d, 0), 0.0)
                bv = bv + av * b_prev
                av = av * a_prev
            h = bv + av * h_in[j]
            b_s[j, rows, :] = h
            out.append(h[SUBLANES - 1:SUBLANES, :])
        return tuple(out)

    h_last = lax.fori_loop(0, tl // SUBLANES, vreg_scan,
                           tuple(hcar[:, j * LANES:(j + 1) * LANES] for j in range(nh)), unroll=4)
    for j in range(nh):
        cols = slice(j * LANES, (j + 1) * LANES)
        hcar[:, cols] = h_last[j]
        o_ref[:, cols] = (b_s[j] * _gelu(gt_ref[:, cols])).astype(o_ref.dtype)


def _rglru(xr, gate, conv_w, conv_b, w_bd, b_cat, sp, layer, bsz, l):
    n, c = xr.shape
    tl = min(RG_TL, l)
    nt = l // tl
    row = pl.BlockSpec((tl, c), lambda b, t: (b * nt + t, 0))
    return pl.pallas_call(
        _rglru_kernel,
        grid=(bsz, nt),
        in_specs=[row, row] + [_layer_spec(a, layer) for a in (conv_w, conv_b, w_bd, b_cat, sp)],
        out_specs=row,
        out_shape=jax.ShapeDtypeStruct((n, c), BF16),
        scratch_shapes=[pltpu.VMEM((tl + 8, c), F32), pltpu.VMEM((c // LANES, tl, LANES), F32),
                        pltpu.VMEM((c // LANES, tl, LANES), F32), pltpu.VMEM((1, c), F32)],
        compiler_params=_cparams("parallel", "arbitrary"),
        name="rglru",
    )(xr, gate, conv_w, conv_b, w_bd, b_cat, sp)


def _block_diag(w):
    nb, d, _ = w.shape
    eye = jnp.eye(nb, dtype=w.dtype)
    return (eye[:, None, :, None] * w[:, :, None, :]).reshape(nb * d, nb * d)


def _s5_kernel(u_ref, m_ref, wab_ref, wo_ref, c1_ref, c2a_ref, c2b_ref, d_ref, y_ref, ps, ug, xa_s, xb_s):
    t, gd = S5_T, S5_GROUP_DIM
    ng, nk, _ = ug.shape
    nh = ps.shape[0]
    per_half = LANES // gd
    rows8 = 8

    for h in range(nh):
        ps[h] = u_ref[:, h * LANES:(h + 1) * LANES]

    lane_block = lax.broadcasted_iota(I32, (rows8, LANES), 1) // gd

    def block_transpose(vs):
        d = per_half // 2
        while d:
            keep = (lane_block & d) == 0
            out = list(vs)
            for i in range(per_half):
                if i & d == 0:
                    out[i] = jnp.where(keep, vs[i], pltpu.roll(vs[i + d], d * gd, 1))
                    out[i + d] = jnp.where(keep, pltpu.roll(vs[i], LANES - d * gd, 1), vs[i + d])
            vs, d = out, d // 2
        return vs

    def to_groups(r, carry):
        base = pl.multiple_of(r * rows8 * t, rows8 * t)
        rows = pl.ds(pl.multiple_of(r * rows8, rows8), rows8)
        for h in range(nh):
            for j in range(t // per_half):
                steps = [ps[h, pl.ds(base + j * per_half + i, rows8, stride=t), :] for i in range(per_half)]
                for k, v in enumerate(block_transpose(steps)):
                    ug[h * per_half + k, rows, j * LANES:(j + 1) * LANES] = v
        return carry

    lax.fori_loop(0, nk // rows8, to_groups, 0, unroll=2)

    def project(g, carry):
        u = ug[g]
        ub = u.astype(BF16)
        rows = pl.ds(pl.multiple_of(g * nk, nk), nk)
        xab = _dot(ub, wab_ref[g])
        xa_s[rows, :] = xab[:, 0:LANES]
        xb_s[rows, :] = xab[:, LANES:2 * LANES]
        ug[g] = _dot(ub, m_ref[g]) + d_ref[g] * u
        return carry

    lax.fori_loop(0, ng, project, 0, unroll=2)

    c1, c2a, c2b = c1_ref[...], c2a_ref[...], c2b_ref[...]

    def step(k, carry):
        xa, xb = carry
        rows = pl.ds(k, ng, stride=nk)
        ia = xa_s[rows, :]
        ib = xb_s[rows, :]
        xa_s[rows, :] = xa
        return c1 * xa + c2a * xb + ia, c1 * xb + c2b * xa + ib

    z = jnp.zeros((ng, xa_s.shape[1]), F32)
    lax.fori_loop(0, nk, step, (z, z), unroll=4)

    def respond(g, carry):
        rows = pl.ds(pl.multiple_of(g * nk, nk), nk)
        ug[g] = _gelu(ug[g] + _dot(xa_s[rows, :].astype(BF16), wo_ref[g]))
        return carry

    lax.fori_loop(0, ng, respond, 0, unroll=2)

    def from_groups(r, carry):
        base = pl.multiple_of(r * rows8 * t, rows8 * t)
        rows = pl.ds(pl.multiple_of(r * rows8, rows8), rows8)
        for h in range(nh):
            for j in range(t // per_half):
                groups = [ug[h * per_half + k, rows, j * LANES:(j + 1) * LANES] for k in range(per_half)]
                for i, v in enumerate(block_transpose(groups)):
                    ps[h, pl.ds(base + j * per_half + i, rows8, stride=t), :] = v
        return carry

    lax.fori_loop(0, nk // rows8, from_groups, 0, unroll=2)
    for h in range(nh):
        y_ref[:, h * LANES:(h + 1) * LANES] = ps[h].astype(y_ref.dtype)


def _s5(us, prm, layer, bsz, l):
    n, w = us.shape
    ng = w // S5_GROUP_DIM
    nk = l // S5_T
    p2 = prm[2].shape[-2]
    row = pl.BlockSpec((l, w), lambda b: (b, 0))
    return pl.pallas_call(
        _s5_kernel,
        grid=(bsz,),
        in_specs=[row] + [_layer_spec(a, layer) for a in prm],
        out_specs=row,
        out_shape=jax.ShapeDtypeStruct((n, w), BF16),
        scratch_shapes=[pltpu.VMEM((w // LANES, l, LANES), F32), pltpu.VMEM((ng, nk, S5_T * S5_GROUP_DIM), F32),
                        pltpu.VMEM((ng * nk, p2), F32), pltpu.VMEM((ng * nk, p2), F32)],
        compiler_params=_cparams("parallel"),
        name="s5",
    )(us, *prm)


def _s5_params(a_re, a_im, log_dt, b_re, b_im, c_re, c_im, d):
    t = S5_T
    g, p = a_re.shape
    gd = S5_GROUP_DIM
    hi = lax.Precision.HIGHEST
    a = lax.complex(a_re.astype(F32), a_im.astype(F32))
    dt = jnp.exp(log_dt.astype(F32))[:, None]
    a_bar = jnp.exp(a * dt)
    bm = lax.complex(b_re.astype(F32), b_im.astype(F32))
    cm = lax.complex(c_re.astype(F32), c_im.astype(F32))
    b_bar = ((a_bar - 1.0) / a)[..., None] * bm
    steps = jnp.arange(t + 1, dtype=F32)
    pw = jnp.exp((a * dt)[:, None, :] * steps[None, :, None])

    cp = (cm[:, None, :, :] * pw[:, :, None, :]).transpose(0, 3, 1, 2)
    w = t * gd
    lag = cp[:, :, :t].reshape(g, p, w)
    k2 = jnp.einsum('gpc,gpx->gcx', jnp.concatenate([b_bar.real, -b_bar.imag], axis=1),
                    jnp.concatenate([lag.real, lag.imag], axis=1), precision=hi)
    shift = np.zeros((t, w, w), np.float32)
    for s in range(t):
        shift[s, np.arange(w - s * gd), np.arange(w - s * gd) + s * gd] = 1.0
    m = jnp.einsum('gcx,sxl->gscl', k2, shift, precision=hi).reshape(g, w, w)
    win = pw[:, :t][:, ::-1][:, :, :, None] * b_bar[:, None]
    win = win.transpose(0, 1, 3, 2).reshape(g, w, p)
    wab = jnp.concatenate([win.real, win.imag, win.imag, win.real], axis=-1)
    co = cp[:, :, 1:].reshape(g, p, w)
    wo = jnp.concatenate([co.real, -co.imag], axis=1)
    lt = pw[:, t]
    c1 = jnp.concatenate([lt.real, lt.real], axis=-1)
    c2a = jnp.concatenate([-lt.imag, lt.imag], axis=-1)
    c2b = jnp.concatenate([lt.imag, -lt.imag], axis=-1)
    dtile = jnp.tile(d.astype(F32).reshape(g, 1, gd), (1, t, 1)).reshape(g, 1, t * gd)
    return (m.astype(BF16), wab.astype(BF16), wo.astype(BF16), c1, c2a, c2b, dtile)


def _out_proj_kernel(ya_ref, yr_ref, ys_ref, x_ref, gg_ref, wglu_ref, wo_ref, gf_ref, *rest, dense, chunks):
    ys = ys_ref[...]
    ys = ys.astype(F32) * _sigmoid(_dot(ys.astype(BF16), wglu_ref[...].astype(BF16)))
    acc = x_ref[...]
    lo = 0
    for y in (ya_ref[...].astype(F32), yr_ref[...].astype(F32), ys):
        w = y.shape[1]
        acc = acc + _dot(_rms(y, gg_ref[:, lo:lo + w]).astype(BF16), wo_ref[lo:lo + w, :].astype(BF16))
        lo += w
    h = _rms(acc, gf_ref[...])
    hb = h.astype(BF16)
    if dense:
        w1_ref, w3_ref, w2_ref, xo_ref = rest
        xo_ref[...] = acc + _swiglu_chunks(hb, w1_ref, w3_ref, w2_ref, chunks)
        return
    rt_ref, xo_ref, h_ref, lg_ref = rest
    xo_ref[...] = acc
    h_ref[...] = hb
    both = _dot(hb, rt_ref[...])
    h_lo = (h - hb.astype(F32)).astype(BF16)
    lg_ref[...] = both[:, 0:LANES] + both[:, LANES:2 * LANES] + _dot(h_lo, rt_ref[:, 0:LANES])


def _out_proj(ya, yr, ys, x2, gg, wglu_all, wo_all, layer, gf, *, ffn=None, router=None):
    n, dm = x2.shape
    tm = min(ROW_TILE, n)
    row = lambda w: pl.BlockSpec((tm, w), lambda i: (i, 0))
    ins = [ya, yr, ys, x2, gg.reshape(1, -1), wglu_all, wo_all, gf.reshape(1, dm)]
    in_specs = [row(ya.shape[1]), row(yr.shape[1]), row(ys.shape[1]), row(dm), _const_spec((1, gg.shape[0])),
                _layer_spec(wglu_all, layer), _layer_spec(wo_all, layer), _const_spec((1, dm))]
    if ffn is not None:
        ins += list(ffn)
        in_specs += [_const_spec(w.shape) for w in ffn]
        out_specs = row(dm)
        out_shape = jax.ShapeDtypeStruct((n, dm), F32)
        chunks = _ffn_chunks(ffn[0].shape[1])
    else:
        ne = router.shape[1]
        r_hi = router.astype(BF16)
        r_lo = (router.astype(F32) - r_hi.astype(F32)).astype(BF16)
        pad = lambda a: jnp.pad(a, ((0, 0), (0, LANES - ne)))
        ins.append(jnp.concatenate([pad(r_hi), pad(r_lo)], axis=1))
        in_specs.append(_const_spec((dm, 2 * LANES)))
        out_specs = [row(dm), row(dm), row(LANES)]
        out_shape = [jax.ShapeDtypeStruct((n, dm), F32), jax.ShapeDtypeStruct((n, dm), BF16),
                     jax.ShapeDtypeStruct((n, LANES), F32)]
        chunks = None
    return pl.pallas_call(
        functools.partial(_out_proj_kernel, dense=ffn is not None, chunks=chunks),
        grid=(n // tm,),
        in_specs=in_specs, out_specs=out_specs, out_shape=out_shape,
        compiler_params=pltpu.CompilerParams(dimension_semantics=("parallel",), vmem_limit_bytes=MIX_VMEM_LIMIT),
        name="out_proj",
    )(*ins)


def _swiglu_chunks(h, w1, w3, w2, chunks, before_chunk=None, after_chunk=None):
    acc = None
    for j, (lo, width) in enumerate(chunks):
        if before_chunk is not None:
            before_chunk(j)
        a = _dot(h, w1[:, lo:lo + width].astype(BF16))
        b = _dot(h, w3[:, lo:lo + width].astype(BF16))
        t = (a * _sigmoid(a) * b).astype(BF16)
        y = _dot(t, w2[lo:lo + width, :].astype(BF16))
        acc = y if acc is None else acc + y
        if after_chunk is not None:
            after_chunk(j)
    return acc


def _ffn_chunks(f):
    if f == sum(w for _, w in FFN_CHUNKS):
        return FFN_CHUNKS
    return ((0, f),)


def _expert_weight_copies(hbm, vmem, sems, chunks, expert, j):
    h1, h3, h2 = hbm
    v1, v3, v2 = vmem
    lo, w = chunks[j]
    return [pltpu.make_async_copy(h1.at[expert, :, lo:lo + w], v1.at[:, lo:lo + w], sems.at[j, 0]),
            pltpu.make_async_copy(h3.at[expert, :, lo:lo + w], v3.at[:, lo:lo + w], sems.at[j, 1]),
            pltpu.make_async_copy(h2.at[expert, lo:lo + w, :], v2.at[lo:lo + w, :], sems.at[j, 2])]


def _expert_ffn_kernel(brow_ref, bexp_ref, role_ref, x_ref, w1_hbm, w3_hbm, w2_hbm, o_ref, w1_v, w3_v, w2_v, sems,
                       *, chunks):
    i = pl.program_id(0)
    copies = functools.partial(_expert_weight_copies, (w1_hbm, w3_hbm, w2_hbm), (w1_v, w3_v, w2_v), sems, chunks)

    @pl.when(i == 0)
    def _():
        for j in range(len(chunks)):
            for cp in copies(bexp_ref[0], j):
                cp.start()

    def block(first, last):
        def before(j):
            for cp in copies(bexp_ref[i], j):
                cp.wait()

        def after(j):
            for cp in copies(bexp_ref[i + 1], j):
                cp.start()

        o_ref[...] = _swiglu_chunks(x_ref[...], w1_v, w3_v, w2_v, chunks, before if first else None,
                                    after if last else None).astype(o_ref.dtype)

    for role in range(4):
        pl.when(role_ref[i] == role)(functools.partial(block, bool(role & 1), bool(role & 2)))


def _expert_ffn(xb, w1, w3, w2, blk_row, blk_exp, role, max_blocks):
    rows, dm = xb.shape
    f = w1.shape[2]
    chunks = _ffn_chunks(f)
    row = pl.BlockSpec((MOE_BLK, dm), lambda i, br, be, ro: (br[i], 0))
    hbm = pl.BlockSpec(memory_space=pl.ANY)
    return pl.pallas_call(
        functools.partial(_expert_ffn_kernel, chunks=chunks),
        grid_spec=pltpu.PrefetchScalarGridSpec(
            num_scalar_prefetch=3, grid=(max_blocks,),
            in_specs=[row, hbm, hbm, hbm], out_specs=row,
            scratch_shapes=[pltpu.VMEM(w1.shape[1:], w1.dtype), pltpu.VMEM(w3.shape[1:], w3.dtype),
                            pltpu.VMEM(w2.shape[1:], w2.dtype), pltpu.SemaphoreType.DMA((len(chunks), 3))]),
        out_shape=jax.ShapeDtypeStruct((rows, dm), BF16),
        compiler_params=_cparams("arbitrary"),
        name="expert_ffn",
    )(blk_row, blk_exp, role, xb, w1, w3, w2)


def _route_tile(logits_tok, filled, ne):
    tm = logits_tok.shape[0]
    logits = logits_tok.T[0:ne, :]
    eidx = lax.broadcasted_iota(I32, (ne, tm), 0)
    m1 = jnp.max(logits, axis=0, keepdims=True)
    i1 = jnp.min(jnp.where(logits == m1, eidx, ne), axis=0, keepdims=True)
    sel1 = eidx == i1
    rest = jnp.where(sel1, -jnp.inf, logits)
    m2 = jnp.max(rest, axis=0, keepdims=True)
    i2 = jnp.min(jnp.where(rest == m2, eidx, ne), axis=0, keepdims=True)
    sel2 = eidx == i2
    e2 = jnp.exp(m2 - m1)
    den = 1.0 + e2
    rf = jnp.where(sel1 | sel2, 1.0, 0.0)
    cnt = jnp.sum(rf, axis=1, keepdims=True)
    before = (lax.broadcasted_iota(I32, (tm, tm), 0) < lax.broadcasted_iota(I32, (tm, tm), 1))
    rank = _dot(rf.astype(BF16), jnp.where(before, 1.0, 0.0).astype(BF16))
    whole = lambda a: jnp.floor(a * (1.0 / BF16_ROWS)) * BF16_ROWS
    phase = filled - whole(filled)
    span = jnp.where(cnt > 0, whole(phase + cnt + (BF16_ROWS - 1)), 0.0)
    ecol = lax.broadcasted_iota(I32, (ne, 1), 0)
    off = jnp.zeros((ne, 1), F32)
    for j in range(ne - 1):
        off = off + jnp.where(ecol > j, span[j:j + 1, :], 0.0)
    place = off + phase + rank
    pos1 = jnp.sum(jnp.where(sel1, place, 0.0), axis=0, keepdims=True)
    pos2 = jnp.sum(jnp.where(sel2, place, 0.0), axis=0, keepdims=True)
    lane_major = jnp.concatenate([pos1, pos2, 1.0 / den, e2 / den, jnp.zeros((4, tm), F32)], axis=0)
    tok_major = jnp.concatenate([lane_major, jnp.zeros((LANES - 8, tm), F32)], axis=0).T
    return lane_major, tok_major, cnt


def _route_kernel(lg_ref, lane_ref, tok_ref, cnt_ref, filled, *, ne):
    tm = MOE_TILE

    @pl.when(pl.program_id(0) == 0)
    def _():
        filled[...] = jnp.zeros_like(filled)

    for t in range(lg_ref.shape[0] // tm):
        lane_major, tok_major, cnt = _route_tile(lg_ref[t * tm:(t + 1) * tm, :], filled[...], ne)
        lane_ref[:, t * tm:(t + 1) * tm] = lane_major
        tok_ref[t * tm:(t + 1) * tm, :] = tok_major
        cnt_ref[t] = jnp.broadcast_to(cnt, (ne, LANES)).astype(I32)
        filled[...] = filled[...] + cnt


def _route(logits, ne):
    n = logits.shape[0]
    tm = MOE_TILE
    span = min(ROUTE_SPAN, n)
    return pl.pallas_call(
        functools.partial(_route_kernel, ne=ne),
        grid=(n // span,),
        in_specs=[pl.BlockSpec((span, LANES), lambda i: (i, 0))],
        out_specs=[pl.BlockSpec((8, span), lambda i: (0, i)), pl.BlockSpec((span, LANES), lambda i: (i, 0)),
                   pl.BlockSpec((span // tm, ne, LANES), lambda i: (i, 0, 0))],
        out_shape=[jax.ShapeDtypeStruct((8, n), F32), jax.ShapeDtypeStruct((n, LANES), F32),
                   jax.ShapeDtypeStruct((n // tm, ne, LANES), I32)],
        scratch_shapes=[pltpu.VMEM((ne, 1), F32)],
        compiler_params=_cparams("arbitrary"),
        name="moe_route",
    )(logits)


def _segment_copies(seg_ref, grouped_ref, base_ref, tile_buf, off_ref, sems, tile, slot, ne, *, to_grouped, wait):
    for e in range(ne):
        seg = seg_ref[tile * ne + e]
        far = base_ref[tile * ne + e]
        near = off_ref[tile * ne + e]
        for piece in SEG_PIECES:
            g = grouped_ref.at[pl.ds(pl.multiple_of(far, BF16_ROWS), piece)]
            t = tile_buf.at[slot, pl.ds(pl.multiple_of(near, BF16_ROWS), piece)]
            cp = pltpu.make_async_copy(t, g, sems.at[slot, e]) if to_grouped else \
                pltpu.make_async_copy(g, t, sems.at[slot, e])
            has = (seg & piece) != 0

            @pl.when(has)
            def _():
                cp.wait() if wait else cp.start()

            step = jnp.where(has, piece, 0)
            far = far + step
            near = near + step


def _tile_rows(ne):
    return TOP_K * MOE_TILE + ne * 2 * BF16_ROWS


def _tail_copies(end_ref, zeros, xb_ref, sems, ne):
    return [pltpu.make_async_copy(zeros, xb_ref.at[pl.ds(pl.multiple_of(end_ref[e], BF16_ROWS), MOE_BLK)], sems.at[e])
            for e in range(ne)]


def _dispatch_kernel(seg_ref, base_ref, off_ref, phase_ref, end_ref, rt_ref, h_ref, xb_ref, res, zeros, partial,
                     sems, tail_sems):
    i = pl.program_id(0)
    nt = pl.num_programs(0)
    ne = tail_sems.shape[0]
    tm = h_ref.shape[0]
    mrows = res.shape[1]
    slot = i % 2
    copies = functools.partial(_segment_copies, seg_ref, xb_ref, base_ref, res, off_ref, sems, ne=ne, to_grouped=True)

    @pl.when(i == 0)
    def _():
        zeros[...] = jnp.zeros_like(zeros)
        partial[...] = jnp.zeros_like(partial)
        for cp in _tail_copies(end_ref, zeros, xb_ref, tail_sems, ne):
            cp.start()

    row = lax.broadcasted_iota(I32, (mrows, tm), 0)
    pos = rt_ref[0:2, :].astype(I32)
    perm = jnp.where((row == pos[0:1, :]) | (row == pos[1:2, :]), 1.0, 0.0).astype(BF16)
    res[slot] = _dot(perm, h_ref[...]).astype(BF16)

    tile_row = lax.broadcasted_iota(I32, (BF16_ROWS, res.shape[2]), 0)
    for e in range(ne):
        @pl.when(seg_ref[i * ne + e] > 0)
        def _():
            first = pl.ds(pl.multiple_of(off_ref[i * ne + e], BF16_ROWS), BF16_ROWS)
            merged = jnp.where(tile_row < phase_ref[i * ne + e], partial[e].astype(F32), res[slot, first, :].astype(F32))
            res[slot, first, :] = merged.astype(BF16)
            last = pl.ds(pl.multiple_of(off_ref[i * ne + e] + seg_ref[i * ne + e] - BF16_ROWS, BF16_ROWS), BF16_ROWS)
            partial[e] = res[slot, last, :]

    @pl.when(i > 0)
    def _():
        copies(tile=i - 1, slot=1 - slot, wait=True)

    copies(tile=i, slot=slot, wait=False)

    @pl.when(i == nt - 1)
    def _():
        copies(tile=i, slot=slot, wait=True)
        for cp in _tail_copies(end_ref, zeros, xb_ref, tail_sems, ne):
            cp.wait()


def _dispatch(h, rt_lane, lay, ne):
    n, dm = h.shape
    tm = MOE_TILE
    return pl.pallas_call(
        _dispatch_kernel,
        grid_spec=pltpu.PrefetchScalarGridSpec(
            num_scalar_prefetch=5, grid=(n // tm,),
            in_specs=[pl.BlockSpec((8, tm), lambda i, *_: (0, i)), pl.BlockSpec((tm, dm), lambda i, *_: (i, 0))],
            out_specs=pl.BlockSpec(memory_space=pl.ANY),
            scratch_shapes=[pltpu.VMEM((2, _tile_rows(ne), dm), BF16), pltpu.VMEM((MOE_BLK, dm), BF16),
                            pltpu.VMEM((ne, BF16_ROWS, dm), BF16),
                            pltpu.SemaphoreType.DMA((2, ne)), pltpu.SemaphoreType.DMA((ne,))]),
        out_shape=jax.ShapeDtypeStruct((lay['rows_total'], dm), BF16),
        compiler_params=_cparams("arbitrary"),
        name="moe_dispatch",
    )(lay['seg'], lay['base'], lay['off'], lay['phase'], lay['end'], rt_lane, h)


def _combine_kernel(seg_ref, base_ref, off_ref, rt_ref, x_ref, gn_ref, yb_ref, o_ref, got, sems, *, ne):
    i = pl.program_id(0)
    nt = pl.num_programs(0)
    tm = x_ref.shape[0]
    mrows = got.shape[1]
    slot = i % 2
    copies = functools.partial(_segment_copies, seg_ref, yb_ref, base_ref, got, off_ref, sems, ne=ne, to_grouped=False)

    @pl.when(i == 0)
    def _():
        got[...] = jnp.zeros_like(got)
        copies(tile=i, slot=slot, wait=False)

    @pl.when(i + 1 < nt)
    def _():
        copies(tile=i + 1, slot=1 - slot, wait=False)

    copies(tile=i, slot=slot, wait=True)

    col = lax.broadcasted_iota(I32, (tm, mrows), 1)
    rt = rt_ref[...]
    rows = got[slot]
    sel = jnp.concatenate([jnp.where(col == rt[:, k:k + 1].astype(I32), 1.0, 0.0).astype(BF16) for k in range(TOP_K)],
                          axis=0)
    picked = _dot(sel, rows)
    y = sum(rt[:, TOP_K + k:TOP_K + k + 1] * picked[k * tm:(k + 1) * tm] for k in range(TOP_K))
    o_ref[...] = _rms(x_ref[...] + y, gn_ref[...])


def _combine(yb, rt_tok, x2, gn, lay, ne):
    n, dm = x2.shape
    tm = MOE_TILE
    mrows = _tile_rows(ne)
    tok = lambda w: pl.BlockSpec((tm, w), lambda i, *_: (i, 0))
    return pl.pallas_call(
        functools.partial(_combine_kernel, ne=ne),
        grid_spec=pltpu.PrefetchScalarGridSpec(
            num_scalar_prefetch=3, grid=(n // tm,),
            in_specs=[tok(LANES), tok(dm), pl.BlockSpec((1, dm), lambda i, *_: (0, 0)),
                      pl.BlockSpec(memory_space=pl.ANY)],
            out_specs=tok(dm),
            scratch_shapes=[pltpu.VMEM((2, mrows, dm), BF16), pltpu.SemaphoreType.DMA((2, ne))]),
        out_shape=jax.ShapeDtypeStruct((n, dm), F32),
        compiler_params=_cparams("arbitrary"),
        name="moe_combine",
    )(lay['seg'], lay['base'], lay['off'], rt_tok, x2, gn.reshape(1, dm), yb)


def _moe_layout(cnt, n):
    nt, ne = cnt.shape
    blk = MOE_BLK
    whole = lambda a: a // BF16_ROWS * BF16_ROWS
    cap = -(-(n + BF16_ROWS + blk) // blk) * blk
    region = jnp.arange(ne, dtype=I32) * cap
    filled = jnp.cumsum(cnt, axis=0) - cnt
    phase = filled - whole(filled)
    span = jnp.where(cnt > 0, whole(phase + cnt + BF16_ROWS - 1), 0)
    base = (region[None, :] + filled - phase).astype(I32)
    off = (jnp.cumsum(span, axis=1) - span).astype(I32)
    rows_e = jnp.sum(cnt, axis=0)
    nblk_e = (rows_e + blk - 1) // blk
    end = (region + whole(rows_e + BF16_ROWS - 1)).astype(I32)
    ends = jnp.cumsum(nblk_e)
    max_blocks = TOP_K * n // blk + ne
    bid = jnp.arange(max_blocks + 1, dtype=I32)
    bexp = jnp.minimum(jnp.sum((bid[:, None] >= ends[None, :]).astype(I32), axis=1), ne - 1)
    brow = bexp * (cap // blk) + bid - (ends - nblk_e)[bexp]
    nblk = ends[-1]
    last = jnp.maximum(nblk - 1, 0)
    used = bid < nblk
    first = bid == (ends - nblk_e)[bexp]
    hand_over = (bid == ends[bexp] - 1) & (bid < last)
    role = jnp.where(used, first.astype(I32) + 2 * hand_over.astype(I32), 4).astype(I32)
    bexp = jnp.where(used, bexp, bexp[last]).astype(I32)
    brow = jnp.where(used, brow, brow[last]).astype(I32)
    flat = lambda a: a.reshape(-1)
    return dict(seg=flat(span.astype(I32)), base=flat(base), off=flat(off), phase=flat(phase.astype(I32)), end=end,
                brow=brow, bexp=bexp, role=role, rows_total=ne * cap, max_blocks=max_blocks)


def _moe(x2, h, logits, gn, w1, w3, w2):
    n, dm = x2.shape
    ne = w1.shape[0]
    rt_lane, rt_tok, cnt = _route(logits, ne)
    lay = _moe_layout(cnt[:, :, 0], n)
    xb = _dispatch(h, rt_lane, lay, ne)
    yb = _expert_ffn(xb, w1, w3, w2, lay['brow'], lay['bexp'], lay['role'], lay['max_blocks'])
    return _combine(yb, rt_tok, x2, gn, lay, ne)


def kernel(x, norm_mix_g, w_in, attn_rel_bias, rg_conv_w, rg_conv_b, rg_wx, rg_bx, rg_wa, rg_ba, rg_lambda, s5_a_re, s5_a_im, s5_log_dt, s5_b_re, s5_b_im, s5_c_re, s5_c_im, s5_d, s5_w_glu, g_group, w_out, norm_ffn_g, ffn_w1, ffn_w3, ffn_w2, moe_router, moe_w1, moe_w3, moe_w2, final_norm_g):
    bsz, l, dm = x.shape
    depth = w_in.shape[0]
    assert depth == 2 and l % ATTN_TQ == 0 and ATTN_TQ == N_PREV * CHUNK, "dense layer, then the MoE layer"
    d_rg = rg_conv_w.shape[2]
    d_s5 = s5_w_glu.shape[1]
    d_attn = (w_in.shape[2] - 2 * d_rg - d_s5) // 3
    x2 = x.reshape(bsz * l, dm)
    attn_bias = jax.vmap(_attn_bias)(attn_rel_bias)
    rg_gates_w = jnp.concatenate([jax.vmap(_block_diag)(rg_wx), jax.vmap(_block_diag)(rg_wa)], axis=2).astype(BF16)
    rg_gates_b = jnp.concatenate([rg_bx, rg_ba], axis=1).astype(F32)[:, None, :]
    rg_decay = jax.nn.softplus(-rg_lambda.astype(F32))[:, None, :]
    s5_prm = jax.vmap(_s5_params)(s5_a_re, s5_a_im, s5_log_dt, s5_b_re, s5_b_im, s5_c_re, s5_c_im, s5_d)
    for layer in range(depth):
        q, k, vt, xr, gate, us = _in_proj(x2, norm_mix_g[layer], w_in, layer, d_attn, d_rg, d_s5)
        y_attn = _attention(q, k, vt, attn_bias, layer, bsz, l)
        y_rg = _rglru(xr, gate, rg_conv_w.astype(F32), rg_conv_b.astype(F32)[:, None, :], rg_gates_w, rg_gates_b,
                      rg_decay, layer, bsz, l)
        y_s5 = _s5(us, s5_prm, layer, bsz, l)
        mixed = (y_attn, y_rg, y_s5, x2, g_group[layer].astype(F32), s5_w_glu, w_out, layer,
                 norm_ffn_g[layer].astype(F32))
        if layer == 0:
            x2 = _out_proj(*mixed, ffn=(ffn_w1[0], ffn_w3[0], ffn_w2[0]))
        else:
            x2, h, logits = _out_proj(*mixed, router=moe_router[0])
            x2 = _moe(x2, h, logits, final_norm_g.astype(F32), moe_w1[0], moe_w3[0], moe_w2[0])
    return x2.reshape(bsz, l, dm)
```

```python
import functools
import math

import jax
import jax.numpy as jnp
import numpy as np
from jax import lax
from jax.experimental import pallas as pl
from jax.experimental.pallas import tpu as pltpu

F32 = jnp.float32
BF16 = jnp.bfloat16
I32 = jnp.int32

EPS = 1e-6
CHUNK = 64
N_PREV = 8
BAND = (N_PREV + 1) * CHUNK
REL_CLIP = 128
HEAD_DIM = 64
RG_C = 8.0
RG_CONV_WIDTH = 4
S5_GROUP_DIM = 16
S5_T = 16
TOP_K = 2
NEG_BIG = -1e30

LANES = 128
SUBLANES = 8
BF16_ROWS = 16
VMEM_LIMIT = 52 * 1024 * 1024
MIX_VMEM_LIMIT = 58 * 1024 * 1024

ROW_TILE = 512
PROJ_TILE = 1024
ATTN_TQ = 512
ATTN_SPAN = 2 * CHUNK
ATTN_WIN = N_PREV * CHUNK + ATTN_SPAN
RG_TL = 1024
MOE_TILE = 256
MOE_BLK = 512
ROUTE_SPAN = 2048
SEG_PIECES = tuple(BF16_ROWS << b for b in range((MOE_TILE // BF16_ROWS).bit_length() - 1, -1, -1))
FFN_CHUNKS = ((0, 512), (512, 512), (1024, 512), (1536, 512), (2048, 512), (2560, 256))


def _cparams(*sem):
    return pltpu.CompilerParams(dimension_semantics=sem, vmem_limit_bytes=VMEM_LIMIT)


def _const_spec(shape):
    nd = len(shape)
    return pl.BlockSpec(shape, lambda *_: (0,) * nd, pipeline_mode=pl.Buffered(1))


def _rms(xf, g):
    var = jnp.mean(xf * xf, axis=-1, keepdims=True)
    return xf * lax.rsqrt(var + EPS) * g


def _sigmoid(x):
    return 1.0 / (1.0 + jnp.exp(-x))


def _gelu(x):
    c = math.sqrt(2.0 / math.pi)
    return 0.5 * x * (1.0 + jnp.tanh(c * (x + 0.044715 * (x * x * x))))


def _dot(a, b):
    return jnp.dot(a, b, preferred_element_type=F32)


def _in_proj_kernel(x_ref, g_ref, w_ref, wvt_ref, q_ref, k_ref, vt_ref, xr_ref, gt_ref, us_ref, *, d_attn, d_rg):
    u = _rms(x_ref[...], g_ref[...]).astype(BF16)
    lo = 0
    for ref, width in ((q_ref, d_attn), (k_ref, d_attn), (None, d_attn),
                       (xr_ref, d_rg), (gt_ref, d_rg), (us_ref, w_ref.shape[1] - 3 * d_attn - 2 * d_rg)):
        if ref is not None:
            ref[...] = _dot(u, w_ref[:, lo:lo + width].astype(BF16)).astype(ref.dtype)
        lo += width
    vt_ref[...] = lax.dot_general(wvt_ref[...].astype(BF16), u, (((1,), (1,)), ((), ())),
                                  preferred_element_type=F32).astype(vt_ref.dtype)


def _layer_spec(stacked, layer):
    rest = stacked.shape[1:]
    return pl.BlockSpec((None,) + rest, lambda *_: (layer,) + (0,) * len(rest), pipeline_mode=pl.Buffered(1))


def _in_proj(x2, g, w_all, layer, d_attn, d_rg, d_s5):
    n, dm = x2.shape
    tm = min(PROJ_TILE, n)
    row = lambda w: pl.BlockSpec((tm, w), lambda i: (i, 0))
    wv_t = w_all[layer, :, 2 * d_attn:3 * d_attn].T
    return pl.pallas_call(
        functools.partial(_in_proj_kernel, d_attn=d_attn, d_rg=d_rg),
        grid=(n // tm,),
        in_specs=[row(dm), _const_spec((1, dm)), _layer_spec(w_all, layer), _const_spec(wv_t.shape)],
        out_specs=[row(d_attn), row(d_attn), pl.BlockSpec((d_attn, tm), lambda i: (0, i)),
                   row(d_rg), row(d_rg), row(d_s5)],
        out_shape=[jax.ShapeDtypeStruct((n, d_attn), BF16)] * 2 + [jax.ShapeDtypeStruct((d_attn, n), BF16)]
        + [jax.ShapeDtypeStruct((n, d_rg), F32)] * 2 + [jax.ShapeDtypeStruct((n, d_s5), F32)],
        compiler_params=_cparams("parallel"),
        name="in_proj",
    )(x2, g.reshape(1, dm), w_all, wv_t)


def _attn_kernel(q_ref, kp_ref, kc_ref, vp_ref, vc_ref, bias_ref, o_ref, kz, sc, *, n_pairs):
    qi = pl.program_id(1)
    tq = q_ref.shape[0]
    kz[0:tq, :] = kp_ref[...]
    kz[tq:2 * tq, :] = kc_ref[...]
    first = lax.broadcasted_iota(I32, (ATTN_SPAN, LANES), 1) < HEAD_DIM
    pairs = [slice(hp * LANES, (hp + 1) * LANES) for hp in range(n_pairs)]

    def score(j, masked):
        r0 = j * ATTN_SPAN
        for hp, cols in enumerate(pairs):
            q2 = q_ref[pl.ds(r0, ATTN_SPAN), cols] * jnp.asarray(HEAD_DIM ** -0.5, BF16)
            zero = jnp.zeros_like(q2)
            qq = jnp.concatenate([jnp.where(first, q2, zero), jnp.where(first, zero, q2)], axis=0)
            s = lax.dot_general(kz[pl.ds(r0, ATTN_WIN), cols], qq, (((1,), (1,)), ((), ())),
                                preferred_element_type=F32) + bias_ref[hp]
            if masked:
                key = lax.broadcasted_iota(I32, (ATTN_WIN, 2 * ATTN_SPAN), 0)
                s = jnp.where(key >= tq - r0, s, NEG_BIG)
            sc[j % 2, hp] = s

    def finish(j):
        r0 = j * ATTN_SPAN
        old = tq - r0
        probs = []
        for hp in range(n_pairs):
            s = sc[j % 2, hp]
            probs.append(jnp.exp((s - jnp.max(s, axis=0, keepdims=True)).astype(BF16)))
        for hp, (cols, p) in enumerate(zip(pairs, probs)):
            ones = lambda w: jnp.ones((BF16_ROWS, w), BF16)
            ov = (_dot(jnp.concatenate([vp_ref[cols, r0:tq], ones(old)], axis=0), p[0:old])
                  + _dot(jnp.concatenate([vc_ref[cols, 0:ATTN_WIN - old], ones(ATTN_WIN - old)], axis=0), p[old:]))
            o2 = (ov[0:LANES] / ov[LANES:LANES + 1]).T
            o_ref[pl.ds(r0, ATTN_SPAN), cols] = jnp.where(first, o2[0:ATTN_SPAN], o2[ATTN_SPAN:]).astype(o_ref.dtype)

    def run(masked):
        groups = tq // ATTN_SPAN
        score(0, masked)
        for j in range(groups):
            if j + 1 < groups:
                score(j + 1, masked)
            finish(j)

    pl.when(qi == 0)(functools.partial(run, True))
    pl.when(qi != 0)(functools.partial(run, False))


def _attention(q, k, vt, bias_t, layer, bsz, l):
    n, da = q.shape
    tq = ATTN_TQ
    nt = l // tq
    n_pairs = da // LANES
    cur = pl.BlockSpec((tq, da), lambda b, i: (b * nt + i, 0))
    prev = pl.BlockSpec((tq, da), lambda b, i: (b * nt + jnp.maximum(i - 1, 0), 0))
    cur_t = pl.BlockSpec((da, tq), lambda b, i: (0, b * nt + i))
    prev_t = pl.BlockSpec((da, tq), lambda b, i: (0, b * nt + jnp.maximum(i - 1, 0)))
    return pl.pallas_call(
        functools.partial(_attn_kernel, n_pairs=n_pairs),
        grid=(bsz, nt),
        in_specs=[cur, prev, cur, prev_t, cur_t, _layer_spec(bias_t, layer)],
        out_specs=cur,
        out_shape=jax.ShapeDtypeStruct((n, da), BF16),
        scratch_shapes=[pltpu.VMEM((2 * tq, da), BF16), pltpu.VMEM((2, n_pairs, ATTN_WIN, 2 * ATTN_SPAN), F32)],
        compiler_params=_cparams("parallel", "parallel"),
        name="chunk_attn",
    )(q, k, k, vt, vt, bias_t)


def _attn_bias(rel_bias):
    h = rel_bias.shape[0]
    tab = rel_bias.astype(F32)
    n_far = N_PREV * CHUNK - REL_CLIP + CHUNK
    lo = 2 * REL_CLIP - (BAND + CHUNK - 1 - n_far)
    ext = jnp.concatenate([jnp.broadcast_to(tab[:, 2 * REL_CLIP:], (h, n_far)), tab[:, lo:2 * REL_CLIP][:, ::-1]], axis=1)
    wide = BAND + CHUNK
    ring = jnp.concatenate([ext[:, CHUNK - 1:], jnp.zeros((h, 1), F32), ext[:, :CHUNK - 1]], axis=1)
    bias = jnp.broadcast_to(ring[:, None, :], (h, CHUNK, wide)).reshape(h, CHUNK * wide)
    bias = bias[:, :CHUNK * (wide - 1)].reshape(h, CHUNK, wide - 1)[..., :BAND]
    band_t = bias.transpose(0, 2, 1)
    groups = ATTN_SPAN // CHUNK
    per_chunk = [jnp.pad(band_t, ((0, 0), (c * CHUNK, (groups - 1 - c) * CHUNK), (0, 0)), constant_values=NEG_BIG)
                 for c in range(groups)]
    win = jnp.stack(per_chunk, axis=2)
    return win.reshape(h // 2, 2, ATTN_WIN, ATTN_SPAN).transpose(0, 2, 1, 3).reshape(h // 2, ATTN_WIN, 2 * ATTN_SPAN)


def _rglru_kernel(x_ref, gt_ref, cw_ref, cb_ref, w_ref, b_ref, sp_ref, o_ref, xpad, a_s, b_s, hcar):
    t = pl.program_id(1)
    tl, c = x_ref.shape
    front = SUBLANES

    @pl.when(t == 0)
    def _():
        xpad[0:front, :] = jnp.zeros((front, c), F32)
        hcar[...] = jnp.zeros_like(hcar)

    xpad[front:front + tl, :] = x_ref[...]
    xc = cb_ref[...] + sum(
        cw_ref[j:j + 1, :] * xpad[front - (RG_CONV_WIDTH - 1) + j:front - (RG_CONV_WIDTH - 1) + j + tl, :]
        for j in range(RG_CONV_WIDTH))
    xpad[0:front, :] = xpad[tl:tl + front, :]
    pre = _dot(xc.astype(BF16), w_ref[...]) + b_ref[...]
    gx = _sigmoid(pre[:, 0:c])
    ga = _sigmoid(pre[:, c:2 * c])
    log_a = -RG_C * ga * sp_ref[...]
    a = jnp.exp(log_a)
    mult = jnp.sqrt(-jnp.tanh(log_a) * (a * a + 1.0))
    b = mult * gx * xc
    nh = c // LANES
    for j in range(nh):
        a_s[j] = a[:, j * LANES:(j + 1) * LANES]
        b_s[j] = b[:, j * LANES:(j + 1) * LANES]

    step_in_vreg = lax.broadcasted_iota(I32, (SUBLANES, LANES), 0)

    def vreg_scan(r, h_in):
        rows = pl.ds(pl.multiple_of(r * SUBLANES, SUBLANES), SUBLANES)
        out = []
        for j in range(nh):
            av, bv = a_s[j, rows, :], b_s[j, rows, :]
            for d in (1, 2, 4):
                seen = step_in_vreg >= d
                a_prev = jnp.where(seen, pltpu.roll(av, d, 0), 1.0)
                b_prev = jnp.where(seen, pltpu.roll(bv, d, 0), 0.0)
                bv = bv + av * b_prev
                av = av * a_prev
            h = bv + av * h_in[j]
            b_s[j, rows, :] = h
            out.append(h[SUBLANES - 1:SUBLANES, :])
        return tuple(out)

    h_last = lax.fori_loop(0, tl // SUBLANES, vreg_scan,
                           tuple(hcar[:, j * LANES:(j + 1) * LANES] for j in range(nh)), unroll=4)
    for j in range(nh):
        cols = slice(j * LANES, (j + 1) * LANES)
        hcar[:, cols] = h_last[j]
        o_ref[:, cols] = (b_s[j] * _gelu(gt_ref[:, cols])).astype(o_ref.dtype)


def _rglru(xr, gate, conv_w, conv_b, w_bd, b_cat, sp, layer, bsz, l):
    n, c = xr.shape
    tl = min(RG_TL, l)
    nt = l // tl
    row = pl.BlockSpec((tl, c), lambda b, t: (b * nt + t, 0))
    return pl.pallas_call(
        _rglru_kernel,
        grid=(bsz, nt),
        in_specs=[row, row] + [_layer_spec(a, layer) for a in (conv_w, conv_b, w_bd, b_cat, sp)],
        out_specs=row,
        out_shape=jax.ShapeDtypeStruct((n, c), BF16),
        scratch_shapes=[pltpu.VMEM((tl + SUBLANES, c), F32), pltpu.VMEM((c // LANES, tl, LANES), F32),
                        pltpu.VMEM((c // LANES, tl, LANES), F32), pltpu.VMEM((1, c), F32)],
        compiler_params=_cparams("parallel", "arbitrary"),
        name="rglru",
    )(xr, gate, conv_w, conv_b, w_bd, b_cat, sp)


def _block_diag(w):
    nb, d, _ = w.shape
    eye = jnp.eye(nb, dtype=w.dtype)
    return (eye[:, None, :, None] * w[:, :, None, :]).reshape(nb * d, nb * d)


def _s5_kernel(u_ref, m_ref, wab_ref, wo_ref, c1_ref, c2a_ref, c2b_ref, d_ref, y_ref, ps, ug, xa_s, xb_s):
    t, gd = S5_T, S5_GROUP_DIM
    ng, nk, _ = ug.shape
    nh = ps.shape[0]
    per_half = LANES // gd
    rows8 = SUBLANES

    for h in range(nh):
        ps[h] = u_ref[:, h * LANES:(h + 1) * LANES]

    lane_block = lax.broadcasted_iota(I32, (rows8, LANES), 1) // gd

    def block_transpose(vs):
        d = per_half // 2
        while d:
            keep = (lane_block & d) == 0
            out = list(vs)
            for i in range(per_half):
                if i & d == 0:
                    out[i] = jnp.where(keep, vs[i], pltpu.roll(vs[i + d], d * gd, 1))
                    out[i + d] = jnp.where(keep, pltpu.roll(vs[i], LANES - d * gd, 1), vs[i + d])
            vs, d = out, d // 2
        return vs

    def to_groups(r, carry):
        base = pl.multiple_of(r * rows8 * t, rows8 * t)
        rows = pl.ds(pl.multiple_of(r * rows8, rows8), rows8)
        for h in range(nh):
            for j in range(t // per_half):
                steps = [ps[h, pl.ds(base + j * per_half + i, rows8, stride=t), :] for i in range(per_half)]
                for k, v in enumerate(block_transpose(steps)):
                    ug[h * per_half + k, rows, j * LANES:(j + 1) * LANES] = v
        return carry

    lax.fori_loop(0, nk // rows8, to_groups, 0, unroll=2)

    def project(g, carry):
        u = ug[g]
        ub = u.astype(BF16)
        rows = pl.ds(pl.multiple_of(g * nk, nk), nk)
        xab = _dot(ub, wab_ref[g])
        xa_s[rows, :] = xab[:, 0:LANES]
        xb_s[rows, :] = xab[:, LANES:2 * LANES]
        ug[g] = _dot(ub, m_ref[g]) + d_ref[g] * u
        return carry

    lax.fori_loop(0, ng, project, 0, unroll=2)

    c1, c2a, c2b = c1_ref[...], c2a_ref[...], c2b_ref[...]

    def step(k, carry):
        xa, xb = carry
        rows = pl.ds(k, ng, stride=nk)
        ia = xa_s[rows, :]
        ib = xb_s[rows, :]
        xa_s[rows, :] = xa
        return c1 * xa + c2a * xb + ia, c1 * xb + c2b * xa + ib

    z = jnp.zeros((ng, xa_s.shape[1]), F32)
    lax.fori_loop(0, nk, step, (z, z), unroll=4)

    def respond(g, carry):
        rows = pl.ds(pl.multiple_of(g * nk, nk), nk)
        ug[g] = _gelu(ug[g] + _dot(xa_s[rows, :].astype(BF16), wo_ref[g]))
        return carry

    lax.fori_loop(0, ng, respond, 0, unroll=2)

    def from_groups(r, carry):
        base = pl.multiple_of(r * rows8 * t, rows8 * t)
        rows = pl.ds(pl.multiple_of(r * rows8, rows8), rows8)
        for h in range(nh):
            for j in range(t // per_half):
                groups = [ug[h * per_half + k, rows, j * LANES:(j + 1) * LANES] for k in range(per_half)]
                for i, v in enumerate(block_transpose(groups)):
                    ps[h, pl.ds(base + j * per_half + i, rows8, stride=t), :] = v
        return carry

    lax.fori_loop(0, nk // rows8, from_groups, 0, unroll=2)
    for h in range(nh):
        y_ref[:, h * LANES:(h + 1) * LANES] = ps[h].astype(y_ref.dtype)


def _s5(us, prm, layer, bsz, l):
    n, w = us.shape
    ng = w // S5_GROUP_DIM
    nk = l // S5_T
    p2 = prm[2].shape[-2]
    row = pl.BlockSpec((l, w), lambda b: (b, 0))
    return pl.pallas_call(
        _s5_kernel,
        grid=(bsz,),
        in_specs=[row] + [_layer_spec(a, layer) for a in prm],
        out_specs=row,
        out_shape=jax.ShapeDtypeStruct((n, w), BF16),
        scratch_shapes=[pltpu.VMEM((w // LANES, l, LANES), F32), pltpu.VMEM((ng, nk, S5_T * S5_GROUP_DIM), F32),
                        pltpu.VMEM((ng * nk, p2), F32), pltpu.VMEM((ng * nk, p2), F32)],
        compiler_params=_cparams("parallel"),
        name="s5",
    )(us, *prm)


def _s5_params(a_re, a_im, log_dt, b_re, b_im, c_re, c_im, d):
    t = S5_T
    g, p = a_re.shape
    gd = S5_GROUP_DIM
    hi = lax.Precision.HIGHEST
    a = lax.complex(a_re.astype(F32), a_im.astype(F32))
    dt = jnp.exp(log_dt.astype(F32))[:, None]
    a_bar = jnp.exp(a * dt)
    bm = lax.complex(b_re.astype(F32), b_im.astype(F32))
    cm = lax.complex(c_re.astype(F32), c_im.astype(F32))
    b_bar = ((a_bar - 1.0) / a)[..., None] * bm
    steps = jnp.arange(t + 1, dtype=F32)
    pw = jnp.exp((a * dt)[:, None, :] * steps[None, :, None])

    cp = (cm[:, None, :, :] * pw[:, :, None, :]).transpose(0, 3, 1, 2)
    w = t * gd
    lag = cp[:, :, :t].reshape(g, p, w)
    k2 = jnp.einsum('gpc,gpx->gcx', jnp.concatenate([b_bar.real, -b_bar.imag], axis=1),
                    jnp.concatenate([lag.real, lag.imag], axis=1), precision=hi)
    shift = np.zeros((t, w, w), np.float32)
    for s in range(t):
        shift[s, np.arange(w - s * gd), np.arange(w - s * gd) + s * gd] = 1.0
    m = jnp.einsum('gcx,sxl->gscl', k2, shift, precision=hi).reshape(g, w, w)
    win = pw[:, :t][:, ::-1][:, :, :, None] * b_bar[:, None]
    win = win.transpose(0, 1, 3, 2).reshape(g, w, p)
    wab = jnp.concatenate([win.real, win.imag, win.imag, win.real], axis=-1)
    co = cp[:, :, 1:].reshape(g, p, w)
    wo = jnp.concatenate([co.real, -co.imag], axis=1)
    lt = pw[:, t]
    c1 = jnp.concatenate([lt.real, lt.real], axis=-1)
    c2a = jnp.concatenate([-lt.imag, lt.imag], axis=-1)
    c2b = jnp.concatenate([lt.imag, -lt.imag], axis=-1)
    dtile = jnp.tile(d.astype(F32).reshape(g, 1, gd), (1, t, 1)).reshape(g, 1, t * gd)
    return (m.astype(BF16), wab.astype(BF16), wo.astype(BF16), c1, c2a, c2b, dtile)


def _out_proj_kernel(ya_ref, yr_ref, ys_ref, x_ref, gg_ref, wglu_ref, wo_ref, gf_ref, *rest, dense, chunks):
    ys = ys_ref[...]
    ys = ys.astype(F32) * _sigmoid(_dot(ys.astype(BF16), wglu_ref[...].astype(BF16)))
    acc = x_ref[...]
    lo = 0
    for y in (ya_ref[...].astype(F32), yr_ref[...].astype(F32), ys):
        w = y.shape[1]
        acc = acc + _dot(_rms(y, gg_ref[:, lo:lo + w]).astype(BF16), wo_ref[lo:lo + w, :].astype(BF16))
        lo += w
    h = _rms(acc, gf_ref[...])
    hb = h.astype(BF16)
    if dense:
        w1_ref, w3_ref, w2_ref, xo_ref = rest
        xo_ref[...] = acc + _swiglu_chunks(hb, w1_ref, w3_ref, w2_ref, chunks)
        return
    rt_ref, xo_ref, h_ref, lg_ref = rest
    xo_ref[...] = acc
    h_ref[...] = hb
    both = _dot(hb, rt_ref[...])
    h_lo = (h - hb.astype(F32)).astype(BF16)
    lg_ref[...] = both[:, 0:LANES] + both[:, LANES:2 * LANES] + _dot(h_lo, rt_ref[:, 0:LANES])


def _out_proj(ya, yr, ys, x2, gg, wglu_all, wo_all, layer, gf, *, ffn=None, router=None):
    n, dm = x2.shape
    tm = min(ROW_TILE, n)
    row = lambda w: pl.BlockSpec((tm, w), lambda i: (i, 0))
    ins = [ya, yr, ys, x2, gg.reshape(1, -1), wglu_all, wo_all, gf.reshape(1, dm)]
    in_specs = [row(ya.shape[1]), row(yr.shape[1]), row(ys.shape[1]), row(dm), _const_spec((1, gg.shape[0])),
                _layer_spec(wglu_all, layer), _layer_spec(wo_all, layer), _const_spec((1, dm))]
    if ffn is not None:
        ins += list(ffn)
        in_specs += [_const_spec(w.shape) for w in ffn]
        out_specs = row(dm)
        out_shape = jax.ShapeDtypeStruct((n, dm), F32)
        chunks = _ffn_chunks(ffn[0].shape[1])
    else:
        ne = router.shape[1]
        r_hi = router.astype(BF16)
        r_lo = (router.astype(F32) - r_hi.astype(F32)).astype(BF16)
        pad = lambda a: jnp.pad(a, ((0, 0), (0, LANES - ne)))
        ins.append(jnp.concatenate([pad(r_hi), pad(r_lo)], axis=1))
        in_specs.append(_const_spec((dm, 2 * LANES)))
        out_specs = [row(dm), row(dm), row(LANES)]
        out_shape = [jax.ShapeDtypeStruct((n, dm), F32), jax.ShapeDtypeStruct((n, dm), BF16),
                     jax.ShapeDtypeStruct((n, LANES), F32)]
        chunks = None
    return pl.pallas_call(
        functools.partial(_out_proj_kernel, dense=ffn is not None, chunks=chunks),
        grid=(n // tm,),
        in_specs=in_specs, out_specs=out_specs, out_shape=out_shape,
        compiler_params=pltpu.CompilerParams(dimension_semantics=("parallel",), vmem_limit_bytes=MIX_VMEM_LIMIT),
        name="out_proj",
    )(*ins)


def _swiglu_chunks(h, w1, w3, w2, chunks, before_chunk=None, after_chunk=None):
    acc = None
    for j, (lo, width) in enumerate(chunks):
        if before_chunk is not None:
            before_chunk(j)
        a = _dot(h, w1[:, lo:lo + width].astype(BF16))
        b = _dot(h, w3[:, lo:lo + width].astype(BF16))
        t = (a * _sigmoid(a) * b).astype(BF16)
        y = _dot(t, w2[lo:lo + width, :].astype(BF16))
        acc = y if acc is None else acc + y
        if after_chunk is not None:
            after_chunk(j)
    return acc


def _ffn_chunks(f):
    if f == sum(w for _, w in FFN_CHUNKS):
        return FFN_CHUNKS
    return ((0, f),)


def _expert_weight_copies(hbm, vmem, sems, chunks, expert, j):
    h1, h3, h2 = hbm
    v1, v3, v2 = vmem
    lo, w = chunks[j]
    return [pltpu.make_async_copy(h1.at[expert, :, lo:lo + w], v1.at[:, lo:lo + w], sems.at[j, 0]),
            pltpu.make_async_copy(h3.at[expert, :, lo:lo + w], v3.at[:, lo:lo + w], sems.at[j, 1]),
            pltpu.make_async_copy(h2.at[expert, lo:lo + w, :], v2.at[lo:lo + w, :], sems.at[j, 2])]


def _expert_ffn_kernel(brow_ref, bexp_ref, role_ref, x_ref, w1_hbm, w3_hbm, w2_hbm, o_ref, w1_v, w3_v, w2_v, sems,
                       *, chunks):
    i = pl.program_id(0)
    copies = functools.partial(_expert_weight_copies, (w1_hbm, w3_hbm, w2_hbm), (w1_v, w3_v, w2_v), sems, chunks)

    @pl.when(i == 0)
    def _():
        for j in range(len(chunks)):
            for cp in copies(bexp_ref[0], j):
                cp.start()

    def block(first, last):
        def before(j):
            for cp in copies(bexp_ref[i], j):
                cp.wait()

        def after(j):
            for cp in copies(bexp_ref[i + 1], j):
                cp.start()

        o_ref[...] = _swiglu_chunks(x_ref[...], w1_v, w3_v, w2_v, chunks, before if first else None,
                                    after if last else None).astype(o_ref.dtype)

    for role in range(4):
        pl.when(role_ref[i] == role)(functools.partial(block, bool(role & 1), bool(role & 2)))


def _expert_ffn(xb, w1, w3, w2, blk_row, blk_exp, role, max_blocks):
    rows, dm = xb.shape
    f = w1.shape[2]
    chunks = _ffn_chunks(f)
    row = pl.BlockSpec((MOE_BLK, dm), lambda i, br, be, ro: (br[i], 0))
    hbm = pl.BlockSpec(memory_space=pl.ANY)
    return pl.pallas_call(
        functools.partial(_expert_ffn_kernel, chunks=chunks),
        grid_spec=pltpu.PrefetchScalarGridSpec(
            num_scalar_prefetch=3, grid=(max_blocks,),
            in_specs=[row, hbm, hbm, hbm], out_specs=row,
            scratch_shapes=[pltpu.VMEM(w1.shape[1:], w1.dtype), pltpu.VMEM(w3.shape[1:], w3.dtype),
                            pltpu.VMEM(w2.shape[1:], w2.dtype), pltpu.SemaphoreType.DMA((len(chunks), 3))]),
        out_shape=jax.ShapeDtypeStruct((rows, dm), BF16),
        compiler_params=_cparams("arbitrary"),
        name="expert_ffn",
    )(blk_row, blk_exp, role, xb, w1, w3, w2)


def _route_tile(logits_tok, filled, ne):
    tm = logits_tok.shape[0]
    logits = logits_tok.T[0:ne, :]
    eidx = lax.broadcasted_iota(I32, (ne, tm), 0)
    m1 = jnp.max(logits, axis=0, keepdims=True)
    i1 = jnp.min(jnp.where(logits == m1, eidx, ne), axis=0, keepdims=True)
    sel1 = eidx == i1
    rest = jnp.where(sel1, -jnp.inf, logits)
    m2 = jnp.max(rest, axis=0, keepdims=True)
    i2 = jnp.min(jnp.where(rest == m2, eidx, ne), axis=0, keepdims=True)
    sel2 = eidx == i2
    e2 = jnp.exp(m2 - m1)
    den = 1.0 + e2
    rf = jnp.where(sel1 | sel2, 1.0, 0.0)
    cnt = jnp.sum(rf, axis=1, keepdims=True)
    before = (lax.broadcasted_iota(I32, (tm, tm), 0) < lax.broadcasted_iota(I32, (tm, tm), 1))
    rank = _dot(rf.astype(BF16), jnp.where(before, 1.0, 0.0).astype(BF16))
    whole = lambda a: jnp.floor(a * (1.0 / BF16_ROWS)) * BF16_ROWS
    phase = filled - whole(filled)
    span = jnp.where(cnt > 0, whole(phase + cnt + (BF16_ROWS - 1)), 0.0)
    ecol = lax.broadcasted_iota(I32, (ne, 1), 0)
    off = jnp.zeros((ne, 1), F32)
    for j in range(ne - 1):
        off = off + jnp.where(ecol > j, span[j:j + 1, :], 0.0)
    place = off + phase + rank
    pos1 = jnp.sum(jnp.where(sel1, place, 0.0), axis=0, keepdims=True)
    pos2 = jnp.sum(jnp.where(sel2, place, 0.0), axis=0, keepdims=True)
    record = [pos1, pos2, 1.0 / den, e2 / den]
    lane_major = jnp.concatenate(record + [jnp.zeros((SUBLANES - len(record), tm), F32)], axis=0)
    tok_major = jnp.concatenate([lane_major, jnp.zeros((LANES - SUBLANES, tm), F32)], axis=0).T
    return lane_major, tok_major, cnt


def _route_kernel(lg_ref, lane_ref, tok_ref, cnt_ref, filled, *, ne):
    tm = MOE_TILE

    @pl.when(pl.program_id(0) == 0)
    def _():
        filled[...] = jnp.zeros_like(filled)

    for t in range(lg_ref.shape[0] // tm):
        lane_major, tok_major, cnt = _route_tile(lg_ref[t * tm:(t + 1) * tm, :], filled[...], ne)
        lane_ref[:, t * tm:(t + 1) * tm] = lane_major
        tok_ref[t * tm:(t + 1) * tm, :] = tok_major
        cnt_ref[t] = jnp.broadcast_to(cnt, (ne, LANES)).astype(I32)
        filled[...] = filled[...] + cnt


def _route(logits, ne):
    n = logits.shape[0]
    tm = MOE_TILE
    span = min(ROUTE_SPAN, n)
    return pl.pallas_call(
        functools.partial(_route_kernel, ne=ne),
        grid=(n // span,),
        in_specs=[pl.BlockSpec((span, LANES), lambda i: (i, 0))],
        out_specs=[pl.BlockSpec((SUBLANES, span), lambda i: (0, i)), pl.BlockSpec((span, LANES), lambda i: (i, 0)),
                   pl.BlockSpec((span // tm, ne, LANES), lambda i: (i, 0, 0))],
        out_shape=[jax.ShapeDtypeStruct((SUBLANES, n), F32), jax.ShapeDtypeStruct((n, LANES), F32),
                   jax.ShapeDtypeStruct((n // tm, ne, LANES), I32)],
        scratch_shapes=[pltpu.VMEM((ne, 1), F32)],
        compiler_params=_cparams("arbitrary"),
        name="moe_route",
    )(logits)


def _segment_copies(seg_ref, grouped_ref, base_ref, tile_buf, off_ref, sems, tile, slot, ne, *, to_grouped, wait):
    for e in range(ne):
        seg = seg_ref[tile * ne + e]
        far = base_ref[tile * ne + e]
        near = off_ref[tile * ne + e]
        for piece in SEG_PIECES:
            g = grouped_ref.at[pl.ds(pl.multiple_of(far, BF16_ROWS), piece)]
            t = tile_buf.at[slot, pl.ds(pl.multiple_of(near, BF16_ROWS), piece)]
            cp = pltpu.make_async_copy(t, g, sems.at[slot, e]) if to_grouped else \
                pltpu.make_async_copy(g, t, sems.at[slot, e])
            has = (seg & piece) != 0

            @pl.when(has)
            def _():
                cp.wait() if wait else cp.start()

            step = jnp.where(has, piece, 0)
            far = far + step
            near = near + step


def _tile_rows(ne):
    return TOP_K * MOE_TILE + ne * 2 * BF16_ROWS


def _tail_copies(end_ref, zeros, xb_ref, sems, ne):
    return [pltpu.make_async_copy(zeros, xb_ref.at[pl.ds(pl.multiple_of(end_ref[e], BF16_ROWS), MOE_BLK)], sems.at[e])
            for e in range(ne)]


def _dispatch_kernel(seg_ref, base_ref, off_ref, phase_ref, end_ref, rt_ref, h_ref, xb_ref, res, zeros, partial,
                     sems, tail_sems):
    i = pl.program_id(0)
    nt = pl.num_programs(0)
    ne = tail_sems.shape[0]
    tm = h_ref.shape[0]
    mrows = res.shape[1]
    slot = i % 2
    copies = functools.partial(_segment_copies, seg_ref, xb_ref, base_ref, res, off_ref, sems, ne=ne, to_grouped=True)

    @pl.when(i == 0)
    def _():
        zeros[...] = jnp.zeros_like(zeros)
        partial[...] = jnp.zeros_like(partial)
        for cp in _tail_copies(end_ref, zeros, xb_ref, tail_sems, ne):
            cp.start()

    row = lax.broadcasted_iota(I32, (mrows, tm), 0)
    pos = rt_ref[0:2, :].astype(I32)
    perm = jnp.where((row == pos[0:1, :]) | (row == pos[1:2, :]), 1.0, 0.0).astype(BF16)
    res[slot] = _dot(perm, h_ref[...]).astype(BF16)

    tile_row = lax.broadcasted_iota(I32, (BF16_ROWS, res.shape[2]), 0)
    for e in range(ne):
        @pl.when(seg_ref[i * ne + e] > 0)
        def _():
            first = pl.ds(pl.multiple_of(off_ref[i * ne + e], BF16_ROWS), BF16_ROWS)
            merged = jnp.where(tile_row < phase_ref[i * ne + e], partial[e].astype(F32), res[slot, first, :].astype(F32))
            res[slot, first, :] = merged.astype(BF16)
            last = pl.ds(pl.multiple_of(off_ref[i * ne + e] + seg_ref[i * ne + e] - BF16_ROWS, BF16_ROWS), BF16_ROWS)
            partial[e] = res[slot, last, :]

    @pl.when(i > 0)
    def _():
        copies(tile=i - 1, slot=1 - slot, wait=True)

    copies(tile=i, slot=slot, wait=False)

    @pl.when(i == nt - 1)
    def _():
        copies(tile=i, slot=slot, wait=True)
        for cp in _tail_copies(end_ref, zeros, xb_ref, tail_sems, ne):
            cp.wait()


def _dispatch(h, rt_lane, lay, ne):
    n, dm = h.shape
    tm = MOE_TILE
    return pl.pallas_call(
        _dispatch_kernel,
        grid_spec=pltpu.PrefetchScalarGridSpec(
            num_scalar_prefetch=5, grid=(n // tm,),
            in_specs=[pl.BlockSpec((SUBLANES, tm), lambda i, *_: (0, i)), pl.BlockSpec((tm, dm), lambda i, *_: (i, 0))],
            out_specs=pl.BlockSpec(memory_space=pl.ANY),
            scratch_shapes=[pltpu.VMEM((2, _tile_rows(ne), dm), BF16), pltpu.VMEM((MOE_BLK, dm), BF16),
                            pltpu.VMEM((ne, BF16_ROWS, dm), BF16),
                            pltpu.SemaphoreType.DMA((2, ne)), pltpu.SemaphoreType.DMA((ne,))]),
        out_shape=jax.ShapeDtypeStruct((lay['rows_total'], dm), BF16),
        compiler_params=_cparams("arbitrary"),
        name="moe_dispatch",
    )(lay['seg'], lay['base'], lay['off'], lay['phase'], lay['end'], rt_lane, h)


def _combine_kernel(seg_ref, base_ref, off_ref, rt_ref, x_ref, gn_ref, yb_ref, o_ref, got, sems, *, ne):
    i = pl.program_id(0)
    nt = pl.num_programs(0)
    tm = x_ref.shape[0]
    mrows = got.shape[1]
    slot = i % 2
    copies = functools.partial(_segment_copies, seg_ref, yb_ref, base_ref, got, off_ref, sems, ne=ne, to_grouped=False)

    @pl.when(i == 0)
    def _():
        got[...] = jnp.zeros_like(got)
        copies(tile=i, slot=slot, wait=False)

    @pl.when(i + 1 < nt)
    def _():
        copies(tile=i + 1, slot=1 - slot, wait=False)

    copies(tile=i, slot=slot, wait=True)

    col = lax.broadcasted_iota(I32, (tm, mrows), 1)
    rt = rt_ref[...]
    rows = got[slot]
    sel = jnp.concatenate([jnp.where(col == rt[:, k:k + 1].astype(I32), 1.0, 0.0).astype(BF16) for k in range(TOP_K)],
                          axis=0)
    picked = _dot(sel, rows)
    y = sum(rt[:, TOP_K + k:TOP_K + k + 1] * picked[k * tm:(k + 1) * tm] for k in range(TOP_K))
    o_ref[...] = _rms(x_ref[...] + y, gn_ref[...])


def _combine(yb, rt_tok, x2, gn, lay, ne):
    n, dm = x2.shape
    tm = MOE_TILE
    mrows = _tile_rows(ne)
    tok = lambda w: pl.BlockSpec((tm, w), lambda i, *_: (i, 0))
    return pl.pallas_call(
        functools.partial(_combine_kernel, ne=ne),
        grid_spec=pltpu.PrefetchScalarGridSpec(
            num_scalar_prefetch=3, grid=(n // tm,),
            in_specs=[tok(LANES), tok(dm), pl.BlockSpec((1, dm), lambda i, *_: (0, 0)),
                      pl.BlockSpec(memory_space=pl.ANY)],
            out_specs=tok(dm),
            scratch_shapes=[pltpu.VMEM((2, mrows, dm), BF16), pltpu.SemaphoreType.DMA((2, ne))]),
        out_shape=jax.ShapeDtypeStruct((n, dm), F32),
        compiler_params=_cparams("arbitrary"),
        name="moe_combine",
    )(lay['seg'], lay['base'], lay['off'], rt_tok, x2, gn.reshape(1, dm), yb)


def _moe_layout(cnt, n):
    nt, ne = cnt.shape
    blk = MOE_BLK
    whole = lambda a: a // BF16_ROWS * BF16_ROWS
    cap = -(-(n + BF16_ROWS + blk) // blk) * blk
    region = jnp.arange(ne, dtype=I32) * cap
    filled = jnp.cumsum(cnt, axis=0) - cnt
    phase = filled - whole(filled)
    span = jnp.where(cnt > 0, whole(phase + cnt + BF16_ROWS - 1), 0)
    base = (region[None, :] + filled - phase).astype(I32)
    off = (jnp.cumsum(span, axis=1) - span).astype(I32)
    rows_e = jnp.sum(cnt, axis=0)
    nblk_e = (rows_e + blk - 1) // blk
    end = (region + whole(rows_e + BF16_ROWS - 1)).astype(I32)
    ends = jnp.cumsum(nblk_e)
    max_blocks = TOP_K * n // blk + ne
    bid = jnp.arange(max_blocks + 1, dtype=I32)
    bexp = jnp.minimum(jnp.sum((bid[:, None] >= ends[None, :]).astype(I32), axis=1), ne - 1)
    brow = bexp * (cap // blk) + bid - (ends - nblk_e)[bexp]
    nblk = ends[-1]
    last = jnp.maximum(nblk - 1, 0)
    used = bid < nblk
    first = bid == (ends - nblk_e)[bexp]
    hand_over = (bid == ends[bexp] - 1) & (bid < last)
    role = jnp.where(used, first.astype(I32) + 2 * hand_over.astype(I32), 4).astype(I32)
    bexp = jnp.where(used, bexp, bexp[last]).astype(I32)
    brow = jnp.where(used, brow, brow[last]).astype(I32)
    flat = lambda a: a.reshape(-1)
    return dict(seg=flat(span.astype(I32)), base=flat(base), off=flat(off), phase=flat(phase.astype(I32)), end=end,
                brow=brow, bexp=bexp, role=role, rows_total=ne * cap, max_blocks=max_blocks)


def _moe(x2, h, logits, gn, w1, w3, w2):
    n, dm = x2.shape
    ne = w1.shape[0]
    rt_lane, rt_tok, cnt = _route(logits, ne)
    lay = _moe_layout(cnt[:, :, 0], n)
    xb = _dispatch(h, rt_lane, lay, ne)
    yb = _expert_ffn(xb, w1, w3, w2, lay['brow'], lay['bexp'], lay['role'], lay['max_blocks'])
    return _combine(yb, rt_tok, x2, gn, lay, ne)


def kernel(x, norm_mix_g, w_in, attn_rel_bias, rg_conv_w, rg_conv_b, rg_wx, rg_bx, rg_wa, rg_ba, rg_lambda, s5_a_re, s5_a_im, s5_log_dt, s5_b_re, s5_b_im, s5_c_re, s5_c_im, s5_d, s5_w_glu, g_group, w_out, norm_ffn_g, ffn_w1, ffn_w3, ffn_w2, moe_router, moe_w1, moe_w3, moe_w2, final_norm_g):
    bsz, l, dm = x.shape
    depth = w_in.shape[0]
    assert depth == 2 and l % ATTN_TQ == 0 and ATTN_TQ == N_PREV * CHUNK, "dense layer, then the MoE layer"
    d_rg = rg_conv_w.shape[2]
    d_s5 = s5_w_glu.shape[1]
    d_attn = (w_in.shape[2] - 2 * d_rg - d_s5) // 3
    x2 = x.reshape(bsz * l, dm)
    attn_bias = jax.vmap(_attn_bias)(attn_rel_bias)
    rg_gates_w = jnp.concatenate([jax.vmap(_block_diag)(rg_wx), jax.vmap(_block_diag)(rg_wa)], axis=2).astype(BF16)
    rg_gates_b = jnp.concatenate([rg_bx, rg_ba], axis=1).astype(F32)[:, None, :]
    rg_decay = jax.nn.softplus(-rg_lambda.astype(F32))[:, None, :]
    s5_prm = jax.vmap(_s5_params)(s5_a_re, s5_a_im, s5_log_dt, s5_b_re, s5_b_im, s5_c_re, s5_c_im, s5_d)
    for layer in range(depth):
        q, k, vt, xr, gate, us = _in_proj(x2, norm_mix_g[layer], w_in, layer, d_attn, d_rg, d_s5)
        y_attn = _attention(q, k, vt, attn_bias, layer, bsz, l)
        y_rg = _rglru(xr, gate, rg_conv_w.astype(F32), rg_conv_b.astype(F32)[:, None, :], rg_gates_w, rg_gates_b,
                      rg_decay, layer, bsz, l)
        y_s5 = _s5(us, s5_prm, layer, bsz, l)
        mixed = (y_attn, y_rg, y_s5, x2, g_group[layer].astype(F32), s5_w_glu, w_out, layer,
                 norm_ffn_g[layer].astype(F32))
        if layer == 0:
            x2 = _out_proj(*mixed, ffn=(ffn_w1[0], ffn_w3[0], ffn_w2[0]))
        else:
            x2, h, logits = _out_proj(*mixed, router=moe_router[0])
            x2 = _moe(x2, h, logits, final_norm_g.astype(F32), moe_w1[0], moe_w3[0], moe_w2[0])
    return x2.reshape(bsz, l, dm)
```

```python
import functools
import math

import jax
import jax.numpy as jnp
import numpy as np
from jax import lax
from jax.experimental import pallas as pl
from jax.experimental.pallas import tpu as pltpu

F32 = jnp.float32
BF16 = jnp.bfloat16
I32 = jnp.int32

EPS = 1e-6
CHUNK = 64
N_PREV = 8
BAND = (N_PREV + 1) * CHUNK
REL_CLIP = 128
HEAD_DIM = 64
RG_C = 8.0
RG_CONV_WIDTH = 4
S5_GROUP_DIM = 16
S5_T = 16
TOP_K = 2
NEG_BIG = -1e30

LANES = 128
SUBLANES = 8
BF16_ROWS = 16
VMEM_LIMIT = 52 * 1024 * 1024
MIX_VMEM_LIMIT = 58 * 1024 * 1024

ROW_TILE = 512
PROJ_TILE = 1024
ATTN_TQ = 512
ATTN_SPAN = 2 * CHUNK
ATTN_WIN = N_PREV * CHUNK + ATTN_SPAN
RG_TL = 1024
MOE_TILE = 256
MOE_BLK = 512
ROUTE_SPAN = 2048
SEG_PIECES = tuple(BF16_ROWS << b for b in range((MOE_TILE // BF16_ROWS).bit_length() - 1, -1, -1))
SEG_LARGE = MOE_TILE // 2
FFN_CHUNKS = ((0, 512), (512, 512), (1024, 512), (1536, 512), (2048, 512), (2560, 256))


def _cparams(*sem):
    return pltpu.CompilerParams(dimension_semantics=sem, vmem_limit_bytes=VMEM_LIMIT)


def _const_spec(shape):
    nd = len(shape)
    return pl.BlockSpec(shape, lambda *_: (0,) * nd, pipeline_mode=pl.Buffered(1))


def _rms(xf, g):
    var = jnp.mean(xf * xf, axis=-1, keepdims=True)
    return xf * lax.rsqrt(var + EPS) * g


def _sigmoid(x):
    return 1.0 / (1.0 + jnp.exp(-x))


def _gelu(x):
    c = math.sqrt(2.0 / math.pi)
    return 0.5 * x * (1.0 + jnp.tanh(c * (x + 0.044715 * (x * x * x))))


def _dot(a, b):
    return jnp.dot(a, b, preferred_element_type=F32)


def _in_proj_kernel(x_ref, g_ref, w_ref, wvt_ref, q_ref, k_ref, vt_ref, xr_ref, gt_ref, us_ref, *, d_attn, d_rg):
    u = _rms(x_ref[...], g_ref[...]).astype(BF16)
    lo = 0
    for ref, width in ((q_ref, d_attn), (k_ref, d_attn), (None, d_attn),
                       (xr_ref, d_rg), (gt_ref, d_rg), (us_ref, w_ref.shape[1] - 3 * d_attn - 2 * d_rg)):
        if ref is not None:
            ref[...] = _dot(u, w_ref[:, lo:lo + width].astype(BF16)).astype(ref.dtype)
        lo += width
    vt_ref[...] = lax.dot_general(wvt_ref[...].astype(BF16), u, (((1,), (1,)), ((), ())),
                                  preferred_element_type=F32).astype(vt_ref.dtype)


def _layer_spec(stacked, layer):
    rest = stacked.shape[1:]
    return pl.BlockSpec((None,) + rest, lambda *_: (layer,) + (0,) * len(rest), pipeline_mode=pl.Buffered(1))


def _in_proj(x2, g, w_all, layer, d_attn, d_rg, d_s5):
    n, dm = x2.shape
    tm = min(PROJ_TILE, n)
    row = lambda w: pl.BlockSpec((tm, w), lambda i: (i, 0))
    wv_t = w_all[layer, :, 2 * d_attn:3 * d_attn].T
    return pl.pallas_call(
        functools.partial(_in_proj_kernel, d_attn=d_attn, d_rg=d_rg),
        grid=(n // tm,),
        in_specs=[row(dm), _const_spec((1, dm)), _layer_spec(w_all, layer), _const_spec(wv_t.shape)],
        out_specs=[row(d_attn), row(d_attn), pl.BlockSpec((d_attn, tm), lambda i: (0, i)),
                   row(d_rg), row(d_rg), row(d_s5)],
        out_shape=[jax.ShapeDtypeStruct((n, d_attn), BF16)] * 2 + [jax.ShapeDtypeStruct((d_attn, n), BF16)]
        + [jax.ShapeDtypeStruct((n, d_rg), F32)] * 2 + [jax.ShapeDtypeStruct((n, d_s5), F32)],
        compiler_params=_cparams("parallel"),
        name="in_proj",
    )(x2, g.reshape(1, dm), w_all, wv_t)


def _attn_kernel(q_ref, kp_ref, kc_ref, vp_ref, vc_ref, bias_ref, o_ref, kz, sc, *, n_pairs):
    qi = pl.program_id(1)
    tq = q_ref.shape[0]
    kz[0:tq, :] = kp_ref[...]
    kz[tq:2 * tq, :] = kc_ref[...]
    first = lax.broadcasted_iota(I32, (ATTN_SPAN, LANES), 1) < HEAD_DIM
    pairs = [slice(hp * LANES, (hp + 1) * LANES) for hp in range(n_pairs)]

    def score(j, masked):
        r0 = j * ATTN_SPAN
        for hp, cols in enumerate(pairs):
            q2 = q_ref[pl.ds(r0, ATTN_SPAN), cols] * jnp.asarray(HEAD_DIM ** -0.5, BF16)
            zero = jnp.zeros_like(q2)
            qq = jnp.concatenate([jnp.where(first, q2, zero), jnp.where(first, zero, q2)], axis=0)
            s = lax.dot_general(kz[pl.ds(r0, ATTN_WIN), cols], qq, (((1,), (1,)), ((), ())),
                                preferred_element_type=F32) + bias_ref[hp]
            if masked:
                key = lax.broadcasted_iota(I32, (ATTN_WIN, 2 * ATTN_SPAN), 0)
                s = jnp.where(key >= tq - r0, s, NEG_BIG)
            sc[j % 2, hp] = s

    def finish(j):
        r0 = j * ATTN_SPAN
        old = tq - r0
        probs = []
        for hp in range(n_pairs):
            s = sc[j % 2, hp]
            probs.append(jnp.exp((s - jnp.max(s, axis=0, keepdims=True)).astype(BF16)))
        for hp, (cols, p) in enumerate(zip(pairs, probs)):
            ones = lambda w: jnp.ones((BF16_ROWS, w), BF16)
            ov = (_dot(jnp.concatenate([vp_ref[cols, r0:tq], ones(old)], axis=0), p[0:old])
                  + _dot(jnp.concatenate([vc_ref[cols, 0:ATTN_WIN - old], ones(ATTN_WIN - old)], axis=0), p[old:]))
            o2 = (ov[0:LANES] / ov[LANES:LANES + 1]).T
            o_ref[pl.ds(r0, ATTN_SPAN), cols] = jnp.where(first, o2[0:ATTN_SPAN], o2[ATTN_SPAN:]).astype(o_ref.dtype)

    def run(masked):
        groups = tq // ATTN_SPAN
        score(0, masked)
        for j in range(groups):
            if j + 1 < groups:
                score(j + 1, masked)
            finish(j)

    pl.when(qi == 0)(functools.partial(run, True))
    pl.when(qi != 0)(functools.partial(run, False))


def _attention(q, k, vt, bias_t, layer, bsz, l):
    n, da = q.shape
    tq = ATTN_TQ
    nt = l // tq
    n_pairs = da // LANES
    cur = pl.BlockSpec((tq, da), lambda b, i: (b * nt + i, 0))
    prev = pl.BlockSpec((tq, da), lambda b, i: (b * nt + jnp.maximum(i - 1, 0), 0))
    cur_t = pl.BlockSpec((da, tq), lambda b, i: (0, b * nt + i))
    prev_t = pl.BlockSpec((da, tq), lambda b, i: (0, b * nt + jnp.maximum(i - 1, 0)))
    return pl.pallas_call(
        functools.partial(_attn_kernel, n_pairs=n_pairs),
        grid=(bsz, nt),
        in_specs=[cur, prev, cur, prev_t, cur_t, _layer_spec(bias_t, layer)],
        out_specs=cur,
        out_shape=jax.ShapeDtypeStruct((n, da), BF16),
        scratch_shapes=[pltpu.VMEM((2 * tq, da), BF16), pltpu.VMEM((2, n_pairs, ATTN_WIN, 2 * ATTN_SPAN), F32)],
        compiler_params=_cparams("parallel", "parallel"),
        name="chunk_attn",
    )(q, k, k, vt, vt, bias_t)


def _attn_bias(rel_bias):
    h = rel_bias.shape[0]
    tab = rel_bias.astype(F32)
    n_far = N_PREV * CHUNK - REL_CLIP + CHUNK
    lo = 2 * REL_CLIP - (BAND + CHUNK - 1 - n_far)
    ext = jnp.concatenate([jnp.broadcast_to(tab[:, 2 * REL_CLIP:], (h, n_far)), tab[:, lo:2 * REL_CLIP][:, ::-1]], axis=1)
    wide = BAND + CHUNK
    ring = jnp.concatenate([ext[:, CHUNK - 1:], jnp.zeros((h, 1), F32), ext[:, :CHUNK - 1]], axis=1)
    bias = jnp.broadcast_to(ring[:, None, :], (h, CHUNK, wide)).reshape(h, CHUNK * wide)
    bias = bias[:, :CHUNK * (wide - 1)].reshape(h, CHUNK, wide - 1)[..., :BAND]
    band_t = bias.transpose(0, 2, 1)
    groups = ATTN_SPAN // CHUNK
    per_chunk = [jnp.pad(band_t, ((0, 0), (c * CHUNK, (groups - 1 - c) * CHUNK), (0, 0)), constant_values=NEG_BIG)
                 for c in range(groups)]
    win = jnp.stack(per_chunk, axis=2)
    return win.reshape(h // 2, 2, ATTN_WIN, ATTN_SPAN).transpose(0, 2, 1, 3).reshape(h // 2, ATTN_WIN, 2 * ATTN_SPAN)


def _rglru_kernel(x_ref, gt_ref, cw_ref, cb_ref, w_ref, b_ref, sp_ref, o_ref, xpad, a_s, b_s, hcar):
    t = pl.program_id(1)
    tl, c = x_ref.shape
    front = SUBLANES

    @pl.when(t == 0)
    def _():
        xpad[0:front, :] = jnp.zeros((front, c), F32)
        hcar[...] = jnp.zeros_like(hcar)

    xpad[front:front + tl, :] = x_ref[...]
    xc = cb_ref[...] + sum(
        cw_ref[j:j + 1, :] * xpad[front - (RG_CONV_WIDTH - 1) + j:front - (RG_CONV_WIDTH - 1) + j + tl, :]
        for j in range(RG_CONV_WIDTH))
    xpad[0:front, :] = xpad[tl:tl + front, :]
    pre = _dot(xc.astype(BF16), w_ref[...]) + b_ref[...]
    gx = _sigmoid(pre[:, 0:c])
    ga = _sigmoid(pre[:, c:2 * c])
    log_a = -RG_C * ga * sp_ref[...]
    a = jnp.exp(log_a)
    mult = jnp.sqrt(-jnp.tanh(log_a) * (a * a + 1.0))
    b = mult * gx * xc
    nh = c // LANES
    for j in range(nh):
        a_s[j] = a[:, j * LANES:(j + 1) * LANES]
        b_s[j] = b[:, j * LANES:(j + 1) * LANES]

    step_in_vreg = lax.broadcasted_iota(I32, (SUBLANES, LANES), 0)

    def vreg_scan(r, h_in):
        rows = pl.ds(pl.multiple_of(r * SUBLANES, SUBLANES), SUBLANES)
        out = []
        for j in range(nh):
            av, bv = a_s[j, rows, :], b_s[j, rows, :]
            for d in (1, 2, 4):
                seen = step_in_vreg >= d
                a_prev = jnp.where(seen, pltpu.roll(av, d, 0), 1.0)
                b_prev = jnp.where(seen, pltpu.roll(bv, d, 0), 0.0)
                bv = bv + av * b_prev
                av = av * a_prev
            h = bv + av * h_in[j]
            b_s[j, rows, :] = h
            out.append(h[SUBLANES - 1:SUBLANES, :])
        return tuple(out)

    h_last = lax.fori_loop(0, tl // SUBLANES, vreg_scan,
                           tuple(hcar[:, j * LANES:(j + 1) * LANES] for j in range(nh)), unroll=4)
    for j in range(nh):
        cols = slice(j * LANES, (j + 1) * LANES)
        hcar[:, cols] = h_last[j]
        o_ref[:, cols] = (b_s[j] * _gelu(gt_ref[:, cols])).astype(o_ref.dtype)


def _rglru(xr, gate, conv_w, conv_b, w_bd, b_cat, sp, layer, bsz, l):
    n, c = xr.shape
    tl = min(RG_TL, l)
    nt = l // tl
    row = pl.BlockSpec((tl, c), lambda b, t: (b * nt + t, 0))
    return pl.pallas_call(
        _rglru_kernel,
        grid=(bsz, nt),
        in_specs=[row, row] + [_layer_spec(a, layer) for a in (conv_w, conv_b, w_bd, b_cat, sp)],
        out_specs=row,
        out_shape=jax.ShapeDtypeStruct((n, c), BF16),
        scratch_shapes=[pltpu.VMEM((tl + SUBLANES, c), F32), pltpu.VMEM((c // LANES, tl, LANES), F32),
                        pltpu.VMEM((c // LANES, tl, LANES), F32), pltpu.VMEM((1, c), F32)],
        compiler_params=_cparams("parallel", "arbitrary"),
        name="rglru",
    )(xr, gate, conv_w, conv_b, w_bd, b_cat, sp)


def _block_diag(w):
    nb, d, _ = w.shape
    eye = jnp.eye(nb, dtype=w.dtype)
    return (eye[:, None, :, None] * w[:, :, None, :]).reshape(nb * d, nb * d)


def _s5_kernel(u_ref, m_ref, wab_ref, wo_ref, c1_ref, c2a_ref, c2b_ref, d_ref, y_ref, ps, ug, xa_s, xb_s):
    t, gd = S5_T, S5_GROUP_DIM
    ng, nk, _ = ug.shape
    nh = ps.shape[0]
    per_half = LANES // gd
    rows8 = SUBLANES

    for h in range(nh):
        ps[h] = u_ref[:, h * LANES:(h + 1) * LANES]

    lane_block = lax.broadcasted_iota(I32, (rows8, LANES), 1) // gd

    def block_transpose(vs):
        d = per_half // 2
        while d:
            keep = (lane_block & d) == 0
            out = list(vs)
            for i in range(per_half):
                if i & d == 0:
                    out[i] = jnp.where(keep, vs[i], pltpu.roll(vs[i + d], d * gd, 1))
                    out[i + d] = jnp.where(keep, pltpu.roll(vs[i], LANES - d * gd, 1), vs[i + d])
            vs, d = out, d // 2
        return vs

    def to_groups(r, carry):
        base = pl.multiple_of(r * rows8 * t, rows8 * t)
        rows = pl.ds(pl.multiple_of(r * rows8, rows8), rows8)
        for h in range(nh):
            for j in range(t // per_half):
                steps = [ps[h, pl.ds(base + j * per_half + i, rows8, stride=t), :] for i in range(per_half)]
                for k, v in enumerate(block_transpose(steps)):
                    ug[h * per_half + k, rows, j * LANES:(j + 1) * LANES] = v
        return carry

    lax.fori_loop(0, nk // rows8, to_groups, 0, unroll=2)

    def project(g, carry):
        u = ug[g]
        ub = u.astype(BF16)
        rows = pl.ds(pl.multiple_of(g * nk, nk), nk)
        xab = _dot(ub, wab_ref[g])
        xa_s[rows, :] = xab[:, 0:LANES]
        xb_s[rows, :] = xab[:, LANES:2 * LANES]
        ug[g] = _dot(ub, m_ref[g]) + d_ref[g] * u
        return carry

    lax.fori_loop(0, ng, project, 0, unroll=2)

    c1, c2a, c2b = c1_ref[...], c2a_ref[...], c2b_ref[...]

    def step(k, carry):
        xa, xb = carry
        rows = pl.ds(k, ng, stride=nk)
        ia = xa_s[rows, :]
        ib = xb_s[rows, :]
        xa_s[rows, :] = xa
        return c1 * xa + c2a * xb + ia, c1 * xb + c2b * xa + ib

    z = jnp.zeros((ng, xa_s.shape[1]), F32)
    lax.fori_loop(0, nk, step, (z, z), unroll=4)

    def respond(g, carry):
        rows = pl.ds(pl.multiple_of(g * nk, nk), nk)
        ug[g] = _gelu(ug[g] + _dot(xa_s[rows, :].astype(BF16), wo_ref[g]))
        return carry

    lax.fori_loop(0, ng, respond, 0, unroll=2)

    def from_groups(r, carry):
        base = pl.multiple_of(r * rows8 * t, rows8 * t)
        rows = pl.ds(pl.multiple_of(r * rows8, rows8), rows8)
        for h in range(nh):
            for j in range(t // per_half):
                groups = [ug[h * per_half + k, rows, j * LANES:(j + 1) * LANES] for k in range(per_half)]
                for i, v in enumerate(block_transpose(groups)):
                    ps[h, pl.ds(base + j * per_half + i, rows8, stride=t), :] = v
        return carry

    lax.fori_loop(0, nk // rows8, from_groups, 0, unroll=2)
    for h in range(nh):
        y_ref[:, h * LANES:(h + 1) * LANES] = ps[h].astype(y_ref.dtype)


def _s5(us, prm, layer, bsz, l):
    n, w = us.shape
    ng = w // S5_GROUP_DIM
    nk = l // S5_T
    p2 = prm[2].shape[-2]
    row = pl.BlockSpec((l, w), lambda b: (b, 0))
    return pl.pallas_call(
        _s5_kernel,
        grid=(bsz,),
        in_specs=[row] + [_layer_spec(a, layer) for a in prm],
        out_specs=row,
        out_shape=jax.ShapeDtypeStruct((n, w), BF16),
        scratch_shapes=[pltpu.VMEM((w // LANES, l, LANES), F32), pltpu.VMEM((ng, nk, S5_T * S5_GROUP_DIM), F32),
                        pltpu.VMEM((ng * nk, p2), F32), pltpu.VMEM((ng * nk, p2), F32)],
        compiler_params=_cparams("parallel"),
        name="s5",
    )(us, *prm)


def _s5_params(a_re, a_im, log_dt, b_re, b_im, c_re, c_im, d):
    t = S5_T
    g, p = a_re.shape
    gd = S5_GROUP_DIM
    hi = lax.Precision.HIGHEST
    a = lax.complex(a_re.astype(F32), a_im.astype(F32))
    dt = jnp.exp(log_dt.astype(F32))[:, None]
    a_bar = jnp.exp(a * dt)
    bm = lax.complex(b_re.astype(F32), b_im.astype(F32))
    cm = lax.complex(c_re.astype(F32), c_im.astype(F32))
    b_bar = ((a_bar - 1.0) / a)[..., None] * bm
    steps = jnp.arange(t + 1, dtype=F32)
    pw = jnp.exp((a * dt)[:, None, :] * steps[None, :, None])

    cp = (cm[:, None, :, :] * pw[:, :, None, :]).transpose(0, 3, 1, 2)
    w = t * gd
    lag = cp[:, :, :t].reshape(g, p, w)
    k2 = jnp.einsum('gpc,gpx->gcx', jnp.concatenate([b_bar.real, -b_bar.imag], axis=1),
                    jnp.concatenate([lag.real, lag.imag], axis=1), precision=hi)
    shift = np.zeros((t, w, w), np.float32)
    for s in range(t):
        shift[s, np.arange(w - s * gd), np.arange(w - s * gd) + s * gd] = 1.0
    m = jnp.einsum('gcx,sxl->gscl', k2, shift, precision=hi).reshape(g, w, w)
    win = pw[:, :t][:, ::-1][:, :, :, None] * b_bar[:, None]
    win = win.transpose(0, 1, 3, 2).reshape(g, w, p)
    wab = jnp.concatenate([win.real, win.imag, win.imag, win.real], axis=-1)
    co = cp[:, :, 1:].reshape(g, p, w)
    wo = jnp.concatenate([co.real, -co.imag], axis=1)
    lt = pw[:, t]
    c1 = jnp.concatenate([lt.real, lt.real], axis=-1)
    c2a = jnp.concatenate([-lt.imag, lt.imag], axis=-1)
    c2b = jnp.concatenate([lt.imag, -lt.imag], axis=-1)
    dtile = jnp.tile(d.astype(F32).reshape(g, 1, gd), (1, t, 1)).reshape(g, 1, t * gd)
    return (m.astype(BF16), wab.astype(BF16), wo.astype(BF16), c1, c2a, c2b, dtile)


def _out_proj_kernel(ya_ref, yr_ref, ys_ref, x_ref, gg_ref, wglu_ref, wo_ref, gf_ref, *rest, dense, chunks):
    ys = ys_ref[...]
    ys = ys.astype(F32) * _sigmoid(_dot(ys.astype(BF16), wglu_ref[...].astype(BF16)))
    acc = x_ref[...]
    lo = 0
    for y in (ya_ref[...].astype(F32), yr_ref[...].astype(F32), ys):
        w = y.shape[1]
        acc = acc + _dot(_rms(y, gg_ref[:, lo:lo + w]).astype(BF16), wo_ref[lo:lo + w, :].astype(BF16))
        lo += w
    h = _rms(acc, gf_ref[...])
    hb = h.astype(BF16)
    if dense:
        w1_ref, w3_ref, w2_ref, xo_ref = rest
        xo_ref[...] = acc + _swiglu_chunks(hb, w1_ref, w3_ref, w2_ref, chunks)
        return
    rt_ref, xo_ref, h_ref, lg_ref = rest
    xo_ref[...] = acc
    h_ref[...] = hb
    both = _dot(hb, rt_ref[...])
    h_lo = (h - hb.astype(F32)).astype(BF16)
    lg_ref[...] = both[:, 0:LANES] + both[:, LANES:2 * LANES] + _dot(h_lo, rt_ref[:, 0:LANES])


def _out_proj(ya, yr, ys, x2, gg, wglu_all, wo_all, layer, gf, *, ffn=None, router=None):
    n, dm = x2.shape
    tm = min(ROW_TILE, n)
    row = lambda w: pl.BlockSpec((tm, w), lambda i: (i, 0))
    ins = [ya, yr, ys, x2, gg.reshape(1, -1), wglu_all, wo_all, gf.reshape(1, dm)]
    in_specs = [row(ya.shape[1]), row(yr.shape[1]), row(ys.shape[1]), row(dm), _const_spec((1, gg.shape[0])),
                _layer_spec(wglu_all, layer), _layer_spec(wo_all, layer), _const_spec((1, dm))]
    if ffn is not None:
        ins += list(ffn)
        in_specs += [_const_spec(w.shape) for w in ffn]
        out_specs = row(dm)
        out_shape = jax.ShapeDtypeStruct((n, dm), F32)
        chunks = _ffn_chunks(ffn[0].shape[1])
    else:
        ne = router.shape[1]
        r_hi = router.astype(BF16)
        r_lo = (router.astype(F32) - r_hi.astype(F32)).astype(BF16)
        pad = lambda a: jnp.pad(a, ((0, 0), (0, LANES - ne)))
        ins.append(jnp.concatenate([pad(r_hi), pad(r_lo)], axis=1))
        in_specs.append(_const_spec((dm, 2 * LANES)))
        out_specs = [row(dm), row(dm), row(LANES)]
        out_shape = [jax.ShapeDtypeStruct((n, dm), F32), jax.ShapeDtypeStruct((n, dm), BF16),
                     jax.ShapeDtypeStruct((n, LANES), F32)]
        chunks = None
    return pl.pallas_call(
        functools.partial(_out_proj_kernel, dense=ffn is not None, chunks=chunks),
        grid=(n // tm,),
        in_specs=in_specs, out_specs=out_specs, out_shape=out_shape,
        compiler_params=pltpu.CompilerParams(dimension_semantics=("parallel",), vmem_limit_bytes=MIX_VMEM_LIMIT),
        name="out_proj",
    )(*ins)


def _swiglu_chunks(h, w1, w3, w2, chunks, before_chunk=None, after_chunk=None):
    acc = None
    for j, (lo, width) in enumerate(chunks):
        if before_chunk is not None:
            before_chunk(j)
        a = _dot(h, w1[:, lo:lo + width].astype(BF16))
        b = _dot(h, w3[:, lo:lo + width].astype(BF16))
        t = (a * _sigmoid(a) * b).astype(BF16)
        y = _dot(t, w2[lo:lo + width, :].astype(BF16))
        acc = y if acc is None else acc + y
        if after_chunk is not None:
            after_chunk(j)
    return acc


def _ffn_chunks(f):
    if f == sum(w for _, w in FFN_CHUNKS):
        return FFN_CHUNKS
    return ((0, f),)


def _expert_weight_copies(hbm, vmem, sems, chunks, expert, j):
    h1, h3, h2 = hbm
    v1, v3, v2 = vmem
    lo, w = chunks[j]
    return [pltpu.make_async_copy(h1.at[expert, :, lo:lo + w], v1.at[:, lo:lo + w], sems.at[j, 0]),
            pltpu.make_async_copy(h3.at[expert, :, lo:lo + w], v3.at[:, lo:lo + w], sems.at[j, 1]),
            pltpu.make_async_copy(h2.at[expert, lo:lo + w, :], v2.at[lo:lo + w, :], sems.at[j, 2])]


def _expert_ffn_kernel(brow_ref, bexp_ref, role_ref, x_ref, w1_hbm, w3_hbm, w2_hbm, o_ref, w1_v, w3_v, w2_v, sems,
                       *, chunks):
    i = pl.program_id(0)
    copies = functools.partial(_expert_weight_copies, (w1_hbm, w3_hbm, w2_hbm), (w1_v, w3_v, w2_v), sems, chunks)

    @pl.when(i == 0)
    def _():
        for j in range(len(chunks)):
            for cp in copies(bexp_ref[0], j):
                cp.start()

    def block(first, last):
        def before(j):
            for cp in copies(bexp_ref[i], j):
                cp.wait()

        def after(j):
            for cp in copies(bexp_ref[i + 1], j):
                cp.start()

        o_ref[...] = _swiglu_chunks(x_ref[...], w1_v, w3_v, w2_v, chunks, before if first else None,
                                    after if last else None).astype(o_ref.dtype)

    for role in range(4):
        pl.when(role_ref[i] == role)(functools.partial(block, bool(role & 1), bool(role & 2)))


def _expert_ffn(xb, w1, w3, w2, blk_row, blk_exp, role, max_blocks):
    rows, dm = xb.shape
    f = w1.shape[2]
    chunks = _ffn_chunks(f)
    row = pl.BlockSpec((MOE_BLK, dm), lambda i, br, be, ro: (br[i], 0))
    hbm = pl.BlockSpec(memory_space=pl.ANY)
    return pl.pallas_call(
        functools.partial(_expert_ffn_kernel, chunks=chunks),
        grid_spec=pltpu.PrefetchScalarGridSpec(
            num_scalar_prefetch=3, grid=(max_blocks,),
            in_specs=[row, hbm, hbm, hbm], out_specs=row,
            scratch_shapes=[pltpu.VMEM(w1.shape[1:], w1.dtype), pltpu.VMEM(w3.shape[1:], w3.dtype),
                            pltpu.VMEM(w2.shape[1:], w2.dtype), pltpu.SemaphoreType.DMA((len(chunks), 3))]),
        out_shape=jax.ShapeDtypeStruct((rows, dm), BF16),
        compiler_params=_cparams("arbitrary"),
        name="expert_ffn",
    )(blk_row, blk_exp, role, xb, w1, w3, w2)


def _route_tile(logits_tok, filled, ne):
    tm = logits_tok.shape[0]
    logits = logits_tok.T[0:ne, :]
    eidx = lax.broadcasted_iota(I32, (ne, tm), 0)
    m1 = jnp.max(logits, axis=0, keepdims=True)
    i1 = jnp.min(jnp.where(logits == m1, eidx, ne), axis=0, keepdims=True)
    sel1 = eidx == i1
    rest = jnp.where(sel1, -jnp.inf, logits)
    m2 = jnp.max(rest, axis=0, keepdims=True)
    i2 = jnp.min(jnp.where(rest == m2, eidx, ne), axis=0, keepdims=True)
    sel2 = eidx == i2
    e2 = jnp.exp(m2 - m1)
    den = 1.0 + e2
    rf = jnp.where(sel1 | sel2, 1.0, 0.0)
    cnt = jnp.sum(rf, axis=1, keepdims=True)
    before = (lax.broadcasted_iota(I32, (tm, tm), 0) < lax.broadcasted_iota(I32, (tm, tm), 1))
    rank = _dot(rf.astype(BF16), jnp.where(before, 1.0, 0.0).astype(BF16))
    whole = lambda a: jnp.floor(a * (1.0 / BF16_ROWS)) * BF16_ROWS
    phase = filled - whole(filled)
    span = jnp.where(cnt > 0, whole(phase + cnt + (BF16_ROWS - 1)), 0.0)
    ecol = lax.broadcasted_iota(I32, (ne, 1), 0)
    off = jnp.zeros((ne, 1), F32)
    for j in range(ne - 1):
        off = off + jnp.where(ecol > j, span[j:j + 1, :], 0.0)
    place = off + phase + rank
    pos1 = jnp.sum(jnp.where(sel1, place, 0.0), axis=0, keepdims=True)
    pos2 = jnp.sum(jnp.where(sel2, place, 0.0), axis=0, keepdims=True)
    record = [pos1, pos2, 1.0 / den, e2 / den]
    lane_major = jnp.concatenate(record + [jnp.zeros((SUBLANES - len(record), tm), F32)], axis=0)
    tok_major = jnp.concatenate([lane_major, jnp.zeros((LANES - SUBLANES, tm), F32)], axis=0).T
    return lane_major, tok_major, cnt


def _route_kernel(lg_ref, lane_ref, tok_ref, cnt_ref, filled, *, ne):
    tm = MOE_TILE

    @pl.when(pl.program_id(0) == 0)
    def _():
        filled[...] = jnp.zeros_like(filled)

    for t in range(lg_ref.shape[0] // tm):
        lane_major, tok_major, cnt = _route_tile(lg_ref[t * tm:(t + 1) * tm, :], filled[...], ne)
        lane_ref[:, t * tm:(t + 1) * tm] = lane_major
        tok_ref[t * tm:(t + 1) * tm, :] = tok_major
        cnt_ref[t] = jnp.broadcast_to(cnt, (ne, LANES)).astype(I32)
        filled[...] = filled[...] + cnt


def _route(logits, ne):
    n = logits.shape[0]
    tm = MOE_TILE
    span = min(ROUTE_SPAN, n)
    return pl.pallas_call(
        functools.partial(_route_kernel, ne=ne),
        grid=(n // span,),
        in_specs=[pl.BlockSpec((span, LANES), lambda i: (i, 0))],
        out_specs=[pl.BlockSpec((SUBLANES, span), lambda i: (0, i)), pl.BlockSpec((span, LANES), lambda i: (i, 0)),
                   pl.BlockSpec((span // tm, ne, LANES), lambda i: (i, 0, 0))],
        out_shape=[jax.ShapeDtypeStruct((SUBLANES, n), F32), jax.ShapeDtypeStruct((n, LANES), F32),
                   jax.ShapeDtypeStruct((n // tm, ne, LANES), I32)],
        scratch_shapes=[pltpu.VMEM((ne, 1), F32)],
        compiler_params=_cparams("arbitrary"),
        name="moe_route",
    )(logits)


def _segment_copies(seg_ref, grouped_ref, base_ref, tile_buf, off_ref, sems, tile, slot, ne, *, to_grouped, wait):
    large = [p for p in SEG_PIECES if p >= SEG_LARGE]
    small = [p for p in SEG_PIECES if p < SEG_LARGE]
    for e in range(ne):
        seg = seg_ref[tile * ne + e]

        def pieces(sizes, done, seg=seg, e=e):
            far = base_ref[tile * ne + e] + done
            near = off_ref[tile * ne + e] + done
            for piece in sizes:
                g = grouped_ref.at[pl.ds(pl.multiple_of(far, BF16_ROWS), piece)]
                t = tile_buf.at[slot, pl.ds(pl.multiple_of(near, BF16_ROWS), piece)]
                cp = pltpu.make_async_copy(t, g, sems.at[slot, e]) if to_grouped else \
                    pltpu.make_async_copy(g, t, sems.at[slot, e])
                has = (seg & piece) != 0

                @pl.when(has)
                def _():
                    cp.wait() if wait else cp.start()

                step = jnp.where(has, piece, 0)
                far = far + step
                near = near + step

        pl.when(seg >= SEG_LARGE)(functools.partial(pieces, large, 0))
        pieces(small, seg & -SEG_LARGE)


def _tile_rows(ne):
    return TOP_K * MOE_TILE + ne * 2 * BF16_ROWS


def _tail_copies(end_ref, zeros, xb_ref, sems, ne):
    return [pltpu.make_async_copy(zeros, xb_ref.at[pl.ds(pl.multiple_of(end_ref[e], BF16_ROWS), MOE_BLK)], sems.at[e])
            for e in range(ne)]


def _dispatch_kernel(seg_ref, base_ref, off_ref, phase_ref, end_ref, rt_ref, h_ref, xb_ref, res, zeros, partial,
                     sems, tail_sems):
    i = pl.program_id(0)
    nt = pl.num_programs(0)
    ne = tail_sems.shape[0]
    tm = h_ref.shape[0]
    mrows = res.shape[1]
    slot = i % 2
    copies = functools.partial(_segment_copies, seg_ref, xb_ref, base_ref, res, off_ref, sems, ne=ne, to_grouped=True)

    @pl.when(i == 0)
    def _():
        zeros[...] = jnp.zeros_like(zeros)
        partial[...] = jnp.zeros_like(partial)
        for cp in _tail_copies(end_ref, zeros, xb_ref, tail_sems, ne):
            cp.start()

    row = lax.broadcasted_iota(I32, (mrows, tm), 0)
    pos = rt_ref[0:2, :].astype(I32)
    perm = jnp.where((row == pos[0:1, :]) | (row == pos[1:2, :]), 1.0, 0.0).astype(BF16)
    res[slot] = _dot(perm, h_ref[...]).astype(BF16)

    tile_row = lax.broadcasted_iota(I32, (BF16_ROWS, res.shape[2]), 0)
    for e in range(ne):
        @pl.when(seg_ref[i * ne + e] > 0)
        def _():
            first = pl.ds(pl.multiple_of(off_ref[i * ne + e], BF16_ROWS), BF16_ROWS)
            merged = jnp.where(tile_row < phase_ref[i * ne + e], partial[e].astype(F32), res[slot, first, :].astype(F32))
            res[slot, first, :] = merged.astype(BF16)
            last = pl.ds(pl.multiple_of(off_ref[i * ne + e] + seg_ref[i * ne + e] - BF16_ROWS, BF16_ROWS), BF16_ROWS)
            partial[e] = res[slot, last, :]

    @pl.when(i > 0)
    def _():
        copies(tile=i - 1, slot=1 - slot, wait=True)

    copies(tile=i, slot=slot, wait=False)

    @pl.when(i == nt - 1)
    def _():
        copies(tile=i, slot=slot, wait=True)
        for cp in _tail_copies(end_ref, zeros, xb_ref, tail_sems, ne):
            cp.wait()


def _dispatch(h, rt_lane, lay, ne):
    n, dm = h.shape
    tm = MOE_TILE
    return pl.pallas_call(
        _dispatch_kernel,
        grid_spec=pltpu.PrefetchScalarGridSpec(
            num_scalar_prefetch=5, grid=(n // tm,),
            in_specs=[pl.BlockSpec((SUBLANES, tm), lambda i, *_: (0, i)), pl.BlockSpec((tm, dm), lambda i, *_: (i, 0))],
            out_specs=pl.BlockSpec(memory_space=pl.ANY),
            scratch_shapes=[pltpu.VMEM((2, _tile_rows(ne), dm), BF16), pltpu.VMEM((MOE_BLK, dm), BF16),
                            pltpu.VMEM((ne, BF16_ROWS, dm), BF16),
                            pltpu.SemaphoreType.DMA((2, ne)), pltpu.SemaphoreType.DMA((ne,))]),
        out_shape=jax.ShapeDtypeStruct((lay['rows_total'], dm), BF16),
        compiler_params=_cparams("arbitrary"),
        name="moe_dispatch",
    )(lay['seg'], lay['base'], lay['off'], lay['phase'], lay['end'], rt_lane, h)


def _combine_kernel(seg_ref, base_ref, off_ref, rt_ref, x_ref, gn_ref, yb_ref, o_ref, got, sems, *, ne):
    i = pl.program_id(0)
    nt = pl.num_programs(0)
    tm = x_ref.shape[0]
    mrows = got.shape[1]
    slot = i % 2
    copies = functools.partial(_segment_copies, seg_ref, yb_ref, base_ref, got, off_ref, sems, ne=ne, to_grouped=False)

    @pl.when(i == 0)
    def _():
        got[...] = jnp.zeros_like(got)
        copies(tile=i, slot=slot, wait=False)

    @pl.when(i + 1 < nt)
    def _():
        copies(tile=i + 1, slot=1 - slot, wait=False)

    copies(tile=i, slot=slot, wait=True)

    col = lax.broadcasted_iota(I32, (tm, mrows), 1)
    rt = rt_ref[...]
    rows = got[slot]
    sel = jnp.concatenate([jnp.where(col == rt[:, k:k + 1].astype(I32), 1.0, 0.0).astype(BF16) for k in range(TOP_K)],
                          axis=0)
    picked = _dot(sel, rows)
    y = sum(rt[:, TOP_K + k:TOP_K + k + 1] * picked[k * tm:(k + 1) * tm] for k in range(TOP_K))
    o_ref[...] = _rms(x_ref[...] + y, gn_ref[...])


def _combine(yb, rt_tok, x2, gn, lay, ne):
    n, dm = x2.shape
    tm = MOE_TILE
    mrows = _tile_rows(ne)
    tok = lambda w: pl.BlockSpec((tm, w), lambda i, *_: (i, 0))
    return pl.pallas_call(
        functools.partial(_combine_kernel, ne=ne),
        grid_spec=pltpu.PrefetchScalarGridSpec(
            num_scalar_prefetch=3, grid=(n // tm,),
            in_specs=[tok(LANES), tok(dm), pl.BlockSpec((1, dm), lambda i, *_: (0, 0)),
                      pl.BlockSpec(memory_space=pl.ANY)],
            out_specs=tok(dm),
            scratch_shapes=[pltpu.VMEM((2, mrows, dm), BF16), pltpu.SemaphoreType.DMA((2, ne))]),
        out_shape=jax.ShapeDtypeStruct((n, dm), F32),
        compiler_params=_cparams("arbitrary"),
        name="moe_combine",
    )(lay['seg'], lay['base'], lay['off'], rt_tok, x2, gn.reshape(1, dm), yb)


def _moe_layout(cnt, n):
    nt, ne = cnt.shape
    blk = MOE_BLK
    whole = lambda a: a // BF16_ROWS * BF16_ROWS
    cap = -(-(n + BF16_ROWS + blk) // blk) * blk
    region = jnp.arange(ne, dtype=I32) * cap
    filled = jnp.cumsum(cnt, axis=0) - cnt
    phase = filled - whole(filled)
    span = jnp.where(cnt > 0, whole(phase + cnt + BF16_ROWS - 1), 0)
    base = (region[None, :] + filled - phase).astype(I32)
    off = (jnp.cumsum(span, axis=1) - span).astype(I32)
    rows_e = jnp.sum(cnt, axis=0)
    nblk_e = (rows_e + blk - 1) // blk
    end = (region + whole(rows_e + BF16_ROWS - 1)).astype(I32)
    ends = jnp.cumsum(nblk_e)
    max_blocks = TOP_K * n // blk + ne
    bid = jnp.arange(max_blocks + 1, dtype=I32)
    bexp = jnp.minimum(jnp.sum((bid[:, None] >= ends[None, :]).astype(I32), axis=1), ne - 1)
    brow = bexp * (cap // blk) + bid - (ends - nblk_e)[bexp]
    nblk = ends[-1]
    last = jnp.maximum(nblk - 1, 0)
    used = bid < nblk
    first = bid == (ends - nblk_e)[bexp]
    hand_over = (bid == ends[bexp] - 1) & (bid < last)
    role = jnp.where(used, first.astype(I32) + 2 * hand_over.astype(I32), 4).astype(I32)
    bexp = jnp.where(used, bexp, bexp[last]).astype(I32)
    brow = jnp.where(used, brow, brow[last]).astype(I32)
    flat = lambda a: a.reshape(-1)
    return dict(seg=flat(span.astype(I32)), base=flat(base), off=flat(off), phase=flat(phase.astype(I32)), end=end,
                brow=brow, bexp=bexp, role=role, rows_total=ne * cap, max_blocks=max_blocks)


def _moe(x2, h, logits, gn, w1, w3, w2):
    n, dm = x2.shape
    ne = w1.shape[0]
    rt_lane, rt_tok, cnt = _route(logits, ne)
    lay = _moe_layout(cnt[:, :, 0], n)
    xb = _dispatch(h, rt_lane, lay, ne)
    yb = _expert_ffn(xb, w1, w3, w2, lay['brow'], lay['bexp'], lay['role'], lay['max_blocks'])
    return _combine(yb, rt_tok, x2, gn, lay, ne)


def kernel(x, norm_mix_g, w_in, attn_rel_bias, rg_conv_w, rg_conv_b, rg_wx, rg_bx, rg_wa, rg_ba, rg_lambda, s5_a_re, s5_a_im, s5_log_dt, s5_b_re, s5_b_im, s5_c_re, s5_c_im, s5_d, s5_w_glu, g_group, w_out, norm_ffn_g, ffn_w1, ffn_w3, ffn_w2, moe_router, moe_w1, moe_w3, moe_w2, final_norm_g):
    bsz, l, dm = x.shape
    depth = w_in.shape[0]
    assert depth == 2 and l % ATTN_TQ == 0 and ATTN_TQ == N_PREV * CHUNK, "dense layer, then the MoE layer"
    d_rg = rg_conv_w.shape[2]
    d_s5 = s5_w_glu.shape[1]
    d_attn = (w_in.shape[2] - 2 * d_rg - d_s5) // 3
    x2 = x.reshape(bsz * l, dm)
    attn_bias = jax.vmap(_attn_bias)(attn_rel_bias)
    rg_gates_w = jnp.concatenate([jax.vmap(_block_diag)(rg_wx), jax.vmap(_block_diag)(rg_wa)], axis=2).astype(BF16)
    rg_gates_b = jnp.concatenate([rg_bx, rg_ba], axis=1).astype(F32)[:, None, :]
    rg_decay = jax.nn.softplus(-rg_lambda.astype(F32))[:, None, :]
    s5_prm = jax.vmap(_s5_params)(s5_a_re, s5_a_im, s5_log_dt, s5_b_re, s5_b_im, s5_c_re, s5_c_im, s5_d)
    for layer in range(depth):
        q, k, vt, xr, gate, us = _in_proj(x2, norm_mix_g[layer], w_in, layer, d_attn, d_rg, d_s5)
        y_attn = _attention(q, k, vt, attn_bias, layer, bsz, l)
        y_rg = _rglru(xr, gate, rg_conv_w.astype(F32), rg_conv_b.astype(F32)[:, None, :], rg_gates_w, rg_gates_b,
                      rg_decay, layer, bsz, l)
        y_s5 = _s5(us, s5_prm, layer, bsz, l)
        mixed = (y_attn, y_rg, y_s5, x2, g_group[layer].astype(F32), s5_w_glu, w_out, layer,
                 norm_ffn_g[layer].astype(F32))
        if layer == 0:
            x2 = _out_proj(*mixed, ffn=(ffn_w1[0], ffn_w3[0], ffn_w2[0]))
        else:
            x2, h, logits = _out_proj(*mixed, router=moe_router[0])
            x2 = _moe(x2, h, logits, final_norm_g.astype(F32), moe_w1[0], moe_w3[0], moe_w2[0])
    return x2.reshape(bsz, l, dm)
```

```python
import functools
import math

import jax
import jax.numpy as jnp
import numpy as np
from jax import lax
from jax.experimental import pallas as pl
from jax.experimental.pallas import tpu as pltpu

F32 = jnp.float32
BF16 = jnp.bfloat16
I32 = jnp.int32

EPS = 1e-6
CHUNK = 64
N_PREV = 8
BAND = (N_PREV + 1) * CHUNK
REL_CLIP = 128
HEAD_DIM = 64
RG_C = 8.0
RG_CONV_WIDTH = 4
S5_GROUP_DIM = 16
S5_T = 16
TOP_K = 2
NEG_BIG = -1e30

LANES = 128
SUBLANES = 8
BF16_ROWS = 16
VMEM_LIMIT = 52 * 1024 * 1024
MIX_VMEM_LIMIT = 58 * 1024 * 1024

ROW_TILE = 512
PROJ_TILE = 1024
ATTN_TQ = 512
ATTN_SPAN = 2 * CHUNK
ATTN_WIN = N_PREV * CHUNK + ATTN_SPAN
RG_TL = 1024
MOE_TILE = 256
MOE_BLK = 512
ROUTE_SPAN = 2048
SEG_PIECES = tuple(BF16_ROWS << b for b in range((MOE_TILE // BF16_ROWS).bit_length() - 1, -1, -1))
COMBINE_TILES = 2
SEG_LARGE = MOE_TILE // 2
FFN_CHUNKS = ((0, 512), (512, 512), (1024, 512), (1536, 512), (2048, 512), (2560, 256))


def _cparams(*sem):
    return pltpu.CompilerParams(dimension_semantics=sem, vmem_limit_bytes=VMEM_LIMIT)


def _const_spec(shape):
    nd = len(shape)
    return pl.BlockSpec(shape, lambda *_: (0,) * nd, pipeline_mode=pl.Buffered(1))


def _rms(xf, g):
    var = jnp.mean(xf * xf, axis=-1, keepdims=True)
    return xf * lax.rsqrt(var + EPS) * g


def _sigmoid(x):
    return 1.0 / (1.0 + jnp.exp(-x))


def _gelu(x):
    c = math.sqrt(2.0 / math.pi)
    return 0.5 * x * (1.0 + jnp.tanh(c * (x + 0.044715 * (x * x * x))))


def _dot(a, b):
    return jnp.dot(a, b, preferred_element_type=F32)


def _in_proj_kernel(x_ref, g_ref, w_ref, wvt_ref, q_ref, k_ref, vt_ref, xr_ref, gt_ref, us_ref, *, d_attn, d_rg):
    u = _rms(x_ref[...], g_ref[...]).astype(BF16)
    lo = 0
    for ref, width in ((q_ref, d_attn), (k_ref, d_attn), (None, d_attn),
                       (xr_ref, d_rg), (gt_ref, d_rg), (us_ref, w_ref.shape[1] - 3 * d_attn - 2 * d_rg)):
        if ref is not None:
            ref[...] = _dot(u, w_ref[:, lo:lo + width].astype(BF16)).astype(ref.dtype)
        lo += width
    vt_ref[...] = lax.dot_general(wvt_ref[...].astype(BF16), u, (((1,), (1,)), ((), ())),
                                  preferred_element_type=F32).astype(vt_ref.dtype)


def _layer_spec(stacked, layer):
    rest = stacked.shape[1:]
    return pl.BlockSpec((None,) + rest, lambda *_: (layer,) + (0,) * len(rest), pipeline_mode=pl.Buffered(1))


def _in_proj(x2, g, w_all, layer, d_attn, d_rg, d_s5):
    n, dm = x2.shape
    tm = min(PROJ_TILE, n)
    row = lambda w: pl.BlockSpec((tm, w), lambda i: (i, 0))
    wv_t = w_all[layer, :, 2 * d_attn:3 * d_attn].T
    return pl.pallas_call(
        functools.partial(_in_proj_kernel, d_attn=d_attn, d_rg=d_rg),
        grid=(n // tm,),
        in_specs=[row(dm), _const_spec((1, dm)), _layer_spec(w_all, layer), _const_spec(wv_t.shape)],
        out_specs=[row(d_attn), row(d_attn), pl.BlockSpec((d_attn, tm), lambda i: (0, i)),
                   row(d_rg), row(d_rg), row(d_s5)],
        out_shape=[jax.ShapeDtypeStruct((n, d_attn), BF16)] * 2 + [jax.ShapeDtypeStruct((d_attn, n), BF16)]
        + [jax.ShapeDtypeStruct((n, d_rg), F32)] * 2 + [jax.ShapeDtypeStruct((n, d_s5), F32)],
        compiler_params=_cparams("parallel"),
        name="in_proj",
    )(x2, g.reshape(1, dm), w_all, wv_t)


def _attn_kernel(q_ref, kp_ref, kc_ref, vp_ref, vc_ref, bias_ref, o_ref, kz, sc, *, n_pairs):
    qi = pl.program_id(1)
    tq = q_ref.shape[0]
    kz[0:tq, :] = kp_ref[...]
    kz[tq:2 * tq, :] = kc_ref[...]
    first = lax.broadcasted_iota(I32, (ATTN_SPAN, LANES), 1) < HEAD_DIM
    pairs = [slice(hp * LANES, (hp + 1) * LANES) for hp in range(n_pairs)]

    def score(j, masked):
        r0 = j * ATTN_SPAN
        for hp, cols in enumerate(pairs):
            q2 = q_ref[pl.ds(r0, ATTN_SPAN), cols] * jnp.asarray(HEAD_DIM ** -0.5, BF16)
            zero = jnp.zeros_like(q2)
            qq = jnp.concatenate([jnp.where(first, q2, zero), jnp.where(first, zero, q2)], axis=0)
            s = lax.dot_general(kz[pl.ds(r0, ATTN_WIN), cols], qq, (((1,), (1,)), ((), ())),
                                preferred_element_type=F32) + bias_ref[hp]
            if masked:
                key = lax.broadcasted_iota(I32, (ATTN_WIN, 2 * ATTN_SPAN), 0)
                s = jnp.where(key >= tq - r0, s, NEG_BIG)
            sc[j % 2, hp] = s

    def finish(j):
        r0 = j * ATTN_SPAN
        old = tq - r0
        probs = []
        for hp in range(n_pairs):
            s = sc[j % 2, hp]
            probs.append(jnp.exp((s - jnp.max(s, axis=0, keepdims=True)).astype(BF16)))
        for hp, (cols, p) in enumerate(zip(pairs, probs)):
            ones = lambda w: jnp.ones((BF16_ROWS, w), BF16)
            ov = (_dot(jnp.concatenate([vp_ref[cols, r0:tq], ones(old)], axis=0), p[0:old])
                  + _dot(jnp.concatenate([vc_ref[cols, 0:ATTN_WIN - old], ones(ATTN_WIN - old)], axis=0), p[old:]))
            o2 = (ov[0:LANES] / ov[LANES:LANES + 1]).T
            o_ref[pl.ds(r0, ATTN_SPAN), cols] = jnp.where(first, o2[0:ATTN_SPAN], o2[ATTN_SPAN:]).astype(o_ref.dtype)

    def run(masked):
        groups = tq // ATTN_SPAN
        score(0, masked)
        for j in range(groups):
            if j + 1 < groups:
                score(j + 1, masked)
            finish(j)

    pl.when(qi == 0)(functools.partial(run, True))
    pl.when(qi != 0)(functools.partial(run, False))


def _attention(q, k, vt, bias_t, layer, bsz, l):
    n, da = q.shape
    tq = ATTN_TQ
    nt = l // tq
    n_pairs = da // LANES
    cur = pl.BlockSpec((tq, da), lambda b, i: (b * nt + i, 0))
    prev = pl.BlockSpec((tq, da), lambda b, i: (b * nt + jnp.maximum(i - 1, 0), 0))
    cur_t = pl.BlockSpec((da, tq), lambda b, i: (0, b * nt + i))
    prev_t = pl.BlockSpec((da, tq), lambda b, i: (0, b * nt + jnp.maximum(i - 1, 0)))
    return pl.pallas_call(
        functools.partial(_attn_kernel, n_pairs=n_pairs),
        grid=(bsz, nt),
        in_specs=[cur, prev, cur, prev_t, cur_t, _layer_spec(bias_t, layer)],
        out_specs=cur,
        out_shape=jax.ShapeDtypeStruct((n, da), BF16),
        scratch_shapes=[pltpu.VMEM((2 * tq, da), BF16), pltpu.VMEM((2, n_pairs, ATTN_WIN, 2 * ATTN_SPAN), F32)],
        compiler_params=_cparams("parallel", "parallel"),
        name="chunk_attn",
    )(q, k, k, vt, vt, bias_t)


def _attn_bias(rel_bias):
    h = rel_bias.shape[0]
    tab = rel_bias.astype(F32)
    n_far = N_PREV * CHUNK - REL_CLIP + CHUNK
    lo = 2 * REL_CLIP - (BAND + CHUNK - 1 - n_far)
    ext = jnp.concatenate([jnp.broadcast_to(tab[:, 2 * REL_CLIP:], (h, n_far)), tab[:, lo:2 * REL_CLIP][:, ::-1]], axis=1)
    wide = BAND + CHUNK
    ring = jnp.concatenate([ext[:, CHUNK - 1:], jnp.zeros((h, 1), F32), ext[:, :CHUNK - 1]], axis=1)
    bias = jnp.broadcast_to(ring[:, None, :], (h, CHUNK, wide)).reshape(h, CHUNK * wide)
    bias = bias[:, :CHUNK * (wide - 1)].reshape(h, CHUNK, wide - 1)[..., :BAND]
    band_t = bias.transpose(0, 2, 1)
    groups = ATTN_SPAN // CHUNK
    per_chunk = [jnp.pad(band_t, ((0, 0), (c * CHUNK, (groups - 1 - c) * CHUNK), (0, 0)), constant_values=NEG_BIG)
                 for c in range(groups)]
    win = jnp.stack(per_chunk, axis=2)
    return win.reshape(h // 2, 2, ATTN_WIN, ATTN_SPAN).transpose(0, 2, 1, 3).reshape(h // 2, ATTN_WIN, 2 * ATTN_SPAN)


def _rglru_kernel(x_ref, gt_ref, cw_ref, cb_ref, w_ref, b_ref, sp_ref, o_ref, xpad, a_s, b_s, hcar):
    t = pl.program_id(1)
    tl, c = x_ref.shape
    front = SUBLANES

    @pl.when(t == 0)
    def _():
        xpad[0:front, :] = jnp.zeros((front, c), F32)
        hcar[...] = jnp.zeros_like(hcar)

    xpad[front:front + tl, :] = x_ref[...]
    xc = cb_ref[...] + sum(
        cw_ref[j:j + 1, :] * xpad[front - (RG_CONV_WIDTH - 1) + j:front - (RG_CONV_WIDTH - 1) + j + tl, :]
        for j in range(RG_CONV_WIDTH))
    xpad[0:front, :] = xpad[tl:tl + front, :]
    pre = _dot(xc.astype(BF16), w_ref[...]) + b_ref[...]
    gx = _sigmoid(pre[:, 0:c])
    ga = _sigmoid(pre[:, c:2 * c])
    log_a = -RG_C * ga * sp_ref[...]
    a = jnp.exp(log_a)
    mult = jnp.sqrt(-jnp.tanh(log_a) * (a * a + 1.0))
    b = mult * gx * xc
    nh = c // LANES
    for j in range(nh):
        a_s[j] = a[:, j * LANES:(j + 1) * LANES]
        b_s[j] = b[:, j * LANES:(j + 1) * LANES]

    step_in_vreg = lax.broadcasted_iota(I32, (SUBLANES, LANES), 0)

    def vreg_scan(r, h_in):
        rows = pl.ds(pl.multiple_of(r * SUBLANES, SUBLANES), SUBLANES)
        out = []
        for j in range(nh):
            av, bv = a_s[j, rows, :], b_s[j, rows, :]
            for d in (1, 2, 4):
                seen = step_in_vreg >= d
                a_prev = jnp.where(seen, pltpu.roll(av, d, 0), 1.0)
                b_prev = jnp.where(seen, pltpu.roll(bv, d, 0), 0.0)
                bv = bv + av * b_prev
                av = av * a_prev
            h = bv + av * h_in[j]
            b_s[j, rows, :] = h
            out.append(h[SUBLANES - 1:SUBLANES, :])
        return tuple(out)

    h_last = lax.fori_loop(0, tl // SUBLANES, vreg_scan,
                           tuple(hcar[:, j * LANES:(j + 1) * LANES] for j in range(nh)), unroll=4)
    for j in range(nh):
        cols = slice(j * LANES, (j + 1) * LANES)
        hcar[:, cols] = h_last[j]
        o_ref[:, cols] = (b_s[j] * _gelu(gt_ref[:, cols])).astype(o_ref.dtype)


def _rglru(xr, gate, conv_w, conv_b, w_bd, b_cat, sp, layer, bsz, l):
    n, c = xr.shape
    tl = min(RG_TL, l)
    nt = l // tl
    row = pl.BlockSpec((tl, c), lambda b, t: (b * nt + t, 0))
    return pl.pallas_call(
        _rglru_kernel,
        grid=(bsz, nt),
        in_specs=[row, row] + [_layer_spec(a, layer) for a in (conv_w, conv_b, w_bd, b_cat, sp)],
        out_specs=row,
        out_shape=jax.ShapeDtypeStruct((n, c), BF16),
        scratch_shapes=[pltpu.VMEM((tl + SUBLANES, c), F32), pltpu.VMEM((c // LANES, tl, LANES), F32),
                        pltpu.VMEM((c // LANES, tl, LANES), F32), pltpu.VMEM((1, c), F32)],
        compiler_params=_cparams("parallel", "arbitrary"),
        name="rglru",
    )(xr, gate, conv_w, conv_b, w_bd, b_cat, sp)


def _block_diag(w):
    nb, d, _ = w.shape
    eye = jnp.eye(nb, dtype=w.dtype)
    return (eye[:, None, :, None] * w[:, :, None, :]).reshape(nb * d, nb * d)


def _s5_kernel(u_ref, m_ref, wab_ref, wo_ref, c1_ref, c2a_ref, c2b_ref, d_ref, y_ref, ps, ug, xa_s, xb_s):
    t, gd = S5_T, S5_GROUP_DIM
    ng, nk, _ = ug.shape
    nh = ps.shape[0]
    per_half = LANES // gd
    rows8 = SUBLANES

    for h in range(nh):
        ps[h] = u_ref[:, h * LANES:(h + 1) * LANES]

    lane_block = lax.broadcasted_iota(I32, (rows8, LANES), 1) // gd

    def block_transpose(vs):
        d = per_half // 2
        while d:
            keep = (lane_block & d) == 0
            out = list(vs)
            for i in range(per_half):
                if i & d == 0:
                    out[i] = jnp.where(keep, vs[i], pltpu.roll(vs[i + d], d * gd, 1))
                    out[i + d] = jnp.where(keep, pltpu.roll(vs[i], LANES - d * gd, 1), vs[i + d])
            vs, d = out, d // 2
        return vs

    def to_groups(r, carry):
        base = pl.multiple_of(r * rows8 * t, rows8 * t)
        rows = pl.ds(pl.multiple_of(r * rows8, rows8), rows8)
        for h in range(nh):
            for j in range(t // per_half):
                steps = [ps[h, pl.ds(base + j * per_half + i, rows8, stride=t), :] for i in range(per_half)]
                for k, v in enumerate(block_transpose(steps)):
                    ug[h * per_half + k, rows, j * LANES:(j + 1) * LANES] = v
        return carry

    lax.fori_loop(0, nk // rows8, to_groups, 0, unroll=2)

    def project(g, carry):
        u = ug[g]
        ub = u.astype(BF16)
        rows = pl.ds(pl.multiple_of(g * nk, nk), nk)
        xab = _dot(ub, wab_ref[g])
        xa_s[rows, :] = xab[:, 0:LANES]
        xb_s[rows, :] = xab[:, LANES:2 * LANES]
        ug[g] = _dot(ub, m_ref[g]) + d_ref[g] * u
        return carry

    lax.fori_loop(0, ng, project, 0, unroll=2)

    c1, c2a, c2b = c1_ref[...], c2a_ref[...], c2b_ref[...]

    def step(k, carry):
        xa, xb = carry
        rows = pl.ds(k, ng, stride=nk)
        ia = xa_s[rows, :]
        ib = xb_s[rows, :]
        xa_s[rows, :] = xa
        return c1 * xa + c2a * xb + ia, c1 * xb + c2b * xa + ib

    z = jnp.zeros((ng, xa_s.shape[1]), F32)
    lax.fori_loop(0, nk, step, (z, z), unroll=4)

    def respond(g, carry):
        rows = pl.ds(pl.multiple_of(g * nk, nk), nk)
        ug[g] = _gelu(ug[g] + _dot(xa_s[rows, :].astype(BF16), wo_ref[g]))
        return carry

    lax.fori_loop(0, ng, respond, 0, unroll=2)

    def from_groups(r, carry):
        base = pl.multiple_of(r * rows8 * t, rows8 * t)
        rows = pl.ds(pl.multiple_of(r * rows8, rows8), rows8)
        for h in range(nh):
            for j in range(t // per_half):
                groups = [ug[h * per_half + k, rows, j * LANES:(j + 1) * LANES] for k in range(per_half)]
                for i, v in enumerate(block_transpose(groups)):
                    ps[h, pl.ds(base + j * per_half + i, rows8, stride=t), :] = v
        return carry

    lax.fori_loop(0, nk // rows8, from_groups, 0, unroll=2)
    for h in range(nh):
        y_ref[:, h * LANES:(h + 1) * LANES] = ps[h].astype(y_ref.dtype)


def _s5(us, prm, layer, bsz, l):
    n, w = us.shape
    ng = w // S5_GROUP_DIM
    nk = l // S5_T
    p2 = prm[2].shape[-2]
    row = pl.BlockSpec((l, w), lambda b: (b, 0))
    return pl.pallas_call(
        _s5_kernel,
        grid=(bsz,),
        in_specs=[row] + [_layer_spec(a, layer) for a in prm],
        out_specs=row,
        out_shape=jax.ShapeDtypeStruct((n, w), BF16),
        scratch_shapes=[pltpu.VMEM((w // LANES, l, LANES), F32), pltpu.VMEM((ng, nk, S5_T * S5_GROUP_DIM), F32),
                        pltpu.VMEM((ng * nk, p2), F32), pltpu.VMEM((ng * nk, p2), F32)],
        compiler_params=_cparams("parallel"),
        name="s5",
    )(us, *prm)


def _s5_params(a_re, a_im, log_dt, b_re, b_im, c_re, c_im, d):
    t = S5_T
    g, p = a_re.shape
    gd = S5_GROUP_DIM
    hi = lax.Precision.HIGHEST
    a = lax.complex(a_re.astype(F32), a_im.astype(F32))
    dt = jnp.exp(log_dt.astype(F32))[:, None]
    a_bar = jnp.exp(a * dt)
    bm = lax.complex(b_re.astype(F32), b_im.astype(F32))
    cm = lax.complex(c_re.astype(F32), c_im.astype(F32))
    b_bar = ((a_bar - 1.0) / a)[..., None] * bm
    steps = jnp.arange(t + 1, dtype=F32)
    pw = jnp.exp((a * dt)[:, None, :] * steps[None, :, None])

    cp = (cm[:, None, :, :] * pw[:, :, None, :]).transpose(0, 3, 1, 2)
    w = t * gd
    lag = cp[:, :, :t].reshape(g, p, w)
    k2 = jnp.einsum('gpc,gpx->gcx', jnp.concatenate([b_bar.real, -b_bar.imag], axis=1),
                    jnp.concatenate([lag.real, lag.imag], axis=1), precision=hi)
    shift = np.zeros((t, w, w), np.float32)
    for s in range(t):
        shift[s, np.arange(w - s * gd), np.arange(w - s * gd) + s * gd] = 1.0
    m = jnp.einsum('gcx,sxl->gscl', k2, shift, precision=hi).reshape(g, w, w)
    win = pw[:, :t][:, ::-1][:, :, :, None] * b_bar[:, None]
    win = win.transpose(0, 1, 3, 2).reshape(g, w, p)
    wab = jnp.concatenate([win.real, win.imag, win.imag, win.real], axis=-1)
    co = cp[:, :, 1:].reshape(g, p, w)
    wo = jnp.concatenate([co.real, -co.imag], axis=1)
    lt = pw[:, t]
    c1 = jnp.concatenate([lt.real, lt.real], axis=-1)
    c2a = jnp.concatenate([-lt.imag, lt.imag], axis=-1)
    c2b = jnp.concatenate([lt.imag, -lt.imag], axis=-1)
    dtile = jnp.tile(d.astype(F32).reshape(g, 1, gd), (1, t, 1)).reshape(g, 1, t * gd)
    return (m.astype(BF16), wab.astype(BF16), wo.astype(BF16), c1, c2a, c2b, dtile)


def _out_proj_kernel(ya_ref, yr_ref, ys_ref, x_ref, gg_ref, wglu_ref, wo_ref, gf_ref, *rest, dense, chunks):
    ys = ys_ref[...]
    ys = ys.astype(F32) * _sigmoid(_dot(ys.astype(BF16), wglu_ref[...].astype(BF16)))
    acc = x_ref[...]
    lo = 0
    for y in (ya_ref[...].astype(F32), yr_ref[...].astype(F32), ys):
        w = y.shape[1]
        acc = acc + _dot(_rms(y, gg_ref[:, lo:lo + w]).astype(BF16), wo_ref[lo:lo + w, :].astype(BF16))
        lo += w
    h = _rms(acc, gf_ref[...])
    hb = h.astype(BF16)
    if dense:
        w1_ref, w3_ref, w2_ref, xo_ref = rest
        xo_ref[...] = acc + _swiglu_chunks(hb, w1_ref, w3_ref, w2_ref, chunks)
        return
    rt_ref, xo_ref, h_ref, lg_ref = rest
    xo_ref[...] = acc
    h_ref[...] = hb
    both = _dot(hb, rt_ref[...])
    h_lo = (h - hb.astype(F32)).astype(BF16)
    lg_ref[...] = both[:, 0:LANES] + both[:, LANES:2 * LANES] + _dot(h_lo, rt_ref[:, 0:LANES])


def _out_proj(ya, yr, ys, x2, gg, wglu_all, wo_all, layer, gf, *, ffn=None, router=None):
    n, dm = x2.shape
    tm = min(ROW_TILE, n)
    row = lambda w: pl.BlockSpec((tm, w), lambda i: (i, 0))
    ins = [ya, yr, ys, x2, gg.reshape(1, -1), wglu_all, wo_all, gf.reshape(1, dm)]
    in_specs = [row(ya.shape[1]), row(yr.shape[1]), row(ys.shape[1]), row(dm), _const_spec((1, gg.shape[0])),
                _layer_spec(wglu_all, layer), _layer_spec(wo_all, layer), _const_spec((1, dm))]
    if ffn is not None:
        ins += list(ffn)
        in_specs += [_const_spec(w.shape) for w in ffn]
        out_specs = row(dm)
        out_shape = jax.ShapeDtypeStruct((n, dm), F32)
        chunks = _ffn_chunks(ffn[0].shape[1])
    else:
        ne = router.shape[1]
        r_hi = router.astype(BF16)
        r_lo = (router.astype(F32) - r_hi.astype(F32)).astype(BF16)
        pad = lambda a: jnp.pad(a, ((0, 0), (0, LANES - ne)))
        ins.append(jnp.concatenate([pad(r_hi), pad(r_lo)], axis=1))
        in_specs.append(_const_spec((dm, 2 * LANES)))
        out_specs = [row(dm), row(dm), row(LANES)]
        out_shape = [jax.ShapeDtypeStruct((n, dm), F32), jax.ShapeDtypeStruct((n, dm), BF16),
                     jax.ShapeDtypeStruct((n, LANES), F32)]
        chunks = None
    return pl.pallas_call(
        functools.partial(_out_proj_kernel, dense=ffn is not None, chunks=chunks),
        grid=(n // tm,),
        in_specs=in_specs, out_specs=out_specs, out_shape=out_shape,
        compiler_params=pltpu.CompilerParams(dimension_semantics=("parallel",), vmem_limit_bytes=MIX_VMEM_LIMIT),
        name="out_proj",
    )(*ins)


def _swiglu_chunks(h, w1, w3, w2, chunks, before_chunk=None, after_chunk=None):
    acc = None
    for j, (lo, width) in enumerate(chunks):
        if before_chunk is not None:
            before_chunk(j)
        a = _dot(h, w1[:, lo:lo + width].astype(BF16))
        b = _dot(h, w3[:, lo:lo + width].astype(BF16))
        t = (a * _sigmoid(a) * b).astype(BF16)
        y = _dot(t, w2[lo:lo + width, :].astype(BF16))
        acc = y if acc is None else acc + y
        if after_chunk is not None:
            after_chunk(j)
    return acc


def _ffn_chunks(f):
    if f == sum(w for _, w in FFN_CHUNKS):
        return FFN_CHUNKS
    return ((0, f),)


def _expert_weight_copies(hbm, vmem, sems, chunks, expert, j):
    h1, h3, h2 = hbm
    v1, v3, v2 = vmem
    lo, w = chunks[j]
    return [pltpu.make_async_copy(h1.at[expert, :, lo:lo + w], v1.at[:, lo:lo + w], sems.at[j, 0]),
            pltpu.make_async_copy(h3.at[expert, :, lo:lo + w], v3.at[:, lo:lo + w], sems.at[j, 1]),
            pltpu.make_async_copy(h2.at[expert, lo:lo + w, :], v2.at[lo:lo + w, :], sems.at[j, 2])]


def _expert_ffn_kernel(brow_ref, bexp_ref, role_ref, x_ref, w1_hbm, w3_hbm, w2_hbm, o_ref, w1_v, w3_v, w2_v, sems,
                       *, chunks):
    i = pl.program_id(0)
    copies = functools.partial(_expert_weight_copies, (w1_hbm, w3_hbm, w2_hbm), (w1_v, w3_v, w2_v), sems, chunks)

    @pl.when(i == 0)
    def _():
        for j in range(len(chunks)):
            for cp in copies(bexp_ref[0], j):
                cp.start()

    def block(first, last):
        def before(j):
            for cp in copies(bexp_ref[i], j):
                cp.wait()

        def after(j):
            for cp in copies(bexp_ref[i + 1], j):
                cp.start()

        o_ref[...] = _swiglu_chunks(x_ref[...], w1_v, w3_v, w2_v, chunks, before if first else None,
                                    after if last else None).astype(o_ref.dtype)

    for role in range(4):
        pl.when(role_ref[i] == role)(functools.partial(block, bool(role & 1), bool(role & 2)))


def _expert_ffn(xb, w1, w3, w2, blk_row, blk_exp, role, max_blocks):
    rows, dm = xb.shape
    f = w1.shape[2]
    chunks = _ffn_chunks(f)
    row = pl.BlockSpec((MOE_BLK, dm), lambda i, br, be, ro: (br[i], 0))
    hbm = pl.BlockSpec(memory_space=pl.ANY)
    return pl.pallas_call(
        functools.partial(_expert_ffn_kernel, chunks=chunks),
        grid_spec=pltpu.PrefetchScalarGridSpec(
            num_scalar_prefetch=3, grid=(max_blocks,),
            in_specs=[row, hbm, hbm, hbm], out_specs=row,
            scratch_shapes=[pltpu.VMEM(w1.shape[1:], w1.dtype), pltpu.VMEM(w3.shape[1:], w3.dtype),
                            pltpu.VMEM(w2.shape[1:], w2.dtype), pltpu.SemaphoreType.DMA((len(chunks), 3))]),
        out_shape=jax.ShapeDtypeStruct((rows, dm), BF16),
        compiler_params=_cparams("arbitrary"),
        name="expert_ffn",
    )(blk_row, blk_exp, role, xb, w1, w3, w2)


def _route_tile(logits_tok, filled, ne):
    tm = logits_tok.shape[0]
    logits = logits_tok.T[0:ne, :]
    eidx = lax.broadcasted_iota(I32, (ne, tm), 0)
    m1 = jnp.max(logits, axis=0, keepdims=True)
    i1 = jnp.min(jnp.where(logits == m1, eidx, ne), axis=0, keepdims=True)
    sel1 = eidx == i1
    rest = jnp.where(sel1, -jnp.inf, logits)
    m2 = jnp.max(rest, axis=0, keepdims=True)
    i2 = jnp.min(jnp.where(rest == m2, eidx, ne), axis=0, keepdims=True)
    sel2 = eidx == i2
    e2 = jnp.exp(m2 - m1)
    den = 1.0 + e2
    rf = jnp.where(sel1 | sel2, 1.0, 0.0)
    cnt = jnp.sum(rf, axis=1, keepdims=True)
    before = (lax.broadcasted_iota(I32, (tm, tm), 0) < lax.broadcasted_iota(I32, (tm, tm), 1))
    rank = _dot(rf.astype(BF16), jnp.where(before, 1.0, 0.0).astype(BF16))
    whole = lambda a: jnp.floor(a * (1.0 / BF16_ROWS)) * BF16_ROWS
    phase = filled - whole(filled)
    span = jnp.where(cnt > 0, whole(phase + cnt + (BF16_ROWS - 1)), 0.0)
    ecol = lax.broadcasted_iota(I32, (ne, 1), 0)
    off = jnp.zeros((ne, 1), F32)
    for j in range(ne - 1):
        off = off + jnp.where(ecol > j, span[j:j + 1, :], 0.0)
    place = off + phase + rank
    pos1 = jnp.sum(jnp.where(sel1, place, 0.0), axis=0, keepdims=True)
    pos2 = jnp.sum(jnp.where(sel2, place, 0.0), axis=0, keepdims=True)
    record = [pos1, pos2, 1.0 / den, e2 / den]
    lane_major = jnp.concatenate(record + [jnp.zeros((SUBLANES - len(record), tm), F32)], axis=0)
    tok_major = jnp.concatenate([lane_major, jnp.zeros((LANES - SUBLANES, tm), F32)], axis=0).T
    return lane_major, tok_major, cnt


def _route_kernel(lg_ref, lane_ref, tok_ref, cnt_ref, filled, *, ne):
    tm = MOE_TILE

    @pl.when(pl.program_id(0) == 0)
    def _():
        filled[...] = jnp.zeros_like(filled)

    for t in range(lg_ref.shape[0] // tm):
        lane_major, tok_major, cnt = _route_tile(lg_ref[t * tm:(t + 1) * tm, :], filled[...], ne)
        lane_ref[:, t * tm:(t + 1) * tm] = lane_major
        tok_ref[t * tm:(t + 1) * tm, :] = tok_major
        cnt_ref[t] = jnp.broadcast_to(cnt, (ne, LANES)).astype(I32)
        filled[...] = filled[...] + cnt


def _route(logits, ne):
    n = logits.shape[0]
    tm = MOE_TILE
    span = min(ROUTE_SPAN, n)
    return pl.pallas_call(
        functools.partial(_route_kernel, ne=ne),
        grid=(n // span,),
        in_specs=[pl.BlockSpec((span, LANES), lambda i: (i, 0))],
        out_specs=[pl.BlockSpec((SUBLANES, span), lambda i: (0, i)), pl.BlockSpec((span, LANES), lambda i: (i, 0)),
                   pl.BlockSpec((span // tm, ne, LANES), lambda i: (i, 0, 0))],
        out_shape=[jax.ShapeDtypeStruct((SUBLANES, n), F32), jax.ShapeDtypeStruct((n, LANES), F32),
                   jax.ShapeDtypeStruct((n // tm, ne, LANES), I32)],
        scratch_shapes=[pltpu.VMEM((ne, 1), F32)],
        compiler_params=_cparams("arbitrary"),
        name="moe_route",
    )(logits)


def _segment_copies(seg_ref, grouped_ref, base_ref, tile_buf, off_ref, sems, tile, slot, ne, *, to_grouped, wait):
    large = [p for p in SEG_PIECES if p >= SEG_LARGE]
    small = [p for p in SEG_PIECES if p < SEG_LARGE]
    for e in range(ne):
        seg = seg_ref[tile * ne + e]

        def pieces(sizes, done, seg=seg, e=e):
            far = base_ref[tile * ne + e] + done
            near = off_ref[tile * ne + e] + done
            for piece in sizes:
                g = grouped_ref.at[pl.ds(pl.multiple_of(far, BF16_ROWS), piece)]
                t = tile_buf.at[slot, pl.ds(pl.multiple_of(near, BF16_ROWS), piece)]
                cp = pltpu.make_async_copy(t, g, sems.at[slot, e]) if to_grouped else \
                    pltpu.make_async_copy(g, t, sems.at[slot, e])
                has = (seg & piece) != 0

                @pl.when(has)
                def _():
                    cp.wait() if wait else cp.start()

                step = jnp.where(has, piece, 0)
                far = far + step
                near = near + step

        pl.when(seg >= SEG_LARGE)(functools.partial(pieces, large, 0))
        pieces(small, seg & -SEG_LARGE)


def _tile_rows(ne):
    return TOP_K * MOE_TILE + ne * 2 * BF16_ROWS


def _tail_copies(end_ref, zeros, xb_ref, sems, ne):
    return [pltpu.make_async_copy(zeros, xb_ref.at[pl.ds(pl.multiple_of(end_ref[e], BF16_ROWS), MOE_BLK)], sems.at[e])
            for e in range(ne)]


def _dispatch_kernel(seg_ref, base_ref, off_ref, phase_ref, end_ref, rt_ref, h_ref, xb_ref, res, zeros, partial,
                     sems, tail_sems):
    i = pl.program_id(0)
    nt = pl.num_programs(0)
    ne = tail_sems.shape[0]
    tm = h_ref.shape[0]
    mrows = res.shape[1]
    slot = i % 2
    copies = functools.partial(_segment_copies, seg_ref, xb_ref, base_ref, res, off_ref, sems, ne=ne, to_grouped=True)

    @pl.when(i == 0)
    def _():
        zeros[...] = jnp.zeros_like(zeros)
        partial[...] = jnp.zeros_like(partial)
        for cp in _tail_copies(end_ref, zeros, xb_ref, tail_sems, ne):
            cp.start()

    row = lax.broadcasted_iota(I32, (mrows, tm), 0)
    pos = rt_ref[0:2, :].astype(I32)
    perm = jnp.where((row == pos[0:1, :]) | (row == pos[1:2, :]), 1.0, 0.0).astype(BF16)
    res[slot] = _dot(perm, h_ref[...]).astype(BF16)

    tile_row = lax.broadcasted_iota(I32, (BF16_ROWS, res.shape[2]), 0)
    for e in range(ne):
        @pl.when(seg_ref[i * ne + e] > 0)
        def _():
            first = pl.ds(pl.multiple_of(off_ref[i * ne + e], BF16_ROWS), BF16_ROWS)
            merged = jnp.where(tile_row < phase_ref[i * ne + e], partial[e].astype(F32), res[slot, first, :].astype(F32))
            res[slot, first, :] = merged.astype(BF16)
            last = pl.ds(pl.multiple_of(off_ref[i * ne + e] + seg_ref[i * ne + e] - BF16_ROWS, BF16_ROWS), BF16_ROWS)
            partial[e] = res[slot, last, :]

    @pl.when(i > 0)
    def _():
        copies(tile=i - 1, slot=1 - slot, wait=True)

    copies(tile=i, slot=slot, wait=False)

    @pl.when(i == nt - 1)
    def _():
        copies(tile=i, slot=slot, wait=True)
        for cp in _tail_copies(end_ref, zeros, xb_ref, tail_sems, ne):
            cp.wait()


def _dispatch(h, rt_lane, lay, ne):
    n, dm = h.shape
    tm = MOE_TILE
    return pl.pallas_call(
        _dispatch_kernel,
        grid_spec=pltpu.PrefetchScalarGridSpec(
            num_scalar_prefetch=5, grid=(n // tm,),
            in_specs=[pl.BlockSpec((SUBLANES, tm), lambda i, *_: (0, i)), pl.BlockSpec((tm, dm), lambda i, *_: (i, 0))],
            out_specs=pl.BlockSpec(memory_space=pl.ANY),
            scratch_shapes=[pltpu.VMEM((2, _tile_rows(ne), dm), BF16), pltpu.VMEM((MOE_BLK, dm), BF16),
                            pltpu.VMEM((ne, BF16_ROWS, dm), BF16),
                            pltpu.SemaphoreType.DMA((2, ne)), pltpu.SemaphoreType.DMA((ne,))]),
        out_shape=jax.ShapeDtypeStruct((lay['rows_total'], dm), BF16),
        compiler_params=_cparams("arbitrary"),
        name="moe_dispatch",
    )(lay['seg'], lay['base'], lay['off'], lay['phase'], lay['end'], rt_lane, h)


def _combine_kernel(seg_ref, base_ref, off_ref, rt_ref, x_ref, gn_ref, yb_ref, o_ref, got, sems, *, ne):
    i = pl.program_id(0)
    steps = pl.num_programs(0)
    tm = MOE_TILE
    mrows = got.shape[1]
    copies = functools.partial(_segment_copies, seg_ref, yb_ref, base_ref, got, off_ref, sems, ne=ne, to_grouped=False)
    slot = lambda step, t: (step % 2) * COMBINE_TILES + t

    def fetch(step, wait):
        for t in range(COMBINE_TILES):
            copies(tile=step * COMBINE_TILES + t, slot=slot(step, t), wait=wait)

    @pl.when(i == 0)
    def _():
        got[...] = jnp.zeros_like(got)
        fetch(i, False)

    pl.when(i + 1 < steps)(lambda: fetch(i + 1, False))
    fetch(i, True)

    col = lax.broadcasted_iota(I32, (tm, mrows), 1)
    for t in range(COMBINE_TILES):
        tok = slice(t * tm, (t + 1) * tm)
        rt = rt_ref[tok, :]
        sel = jnp.concatenate(
            [jnp.where(col == rt[:, k:k + 1].astype(I32), 1.0, 0.0).astype(BF16) for k in range(TOP_K)], axis=0)
        picked = _dot(sel, got[slot(i, t)])
        y = sum(rt[:, TOP_K + k:TOP_K + k + 1] * picked[k * tm:(k + 1) * tm] for k in range(TOP_K))
        o_ref[tok, :] = _rms(x_ref[tok, :] + y, gn_ref[...])


def _combine(yb, rt_tok, x2, gn, lay, ne):
    n, dm = x2.shape
    tm = COMBINE_TILES * MOE_TILE
    mrows = _tile_rows(ne)
    tok = lambda w: pl.BlockSpec((tm, w), lambda i, *_: (i, 0))
    return pl.pallas_call(
        functools.partial(_combine_kernel, ne=ne),
        grid_spec=pltpu.PrefetchScalarGridSpec(
            num_scalar_prefetch=3, grid=(n // tm,),
            in_specs=[tok(LANES), tok(dm), pl.BlockSpec((1, dm), lambda i, *_: (0, 0)),
                      pl.BlockSpec(memory_space=pl.ANY)],
            out_specs=tok(dm),
            scratch_shapes=[pltpu.VMEM((2 * COMBINE_TILES, mrows, dm), BF16),
                            pltpu.SemaphoreType.DMA((2 * COMBINE_TILES, ne))]),
        out_shape=jax.ShapeDtypeStruct((n, dm), F32),
        compiler_params=_cparams("arbitrary"),
        name="moe_combine",
    )(lay['seg'], lay['base'], lay['off'], rt_tok, x2, gn.reshape(1, dm), yb)


def _moe_layout(cnt, n):
    nt, ne = cnt.shape
    blk = MOE_BLK
    whole = lambda a: a // BF16_ROWS * BF16_ROWS
    cap = -(-(n + BF16_ROWS + blk) // blk) * blk
    region = jnp.arange(ne, dtype=I32) * cap
    filled = jnp.cumsum(cnt, axis=0) - cnt
    phase = filled - whole(filled)
    span = jnp.where(cnt > 0, whole(phase + cnt + BF16_ROWS - 1), 0)
    base = (region[None, :] + filled - phase).astype(I32)
    off = (jnp.cumsum(span, axis=1) - span).astype(I32)
    rows_e = jnp.sum(cnt, axis=0)
    nblk_e = (rows_e + blk - 1) // blk
    end = (region + whole(rows_e + BF16_ROWS - 1)).astype(I32)
    ends = jnp.cumsum(nblk_e)
    max_blocks = TOP_K * n // blk + ne
    bid = jnp.arange(max_blocks + 1, dtype=I32)
    bexp = jnp.minimum(jnp.sum((bid[:, None] >= ends[None, :]).astype(I32), axis=1), ne - 1)
    brow = bexp * (cap // blk) + bid - (ends - nblk_e)[bexp]
    nblk = ends[-1]
    last = jnp.maximum(nblk - 1, 0)
    used = bid < nblk
    first = bid == (ends - nblk_e)[bexp]
    hand_over = (bid == ends[bexp] - 1) & (bid < last)
    role = jnp.where(used, first.astype(I32) + 2 * hand_over.astype(I32), 4).astype(I32)
    bexp = jnp.where(used, bexp, bexp[last]).astype(I32)
    brow = jnp.where(used, brow, brow[last]).astype(I32)
    flat = lambda a: a.reshape(-1)
    return dict(seg=flat(span.astype(I32)), base=flat(base), off=flat(off), phase=flat(phase.astype(I32)), end=end,
                brow=brow, bexp=bexp, role=role, rows_total=ne * cap, max_blocks=max_blocks)


def _moe(x2, h, logits, gn, w1, w3, w2):
    n, dm = x2.shape
    ne = w1.shape[0]
    rt_lane, rt_tok, cnt = _route(logits, ne)
    lay = _moe_layout(cnt[:, :, 0], n)
    xb = _dispatch(h, rt_lane, lay, ne)
    yb = _expert_ffn(xb, w1, w3, w2, lay['brow'], lay['bexp'], lay['role'], lay['max_blocks'])
    return _combine(yb, rt_tok, x2, gn, lay, ne)


def kernel(x, norm_mix_g, w_in, attn_rel_bias, rg_conv_w, rg_conv_b, rg_wx, rg_bx, rg_wa, rg_ba, rg_lambda, s5_a_re, s5_a_im, s5_log_dt, s5_b_re, s5_b_im, s5_c_re, s5_c_im, s5_d, s5_w_glu, g_group, w_out, norm_ffn_g, ffn_w1, ffn_w3, ffn_w2, moe_router, moe_w1, moe_w3, moe_w2, final_norm_g):
    bsz, l, dm = x.shape
    depth = w_in.shape[0]
    assert depth == 2 and l % ATTN_TQ == 0 and ATTN_TQ == N_PREV * CHUNK, "dense layer, then the MoE layer"
    d_rg = rg_conv_w.shape[2]
    d_s5 = s5_w_glu.shape[1]
    d_attn = (w_in.shape[2] - 2 * d_rg - d_s5) // 3
    x2 = x.reshape(bsz * l, dm)
    attn_bias = jax.vmap(_attn_bias)(attn_rel_bias)
    rg_gates_w = jnp.concatenate([jax.vmap(_block_diag)(rg_wx), jax.vmap(_block_diag)(rg_wa)], axis=2).astype(BF16)
    rg_gates_b = jnp.concatenate([rg_bx, rg_ba], axis=1).astype(F32)[:, None, :]
    rg_decay = jax.nn.softplus(-rg_lambda.astype(F32))[:, None, :]
    s5_prm = jax.vmap(_s5_params)(s5_a_re, s5_a_im, s5_log_dt, s5_b_re, s5_b_im, s5_c_re, s5_c_im, s5_d)
    for layer in range(depth):
        q, k, vt, xr, gate, us = _in_proj(x2, norm_mix_g[layer], w_in, layer, d_attn, d_rg, d_s5)
        y_attn = _attention(q, k, vt, attn_bias, layer, bsz, l)
        y_rg = _rglru(xr, gate, rg_conv_w.astype(F32), rg_conv_b.astype(F32)[:, None, :], rg_gates_w, rg_gates_b,
                      rg_decay, layer, bsz, l)
        y_s5 = _s5(us, s5_prm, layer, bsz, l)
        mixed = (y_attn, y_rg, y_s5, x2, g_group[layer].astype(F32), s5_w_glu, w_out, layer,
                 norm_ffn_g[layer].astype(F32))
        if layer == 0:
            x2 = _out_proj(*mixed, ffn=(ffn_w1[0], ffn_w3[0], ffn_w2[0]))
        else:
            x2, h, logits = _out_proj(*mixed, router=moe_router[0])
            x2 = _moe(x2, h, logits, final_norm_g.astype(F32), moe_w1[0], moe_w3[0], moe_w2[0])
    return x2.reshape(bsz, l, dm)
```

```python
import functools
import math

import jax
import jax.numpy as jnp
import numpy as np
from jax import lax
from jax.experimental import pallas as pl
from jax.experimental.pallas import tpu as pltpu

F32 = jnp.float32
BF16 = jnp.bfloat16
I32 = jnp.int32

EPS = 1e-6
CHUNK = 64
N_PREV = 8
BAND = (N_PREV + 1) * CHUNK
REL_CLIP = 128
HEAD_DIM = 64
RG_C = 8.0
RG_CONV_WIDTH = 4
S5_GROUP_DIM = 16
S5_T = 16
TOP_K = 2
NEG_BIG = -1e30

LANES = 128
SUBLANES = 8
BF16_ROWS = 16
VMEM_LIMIT = 52 * 1024 * 1024
MIX_VMEM_LIMIT = 58 * 1024 * 1024

ROW_TILE = 512
PROJ_TILE = 1024
ATTN_TQ = 512
ATTN_SPAN = 2 * CHUNK
ATTN_WIN = N_PREV * CHUNK + ATTN_SPAN
RG_TL = 1024
MOE_TILE = 256
MOE_BLK = 512
ROUTE_SPAN = 2048
SEG_PIECES = tuple(BF16_ROWS << b for b in range((MOE_TILE // BF16_ROWS).bit_length() - 1, -1, -1))
COMBINE_TILES = 4
SEG_LARGE = MOE_TILE // 2
FFN_CHUNKS = ((0, 512), (512, 512), (1024, 512), (1536, 512), (2048, 512), (2560, 256))


def _cparams(*sem):
    return pltpu.CompilerParams(dimension_semantics=sem, vmem_limit_bytes=VMEM_LIMIT)


def _const_spec(shape):
    nd = len(shape)
    return pl.BlockSpec(shape, lambda *_: (0,) * nd, pipeline_mode=pl.Buffered(1))


def _rms(xf, g):
    var = jnp.mean(xf * xf, axis=-1, keepdims=True)
    return xf * lax.rsqrt(var + EPS) * g


def _sigmoid(x):
    return 1.0 / (1.0 + jnp.exp(-x))


def _gelu(x):
    c = math.sqrt(2.0 / math.pi)
    return 0.5 * x * (1.0 + jnp.tanh(c * (x + 0.044715 * (x * x * x))))


def _dot(a, b):
    return jnp.dot(a, b, preferred_element_type=F32)


def _in_proj_kernel(x_ref, g_ref, w_ref, wvt_ref, q_ref, k_ref, vt_ref, xr_ref, gt_ref, us_ref, *, d_attn, d_rg):
    u = _rms(x_ref[...], g_ref[...]).astype(BF16)
    lo = 0
    for ref, width in ((q_ref, d_attn), (k_ref, d_attn), (None, d_attn),
                       (xr_ref, d_rg), (gt_ref, d_rg), (us_ref, w_ref.shape[1] - 3 * d_attn - 2 * d_rg)):
        if ref is not None:
            ref[...] = _dot(u, w_ref[:, lo:lo + width].astype(BF16)).astype(ref.dtype)
        lo += width
    vt_ref[...] = lax.dot_general(wvt_ref[...].astype(BF16), u, (((1,), (1,)), ((), ())),
                                  preferred_element_type=F32).astype(vt_ref.dtype)


def _layer_spec(stacked, layer):
    rest = stacked.shape[1:]
    return pl.BlockSpec((None,) + rest, lambda *_: (layer,) + (0,) * len(rest), pipeline_mode=pl.Buffered(1))


def _in_proj(x2, g, w_all, layer, d_attn, d_rg, d_s5):
    n, dm = x2.shape
    tm = min(PROJ_TILE, n)
    row = lambda w: pl.BlockSpec((tm, w), lambda i: (i, 0))
    wv_t = w_all[layer, :, 2 * d_attn:3 * d_attn].T
    return pl.pallas_call(
        functools.partial(_in_proj_kernel, d_attn=d_attn, d_rg=d_rg),
        grid=(n // tm,),
        in_specs=[row(dm), _const_spec((1, dm)), _layer_spec(w_all, layer), _const_spec(wv_t.shape)],
        out_specs=[row(d_attn), row(d_attn), pl.BlockSpec((d_attn, tm), lambda i: (0, i)),
                   row(d_rg), row(d_rg), row(d_s5)],
        out_shape=[jax.ShapeDtypeStruct((n, d_attn), BF16)] * 2 + [jax.ShapeDtypeStruct((d_attn, n), BF16)]
        + [jax.ShapeDtypeStruct((n, d_rg), F32)] * 2 + [jax.ShapeDtypeStruct((n, d_s5), F32)],
        compiler_params=_cparams("parallel"),
        name="in_proj",
    )(x2, g.reshape(1, dm), w_all, wv_t)


def _attn_kernel(q_ref, kp_ref, kc_ref, vp_ref, vc_ref, bias_ref, o_ref, kz, sc, *, n_pairs):
    qi = pl.program_id(1)
    tq = q_ref.shape[0]
    kz[0:tq, :] = kp_ref[...]
    kz[tq:2 * tq, :] = kc_ref[...]
    first = lax.broadcasted_iota(I32, (ATTN_SPAN, LANES), 1) < HEAD_DIM
    pairs = [slice(hp * LANES, (hp + 1) * LANES) for hp in range(n_pairs)]

    def score(j, masked):
        r0 = j * ATTN_SPAN
        for hp, cols in enumerate(pairs):
            q2 = q_ref[pl.ds(r0, ATTN_SPAN), cols] * jnp.asarray(HEAD_DIM ** -0.5, BF16)
            zero = jnp.zeros_like(q2)
            qq = jnp.concatenate([jnp.where(first, q2, zero), jnp.where(first, zero, q2)], axis=0)
            s = lax.dot_general(kz[pl.ds(r0, ATTN_WIN), cols], qq, (((1,), (1,)), ((), ())),
                                preferred_element_type=F32) + bias_ref[hp]
            if masked:
                key = lax.broadcasted_iota(I32, (ATTN_WIN, 2 * ATTN_SPAN), 0)
                s = jnp.where(key >= tq - r0, s, NEG_BIG)
            sc[j % 2, hp] = s

    def finish(j):
        r0 = j * ATTN_SPAN
        old = tq - r0
        probs = []
        for hp in range(n_pairs):
            s = sc[j % 2, hp]
            probs.append(jnp.exp((s - jnp.max(s, axis=0, keepdims=True)).astype(BF16)))
        for hp, (cols, p) in enumerate(zip(pairs, probs)):
            ones = lambda w: jnp.ones((BF16_ROWS, w), BF16)
            ov = (_dot(jnp.concatenate([vp_ref[cols, r0:tq], ones(old)], axis=0), p[0:old])
                  + _dot(jnp.concatenate([vc_ref[cols, 0:ATTN_WIN - old], ones(ATTN_WIN - old)], axis=0), p[old:]))
            o2 = (ov[0:LANES] / ov[LANES:LANES + 1]).T
            o_ref[pl.ds(r0, ATTN_SPAN), cols] = jnp.where(first, o2[0:ATTN_SPAN], o2[ATTN_SPAN:]).astype(o_ref.dtype)

    def run(masked):
        groups = tq // ATTN_SPAN
        score(0, masked)
        for j in range(groups):
            if j + 1 < groups:
                score(j + 1, masked)
            finish(j)

    pl.when(qi == 0)(functools.partial(run, True))
    pl.when(qi != 0)(functools.partial(run, False))


def _attention(q, k, vt, bias_t, layer, bsz, l):
    n, da = q.shape
    tq = ATTN_TQ
    nt = l // tq
    n_pairs = da // LANES
    cur = pl.BlockSpec((tq, da), lambda b, i: (b * nt + i, 0))
    prev = pl.BlockSpec((tq, da), lambda b, i: (b * nt + jnp.maximum(i - 1, 0), 0))
    cur_t = pl.BlockSpec((da, tq), lambda b, i: (0, b * nt + i))
    prev_t = pl.BlockSpec((da, tq), lambda b, i: (0, b * nt + jnp.maximum(i - 1, 0)))
    return pl.pallas_call(
        functools.partial(_attn_kernel, n_pairs=n_pairs),
        grid=(bsz, nt),
        in_specs=[cur, prev, cur, prev_t, cur_t, _layer_spec(bias_t, layer)],
        out_specs=cur,
        out_shape=jax.ShapeDtypeStruct((n, da), BF16),
        scratch_shapes=[pltpu.VMEM((2 * tq, da), BF16), pltpu.VMEM((2, n_pairs, ATTN_WIN, 2 * ATTN_SPAN), F32)],
        compiler_params=_cparams("parallel", "parallel"),
        name="chunk_attn",
    )(q, k, k, vt, vt, bias_t)


def _attn_bias(rel_bias):
    h = rel_bias.shape[0]
    tab = rel_bias.astype(F32)
    n_far = N_PREV * CHUNK - REL_CLIP + CHUNK
    lo = 2 * REL_CLIP - (BAND + CHUNK - 1 - n_far)
    ext = jnp.concatenate([jnp.broadcast_to(tab[:, 2 * REL_CLIP:], (h, n_far)), tab[:, lo:2 * REL_CLIP][:, ::-1]], axis=1)
    wide = BAND + CHUNK
    ring = jnp.concatenate([ext[:, CHUNK - 1:], jnp.zeros((h, 1), F32), ext[:, :CHUNK - 1]], axis=1)
    bias = jnp.broadcast_to(ring[:, None, :], (h, CHUNK, wide)).reshape(h, CHUNK * wide)
    bias = bias[:, :CHUNK * (wide - 1)].reshape(h, CHUNK, wide - 1)[..., :BAND]
    band_t = bias.transpose(0, 2, 1)
    groups = ATTN_SPAN // CHUNK
    per_chunk = [jnp.pad(band_t, ((0, 0), (c * CHUNK, (groups - 1 - c) * CHUNK), (0, 0)), constant_values=NEG_BIG)
                 for c in range(groups)]
    win = jnp.stack(per_chunk, axis=2)
    return win.reshape(h // 2, 2, ATTN_WIN, ATTN_SPAN).transpose(0, 2, 1, 3).reshape(h // 2, ATTN_WIN, 2 * ATTN_SPAN)


def _rglru_kernel(x_ref, gt_ref, cw_ref, cb_ref, w_ref, b_ref, sp_ref, o_ref, xpad, a_s, b_s, hcar):
    t = pl.program_id(1)
    tl, c = x_ref.shape
    front = SUBLANES

    @pl.when(t == 0)
    def _():
        xpad[0:front, :] = jnp.zeros((front, c), F32)
        hcar[...] = jnp.zeros_like(hcar)

    xpad[front:front + tl, :] = x_ref[...]
    xc = cb_ref[...] + sum(
        cw_ref[j:j + 1, :] * xpad[front - (RG_CONV_WIDTH - 1) + j:front - (RG_CONV_WIDTH - 1) + j + tl, :]
        for j in range(RG_CONV_WIDTH))
    xpad[0:front, :] = xpad[tl:tl + front, :]
    pre = _dot(xc.astype(BF16), w_ref[...]) + b_ref[...]
    gx = _sigmoid(pre[:, 0:c])
    ga = _sigmoid(pre[:, c:2 * c])
    log_a = -RG_C * ga * sp_ref[...]
    a = jnp.exp(log_a)
    mult = jnp.sqrt(-jnp.tanh(log_a) * (a * a + 1.0))
    b = mult * gx * xc
    nh = c // LANES
    for j in range(nh):
        a_s[j] = a[:, j * LANES:(j + 1) * LANES]
        b_s[j] = b[:, j * LANES:(j + 1) * LANES]

    step_in_vreg = lax.broadcasted_iota(I32, (SUBLANES, LANES), 0)

    def vreg_scan(r, h_in):
        rows = pl.ds(pl.multiple_of(r * SUBLANES, SUBLANES), SUBLANES)
        out = []
        for j in range(nh):
            av, bv = a_s[j, rows, :], b_s[j, rows, :]
            for d in (1, 2, 4):
                seen = step_in_vreg >= d
                a_prev = jnp.where(seen, pltpu.roll(av, d, 0), 1.0)
                b_prev = jnp.where(seen, pltpu.roll(bv, d, 0), 0.0)
                bv = bv + av * b_prev
                av = av * a_prev
            h = bv + av * h_in[j]
            b_s[j, rows, :] = h
            out.append(h[SUBLANES - 1:SUBLANES, :])
        return tuple(out)

    h_last = lax.fori_loop(0, tl // SUBLANES, vreg_scan,
                           tuple(hcar[:, j * LANES:(j + 1) * LANES] for j in range(nh)), unroll=4)
    for j in range(nh):
        cols = slice(j * LANES, (j + 1) * LANES)
        hcar[:, cols] = h_last[j]
        o_ref[:, cols] = (b_s[j] * _gelu(gt_ref[:, cols])).astype(o_ref.dtype)


def _rglru(xr, gate, conv_w, conv_b, w_bd, b_cat, sp, layer, bsz, l):
    n, c = xr.shape
    tl = min(RG_TL, l)
    nt = l // tl
    row = pl.BlockSpec((tl, c), lambda b, t: (b * nt + t, 0))
    return pl.pallas_call(
        _rglru_kernel,
        grid=(bsz, nt),
        in_specs=[row, row] + [_layer_spec(a, layer) for a in (conv_w, conv_b, w_bd, b_cat, sp)],
        out_specs=row,
        out_shape=jax.ShapeDtypeStruct((n, c), BF16),
        scratch_shapes=[pltpu.VMEM((tl + SUBLANES, c), F32), pltpu.VMEM((c // LANES, tl, LANES), F32),
                        pltpu.VMEM((c // LANES, tl, LANES), F32), pltpu.VMEM((1, c), F32)],
        compiler_params=_cparams("parallel", "arbitrary"),
        name="rglru",
    )(xr, gate, conv_w, conv_b, w_bd, b_cat, sp)


def _block_diag(w):
    nb, d, _ = w.shape
    eye = jnp.eye(nb, dtype=w.dtype)
    return (eye[:, None, :, None] * w[:, :, None, :]).reshape(nb * d, nb * d)


def _s5_kernel(u_ref, m_ref, wab_ref, wo_ref, c1_ref, c2a_ref, c2b_ref, d_ref, y_ref, ps, ug, xa_s, xb_s):
    t, gd = S5_T, S5_GROUP_DIM
    ng, nk, _ = ug.shape
    nh = ps.shape[0]
    per_half = LANES // gd
    rows8 = SUBLANES

    for h in range(nh):
        ps[h] = u_ref[:, h * LANES:(h + 1) * LANES]

    lane_block = lax.broadcasted_iota(I32, (rows8, LANES), 1) // gd

    def block_transpose(vs):
        d = per_half // 2
        while d:
            keep = (lane_block & d) == 0
            out = list(vs)
            for i in range(per_half):
                if i & d == 0:
                    out[i] = jnp.where(keep, vs[i], pltpu.roll(vs[i + d], d * gd, 1))
                    out[i + d] = jnp.where(keep, pltpu.roll(vs[i], LANES - d * gd, 1), vs[i + d])
            vs, d = out, d // 2
        return vs

    def to_groups(r, carry):
        base = pl.multiple_of(r * rows8 * t, rows8 * t)
        rows = pl.ds(pl.multiple_of(r * rows8, rows8), rows8)
        for h in range(nh):
            for j in range(t // per_half):
                steps = [ps[h, pl.ds(base + j * per_half + i, rows8, stride=t), :] for i in range(per_half)]
                for k, v in enumerate(block_transpose(steps)):
                    ug[h * per_half + k, rows, j * LANES:(j + 1) * LANES] = v
        return carry

    lax.fori_loop(0, nk // rows8, to_groups, 0, unroll=2)

    def project(g, carry):
        u = ug[g]
        ub = u.astype(BF16)
        rows = pl.ds(pl.multiple_of(g * nk, nk), nk)
        xab = _dot(ub, wab_ref[g])
        xa_s[rows, :] = xab[:, 0:LANES]
        xb_s[rows, :] = xab[:, LANES:2 * LANES]
        ug[g] = _dot(ub, m_ref[g]) + d_ref[g] * u
        return carry

    lax.fori_loop(0, ng, project, 0, unroll=2)

    c1, c2a, c2b = c1_ref[...], c2a_ref[...], c2b_ref[...]

    def step(k, carry):
        xa, xb = carry
        rows = pl.ds(k, ng, stride=nk)
        ia = xa_s[rows, :]
        ib = xb_s[rows, :]
        xa_s[rows, :] = xa
        return c1 * xa + c2a * xb + ia, c1 * xb + c2b * xa + ib

    z = jnp.zeros((ng, xa_s.shape[1]), F32)
    lax.fori_loop(0, nk, step, (z, z), unroll=4)

    def respond(g, carry):
        rows = pl.ds(pl.multiple_of(g * nk, nk), nk)
        ug[g] = _gelu(ug[g] + _dot(xa_s[rows, :].astype(BF16), wo_ref[g]))
        return carry

    lax.fori_loop(0, ng, respond, 0, unroll=2)

    def from_groups(r, carry):
        base = pl.multiple_of(r * rows8 * t, rows8 * t)
        rows = pl.ds(pl.multiple_of(r * rows8, rows8), rows8)
        for h in range(nh):
            for j in range(t // per_half):
                groups = [ug[h * per_half + k, rows, j * LANES:(j + 1) * LANES] for k in range(per_half)]
                for i, v in enumerate(block_transpose(groups)):
                    ps[h, pl.ds(base + j * per_half + i, rows8, stride=t), :] = v
        return carry

    lax.fori_loop(0, nk // rows8, from_groups, 0, unroll=2)
    for h in range(nh):
        y_ref[:, h * LANES:(h + 1) * LANES] = ps[h].astype(y_ref.dtype)


def _s5(us, prm, layer, bsz, l):
    n, w = us.shape
    ng = w // S5_GROUP_DIM
    nk = l // S5_T
    p2 = prm[2].shape[-2]
    row = pl.BlockSpec((l, w), lambda b: (b, 0))
    return pl.pallas_call(
        _s5_kernel,
        grid=(bsz,),
        in_specs=[row] + [_layer_spec(a, layer) for a in prm],
        out_specs=row,
        out_shape=jax.ShapeDtypeStruct((n, w), BF16),
        scratch_shapes=[pltpu.VMEM((w // LANES, l, LANES), F32), pltpu.VMEM((ng, nk, S5_T * S5_GROUP_DIM), F32),
                        pltpu.VMEM((ng * nk, p2), F32), pltpu.VMEM((ng * nk, p2), F32)],
        compiler_params=_cparams("parallel"),
        name="s5",
    )(us, *prm)


def _s5_params(a_re, a_im, log_dt, b_re, b_im, c_re, c_im, d):
    t = S5_T
    g, p = a_re.shape
    gd = S5_GROUP_DIM
    hi = lax.Precision.HIGHEST
    a = lax.complex(a_re.astype(F32), a_im.astype(F32))
    dt = jnp.exp(log_dt.astype(F32))[:, None]
    a_bar = jnp.exp(a * dt)
    bm = lax.complex(b_re.astype(F32), b_im.astype(F32))
    cm = lax.complex(c_re.astype(F32), c_im.astype(F32))
    b_bar = ((a_bar - 1.0) / a)[..., None] * bm
    steps = jnp.arange(t + 1, dtype=F32)
    pw = jnp.exp((a * dt)[:, None, :] * steps[None, :, None])

    cp = (cm[:, None, :, :] * pw[:, :, None, :]).transpose(0, 3, 1, 2)
    w = t * gd
    lag = cp[:, :, :t].reshape(g, p, w)
    k2 = jnp.einsum('gpc,gpx->gcx', jnp.concatenate([b_bar.real, -b_bar.imag], axis=1),
                    jnp.concatenate([lag.real, lag.imag], axis=1), precision=hi)
    shift = np.zeros((t, w, w), np.float32)
    for s in range(t):
        shift[s, np.arange(w - s * gd), np.arange(w - s * gd) + s * gd] = 1.0
    m = jnp.einsum('gcx,sxl->gscl', k2, shift, precision=hi).reshape(g, w, w)
    win = pw[:, :t][:, ::-1][:, :, :, None] * b_bar[:, None]
    win = win.transpose(0, 1, 3, 2).reshape(g, w, p)
    wab = jnp.concatenate([win.real, win.imag, win.imag, win.real], axis=-1)
    co = cp[:, :, 1:].reshape(g, p, w)
    wo = jnp.concatenate([co.real, -co.imag], axis=1)
    lt = pw[:, t]
    c1 = jnp.concatenate([lt.real, lt.real], axis=-1)
    c2a = jnp.concatenate([-lt.imag, lt.imag], axis=-1)
    c2b = jnp.concatenate([lt.imag, -lt.imag], axis=-1)
    dtile = jnp.tile(d.astype(F32).reshape(g, 1, gd), (1, t, 1)).reshape(g, 1, t * gd)
    return (m.astype(BF16), wab.astype(BF16), wo.astype(BF16), c1, c2a, c2b, dtile)


def _out_proj_kernel(ya_ref, yr_ref, ys_ref, x_ref, gg_ref, wglu_ref, wo_ref, gf_ref, *rest, dense, chunks):
    ys = ys_ref[...]
    ys = ys.astype(F32) * _sigmoid(_dot(ys.astype(BF16), wglu_ref[...].astype(BF16)))
    acc = x_ref[...]
    lo = 0
    for y in (ya_ref[...].astype(F32), yr_ref[...].astype(F32), ys):
        w = y.shape[1]
        acc = acc + _dot(_rms(y, gg_ref[:, lo:lo + w]).astype(BF16), wo_ref[lo:lo + w, :].astype(BF16))
        lo += w
    h = _rms(acc, gf_ref[...])
    hb = h.astype(BF16)
    if dense:
        w1_ref, w3_ref, w2_ref, xo_ref = rest
        xo_ref[...] = acc + _swiglu_chunks(hb, w1_ref, w3_ref, w2_ref, chunks)
        return
    rt_ref, xo_ref, h_ref, lg_ref = rest
    xo_ref[...] = acc
    h_ref[...] = hb
    both = _dot(hb, rt_ref[...])
    h_lo = (h - hb.astype(F32)).astype(BF16)
    lg_ref[...] = both[:, 0:LANES] + both[:, LANES:2 * LANES] + _dot(h_lo, rt_ref[:, 0:LANES])


def _out_proj(ya, yr, ys, x2, gg, wglu_all, wo_all, layer, gf, *, ffn=None, router=None):
    n, dm = x2.shape
    tm = min(ROW_TILE, n)
    row = lambda w: pl.BlockSpec((tm, w), lambda i: (i, 0))
    ins = [ya, yr, ys, x2, gg.reshape(1, -1), wglu_all, wo_all, gf.reshape(1, dm)]
    in_specs = [row(ya.shape[1]), row(yr.shape[1]), row(ys.shape[1]), row(dm), _const_spec((1, gg.shape[0])),
                _layer_spec(wglu_all, layer), _layer_spec(wo_all, layer), _const_spec((1, dm))]
    if ffn is not None:
        ins += list(ffn)
        in_specs += [_const_spec(w.shape) for w in ffn]
        out_specs = row(dm)
        out_shape = jax.ShapeDtypeStruct((n, dm), F32)
        chunks = _ffn_chunks(ffn[0].shape[1])
    else:
        ne = router.shape[1]
        r_hi = router.astype(BF16)
        r_lo = (router.astype(F32) - r_hi.astype(F32)).astype(BF16)
        pad = lambda a: jnp.pad(a, ((0, 0), (0, LANES - ne)))
        ins.append(jnp.concatenate([pad(r_hi), pad(r_lo)], axis=1))
        in_specs.append(_const_spec((dm, 2 * LANES)))
        out_specs = [row(dm), row(dm), row(LANES)]
        out_shape = [jax.ShapeDtypeStruct((n, dm), F32), jax.ShapeDtypeStruct((n, dm), BF16),
                     jax.ShapeDtypeStruct((n, LANES), F32)]
        chunks = None
    return pl.pallas_call(
        functools.partial(_out_proj_kernel, dense=ffn is not None, chunks=chunks),
        grid=(n // tm,),
        in_specs=in_specs, out_specs=out_specs, out_shape=out_shape,
        compiler_params=pltpu.CompilerParams(dimension_semantics=("parallel",), vmem_limit_bytes=MIX_VMEM_LIMIT),
        name="out_proj",
    )(*ins)


def _swiglu_chunks(h, w1, w3, w2, chunks, before_chunk=None, after_chunk=None):
    acc = None
    for j, (lo, width) in enumerate(chunks):
        if before_chunk is not None:
            before_chunk(j)
        a = _dot(h, w1[:, lo:lo + width].astype(BF16))
        b = _dot(h, w3[:, lo:lo + width].astype(BF16))
        t = (a * _sigmoid(a) * b).astype(BF16)
        y = _dot(t, w2[lo:lo + width, :].astype(BF16))
        acc = y if acc is None else acc + y
        if after_chunk is not None:
            after_chunk(j)
    return acc


def _ffn_chunks(f):
    if f == sum(w for _, w in FFN_CHUNKS):
        return FFN_CHUNKS
    return ((0, f),)


def _expert_weight_copies(hbm, vmem, sems, chunks, expert, j):
    h1, h3, h2 = hbm
    v1, v3, v2 = vmem
    lo, w = chunks[j]
    return [pltpu.make_async_copy(h1.at[expert, :, lo:lo + w], v1.at[:, lo:lo + w], sems.at[j, 0]),
            pltpu.make_async_copy(h3.at[expert, :, lo:lo + w], v3.at[:, lo:lo + w], sems.at[j, 1]),
            pltpu.make_async_copy(h2.at[expert, lo:lo + w, :], v2.at[lo:lo + w, :], sems.at[j, 2])]


def _expert_ffn_kernel(brow_ref, bexp_ref, role_ref, x_ref, w1_hbm, w3_hbm, w2_hbm, o_ref, w1_v, w3_v, w2_v, sems,
                       *, chunks):
    i = pl.program_id(0)
    copies = functools.partial(_expert_weight_copies, (w1_hbm, w3_hbm, w2_hbm), (w1_v, w3_v, w2_v), sems, chunks)

    @pl.when(i == 0)
    def _():
        for j in range(len(chunks)):
            for cp in copies(bexp_ref[0], j):
                cp.start()

    def block(first, last):
        def before(j):
            for cp in copies(bexp_ref[i], j):
                cp.wait()

        def after(j):
            for cp in copies(bexp_ref[i + 1], j):
                cp.start()

        o_ref[...] = _swiglu_chunks(x_ref[...], w1_v, w3_v, w2_v, chunks, before if first else None,
                                    after if last else None).astype(o_ref.dtype)

    for role in range(4):
        pl.when(role_ref[i] == role)(functools.partial(block, bool(role & 1), bool(role & 2)))


def _expert_ffn(xb, w1, w3, w2, blk_row, blk_exp, role, max_blocks):
    rows, dm = xb.shape
    f = w1.shape[2]
    chunks = _ffn_chunks(f)
    row = pl.BlockSpec((MOE_BLK, dm), lambda i, br, be, ro: (br[i], 0))
    hbm = pl.BlockSpec(memory_space=pl.ANY)
    return pl.pallas_call(
        functools.partial(_expert_ffn_kernel, chunks=chunks),
        grid_spec=pltpu.PrefetchScalarGridSpec(
            num_scalar_prefetch=3, grid=(max_blocks,),
            in_specs=[row, hbm, hbm, hbm], out_specs=row,
            scratch_shapes=[pltpu.VMEM(w1.shape[1:], w1.dtype), pltpu.VMEM(w3.shape[1:], w3.dtype),
                            pltpu.VMEM(w2.shape[1:], w2.dtype), pltpu.SemaphoreType.DMA((len(chunks), 3))]),
        out_shape=jax.ShapeDtypeStruct((rows, dm), BF16),
        compiler_params=_cparams("arbitrary"),
        name="expert_ffn",
    )(blk_row, blk_exp, role, xb, w1, w3, w2)


def _route_tile(logits_tok, filled, ne):
    tm = logits_tok.shape[0]
    logits = logits_tok.T[0:ne, :]
    eidx = lax.broadcasted_iota(I32, (ne, tm), 0)
    m1 = jnp.max(logits, axis=0, keepdims=True)
    i1 = jnp.min(jnp.where(logits == m1, eidx, ne), axis=0, keepdims=True)
    sel1 = eidx == i1
    rest = jnp.where(sel1, -jnp.inf, logits)
    m2 = jnp.max(rest, axis=0, keepdims=True)
    i2 = jnp.min(jnp.where(rest == m2, eidx, ne), axis=0, keepdims=True)
    sel2 = eidx == i2
    e2 = jnp.exp(m2 - m1)
    den = 1.0 + e2
    rf = jnp.where(sel1 | sel2, 1.0, 0.0)
    cnt = jnp.sum(rf, axis=1, keepdims=True)
    before = (lax.broadcasted_iota(I32, (tm, tm), 0) < lax.broadcasted_iota(I32, (tm, tm), 1))
    rank = _dot(rf.astype(BF16), jnp.where(before, 1.0, 0.0).astype(BF16))
    whole = lambda a: jnp.floor(a * (1.0 / BF16_ROWS)) * BF16_ROWS
    phase = filled - whole(filled)
    span = jnp.where(cnt > 0, whole(phase + cnt + (BF16_ROWS - 1)), 0.0)
    ecol = lax.broadcasted_iota(I32, (ne, 1), 0)
    off = jnp.zeros((ne, 1), F32)
    for j in range(ne - 1):
        off = off + jnp.where(ecol > j, span[j:j + 1, :], 0.0)
    place = off + phase + rank
    pos1 = jnp.sum(jnp.where(sel1, place, 0.0), axis=0, keepdims=True)
    pos2 = jnp.sum(jnp.where(sel2, place, 0.0), axis=0, keepdims=True)
    record = [pos1, pos2, 1.0 / den, e2 / den]
    lane_major = jnp.concatenate(record + [jnp.zeros((SUBLANES - len(record), tm), F32)], axis=0)
    tok_major = jnp.concatenate([lane_major, jnp.zeros((LANES - SUBLANES, tm), F32)], axis=0).T
    return lane_major, tok_major, cnt


def _route_kernel(lg_ref, lane_ref, tok_ref, cnt_ref, filled, *, ne):
    tm = MOE_TILE

    @pl.when(pl.program_id(0) == 0)
    def _():
        filled[...] = jnp.zeros_like(filled)

    for t in range(lg_ref.shape[0] // tm):
        lane_major, tok_major, cnt = _route_tile(lg_ref[t * tm:(t + 1) * tm, :], filled[...], ne)
        lane_ref[:, t * tm:(t + 1) * tm] = lane_major
        tok_ref[t * tm:(t + 1) * tm, :] = tok_major
        cnt_ref[t] = jnp.broadcast_to(cnt, (ne, LANES)).astype(I32)
        filled[...] = filled[...] + cnt


def _route(logits, ne):
    n = logits.shape[0]
    tm = MOE_TILE
    span = min(ROUTE_SPAN, n)
    return pl.pallas_call(
        functools.partial(_route_kernel, ne=ne),
        grid=(n // span,),
        in_specs=[pl.BlockSpec((span, LANES), lambda i: (i, 0))],
        out_specs=[pl.BlockSpec((SUBLANES, span), lambda i: (0, i)), pl.BlockSpec((span, LANES), lambda i: (i, 0)),
                   pl.BlockSpec((span // tm, ne, LANES), lambda i: (i, 0, 0))],
        out_shape=[jax.ShapeDtypeStruct((SUBLANES, n), F32), jax.ShapeDtypeStruct((n, LANES), F32),
                   jax.ShapeDtypeStruct((n // tm, ne, LANES), I32)],
        scratch_shapes=[pltpu.VMEM((ne, 1), F32)],
        compiler_params=_cparams("arbitrary"),
        name="moe_route",
    )(logits)


def _segment_copies(seg_ref, grouped_ref, base_ref, tile_buf, off_ref, sems, tile, slot, ne, *, to_grouped, wait):
    large = [p for p in SEG_PIECES if p >= SEG_LARGE]
    small = [p for p in SEG_PIECES if p < SEG_LARGE]
    for e in range(ne):
        seg = seg_ref[tile * ne + e]

        def pieces(sizes, done, seg=seg, e=e):
            far = base_ref[tile * ne + e] + done
            near = off_ref[tile * ne + e] + done
            for piece in sizes:
                g = grouped_ref.at[pl.ds(pl.multiple_of(far, BF16_ROWS), piece)]
                t = tile_buf.at[slot, pl.ds(pl.multiple_of(near, BF16_ROWS), piece)]
                cp = pltpu.make_async_copy(t, g, sems.at[slot, e]) if to_grouped else \
                    pltpu.make_async_copy(g, t, sems.at[slot, e])
                has = (seg & piece) != 0

                @pl.when(has)
                def _():
                    cp.wait() if wait else cp.start()

                step = jnp.where(has, piece, 0)
                far = far + step
                near = near + step

        pl.when(seg >= SEG_LARGE)(functools.partial(pieces, large, 0))
        pieces(small, seg & -SEG_LARGE)


def _tile_rows(ne):
    return TOP_K * MOE_TILE + ne * 2 * BF16_ROWS


def _tail_copies(end_ref, zeros, xb_ref, sems, ne):
    return [pltpu.make_async_copy(zeros, xb_ref.at[pl.ds(pl.multiple_of(end_ref[e], BF16_ROWS), MOE_BLK)], sems.at[e])
            for e in range(ne)]


def _dispatch_kernel(seg_ref, base_ref, off_ref, phase_ref, end_ref, rt_ref, h_ref, xb_ref, res, zeros, partial,
                     sems, tail_sems):
    i = pl.program_id(0)
    nt = pl.num_programs(0)
    ne = tail_sems.shape[0]
    tm = h_ref.shape[0]
    mrows = res.shape[1]
    slot = i % 2
    copies = functools.partial(_segment_copies, seg_ref, xb_ref, base_ref, res, off_ref, sems, ne=ne, to_grouped=True)

    @pl.when(i == 0)
    def _():
        zeros[...] = jnp.zeros_like(zeros)
        partial[...] = jnp.zeros_like(partial)
        for cp in _tail_copies(end_ref, zeros, xb_ref, tail_sems, ne):
            cp.start()

    row = lax.broadcasted_iota(I32, (mrows, tm), 0)
    pos = rt_ref[0:2, :].astype(I32)
    perm = jnp.where((row == pos[0:1, :]) | (row == pos[1:2, :]), 1.0, 0.0).astype(BF16)
    res[slot] = _dot(perm, h_ref[...]).astype(BF16)

    tile_row = lax.broadcasted_iota(I32, (BF16_ROWS, res.shape[2]), 0)
    for e in range(ne):
        @pl.when(seg_ref[i * ne + e] > 0)
        def _():
            first = pl.ds(pl.multiple_of(off_ref[i * ne + e], BF16_ROWS), BF16_ROWS)
            merged = jnp.where(tile_row < phase_ref[i * ne + e], partial[e].astype(F32), res[slot, first, :].astype(F32))
            res[slot, first, :] = merged.astype(BF16)
            last = pl.ds(pl.multiple_of(off_ref[i * ne + e] + seg_ref[i * ne + e] - BF16_ROWS, BF16_ROWS), BF16_ROWS)
            partial[e] = res[slot, last, :]

    @pl.when(i > 0)
    def _():
        copies(tile=i - 1, slot=1 - slot, wait=True)

    copies(tile=i, slot=slot, wait=False)

    @pl.when(i == nt - 1)
    def _():
        copies(tile=i, slot=slot, wait=True)
        for cp in _tail_copies(end_ref, zeros, xb_ref, tail_sems, ne):
            cp.wait()


def _dispatch(h, rt_lane, lay, ne):
    n, dm = h.shape
    tm = MOE_TILE
    return pl.pallas_call(
        _dispatch_kernel,
        grid_spec=pltpu.PrefetchScalarGridSpec(
            num_scalar_prefetch=5, grid=(n // tm,),
            in_specs=[pl.BlockSpec((SUBLANES, tm), lambda i, *_: (0, i)), pl.BlockSpec((tm, dm), lambda i, *_: (i, 0))],
            out_specs=pl.BlockSpec(memory_space=pl.ANY),
            scratch_shapes=[pltpu.VMEM((2, _tile_rows(ne), dm), BF16), pltpu.VMEM((MOE_BLK, dm), BF16),
                            pltpu.VMEM((ne, BF16_ROWS, dm), BF16),
                            pltpu.SemaphoreType.DMA((2, ne)), pltpu.SemaphoreType.DMA((ne,))]),
        out_shape=jax.ShapeDtypeStruct((lay['rows_total'], dm), BF16),
        compiler_params=_cparams("arbitrary"),
        name="moe_dispatch",
    )(lay['seg'], lay['base'], lay['off'], lay['phase'], lay['end'], rt_lane, h)


def _combine_kernel(seg_ref, base_ref, off_ref, rt_ref, x_ref, gn_ref, yb_ref, o_ref, got, sems, *, ne):
    i = pl.program_id(0)
    steps = pl.num_programs(0)
    tm = MOE_TILE
    mrows = got.shape[1]
    copies = functools.partial(_segment_copies, seg_ref, yb_ref, base_ref, got, off_ref, sems, ne=ne, to_grouped=False)
    slot = lambda step, t: (step % 2) * COMBINE_TILES + t

    def fetch(step, wait):
        for t in range(COMBINE_TILES):
            copies(tile=step * COMBINE_TILES + t, slot=slot(step, t), wait=wait)

    @pl.when(i == 0)
    def _():
        got[...] = jnp.zeros_like(got)
        fetch(i, False)

    pl.when(i + 1 < steps)(lambda: fetch(i + 1, False))
    fetch(i, True)

    col = lax.broadcasted_iota(I32, (tm, mrows), 1)
    for t in range(COMBINE_TILES):
        tok = slice(t * tm, (t + 1) * tm)
        rt = rt_ref[tok, :]
        sel = jnp.concatenate(
            [jnp.where(col == rt[:, k:k + 1].astype(I32), 1.0, 0.0).astype(BF16) for k in range(TOP_K)], axis=0)
        picked = _dot(sel, got[slot(i, t)])
        y = sum(rt[:, TOP_K + k:TOP_K + k + 1] * picked[k * tm:(k + 1) * tm] for k in range(TOP_K))
        o_ref[tok, :] = _rms(x_ref[tok, :] + y, gn_ref[...])


def _combine(yb, rt_tok, x2, gn, lay, ne):
    n, dm = x2.shape
    tm = COMBINE_TILES * MOE_TILE
    mrows = _tile_rows(ne)
    tok = lambda w: pl.BlockSpec((tm, w), lambda i, *_: (i, 0))
    return pl.pallas_call(
        functools.partial(_combine_kernel, ne=ne),
        grid_spec=pltpu.PrefetchScalarGridSpec(
            num_scalar_prefetch=3, grid=(n // tm,),
            in_specs=[tok(LANES), tok(dm), pl.BlockSpec((1, dm), lambda i, *_: (0, 0)),
                      pl.BlockSpec(memory_space=pl.ANY)],
            out_specs=tok(dm),
            scratch_shapes=[pltpu.VMEM((2 * COMBINE_TILES, mrows, dm), BF16),
                            pltpu.SemaphoreType.DMA((2 * COMBINE_TILES, ne))]),
        out_shape=jax.ShapeDtypeStruct((n, dm), F32),
        compiler_params=_cparams("arbitrary"),
        name="moe_combine",
    )(lay['seg'], lay['base'], lay['off'], rt_tok, x2, gn.reshape(1, dm), yb)


def _moe_layout(cnt, n):
    nt, ne = cnt.shape
    blk = MOE_BLK
    whole = lambda a: a // BF16_ROWS * BF16_ROWS
    cap = -(-(n + BF16_ROWS + blk) // blk) * blk
    region = jnp.arange(ne, dtype=I32) * cap
    filled = jnp.cumsum(cnt, axis=0) - cnt
    phase = filled - whole(filled)
    span = jnp.where(cnt > 0, whole(phase + cnt + BF16_ROWS - 1), 0)
    base = (region[None, :] + filled - phase).astype(I32)
    off = (jnp.cumsum(span, axis=1) - span).astype(I32)
    rows_e = jnp.sum(cnt, axis=0)
    nblk_e = (rows_e + blk - 1) // blk
    end = (region + whole(rows_e + BF16_ROWS - 1)).astype(I32)
    ends = jnp.cumsum(nblk_e)
    max_blocks = TOP_K * n // blk + ne
    bid = jnp.arange(max_blocks + 1, dtype=I32)
    bexp = jnp.minimum(jnp.sum((bid[:, None] >= ends[None, :]).astype(I32), axis=1), ne - 1)
    brow = bexp * (cap // blk) + bid - (ends - nblk_e)[bexp]
    nblk = ends[-1]
    last = jnp.maximum(nblk - 1, 0)
    used = bid < nblk
    first = bid == (ends - nblk_e)[bexp]
    hand_over = (bid == ends[bexp] - 1) & (bid < last)
    role = jnp.where(used, first.astype(I32) + 2 * hand_over.astype(I32), 4).astype(I32)
    bexp = jnp.where(used, bexp, bexp[last]).astype(I32)
    brow = jnp.where(used, brow, brow[last]).astype(I32)
    flat = lambda a: a.reshape(-1)
    return dict(seg=flat(span.astype(I32)), base=flat(base), off=flat(off), phase=flat(phase.astype(I32)), end=end,
                brow=brow, bexp=bexp, role=role, rows_total=ne * cap, max_blocks=max_blocks)


def _moe(x2, h, logits, gn, w1, w3, w2):
    n, dm = x2.shape
    ne = w1.shape[0]
    rt_lane, rt_tok, cnt = _route(logits, ne)
    lay = _moe_layout(cnt[:, :, 0], n)
    xb = _dispatch(h, rt_lane, lay, ne)
    yb = _expert_ffn(xb, w1, w3, w2, lay['brow'], lay['bexp'], lay['role'], lay['max_blocks'])
    return _combine(yb, rt_tok, x2, gn, lay, ne)


def kernel(x, norm_mix_g, w_in, attn_rel_bias, rg_conv_w, rg_conv_b, rg_wx, rg_bx, rg_wa, rg_ba, rg_lambda, s5_a_re, s5_a_im, s5_log_dt, s5_b_re, s5_b_im, s5_c_re, s5_c_im, s5_d, s5_w_glu, g_group, w_out, norm_ffn_g, ffn_w1, ffn_w3, ffn_w2, moe_router, moe_w1, moe_w3, moe_w2, final_norm_g):
    bsz, l, dm = x.shape
    depth = w_in.shape[0]
    assert depth == 2 and l % ATTN_TQ == 0 and ATTN_TQ == N_PREV * CHUNK, "dense layer, then the MoE layer"
    d_rg = rg_conv_w.shape[2]
    d_s5 = s5_w_glu.shape[1]
    d_attn = (w_in.shape[2] - 2 * d_rg - d_s5) // 3
    x2 = x.reshape(bsz * l, dm)
    attn_bias = jax.vmap(_attn_bias)(attn_rel_bias)
    rg_gates_w = jnp.concatenate([jax.vmap(_block_diag)(rg_wx), jax.vmap(_block_diag)(rg_wa)], axis=2).astype(BF16)
    rg_gates_b = jnp.concatenate([rg_bx, rg_ba], axis=1).astype(F32)[:, None, :]
    rg_decay = jax.nn.softplus(-rg_lambda.astype(F32))[:, None, :]
    s5_prm = jax.vmap(_s5_params)(s5_a_re, s5_a_im, s5_log_dt, s5_b_re, s5_b_im, s5_c_re, s5_c_im, s5_d)
    for layer in range(depth):
        q, k, vt, xr, gate, us = _in_proj(x2, norm_mix_g[layer], w_in, layer, d_attn, d_rg, d_s5)
        y_attn = _attention(q, k, vt, attn_bias, layer, bsz, l)
        y_rg = _rglru(xr, gate, rg_conv_w.astype(F32), rg_conv_b.astype(F32)[:, None, :], rg_gates_w, rg_gates_b,
                      rg_decay, layer, bsz, l)
        y_s5 = _s5(us, s5_prm, layer, bsz, l)
        mixed = (y_attn, y_rg, y_s5, x2, g_group[layer].astype(F32), s5_w_glu, w_out, layer,
                 norm_ffn_g[layer].astype(F32))
        if layer == 0:
            x2 = _out_proj(*mixed, ffn=(ffn_w1[0], ffn_w3[0], ffn_w2[0]))
        else:
            x2, h, logits = _out_proj(*mixed, router=moe_router[0])
            x2 = _moe(x2, h, logits, final_norm_g.astype(F32), moe_w1[0], moe_w3[0], moe_w2[0])
    return x2.reshape(bsz, l, dm)
```

```python
import functools
import math

import jax
import jax.numpy as jnp
import numpy as np
from jax import lax
from jax.experimental import pallas as pl
from jax.experimental.pallas import tpu as pltpu

F32 = jnp.float32
BF16 = jnp.bfloat16
I32 = jnp.int32

EPS = 1e-6
CHUNK = 64
N_PREV = 8
BAND = (N_PREV + 1) * CHUNK
REL_CLIP = 128
HEAD_DIM = 64
RG_C = 8.0
RG_CONV_WIDTH = 4
S5_GROUP_DIM = 16
S5_T = 16
TOP_K = 2
NEG_BIG = -1e30

LANES = 128
SUBLANES = 8
BF16_ROWS = 16
VMEM_LIMIT = 52 * 1024 * 1024
MIX_VMEM_LIMIT = 58 * 1024 * 1024

ROW_TILE = 512
PROJ_TILE = 1024
ATTN_TQ = 512
ATTN_SPAN = 2 * CHUNK
ATTN_WIN = N_PREV * CHUNK + ATTN_SPAN
RG_TL = 1024
MOE_TILE = 256
MOE_BLK = 512
ROUTE_SPAN = 2048
SEG_PIECES = tuple(BF16_ROWS << b for b in range((MOE_TILE // BF16_ROWS).bit_length() - 1, -1, -1))
COMBINE_TILES = 2
SEG_LARGE = MOE_TILE // 2
FFN_CHUNKS = ((0, 512), (512, 512), (1024, 512), (1536, 512), (2048, 512), (2560, 256))


def _cparams(*sem):
    return pltpu.CompilerParams(dimension_semantics=sem, vmem_limit_bytes=VMEM_LIMIT)


def _const_spec(shape):
    nd = len(shape)
    return pl.BlockSpec(shape, lambda *_: (0,) * nd, pipeline_mode=pl.Buffered(1))


def _rms(xf, g):
    var = jnp.mean(xf * xf, axis=-1, keepdims=True)
    return xf * lax.rsqrt(var + EPS) * g


def _sigmoid(x):
    return 1.0 / (1.0 + jnp.exp(-x))


def _gelu(x):
    c = math.sqrt(2.0 / math.pi)
    return 0.5 * x * (1.0 + jnp.tanh(c * (x + 0.044715 * (x * x * x))))


def _dot(a, b):
    return jnp.dot(a, b, preferred_element_type=F32)


def _in_proj_kernel(x_ref, g_ref, w_ref, wvt_ref, q_ref, k_ref, vt_ref, xr_ref, gt_ref, us_ref, *, d_attn, d_rg):
    u = _rms(x_ref[...], g_ref[...]).astype(BF16)
    lo = 0
    for ref, width in ((q_ref, d_attn), (k_ref, d_attn), (None, d_attn),
                       (xr_ref, d_rg), (gt_ref, d_rg), (us_ref, w_ref.shape[1] - 3 * d_attn - 2 * d_rg)):
        if ref is not None:
            ref[...] = _dot(u, w_ref[:, lo:lo + width].astype(BF16)).astype(ref.dtype)
        lo += width
    vt_ref[...] = lax.dot_general(wvt_ref[...].astype(BF16), u, (((1,), (1,)), ((), ())),
                                  preferred_element_type=F32).astype(vt_ref.dtype)


def _layer_spec(stacked, layer):
    rest = stacked.shape[1:]
    return pl.BlockSpec((None,) + rest, lambda *_: (layer,) + (0,) * len(rest), pipeline_mode=pl.Buffered(1))


def _in_proj(x2, g, w_all, layer, d_attn, d_rg, d_s5):
    n, dm = x2.shape
    tm = min(PROJ_TILE, n)
    row = lambda w: pl.BlockSpec((tm, w), lambda i: (i, 0))
    wv_t = w_all[layer, :, 2 * d_attn:3 * d_attn].T
    return pl.pallas_call(
        functools.partial(_in_proj_kernel, d_attn=d_attn, d_rg=d_rg),
        grid=(n // tm,),
        in_specs=[row(dm), _const_spec((1, dm)), _layer_spec(w_all, layer), _const_spec(wv_t.shape)],
        out_specs=[row(d_attn), row(d_attn), pl.BlockSpec((d_attn, tm), lambda i: (0, i)),
                   row(d_rg), row(d_rg), row(d_s5)],
        out_shape=[jax.ShapeDtypeStruct((n, d_attn), BF16)] * 2 + [jax.ShapeDtypeStruct((d_attn, n), BF16)]
        + [jax.ShapeDtypeStruct((n, d_rg), F32)] * 2 + [jax.ShapeDtypeStruct((n, d_s5), F32)],
        compiler_params=_cparams("parallel"),
        name="in_proj",
    )(x2, g.reshape(1, dm), w_all, wv_t)


def _attn_kernel(q_ref, kp_ref, kc_ref, vp_ref, vc_ref, bias_ref, o_ref, kz, sc, *, n_pairs):
    qi = pl.program_id(1)
    tq = q_ref.shape[0]
    kz[0:tq, :] = kp_ref[...]
    kz[tq:2 * tq, :] = kc_ref[...]
    first = lax.broadcasted_iota(I32, (ATTN_SPAN, LANES), 1) < HEAD_DIM
    pairs = [slice(hp * LANES, (hp + 1) * LANES) for hp in range(n_pairs)]

    def score(j, masked):
        r0 = j * ATTN_SPAN
        for hp, cols in enumerate(pairs):
            q2 = q_ref[pl.ds(r0, ATTN_SPAN), cols] * jnp.asarray(HEAD_DIM ** -0.5, BF16)
            zero = jnp.zeros_like(q2)
            qq = jnp.concatenate([jnp.where(first, q2, zero), jnp.where(first, zero, q2)], axis=0)
            s = lax.dot_general(kz[pl.ds(r0, ATTN_WIN), cols], qq, (((1,), (1,)), ((), ())),
                                preferred_element_type=F32) + bias_ref[hp]
            if masked:
                key = lax.broadcasted_iota(I32, (ATTN_WIN, 2 * ATTN_SPAN), 0)
                s = jnp.where(key >= tq - r0, s, NEG_BIG)
            sc[j % 2, hp] = s

    def finish(j):
        r0 = j * ATTN_SPAN
        old = tq - r0
        probs = []
        for hp in range(n_pairs):
            s = sc[j % 2, hp]
            probs.append(jnp.exp((s - jnp.max(s, axis=0, keepdims=True)).astype(BF16)))
        for hp, (cols, p) in enumerate(zip(pairs, probs)):
            ones = lambda w: jnp.ones((BF16_ROWS, w), BF16)
            ov = (_dot(jnp.concatenate([vp_ref[cols, r0:tq], ones(old)], axis=0), p[0:old])
                  + _dot(jnp.concatenate([vc_ref[cols, 0:ATTN_WIN - old], ones(ATTN_WIN - old)], axis=0), p[old:]))
            o2 = (ov[0:LANES] / ov[LANES:LANES + 1]).T
            o_ref[pl.ds(r0, ATTN_SPAN), cols] = jnp.where(first, o2[0:ATTN_SPAN], o2[ATTN_SPAN:]).astype(o_ref.dtype)

    def run(masked):
        groups = tq // ATTN_SPAN
        score(0, masked)
        for j in range(groups):
            if j + 1 < groups:
                score(j + 1, masked)
            finish(j)

    pl.when(qi == 0)(functools.partial(run, True))
    pl.when(qi != 0)(functools.partial(run, False))


def _attention(q, k, vt, bias_t, layer, bsz, l):
    n, da = q.shape
    tq = ATTN_TQ
    nt = l // tq
    n_pairs = da // LANES
    cur = pl.BlockSpec((tq, da), lambda b, i: (b * nt + i, 0))
    prev = pl.BlockSpec((tq, da), lambda b, i: (b * nt + jnp.maximum(i - 1, 0), 0))
    cur_t = pl.BlockSpec((da, tq), lambda b, i: (0, b * nt + i))
    prev_t = pl.BlockSpec((da, tq), lambda b, i: (0, b * nt + jnp.maximum(i - 1, 0)))
    return pl.pallas_call(
        functools.partial(_attn_kernel, n_pairs=n_pairs),
        grid=(bsz, nt),
        in_specs=[cur, prev, cur, prev_t, cur_t, _layer_spec(bias_t, layer)],
        out_specs=cur,
        out_shape=jax.ShapeDtypeStruct((n, da), BF16),
        scratch_shapes=[pltpu.VMEM((2 * tq, da), BF16), pltpu.VMEM((2, n_pairs, ATTN_WIN, 2 * ATTN_SPAN), F32)],
        compiler_params=_cparams("parallel", "parallel"),
        name="chunk_attn",
    )(q, k, k, vt, vt, bias_t)


def _attn_bias(rel_bias):
    h = rel_bias.shape[0]
    tab = rel_bias.astype(F32)
    n_far = N_PREV * CHUNK - REL_CLIP + CHUNK
    lo = 2 * REL_CLIP - (BAND + CHUNK - 1 - n_far)
    ext = jnp.concatenate([jnp.broadcast_to(tab[:, 2 * REL_CLIP:], (h, n_far)), tab[:, lo:2 * REL_CLIP][:, ::-1]], axis=1)
    wide = BAND + CHUNK
    ring = jnp.concatenate([ext[:, CHUNK - 1:], jnp.zeros((h, 1), F32), ext[:, :CHUNK - 1]], axis=1)
    bias = jnp.broadcast_to(ring[:, None, :], (h, CHUNK, wide)).reshape(h, CHUNK * wide)
    bias = bias[:, :CHUNK * (wide - 1)].reshape(h, CHUNK, wide - 1)[..., :BAND]
    band_t = bias.transpose(0, 2, 1)
    groups = ATTN_SPAN // CHUNK
    per_chunk = [jnp.pad(band_t, ((0, 0), (c * CHUNK, (groups - 1 - c) * CHUNK), (0, 0)), constant_values=NEG_BIG)
                 for c in range(groups)]
    win = jnp.stack(per_chunk, axis=2)
    return win.reshape(h // 2, 2, ATTN_WIN, ATTN_SPAN).transpose(0, 2, 1, 3).reshape(h // 2, ATTN_WIN, 2 * ATTN_SPAN)


def _rglru_kernel(x_ref, gt_ref, cw_ref, cb_ref, w_ref, b_ref, sp_ref, o_ref, xpad, a_s, b_s, hcar):
    t = pl.program_id(1)
    tl, c = x_ref.shape
    front = SUBLANES

    @pl.when(t == 0)
    def _():
        xpad[0:front, :] = jnp.zeros((front, c), F32)
        hcar[...] = jnp.zeros_like(hcar)

    xpad[front:front + tl, :] = x_ref[...]
    xc = cb_ref[...] + sum(
        cw_ref[j:j + 1, :] * xpad[front - (RG_CONV_WIDTH - 1) + j:front - (RG_CONV_WIDTH - 1) + j + tl, :]
        for j in range(RG_CONV_WIDTH))
    xpad[0:front, :] = xpad[tl:tl + front, :]
    pre = _dot(xc.astype(BF16), w_ref[...]) + b_ref[...]
    gx = _sigmoid(pre[:, 0:c])
    ga = _sigmoid(pre[:, c:2 * c])
    log_a = -RG_C * ga * sp_ref[...]
    a = jnp.exp(log_a)
    mult = jnp.sqrt(-jnp.tanh(log_a) * (a * a + 1.0))
    b = mult * gx * xc
    nh = c // LANES
    for j in range(nh):
        a_s[j] = a[:, j * LANES:(j + 1) * LANES]
        b_s[j] = b[:, j * LANES:(j + 1) * LANES]

    step_in_vreg = lax.broadcasted_iota(I32, (SUBLANES, LANES), 0)

    def vreg_scan(r, h_in):
        rows = pl.ds(pl.multiple_of(r * SUBLANES, SUBLANES), SUBLANES)
        out = []
        for j in range(nh):
            av, bv = a_s[j, rows, :], b_s[j, rows, :]
            for d in (1, 2, 4):
                seen = step_in_vreg >= d
                a_prev = jnp.where(seen, pltpu.roll(av, d, 0), 1.0)
                b_prev = jnp.where(seen, pltpu.roll(bv, d, 0), 0.0)
                bv = bv + av * b_prev
                av = av * a_prev
            h = bv + av * h_in[j]
            b_s[j, rows, :] = h
            out.append(h[SUBLANES - 1:SUBLANES, :])
        return tuple(out)

    h_last = lax.fori_loop(0, tl // SUBLANES, vreg_scan,
                           tuple(hcar[:, j * LANES:(j + 1) * LANES] for j in range(nh)), unroll=4)
    for j in range(nh):
        cols = slice(j * LANES, (j + 1) * LANES)
        hcar[:, cols] = h_last[j]
        o_ref[:, cols] = (b_s[j] * _gelu(gt_ref[:, cols])).astype(o_ref.dtype)


def _rglru(xr, gate, conv_w, conv_b, w_bd, b_cat, sp, layer, bsz, l):
    n, c = xr.shape
    tl = min(RG_TL, l)
    nt = l // tl
    row = pl.BlockSpec((tl, c), lambda b, t: (b * nt + t, 0))
    return pl.pallas_call(
        _rglru_kernel,
        grid=(bsz, nt),
        in_specs=[row, row] + [_layer_spec(a, layer) for a in (conv_w, conv_b, w_bd, b_cat, sp)],
        out_specs=row,
        out_shape=jax.ShapeDtypeStruct((n, c), BF16),
        scratch_shapes=[pltpu.VMEM((tl + SUBLANES, c), F32), pltpu.VMEM((c // LANES, tl, LANES), F32),
                        pltpu.VMEM((c // LANES, tl, LANES), F32), pltpu.VMEM((1, c), F32)],
        compiler_params=_cparams("parallel", "arbitrary"),
        name="rglru",
    )(xr, gate, conv_w, conv_b, w_bd, b_cat, sp)


def _block_diag(w):
    nb, d, _ = w.shape
    eye = jnp.eye(nb, dtype=w.dtype)
    return (eye[:, None, :, None] * w[:, :, None, :]).reshape(nb * d, nb * d)


def _s5_kernel(u_ref, m_ref, wab_ref, wo_ref, c1_ref, c2a_ref, c2b_ref, d_ref, y_ref, ps, ug, xa_s, xb_s):
    t, gd = S5_T, S5_GROUP_DIM
    ng, nk, _ = ug.shape
    nh = ps.shape[0]
    per_half = LANES // gd
    rows8 = SUBLANES

    for h in range(nh):
        ps[h] = u_ref[:, h * LANES:(h + 1) * LANES]

    lane_block = lax.broadcasted_iota(I32, (rows8, LANES), 1) // gd

    def block_transpose(vs):
        d = per_half // 2
        while d:
            keep = (lane_block & d) == 0
            out = list(vs)
            for i in range(per_half):
                if i & d == 0:
                    out[i] = jnp.where(keep, vs[i], pltpu.roll(vs[i + d], d * gd, 1))
                    out[i + d] = jnp.where(keep, pltpu.roll(vs[i], LANES - d * gd, 1), vs[i + d])
            vs, d = out, d // 2
        return vs

    def to_groups(r, carry):
        base = pl.multiple_of(r * rows8 * t, rows8 * t)
        rows = pl.ds(pl.multiple_of(r * rows8, rows8), rows8)
        for h in range(nh):
            for j in range(t // per_half):
                steps = [ps[h, pl.ds(base + j * per_half + i, rows8, stride=t), :] for i in range(per_half)]
                for k, v in enumerate(block_transpose(steps)):
                    ug[h * per_half + k, rows, j * LANES:(j + 1) * LANES] = v
        return carry

    lax.fori_loop(0, nk // rows8, to_groups, 0, unroll=2)

    def project(g, carry):
        u = ug[g]
        ub = u.astype(BF16)
        rows = pl.ds(pl.multiple_of(g * nk, nk), nk)
        xab = _dot(ub, wab_ref[g])
        xa_s[rows, :] = xab[:, 0:LANES]
        xb_s[rows, :] = xab[:, LANES:2 * LANES]
        ug[g] = _dot(ub, m_ref[g]) + d_ref[g] * u
        return carry

    lax.fori_loop(0, ng, project, 0, unroll=8)

    c1, c2a, c2b = c1_ref[...], c2a_ref[...], c2b_ref[...]

    def step(k, carry):
        xa, xb = carry
        rows = pl.ds(k, ng, stride=nk)
        ia = xa_s[rows, :]
        ib = xb_s[rows, :]
        xa_s[rows, :] = xa
        return c1 * xa + c2a * xb + ia, c1 * xb + c2b * xa + ib

    z = jnp.zeros((ng, xa_s.shape[1]), F32)
    lax.fori_loop(0, nk, step, (z, z), unroll=8)

    def respond(g, carry):
        rows = pl.ds(pl.multiple_of(g * nk, nk), nk)
        ug[g] = _gelu(ug[g] + _dot(xa_s[rows, :].astype(BF16), wo_ref[g]))
        return carry

    lax.fori_loop(0, ng, respond, 0, unroll=8)

    def from_groups(r, carry):
        base = pl.multiple_of(r * rows8 * t, rows8 * t)
        rows = pl.ds(pl.multiple_of(r * rows8, rows8), rows8)
        for h in range(nh):
            for j in range(t // per_half):
                groups = [ug[h * per_half + k, rows, j * LANES:(j + 1) * LANES] for k in range(per_half)]
                for i, v in enumerate(block_transpose(groups)):
                    ps[h, pl.ds(base + j * per_half + i, rows8, stride=t), :] = v
        return carry

    lax.fori_loop(0, nk // rows8, from_groups, 0, unroll=2)
    for h in range(nh):
        y_ref[:, h * LANES:(h + 1) * LANES] = ps[h].astype(y_ref.dtype)


def _s5(us, prm, layer, bsz, l):
    n, w = us.shape
    ng = w // S5_GROUP_DIM
    nk = l // S5_T
    p2 = prm[2].shape[-2]
    row = pl.BlockSpec((l, w), lambda b: (b, 0))
    return pl.pallas_call(
        _s5_kernel,
        grid=(bsz,),
        in_specs=[row] + [_layer_spec(a, layer) for a in prm],
        out_specs=row,
        out_shape=jax.ShapeDtypeStruct((n, w), BF16),
        scratch_shapes=[pltpu.VMEM((w // LANES, l, LANES), F32), pltpu.VMEM((ng, nk, S5_T * S5_GROUP_DIM), F32),
                        pltpu.VMEM((ng * nk, p2), F32), pltpu.VMEM((ng * nk, p2), F32)],
        compiler_params=_cparams("parallel"),
        name="s5",
    )(us, *prm)


def _s5_params(a_re, a_im, log_dt, b_re, b_im, c_re, c_im, d):
    t = S5_T
    g, p = a_re.shape
    gd = S5_GROUP_DIM
    hi = lax.Precision.HIGHEST
    a = lax.complex(a_re.astype(F32), a_im.astype(F32))
    dt = jnp.exp(log_dt.astype(F32))[:, None]
    a_bar = jnp.exp(a * dt)
    bm = lax.complex(b_re.astype(F32), b_im.astype(F32))
    cm = lax.complex(c_re.astype(F32), c_im.astype(F32))
    b_bar = ((a_bar - 1.0) / a)[..., None] * bm
    steps = jnp.arange(t + 1, dtype=F32)
    pw = jnp.exp((a * dt)[:, None, :] * steps[None, :, None])

    cp = (cm[:, None, :, :] * pw[:, :, None, :]).transpose(0, 3, 1, 2)
    w = t * gd
    lag = cp[:, :, :t].reshape(g, p, w)
    k2 = jnp.einsum('gpc,gpx->gcx', jnp.concatenate([b_bar.real, -b_bar.imag], axis=1),
                    jnp.concatenate([lag.real, lag.imag], axis=1), precision=hi)
    shift = np.zeros((t, w, w), np.float32)
    for s in range(t):
        shift[s, np.arange(w - s * gd), np.arange(w - s * gd) + s * gd] = 1.0
    m = jnp.einsum('gcx,sxl->gscl', k2, shift, precision=hi).reshape(g, w, w)
    win = pw[:, :t][:, ::-1][:, :, :, None] * b_bar[:, None]
    win = win.transpose(0, 1, 3, 2).reshape(g, w, p)
    wab = jnp.concatenate([win.real, win.imag, win.imag, win.real], axis=-1)
    co = cp[:, :, 1:].reshape(g, p, w)
    wo = jnp.concatenate([co.real, -co.imag], axis=1)
    lt = pw[:, t]
    c1 = jnp.concatenate([lt.real, lt.real], axis=-1)
    c2a = jnp.concatenate([-lt.imag, lt.imag], axis=-1)
    c2b = jnp.concatenate([lt.imag, -lt.imag], axis=-1)
    dtile = jnp.tile(d.astype(F32).reshape(g, 1, gd), (1, t, 1)).reshape(g, 1, t * gd)
    return (m.astype(BF16), wab.astype(BF16), wo.astype(BF16), c1, c2a, c2b, dtile)


def _out_proj_kernel(ya_ref, yr_ref, ys_ref, x_ref, gg_ref, wglu_ref, wo_ref, gf_ref, *rest, dense, chunks):
    ys = ys_ref[...]
    ys = ys.astype(F32) * _sigmoid(_dot(ys.astype(BF16), wglu_ref[...].astype(BF16)))
    acc = x_ref[...]
    lo = 0
    for y in (ya_ref[...].astype(F32), yr_ref[...].astype(F32), ys):
        w = y.shape[1]
        acc = acc + _dot(_rms(y, gg_ref[:, lo:lo + w]).astype(BF16), wo_ref[lo:lo + w, :].astype(BF16))
        lo += w
    h = _rms(acc, gf_ref[...])
    hb = h.astype(BF16)
    if dense:
        w1_ref, w3_ref, w2_ref, xo_ref = rest
        xo_ref[...] = acc + _swiglu_chunks(hb, w1_ref, w3_ref, w2_ref, chunks)
        return
    rt_ref, xo_ref, h_ref, lg_ref = rest
    xo_ref[...] = acc
    h_ref[...] = hb
    both = _dot(hb, rt_ref[...])
    h_lo = (h - hb.astype(F32)).astype(BF16)
    lg_ref[...] = both[:, 0:LANES] + both[:, LANES:2 * LANES] + _dot(h_lo, rt_ref[:, 0:LANES])


def _out_proj(ya, yr, ys, x2, gg, wglu_all, wo_all, layer, gf, *, ffn=None, router=None):
    n, dm = x2.shape
    tm = min(ROW_TILE, n)
    row = lambda w: pl.BlockSpec((tm, w), lambda i: (i, 0))
    ins = [ya, yr, ys, x2, gg.reshape(1, -1), wglu_all, wo_all, gf.reshape(1, dm)]
    in_specs = [row(ya.shape[1]), row(yr.shape[1]), row(ys.shape[1]), row(dm), _const_spec((1, gg.shape[0])),
                _layer_spec(wglu_all, layer), _layer_spec(wo_all, layer), _const_spec((1, dm))]
    if ffn is not None:
        ins += list(ffn)
        in_specs += [_const_spec(w.shape) for w in ffn]
        out_specs = row(dm)
        out_shape = jax.ShapeDtypeStruct((n, dm), F32)
        chunks = _ffn_chunks(ffn[0].shape[1])
    else:
        ne = router.shape[1]
        r_hi = router.astype(BF16)
        r_lo = (router.astype(F32) - r_hi.astype(F32)).astype(BF16)
        pad = lambda a: jnp.pad(a, ((0, 0), (0, LANES - ne)))
        ins.append(jnp.concatenate([pad(r_hi), pad(r_lo)], axis=1))
        in_specs.append(_const_spec((dm, 2 * LANES)))
        out_specs = [row(dm), row(dm), row(LANES)]
        out_shape = [jax.ShapeDtypeStruct((n, dm), F32), jax.ShapeDtypeStruct((n, dm), BF16),
                     jax.ShapeDtypeStruct((n, LANES), F32)]
        chunks = None
    return pl.pallas_call(
        functools.partial(_out_proj_kernel, dense=ffn is not None, chunks=chunks),
        grid=(n // tm,),
        in_specs=in_specs, out_specs=out_specs, out_shape=out_shape,
        compiler_params=pltpu.CompilerParams(dimension_semantics=("parallel",), vmem_limit_bytes=MIX_VMEM_LIMIT),
        name="out_proj",
    )(*ins)


def _swiglu_chunks(h, w1, w3, w2, chunks, before_chunk=None, after_chunk=None):
    acc = None
    for j, (lo, width) in enumerate(chunks):
        if before_chunk is not None:
            before_chunk(j)
        a = _dot(h, w1[:, lo:lo + width].astype(BF16))
        b = _dot(h, w3[:, lo:lo + width].astype(BF16))
        t = (a * _sigmoid(a) * b).astype(BF16)
        y = _dot(t, w2[lo:lo + width, :].astype(BF16))
        acc = y if acc is None else acc + y
        if after_chunk is not None:
            after_chunk(j)
    return acc


def _ffn_chunks(f):
    if f == sum(w for _, w in FFN_CHUNKS):
        return FFN_CHUNKS
    return ((0, f),)


def _expert_weight_copies(hbm, vmem, sems, chunks, expert, j):
    h1, h3, h2 = hbm
    v1, v3, v2 = vmem
    lo, w = chunks[j]
    return [pltpu.make_async_copy(h1.at[expert, :, lo:lo + w], v1.at[:, lo:lo + w], sems.at[j, 0]),
            pltpu.make_async_copy(h3.at[expert, :, lo:lo + w], v3.at[:, lo:lo + w], sems.at[j, 1]),
            pltpu.make_async_copy(h2.at[expert, lo:lo + w, :], v2.at[lo:lo + w, :], sems.at[j, 2])]


def _expert_ffn_kernel(brow_ref, bexp_ref, role_ref, x_ref, w1_hbm, w3_hbm, w2_hbm, o_ref, w1_v, w3_v, w2_v, sems,
                       *, chunks):
    i = pl.program_id(0)
    copies = functools.partial(_expert_weight_copies, (w1_hbm, w3_hbm, w2_hbm), (w1_v, w3_v, w2_v), sems, chunks)

    @pl.when(i == 0)
    def _():
        for j in range(len(chunks)):
            for cp in copies(bexp_ref[0], j):
                cp.start()

    def block(first, last):
        def before(j):
            for cp in copies(bexp_ref[i], j):
                cp.wait()

        def after(j):
            for cp in copies(bexp_ref[i + 1], j):
                cp.start()

        o_ref[...] = _swiglu_chunks(x_ref[...], w1_v, w3_v, w2_v, chunks, before if first else None,
                                    after if last else None).astype(o_ref.dtype)

    for role in range(4):
        pl.when(role_ref[i] == role)(functools.partial(block, bool(role & 1), bool(role & 2)))


def _expert_ffn(xb, w1, w3, w2, blk_row, blk_exp, role, max_blocks):
    rows, dm = xb.shape
    f = w1.shape[2]
    chunks = _ffn_chunks(f)
    row = pl.BlockSpec((MOE_BLK, dm), lambda i, br, be, ro: (br[i], 0))
    hbm = pl.BlockSpec(memory_space=pl.ANY)
    return pl.pallas_call(
        functools.partial(_expert_ffn_kernel, chunks=chunks),
        grid_spec=pltpu.PrefetchScalarGridSpec(
            num_scalar_prefetch=3, grid=(max_blocks,),
            in_specs=[row, hbm, hbm, hbm], out_specs=row,
            scratch_shapes=[pltpu.VMEM(w1.shape[1:], w1.dtype), pltpu.VMEM(w3.shape[1:], w3.dtype),
                            pltpu.VMEM(w2.shape[1:], w2.dtype), pltpu.SemaphoreType.DMA((len(chunks), 3))]),
        out_shape=jax.ShapeDtypeStruct((rows, dm), BF16),
        compiler_params=_cparams("arbitrary"),
        name="expert_ffn",
    )(blk_row, blk_exp, role, xb, w1, w3, w2)


def _route_tile(logits_tok, filled, ne):
    tm = logits_tok.shape[0]
    logits = logits_tok.T[0:ne, :]
    eidx = lax.broadcasted_iota(I32, (ne, tm), 0)
    m1 = jnp.max(logits, axis=0, keepdims=True)
    i1 = jnp.min(jnp.where(logits == m1, eidx, ne), axis=0, keepdims=True)
    sel1 = eidx == i1
    rest = jnp.where(sel1, -jnp.inf, logits)
    m2 = jnp.max(rest, axis=0, keepdims=True)
    i2 = jnp.min(jnp.where(rest == m2, eidx, ne), axis=0, keepdims=True)
    sel2 = eidx == i2
    e2 = jnp.exp(m2 - m1)
    den = 1.0 + e2
    rf = jnp.where(sel1 | sel2, 1.0, 0.0)
    cnt = jnp.sum(rf, axis=1, keepdims=True)
    before = (lax.broadcasted_iota(I32, (tm, tm), 0) < lax.broadcasted_iota(I32, (tm, tm), 1))
    rank = _dot(rf.astype(BF16), jnp.where(before, 1.0, 0.0).astype(BF16))
    whole = lambda a: jnp.floor(a * (1.0 / BF16_ROWS)) * BF16_ROWS
    phase = filled - whole(filled)
    span = jnp.where(cnt > 0, whole(phase + cnt + (BF16_ROWS - 1)), 0.0)
    ecol = lax.broadcasted_iota(I32, (ne, 1), 0)
    off = jnp.zeros((ne, 1), F32)
    for j in range(ne - 1):
        off = off + jnp.where(ecol > j, span[j:j + 1, :], 0.0)
    place = off + phase + rank
    pos1 = jnp.sum(jnp.where(sel1, place, 0.0), axis=0, keepdims=True)
    pos2 = jnp.sum(jnp.where(sel2, place, 0.0), axis=0, keepdims=True)
    record = [pos1, pos2, 1.0 / den, e2 / den]
    lane_major = jnp.concatenate(record + [jnp.zeros((SUBLANES - len(record), tm), F32)], axis=0)
    tok_major = jnp.concatenate([lane_major, jnp.zeros((LANES - SUBLANES, tm), F32)], axis=0).T
    return lane_major, tok_major, cnt


def _route_kernel(lg_ref, lane_ref, tok_ref, cnt_ref, filled, *, ne):
    tm = MOE_TILE

    @pl.when(pl.program_id(0) == 0)
    def _():
        filled[...] = jnp.zeros_like(filled)

    for t in range(lg_ref.shape[0] // tm):
        lane_major, tok_major, cnt = _route_tile(lg_ref[t * tm:(t + 1) * tm, :], filled[...], ne)
        lane_ref[:, t * tm:(t + 1) * tm] = lane_major
        tok_ref[t * tm:(t + 1) * tm, :] = tok_major
        cnt_ref[t] = jnp.broadcast_to(cnt, (ne, LANES)).astype(I32)
        filled[...] = filled[...] + cnt


def _route(logits, ne):
    n = logits.shape[0]
    tm = MOE_TILE
    span = min(ROUTE_SPAN, n)
    return pl.pallas_call(
        functools.partial(_route_kernel, ne=ne),
        grid=(n // span,),
        in_specs=[pl.BlockSpec((span, LANES), lambda i: (i, 0))],
        out_specs=[pl.BlockSpec((SUBLANES, span), lambda i: (0, i)), pl.BlockSpec((span, LANES), lambda i: (i, 0)),
                   pl.BlockSpec((span // tm, ne, LANES), lambda i: (i, 0, 0))],
        out_shape=[jax.ShapeDtypeStruct((SUBLANES, n), F32), jax.ShapeDtypeStruct((n, LANES), F32),
                   jax.ShapeDtypeStruct((n // tm, ne, LANES), I32)],
        scratch_shapes=[pltpu.VMEM((ne, 1), F32)],
        compiler_params=_cparams("arbitrary"),
        name="moe_route",
    )(logits)


def _segment_copies(seg_ref, grouped_ref, base_ref, tile_buf, off_ref, sems, tile, slot, ne, *, to_grouped, wait):
    large = [p for p in SEG_PIECES if p >= SEG_LARGE]
    small = [p for p in SEG_PIECES if p < SEG_LARGE]
    for e in range(ne):
        seg = seg_ref[tile * ne + e]

        def pieces(sizes, done, seg=seg, e=e):
            far = base_ref[tile * ne + e] + done
            near = off_ref[tile * ne + e] + done
            for piece in sizes:
                g = grouped_ref.at[pl.ds(pl.multiple_of(far, BF16_ROWS), piece)]
                t = tile_buf.at[slot, pl.ds(pl.multiple_of(near, BF16_ROWS), piece)]
                cp = pltpu.make_async_copy(t, g, sems.at[slot, e]) if to_grouped else \
                    pltpu.make_async_copy(g, t, sems.at[slot, e])
                has = (seg & piece) != 0

                @pl.when(has)
                def _():
                    cp.wait() if wait else cp.start()

                step = jnp.where(has, piece, 0)
                far = far + step
                near = near + step

        pl.when(seg >= SEG_LARGE)(functools.partial(pieces, large, 0))
        pieces(small, seg & -SEG_LARGE)


def _tile_rows(ne):
    return TOP_K * MOE_TILE + ne * 2 * BF16_ROWS


def _tail_copies(end_ref, zeros, xb_ref, sems, ne):
    return [pltpu.make_async_copy(zeros, xb_ref.at[pl.ds(pl.multiple_of(end_ref[e], BF16_ROWS), MOE_BLK)], sems.at[e])
            for e in range(ne)]


def _dispatch_kernel(seg_ref, base_ref, off_ref, phase_ref, end_ref, rt_ref, h_ref, xb_ref, res, zeros, partial,
                     sems, tail_sems):
    i = pl.program_id(0)
    nt = pl.num_programs(0)
    ne = tail_sems.shape[0]
    tm = h_ref.shape[0]
    mrows = res.shape[1]
    slot = i % 2
    copies = functools.partial(_segment_copies, seg_ref, xb_ref, base_ref, res, off_ref, sems, ne=ne, to_grouped=True)

    @pl.when(i == 0)
    def _():
        zeros[...] = jnp.zeros_like(zeros)
        partial[...] = jnp.zeros_like(partial)
        for cp in _tail_copies(end_ref, zeros, xb_ref, tail_sems, ne):
            cp.start()

    row = lax.broadcasted_iota(I32, (mrows, tm), 0)
    pos = rt_ref[0:2, :].astype(I32)
    perm = jnp.where((row == pos[0:1, :]) | (row == pos[1:2, :]), 1.0, 0.0).astype(BF16)
    res[slot] = _dot(perm, h_ref[...]).astype(BF16)

    tile_row = lax.broadcasted_iota(I32, (BF16_ROWS, res.shape[2]), 0)
    for e in range(ne):
        @pl.when(seg_ref[i * ne + e] > 0)
        def _():
            first = pl.ds(pl.multiple_of(off_ref[i * ne + e], BF16_ROWS), BF16_ROWS)
            merged = jnp.where(tile_row < phase_ref[i * ne + e], partial[e].astype(F32), res[slot, first, :].astype(F32))
            res[slot, first, :] = merged.astype(BF16)
            last = pl.ds(pl.multiple_of(off_ref[i * ne + e] + seg_ref[i * ne + e] - BF16_ROWS, BF16_ROWS), BF16_ROWS)
            partial[e] = res[slot, last, :]

    @pl.when(i > 0)
    def _():
        copies(tile=i - 1, slot=1 - slot, wait=True)

    copies(tile=i, slot=slot, wait=False)

    @pl.when(i == nt - 1)
    def _():
        copies(tile=i, slot=slot, wait=True)
        for cp in _tail_copies(end_ref, zeros, xb_ref, tail_sems, ne):
            cp.wait()


def _dispatch(h, rt_lane, lay, ne):
    n, dm = h.shape
    tm = MOE_TILE
    return pl.pallas_call(
        _dispatch_kernel,
        grid_spec=pltpu.PrefetchScalarGridSpec(
            num_scalar_prefetch=5, grid=(n // tm,),
            in_specs=[pl.BlockSpec((SUBLANES, tm), lambda i, *_: (0, i)), pl.BlockSpec((tm, dm), lambda i, *_: (i, 0))],
            out_specs=pl.BlockSpec(memory_space=pl.ANY),
            scratch_shapes=[pltpu.VMEM((2, _tile_rows(ne), dm), BF16), pltpu.VMEM((MOE_BLK, dm), BF16),
                            pltpu.VMEM((ne, BF16_ROWS, dm), BF16),
                            pltpu.SemaphoreType.DMA((2, ne)), pltpu.SemaphoreType.DMA((ne,))]),
        out_shape=jax.ShapeDtypeStruct((lay['rows_total'], dm), BF16),
        compiler_params=_cparams("arbitrary"),
        name="moe_dispatch",
    )(lay['seg'], lay['base'], lay['off'], lay['phase'], lay['end'], rt_lane, h)


def _combine_kernel(seg_ref, base_ref, off_ref, rt_ref, x_ref, gn_ref, yb_ref, o_ref, got, sems, *, ne):
    i = pl.program_id(0)
    steps = pl.num_programs(0)
    tm = MOE_TILE
    mrows = got.shape[1]
    copies = functools.partial(_segment_copies, seg_ref, yb_ref, base_ref, got, off_ref, sems, ne=ne, to_grouped=False)
    slot = lambda step, t: (step % 2) * COMBINE_TILES + t

    def fetch(step, wait):
        for t in range(COMBINE_TILES):
            copies(tile=step * COMBINE_TILES + t, slot=slot(step, t), wait=wait)

    @pl.when(i == 0)
    def _():
        got[...] = jnp.zeros_like(got)
        fetch(i, False)

    pl.when(i + 1 < steps)(lambda: fetch(i + 1, False))
    fetch(i, True)

    col = lax.broadcasted_iota(I32, (tm, mrows), 1)
    for t in range(COMBINE_TILES):
        tok = slice(t * tm, (t + 1) * tm)
        rt = rt_ref[tok, :]
        sel = jnp.concatenate(
            [jnp.where(col == rt[:, k:k + 1].astype(I32), 1.0, 0.0).astype(BF16) for k in range(TOP_K)], axis=0)
        picked = _dot(sel, got[slot(i, t)])
        y = sum(rt[:, TOP_K + k:TOP_K + k + 1] * picked[k * tm:(k + 1) * tm] for k in range(TOP_K))
        o_ref[tok, :] = _rms(x_ref[tok, :] + y, gn_ref[...])


def _combine(yb, rt_tok, x2, gn, lay, ne):
    n, dm = x2.shape
    tm = COMBINE_TILES * MOE_TILE
    mrows = _tile_rows(ne)
    tok = lambda w: pl.BlockSpec((tm, w), lambda i, *_: (i, 0))
    return pl.pallas_call(
        functools.partial(_combine_kernel, ne=ne),
        grid_spec=pltpu.PrefetchScalarGridSpec(
            num_scalar_prefetch=3, grid=(n // tm,),
            in_specs=[tok(LANES), tok(dm), pl.BlockSpec((1, dm), lambda i, *_: (0, 0)),
                      pl.BlockSpec(memory_space=pl.ANY)],
            out_specs=tok(dm),
            scratch_shapes=[pltpu.VMEM((2 * COMBINE_TILES, mrows, dm), BF16),
                            pltpu.SemaphoreType.DMA((2 * COMBINE_TILES, ne))]),
        out_shape=jax.ShapeDtypeStruct((n, dm), F32),
        compiler_params=_cparams("arbitrary"),
        name="moe_combine",
    )(lay['seg'], lay['base'], lay['off'], rt_tok, x2, gn.reshape(1, dm), yb)


def _moe_layout(cnt, n):
    nt, ne = cnt.shape
    blk = MOE_BLK
    whole = lambda a: a // BF16_ROWS * BF16_ROWS
    cap = -(-(n + BF16_ROWS + blk) // blk) * blk
    region = jnp.arange(ne, dtype=I32) * cap
    filled = jnp.cumsum(cnt, axis=0) - cnt
    phase = filled - whole(filled)
    span = jnp.where(cnt > 0, whole(phase + cnt + BF16_ROWS - 1), 0)
    base = (region[None, :] + filled - phase).astype(I32)
    off = (jnp.cumsum(span, axis=1) - span).astype(I32)
    rows_e = jnp.sum(cnt, axis=0)
    nblk_e = (rows_e + blk - 1) // blk
    end = (region + whole(rows_e + BF16_ROWS - 1)).astype(I32)
    ends = jnp.cumsum(nblk_e)
    max_blocks = TOP_K * n // blk + ne
    bid = jnp.arange(max_blocks + 1, dtype=I32)
    bexp = jnp.minimum(jnp.sum((bid[:, None] >= ends[None, :]).astype(I32), axis=1), ne - 1)
    brow = bexp * (cap // blk) + bid - (ends - nblk_e)[bexp]
    nblk = ends[-1]
    last = jnp.maximum(nblk - 1, 0)
    used = bid < nblk
    first = bid == (ends - nblk_e)[bexp]
    hand_over = (bid == ends[bexp] - 1) & (bid < last)
    role = jnp.where(used, first.astype(I32) + 2 * hand_over.astype(I32), 4).astype(I32)
    bexp = jnp.where(used, bexp, bexp[last]).astype(I32)
    brow = jnp.where(used, brow, brow[last]).astype(I32)
    flat = lambda a: a.reshape(-1)
    return dict(seg=flat(span.astype(I32)), base=flat(base), off=flat(off), phase=flat(phase.astype(I32)), end=end,
                brow=brow, bexp=bexp, role=role, rows_total=ne * cap, max_blocks=max_blocks)


def _moe(x2, h, logits, gn, w1, w3, w2):
    n, dm = x2.shape
    ne = w1.shape[0]
    rt_lane, rt_tok, cnt = _route(logits, ne)
    lay = _moe_layout(cnt[:, :, 0], n)
    xb = _dispatch(h, rt_lane, lay, ne)
    yb = _expert_ffn(xb, w1, w3, w2, lay['brow'], lay['bexp'], lay['role'], lay['max_blocks'])
    return _combine(yb, rt_tok, x2, gn, lay, ne)


def kernel(x, norm_mix_g, w_in, attn_rel_bias, rg_conv_w, rg_conv_b, rg_wx, rg_bx, rg_wa, rg_ba, rg_lambda, s5_a_re, s5_a_im, s5_log_dt, s5_b_re, s5_b_im, s5_c_re, s5_c_im, s5_d, s5_w_glu, g_group, w_out, norm_ffn_g, ffn_w1, ffn_w3, ffn_w2, moe_router, moe_w1, moe_w3, moe_w2, final_norm_g):
    bsz, l, dm = x.shape
    depth = w_in.shape[0]
    assert depth == 2 and l % ATTN_TQ == 0 and ATTN_TQ == N_PREV * CHUNK, "dense layer, then the MoE layer"
    d_rg = rg_conv_w.shape[2]
    d_s5 = s5_w_glu.shape[1]
    d_attn = (w_in.shape[2] - 2 * d_rg - d_s5) // 3
    x2 = x.reshape(bsz * l, dm)
    attn_bias = jax.vmap(_attn_bias)(attn_rel_bias)
    rg_gates_w = jnp.concatenate([jax.vmap(_block_diag)(rg_wx), jax.vmap(_block_diag)(rg_wa)], axis=2).astype(BF16)
    rg_gates_b = jnp.concatenate([rg_bx, rg_ba], axis=1).astype(F32)[:, None, :]
    rg_decay = jax.nn.softplus(-rg_lambda.astype(F32))[:, None, :]
    s5_prm = jax.vmap(_s5_params)(s5_a_re, s5_a_im, s5_log_dt, s5_b_re, s5_b_im, s5_c_re, s5_c_im, s5_d)
    for layer in range(depth):
        q, k, vt, xr, gate, us = _in_proj(x2, norm_mix_g[layer], w_in, layer, d_attn, d_rg, d_s5)
        y_attn = _attention(q, k, vt, attn_bias, layer, bsz, l)
        y_rg = _rglru(xr, gate, rg_conv_w.astype(F32), rg_conv_b.astype(F32)[:, None, :], rg_gates_w, rg_gates_b,
                      rg_decay, layer, bsz, l)
        y_s5 = _s5(us, s5_prm, layer, bsz, l)
        mixed = (y_attn, y_rg, y_s5, x2, g_group[layer].astype(F32), s5_w_glu, w_out, layer,
                 norm_ffn_g[layer].astype(F32))
        if layer == 0:
            x2 = _out_proj(*mixed, ffn=(ffn_w1[0], ffn_w3[0], ffn_w2[0]))
        else:
            x2, h, logits = _out_proj(*mixed, router=moe_router[0])
            x2 = _moe(x2, h, logits, final_norm_g.astype(F32), moe_w1[0], moe_w3[0], moe_w2[0])
    return x2.reshape(bsz, l, dm)
```

```python
import functools
import math

import jax
import jax.numpy as jnp
import numpy as np
from jax import lax
from jax.experimental import pallas as pl
from jax.experimental.pallas import tpu as pltpu

F32 = jnp.float32
BF16 = jnp.bfloat16
I32 = jnp.int32

EPS = 1e-6
CHUNK = 64
N_PREV = 8
BAND = (N_PREV + 1) * CHUNK
REL_CLIP = 128
HEAD_DIM = 64
RG_C = 8.0
RG_CONV_WIDTH = 4
S5_GROUP_DIM = 16
S5_T = 16
TOP_K = 2
NEG_BIG = -1e30

LANES = 128
SUBLANES = 8
BF16_ROWS = 16
VMEM_LIMIT = 52 * 1024 * 1024
MIX_VMEM_LIMIT = 58 * 1024 * 1024

ROW_TILE = 512
PROJ_TILE = 1024
ATTN_TQ = 512
ATTN_SPAN = 2 * CHUNK
ATTN_WIN = N_PREV * CHUNK + ATTN_SPAN
RG_TL = 1024
MOE_TILE = 256
MOE_BLK = 512
ROUTE_SPAN = 2048
SEG_PIECES = tuple(BF16_ROWS << b for b in range((MOE_TILE // BF16_ROWS).bit_length() - 1, -1, -1))
COMBINE_TILES = 2
SEG_LARGE = MOE_TILE // 2
DENSE_CHUNK = 512
EXPERT_CHUNK = 1024


def _cparams(*sem):
    return pltpu.CompilerParams(dimension_semantics=sem, vmem_limit_bytes=VMEM_LIMIT)


def _const_spec(shape):
    nd = len(shape)
    return pl.BlockSpec(shape, lambda *_: (0,) * nd, pipeline_mode=pl.Buffered(1))


def _rms(xf, g):
    var = jnp.mean(xf * xf, axis=-1, keepdims=True)
    return xf * lax.rsqrt(var + EPS) * g


def _sigmoid(x):
    return 1.0 / (1.0 + jnp.exp(-x))


def _gelu(x):
    c = math.sqrt(2.0 / math.pi)
    return 0.5 * x * (1.0 + jnp.tanh(c * (x + 0.044715 * (x * x * x))))


def _dot(a, b):
    return jnp.dot(a, b, preferred_element_type=F32)


def _in_proj_kernel(x_ref, g_ref, w_ref, wvt_ref, q_ref, k_ref, vt_ref, xr_ref, gt_ref, us_ref, *, d_attn, d_rg):
    u = _rms(x_ref[...], g_ref[...]).astype(BF16)
    lo = 0
    for ref, width in ((q_ref, d_attn), (k_ref, d_attn), (None, d_attn),
                       (xr_ref, d_rg), (gt_ref, d_rg), (us_ref, w_ref.shape[1] - 3 * d_attn - 2 * d_rg)):
        if ref is not None:
            ref[...] = _dot(u, w_ref[:, lo:lo + width].astype(BF16)).astype(ref.dtype)
        lo += width
    vt_ref[...] = lax.dot_general(wvt_ref[...].astype(BF16), u, (((1,), (1,)), ((), ())),
                                  preferred_element_type=F32).astype(vt_ref.dtype)


def _layer_spec(stacked, layer):
    rest = stacked.shape[1:]
    return pl.BlockSpec((None,) + rest, lambda *_: (layer,) + (0,) * len(rest), pipeline_mode=pl.Buffered(1))


def _in_proj(x2, g, w_all, layer, d_attn, d_rg, d_s5):
    n, dm = x2.shape
    tm = min(PROJ_TILE, n)
    row = lambda w: pl.BlockSpec((tm, w), lambda i: (i, 0))
    wv_t = w_all[layer, :, 2 * d_attn:3 * d_attn].T
    return pl.pallas_call(
        functools.partial(_in_proj_kernel, d_attn=d_attn, d_rg=d_rg),
        grid=(n // tm,),
        in_specs=[row(dm), _const_spec((1, dm)), _layer_spec(w_all, layer), _const_spec(wv_t.shape)],
        out_specs=[row(d_attn), row(d_attn), pl.BlockSpec((d_attn, tm), lambda i: (0, i)),
                   row(d_rg), row(d_rg), row(d_s5)],
        out_shape=[jax.ShapeDtypeStruct((n, d_attn), BF16)] * 2 + [jax.ShapeDtypeStruct((d_attn, n), BF16)]
        + [jax.ShapeDtypeStruct((n, d_rg), F32)] * 2 + [jax.ShapeDtypeStruct((n, d_s5), F32)],
        compiler_params=_cparams("parallel"),
        name="in_proj",
    )(x2, g.reshape(1, dm), w_all, wv_t)


def _attn_kernel(q_ref, kp_ref, kc_ref, vp_ref, vc_ref, bias_ref, o_ref, kz, sc, *, n_pairs):
    qi = pl.program_id(1)
    tq = q_ref.shape[0]
    kz[0:tq, :] = kp_ref[...]
    kz[tq:2 * tq, :] = kc_ref[...]
    first = lax.broadcasted_iota(I32, (ATTN_SPAN, LANES), 1) < HEAD_DIM
    pairs = [slice(hp * LANES, (hp + 1) * LANES) for hp in range(n_pairs)]

    def score(j, masked):
        r0 = j * ATTN_SPAN
        for hp, cols in enumerate(pairs):
            q2 = q_ref[pl.ds(r0, ATTN_SPAN), cols] * jnp.asarray(HEAD_DIM ** -0.5, BF16)
            zero = jnp.zeros_like(q2)
            qq = jnp.concatenate([jnp.where(first, q2, zero), jnp.where(first, zero, q2)], axis=0)
            s = lax.dot_general(kz[pl.ds(r0, ATTN_WIN), cols], qq, (((1,), (1,)), ((), ())),
                                preferred_element_type=F32) + bias_ref[hp]
            if masked:
                key = lax.broadcasted_iota(I32, (ATTN_WIN, 2 * ATTN_SPAN), 0)
                s = jnp.where(key >= tq - r0, s, NEG_BIG)
            sc[j % 2, hp] = s

    def finish(j):
        r0 = j * ATTN_SPAN
        old = tq - r0
        probs = []
        for hp in range(n_pairs):
            s = sc[j % 2, hp]
            probs.append(jnp.exp((s - jnp.max(s, axis=0, keepdims=True)).astype(BF16)))
        for hp, (cols, p) in enumerate(zip(pairs, probs)):
            ones = lambda w: jnp.ones((BF16_ROWS, w), BF16)
            ov = (_dot(jnp.concatenate([vp_ref[cols, r0:tq], ones(old)], axis=0), p[0:old])
                  + _dot(jnp.concatenate([vc_ref[cols, 0:ATTN_WIN - old], ones(ATTN_WIN - old)], axis=0), p[old:]))
            o2 = (ov[0:LANES] / ov[LANES:LANES + 1]).T
            o_ref[pl.ds(r0, ATTN_SPAN), cols] = jnp.where(first, o2[0:ATTN_SPAN], o2[ATTN_SPAN:]).astype(o_ref.dtype)

    def run(masked):
        groups = tq // ATTN_SPAN
        score(0, masked)
        for j in range(groups):
            if j + 1 < groups:
                score(j + 1, masked)
            finish(j)

    pl.when(qi == 0)(functools.partial(run, True))
    pl.when(qi != 0)(functools.partial(run, False))


def _attention(q, k, vt, bias_t, layer, bsz, l):
    n, da = q.shape
    tq = ATTN_TQ
    nt = l // tq
    n_pairs = da // LANES
    cur = pl.BlockSpec((tq, da), lambda b, i: (b * nt + i, 0))
    prev = pl.BlockSpec((tq, da), lambda b, i: (b * nt + jnp.maximum(i - 1, 0), 0))
    cur_t = pl.BlockSpec((da, tq), lambda b, i: (0, b * nt + i))
    prev_t = pl.BlockSpec((da, tq), lambda b, i: (0, b * nt + jnp.maximum(i - 1, 0)))
    return pl.pallas_call(
        functools.partial(_attn_kernel, n_pairs=n_pairs),
        grid=(bsz, nt),
        in_specs=[cur, prev, cur, prev_t, cur_t, _layer_spec(bias_t, layer)],
        out_specs=cur,
        out_shape=jax.ShapeDtypeStruct((n, da), BF16),
        scratch_shapes=[pltpu.VMEM((2 * tq, da), BF16), pltpu.VMEM((2, n_pairs, ATTN_WIN, 2 * ATTN_SPAN), F32)],
        compiler_params=_cparams("parallel", "parallel"),
        name="chunk_attn",
    )(q, k, k, vt, vt, bias_t)


def _attn_bias(rel_bias):
    h = rel_bias.shape[0]
    tab = rel_bias.astype(F32)
    n_far = N_PREV * CHUNK - REL_CLIP + CHUNK
    lo = 2 * REL_CLIP - (BAND + CHUNK - 1 - n_far)
    ext = jnp.concatenate([jnp.broadcast_to(tab[:, 2 * REL_CLIP:], (h, n_far)), tab[:, lo:2 * REL_CLIP][:, ::-1]], axis=1)
    wide = BAND + CHUNK
    ring = jnp.concatenate([ext[:, CHUNK - 1:], jnp.zeros((h, 1), F32), ext[:, :CHUNK - 1]], axis=1)
    bias = jnp.broadcast_to(ring[:, None, :], (h, CHUNK, wide)).reshape(h, CHUNK * wide)
    bias = bias[:, :CHUNK * (wide - 1)].reshape(h, CHUNK, wide - 1)[..., :BAND]
    band_t = bias.transpose(0, 2, 1)
    groups = ATTN_SPAN // CHUNK
    per_chunk = [jnp.pad(band_t, ((0, 0), (c * CHUNK, (groups - 1 - c) * CHUNK), (0, 0)), constant_values=NEG_BIG)
                 for c in range(groups)]
    win = jnp.stack(per_chunk, axis=2)
    return win.reshape(h // 2, 2, ATTN_WIN, ATTN_SPAN).transpose(0, 2, 1, 3).reshape(h // 2, ATTN_WIN, 2 * ATTN_SPAN)


def _rglru_kernel(x_ref, gt_ref, cw_ref, cb_ref, w_ref, b_ref, sp_ref, o_ref, xpad, a_s, b_s, hcar):
    t = pl.program_id(1)
    tl, c = x_ref.shape
    front = SUBLANES

    @pl.when(t == 0)
    def _():
        xpad[0:front, :] = jnp.zeros((front, c), F32)
        hcar[...] = jnp.zeros_like(hcar)

    xpad[front:front + tl, :] = x_ref[...]
    xc = cb_ref[...] + sum(
        cw_ref[j:j + 1, :] * xpad[front - (RG_CONV_WIDTH - 1) + j:front - (RG_CONV_WIDTH - 1) + j + tl, :]
        for j in range(RG_CONV_WIDTH))
    xpad[0:front, :] = xpad[tl:tl + front, :]
    pre = _dot(xc.astype(BF16), w_ref[...]) + b_ref[...]
    gx = _sigmoid(pre[:, 0:c])
    ga = _sigmoid(pre[:, c:2 * c])
    log_a = -RG_C * ga * sp_ref[...]
    a = jnp.exp(log_a)
    mult = jnp.sqrt(-jnp.tanh(log_a) * (a * a + 1.0))
    b = mult * gx * xc
    nh = c // LANES
    for j in range(nh):
        a_s[j] = a[:, j * LANES:(j + 1) * LANES]
        b_s[j] = b[:, j * LANES:(j + 1) * LANES]

    step_in_vreg = lax.broadcasted_iota(I32, (SUBLANES, LANES), 0)

    def vreg_scan(r, h_in):
        rows = pl.ds(pl.multiple_of(r * SUBLANES, SUBLANES), SUBLANES)
        out = []
        for j in range(nh):
            av, bv = a_s[j, rows, :], b_s[j, rows, :]
            for d in (1, 2, 4):
                seen = step_in_vreg >= d
                a_prev = jnp.where(seen, pltpu.roll(av, d, 0), 1.0)
                b_prev = jnp.where(seen, pltpu.roll(bv, d, 0), 0.0)
                bv = bv + av * b_prev
                av = av * a_prev
            h = bv + av * h_in[j]
            b_s[j, rows, :] = h
            out.append(h[SUBLANES - 1:SUBLANES, :])
        return tuple(out)

    h_last = lax.fori_loop(0, tl // SUBLANES, vreg_scan,
                           tuple(hcar[:, j * LANES:(j + 1) * LANES] for j in range(nh)), unroll=4)
    for j in range(nh):
        cols = slice(j * LANES, (j + 1) * LANES)
        hcar[:, cols] = h_last[j]
        o_ref[:, cols] = (b_s[j] * _gelu(gt_ref[:, cols])).astype(o_ref.dtype)


def _rglru(xr, gate, conv_w, conv_b, w_bd, b_cat, sp, layer, bsz, l):
    n, c = xr.shape
    tl = min(RG_TL, l)
    nt = l // tl
    row = pl.BlockSpec((tl, c), lambda b, t: (b * nt + t, 0))
    return pl.pallas_call(
        _rglru_kernel,
        grid=(bsz, nt),
        in_specs=[row, row] + [_layer_spec(a, layer) for a in (conv_w, conv_b, w_bd, b_cat, sp)],
        out_specs=row,
        out_shape=jax.ShapeDtypeStruct((n, c), BF16),
        scratch_shapes=[pltpu.VMEM((tl + SUBLANES, c), F32), pltpu.VMEM((c // LANES, tl, LANES), F32),
                        pltpu.VMEM((c // LANES, tl, LANES), F32), pltpu.VMEM((1, c), F32)],
        compiler_params=_cparams("parallel", "arbitrary"),
        name="rglru",
    )(xr, gate, conv_w, conv_b, w_bd, b_cat, sp)


def _block_diag(w):
    nb, d, _ = w.shape
    eye = jnp.eye(nb, dtype=w.dtype)
    return (eye[:, None, :, None] * w[:, :, None, :]).reshape(nb * d, nb * d)


def _s5_kernel(u_ref, m_ref, wab_ref, wo_ref, c1_ref, c2a_ref, c2b_ref, d_ref, y_ref, ps, ug, xa_s, xb_s):
    t, gd = S5_T, S5_GROUP_DIM
    ng, nk, _ = ug.shape
    nh = ps.shape[0]
    per_half = LANES // gd
    rows8 = SUBLANES

    for h in range(nh):
        ps[h] = u_ref[:, h * LANES:(h + 1) * LANES]

    lane_block = lax.broadcasted_iota(I32, (rows8, LANES), 1) // gd

    def block_transpose(vs):
        d = per_half // 2
        while d:
            keep = (lane_block & d) == 0
            out = list(vs)
            for i in range(per_half):
                if i & d == 0:
                    out[i] = jnp.where(keep, vs[i], pltpu.roll(vs[i + d], d * gd, 1))
                    out[i + d] = jnp.where(keep, pltpu.roll(vs[i], LANES - d * gd, 1), vs[i + d])
            vs, d = out, d // 2
        return vs

    def to_groups(r, carry):
        base = pl.multiple_of(r * rows8 * t, rows8 * t)
        rows = pl.ds(pl.multiple_of(r * rows8, rows8), rows8)
        for h in range(nh):
            for j in range(t // per_half):
                steps = [ps[h, pl.ds(base + j * per_half + i, rows8, stride=t), :] for i in range(per_half)]
                for k, v in enumerate(block_transpose(steps)):
                    ug[h * per_half + k, rows, j * LANES:(j + 1) * LANES] = v
        return carry

    lax.fori_loop(0, nk // rows8, to_groups, 0, unroll=2)

    def project(g, carry):
        u = ug[g]
        ub = u.astype(BF16)
        rows = pl.ds(pl.multiple_of(g * nk, nk), nk)
        xab = _dot(ub, wab_ref[g])
        xa_s[rows, :] = xab[:, 0:LANES]
        xb_s[rows, :] = xab[:, LANES:2 * LANES]
        ug[g] = _dot(ub, m_ref[g]) + d_ref[g] * u
        return carry

    lax.fori_loop(0, ng, project, 0, unroll=8)

    c1, c2a, c2b = c1_ref[...], c2a_ref[...], c2b_ref[...]

    def step(k, carry):
        xa, xb = carry
        rows = pl.ds(k, ng, stride=nk)
        ia = xa_s[rows, :]
        ib = xb_s[rows, :]
        xa_s[rows, :] = xa
        return c1 * xa + c2a * xb + ia, c1 * xb + c2b * xa + ib

    z = jnp.zeros((ng, xa_s.shape[1]), F32)
    lax.fori_loop(0, nk, step, (z, z), unroll=8)

    def respond(g, carry):
        rows = pl.ds(pl.multiple_of(g * nk, nk), nk)
        ug[g] = _gelu(ug[g] + _dot(xa_s[rows, :].astype(BF16), wo_ref[g]))
        return carry

    lax.fori_loop(0, ng, respond, 0, unroll=8)

    def from_groups(r, carry):
        base = pl.multiple_of(r * rows8 * t, rows8 * t)
        rows = pl.ds(pl.multiple_of(r * rows8, rows8), rows8)
        for h in range(nh):
            for j in range(t // per_half):
                groups = [ug[h * per_half + k, rows, j * LANES:(j + 1) * LANES] for k in range(per_half)]
                for i, v in enumerate(block_transpose(groups)):
                    ps[h, pl.ds(base + j * per_half + i, rows8, stride=t), :] = v
        return carry

    lax.fori_loop(0, nk // rows8, from_groups, 0, unroll=2)
    for h in range(nh):
        y_ref[:, h * LANES:(h + 1) * LANES] = ps[h].astype(y_ref.dtype)


def _s5(us, prm, layer, bsz, l):
    n, w = us.shape
    ng = w // S5_GROUP_DIM
    nk = l // S5_T
    p2 = prm[2].shape[-2]
    row = pl.BlockSpec((l, w), lambda b: (b, 0))
    return pl.pallas_call(
        _s5_kernel,
        grid=(bsz,),
        in_specs=[row] + [_layer_spec(a, layer) for a in prm],
        out_specs=row,
        out_shape=jax.ShapeDtypeStruct((n, w), BF16),
        scratch_shapes=[pltpu.VMEM((w // LANES, l, LANES), F32), pltpu.VMEM((ng, nk, S5_T * S5_GROUP_DIM), F32),
                        pltpu.VMEM((ng * nk, p2), F32), pltpu.VMEM((ng * nk, p2), F32)],
        compiler_params=_cparams("parallel"),
        name="s5",
    )(us, *prm)


def _s5_params(a_re, a_im, log_dt, b_re, b_im, c_re, c_im, d):
    t = S5_T
    g, p = a_re.shape
    gd = S5_GROUP_DIM
    hi = lax.Precision.HIGHEST
    a = lax.complex(a_re.astype(F32), a_im.astype(F32))
    dt = jnp.exp(log_dt.astype(F32))[:, None]
    a_bar = jnp.exp(a * dt)
    bm = lax.complex(b_re.astype(F32), b_im.astype(F32))
    cm = lax.complex(c_re.astype(F32), c_im.astype(F32))
    b_bar = ((a_bar - 1.0) / a)[..., None] * bm
    steps = jnp.arange(t + 1, dtype=F32)
    pw = jnp.exp((a * dt)[:, None, :] * steps[None, :, None])

    cp = (cm[:, None, :, :] * pw[:, :, None, :]).transpose(0, 3, 1, 2)
    w = t * gd
    lag = cp[:, :, :t].reshape(g, p, w)
    k2 = jnp.einsum('gpc,gpx->gcx', jnp.concatenate([b_bar.real, -b_bar.imag], axis=1),
                    jnp.concatenate([lag.real, lag.imag], axis=1), precision=hi)
    shift = np.zeros((t, w, w), np.float32)
    for s in range(t):
        shift[s, np.arange(w - s * gd), np.arange(w - s * gd) + s * gd] = 1.0
    m = jnp.einsum('gcx,sxl->gscl', k2, shift, precision=hi).reshape(g, w, w)
    win = pw[:, :t][:, ::-1][:, :, :, None] * b_bar[:, None]
    win = win.transpose(0, 1, 3, 2).reshape(g, w, p)
    wab = jnp.concatenate([win.real, win.imag, win.imag, win.real], axis=-1)
    co = cp[:, :, 1:].reshape(g, p, w)
    wo = jnp.concatenate([co.real, -co.imag], axis=1)
    lt = pw[:, t]
    c1 = jnp.concatenate([lt.real, lt.real], axis=-1)
    c2a = jnp.concatenate([-lt.imag, lt.imag], axis=-1)
    c2b = jnp.concatenate([lt.imag, -lt.imag], axis=-1)
    dtile = jnp.tile(d.astype(F32).reshape(g, 1, gd), (1, t, 1)).reshape(g, 1, t * gd)
    return (m.astype(BF16), wab.astype(BF16), wo.astype(BF16), c1, c2a, c2b, dtile)


def _out_proj_kernel(ya_ref, yr_ref, ys_ref, x_ref, gg_ref, wglu_ref, wo_ref, gf_ref, *rest, dense, chunks):
    ys = ys_ref[...]
    ys = ys.astype(F32) * _sigmoid(_dot(ys.astype(BF16), wglu_ref[...].astype(BF16)))
    acc = x_ref[...]
    lo = 0
    for y in (ya_ref[...].astype(F32), yr_ref[...].astype(F32), ys):
        w = y.shape[1]
        acc = acc + _dot(_rms(y, gg_ref[:, lo:lo + w]).astype(BF16), wo_ref[lo:lo + w, :].astype(BF16))
        lo += w
    h = _rms(acc, gf_ref[...])
    hb = h.astype(BF16)
    if dense:
        w1_ref, w3_ref, w2_ref, xo_ref = rest
        xo_ref[...] = acc + _swiglu_chunks(hb, w1_ref, w3_ref, w2_ref, chunks)
        return
    rt_ref, xo_ref, h_ref, lg_ref = rest
    xo_ref[...] = acc
    h_ref[...] = hb
    both = _dot(hb, rt_ref[...])
    h_lo = (h - hb.astype(F32)).astype(BF16)
    lg_ref[...] = both[:, 0:LANES] + both[:, LANES:2 * LANES] + _dot(h_lo, rt_ref[:, 0:LANES])


def _out_proj(ya, yr, ys, x2, gg, wglu_all, wo_all, layer, gf, *, ffn=None, router=None):
    n, dm = x2.shape
    tm = min(ROW_TILE, n)
    row = lambda w: pl.BlockSpec((tm, w), lambda i: (i, 0))
    ins = [ya, yr, ys, x2, gg.reshape(1, -1), wglu_all, wo_all, gf.reshape(1, dm)]
    in_specs = [row(ya.shape[1]), row(yr.shape[1]), row(ys.shape[1]), row(dm), _const_spec((1, gg.shape[0])),
                _layer_spec(wglu_all, layer), _layer_spec(wo_all, layer), _const_spec((1, dm))]
    if ffn is not None:
        ins += list(ffn)
        in_specs += [_const_spec(w.shape) for w in ffn]
        out_specs = row(dm)
        out_shape = jax.ShapeDtypeStruct((n, dm), F32)
        chunks = _ffn_chunks(ffn[0].shape[1], DENSE_CHUNK)
    else:
        ne = router.shape[1]
        r_hi = router.astype(BF16)
        r_lo = (router.astype(F32) - r_hi.astype(F32)).astype(BF16)
        pad = lambda a: jnp.pad(a, ((0, 0), (0, LANES - ne)))
        ins.append(jnp.concatenate([pad(r_hi), pad(r_lo)], axis=1))
        in_specs.append(_const_spec((dm, 2 * LANES)))
        out_specs = [row(dm), row(dm), row(LANES)]
        out_shape = [jax.ShapeDtypeStruct((n, dm), F32), jax.ShapeDtypeStruct((n, dm), BF16),
                     jax.ShapeDtypeStruct((n, LANES), F32)]
        chunks = None
    return pl.pallas_call(
        functools.partial(_out_proj_kernel, dense=ffn is not None, chunks=chunks),
        grid=(n // tm,),
        in_specs=in_specs, out_specs=out_specs, out_shape=out_shape,
        compiler_params=pltpu.CompilerParams(dimension_semantics=("parallel",), vmem_limit_bytes=MIX_VMEM_LIMIT),
        name="out_proj",
    )(*ins)


def _swiglu_chunks(h, w1, w3, w2, chunks, before_chunk=None, after_chunk=None):
    acc = None
    for j, (lo, width) in enumerate(chunks):
        if before_chunk is not None:
            before_chunk(j)
        a = _dot(h, w1[:, lo:lo + width].astype(BF16))
        b = _dot(h, w3[:, lo:lo + width].astype(BF16))
        t = (a * _sigmoid(a) * b).astype(BF16)
        y = _dot(t, w2[lo:lo + width, :].astype(BF16))
        acc = y if acc is None else acc + y
        if after_chunk is not None:
            after_chunk(j)
    return acc


def _ffn_chunks(f, width):
    return tuple((lo, min(width, f - lo)) for lo in range(0, f, width))


def _expert_weight_copies(hbm, vmem, sems, chunks, expert, j):
    h1, h3, h2 = hbm
    v1, v3, v2 = vmem
    lo, w = chunks[j]
    return [pltpu.make_async_copy(h1.at[expert, :, lo:lo + w], v1.at[:, lo:lo + w], sems.at[j, 0]),
            pltpu.make_async_copy(h3.at[expert, :, lo:lo + w], v3.at[:, lo:lo + w], sems.at[j, 1]),
            pltpu.make_async_copy(h2.at[expert, lo:lo + w, :], v2.at[lo:lo + w, :], sems.at[j, 2])]


def _expert_ffn_kernel(brow_ref, bexp_ref, role_ref, x_ref, w1_hbm, w3_hbm, w2_hbm, o_ref, w1_v, w3_v, w2_v, sems,
                       *, chunks):
    i = pl.program_id(0)
    copies = functools.partial(_expert_weight_copies, (w1_hbm, w3_hbm, w2_hbm), (w1_v, w3_v, w2_v), sems, chunks)

    @pl.when(i == 0)
    def _():
        for j in range(len(chunks)):
            for cp in copies(bexp_ref[0], j):
                cp.start()

    def block(first, last):
        def before(j):
            for cp in copies(bexp_ref[i], j):
                cp.wait()

        def after(j):
            for cp in copies(bexp_ref[i + 1], j):
                cp.start()

        o_ref[...] = _swiglu_chunks(x_ref[...], w1_v, w3_v, w2_v, chunks, before if first else None,
                                    after if last else None).astype(o_ref.dtype)

    for role in range(4):
        pl.when(role_ref[i] == role)(functools.partial(block, bool(role & 1), bool(role & 2)))


def _expert_ffn(xb, w1, w3, w2, blk_row, blk_exp, role, max_blocks):
    rows, dm = xb.shape
    f = w1.shape[2]
    chunks = _ffn_chunks(f, EXPERT_CHUNK)
    row = pl.BlockSpec((MOE_BLK, dm), lambda i, br, be, ro: (br[i], 0))
    hbm = pl.BlockSpec(memory_space=pl.ANY)
    return pl.pallas_call(
        functools.partial(_expert_ffn_kernel, chunks=chunks),
        grid_spec=pltpu.PrefetchScalarGridSpec(
            num_scalar_prefetch=3, grid=(max_blocks,),
            in_specs=[row, hbm, hbm, hbm], out_specs=row,
            scratch_shapes=[pltpu.VMEM(w1.shape[1:], w1.dtype), pltpu.VMEM(w3.shape[1:], w3.dtype),
                            pltpu.VMEM(w2.shape[1:], w2.dtype), pltpu.SemaphoreType.DMA((len(chunks), 3))]),
        out_shape=jax.ShapeDtypeStruct((rows, dm), BF16),
        compiler_params=_cparams("arbitrary"),
        name="expert_ffn",
    )(blk_row, blk_exp, role, xb, w1, w3, w2)


def _route_tile(logits_tok, filled, ne):
    tm = logits_tok.shape[0]
    logits = logits_tok.T[0:ne, :]
    eidx = lax.broadcasted_iota(I32, (ne, tm), 0)
    m1 = jnp.max(logits, axis=0, keepdims=True)
    i1 = jnp.min(jnp.where(logits == m1, eidx, ne), axis=0, keepdims=True)
    sel1 = eidx == i1
    rest = jnp.where(sel1, -jnp.inf, logits)
    m2 = jnp.max(rest, axis=0, keepdims=True)
    i2 = jnp.min(jnp.where(rest == m2, eidx, ne), axis=0, keepdims=True)
    sel2 = eidx == i2
    e2 = jnp.exp(m2 - m1)
    den = 1.0 + e2
    rf = jnp.where(sel1 | sel2, 1.0, 0.0)
    cnt = jnp.sum(rf, axis=1, keepdims=True)
    before = (lax.broadcasted_iota(I32, (tm, tm), 0) < lax.broadcasted_iota(I32, (tm, tm), 1))
    rank = _dot(rf.astype(BF16), jnp.where(before, 1.0, 0.0).astype(BF16))
    whole = lambda a: jnp.floor(a * (1.0 / BF16_ROWS)) * BF16_ROWS
    phase = filled - whole(filled)
    span = jnp.where(cnt > 0, whole(phase + cnt + (BF16_ROWS - 1)), 0.0)
    ecol = lax.broadcasted_iota(I32, (ne, 1), 0)
    off = jnp.zeros((ne, 1), F32)
    for j in range(ne - 1):
        off = off + jnp.where(ecol > j, span[j:j + 1, :], 0.0)
    place = off + phase + rank
    pos1 = jnp.sum(jnp.where(sel1, place, 0.0), axis=0, keepdims=True)
    pos2 = jnp.sum(jnp.where(sel2, place, 0.0), axis=0, keepdims=True)
    record = [pos1, pos2, 1.0 / den, e2 / den]
    lane_major = jnp.concatenate(record + [jnp.zeros((SUBLANES - len(record), tm), F32)], axis=0)
    tok_major = jnp.concatenate([lane_major, jnp.zeros((LANES - SUBLANES, tm), F32)], axis=0).T
    return lane_major, tok_major, cnt


def _route_kernel(lg_ref, lane_ref, tok_ref, cnt_ref, filled, *, ne):
    tm = MOE_TILE

    @pl.when(pl.program_id(0) == 0)
    def _():
        filled[...] = jnp.zeros_like(filled)

    for t in range(lg_ref.shape[0] // tm):
        lane_major, tok_major, cnt = _route_tile(lg_ref[t * tm:(t + 1) * tm, :], filled[...], ne)
        lane_ref[:, t * tm:(t + 1) * tm] = lane_major
        tok_ref[t * tm:(t + 1) * tm, :] = tok_major
        cnt_ref[t] = jnp.broadcast_to(cnt, (ne, LANES)).astype(I32)
        filled[...] = filled[...] + cnt


def _route(logits, ne):
    n = logits.shape[0]
    tm = MOE_TILE
    span = min(ROUTE_SPAN, n)
    return pl.pallas_call(
        functools.partial(_route_kernel, ne=ne),
        grid=(n // span,),
        in_specs=[pl.BlockSpec((span, LANES), lambda i: (i, 0))],
        out_specs=[pl.BlockSpec((SUBLANES, span), lambda i: (0, i)), pl.BlockSpec((span, LANES), lambda i: (i, 0)),
                   pl.BlockSpec((span // tm, ne, LANES), lambda i: (i, 0, 0))],
        out_shape=[jax.ShapeDtypeStruct((SUBLANES, n), F32), jax.ShapeDtypeStruct((n, LANES), F32),
                   jax.ShapeDtypeStruct((n // tm, ne, LANES), I32)],
        scratch_shapes=[pltpu.VMEM((ne, 1), F32)],
        compiler_params=_cparams("arbitrary"),
        name="moe_route",
    )(logits)


def _segment_copies(seg_ref, grouped_ref, base_ref, tile_buf, off_ref, sems, tile, slot, ne, *, to_grouped, wait):
    large = [p for p in SEG_PIECES if p >= SEG_LARGE]
    small = [p for p in SEG_PIECES if p < SEG_LARGE]
    for e in range(ne):
        seg = seg_ref[tile * ne + e]

        def pieces(sizes, done, seg=seg, e=e):
            far = base_ref[tile * ne + e] + done
            near = off_ref[tile * ne + e] + done
            for piece in sizes:
                g = grouped_ref.at[pl.ds(pl.multiple_of(far, BF16_ROWS), piece)]
                t = tile_buf.at[slot, pl.ds(pl.multiple_of(near, BF16_ROWS), piece)]
                cp = pltpu.make_async_copy(t, g, sems.at[slot, e]) if to_grouped else \
                    pltpu.make_async_copy(g, t, sems.at[slot, e])
                has = (seg & piece) != 0

                @pl.when(has)
                def _():
                    cp.wait() if wait else cp.start()

                step = jnp.where(has, piece, 0)
                far = far + step
                near = near + step

        pl.when(seg >= SEG_LARGE)(functools.partial(pieces, large, 0))
        pieces(small, seg & -SEG_LARGE)


def _tile_rows(ne):
    return TOP_K * MOE_TILE + ne * 2 * BF16_ROWS


def _tail_copies(end_ref, zeros, xb_ref, sems, ne):
    return [pltpu.make_async_copy(zeros, xb_ref.at[pl.ds(pl.multiple_of(end_ref[e], BF16_ROWS), MOE_BLK)], sems.at[e])
            for e in range(ne)]


def _dispatch_kernel(seg_ref, base_ref, off_ref, phase_ref, end_ref, rt_ref, h_ref, xb_ref, res, zeros, partial,
                     sems, tail_sems):
    i = pl.program_id(0)
    nt = pl.num_programs(0)
    ne = tail_sems.shape[0]
    tm = h_ref.shape[0]
    mrows = res.shape[1]
    slot = i % 2
    copies = functools.partial(_segment_copies, seg_ref, xb_ref, base_ref, res, off_ref, sems, ne=ne, to_grouped=True)

    @pl.when(i == 0)
    def _():
        zeros[...] = jnp.zeros_like(zeros)
        partial[...] = jnp.zeros_like(partial)
        for cp in _tail_copies(end_ref, zeros, xb_ref, tail_sems, ne):
            cp.start()

    row = lax.broadcasted_iota(I32, (mrows, tm), 0)
    pos = rt_ref[0:2, :].astype(I32)
    perm = jnp.where((row == pos[0:1, :]) | (row == pos[1:2, :]), 1.0, 0.0).astype(BF16)
    res[slot] = _dot(perm, h_ref[...]).astype(BF16)

    tile_row = lax.broadcasted_iota(I32, (BF16_ROWS, res.shape[2]), 0)
    for e in range(ne):
        @pl.when(seg_ref[i * ne + e] > 0)
        def _():
            first = pl.ds(pl.multiple_of(off_ref[i * ne + e], BF16_ROWS), BF16_ROWS)
            merged = jnp.where(tile_row < phase_ref[i * ne + e], partial[e].astype(F32), res[slot, first, :].astype(F32))
            res[slot, first, :] = merged.astype(BF16)
            last = pl.ds(pl.multiple_of(off_ref[i * ne + e] + seg_ref[i * ne + e] - BF16_ROWS, BF16_ROWS), BF16_ROWS)
            partial[e] = res[slot, last, :]

    @pl.when(i > 0)
    def _():
        copies(tile=i - 1, slot=1 - slot, wait=True)

    copies(tile=i, slot=slot, wait=False)

    @pl.when(i == nt - 1)
    def _():
        copies(tile=i, slot=slot, wait=True)
        for cp in _tail_copies(end_ref, zeros, xb_ref, tail_sems, ne):
            cp.wait()


def _dispatch(h, rt_lane, lay, ne):
    n, dm = h.shape
    tm = MOE_TILE
    return pl.pallas_call(
        _dispatch_kernel,
        grid_spec=pltpu.PrefetchScalarGridSpec(
            num_scalar_prefetch=5, grid=(n // tm,),
            in_specs=[pl.BlockSpec((SUBLANES, tm), lambda i, *_: (0, i)), pl.BlockSpec((tm, dm), lambda i, *_: (i, 0))],
            out_specs=pl.BlockSpec(memory_space=pl.ANY),
            scratch_shapes=[pltpu.VMEM((2, _tile_rows(ne), dm), BF16), pltpu.VMEM((MOE_BLK, dm), BF16),
                            pltpu.VMEM((ne, BF16_ROWS, dm), BF16),
                            pltpu.SemaphoreType.DMA((2, ne)), pltpu.SemaphoreType.DMA((ne,))]),
        out_shape=jax.ShapeDtypeStruct((lay['rows_total'], dm), BF16),
        compiler_params=_cparams("arbitrary"),
        name="moe_dispatch",
    )(lay['seg'], lay['base'], lay['off'], lay['phase'], lay['end'], rt_lane, h)


def _combine_kernel(seg_ref, base_ref, off_ref, rt_ref, x_ref, gn_ref, yb_ref, o_ref, got, sems, *, ne):
    i = pl.program_id(0)
    steps = pl.num_programs(0)
    tm = MOE_TILE
    mrows = got.shape[1]
    copies = functools.partial(_segment_copies, seg_ref, yb_ref, base_ref, got, off_ref, sems, ne=ne, to_grouped=False)
    slot = lambda step, t: (step % 2) * COMBINE_TILES + t

    def fetch(step, wait):
        for t in range(COMBINE_TILES):
            copies(tile=step * COMBINE_TILES + t, slot=slot(step, t), wait=wait)

    @pl.when(i == 0)
    def _():
        got[...] = jnp.zeros_like(got)
        fetch(i, False)

    pl.when(i + 1 < steps)(lambda: fetch(i + 1, False))
    fetch(i, True)

    col = lax.broadcasted_iota(I32, (tm, mrows), 1)
    for t in range(COMBINE_TILES):
        tok = slice(t * tm, (t + 1) * tm)
        rt = rt_ref[tok, :]
        sel = jnp.concatenate(
            [jnp.where(col == rt[:, k:k + 1].astype(I32), 1.0, 0.0).astype(BF16) for k in range(TOP_K)], axis=0)
        picked = _dot(sel, got[slot(i, t)])
        y = sum(rt[:, TOP_K + k:TOP_K + k + 1] * picked[k * tm:(k + 1) * tm] for k in range(TOP_K))
        o_ref[tok, :] = _rms(x_ref[tok, :] + y, gn_ref[...])


def _combine(yb, rt_tok, x2, gn, lay, ne):
    n, dm = x2.shape
    tm = COMBINE_TILES * MOE_TILE
    mrows = _tile_rows(ne)
    tok = lambda w: pl.BlockSpec((tm, w), lambda i, *_: (i, 0))
    return pl.pallas_call(
        functools.partial(_combine_kernel, ne=ne),
        grid_spec=pltpu.PrefetchScalarGridSpec(
            num_scalar_prefetch=3, grid=(n // tm,),
            in_specs=[tok(LANES), tok(dm), pl.BlockSpec((1, dm), lambda i, *_: (0, 0)),
                      pl.BlockSpec(memory_space=pl.ANY)],
            out_specs=tok(dm),
            scratch_shapes=[pltpu.VMEM((2 * COMBINE_TILES, mrows, dm), BF16),
                            pltpu.SemaphoreType.DMA((2 * COMBINE_TILES, ne))]),
        out_shape=jax.ShapeDtypeStruct((n, dm), F32),
        compiler_params=_cparams("arbitrary"),
        name="moe_combine",
    )(lay['seg'], lay['base'], lay['off'], rt_tok, x2, gn.reshape(1, dm), yb)


def _moe_layout(cnt, n):
    nt, ne = cnt.shape
    blk = MOE_BLK
    whole = lambda a: a // BF16_ROWS * BF16_ROWS
    cap = -(-(n + BF16_ROWS + blk) // blk) * blk
    region = jnp.arange(ne, dtype=I32) * cap
    filled = jnp.cumsum(cnt, axis=0) - cnt
    phase = filled - whole(filled)
    span = jnp.where(cnt > 0, whole(phase + cnt + BF16_ROWS - 1), 0)
    base = (region[None, :] + filled - phase).astype(I32)
    off = (jnp.cumsum(span, axis=1) - span).astype(I32)
    rows_e = jnp.sum(cnt, axis=0)
    nblk_e = (rows_e + blk - 1) // blk
    end = (region + whole(rows_e + BF16_ROWS - 1)).astype(I32)
    ends = jnp.cumsum(nblk_e)
    max_blocks = TOP_K * n // blk + ne
    bid = jnp.arange(max_blocks + 1, dtype=I32)
    bexp = jnp.minimum(jnp.sum((bid[:, None] >= ends[None, :]).astype(I32), axis=1), ne - 1)
    brow = bexp * (cap // blk) + bid - (ends - nblk_e)[bexp]
    nblk = ends[-1]
    last = jnp.maximum(nblk - 1, 0)
    used = bid < nblk
    first = bid == (ends - nblk_e)[bexp]
    hand_over = (bid == ends[bexp] - 1) & (bid < last)
    role = jnp.where(used, first.astype(I32) + 2 * hand_over.astype(I32), 4).astype(I32)
    bexp = jnp.where(used, bexp, bexp[last]).astype(I32)
    brow = jnp.where(used, brow, brow[last]).astype(I32)
    flat = lambda a: a.reshape(-1)
    return dict(seg=flat(span.astype(I32)), base=flat(base), off=flat(off), phase=flat(phase.astype(I32)), end=end,
                brow=brow, bexp=bexp, role=role, rows_total=ne * cap, max_blocks=max_blocks)


def _moe(x2, h, logits, gn, w1, w3, w2):
    n, dm = x2.shape
    ne = w1.shape[0]
    rt_lane, rt_tok, cnt = _route(logits, ne)
    lay = _moe_layout(cnt[:, :, 0], n)
    xb = _dispatch(h, rt_lane, lay, ne)
    yb = _expert_ffn(xb, w1, w3, w2, lay['brow'], lay['bexp'], lay['role'], lay['max_blocks'])
    return _combine(yb, rt_tok, x2, gn, lay, ne)


def kernel(x, norm_mix_g, w_in, attn_rel_bias, rg_conv_w, rg_conv_b, rg_wx, rg_bx, rg_wa, rg_ba, rg_lambda, s5_a_re, s5_a_im, s5_log_dt, s5_b_re, s5_b_im, s5_c_re, s5_c_im, s5_d, s5_w_glu, g_group, w_out, norm_ffn_g, ffn_w1, ffn_w3, ffn_w2, moe_router, moe_w1, moe_w3, moe_w2, final_norm_g):
    bsz, l, dm = x.shape
    depth = w_in.shape[0]
    assert depth == 2 and l % ATTN_TQ == 0 and ATTN_TQ == N_PREV * CHUNK, "dense layer, then the MoE layer"
    d_rg = rg_conv_w.shape[2]
    d_s5 = s5_w_glu.shape[1]
    d_attn = (w_in.shape[2] - 2 * d_rg - d_s5) // 3
    x2 = x.reshape(bsz * l, dm)
    attn_bias = jax.vmap(_attn_bias)(attn_rel_bias)
    rg_gates_w = jnp.concatenate([jax.vmap(_block_diag)(rg_wx), jax.vmap(_block_diag)(rg_wa)], axis=2).astype(BF16)
    rg_gates_b = jnp.concatenate([rg_bx, rg_ba], axis=1).astype(F32)[:, None, :]
    rg_decay = jax.nn.softplus(-rg_lambda.astype(F32))[:, None, :]
    s5_prm = jax.vmap(_s5_params)(s5_a_re, s5_a_im, s5_log_dt, s5_b_re, s5_b_im, s5_c_re, s5_c_im, s5_d)
    for layer in range(depth):
        q, k, vt, xr, gate, us = _in_proj(x2, norm_mix_g[layer], w_in, layer, d_attn, d_rg, d_s5)
        y_attn = _attention(q, k, vt, attn_bias, layer, bsz, l)
        y_rg = _rglru(xr, gate, rg_conv_w.astype(F32), rg_conv_b.astype(F32)[:, None, :], rg_gates_w, rg_gates_b,
                      rg_decay, layer, bsz, l)
        y_s5 = _s5(us, s5_prm, layer, bsz, l)
        mixed = (y_attn, y_rg, y_s5, x2, g_group[layer].astype(F32), s5_w_glu, w_out, layer,
                 norm_ffn_g[layer].astype(F32))
        if layer == 0:
            x2 = _out_proj(*mixed, ffn=(ffn_w1[0], ffn_w3[0], ffn_w2[0]))
        else:
            x2, h, logits = _out_proj(*mixed, router=moe_router[0])
            x2 = _moe(x2, h, logits, final_norm_g.astype(F32), moe_w1[0], moe_w3[0], moe_w2[0])
    return x2.reshape(bsz, l, dm)
```

```python
import functools
import math

import jax
import jax.numpy as jnp
import numpy as np
from jax import lax
from jax.experimental import pallas as pl
from jax.experimental.pallas import tpu as pltpu

F32 = jnp.float32
BF16 = jnp.bfloat16
I32 = jnp.int32

EPS = 1e-6
CHUNK = 64
N_PREV = 8
BAND = (N_PREV + 1) * CHUNK
REL_CLIP = 128
HEAD_DIM = 64
RG_C = 8.0
RG_CONV_WIDTH = 4
S5_GROUP_DIM = 16
S5_T = 16
TOP_K = 2
NEG_BIG = -1e30

LANES = 128
SUBLANES = 8
BF16_ROWS = 16
VMEM_LIMIT = 52 * 1024 * 1024
MIX_VMEM_LIMIT = 58 * 1024 * 1024

ROW_TILE = 512
PROJ_TILE = 1024
ATTN_TQ = 512
ATTN_SPAN = 2 * CHUNK
ATTN_WIN = N_PREV * CHUNK + ATTN_SPAN
RG_TL = 1024
MOE_TILE = 256
MOE_BLK = 512
ROUTE_SPAN = 2048
SEG_PIECES = tuple(BF16_ROWS << b for b in range((MOE_TILE // BF16_ROWS).bit_length() - 1, -1, -1))
COMBINE_TILES = 2
SEG_LARGE = MOE_TILE // 2
DENSE_CHUNK = 512
EXPERT_CHUNK = 512


def _cparams(*sem):
    return pltpu.CompilerParams(dimension_semantics=sem, vmem_limit_bytes=VMEM_LIMIT)


def _const_spec(shape):
    nd = len(shape)
    return pl.BlockSpec(shape, lambda *_: (0,) * nd, pipeline_mode=pl.Buffered(1))


def _rms(xf, g):
    var = jnp.mean(xf * xf, axis=-1, keepdims=True)
    return xf * lax.rsqrt(var + EPS) * g


def _sigmoid(x):
    return 1.0 / (1.0 + jnp.exp(-x))


def _gelu(x):
    c = math.sqrt(2.0 / math.pi)
    return 0.5 * x * (1.0 + jnp.tanh(c * (x + 0.044715 * (x * x * x))))


def _dot(a, b):
    return jnp.dot(a, b, preferred_element_type=F32)


def _in_proj_kernel(x_ref, g_ref, w_ref, wvt_ref, q_ref, k_ref, vt_ref, xr_ref, gt_ref, us_ref, *, d_attn, d_rg):
    u = _rms(x_ref[...], g_ref[...]).astype(BF16)
    lo = 0
    for ref, width in ((q_ref, d_attn), (k_ref, d_attn), (None, d_attn),
                       (xr_ref, d_rg), (gt_ref, d_rg), (us_ref, w_ref.shape[1] - 3 * d_attn - 2 * d_rg)):
        if ref is not None:
            ref[...] = _dot(u, w_ref[:, lo:lo + width].astype(BF16)).astype(ref.dtype)
        lo += width
    vt_ref[...] = lax.dot_general(wvt_ref[...].astype(BF16), u, (((1,), (1,)), ((), ())),
                                  preferred_element_type=F32).astype(vt_ref.dtype)


def _layer_spec(stacked, layer):
    rest = stacked.shape[1:]
    return pl.BlockSpec((None,) + rest, lambda *_: (layer,) + (0,) * len(rest), pipeline_mode=pl.Buffered(1))


def _in_proj(x2, g, w_all, layer, d_attn, d_rg, d_s5):
    n, dm = x2.shape
    tm = min(PROJ_TILE, n)
    row = lambda w: pl.BlockSpec((tm, w), lambda i: (i, 0))
    wv_t = w_all[layer, :, 2 * d_attn:3 * d_attn].T
    return pl.pallas_call(
        functools.partial(_in_proj_kernel, d_attn=d_attn, d_rg=d_rg),
        grid=(n // tm,),
        in_specs=[row(dm), _const_spec((1, dm)), _layer_spec(w_all, layer), _const_spec(wv_t.shape)],
        out_specs=[row(d_attn), row(d_attn), pl.BlockSpec((d_attn, tm), lambda i: (0, i)),
                   row(d_rg), row(d_rg), row(d_s5)],
        out_shape=[jax.ShapeDtypeStruct((n, d_attn), BF16)] * 2 + [jax.ShapeDtypeStruct((d_attn, n), BF16)]
        + [jax.ShapeDtypeStruct((n, d_rg), F32)] * 2 + [jax.ShapeDtypeStruct((n, d_s5), F32)],
        compiler_params=_cparams("parallel"),
        name="in_proj",
    )(x2, g.reshape(1, dm), w_all, wv_t)


def _attn_kernel(q_ref, kp_ref, kc_ref, vp_ref, vc_ref, bias_ref, o_ref, kz, sc, *, n_pairs):
    qi = pl.program_id(1)
    tq = q_ref.shape[0]
    kz[0:tq, :] = kp_ref[...]
    kz[tq:2 * tq, :] = kc_ref[...]
    first = lax.broadcasted_iota(I32, (ATTN_SPAN, LANES), 1) < HEAD_DIM
    pairs = [slice(hp * LANES, (hp + 1) * LANES) for hp in range(n_pairs)]

    def score(j, masked):
        r0 = j * ATTN_SPAN
        for hp, cols in enumerate(pairs):
            q2 = q_ref[pl.ds(r0, ATTN_SPAN), cols] * jnp.asarray(HEAD_DIM ** -0.5, BF16)
            zero = jnp.zeros_like(q2)
            qq = jnp.concatenate([jnp.where(first, q2, zero), jnp.where(first, zero, q2)], axis=0)
            s = lax.dot_general(kz[pl.ds(r0, ATTN_WIN), cols], qq, (((1,), (1,)), ((), ())),
                                preferred_element_type=F32) + bias_ref[hp]
            if masked:
                key = lax.broadcasted_iota(I32, (ATTN_WIN, 2 * ATTN_SPAN), 0)
                s = jnp.where(key >= tq - r0, s, NEG_BIG)
            sc[j % 2, hp] = s

    def finish(j):
        r0 = j * ATTN_SPAN
        old = tq - r0
        probs = []
        for hp in range(n_pairs):
            s = sc[j % 2, hp]
            probs.append(jnp.exp((s - jnp.max(s, axis=0, keepdims=True)).astype(BF16)))
        for hp, (cols, p) in enumerate(zip(pairs, probs)):
            ones = lambda w: jnp.ones((BF16_ROWS, w), BF16)
            ov = (_dot(jnp.concatenate([vp_ref[cols, r0:tq], ones(old)], axis=0), p[0:old])
                  + _dot(jnp.concatenate([vc_ref[cols, 0:ATTN_WIN - old], ones(ATTN_WIN - old)], axis=0), p[old:]))
            o2 = (ov[0:LANES] / ov[LANES:LANES + 1]).T
            o_ref[pl.ds(r0, ATTN_SPAN), cols] = jnp.where(first, o2[0:ATTN_SPAN], o2[ATTN_SPAN:]).astype(o_ref.dtype)

    def run(masked):
        groups = tq // ATTN_SPAN
        score(0, masked)
        for j in range(groups):
            if j + 1 < groups:
                score(j + 1, masked)
            finish(j)

    pl.when(qi == 0)(functools.partial(run, True))
    pl.when(qi != 0)(functools.partial(run, False))


def _attention(q, k, vt, bias_t, layer, bsz, l):
    n, da = q.shape
    tq = ATTN_TQ
    nt = l // tq
    n_pairs = da // LANES
    cur = pl.BlockSpec((tq, da), lambda b, i: (b * nt + i, 0))
    prev = pl.BlockSpec((tq, da), lambda b, i: (b * nt + jnp.maximum(i - 1, 0), 0))
    cur_t = pl.BlockSpec((da, tq), lambda b, i: (0, b * nt + i))
    prev_t = pl.BlockSpec((da, tq), lambda b, i: (0, b * nt + jnp.maximum(i - 1, 0)))
    return pl.pallas_call(
        functools.partial(_attn_kernel, n_pairs=n_pairs),
        grid=(bsz, nt),
        in_specs=[cur, prev, cur, prev_t, cur_t, _layer_spec(bias_t, layer)],
        out_specs=cur,
        out_shape=jax.ShapeDtypeStruct((n, da), BF16),
        scratch_shapes=[pltpu.VMEM((2 * tq, da), BF16), pltpu.VMEM((2, n_pairs, ATTN_WIN, 2 * ATTN_SPAN), F32)],
        compiler_params=_cparams("parallel", "parallel"),
        name="chunk_attn",
    )(q, k, k, vt, vt, bias_t)


def _attn_bias(rel_bias):
    h = rel_bias.shape[0]
    tab = rel_bias.astype(F32)
    n_far = N_PREV * CHUNK - REL_CLIP + CHUNK
    lo = 2 * REL_CLIP - (BAND + CHUNK - 1 - n_far)
    ext = jnp.concatenate([jnp.broadcast_to(tab[:, 2 * REL_CLIP:], (h, n_far)), tab[:, lo:2 * REL_CLIP][:, ::-1]], axis=1)
    wide = BAND + CHUNK
    ring = jnp.concatenate([ext[:, CHUNK - 1:], jnp.zeros((h, 1), F32), ext[:, :CHUNK - 1]], axis=1)
    bias = jnp.broadcast_to(ring[:, None, :], (h, CHUNK, wide)).reshape(h, CHUNK * wide)
    bias = bias[:, :CHUNK * (wide - 1)].reshape(h, CHUNK, wide - 1)[..., :BAND]
    band_t = bias.transpose(0, 2, 1)
    groups = ATTN_SPAN // CHUNK
    per_chunk = [jnp.pad(band_t, ((0, 0), (c * CHUNK, (groups - 1 - c) * CHUNK), (0, 0)), constant_values=NEG_BIG)
                 for c in range(groups)]
    win = jnp.stack(per_chunk, axis=2)
    return win.reshape(h // 2, 2, ATTN_WIN, ATTN_SPAN).transpose(0, 2, 1, 3).reshape(h // 2, ATTN_WIN, 2 * ATTN_SPAN)


def _rglru_kernel(x_ref, gt_ref, cw_ref, cb_ref, w_ref, b_ref, sp_ref, o_ref, xpad, a_s, b_s, hcar):
    t = pl.program_id(1)
    tl, c = x_ref.shape
    front = SUBLANES

    @pl.when(t == 0)
    def _():
        xpad[0:front, :] = jnp.zeros((front, c), F32)
        hcar[...] = jnp.zeros_like(hcar)

    xpad[front:front + tl, :] = x_ref[...]
    xc = cb_ref[...] + sum(
        cw_ref[j:j + 1, :] * xpad[front - (RG_CONV_WIDTH - 1) + j:front - (RG_CONV_WIDTH - 1) + j + tl, :]
        for j in range(RG_CONV_WIDTH))
    xpad[0:front, :] = xpad[tl:tl + front, :]
    pre = _dot(xc.astype(BF16), w_ref[...]) + b_ref[...]
    gx = _sigmoid(pre[:, 0:c])
    ga = _sigmoid(pre[:, c:2 * c])
    log_a = -RG_C * ga * sp_ref[...]
    a = jnp.exp(log_a)
    mult = jnp.sqrt(-jnp.tanh(log_a) * (a * a + 1.0))
    b = mult * gx * xc
    nh = c // LANES
    for j in range(nh):
        a_s[j] = a[:, j * LANES:(j + 1) * LANES]
        b_s[j] = b[:, j * LANES:(j + 1) * LANES]

    step_in_vreg = lax.broadcasted_iota(I32, (SUBLANES, LANES), 0)

    def vreg_scan(r, h_in):
        rows = pl.ds(pl.multiple_of(r * SUBLANES, SUBLANES), SUBLANES)
        out = []
        for j in range(nh):
            av, bv = a_s[j, rows, :], b_s[j, rows, :]
            for d in (1, 2, 4):
                seen = step_in_vreg >= d
                a_prev = jnp.where(seen, pltpu.roll(av, d, 0), 1.0)
                b_prev = jnp.where(seen, pltpu.roll(bv, d, 0), 0.0)
                bv = bv + av * b_prev
                av = av * a_prev
            h = bv + av * h_in[j]
            b_s[j, rows, :] = h
            out.append(h[SUBLANES - 1:SUBLANES, :])
        return tuple(out)

    h_last = lax.fori_loop(0, tl // SUBLANES, vreg_scan,
                           tuple(hcar[:, j * LANES:(j + 1) * LANES] for j in range(nh)), unroll=4)
    for j in range(nh):
        cols = slice(j * LANES, (j + 1) * LANES)
        hcar[:, cols] = h_last[j]
        o_ref[:, cols] = (b_s[j] * _gelu(gt_ref[:, cols])).astype(o_ref.dtype)


def _rglru(xr, gate, conv_w, conv_b, w_bd, b_cat, sp, layer, bsz, l):
    n, c = xr.shape
    tl = min(RG_TL, l)
    nt = l // tl
    row = pl.BlockSpec((tl, c), lambda b, t: (b * nt + t, 0))
    return pl.pallas_call(
        _rglru_kernel,
        grid=(bsz, nt),
        in_specs=[row, row] + [_layer_spec(a, layer) for a in (conv_w, conv_b, w_bd, b_cat, sp)],
        out_specs=row,
        out_shape=jax.ShapeDtypeStruct((n, c), BF16),
        scratch_shapes=[pltpu.VMEM((tl + SUBLANES, c), F32), pltpu.VMEM((c // LANES, tl, LANES), F32),
                        pltpu.VMEM((c // LANES, tl, LANES), F32), pltpu.VMEM((1, c), F32)],
        compiler_params=_cparams("parallel", "arbitrary"),
        name="rglru",
    )(xr, gate, conv_w, conv_b, w_bd, b_cat, sp)


def _block_diag(w):
    nb, d, _ = w.shape
    eye = jnp.eye(nb, dtype=w.dtype)
    return (eye[:, None, :, None] * w[:, :, None, :]).reshape(nb * d, nb * d)


def _s5_kernel(u_ref, m_ref, wab_ref, wo_ref, c1_ref, c2a_ref, c2b_ref, d_ref, y_ref, ps, ug, xa_s, xb_s):
    t, gd = S5_T, S5_GROUP_DIM
    ng, nk, _ = ug.shape
    nh = ps.shape[0]
    per_half = LANES // gd
    rows8 = SUBLANES

    for h in range(nh):
        ps[h] = u_ref[:, h * LANES:(h + 1) * LANES]

    lane_block = lax.broadcasted_iota(I32, (rows8, LANES), 1) // gd

    def block_transpose(vs):
        d = per_half // 2
        while d:
            keep = (lane_block & d) == 0
            out = list(vs)
            for i in range(per_half):
                if i & d == 0:
                    out[i] = jnp.where(keep, vs[i], pltpu.roll(vs[i + d], d * gd, 1))
                    out[i + d] = jnp.where(keep, pltpu.roll(vs[i], LANES - d * gd, 1), vs[i + d])
            vs, d = out, d // 2
        return vs

    def to_groups(r, carry):
        base = pl.multiple_of(r * rows8 * t, rows8 * t)
        rows = pl.ds(pl.multiple_of(r * rows8, rows8), rows8)
        for h in range(nh):
            for j in range(t // per_half):
                steps = [ps[h, pl.ds(base + j * per_half + i, rows8, stride=t), :] for i in range(per_half)]
                for k, v in enumerate(block_transpose(steps)):
                    ug[h * per_half + k, rows, j * LANES:(j + 1) * LANES] = v
        return carry

    lax.fori_loop(0, nk // rows8, to_groups, 0, unroll=2)

    def project(g, carry):
        u = ug[g]
        ub = u.astype(BF16)
        rows = pl.ds(pl.multiple_of(g * nk, nk), nk)
        xab = _dot(ub, wab_ref[g])
        xa_s[rows, :] = xab[:, 0:LANES]
        xb_s[rows, :] = xab[:, LANES:2 * LANES]
        ug[g] = _dot(ub, m_ref[g]) + d_ref[g] * u
        return carry

    lax.fori_loop(0, ng, project, 0, unroll=8)

    c1, c2a, c2b = c1_ref[...], c2a_ref[...], c2b_ref[...]

    def step(k, carry):
        xa, xb = carry
        rows = pl.ds(k, ng, stride=nk)
        ia = xa_s[rows, :]
        ib = xb_s[rows, :]
        xa_s[rows, :] = xa
        return c1 * xa + c2a * xb + ia, c1 * xb + c2b * xa + ib

    z = jnp.zeros((ng, xa_s.shape[1]), F32)
    lax.fori_loop(0, nk, step, (z, z), unroll=8)

    def respond(g, carry):
        rows = pl.ds(pl.multiple_of(g * nk, nk), nk)
        ug[g] = _gelu(ug[g] + _dot(xa_s[rows, :].astype(BF16), wo_ref[g]))
        return carry

    lax.fori_loop(0, ng, respond, 0, unroll=8)

    def from_groups(r, carry):
        base = pl.multiple_of(r * rows8 * t, rows8 * t)
        rows = pl.ds(pl.multiple_of(r * rows8, rows8), rows8)
        for h in range(nh):
            for j in range(t // per_half):
                groups = [ug[h * per_half + k, rows, j * LANES:(j + 1) * LANES] for k in range(per_half)]
                for i, v in enumerate(block_transpose(groups)):
                    ps[h, pl.ds(base + j * per_half + i, rows8, stride=t), :] = v
        return carry

    lax.fori_loop(0, nk // rows8, from_groups, 0, unroll=2)
    for h in range(nh):
        y_ref[:, h * LANES:(h + 1) * LANES] = ps[h].astype(y_ref.dtype)


def _s5(us, prm, layer, bsz, l):
    n, w = us.shape
    ng = w // S5_GROUP_DIM
    nk = l // S5_T
    p2 = prm[2].shape[-2]
    row = pl.BlockSpec((l, w), lambda b: (b, 0))
    return pl.pallas_call(
        _s5_kernel,
        grid=(bsz,),
        in_specs=[row] + [_layer_spec(a, layer) for a in prm],
        out_specs=row,
        out_shape=jax.ShapeDtypeStruct((n, w), BF16),
        scratch_shapes=[pltpu.VMEM((w // LANES, l, LANES), F32), pltpu.VMEM((ng, nk, S5_T * S5_GROUP_DIM), F32),
                        pltpu.VMEM((ng * nk, p2), F32), pltpu.VMEM((ng * nk, p2), F32)],
        compiler_params=_cparams("parallel"),
        name="s5",
    )(us, *prm)


def _s5_params(a_re, a_im, log_dt, b_re, b_im, c_re, c_im, d):
    t = S5_T
    g, p = a_re.shape
    gd = S5_GROUP_DIM
    hi = lax.Precision.HIGHEST
    a = lax.complex(a_re.astype(F32), a_im.astype(F32))
    dt = jnp.exp(log_dt.astype(F32))[:, None]
    a_bar = jnp.exp(a * dt)
    bm = lax.complex(b_re.astype(F32), b_im.astype(F32))
    cm = lax.complex(c_re.astype(F32), c_im.astype(F32))
    b_bar = ((a_bar - 1.0) / a)[..., None] * bm
    steps = jnp.arange(t + 1, dtype=F32)
    pw = jnp.exp((a * dt)[:, None, :] * steps[None, :, None])

    cp = (cm[:, None, :, :] * pw[:, :, None, :]).transpose(0, 3, 1, 2)
    w = t * gd
    lag = cp[:, :, :t].reshape(g, p, w)
    k2 = jnp.einsum('gpc,gpx->gcx', jnp.concatenate([b_bar.real, -b_bar.imag], axis=1),
                    jnp.concatenate([lag.real, lag.imag], axis=1), precision=hi)
    shift = np.zeros((t, w, w), np.float32)
    for s in range(t):
        shift[s, np.arange(w - s * gd), np.arange(w - s * gd) + s * gd] = 1.0
    m = jnp.einsum('gcx,sxl->gscl', k2, shift, precision=hi).reshape(g, w, w)
    win = pw[:, :t][:, ::-1][:, :, :, None] * b_bar[:, None]
    win = win.transpose(0, 1, 3, 2).reshape(g, w, p)
    wab = jnp.concatenate([win.real, win.imag, win.imag, win.real], axis=-1)
    co = cp[:, :, 1:].reshape(g, p, w)
    wo = jnp.concatenate([co.real, -co.imag], axis=1)
    lt = pw[:, t]
    c1 = jnp.concatenate([lt.real, lt.real], axis=-1)
    c2a = jnp.concatenate([-lt.imag, lt.imag], axis=-1)
    c2b = jnp.concatenate([lt.imag, -lt.imag], axis=-1)
    dtile = jnp.tile(d.astype(F32).reshape(g, 1, gd), (1, t, 1)).reshape(g, 1, t * gd)
    return (m.astype(BF16), wab.astype(BF16), wo.astype(BF16), c1, c2a, c2b, dtile)


def _out_proj_kernel(ya_ref, yr_ref, ys_ref, x_ref, gg_ref, wglu_ref, wo_ref, gf_ref, *rest, dense, chunks):
    ys = ys_ref[...]
    ys = ys.astype(F32) * _sigmoid(_dot(ys.astype(BF16), wglu_ref[...].astype(BF16)))
    acc = x_ref[...]
    lo = 0
    for y in (ya_ref[...].astype(F32), yr_ref[...].astype(F32), ys):
        w = y.shape[1]
        acc = acc + _dot(_rms(y, gg_ref[:, lo:lo + w]).astype(BF16), wo_ref[lo:lo + w, :].astype(BF16))
        lo += w
    h = _rms(acc, gf_ref[...])
    hb = h.astype(BF16)
    if dense:
        w1_ref, w3_ref, w2_ref, xo_ref = rest
        xo_ref[...] = acc + _swiglu_chunks(hb, w1_ref, w3_ref, w2_ref, chunks)
        return
    rt_ref, xo_ref, h_ref, lg_ref = rest
    xo_ref[...] = acc
    h_ref[...] = hb
    both = _dot(hb, rt_ref[...])
    h_lo = (h - hb.astype(F32)).astype(BF16)
    lg_ref[...] = both[:, 0:LANES] + both[:, LANES:2 * LANES] + _dot(h_lo, rt_ref[:, 0:LANES])


def _out_proj(ya, yr, ys, x2, gg, wglu_all, wo_all, layer, gf, *, ffn=None, router=None):
    n, dm = x2.shape
    tm = min(ROW_TILE, n)
    row = lambda w: pl.BlockSpec((tm, w), lambda i: (i, 0))
    ins = [ya, yr, ys, x2, gg.reshape(1, -1), wglu_all, wo_all, gf.reshape(1, dm)]
    in_specs = [row(ya.shape[1]), row(yr.shape[1]), row(ys.shape[1]), row(dm), _const_spec((1, gg.shape[0])),
                _layer_spec(wglu_all, layer), _layer_spec(wo_all, layer), _const_spec((1, dm))]
    if ffn is not None:
        ins += list(ffn)
        in_specs += [_const_spec(w.shape) for w in ffn]
        out_specs = row(dm)
        out_shape = jax.ShapeDtypeStruct((n, dm), F32)
        chunks = _ffn_chunks(ffn[0].shape[1], DENSE_CHUNK)
    else:
        ne = router.shape[1]
        r_hi = router.astype(BF16)
        r_lo = (router.astype(F32) - r_hi.astype(F32)).astype(BF16)
        pad = lambda a: jnp.pad(a, ((0, 0), (0, LANES - ne)))
        ins.append(jnp.concatenate([pad(r_hi), pad(r_lo)], axis=1))
        in_specs.append(_const_spec((dm, 2 * LANES)))
        out_specs = [row(dm), row(dm), row(LANES)]
        out_shape = [jax.ShapeDtypeStruct((n, dm), F32), jax.ShapeDtypeStruct((n, dm), BF16),
                     jax.ShapeDtypeStruct((n, LANES), F32)]
        chunks = None
    return pl.pallas_call(
        functools.partial(_out_proj_kernel, dense=ffn is not None, chunks=chunks),
        grid=(n // tm,),
        in_specs=in_specs, out_specs=out_specs, out_shape=out_shape,
        compiler_params=pltpu.CompilerParams(dimension_semantics=("parallel",), vmem_limit_bytes=MIX_VMEM_LIMIT),
        name="out_proj",
    )(*ins)


def _swiglu_chunks(h, w1, w3, w2, chunks, before_chunk=None, after_chunk=None):
    acc = None
    for j, (lo, width) in enumerate(chunks):
        if before_chunk is not None:
            before_chunk(j)
        a = _dot(h, w1[:, lo:lo + width].astype(BF16))
        b = _dot(h, w3[:, lo:lo + width].astype(BF16))
        t = (a * _sigmoid(a) * b).astype(BF16)
        y = _dot(t, w2[lo:lo + width, :].astype(BF16))
        acc = y if acc is None else acc + y
        if after_chunk is not None:
            after_chunk(j)
    return acc


def _ffn_chunks(f, width):
    return tuple((lo, min(width, f - lo)) for lo in range(0, f, width))


def _expert_weight_copies(hbm, vmem, sems, chunks, expert, j):
    h1, h3, h2 = hbm
    v1, v3, v2 = vmem
    lo, w = chunks[j]
    return [pltpu.make_async_copy(h1.at[expert, :, lo:lo + w], v1.at[:, lo:lo + w], sems.at[j, 0]),
            pltpu.make_async_copy(h3.at[expert, :, lo:lo + w], v3.at[:, lo:lo + w], sems.at[j, 1]),
            pltpu.make_async_copy(h2.at[expert, lo:lo + w, :], v2.at[lo:lo + w, :], sems.at[j, 2])]


def _expert_ffn_kernel(brow_ref, bexp_ref, role_ref, x_ref, w1_hbm, w3_hbm, w2_hbm, o_ref, w1_v, w3_v, w2_v, sems,
                       *, chunks):
    i = pl.program_id(0)
    copies = functools.partial(_expert_weight_copies, (w1_hbm, w3_hbm, w2_hbm), (w1_v, w3_v, w2_v), sems, chunks)

    @pl.when(i == 0)
    def _():
        for j in range(len(chunks)):
            for cp in copies(bexp_ref[0], j):
                cp.start()

    def block(first, last):
        def before(j):
            for cp in copies(bexp_ref[i], j):
                cp.wait()

        def after(j):
            for cp in copies(bexp_ref[i + 1], j):
                cp.start()

        o_ref[...] = _swiglu_chunks(x_ref[...], w1_v, w3_v, w2_v, chunks, before if first else None,
                                    after if last else None).astype(o_ref.dtype)

    for role in range(4):
        pl.when(role_ref[i] == role)(functools.partial(block, bool(role & 1), bool(role & 2)))


def _expert_ffn(xb, w1, w3, w2, blk_row, blk_exp, role, max_blocks):
    rows, dm = xb.shape
    f = w1.shape[2]
    chunks = _ffn_chunks(f, EXPERT_CHUNK)
    row = pl.BlockSpec((MOE_BLK, dm), lambda i, br, be, ro: (br[i], 0))
    hbm = pl.BlockSpec(memory_space=pl.ANY)
    return pl.pallas_call(
        functools.partial(_expert_ffn_kernel, chunks=chunks),
        grid_spec=pltpu.PrefetchScalarGridSpec(
            num_scalar_prefetch=3, grid=(max_blocks,),
            in_specs=[row, hbm, hbm, hbm], out_specs=row,
            scratch_shapes=[pltpu.VMEM(w1.shape[1:], w1.dtype), pltpu.VMEM(w3.shape[1:], w3.dtype),
                            pltpu.VMEM(w2.shape[1:], w2.dtype), pltpu.SemaphoreType.DMA((len(chunks), 3))]),
        out_shape=jax.ShapeDtypeStruct((rows, dm), BF16),
        compiler_params=_cparams("arbitrary"),
        name="expert_ffn",
    )(blk_row, blk_exp, role, xb, w1, w3, w2)


def _route_tile(logits_tok, filled, ne):
    tm = logits_tok.shape[0]
    logits = logits_tok.T[0:ne, :]
    eidx = lax.broadcasted_iota(I32, (ne, tm), 0)
    m1 = jnp.max(logits, axis=0, keepdims=True)
    i1 = jnp.min(jnp.where(logits == m1, eidx, ne), axis=0, keepdims=True)
    sel1 = eidx == i1
    rest = jnp.where(sel1, -jnp.inf, logits)
    m2 = jnp.max(rest, axis=0, keepdims=True)
    i2 = jnp.min(jnp.where(rest == m2, eidx, ne), axis=0, keepdims=True)
    sel2 = eidx == i2
    e2 = jnp.exp(m2 - m1)
    den = 1.0 + e2
    rf = jnp.where(sel1 | sel2, 1.0, 0.0)
    cnt = jnp.sum(rf, axis=1, keepdims=True)
    before = (lax.broadcasted_iota(I32, (tm, tm), 0) < lax.broadcasted_iota(I32, (tm, tm), 1))
    rank = _dot(rf.astype(BF16), jnp.where(before, 1.0, 0.0).astype(BF16))
    whole = lambda a: jnp.floor(a * (1.0 / BF16_ROWS)) * BF16_ROWS
    phase = filled - whole(filled)
    span = jnp.where(cnt > 0, whole(phase + cnt + (BF16_ROWS - 1)), 0.0)
    ecol = lax.broadcasted_iota(I32, (ne, 1), 0)
    off = jnp.zeros((ne, 1), F32)
    for j in range(ne - 1):
        off = off + jnp.where(ecol > j, span[j:j + 1, :], 0.0)
    place = off + phase + rank
    pos1 = jnp.sum(jnp.where(sel1, place, 0.0), axis=0, keepdims=True)
    pos2 = jnp.sum(jnp.where(sel2, place, 0.0), axis=0, keepdims=True)
    record = [pos1, pos2, 1.0 / den, e2 / den]
    lane_major = jnp.concatenate(record + [jnp.zeros((SUBLANES - len(record), tm), F32)], axis=0)
    tok_major = jnp.concatenate([lane_major, jnp.zeros((LANES - SUBLANES, tm), F32)], axis=0).T
    return lane_major, tok_major, cnt


def _route_kernel(lg_ref, lane_ref, tok_ref, cnt_ref, filled, *, ne):
    tm = MOE_TILE

    @pl.when(pl.program_id(0) == 0)
    def _():
        filled[...] = jnp.zeros_like(filled)

    for t in range(lg_ref.shape[0] // tm):
        lane_major, tok_major, cnt = _route_tile(lg_ref[t * tm:(t + 1) * tm, :], filled[...], ne)
        lane_ref[:, t * tm:(t + 1) * tm] = lane_major
        tok_ref[t * tm:(t + 1) * tm, :] = tok_major
        cnt_ref[t] = jnp.broadcast_to(cnt, (ne, LANES)).astype(I32)
        filled[...] = filled[...] + cnt


def _route(logits, ne):
    n = logits.shape[0]
    tm = MOE_TILE
    span = min(ROUTE_SPAN, n)
    return pl.pallas_call(
        functools.partial(_route_kernel, ne=ne),
        grid=(n // span,),
        in_specs=[pl.BlockSpec((span, LANES), lambda i: (i, 0))],
        out_specs=[pl.BlockSpec((SUBLANES, span), lambda i: (0, i)), pl.BlockSpec((span, LANES), lambda i: (i, 0)),
                   pl.BlockSpec((span // tm, ne, LANES), lambda i: (i, 0, 0))],
        out_shape=[jax.ShapeDtypeStruct((SUBLANES, n), F32), jax.ShapeDtypeStruct((n, LANES), F32),
                   jax.ShapeDtypeStruct((n // tm, ne, LANES), I32)],
        scratch_shapes=[pltpu.VMEM((ne, 1), F32)],
        compiler_params=_cparams("arbitrary"),
        name="moe_route",
    )(logits)


def _segment_copies(seg_ref, grouped_ref, base_ref, tile_buf, off_ref, sems, tile, slot, ne, *, to_grouped, wait):
    large = [p for p in SEG_PIECES if p >= SEG_LARGE]
    small = [p for p in SEG_PIECES if p < SEG_LARGE]
    for e in range(ne):
        seg = seg_ref[tile * ne + e]

        def pieces(sizes, done, seg=seg, e=e):
            far = base_ref[tile * ne + e] + done
            near = off_ref[tile * ne + e] + done
            for piece in sizes:
                g = grouped_ref.at[pl.ds(pl.multiple_of(far, BF16_ROWS), piece)]
                t = tile_buf.at[slot, pl.ds(pl.multiple_of(near, BF16_ROWS), piece)]
                cp = pltpu.make_async_copy(t, g, sems.at[slot, e]) if to_grouped else \
                    pltpu.make_async_copy(g, t, sems.at[slot, e])
                has = (seg & piece) != 0

                @pl.when(has)
                def _():
                    cp.wait() if wait else cp.start()

                step = jnp.where(has, piece, 0)
                far = far + step
                near = near + step

        pl.when(seg >= SEG_LARGE)(functools.partial(pieces, large, 0))
        pieces(small, seg & -SEG_LARGE)


def _tile_rows(ne):
    return TOP_K * MOE_TILE + ne * 2 * BF16_ROWS


def _tail_copies(end_ref, zeros, xb_ref, sems, ne):
    return [pltpu.make_async_copy(zeros, xb_ref.at[pl.ds(pl.multiple_of(end_ref[e], BF16_ROWS), MOE_BLK)], sems.at[e])
            for e in range(ne)]


def _dispatch_kernel(seg_ref, base_ref, off_ref, phase_ref, end_ref, rt_ref, h_ref, xb_ref, res, zeros, partial,
                     sems, tail_sems):
    i = pl.program_id(0)
    nt = pl.num_programs(0)
    ne = tail_sems.shape[0]
    tm = h_ref.shape[0]
    mrows = res.shape[1]
    slot = i % 2
    copies = functools.partial(_segment_copies, seg_ref, xb_ref, base_ref, res, off_ref, sems, ne=ne, to_grouped=True)

    @pl.when(i == 0)
    def _():
        zeros[...] = jnp.zeros_like(zeros)
        partial[...] = jnp.zeros_like(partial)
        for cp in _tail_copies(end_ref, zeros, xb_ref, tail_sems, ne):
            cp.start()

    row = lax.broadcasted_iota(I32, (mrows, tm), 0)
    pos = rt_ref[0:2, :].astype(I32)
    perm = jnp.where((row == pos[0:1, :]) | (row == pos[1:2, :]), 1.0, 0.0).astype(BF16)
    res[slot] = _dot(perm, h_ref[...]).astype(BF16)

    tile_row = lax.broadcasted_iota(I32, (BF16_ROWS, res.shape[2]), 0)
    for e in range(ne):
        @pl.when(seg_ref[i * ne + e] > 0)
        def _():
            first = pl.ds(pl.multiple_of(off_ref[i * ne + e], BF16_ROWS), BF16_ROWS)
            merged = jnp.where(tile_row < phase_ref[i * ne + e], partial[e].astype(F32), res[slot, first, :].astype(F32))
            res[slot, first, :] = merged.astype(BF16)
            last = pl.ds(pl.multiple_of(off_ref[i * ne + e] + seg_ref[i * ne + e] - BF16_ROWS, BF16_ROWS), BF16_ROWS)
            partial[e] = res[slot, last, :]

    @pl.when(i > 0)
    def _():
        copies(tile=i - 1, slot=1 - slot, wait=True)

    copies(tile=i, slot=slot, wait=False)

    @pl.when(i == nt - 1)
    def _():
        copies(tile=i, slot=slot, wait=True)
        for cp in _tail_copies(end_ref, zeros, xb_ref, tail_sems, ne):
            cp.wait()


def _dispatch(h, rt_lane, lay, ne):
    n, dm = h.shape
    tm = MOE_TILE
    return pl.pallas_call(
        _dispatch_kernel,
        grid_spec=pltpu.PrefetchScalarGridSpec(
            num_scalar_prefetch=5, grid=(n // tm,),
            in_specs=[pl.BlockSpec((SUBLANES, tm), lambda i, *_: (0, i)), pl.BlockSpec((tm, dm), lambda i, *_: (i, 0))],
            out_specs=pl.BlockSpec(memory_space=pl.ANY),
            scratch_shapes=[pltpu.VMEM((2, _tile_rows(ne), dm), BF16), pltpu.VMEM((MOE_BLK, dm), BF16),
                            pltpu.VMEM((ne, BF16_ROWS, dm), BF16),
                            pltpu.SemaphoreType.DMA((2, ne)), pltpu.SemaphoreType.DMA((ne,))]),
        out_shape=jax.ShapeDtypeStruct((lay['rows_total'], dm), BF16),
        compiler_params=_cparams("arbitrary"),
        name="moe_dispatch",
    )(lay['seg'], lay['base'], lay['off'], lay['phase'], lay['end'], rt_lane, h)


def _combine_kernel(seg_ref, base_ref, off_ref, rt_ref, x_ref, gn_ref, yb_ref, o_ref, got, sems, *, ne):
    i = pl.program_id(0)
    steps = pl.num_programs(0)
    tm = MOE_TILE
    mrows = got.shape[1]
    copies = functools.partial(_segment_copies, seg_ref, yb_ref, base_ref, got, off_ref, sems, ne=ne, to_grouped=False)
    slot = lambda step, t: (step % 2) * COMBINE_TILES + t

    def fetch(step, wait):
        for t in range(COMBINE_TILES):
            copies(tile=step * COMBINE_TILES + t, slot=slot(step, t), wait=wait)

    @pl.when(i == 0)
    def _():
        got[...] = jnp.zeros_like(got)
        fetch(i, False)

    pl.when(i + 1 < steps)(lambda: fetch(i + 1, False))
    fetch(i, True)

    col = lax.broadcasted_iota(I32, (tm, mrows), 1)
    for t in range(COMBINE_TILES):
        tok = slice(t * tm, (t + 1) * tm)
        rt = rt_ref[tok, :]
        sel = jnp.concatenate(
            [jnp.where(col == rt[:, k:k + 1].astype(I32), 1.0, 0.0).astype(BF16) for k in range(TOP_K)], axis=0)
        picked = _dot(sel, got[slot(i, t)])
        y = sum(rt[:, TOP_K + k:TOP_K + k + 1] * picked[k * tm:(k + 1) * tm] for k in range(TOP_K))
        o_ref[tok, :] = _rms(x_ref[tok, :] + y, gn_ref[...])


def _combine(yb, rt_tok, x2, gn, lay, ne):
    n, dm = x2.shape
    tm = COMBINE_TILES * MOE_TILE
    mrows = _tile_rows(ne)
    tok = lambda w: pl.BlockSpec((tm, w), lambda i, *_: (i, 0))
    return pl.pallas_call(
        functools.partial(_combine_kernel, ne=ne),
        grid_spec=pltpu.PrefetchScalarGridSpec(
            num_scalar_prefetch=3, grid=(n // tm,),
            in_specs=[tok(LANES), tok(dm), pl.BlockSpec((1, dm), lambda i, *_: (0, 0)),
                      pl.BlockSpec(memory_space=pl.ANY)],
            out_specs=tok(dm),
            scratch_shapes=[pltpu.VMEM((2 * COMBINE_TILES, mrows, dm), BF16),
                            pltpu.SemaphoreType.DMA((2 * COMBINE_TILES, ne))]),
        out_shape=jax.ShapeDtypeStruct((n, dm), F32),
        compiler_params=_cparams("arbitrary"),
        name="moe_combine",
    )(lay['seg'], lay['base'], lay['off'], rt_tok, x2, gn.reshape(1, dm), yb)


def _moe_layout(cnt, n):
    nt, ne = cnt.shape
    blk = MOE_BLK
    whole = lambda a: a // BF16_ROWS * BF16_ROWS
    cap = -(-(n + BF16_ROWS + blk) // blk) * blk
    region = jnp.arange(ne, dtype=I32) * cap
    filled = jnp.cumsum(cnt, axis=0) - cnt
    phase = filled - whole(filled)
    span = jnp.where(cnt > 0, whole(phase + cnt + BF16_ROWS - 1), 0)
    base = (region[None, :] + filled - phase).astype(I32)
    off = (jnp.cumsum(span, axis=1) - span).astype(I32)
    rows_e = jnp.sum(cnt, axis=0)
    nblk_e = (rows_e + blk - 1) // blk
    end = (region + whole(rows_e + BF16_ROWS - 1)).astype(I32)
    ends = jnp.cumsum(nblk_e)
    max_blocks = TOP_K * n // blk + ne
    bid = jnp.arange(max_blocks + 1, dtype=I32)
    bexp = jnp.minimum(jnp.sum((bid[:, None] >= ends[None, :]).astype(I32), axis=1), ne - 1)
    brow = bexp * (cap // blk) + bid - (ends - nblk_e)[bexp]
    nblk = ends[-1]
    last = jnp.maximum(nblk - 1, 0)
    used = bid < nblk
    first = bid == (ends - nblk_e)[bexp]
    hand_over = (bid == ends[bexp] - 1) & (bid < last)
    role = jnp.where(used, first.astype(I32) + 2 * hand_over.astype(I32), 4).astype(I32)
    bexp = jnp.where(used, bexp, bexp[last]).astype(I32)
    brow = jnp.where(used, brow, brow[last]).astype(I32)
    flat = lambda a: a.reshape(-1)
    return dict(seg=flat(span.astype(I32)), base=flat(base), off=flat(off), phase=flat(phase.astype(I32)), end=end,
                brow=brow, bexp=bexp, role=role, rows_total=ne * cap, max_blocks=max_blocks)


def _moe(x2, h, logits, gn, w1, w3, w2):
    n, dm = x2.shape
    ne = w1.shape[0]
    rt_lane, rt_tok, cnt = _route(logits, ne)
    lay = _moe_layout(cnt[:, :, 0], n)
    xb = _dispatch(h, rt_lane, lay, ne)
    yb = _expert_ffn(xb, w1, w3, w2, lay['brow'], lay['bexp'], lay['role'], lay['max_blocks'])
    return _combine(yb, rt_tok, x2, gn, lay, ne)


def kernel(x, norm_mix_g, w_in, attn_rel_bias, rg_conv_w, rg_conv_b, rg_wx, rg_bx, rg_wa, rg_ba, rg_lambda, s5_a_re, s5_a_im, s5_log_dt, s5_b_re, s5_b_im, s5_c_re, s5_c_im, s5_d, s5_w_glu, g_group, w_out, norm_ffn_g, ffn_w1, ffn_w3, ffn_w2, moe_router, moe_w1, moe_w3, moe_w2, final_norm_g):
    bsz, l, dm = x.shape
    depth = w_in.shape[0]
    assert depth == 2 and l % ATTN_TQ == 0 and ATTN_TQ == N_PREV * CHUNK, "dense layer, then the MoE layer"
    d_rg = rg_conv_w.shape[2]
    d_s5 = s5_w_glu.shape[1]
    d_attn = (w_in.shape[2] - 2 * d_rg - d_s5) // 3
    x2 = x.reshape(bsz * l, dm)
    attn_bias = jax.vmap(_attn_bias)(attn_rel_bias)
    rg_gates_w = jnp.concatenate([jax.vmap(_block_diag)(rg_wx), jax.vmap(_block_diag)(rg_wa)], axis=2).astype(BF16)
    rg_gates_b = jnp.concatenate([rg_bx, rg_ba], axis=1).astype(F32)[:, None, :]
    rg_decay = jax.nn.softplus(-rg_lambda.astype(F32))[:, None, :]
    s5_prm = jax.vmap(_s5_params)(s5_a_re, s5_a_im, s5_log_dt, s5_b_re, s5_b_im, s5_c_re, s5_c_im, s5_d)
    for layer in range(depth):
        q, k, vt, xr, gate, us = _in_proj(x2, norm_mix_g[layer], w_in, layer, d_attn, d_rg, d_s5)
        y_attn = _attention(q, k, vt, attn_bias, layer, bsz, l)
        y_rg = _rglru(xr, gate, rg_conv_w.astype(F32), rg_conv_b.astype(F32)[:, None, :], rg_gates_w, rg_gates_b,
                      rg_decay, layer, bsz, l)
        y_s5 = _s5(us, s5_prm, layer, bsz, l)
        mixed = (y_attn, y_rg, y_s5, x2, g_group[layer].astype(F32), s5_w_glu, w_out, layer,
                 norm_ffn_g[layer].astype(F32))
        if layer == 0:
            x2 = _out_proj(*mixed, ffn=(ffn_w1[0], ffn_w3[0], ffn_w2[0]))
        else:
            x2, h, logits = _out_proj(*mixed, router=moe_router[0])
            x2 = _moe(x2, h, logits, final_norm_g.astype(F32), moe_w1[0], moe_w3[0], moe_w2[0])
    return x2.reshape(bsz, l, dm)
```

```python
import functools
import math

import jax
import jax.numpy as jnp
import numpy as np
from jax import lax
from jax.experimental import pallas as pl
from jax.experimental.pallas import tpu as pltpu

F32 = jnp.float32
BF16 = jnp.bfloat16
I32 = jnp.int32

EPS = 1e-6
CHUNK = 64
N_PREV = 8
BAND = (N_PREV + 1) * CHUNK
REL_CLIP = 128
HEAD_DIM = 64
RG_C = 8.0
RG_CONV_WIDTH = 4
S5_GROUP_DIM = 16
S5_T = 16
TOP_K = 2
NEG_BIG = -1e30

LANES = 128
SUBLANES = 8
BF16_ROWS = 16
VMEM_LIMIT = 52 * 1024 * 1024
MIX_VMEM_LIMIT = 58 * 1024 * 1024

ROW_TILE = 512
PROJ_TILE = 1024
ATTN_TQ = 512
ATTN_SPAN = 2 * CHUNK
ATTN_WIN = N_PREV * CHUNK + ATTN_SPAN
RG_TL = 2048
MOE_TILE = 256
MOE_BLK = 512
ROUTE_SPAN = 2048
SEG_PIECES = tuple(BF16_ROWS << b for b in range((MOE_TILE // BF16_ROWS).bit_length() - 1, -1, -1))
COMBINE_TILES = 2
SEG_LARGE = MOE_TILE // 2
FFN_CHUNKS = ((0, 512), (512, 512), (1024, 512), (1536, 512), (2048, 512), (2560, 256))


def _cparams(*sem):
    return pltpu.CompilerParams(dimension_semantics=sem, vmem_limit_bytes=VMEM_LIMIT)


def _const_spec(shape):
    nd = len(shape)
    return pl.BlockSpec(shape, lambda *_: (0,) * nd, pipeline_mode=pl.Buffered(1))


def _rms(xf, g):
    var = jnp.mean(xf * xf, axis=-1, keepdims=True)
    return xf * lax.rsqrt(var + EPS) * g


def _sigmoid(x):
    return 1.0 / (1.0 + jnp.exp(-x))


def _gelu(x):
    c = math.sqrt(2.0 / math.pi)
    return 0.5 * x * (1.0 + jnp.tanh(c * (x + 0.044715 * (x * x * x))))


def _dot(a, b):
    return jnp.dot(a, b, preferred_element_type=F32)


def _in_proj_kernel(x_ref, g_ref, w_ref, wvt_ref, q_ref, k_ref, vt_ref, xr_ref, gt_ref, us_ref, *, d_attn, d_rg):
    u = _rms(x_ref[...], g_ref[...]).astype(BF16)
    lo = 0
    for ref, width in ((q_ref, d_attn), (k_ref, d_attn), (None, d_attn),
                       (xr_ref, d_rg), (gt_ref, d_rg), (us_ref, w_ref.shape[1] - 3 * d_attn - 2 * d_rg)):
        if ref is not None:
            ref[...] = _dot(u, w_ref[:, lo:lo + width].astype(BF16)).astype(ref.dtype)
        lo += width
    vt_ref[...] = lax.dot_general(wvt_ref[...].astype(BF16), u, (((1,), (1,)), ((), ())),
                                  preferred_element_type=F32).astype(vt_ref.dtype)


def _layer_spec(stacked, layer):
    rest = stacked.shape[1:]
    return pl.BlockSpec((None,) + rest, lambda *_: (layer,) + (0,) * len(rest), pipeline_mode=pl.Buffered(1))


def _in_proj(x2, g, w_all, layer, d_attn, d_rg, d_s5):
    n, dm = x2.shape
    tm = min(PROJ_TILE, n)
    row = lambda w: pl.BlockSpec((tm, w), lambda i: (i, 0))
    wv_t = w_all[layer, :, 2 * d_attn:3 * d_attn].T
    return pl.pallas_call(
        functools.partial(_in_proj_kernel, d_attn=d_attn, d_rg=d_rg),
        grid=(n // tm,),
        in_specs=[row(dm), _const_spec((1, dm)), _layer_spec(w_all, layer), _const_spec(wv_t.shape)],
        out_specs=[row(d_attn), row(d_attn), pl.BlockSpec((d_attn, tm), lambda i: (0, i)),
                   row(d_rg), row(d_rg), row(d_s5)],
        out_shape=[jax.ShapeDtypeStruct((n, d_attn), BF16)] * 2 + [jax.ShapeDtypeStruct((d_attn, n), BF16)]
        + [jax.ShapeDtypeStruct((n, d_rg), F32)] * 2 + [jax.ShapeDtypeStruct((n, d_s5), F32)],
        compiler_params=_cparams("parallel"),
        name="in_proj",
    )(x2, g.reshape(1, dm), w_all, wv_t)


def _attn_kernel(q_ref, kp_ref, kc_ref, vp_ref, vc_ref, bias_ref, o_ref, kz, sc, *, n_pairs):
    qi = pl.program_id(1)
    tq = q_ref.shape[0]
    kz[0:tq, :] = kp_ref[...]
    kz[tq:2 * tq, :] = kc_ref[...]
    first = lax.broadcasted_iota(I32, (ATTN_SPAN, LANES), 1) < HEAD_DIM
    pairs = [slice(hp * LANES, (hp + 1) * LANES) for hp in range(n_pairs)]

    def score(j, masked):
        r0 = j * ATTN_SPAN
        for hp, cols in enumerate(pairs):
            q2 = q_ref[pl.ds(r0, ATTN_SPAN), cols] * jnp.asarray(HEAD_DIM ** -0.5, BF16)
            zero = jnp.zeros_like(q2)
            qq = jnp.concatenate([jnp.where(first, q2, zero), jnp.where(first, zero, q2)], axis=0)
            s = lax.dot_general(kz[pl.ds(r0, ATTN_WIN), cols], qq, (((1,), (1,)), ((), ())),
                                preferred_element_type=F32) + bias_ref[hp]
            if masked:
                key = lax.broadcasted_iota(I32, (ATTN_WIN, 2 * ATTN_SPAN), 0)
                s = jnp.where(key >= tq - r0, s, NEG_BIG)
            sc[j % 2, hp] = s

    def finish(j):
        r0 = j * ATTN_SPAN
        old = tq - r0
        probs = []
        for hp in range(n_pairs):
            s = sc[j % 2, hp]
            probs.append(jnp.exp((s - jnp.max(s, axis=0, keepdims=True)).astype(BF16)))
        for hp, (cols, p) in enumerate(zip(pairs, probs)):
            ones = lambda w: jnp.ones((BF16_ROWS, w), BF16)
            ov = (_dot(jnp.concatenate([vp_ref[cols, r0:tq], ones(old)], axis=0), p[0:old])
                  + _dot(jnp.concatenate([vc_ref[cols, 0:ATTN_WIN - old], ones(ATTN_WIN - old)], axis=0), p[old:]))
            o2 = (ov[0:LANES] / ov[LANES:LANES + 1]).T
            o_ref[pl.ds(r0, ATTN_SPAN), cols] = jnp.where(first, o2[0:ATTN_SPAN], o2[ATTN_SPAN:]).astype(o_ref.dtype)

    def run(masked):
        groups = tq // ATTN_SPAN
        score(0, masked)
        for j in range(groups):
            if j + 1 < groups:
                score(j + 1, masked)
            finish(j)

    pl.when(qi == 0)(functools.partial(run, True))
    pl.when(qi != 0)(functools.partial(run, False))


def _attention(q, k, vt, bias_t, layer, bsz, l):
    n, da = q.shape
    tq = ATTN_TQ
    nt = l // tq
    n_pairs = da // LANES
    cur = pl.BlockSpec((tq, da), lambda b, i: (b * nt + i, 0))
    prev = pl.BlockSpec((tq, da), lambda b, i: (b * nt + jnp.maximum(i - 1, 0), 0))
    cur_t = pl.BlockSpec((da, tq), lambda b, i: (0, b * nt + i))
    prev_t = pl.BlockSpec((da, tq), lambda b, i: (0, b * nt + jnp.maximum(i - 1, 0)))
    return pl.pallas_call(
        functools.partial(_attn_kernel, n_pairs=n_pairs),
        grid=(bsz, nt),
        in_specs=[cur, prev, cur, prev_t, cur_t, _layer_spec(bias_t, layer)],
        out_specs=cur,
        out_shape=jax.ShapeDtypeStruct((n, da), BF16),
        scratch_shapes=[pltpu.VMEM((2 * tq, da), BF16), pltpu.VMEM((2, n_pairs, ATTN_WIN, 2 * ATTN_SPAN), F32)],
        compiler_params=_cparams("parallel", "parallel"),
        name="chunk_attn",
    )(q, k, k, vt, vt, bias_t)


def _attn_bias(rel_bias):
    h = rel_bias.shape[0]
    tab = rel_bias.astype(F32)
    n_far = N_PREV * CHUNK - REL_CLIP + CHUNK
    lo = 2 * REL_CLIP - (BAND + CHUNK - 1 - n_far)
    ext = jnp.concatenate([jnp.broadcast_to(tab[:, 2 * REL_CLIP:], (h, n_far)), tab[:, lo:2 * REL_CLIP][:, ::-1]], axis=1)
    wide = BAND + CHUNK
    ring = jnp.concatenate([ext[:, CHUNK - 1:], jnp.zeros((h, 1), F32), ext[:, :CHUNK - 1]], axis=1)
    bias = jnp.broadcast_to(ring[:, None, :], (h, CHUNK, wide)).reshape(h, CHUNK * wide)
    bias = bias[:, :CHUNK * (wide - 1)].reshape(h, CHUNK, wide - 1)[..., :BAND]
    band_t = bias.transpose(0, 2, 1)
    groups = ATTN_SPAN // CHUNK
    per_chunk = [jnp.pad(band_t, ((0, 0), (c * CHUNK, (groups - 1 - c) * CHUNK), (0, 0)), constant_values=NEG_BIG)
                 for c in range(groups)]
    win = jnp.stack(per_chunk, axis=2)
    return win.reshape(h // 2, 2, ATTN_WIN, ATTN_SPAN).transpose(0, 2, 1, 3).reshape(h // 2, ATTN_WIN, 2 * ATTN_SPAN)


def _rglru_kernel(x_ref, gt_ref, cw_ref, cb_ref, w_ref, b_ref, sp_ref, o_ref, xpad, a_s, b_s, hcar):
    t = pl.program_id(1)
    tl, c = x_ref.shape
    front = SUBLANES

    @pl.when(t == 0)
    def _():
        xpad[0:front, :] = jnp.zeros((front, c), F32)
        hcar[...] = jnp.zeros_like(hcar)

    xpad[front:front + tl, :] = x_ref[...]
    xc = cb_ref[...] + sum(
        cw_ref[j:j + 1, :] * xpad[front - (RG_CONV_WIDTH - 1) + j:front - (RG_CONV_WIDTH - 1) + j + tl, :]
        for j in range(RG_CONV_WIDTH))
    xpad[0:front, :] = xpad[tl:tl + front, :]
    pre = _dot(xc.astype(BF16), w_ref[...]) + b_ref[...]
    gx = _sigmoid(pre[:, 0:c])
    ga = _sigmoid(pre[:, c:2 * c])
    log_a = -RG_C * ga * sp_ref[...]
    a = jnp.exp(log_a)
    mult = jnp.sqrt(-jnp.tanh(log_a) * (a * a + 1.0))
    b = mult * gx * xc
    nh = c // LANES
    for j in range(nh):
        a_s[j] = a[:, j * LANES:(j + 1) * LANES]
        b_s[j] = b[:, j * LANES:(j + 1) * LANES]

    step_in_vreg = lax.broadcasted_iota(I32, (SUBLANES, LANES), 0)

    def vreg_scan(r, h_in):
        rows = pl.ds(pl.multiple_of(r * SUBLANES, SUBLANES), SUBLANES)
        out = []
        for j in range(nh):
            av, bv = a_s[j, rows, :], b_s[j, rows, :]
            for d in (1, 2, 4):
                seen = step_in_vreg >= d
                a_prev = jnp.where(seen, pltpu.roll(av, d, 0), 1.0)
                b_prev = jnp.where(seen, pltpu.roll(bv, d, 0), 0.0)
                bv = bv + av * b_prev
                av = av * a_prev
            h = bv + av * h_in[j]
            b_s[j, rows, :] = h
            out.append(h[SUBLANES - 1:SUBLANES, :])
        return tuple(out)

    h_last = lax.fori_loop(0, tl // SUBLANES, vreg_scan,
                           tuple(hcar[:, j * LANES:(j + 1) * LANES] for j in range(nh)), unroll=4)
    for j in range(nh):
        cols = slice(j * LANES, (j + 1) * LANES)
        hcar[:, cols] = h_last[j]
        o_ref[:, cols] = (b_s[j] * _gelu(gt_ref[:, cols])).astype(o_ref.dtype)


def _rglru(xr, gate, conv_w, conv_b, w_bd, b_cat, sp, layer, bsz, l):
    n, c = xr.shape
    tl = min(RG_TL, l)
    nt = l // tl
    row = pl.BlockSpec((tl, c), lambda b, t: (b * nt + t, 0))
    return pl.pallas_call(
        _rglru_kernel,
        grid=(bsz, nt),
        in_specs=[row, row] + [_layer_spec(a, layer) for a in (conv_w, conv_b, w_bd, b_cat, sp)],
        out_specs=row,
        out_shape=jax.ShapeDtypeStruct((n, c), BF16),
        scratch_shapes=[pltpu.VMEM((tl + SUBLANES, c), F32), pltpu.VMEM((c // LANES, tl, LANES), F32),
                        pltpu.VMEM((c // LANES, tl, LANES), F32), pltpu.VMEM((1, c), F32)],
        compiler_params=_cparams("parallel", "arbitrary"),
        name="rglru",
    )(xr, gate, conv_w, conv_b, w_bd, b_cat, sp)


def _block_diag(w):
    nb, d, _ = w.shape
    eye = jnp.eye(nb, dtype=w.dtype)
    return (eye[:, None, :, None] * w[:, :, None, :]).reshape(nb * d, nb * d)


def _s5_kernel(u_ref, m_ref, wab_ref, wo_ref, c1_ref, c2a_ref, c2b_ref, d_ref, y_ref, ps, ug, xa_s, xb_s):
    t, gd = S5_T, S5_GROUP_DIM
    ng, nk, _ = ug.shape
    nh = ps.shape[0]
    per_half = LANES // gd
    rows8 = SUBLANES

    for h in range(nh):
        ps[h] = u_ref[:, h * LANES:(h + 1) * LANES]

    lane_block = lax.broadcasted_iota(I32, (rows8, LANES), 1) // gd

    def block_transpose(vs):
        d = per_half // 2
        while d:
            keep = (lane_block & d) == 0
            out = list(vs)
            for i in range(per_half):
                if i & d == 0:
                    out[i] = jnp.where(keep, vs[i], pltpu.roll(vs[i + d], d * gd, 1))
                    out[i + d] = jnp.where(keep, pltpu.roll(vs[i], LANES - d * gd, 1), vs[i + d])
            vs, d = out, d // 2
        return vs

    def to_groups(r, carry):
        base = pl.multiple_of(r * rows8 * t, rows8 * t)
        rows = pl.ds(pl.multiple_of(r * rows8, rows8), rows8)
        for h in range(nh):
            for j in range(t // per_half):
                steps = [ps[h, pl.ds(base + j * per_half + i, rows8, stride=t), :] for i in range(per_half)]
                for k, v in enumerate(block_transpose(steps)):
                    ug[h * per_half + k, rows, j * LANES:(j + 1) * LANES] = v
        return carry

    lax.fori_loop(0, nk // rows8, to_groups, 0, unroll=2)

    def project(g, carry):
        u = ug[g]
        ub = u.astype(BF16)
        rows = pl.ds(pl.multiple_of(g * nk, nk), nk)
        xab = _dot(ub, wab_ref[g])
        xa_s[rows, :] = xab[:, 0:LANES]
        xb_s[rows, :] = xab[:, LANES:2 * LANES]
        ug[g] = _dot(ub, m_ref[g]) + d_ref[g] * u
        return carry

    lax.fori_loop(0, ng, project, 0, unroll=8)

    c1, c2a, c2b = c1_ref[...], c2a_ref[...], c2b_ref[...]

    def step(k, carry):
        xa, xb = carry
        rows = pl.ds(k, ng, stride=nk)
        ia = xa_s[rows, :]
        ib = xb_s[rows, :]
        xa_s[rows, :] = xa
        return c1 * xa + c2a * xb + ia, c1 * xb + c2b * xa + ib

    z = jnp.zeros((ng, xa_s.shape[1]), F32)
    lax.fori_loop(0, nk, step, (z, z), unroll=8)

    def respond(g, carry):
        rows = pl.ds(pl.multiple_of(g * nk, nk), nk)
        ug[g] = _gelu(ug[g] + _dot(xa_s[rows, :].astype(BF16), wo_ref[g]))
        return carry

    lax.fori_loop(0, ng, respond, 0, unroll=8)

    def from_groups(r, carry):
        base = pl.multiple_of(r * rows8 * t, rows8 * t)
        rows = pl.ds(pl.multiple_of(r * rows8, rows8), rows8)
        for h in range(nh):
            for j in range(t // per_half):
                groups = [ug[h * per_half + k, rows, j * LANES:(j + 1) * LANES] for k in range(per_half)]
                for i, v in enumerate(block_transpose(groups)):
                    ps[h, pl.ds(base + j * per_half + i, rows8, stride=t), :] = v
        return carry

    lax.fori_loop(0, nk // rows8, from_groups, 0, unroll=2)
    for h in range(nh):
        y_ref[:, h * LANES:(h + 1) * LANES] = ps[h].astype(y_ref.dtype)


def _s5(us, prm, layer, bsz, l):
    n, w = us.shape
    ng = w // S5_GROUP_DIM
    nk = l // S5_T
    p2 = prm[2].shape[-2]
    row = pl.BlockSpec((l, w), lambda b: (b, 0))
    return pl.pallas_call(
        _s5_kernel,
        grid=(bsz,),
        in_specs=[row] + [_layer_spec(a, layer) for a in prm],
        out_specs=row,
        out_shape=jax.ShapeDtypeStruct((n, w), BF16),
        scratch_shapes=[pltpu.VMEM((w // LANES, l, LANES), F32), pltpu.VMEM((ng, nk, S5_T * S5_GROUP_DIM), F32),
                        pltpu.VMEM((ng * nk, p2), F32), pltpu.VMEM((ng * nk, p2), F32)],
        compiler_params=_cparams("parallel"),
        name="s5",
    )(us, *prm)


def _s5_params(a_re, a_im, log_dt, b_re, b_im, c_re, c_im, d):
    t = S5_T
    g, p = a_re.shape
    gd = S5_GROUP_DIM
    hi = lax.Precision.HIGHEST
    a = lax.complex(a_re.astype(F32), a_im.astype(F32))
    dt = jnp.exp(log_dt.astype(F32))[:, None]
    a_bar = jnp.exp(a * dt)
    bm = lax.complex(b_re.astype(F32), b_im.astype(F32))
    cm = lax.complex(c_re.astype(F32), c_im.astype(F32))
    b_bar = ((a_bar - 1.0) / a)[..., None] * bm
    steps = jnp.arange(t + 1, dtype=F32)
    pw = jnp.exp((a * dt)[:, None, :] * steps[None, :, None])

    cp = (cm[:, None, :, :] * pw[:, :, None, :]).transpose(0, 3, 1, 2)
    w = t * gd
    lag = cp[:, :, :t].reshape(g, p, w)
    k2 = jnp.einsum('gpc,gpx->gcx', jnp.concatenate([b_bar.real, -b_bar.imag], axis=1),
                    jnp.concatenate([lag.real, lag.imag], axis=1), precision=hi)
    shift = np.zeros((t, w, w), np.float32)
    for s in range(t):
        shift[s, np.arange(w - s * gd), np.arange(w - s * gd) + s * gd] = 1.0
    m = jnp.einsum('gcx,sxl->gscl', k2, shift, precision=hi).reshape(g, w, w)
    win = pw[:, :t][:, ::-1][:, :, :, None] * b_bar[:, None]
    win = win.transpose(0, 1, 3, 2).reshape(g, w, p)
    wab = jnp.concatenate([win.real, win.imag, win.imag, win.real], axis=-1)
    co = cp[:, :, 1:].reshape(g, p, w)
    wo = jnp.concatenate([co.real, -co.imag], axis=1)
    lt = pw[:, t]
    c1 = jnp.concatenate([lt.real, lt.real], axis=-1)
    c2a = jnp.concatenate([-lt.imag, lt.imag], axis=-1)
    c2b = jnp.concatenate([lt.imag, -lt.imag], axis=-1)
    dtile = jnp.tile(d.astype(F32).reshape(g, 1, gd), (1, t, 1)).reshape(g, 1, t * gd)
    return (m.astype(BF16), wab.astype(BF16), wo.astype(BF16), c1, c2a, c2b, dtile)


def _out_proj_kernel(ya_ref, yr_ref, ys_ref, x_ref, gg_ref, wglu_ref, wo_ref, gf_ref, *rest, dense, chunks):
    ys = ys_ref[...]
    ys = ys.astype(F32) * _sigmoid(_dot(ys.astype(BF16), wglu_ref[...].astype(BF16)))
    acc = x_ref[...]
    lo = 0
    for y in (ya_ref[...].astype(F32), yr_ref[...].astype(F32), ys):
        w = y.shape[1]
        acc = acc + _dot(_rms(y, gg_ref[:, lo:lo + w]).astype(BF16), wo_ref[lo:lo + w, :].astype(BF16))
        lo += w
    h = _rms(acc, gf_ref[...])
    hb = h.astype(BF16)
    if dense:
        w1_ref, w3_ref, w2_ref, xo_ref = rest
        xo_ref[...] = acc + _swiglu_chunks(hb, w1_ref, w3_ref, w2_ref, chunks)
        return
    rt_ref, xo_ref, h_ref, lg_ref = rest
    xo_ref[...] = acc
    h_ref[...] = hb
    both = _dot(hb, rt_ref[...])
    h_lo = (h - hb.astype(F32)).astype(BF16)
    lg_ref[...] = both[:, 0:LANES] + both[:, LANES:2 * LANES] + _dot(h_lo, rt_ref[:, 0:LANES])


def _out_proj(ya, yr, ys, x2, gg, wglu_all, wo_all, layer, gf, *, ffn=None, router=None):
    n, dm = x2.shape
    tm = min(ROW_TILE, n)
    row = lambda w: pl.BlockSpec((tm, w), lambda i: (i, 0))
    ins = [ya, yr, ys, x2, gg.reshape(1, -1), wglu_all, wo_all, gf.reshape(1, dm)]
    in_specs = [row(ya.shape[1]), row(yr.shape[1]), row(ys.shape[1]), row(dm), _const_spec((1, gg.shape[0])),
                _layer_spec(wglu_all, layer), _layer_spec(wo_all, layer), _const_spec((1, dm))]
    if ffn is not None:
        ins += list(ffn)
        in_specs += [_const_spec(w.shape) for w in ffn]
        out_specs = row(dm)
        out_shape = jax.ShapeDtypeStruct((n, dm), F32)
        chunks = _ffn_chunks(ffn[0].shape[1])
    else:
        ne = router.shape[1]
        r_hi = router.astype(BF16)
        r_lo = (router.astype(F32) - r_hi.astype(F32)).astype(BF16)
        pad = lambda a: jnp.pad(a, ((0, 0), (0, LANES - ne)))
        ins.append(jnp.concatenate([pad(r_hi), pad(r_lo)], axis=1))
        in_specs.append(_const_spec((dm, 2 * LANES)))
        out_specs = [row(dm), row(dm), row(LANES)]
        out_shape = [jax.ShapeDtypeStruct((n, dm), F32), jax.ShapeDtypeStruct((n, dm), BF16),
                     jax.ShapeDtypeStruct((n, LANES), F32)]
        chunks = None
    return pl.pallas_call(
        functools.partial(_out_proj_kernel, dense=ffn is not None, chunks=chunks),
        grid=(n // tm,),
        in_specs=in_specs, out_specs=out_specs, out_shape=out_shape,
        compiler_params=pltpu.CompilerParams(dimension_semantics=("parallel",), vmem_limit_bytes=MIX_VMEM_LIMIT),
        name="out_proj",
    )(*ins)


def _swiglu_chunks(h, w1, w3, w2, chunks, before_chunk=None, after_chunk=None):
    acc = None
    for j, (lo, width) in enumerate(chunks):
        if before_chunk is not None:
            before_chunk(j)
        a = _dot(h, w1[:, lo:lo + width].astype(BF16))
        b = _dot(h, w3[:, lo:lo + width].astype(BF16))
        t = (a * _sigmoid(a) * b).astype(BF16)
        y = _dot(t, w2[lo:lo + width, :].astype(BF16))
        acc = y if acc is None else acc + y
        if after_chunk is not None:
            after_chunk(j)
    return acc


def _ffn_chunks(f):
    if f == sum(w for _, w in FFN_CHUNKS):
        return FFN_CHUNKS
    return ((0, f),)


def _expert_weight_copies(hbm, vmem, sems, chunks, expert, j):
    h1, h3, h2 = hbm
    v1, v3, v2 = vmem
    lo, w = chunks[j]
    return [pltpu.make_async_copy(h1.at[expert, :, lo:lo + w], v1.at[:, lo:lo + w], sems.at[j, 0]),
            pltpu.make_async_copy(h3.at[expert, :, lo:lo + w], v3.at[:, lo:lo + w], sems.at[j, 1]),
            pltpu.make_async_copy(h2.at[expert, lo:lo + w, :], v2.at[lo:lo + w, :], sems.at[j, 2])]


def _expert_ffn_kernel(brow_ref, bexp_ref, role_ref, x_ref, w1_hbm, w3_hbm, w2_hbm, o_ref, w1_v, w3_v, w2_v, sems,
                       *, chunks):
    i = pl.program_id(0)
    copies = functools.partial(_expert_weight_copies, (w1_hbm, w3_hbm, w2_hbm), (w1_v, w3_v, w2_v), sems, chunks)

    @pl.when(i == 0)
    def _():
        for j in range(len(chunks)):
            for cp in copies(bexp_ref[0], j):
                cp.start()

    def block(first, last):
        def before(j):
            for cp in copies(bexp_ref[i], j):
                cp.wait()

        def after(j):
            for cp in copies(bexp_ref[i + 1], j):
                cp.start()

        o_ref[...] = _swiglu_chunks(x_ref[...], w1_v, w3_v, w2_v, chunks, before if first else None,
                                    after if last else None).astype(o_ref.dtype)

    for role in range(4):
        pl.when(role_ref[i] == role)(functools.partial(block, bool(role & 1), bool(role & 2)))


def _expert_ffn(xb, w1, w3, w2, blk_row, blk_exp, role, max_blocks):
    rows, dm = xb.shape
    f = w1.shape[2]
    chunks = _ffn_chunks(f)
    row = pl.BlockSpec((MOE_BLK, dm), lambda i, br, be, ro: (br[i], 0))
    hbm = pl.BlockSpec(memory_space=pl.ANY)
    return pl.pallas_call(
        functools.partial(_expert_ffn_kernel, chunks=chunks),
        grid_spec=pltpu.PrefetchScalarGridSpec(
            num_scalar_prefetch=3, grid=(max_blocks,),
            in_specs=[row, hbm, hbm, hbm], out_specs=row,
            scratch_shapes=[pltpu.VMEM(w1.shape[1:], w1.dtype), pltpu.VMEM(w3.shape[1:], w3.dtype),
                            pltpu.VMEM(w2.shape[1:], w2.dtype), pltpu.SemaphoreType.DMA((len(chunks), 3))]),
        out_shape=jax.ShapeDtypeStruct((rows, dm), BF16),
        compiler_params=_cparams("arbitrary"),
        name="expert_ffn",
    )(blk_row, blk_exp, role, xb, w1, w3, w2)


def _route_tile(logits_tok, filled, ne):
    tm = logits_tok.shape[0]
    logits = logits_tok.T[0:ne, :]
    eidx = lax.broadcasted_iota(I32, (ne, tm), 0)
    m1 = jnp.max(logits, axis=0, keepdims=True)
    i1 = jnp.min(jnp.where(logits == m1, eidx, ne), axis=0, keepdims=True)
    sel1 = eidx == i1
    rest = jnp.where(sel1, -jnp.inf, logits)
    m2 = jnp.max(rest, axis=0, keepdims=True)
    i2 = jnp.min(jnp.where(rest == m2, eidx, ne), axis=0, keepdims=True)
    sel2 = eidx == i2
    e2 = jnp.exp(m2 - m1)
    den = 1.0 + e2
    rf = jnp.where(sel1 | sel2, 1.0, 0.0)
    cnt = jnp.sum(rf, axis=1, keepdims=True)
    before = (lax.broadcasted_iota(I32, (tm, tm), 0) < lax.broadcasted_iota(I32, (tm, tm), 1))
    rank = _dot(rf.astype(BF16), jnp.where(before, 1.0, 0.0).astype(BF16))
    whole = lambda a: jnp.floor(a * (1.0 / BF16_ROWS)) * BF16_ROWS
    phase = filled - whole(filled)
    span = jnp.where(cnt > 0, whole(phase + cnt + (BF16_ROWS - 1)), 0.0)
    ecol = lax.broadcasted_iota(I32, (ne, 1), 0)
    off = jnp.zeros((ne, 1), F32)
    for j in range(ne - 1):
        off = off + jnp.where(ecol > j, span[j:j + 1, :], 0.0)
    place = off + phase + rank
    pos1 = jnp.sum(jnp.where(sel1, place, 0.0), axis=0, keepdims=True)
    pos2 = jnp.sum(jnp.where(sel2, place, 0.0), axis=0, keepdims=True)
    record = [pos1, pos2, 1.0 / den, e2 / den]
    lane_major = jnp.concatenate(record + [jnp.zeros((SUBLANES - len(record), tm), F32)], axis=0)
    tok_major = jnp.concatenate([lane_major, jnp.zeros((LANES - SUBLANES, tm), F32)], axis=0).T
    return lane_major, tok_major, cnt


def _route_kernel(lg_ref, lane_ref, tok_ref, cnt_ref, filled, *, ne):
    tm = MOE_TILE

    @pl.when(pl.program_id(0) == 0)
    def _():
        filled[...] = jnp.zeros_like(filled)

    for t in range(lg_ref.shape[0] // tm):
        lane_major, tok_major, cnt = _route_tile(lg_ref[t * tm:(t + 1) * tm, :], filled[...], ne)
        lane_ref[:, t * tm:(t + 1) * tm] = lane_major
        tok_ref[t * tm:(t + 1) * tm, :] = tok_major
        cnt_ref[t] = jnp.broadcast_to(cnt, (ne, LANES)).astype(I32)
        filled[...] = filled[...] + cnt


def _route(logits, ne):
    n = logits.shape[0]
    tm = MOE_TILE
    span = min(ROUTE_SPAN, n)
    return pl.pallas_call(
        functools.partial(_route_kernel, ne=ne),
        grid=(n // span,),
        in_specs=[pl.BlockSpec((span, LANES), lambda i: (i, 0))],
        out_specs=[pl.BlockSpec((SUBLANES, span), lambda i: (0, i)), pl.BlockSpec((span, LANES), lambda i: (i, 0)),
                   pl.BlockSpec((span // tm, ne, LANES), lambda i: (i, 0, 0))],
        out_shape=[jax.ShapeDtypeStruct((SUBLANES, n), F32), jax.ShapeDtypeStruct((n, LANES), F32),
                   jax.ShapeDtypeStruct((n // tm, ne, LANES), I32)],
        scratch_shapes=[pltpu.VMEM((ne, 1), F32)],
        compiler_params=_cparams("arbitrary"),
        name="moe_route",
    )(logits)


def _segment_copies(seg_ref, grouped_ref, base_ref, tile_buf, off_ref, sems, tile, slot, ne, *, to_grouped, wait):
    large = [p for p in SEG_PIECES if p >= SEG_LARGE]
    small = [p for p in SEG_PIECES if p < SEG_LARGE]
    for e in range(ne):
        seg = seg_ref[tile * ne + e]

        def pieces(sizes, done, seg=seg, e=e):
            far = base_ref[tile * ne + e] + done
            near = off_ref[tile * ne + e] + done
            for piece in sizes:
                g = grouped_ref.at[pl.ds(pl.multiple_of(far, BF16_ROWS), piece)]
                t = tile_buf.at[slot, pl.ds(pl.multiple_of(near, BF16_ROWS), piece)]
                cp = pltpu.make_async_copy(t, g, sems.at[slot, e]) if to_grouped else \
                    pltpu.make_async_copy(g, t, sems.at[slot, e])
                has = (seg & piece) != 0

                @pl.when(has)
                def _():
                    cp.wait() if wait else cp.start()

                step = jnp.where(has, piece, 0)
                far = far + step
                near = near + step

        pl.when(seg >= SEG_LARGE)(functools.partial(pieces, large, 0))
        pieces(small, seg & -SEG_LARGE)


def _tile_rows(ne):
    return TOP_K * MOE_TILE + ne * 2 * BF16_ROWS


def _tail_copies(end_ref, zeros, xb_ref, sems, ne):
    return [pltpu.make_async_copy(zeros, xb_ref.at[pl.ds(pl.multiple_of(end_ref[e], BF16_ROWS), MOE_BLK)], sems.at[e])
            for e in range(ne)]


def _dispatch_kernel(seg_ref, base_ref, off_ref, phase_ref, end_ref, rt_ref, h_ref, xb_ref, res, zeros, partial,
                     sems, tail_sems):
    i = pl.program_id(0)
    nt = pl.num_programs(0)
    ne = tail_sems.shape[0]
    tm = h_ref.shape[0]
    mrows = res.shape[1]
    slot = i % 2
    copies = functools.partial(_segment_copies, seg_ref, xb_ref, base_ref, res, off_ref, sems, ne=ne, to_grouped=True)

    @pl.when(i == 0)
    def _():
        zeros[...] = jnp.zeros_like(zeros)
        partial[...] = jnp.zeros_like(partial)
        for cp in _tail_copies(end_ref, zeros, xb_ref, tail_sems, ne):
            cp.start()

    row = lax.broadcasted_iota(I32, (mrows, tm), 0)
    pos = rt_ref[0:2, :].astype(I32)
    perm = jnp.where((row == pos[0:1, :]) | (row == pos[1:2, :]), 1.0, 0.0).astype(BF16)
    res[slot] = _dot(perm, h_ref[...]).astype(BF16)

    tile_row = lax.broadcasted_iota(I32, (BF16_ROWS, res.shape[2]), 0)
    for e in range(ne):
        @pl.when(seg_ref[i * ne + e] > 0)
        def _():
            first = pl.ds(pl.multiple_of(off_ref[i * ne + e], BF16_ROWS), BF16_ROWS)
            merged = jnp.where(tile_row < phase_ref[i * ne + e], partial[e].astype(F32), res[slot, first, :].astype(F32))
            res[slot, first, :] = merged.astype(BF16)
            last = pl.ds(pl.multiple_of(off_ref[i * ne + e] + seg_ref[i * ne + e] - BF16_ROWS, BF16_ROWS), BF16_ROWS)
            partial[e] = res[slot, last, :]

    @pl.when(i > 0)
    def _():
        copies(tile=i - 1, slot=1 - slot, wait=True)

    copies(tile=i, slot=slot, wait=False)

    @pl.when(i == nt - 1)
    def _():
        copies(tile=i, slot=slot, wait=True)
        for cp in _tail_copies(end_ref, zeros, xb_ref, tail_sems, ne):
            cp.wait()


def _dispatch(h, rt_lane, lay, ne):
    n, dm = h.shape
    tm = MOE_TILE
    return pl.pallas_call(
        _dispatch_kernel,
        grid_spec=pltpu.PrefetchScalarGridSpec(
            num_scalar_prefetch=5, grid=(n // tm,),
            in_specs=[pl.BlockSpec((SUBLANES, tm), lambda i, *_: (0, i)), pl.BlockSpec((tm, dm), lambda i, *_: (i, 0))],
            out_specs=pl.BlockSpec(memory_space=pl.ANY),
            scratch_shapes=[pltpu.VMEM((2, _tile_rows(ne), dm), BF16), pltpu.VMEM((MOE_BLK, dm), BF16),
                            pltpu.VMEM((ne, BF16_ROWS, dm), BF16),
                            pltpu.SemaphoreType.DMA((2, ne)), pltpu.SemaphoreType.DMA((ne,))]),
        out_shape=jax.ShapeDtypeStruct((lay['rows_total'], dm), BF16),
        compiler_params=_cparams("arbitrary"),
        name="moe_dispatch",
    )(lay['seg'], lay['base'], lay['off'], lay['phase'], lay['end'], rt_lane, h)


def _combine_kernel(seg_ref, base_ref, off_ref, rt_ref, x_ref, gn_ref, yb_ref, o_ref, got, sems, *, ne):
    i = pl.program_id(0)
    steps = pl.num_programs(0)
    tm = MOE_TILE
    mrows = got.shape[1]
    copies = functools.partial(_segment_copies, seg_ref, yb_ref, base_ref, got, off_ref, sems, ne=ne, to_grouped=False)
    slot = lambda step, t: (step % 2) * COMBINE_TILES + t

    def fetch(step, wait):
        for t in range(COMBINE_TILES):
            copies(tile=step * COMBINE_TILES + t, slot=slot(step, t), wait=wait)

    @pl.when(i == 0)
    def _():
        got[...] = jnp.zeros_like(got)
        fetch(i, False)

    pl.when(i + 1 < steps)(lambda: fetch(i + 1, False))
    fetch(i, True)

    col = lax.broadcasted_iota(I32, (tm, mrows), 1)
    for t in range(COMBINE_TILES):
        tok = slice(t * tm, (t + 1) * tm)
        rt = rt_ref[tok, :]
        sel = jnp.concatenate(
            [jnp.where(col == rt[:, k:k + 1].astype(I32), 1.0, 0.0).astype(BF16) for k in range(TOP_K)], axis=0)
        picked = _dot(sel, got[slot(i, t)])
        y = sum(rt[:, TOP_K + k:TOP_K + k + 1] * picked[k * tm:(k + 1) * tm] for k in range(TOP_K))
        o_ref[tok, :] = _rms(x_ref[tok, :] + y, gn_ref[...])


def _combine(yb, rt_tok, x2, gn, lay, ne):
    n, dm = x2.shape
    tm = COMBINE_TILES * MOE_TILE
    mrows = _tile_rows(ne)
    tok = lambda w: pl.BlockSpec((tm, w), lambda i, *_: (i, 0))
    return pl.pallas_call(
        functools.partial(_combine_kernel, ne=ne),
        grid_spec=pltpu.PrefetchScalarGridSpec(
            num_scalar_prefetch=3, grid=(n // tm,),
            in_specs=[tok(LANES), tok(dm), pl.BlockSpec((1, dm), lambda i, *_: (0, 0)),
                      pl.BlockSpec(memory_space=pl.ANY)],
            out_specs=tok(dm),
            scratch_shapes=[pltpu.VMEM((2 * COMBINE_TILES, mrows, dm), BF16),
                            pltpu.SemaphoreType.DMA((2 * COMBINE_TILES, ne))]),
        out_shape=jax.ShapeDtypeStruct((n, dm), F32),
        compiler_params=_cparams("arbitrary"),
        name="moe_combine",
    )(lay['seg'], lay['base'], lay['off'], rt_tok, x2, gn.reshape(1, dm), yb)


def _moe_layout(cnt, n):
    nt, ne = cnt.shape
    blk = MOE_BLK
    whole = lambda a: a // BF16_ROWS * BF16_ROWS
    cap = -(-(n + BF16_ROWS + blk) // blk) * blk
    region = jnp.arange(ne, dtype=I32) * cap
    filled = jnp.cumsum(cnt, axis=0) - cnt
    phase = filled - whole(filled)
    span = jnp.where(cnt > 0, whole(phase + cnt + BF16_ROWS - 1), 0)
    base = (region[None, :] + filled - phase).astype(I32)
    off = (jnp.cumsum(span, axis=1) - span).astype(I32)
    rows_e = jnp.sum(cnt, axis=0)
    nblk_e = (rows_e + blk - 1) // blk
    end = (region + whole(rows_e + BF16_ROWS - 1)).astype(I32)
    ends = jnp.cumsum(nblk_e)
    max_blocks = TOP_K * n // blk + ne
    bid = jnp.arange(max_blocks + 1, dtype=I32)
    bexp = jnp.minimum(jnp.sum((bid[:, None] >= ends[None, :]).astype(I32), axis=1), ne - 1)
    brow = bexp * (cap // blk) + bid - (ends - nblk_e)[bexp]
    nblk = ends[-1]
    last = jnp.maximum(nblk - 1, 0)
    used = bid < nblk
    first = bid == (ends - nblk_e)[bexp]
    hand_over = (bid == ends[bexp] - 1) & (bid < last)
    role = jnp.where(used, first.astype(I32) + 2 * hand_over.astype(I32), 4).astype(I32)
    bexp = jnp.where(used, bexp, bexp[last]).astype(I32)
    brow = jnp.where(used, brow, brow[last]).astype(I32)
    flat = lambda a: a.reshape(-1)
    return dict(seg=flat(span.astype(I32)), base=flat(base), off=flat(off), phase=flat(phase.astype(I32)), end=end,
                brow=brow, bexp=bexp, role=role, rows_total=ne * cap, max_blocks=max_blocks)


def _moe(x2, h, logits, gn, w1, w3, w2):
    n, dm = x2.shape
    ne = w1.shape[0]
    rt_lane, rt_tok, cnt = _route(logits, ne)
    lay = _moe_layout(cnt[:, :, 0], n)
    xb = _dispatch(h, rt_lane, lay, ne)
    yb = _expert_ffn(xb, w1, w3, w2, lay['brow'], lay['bexp'], lay['role'], lay['max_blocks'])
    return _combine(yb, rt_tok, x2, gn, lay, ne)


def kernel(x, norm_mix_g, w_in, attn_rel_bias, rg_conv_w, rg_conv_b, rg_wx, rg_bx, rg_wa, rg_ba, rg_lambda, s5_a_re, s5_a_im, s5_log_dt, s5_b_re, s5_b_im, s5_c_re, s5_c_im, s5_d, s5_w_glu, g_group, w_out, norm_ffn_g, ffn_w1, ffn_w3, ffn_w2, moe_router, moe_w1, moe_w3, moe_w2, final_norm_g):
    bsz, l, dm = x.shape
    depth = w_in.shape[0]
    assert depth == 2 and l % ATTN_TQ == 0 and ATTN_TQ == N_PREV * CHUNK, "dense layer, then the MoE layer"
    d_rg = rg_conv_w.shape[2]
    d_s5 = s5_w_glu.shape[1]
    d_attn = (w_in.shape[2] - 2 * d_rg - d_s5) // 3
    x2 = x.reshape(bsz * l, dm)
    attn_bias = jax.vmap(_attn_bias)(attn_rel_bias)
    rg_gates_w = jnp.concatenate([jax.vmap(_block_diag)(rg_wx), jax.vmap(_block_diag)(rg_wa)], axis=2).astype(BF16)
    rg_gates_b = jnp.concatenate([rg_bx, rg_ba], axis=1).astype(F32)[:, None, :]
    rg_decay = jax.nn.softplus(-rg_lambda.astype(F32))[:, None, :]
    s5_prm = jax.vmap(_s5_params)(s5_a_re, s5_a_im, s5_log_dt, s5_b_re, s5_b_im, s5_c_re, s5_c_im, s5_d)
    for layer in range(depth):
        q, k, vt, xr, gate, us = _in_proj(x2, norm_mix_g[layer], w_in, layer, d_attn, d_rg, d_s5)
        y_attn = _attention(q, k, vt, attn_bias, layer, bsz, l)
        y_rg = _rglru(xr, gate, rg_conv_w.astype(F32), rg_conv_b.astype(F32)[:, None, :], rg_gates_w, rg_gates_b,
                      rg_decay, layer, bsz, l)
        y_s5 = _s5(us, s5_prm, layer, bsz, l)
        mixed = (y_attn, y_rg, y_s5, x2, g_group[layer].astype(F32), s5_w_glu, w_out, layer,
                 norm_ffn_g[layer].astype(F32))
        if layer == 0:
            x2 = _out_proj(*mixed, ffn=(ffn_w1[0], ffn_w3[0], ffn_w2[0]))
        else:
            x2, h, logits = _out_proj(*mixed, router=moe_router[0])
            x2 = _moe(x2, h, logits, final_norm_g.astype(F32), moe_w1[0], moe_w3[0], moe_w2[0])
    return x2.reshape(bsz, l, dm)
```
